```python
import jax, jax.numpy as jnp
from jax import lax
import numpy as np

D_MODEL = 1024
BATCH = 8
SEQ = 8192
DEPTH = 2

N_MIXERS = 2
HEAD_DIM = 64
N_SLOTS = 8
DILATED_GROUPS = ((128, 1), (512, 4), (2048, 16))
N_GROUPS = 3
GROUP_WIDTH = N_SLOTS * HEAD_DIM
ROT_DIM = HEAD_DIM // 4
ROPE_THETA = 500000.0
CONV_KERNEL = 31
CONV_INNER = D_MODEL
FFN_DIM = 2816
FFN_CONV = 3
EPS = 1e-6

kernel_name = "hybrid_dilated_attn_conformer_convffn"


def _rmsnorm(x, g):
    xf = x.astype(jnp.float32)
    y = xf * lax.rsqrt(jnp.mean(xf * xf, axis=-1, keepdims=True) + EPS)
    return (y * g.astype(jnp.float32)).astype(x.dtype)


def _layernorm(x, g, b):
    xf = x.astype(jnp.float32)
    mu = jnp.mean(xf, axis=-1, keepdims=True)
    var = jnp.mean(jnp.square(xf - mu), axis=-1, keepdims=True)
    y = (xf - mu) * lax.rsqrt(var + EPS)
    return (y * g.astype(jnp.float32) + b.astype(jnp.float32)).astype(x.dtype)


def _causal_depthwise_conv(x, w):
    k, c = w.shape
    return lax.conv_general_dilated(
        x, w[:, None, :].astype(x.dtype), window_strides=(1,),
        padding=[(k - 1, 0)], dimension_numbers=("NWC", "WIO", "NWC"),
        feature_group_count=c)


def _partial_rope(t, positions):
    half = ROT_DIM // 2
    inv_freq = ROPE_THETA ** (-jnp.arange(half, dtype=jnp.float32) / half)
    ang = positions.astype(jnp.float32)[:, :, None] * inv_freq
    cos = jnp.cos(ang)[:, :, None, None, :]
    sin = jnp.sin(ang)[:, :, None, None, :]
    tr = t[..., :ROT_DIM].astype(jnp.float32)
    t1, t2 = tr[..., :half], tr[..., half:]
    rot = jnp.concatenate([t1 * cos - t2 * sin, t2 * cos + t1 * sin], axis=-1)
    return jnp.concatenate([rot.astype(t.dtype), t[..., ROT_DIM:]], axis=-1)


def _banded_causal_attention(q, k, v, span):
    n, h, l, dh = q.shape
    nb = l // span
    qb = q.reshape(n, h, nb, span, dh)
    kb = k.reshape(n, h, nb, span, dh)
    vb = v.reshape(n, h, nb, span, dh)

    def with_prev(t):
        prev = jnp.pad(t, ((0, 0), (0, 0), (1, 0), (0, 0), (0, 0)))[:, :, :-1]
        return jnp.concatenate([prev, t], axis=3)

    kc, vc = with_prev(kb), with_prev(vb)
    s = jnp.einsum("nhbqd,nhbkd->nhbqk", qb, kc,
                   preferred_element_type=jnp.float32) * (HEAD_DIM ** -0.5)
    qi = jnp.arange(span)[:, None] + span
    ki = jnp.arange(2 * span)[None, :]
    dist = qi - ki
    band = (dist >= 0) & (dist <= span)
    has_prev = (jnp.arange(nb)[:, None, None] > 0) | (ki[None] >= span)
    mask = band[None] & has_prev
    s = jnp.where(mask, s, -jnp.inf)
    m = jnp.max(s, axis=-1, keepdims=True)
    p = jnp.exp(s - m)
    denom = jnp.sum(p, axis=-1)
    o = jnp.einsum("nhbqk,nhbkd->nhbqd", p, vc.astype(jnp.float32)) / denom[..., None]
    lse = m[..., 0] + jnp.log(denom)
    return o.reshape(n, h, l, dh), lse.reshape(n, h, l)


def _dilated_group_attention(q, k, v, window, dilation):
    b, s, h, dh = q.shape
    span = window // dilation
    l = s // dilation
    lp = -(-l // span) * span

    def to_phases(t):
        t = t.reshape(b, l, dilation, h, dh).transpose(0, 2, 3, 1, 4)
        t = t.reshape(b * dilation, h, l, dh)
        return jnp.pad(t, ((0, 0), (0, 0), (0, lp - l), (0, 0)))

    o, lse = _banded_causal_attention(to_phases(q), to_phases(k), to_phases(v), span)
    o = o[:, :, :l].reshape(b, dilation, h, l, dh).transpose(0, 3, 1, 2, 4)
    lse = lse[:, :, :l].reshape(b, dilation, h, l).transpose(0, 3, 1, 2)
    return o.reshape(b, s, h, dh), lse.reshape(b, s, h)


def _dilated_attention(h, positions, w_qkv, w_o):
    b, s, _ = h.shape
    qkv = (h @ w_qkv).reshape(b, s, 3, N_GROUPS, N_SLOTS, HEAD_DIM)
    q = _partial_rope(qkv[:, :, 0], positions)
    k = _partial_rope(qkv[:, :, 1], positions)
    v = qkv[:, :, 2]
    outs, lses = [], []
    for g, (window, dilation) in enumerate(DILATED_GROUPS):
        o_g, lse_g = _dilated_group_attention(q[:, :, g], k[:, :, g], v[:, :, g],
                                              window, dilation)
        outs.append(o_g)
        lses.append(lse_g)
    o = jnp.stack(outs, axis=0)
    wgt = jax.nn.softmax(jnp.stack(lses, axis=0), axis=0)
    mixed = jnp.sum(wgt[..., None] * o, axis=0).astype(h.dtype)
    return mixed.reshape(b, s, GROUP_WIDTH) @ w_o


def _conformer_conv(h, w_pw1, b_pw1, w_dw, b_dw, ln_g, ln_b, w_pw2, b_pw2):
    a, gate = jnp.split(h @ w_pw1 + b_pw1, 2, axis=-1)
    u = a * jax.nn.sigmoid(gate)
    u = _causal_depthwise_conv(u, w_dw) + b_dw
    u = jax.nn.silu(_layernorm(u, ln_g, ln_b))
    return u @ w_pw2 + b_pw2


def _conv_ffn(h, w_up, w_dw, b_dw, w_down):
    ug = _causal_depthwise_conv(h @ w_up, w_dw) + b_dw
    up, gate = jnp.split(ug, 2, axis=-1)
    return (jax.nn.silu(gate) * up) @ w_down


def _fwd_setup_inputs(seed: int = 0) -> dict:
    key = jax.random.key(seed)
    ks = jax.random.split(key, 20)
    n_attn = (DEPTH + N_MIXERS - 1) // N_MIXERS
    n_conv = DEPTH // N_MIXERS
    f32 = jnp.float32
    nrm = lambda k, shape, fan: jax.random.normal(k, shape, f32) * (fan ** -0.5)
    small = lambda k, shape: 0.02 * jax.random.normal(k, shape, f32)
    x = jax.random.normal(ks[0], (BATCH, SEQ, D_MODEL), f32)
    offset = jax.random.randint(ks[1], (BATCH, 1), 0, 4096, dtype=jnp.int32)
    positions = (jnp.arange(SEQ, dtype=jnp.int32)[None, :] + offset).astype(jnp.int32)
    return {
        "x": x,
        "positions": positions,
        "norm_g": 1.0 + small(ks[2], (DEPTH, 4, D_MODEL)),
        "attn_w_qkv": nrm(ks[3], (n_attn, D_MODEL, 3 * N_GROUPS * GROUP_WIDTH), D_MODEL),
        "attn_w_o": nrm(ks[4], (n_attn, GROUP_WIDTH, D_MODEL), GROUP_WIDTH),
        "conv_w_pw1": nrm(ks[5], (n_conv, D_MODEL, 2 * CONV_INNER), D_MODEL),
        "conv_b_pw1": small(ks[6], (n_conv, 2 * CONV_INNER)),
        "conv_w_dw": nrm(ks[7], (n_conv, CONV_KERNEL, CONV_INNER), CONV_KERNEL),
        "conv_b_dw": small(ks[8], (n_conv, CONV_INNER)),
        "conv_ln_g": 1.0 + small(ks[9], (n_conv, CONV_INNER)),
        "conv_ln_b": small(ks[10], (n_conv, CONV_INNER)),
        "conv_w_pw2": nrm(ks[11], (n_conv, CONV_INNER, D_MODEL), CONV_INNER),
        "conv_b_pw2": small(ks[12], (n_conv, D_MODEL)),
        "ffn_w_up": nrm(ks[13], (DEPTH, D_MODEL, 2 * FFN_DIM), D_MODEL),
        "ffn_w_dw": nrm(ks[14], (DEPTH, FFN_CONV, 2 * FFN_DIM), FFN_CONV),
        "ffn_b_dw": small(ks[15], (DEPTH, 2 * FFN_DIM)),
        "ffn_w_down": nrm(ks[16], (DEPTH, FFN_DIM, D_MODEL), FFN_DIM),
    }


def _fwd_reference(x, positions, norm_g, attn_w_qkv, attn_w_o, conv_w_pw1, conv_b_pw1,
              conv_w_dw, conv_b_dw, conv_ln_g, conv_ln_b, conv_w_pw2, conv_b_pw2,
              ffn_w_up, ffn_w_dw, ffn_b_dw, ffn_w_down):
    for i in range(DEPTH):
        g = norm_g[i]
        j = i // N_MIXERS
        hn = _rmsnorm(x, g[0])
        if i % N_MIXERS == 0:
            y = _dilated_attention(hn, positions, attn_w_qkv[j], attn_w_o[j])
        else:
            y = _conformer_conv(hn, conv_w_pw1[j], conv_b_pw1[j], conv_w_dw[j],
                                conv_b_dw[j], conv_ln_g[j], conv_ln_b[j],
                                conv_w_pw2[j], conv_b_pw2[j])
        x = x + _rmsnorm(y, g[1])
        hn = _rmsnorm(x, g[2])
        y = _conv_ffn(hn, ffn_w_up[i], ffn_w_dw[i], ffn_b_dw[i], ffn_w_down[i])
        x = x + _rmsnorm(y, g[3])
    return x


import jax as _jax
import jax.numpy as _jnp

TWIN_FORMAT = 'train_step'
FWD_PARAMS = ['x', 'positions', 'norm_g', 'attn_w_qkv', 'attn_w_o', 'conv_w_pw1', 'conv_b_pw1', 'conv_w_dw', 'conv_b_dw', 'conv_ln_g', 'conv_ln_b', 'conv_w_pw2', 'conv_b_pw2', 'ffn_w_up', 'ffn_w_dw', 'ffn_b_dw', 'ffn_w_down']
TWIN_WEIGHTS = ['norm_g', 'attn_w_qkv', 'attn_w_o', 'conv_w_pw1', 'conv_b_pw1', 'conv_w_dw', 'conv_b_dw', 'conv_ln_g', 'conv_ln_b', 'conv_w_pw2', 'conv_b_pw2', 'ffn_w_up', 'ffn_w_dw', 'ffn_b_dw', 'ffn_w_down']
TWIN_DIFF_INPUT = 'x'
TWIN_INPUTS = ['x', 'positions', 'norm_g', 'attn_w_qkv', 'attn_w_o', 'conv_w_pw1', 'conv_b_pw1', 'conv_w_dw', 'conv_b_dw', 'conv_ln_g', 'conv_ln_b', 'conv_w_pw2', 'conv_b_pw2', 'ffn_w_up', 'ffn_w_dw', 'ffn_b_dw', 'ffn_w_down', 'loss_target', 'm_norm_g', 'm_attn_w_qkv', 'm_attn_w_o', 'm_conv_w_pw1', 'm_conv_b_pw1', 'm_conv_w_dw', 'm_conv_b_dw', 'm_conv_ln_g', 'm_conv_ln_b', 'm_conv_w_pw2', 'm_conv_b_pw2', 'm_ffn_w_up', 'm_ffn_w_dw', 'm_ffn_b_dw', 'm_ffn_w_down', 'v_norm_g', 'v_attn_w_qkv', 'v_attn_w_o', 'v_conv_w_pw1', 'v_conv_b_pw1', 'v_conv_w_dw', 'v_conv_b_dw', 'v_conv_ln_g', 'v_conv_ln_b', 'v_conv_w_pw2', 'v_conv_b_pw2', 'v_ffn_w_up', 'v_ffn_w_dw', 'v_ffn_b_dw', 'v_ffn_w_down']
TWIN_OUTPUTS = ['loss', 'grad_x', 'grad_norm_g', 'grad_attn_w_qkv', 'grad_attn_w_o', 'grad_conv_w_pw1', 'grad_conv_b_pw1', 'grad_conv_w_dw', 'grad_conv_b_dw', 'grad_conv_ln_g', 'grad_conv_ln_b', 'grad_conv_w_pw2', 'grad_conv_b_pw2', 'grad_ffn_w_up', 'grad_ffn_w_dw', 'grad_ffn_b_dw', 'grad_ffn_w_down', 'delta_norm_g', 'delta_attn_w_qkv', 'delta_attn_w_o', 'delta_conv_w_pw1', 'delta_conv_b_pw1', 'delta_conv_w_dw', 'delta_conv_b_dw', 'delta_conv_ln_g', 'delta_conv_ln_b', 'delta_conv_w_pw2', 'delta_conv_b_pw2', 'delta_ffn_w_up', 'delta_ffn_w_dw', 'delta_ffn_b_dw', 'delta_ffn_w_down', 'new_m_norm_g', 'new_m_attn_w_qkv', 'new_m_attn_w_o', 'new_m_conv_w_pw1', 'new_m_conv_b_pw1', 'new_m_conv_w_dw', 'new_m_conv_b_dw', 'new_m_conv_ln_g', 'new_m_conv_ln_b', 'new_m_conv_w_pw2', 'new_m_conv_b_pw2', 'new_m_ffn_w_up', 'new_m_ffn_w_dw', 'new_m_ffn_b_dw', 'new_m_ffn_w_down', 'new_v_norm_g', 'new_v_attn_w_qkv', 'new_v_attn_w_o', 'new_v_conv_w_pw1', 'new_v_conv_b_pw1', 'new_v_conv_w_dw', 'new_v_conv_b_dw', 'new_v_conv_ln_g', 'new_v_conv_ln_b', 'new_v_conv_w_pw2', 'new_v_conv_b_pw2', 'new_v_ffn_w_up', 'new_v_ffn_w_dw', 'new_v_ffn_b_dw', 'new_v_ffn_w_down']
TWIN_LEAF_KINDS = {'loss': 'loss', 'grad_x': 'grad_x', 'grad_norm_g': 'grad_w', 'grad_attn_w_qkv': 'grad_w', 'grad_attn_w_o': 'grad_w', 'grad_conv_w_pw1': 'grad_w', 'grad_conv_b_pw1': 'grad_w', 'grad_conv_w_dw': 'grad_w', 'grad_conv_b_dw': 'grad_w', 'grad_conv_ln_g': 'grad_w', 'grad_conv_ln_b': 'grad_w', 'grad_conv_w_pw2': 'grad_w', 'grad_conv_b_pw2': 'grad_w', 'grad_ffn_w_up': 'grad_w', 'grad_ffn_w_dw': 'grad_w', 'grad_ffn_b_dw': 'grad_w', 'grad_ffn_w_down': 'grad_w', 'delta_norm_g': 'delta_w', 'delta_attn_w_qkv': 'delta_w', 'delta_attn_w_o': 'delta_w', 'delta_conv_w_pw1': 'delta_w', 'delta_conv_b_pw1': 'delta_w', 'delta_conv_w_dw': 'delta_w', 'delta_conv_b_dw': 'delta_w', 'delta_conv_ln_g': 'delta_w', 'delta_conv_ln_b': 'delta_w', 'delta_conv_w_pw2': 'delta_w', 'delta_conv_b_pw2': 'delta_w', 'delta_ffn_w_up': 'delta_w', 'delta_ffn_w_dw': 'delta_w', 'delta_ffn_b_dw': 'delta_w', 'delta_ffn_w_down': 'delta_w', 'new_m_norm_g': 'new_m', 'new_m_attn_w_qkv': 'new_m', 'new_m_attn_w_o': 'new_m', 'new_m_conv_w_pw1': 'new_m', 'new_m_conv_b_pw1': 'new_m', 'new_m_conv_w_dw': 'new_m', 'new_m_conv_b_dw': 'new_m', 'new_m_conv_ln_g': 'new_m', 'new_m_conv_ln_b': 'new_m', 'new_m_conv_w_pw2': 'new_m', 'new_m_conv_b_pw2': 'new_m', 'new_m_ffn_w_up': 'new_m', 'new_m_ffn_w_dw': 'new_m', 'new_m_ffn_b_dw': 'new_m', 'new_m_ffn_w_down': 'new_m', 'new_v_norm_g': 'new_v', 'new_v_attn_w_qkv': 'new_v', 'new_v_attn_w_o': 'new_v', 'new_v_conv_w_pw1': 'new_v', 'new_v_conv_b_pw1': 'new_v', 'new_v_conv_w_dw': 'new_v', 'new_v_conv_b_dw': 'new_v', 'new_v_conv_ln_g': 'new_v', 'new_v_conv_ln_b': 'new_v', 'new_v_conv_w_pw2': 'new_v', 'new_v_conv_b_pw2': 'new_v', 'new_v_ffn_w_up': 'new_v', 'new_v_ffn_w_dw': 'new_v', 'new_v_ffn_b_dw': 'new_v', 'new_v_ffn_w_down': 'new_v'}


def _forward(args):
    return _fwd_reference(*[args[k] for k in FWD_PARAMS])


def _output_shape():
    def fwd():
        inp = _fwd_setup_inputs(0)
        return _fwd_reference(*[inp[k] for k in FWD_PARAMS])
    out = _jax.eval_shape(fwd)
    return out.shape, out.dtype

N_MICROBATCH = 1
ADAM_LR = 0.001
ADAM_B1 = 0.9
ADAM_B2 = 0.999
ADAM_EPS = 1e-08
ADAM_WD = 0.01
ADAM_STEP = 10
PER_EXAMPLE_BATCH_AXIS = {'x': 0, 'positions': 0, 'loss_target': 0}
SHARED_INPUTS = []
_WEIGHT_DTYPES = {'norm_g': _jnp.float32, 'attn_w_qkv': _jnp.float32, 'attn_w_o': _jnp.float32, 'conv_w_pw1': _jnp.float32, 'conv_b_pw1': _jnp.float32, 'conv_w_dw': _jnp.float32, 'conv_b_dw': _jnp.float32, 'conv_ln_g': _jnp.float32, 'conv_ln_b': _jnp.float32, 'conv_w_pw2': _jnp.float32, 'conv_b_pw2': _jnp.float32, 'ffn_w_up': _jnp.float32, 'ffn_w_dw': _jnp.float32, 'ffn_b_dw': _jnp.float32, 'ffn_w_down': _jnp.float32}
MOMENT_SCALE = {'norm_g': 4.498812e+01, 'attn_w_qkv': 1.081384e+00, 'attn_w_o': 1.584148e+00, 'conv_w_pw1': 6.509334e-01, 'conv_b_pw1': 3.100585e+00, 'conv_w_dw': 9.496529e-01, 'conv_b_dw': 7.976102e+00, 'conv_ln_g': 3.211299e+00, 'conv_ln_b': 4.602006e+00, 'conv_w_pw2': 1.964754e+00, 'conv_b_pw2': 9.474748e+00, 'ffn_w_up': 6.177589e-01, 'ffn_w_dw': 7.138833e-01, 'ffn_b_dw': 3.810556e+00, 'ffn_w_down': 1.234842e+00}


def _to_microbatches(a, axis):
    t = _jnp.moveaxis(a, axis, 0)
    t = t.reshape((N_MICROBATCH, t.shape[0] // N_MICROBATCH) + t.shape[1:])
    return _jnp.moveaxis(t, 1, axis + 1)


def setup_inputs(seed: int = 0) -> dict:
    inp = _fwd_setup_inputs(seed)
    key = _jax.random.fold_in(_jax.random.key(seed), 7919)
    shape, _ = _output_shape()
    out = dict(inp)
    out["loss_target"] = _jax.random.normal(_jax.random.fold_in(key, 0), shape, _jnp.float32)
    for i, name in enumerate(TWIN_WEIGHTS):
        w = inp[name].astype(_jnp.float32)
        if MOMENT_SCALE is None:
            s = _jnp.sqrt(_jnp.mean(_jnp.square(w)) + 1e-30)
        else:
            s = MOMENT_SCALE[name]
        km, kv = _jax.random.split(_jax.random.fold_in(key, i + 1))
        out[name] = w
        out["m_" + name] = s * _jax.random.normal(km, w.shape, _jnp.float32)
        out["v_" + name] = (s * s) * _jax.random.uniform(kv, w.shape, _jnp.float32, 0.5, 1.5)
    if N_MICROBATCH > 1:
        for name, axis in PER_EXAMPLE_BATCH_AXIS.items():
            out[name] = _to_microbatches(out[name], axis)
    return {'x': out['x'], 'positions': out['positions'], 'norm_g': out['norm_g'], 'attn_w_qkv': out['attn_w_qkv'], 'attn_w_o': out['attn_w_o'], 'conv_w_pw1': out['conv_w_pw1'], 'conv_b_pw1': out['conv_b_pw1'], 'conv_w_dw': out['conv_w_dw'], 'conv_b_dw': out['conv_b_dw'], 'conv_ln_g': out['conv_ln_g'], 'conv_ln_b': out['conv_ln_b'], 'conv_w_pw2': out['conv_w_pw2'], 'conv_b_pw2': out['conv_b_pw2'], 'ffn_w_up': out['ffn_w_up'], 'ffn_w_dw': out['ffn_w_dw'], 'ffn_b_dw': out['ffn_b_dw'], 'ffn_w_down': out['ffn_w_down'], 'loss_target': out['loss_target'], 'm_norm_g': out['m_norm_g'], 'm_attn_w_qkv': out['m_attn_w_qkv'], 'm_attn_w_o': out['m_attn_w_o'], 'm_conv_w_pw1': out['m_conv_w_pw1'], 'm_conv_b_pw1': out['m_conv_b_pw1'], 'm_conv_w_dw': out['m_conv_w_dw'], 'm_conv_b_dw': out['m_conv_b_dw'], 'm_conv_ln_g': out['m_conv_ln_g'], 'm_conv_ln_b': out['m_conv_ln_b'], 'm_conv_w_pw2': out['m_conv_w_pw2'], 'm_conv_b_pw2': out['m_conv_b_pw2'], 'm_ffn_w_up': out['m_ffn_w_up'], 'm_ffn_w_dw': out['m_ffn_w_dw'], 'm_ffn_b_dw': out['m_ffn_b_dw'], 'm_ffn_w_down': out['m_ffn_w_down'], 'v_norm_g': out['v_norm_g'], 'v_attn_w_qkv': out['v_attn_w_qkv'], 'v_attn_w_o': out['v_attn_w_o'], 'v_conv_w_pw1': out['v_conv_w_pw1'], 'v_conv_b_pw1': out['v_conv_b_pw1'], 'v_conv_w_dw': out['v_conv_w_dw'], 'v_conv_b_dw': out['v_conv_b_dw'], 'v_conv_ln_g': out['v_conv_ln_g'], 'v_conv_ln_b': out['v_conv_ln_b'], 'v_conv_w_pw2': out['v_conv_w_pw2'], 'v_conv_b_pw2': out['v_conv_b_pw2'], 'v_ffn_w_up': out['v_ffn_w_up'], 'v_ffn_w_dw': out['v_ffn_w_dw'], 'v_ffn_b_dw': out['v_ffn_b_dw'], 'v_ffn_w_down': out['v_ffn_w_down']}


def _loss(weights, diff, rest, loss_target):
    with _jax.named_scope("forward"):
        args = {**rest, TWIN_DIFF_INPUT: diff, **{k: w.astype(_WEIGHT_DTYPES[k]) for k, w in weights.items()}}
        y = _forward(args)
    with _jax.named_scope("loss_head"):
        err = _jnp.square(y.astype(_jnp.float32) - loss_target)
        return 0.5 * _jnp.sum(_jnp.mean(err, axis=-1)) if err.ndim else 0.5 * err


def _adamw(w, g, m, v):
    m = ADAM_B1 * m + (1.0 - ADAM_B1) * g
    v = ADAM_B2 * v + (1.0 - ADAM_B2) * _jnp.square(g)
    m_hat = m / (1.0 - ADAM_B1 ** ADAM_STEP)
    v_hat = v / (1.0 - ADAM_B2 ** ADAM_STEP)
    delta = -ADAM_LR * (m_hat / (_jnp.sqrt(v_hat) + ADAM_EPS) + ADAM_WD * w)
    return delta, m, v


def reference(x, positions, norm_g, attn_w_qkv, attn_w_o, conv_w_pw1, conv_b_pw1, conv_w_dw, conv_b_dw, conv_ln_g, conv_ln_b, conv_w_pw2, conv_b_pw2, ffn_w_up, ffn_w_dw, ffn_b_dw, ffn_w_down, loss_target, m_norm_g, m_attn_w_qkv, m_attn_w_o, m_conv_w_pw1, m_conv_b_pw1, m_conv_w_dw, m_conv_b_dw, m_conv_ln_g, m_conv_ln_b, m_conv_w_pw2, m_conv_b_pw2, m_ffn_w_up, m_ffn_w_dw, m_ffn_b_dw, m_ffn_w_down, v_norm_g, v_attn_w_qkv, v_attn_w_o, v_conv_w_pw1, v_conv_b_pw1, v_conv_w_dw, v_conv_b_dw, v_conv_ln_g, v_conv_ln_b, v_conv_w_pw2, v_conv_b_pw2, v_ffn_w_up, v_ffn_w_dw, v_ffn_b_dw, v_ffn_w_down):
    given = dict(x=x, positions=positions, norm_g=norm_g, attn_w_qkv=attn_w_qkv, attn_w_o=attn_w_o, conv_w_pw1=conv_w_pw1, conv_b_pw1=conv_b_pw1, conv_w_dw=conv_w_dw, conv_b_dw=conv_b_dw, conv_ln_g=conv_ln_g, conv_ln_b=conv_ln_b, conv_w_pw2=conv_w_pw2, conv_b_pw2=conv_b_pw2, ffn_w_up=ffn_w_up, ffn_w_dw=ffn_w_dw, ffn_b_dw=ffn_b_dw, ffn_w_down=ffn_w_down, loss_target=loss_target, m_norm_g=m_norm_g, m_attn_w_qkv=m_attn_w_qkv, m_attn_w_o=m_attn_w_o, m_conv_w_pw1=m_conv_w_pw1, m_conv_b_pw1=m_conv_b_pw1, m_conv_w_dw=m_conv_w_dw, m_conv_b_dw=m_conv_b_dw, m_conv_ln_g=m_conv_ln_g, m_conv_ln_b=m_conv_ln_b, m_conv_w_pw2=m_conv_w_pw2, m_conv_b_pw2=m_conv_b_pw2, m_ffn_w_up=m_ffn_w_up, m_ffn_w_dw=m_ffn_w_dw, m_ffn_b_dw=m_ffn_b_dw, m_ffn_w_down=m_ffn_w_down, v_norm_g=v_norm_g, v_attn_w_qkv=v_attn_w_qkv, v_attn_w_o=v_attn_w_o, v_conv_w_pw1=v_conv_w_pw1, v_conv_b_pw1=v_conv_b_pw1, v_conv_w_dw=v_conv_w_dw, v_conv_b_dw=v_conv_b_dw, v_conv_ln_g=v_conv_ln_g, v_conv_ln_b=v_conv_ln_b, v_conv_w_pw2=v_conv_w_pw2, v_conv_b_pw2=v_conv_b_pw2, v_ffn_w_up=v_ffn_w_up, v_ffn_w_dw=v_ffn_w_dw, v_ffn_b_dw=v_ffn_b_dw, v_ffn_w_down=v_ffn_w_down)
    weights = {n: given[n] for n in TWIN_WEIGHTS}
    shared = {n: given[n] for n in SHARED_INPUTS}
    per_example = {n: given[n] for n in ['x', 'positions']}
    grad_fn = _jax.value_and_grad(_loss, argnums=(0, 1))

    def one_microbatch(ex, loss_target):
        ex = dict(ex)
        diff = ex.pop(TWIN_DIFF_INPUT)
        return grad_fn(weights, diff, {**shared, **ex}, loss_target)

    if N_MICROBATCH == 1:
        loss, (grad_w, grad_x) = one_microbatch(per_example, given["loss_target"])
    else:
        def body(carry, xs):
            loss_sum, grad_sum = carry
            l_k, (gw_k, gx_k) = one_microbatch(xs[0], xs[1])
            with _jax.named_scope("update"):
                return (loss_sum + l_k, _jax.tree.map(_jnp.add, grad_sum, gw_k)), gx_k

        init = (_jnp.zeros((), _jnp.float32), _jax.tree.map(_jnp.zeros_like, weights))
        (loss, grad_w), grad_x = _jax.lax.scan(body, init, (per_example, given["loss_target"]))
    with _jax.named_scope("update"):
        delta_w, new_m, new_v = {}, {}, {}
        for n in TWIN_WEIGHTS:
            delta_w[n], new_m[n], new_v[n] = _adamw(weights[n], grad_w[n], given["m_" + n], given["v_" + n])
    return (loss, grad_x, *[grad_w[n] for n in TWIN_WEIGHTS], *[delta_w[n] for n in TWIN_WEIGHTS],
            *[new_m[n] for n in TWIN_WEIGHTS], *[new_v[n] for n in TWIN_WEIGHTS])
```

```python
import functools
import math

import numpy as np
import jax
import jax.numpy as jnp
from jax import lax
from jax.experimental import pallas as pl
from jax.experimental.pallas import tpu as pltpu

F32 = jnp.float32
BF16 = jnp.bfloat16
EPS = 1e-6
N_DEV = 8
HEAD_DIM = 64
GROUP_WIDTH = 512
DILATIONS = (1, 4, 16)
SPAN = 128
ROT_DIM = 16
ROPE_THETA = 500000.0
CONV_KERNEL = 31
CONV_HALO = 32
FFN_CONV = 3
ADAM_LR, ADAM_B1, ADAM_B2, ADAM_EPS, ADAM_WD, ADAM_STEP = 0.001, 0.9, 0.999, 1e-08, 0.01, 10
VMEM_LIMIT_BYTES = 56 * 1024 * 1024
PACK_COLS = 1024
MESH = pl.DeviceIdType.MESH
ANY = pl.BlockSpec(memory_space=pl.ANY)


def _params(*sem):
    return pltpu.CompilerParams(dimension_semantics=sem, vmem_limit_bytes=VMEM_LIMIT_BYTES)


def _sigmoid(v):
    return 1.0 / (1.0 + jnp.exp(-v))


def _full(shape):
    return pl.BlockSpec(shape, lambda *_: (0,) * len(shape))


def _rows(tm, width):
    return pl.BlockSpec((tm, width), lambda i, *_: (i, 0))


def _all_gather(shard, name):
    r, c_ = shard.shape

    def body(x_ref, out_ref, send_sems, recv_sems, local_sem):
        x, y, c = lax.axis_index("x"), lax.axis_index("y"), lax.axis_index("c")
        me, sibling = (x, y, c), (x, y, 1 - c)
        chips = [(1 - x, y), (x, 1 - y), (1 - x, 1 - y)]

        def rows(px, py, pc):
            return out_ref.at[4 * px + 2 * py + pc]

        def copy(k, block, to, src=None):
            return pltpu.make_async_remote_copy(
                src_ref=rows(*block) if src is None else src, dst_ref=rows(*block),
                send_sem=send_sems.at[k], recv_sem=recv_sems.at[k], device_id=to, device_id_type=MESH)

        mine = pltpu.make_async_copy(x_ref, rows(*me), local_sem)
        mine.start()
        first = [copy(0, me, sibling, src=x_ref)]
        first += [copy(1 + j, me, (*chip, c), src=x_ref) for j, chip in enumerate(chips)]
        for cp in first:
            cp.start()
        passed = [copy(4 + j, (*chip, c), sibling) for j, chip in enumerate(chips)]
        for j, chip in enumerate(chips):
            copy(1 + j, (*chip, c), me).wait_recv()
            passed[j].start()
        copy(0, sibling, me).wait_recv()
        for j, chip in enumerate(chips):
            copy(4 + j, (*chip, 1 - c), me).wait_recv()
        for cp in first + passed:
            cp.wait_send()
        mine.wait()

    return pl.pallas_call(
        body, name=name, out_shape=jax.ShapeDtypeStruct((N_DEV, r, c_), shard.dtype),
        in_specs=[ANY], out_specs=ANY,
        scratch_shapes=[pltpu.SemaphoreType.DMA((7,)), pltpu.SemaphoreType.DMA((7,)), pltpu.SemaphoreType.DMA],
    )(shard)


def _rs_sibling(g):
    _, r, c_ = g.shape

    def body(g_ref, out_ref, send_sems, recv_sems):
        x, y, c = lax.axis_index("x"), lax.axis_index("y"), lax.axis_index("c")
        copies = [pltpu.make_async_remote_copy(
            src_ref=g_ref.at[2 * q + (1 - c)], dst_ref=out_ref.at[q], send_sem=send_sems.at[q],
            recv_sem=recv_sems.at[q], device_id=(x, y, 1 - c), device_id_type=MESH) for q in range(4)]
        for cp in copies:
            cp.start()
        for cp in copies:
            cp.wait_recv()
        for cp in copies:
            cp.wait_send()

    return pl.pallas_call(
        body, name="rs_sibling", out_shape=jax.ShapeDtypeStruct((4, r, c_), g.dtype),
        in_specs=[ANY], out_specs=ANY,
        scratch_shapes=[pltpu.SemaphoreType.DMA((4,)), pltpu.SemaphoreType.DMA((4,))],
    )(g)


def _rs_pair_add(g, got, core):
    _, r, c_ = g.shape
    tr = 128

    def body(core_ref, g_ref, got_ref, o_ref):
        o_ref[...] = g_ref[...] + got_ref[...]

    return pl.pallas_call(
        body, name="rs_pair_add", out_shape=jax.ShapeDtypeStruct((4, r, c_), F32),
        grid_spec=pltpu.PrefetchScalarGridSpec(
            num_scalar_prefetch=1, grid=(4, r // tr),
            in_specs=[pl.BlockSpec((None, tr, c_), lambda q, i, core_ref: (2 * q + core_ref[0], i, 0)),
                      pl.BlockSpec((None, tr, c_), lambda q, i, core_ref: (q, i, 0))],
            out_specs=pl.BlockSpec((None, tr, c_), lambda q, i, core_ref: (q, i, 0))),
        compiler_params=_params("parallel", "parallel"),
    )(core, g, got)


def _rs_chips(part):
    _, r, c_ = part.shape

    def body(p_ref, out_ref, send_sems, recv_sems, local_sem):
        x, y, c = lax.axis_index("x"), lax.axis_index("y"), lax.axis_index("c")
        my_chip = 2 * x + y
        chips = [(1 - x, y), (x, 1 - y), (1 - x, 1 - y)]
        mine = pltpu.make_async_copy(p_ref.at[my_chip], out_ref.at[my_chip], local_sem)
        mine.start()
        copies = [pltpu.make_async_remote_copy(
            src_ref=p_ref.at[2 * qx + qy], dst_ref=out_ref.at[my_chip], send_sem=send_sems.at[k],
            recv_sem=recv_sems.at[k], device_id=(qx, qy, c), device_id_type=MESH)
            for k, (qx, qy) in enumerate(chips)]
        for cp in copies:
            cp.start()
        for cp in copies:
            cp.wait_recv()
        for cp in copies:
            cp.wait_send()
        mine.wait()

    return pl.pallas_call(
        body, name="rs_chips", out_shape=jax.ShapeDtypeStruct((4, r, c_), part.dtype),
        in_specs=[ANY], out_specs=ANY,
        scratch_shapes=[pltpu.SemaphoreType.DMA((3,)), pltpu.SemaphoreType.DMA((3,)), pltpu.SemaphoreType.DMA],
    )(part)


def _adamw_math(w, g, m, v):
    m = ADAM_B1 * m + (1.0 - ADAM_B1) * g
    v = ADAM_B2 * v + (1.0 - ADAM_B2) * (g * g)
    m_hat = m / (1.0 - ADAM_B1 ** ADAM_STEP)
    v_hat = v / (1.0 - ADAM_B2 ** ADAM_STEP)
    delta = -ADAM_LR * (m_hat / (jnp.sqrt(v_hat) + ADAM_EPS) + ADAM_WD * w)
    return delta, m, v


def _sum_adamw(parts, w, m, v, name):
    n, r, c_ = parts.shape
    tr = 128 if r % 128 == 0 else r

    def body(p_ref, w_ref, m_ref, v_ref, g_ref, d_ref, nm_ref, nv_ref):
        g = p_ref[0]
        for s in range(1, n):
            g = g + p_ref[s]
        d, nm, nv = _adamw_math(w_ref[...], g, m_ref[...], v_ref[...])
        g_ref[...] = g
        d_ref[...] = d
        nm_ref[...] = nm
        nv_ref[...] = nv

    spec = pl.BlockSpec((tr, c_), lambda i: (i, 0))
    return pl.pallas_call(
        body, name=name, out_shape=[jax.ShapeDtypeStruct((r, c_), F32)] * 4, grid=(r // tr,),
        in_specs=[pl.BlockSpec((n, tr, c_), lambda i: (0, i, 0)), spec, spec, spec], out_specs=[spec] * 4,
        compiler_params=_params("parallel"),
    )(parts, w, m, v)


def _rope_tables(pos_col, freq_row):
    s = pos_col.shape[0]
    tm = min(1024, s)

    def body(p_ref, f_ref, c_ref, su_ref, sd_ref):
        ang = p_ref[...].astype(F32) * f_ref[...]
        lane = lax.broadcasted_iota(jnp.int32, ang.shape, 1) & (HEAD_DIM - 1)
        cs, sn = jnp.cos(ang), jnp.sin(ang)
        c_ref[...] = jnp.where(lane < ROT_DIM, cs, 1.0)
        su_ref[...] = jnp.where((lane >= ROT_DIM // 2) & (lane < ROT_DIM), sn, 0.0)
        sd_ref[...] = jnp.where(lane < ROT_DIM // 2, -sn, 0.0)

    return pl.pallas_call(
        body, name="rope_tables", out_shape=[jax.ShapeDtypeStruct((s, 128), F32)] * 3, grid=(s // tm,),
        in_specs=[pl.BlockSpec((tm, 1), lambda i: (i, 0)), _full((1, 128))],
        out_specs=[_rows(tm, 128)] * 3, compiler_params=_params("parallel"),
    )(pos_col, freq_row)


def _rope_apply(t, cos, sin_up, sin_dn):
    w = t.shape[1]
    return t * cos + pltpu.roll(t, 8, 1) * sin_up + pltpu.roll(t, w - 8, 1) * sin_dn


def _rope_transpose(dr, cos, sin_up, sin_dn):
    w = dr.shape[1]
    return dr * cos + pltpu.roll(dr * sin_up, w - 8, 1) + pltpu.roll(dr * sin_dn, 8, 1)


def _norm_matmul(x, g, w, *, tn, name, bias=None, rope=None, rope_blocks=0, tm=512):
    s, d = x.shape
    n = w.shape[1]
    tm = min(tm, s)

    def body(*refs):
        x_ref, g_ref, w_ref = refs[:3]
        k = 3
        b_ref = None
        if bias is not None:
            b_ref = refs[k]
            k += 1
        if rope is not None:
            c_ref, su_ref, sd_ref = refs[k:k + 3]
            k += 3
        h_ref, o_ref = refs[k:k + 2]
        j = pl.program_id(1)

        @pl.when(j == 0)
        def _():
            xv = x_ref[...]
            r = lax.rsqrt(jnp.mean(xv * xv, axis=-1, keepdims=True) + EPS)
            h_ref[...] = (xv * r * g_ref[...]).astype(BF16)

        acc = jnp.dot(h_ref[...], w_ref[...], preferred_element_type=F32)
        if b_ref is not None:
            acc = acc + b_ref[...]
        if rope is None:
            o_ref[...] = acc.astype(BF16)
        else:
            @pl.when(j < rope_blocks)
            def _():
                reps = tn // 128
                o_ref[...] = _rope_apply(acc, jnp.tile(c_ref[...], (1, reps)), jnp.tile(su_ref[...], (1, reps)),
                                         jnp.tile(sd_ref[...], (1, reps))).astype(BF16)

            @pl.when(j >= rope_blocks)
            def _():
                o_ref[...] = acc.astype(BF16)

    in_specs = [_rows(tm, d), _full((1, d)), pl.BlockSpec((d, tn), lambda i, j: (0, j))]
    args = [x, g, w]
    if bias is not None:
        in_specs.append(pl.BlockSpec((1, tn), lambda i, j: (0, j)))
        args.append(bias)
    if rope is not None:
        in_specs += [_rows(tm, 128)] * 3
        args += list(rope)
    return pl.pallas_call(
        body, name=name,
        out_shape=[jax.ShapeDtypeStruct((s, d), BF16), jax.ShapeDtypeStruct((s, n), BF16)],
        grid=(s // tm, n // tn), in_specs=in_specs,
        out_specs=[_rows(tm, d), pl.BlockSpec((tm, tn), lambda i, j: (i, j))],
        compiler_params=_params("parallel", "arbitrary"),
    )(*args)


def _head_masks(rows=SPAN):
    lane = lax.broadcasted_iota(jnp.int32, (rows, 128), 1)
    masks = [lane < HEAD_DIM, lane >= HEAD_DIM]
    lane1 = lax.broadcasted_iota(jnp.int32, (1, 128), 1)
    keep = [jnp.where(lane1 < HEAD_DIM, 1.0, 0.0).astype(BF16), jnp.where(lane1 >= HEAD_DIM, 1.0, 0.0).astype(BF16)]
    return masks, keep


def _attn_fwd(qkv, grp, dil):
    s, ncol = qkv.shape
    nblk = ncol // GROUP_WIDTH
    l = s // dil
    nb = l // SPAN
    qv = qkv.reshape(l, dil * ncol)

    def body(q_ref, kp_ref, kc_ref, vp_ref, vc_ref, o_ref, l_ref):
        b = pl.program_id(1)
        row = lax.broadcasted_iota(jnp.int32, (SPAN, 2 * SPAN), 0)
        col = lax.broadcasted_iota(jnp.int32, (SPAN, 2 * SPAN), 1)
        no_prev = jnp.where(b > 0, 0, 4 * SPAN)
        valid = ((col < SPAN) & (col >= row + no_prev)) | ((col >= SPAN) & (col - SPAN <= row))
        masks, keep = _head_masks()
        for p in range(GROUP_WIDTH // 128):
            sl = slice(p * 128, (p + 1) * 128)
            qp = q_ref[:, sl]
            kk = jnp.concatenate([kp_ref[:, sl], kc_ref[:, sl]], axis=0)
            vv = jnp.concatenate([vp_ref[:, sl], vc_ref[:, sl]], axis=0)
            outs, lses = [], []
            for h in range(2):
                sc = lax.dot_general(qp * keep[h], kk, (((1,), (1,)), ((), ())), preferred_element_type=F32) * (HEAD_DIM ** -0.5)
                sc = jnp.where(valid, sc, -1e30)
                mx = jnp.max(sc, axis=-1, keepdims=True)
                pe = jnp.exp(sc - mx)
                den = jnp.sum(pe, axis=-1, keepdims=True)
                pv = jnp.dot(pe.astype(BF16), vv, preferred_element_type=F32)
                outs.append(pv / den)
                lses.append(jnp.broadcast_to(mx + jnp.log(den), (SPAN, 128)))
            o_ref[:, sl] = jnp.where(masks[0], outs[0], outs[1])
            l_ref[:, sl] = jnp.where(masks[0], lses[0], lses[1])

    blk = (SPAN, GROUP_WIDTH)
    cur = lambda t: pl.BlockSpec(blk, lambda r, b: (b, r * nblk + 3 * t + grp))
    prev = lambda t: pl.BlockSpec(blk, lambda r, b: (jnp.maximum(b - 1, 0), r * nblk + 3 * t + grp))
    out = pl.BlockSpec(blk, lambda r, b: (b, r))
    o, lse = pl.pallas_call(
        body, name=f"attn_fwd_g{grp}", out_shape=[jax.ShapeDtypeStruct((l, dil * GROUP_WIDTH), F32)] * 2,
        grid=(dil, nb), in_specs=[cur(0), prev(1), cur(1), prev(2), cur(2)], out_specs=[out, out],
        compiler_params=_params("parallel", "arbitrary"),
    )(qv, qv, qv, qv, qv)
    return o.reshape(s, GROUP_WIDTH), lse.reshape(s, GROUP_WIDTH)


def _resnorm_store(y, x_ref, g_ref, y_ref, xo_ref):
    r = lax.rsqrt(jnp.mean(y * y, axis=-1, keepdims=True) + EPS)
    y_ref[...] = y
    xo_ref[...] = x_ref[...] + y * r * g_ref[...]


def _mix_wo(os_, ls_, wo, x, g, tm=256):
    s, d = x.shape
    gw = wo.shape[0]
    tm = min(tm, s)

    def body(o0, o1, o2, l0, l1, l2, w_ref, x_ref, g_ref, y_ref, xo_ref, mixed_ref, lse_ref):
        a0, a1, a2 = l0[...], l1[...], l2[...]
        mx = jnp.maximum(jnp.maximum(a0, a1), a2)
        e0, e1, e2 = jnp.exp(a0 - mx), jnp.exp(a1 - mx), jnp.exp(a2 - mx)
        den = e0 + e1 + e2
        mixed = (e0 / den) * o0[...] + (e1 / den) * o1[...] + (e2 / den) * o2[...]
        mixed_ref[...] = mixed.astype(BF16)
        lse_ref[...] = mx + jnp.log(den)
        y = jnp.dot(mixed.astype(BF16), w_ref[...], preferred_element_type=F32)
        _resnorm_store(y, x_ref, g_ref, y_ref, xo_ref)

    return pl.pallas_call(
        body, name="mix_wo",
        out_shape=[jax.ShapeDtypeStruct((s, d), F32), jax.ShapeDtypeStruct((s, d), F32),
                   jax.ShapeDtypeStruct((s, gw), BF16), jax.ShapeDtypeStruct((s, gw), F32)],
        grid=(s // tm,), in_specs=[_rows(tm, gw)] * 6 + [_full((gw, d)), _rows(tm, d), _full((1, d))],
        out_specs=[_rows(tm, d), _rows(tm, d), _rows(tm, gw), _rows(tm, gw)],
        compiler_params=_params("parallel"),
    )(*os_, *ls_, wo, x, g)


def _matmul_resnorm(a, w, x, g, *, name, bias=None, tm=512):
    s, k = a.shape
    d = w.shape[1]
    tm = min(tm, s)

    def body(*refs):
        a_ref, w_ref = refs[:2]
        b_ref = refs[2] if bias is not None else None
        x_ref, g_ref, y_ref, xo_ref = refs[-4:]
        y = jnp.dot(a_ref[...], w_ref[...], preferred_element_type=F32)
        if b_ref is not None:
            y = y + b_ref[...]
        _resnorm_store(y, x_ref, g_ref, y_ref, xo_ref)

    in_specs = [_rows(tm, k), _full((k, d))] + ([_full((1, d))] if bias is not None else []) + [_rows(tm, d), _full((1, d))]
    args = [a, w] + ([bias] if bias is not None else []) + [x, g]
    return pl.pallas_call(
        body, name=name, out_shape=[jax.ShapeDtypeStruct((s, d), F32)] * 2, grid=(s // tm,),
        in_specs=in_specs, out_specs=[_rows(tm, d)] * 2, compiler_params=_params("parallel"),
    )(*args)


def _conv3_taps(z, halo, first):
    row = lax.broadcasted_iota(jnp.int32, z.shape, 0)
    halo = halo * jnp.where(first, 0.0, 1.0)
    h6, h7 = halo[6:7, :], halo[7:8, :]
    z1 = jnp.where(row == 0, h7, pltpu.roll(z, 1, 0))
    z2 = jnp.where(row == 0, h6, jnp.where(row == 1, h7, pltpu.roll(z, 2, 0)))
    return z2, z1


def _ffn_cols(f):
    return 256 if f % 256 == 0 else 128


def _ffn_act(z, w_dw, b_dw, tm=512):
    s, f2 = z.shape
    f = f2 // 2
    tm = min(tm, s)
    tc = _ffn_cols(f)
    nfc = f // tc

    def body(zu, zg, hu, hg, wu, wg, bu, bg, o_ref):
        first = pl.program_id(0) == 0

        def conv(z_ref, h_ref, w_ref, b_ref):
            zc = z_ref[...].astype(F32)
            z2, z1 = _conv3_taps(zc, h_ref[...].astype(F32), first)
            return w_ref[0:1, :] * z2 + w_ref[1:2, :] * z1 + w_ref[2:3, :] * zc + b_ref[...]

        up, gate = conv(zu, hu, wu, bu), conv(zg, hg, wg, bg)
        o_ref[...] = (gate * _sigmoid(gate) * up).astype(BF16)

    hb = tm // 8
    tile = lambda off: pl.BlockSpec((tm, tc), lambda i, j: (i, off + j))
    halo = lambda off: pl.BlockSpec((8, tc), lambda i, j: (jnp.maximum(i * hb - 1, 0), off + j))
    prm = lambda rows, off: pl.BlockSpec((rows, tc), lambda i, j: (0, off + j))
    return pl.pallas_call(
        body, name="ffn_act", out_shape=jax.ShapeDtypeStruct((s, f), BF16), grid=(s // tm, nfc),
        in_specs=[tile(0), tile(nfc), halo(0), halo(nfc), prm(FFN_CONV, 0), prm(FFN_CONV, nfc), prm(1, 0), prm(1, nfc)],
        out_specs=pl.BlockSpec((tm, tc), lambda i, j: (i, j)), compiler_params=_params("parallel", "parallel"),
    )(z, z, z, z, w_dw, w_dw, b_dw, b_dw)


def _glu_conv(ag_ref, halo_ref, w_ref, ext_ref, first, c):
    tm = ag_ref.shape[0]
    hal = halo_ref[...].astype(F32)
    ext_ref[0:CONV_HALO, :] = hal[:, :c] * _sigmoid(hal[:, c:]) * jnp.where(first, 0.0, 1.0)
    ag = ag_ref[...].astype(F32)
    ext_ref[CONV_HALO:, :] = ag[:, :c] * _sigmoid(ag[:, c:])
    base = CONV_HALO - (CONV_KERNEL - 1)
    acc = w_ref[0:1, :] * ext_ref[base:base + tm, :]
    for j in range(1, CONV_KERNEL):
        acc = acc + w_ref[j:j + 1, :] * ext_ref[base + j:base + j + tm, :]
    return acc


def _layernorm_stats(u1):
    mu = jnp.mean(u1, axis=-1, keepdims=True)
    cen = u1 - mu
    rstd = lax.rsqrt(jnp.mean(cen * cen, axis=-1, keepdims=True) + EPS)
    return cen * rstd, rstd


def _conv_mid(ag, w_dw, b_dw, ln_g, ln_b, tm=256):
    s, c2 = ag.shape
    c = c2 // 2
    tm = min(tm, s)

    def body(ag_ref, halo_ref, w_ref, b_ref, g_ref, bb_ref, o_ref, ext_ref):
        u1 = _glu_conv(ag_ref, halo_ref, w_ref, ext_ref, pl.program_id(0) == 0, c) + b_ref[...]
        xh, _ = _layernorm_stats(u1)
        u2 = xh * g_ref[...] + bb_ref[...]
        o_ref[...] = (u2 * _sigmoid(u2)).astype(BF16)

    hb = tm // CONV_HALO
    return pl.pallas_call(
        body, name="conv_mid", out_shape=jax.ShapeDtypeStruct((s, c), BF16), grid=(s // tm,),
        in_specs=[_rows(tm, c2), pl.BlockSpec((CONV_HALO, c2), lambda i: (jnp.maximum(i * hb - 1, 0), 0)),
                  _full((CONV_KERNEL, c)), _full((1, c)), _full((1, c)), _full((1, c))],
        out_specs=_rows(tm, c), scratch_shapes=[pltpu.VMEM((CONV_HALO + tm, c), F32)],
        compiler_params=_params("arbitrary"),
    )(ag, ag, w_dw, b_dw, ln_g, ln_b)


def _loss_grad(xo, target, tm=512):
    s, d = xo.shape
    tm = min(tm, s)

    def body(x_ref, t_ref, dx_ref, loss_ref):
        @pl.when(pl.program_id(0) == 0)
        def _():
            loss_ref[...] = jnp.zeros_like(loss_ref)

        err = x_ref[...] - t_ref[...]
        dx_ref[...] = err * (1.0 / d)
        loss_ref[...] += 0.5 * jnp.sum(jnp.mean(err * err, axis=-1, keepdims=True))

    return pl.pallas_call(
        body, name="loss_grad", out_shape=[jax.ShapeDtypeStruct((s, d), F32), jax.ShapeDtypeStruct((1, 128), F32)],
        grid=(s // tm,), in_specs=[_rows(tm, d)] * 2, out_specs=[_rows(tm, d), _full((1, 128))],
        compiler_params=_params("arbitrary"),
    )(xo, target)


def _postnorm_bwd(y, g, dxo, *, name, with_bias_grad=False, tm=512):
    s, d = y.shape
    tm = min(tm, s)

    def body(y_ref, g_ref, dx_ref, dy_ref, dg_ref, *rest):
        @pl.when(pl.program_id(0) == 0)
        def _():
            dg_ref[...] = jnp.zeros_like(dg_ref)
            for r_ in rest:
                r_[...] = jnp.zeros_like(r_)

        yv, dxo_v = y_ref[...], dx_ref[...]
        r = lax.rsqrt(jnp.mean(yv * yv, axis=-1, keepdims=True) + EPS)
        yh = yv * r
        dyh = dxo_v * g_ref[...]
        dy = r * (dyh - yh * jnp.mean(dyh * yh, axis=-1, keepdims=True))
        dy_ref[...] = dy.astype(BF16)
        dg_ref[...] += jnp.sum(dxo_v * yh, axis=0, keepdims=True)
        for r_ in rest:
            r_[...] += jnp.sum(dy, axis=0, keepdims=True)

    nacc = 2 if with_bias_grad else 1
    return pl.pallas_call(
        body, name=name, out_shape=[jax.ShapeDtypeStruct((s, d), BF16)] + [jax.ShapeDtypeStruct((1, d), F32)] * nacc,
        grid=(s // tm,), in_specs=[_rows(tm, d), _full((1, d)), _rows(tm, d)],
        out_specs=[_rows(tm, d)] + [_full((1, d))] * nacc, compiler_params=_params("arbitrary"),
    )(y, g, dxo)


def _matmul_nt(gmat, w, *, name, out_dtype, tm=512):
    s, k = gmat.shape
    n = w.shape[0]
    tm = min(tm, s)

    def body(g_ref, w_ref, o_ref):
        o_ref[...] = lax.dot_general(g_ref[...], w_ref[...], (((1,), (1,)), ((), ())),
                                     preferred_element_type=F32).astype(out_dtype)

    return pl.pallas_call(
        body, name=name, out_shape=jax.ShapeDtypeStruct((s, n), out_dtype), grid=(s // tm,),
        in_specs=[_rows(tm, k), _full((n, k))], out_specs=_rows(tm, n), compiler_params=_params("parallel"),
    )(gmat, w)


def _matmul_nt_prenorm_bwd(pieces, w, x, g, dres, *, name, tm=256):
    s, d = x.shape
    tm = min(tm, s)
    np_ = len(pieces)

    def body(*refs):
        p_refs, w_refs = refs[:np_], refs[np_:2 * np_]
        x_ref, g_ref, r_ref, dx_ref, dg_ref = refs[2 * np_:]

        @pl.when(pl.program_id(0) == 0)
        def _():
            dg_ref[...] = jnp.zeros_like(dg_ref)

        dh = None
        for p_ref, w_ref in zip(p_refs, w_refs):
            t = lax.dot_general(p_ref[...], w_ref[...], (((1,), (1,)), ((), ())), preferred_element_type=F32)
            dh = t if dh is None else dh + t
        xv = x_ref[...]
        r = lax.rsqrt(jnp.mean(xv * xv, axis=-1, keepdims=True) + EPS)
        xh = xv * r
        dyh = dh * g_ref[...]
        dx_ref[...] = r_ref[...] + r * (dyh - xh * jnp.mean(dyh * xh, axis=-1, keepdims=True))
        dg_ref[...] += jnp.sum(dh * xh, axis=0, keepdims=True)

    in_specs = [_rows(tm, kc) for _, _, kc in pieces]
    for _, c0, kc in pieces:
        assert c0 % kc == 0
        in_specs.append(pl.BlockSpec((d, kc), lambda i, _b=c0 // kc: (0, _b)))
    in_specs += [_rows(tm, d), _full((1, d)), _rows(tm, d)]
    return pl.pallas_call(
        body, name=name, out_shape=[jax.ShapeDtypeStruct((s, d), F32), jax.ShapeDtypeStruct((1, d), F32)],
        grid=(s // tm,), in_specs=in_specs, out_specs=[_rows(tm, d), _full((1, d))],
        compiler_params=_params("arbitrary"),
    )(*[p for p, _, _ in pieces], *[w] * np_, x, g, dres)


def _matmul_tn(a, gmat, *, name, tn, ts=512):
    s, ka = a.shape
    n = gmat.shape[1]
    ts = min(ts, s)
    tn = min(tn, n)

    def body(a_ref, g_ref, o_ref):
        @pl.when(pl.program_id(1) == 0)
        def _():
            o_ref[...] = jnp.zeros_like(o_ref)

        o_ref[...] += lax.dot_general(a_ref[...], g_ref[...], (((0,), (0,)), ((), ())), preferred_element_type=F32)

    return pl.pallas_call(
        body, name=name, out_shape=jax.ShapeDtypeStruct((ka, n), F32), grid=(n // tn, s // ts),
        in_specs=[pl.BlockSpec((ts, ka), lambda j, i: (i, 0)), pl.BlockSpec((ts, tn), lambda j, i: (i, j))],
        out_specs=pl.BlockSpec((ka, tn), lambda j, i: (0, j)), compiler_params=_params("parallel", "arbitrary"),
    )(a, gmat)


def _ffn_act_bwd(z, dact, w_dw, b_dw, tm=512):
    s, f2 = z.shape
    f = f2 // 2
    tm = min(tm, s)
    tc = _ffn_cols(f)
    nfc = f // tc

    def body(zu, zg, hu, hg, wu, wg, bu, bg, da_ref, du_ref, dgt_ref, dbu_ref, dbg_ref, dwu_ref, dwg_ref):
        i = pl.program_id(1)

        @pl.when(i == 0)
        def _():
            for r_ in (dbu_ref, dbg_ref, dwu_ref, dwg_ref):
                r_[...] = jnp.zeros_like(r_)

        def conv(z_ref, h_ref, w_ref, b_ref):
            zc = z_ref[...].astype(F32)
            z2, z1 = _conv3_taps(zc, h_ref[...].astype(F32), i == 0)
            return (z2, z1, zc), w_ref[0:1, :] * z2 + w_ref[1:2, :] * z1 + w_ref[2:3, :] * zc + b_ref[...]

        taps_u, up = conv(zu, hu, wu, bu)
        taps_g, gate = conv(zg, hg, wg, bg)
        da = da_ref[...].astype(F32)
        sg = _sigmoid(gate)
        d_up = da * (gate * sg)
        d_gate = da * up * (sg * (1.0 + gate * (1.0 - sg)))
        du_ref[...] = d_up.astype(BF16)
        dgt_ref[...] = d_gate.astype(BF16)
        for dv, taps, db_ref, dw_ref in ((d_up, taps_u, dbu_ref, dwu_ref), (d_gate, taps_g, dbg_ref, dwg_ref)):
            db_ref[...] += jnp.sum(dv, axis=0, keepdims=True)
            for k_, tap in enumerate(taps):
                dw_ref[k_:k_ + 1, :] += jnp.sum(dv * tap, axis=0, keepdims=True)

    hb = tm // 8
    tile = lambda off: pl.BlockSpec((tm, tc), lambda j, i: (i, off + j))
    halo = lambda off: pl.BlockSpec((8, tc), lambda j, i: (jnp.maximum(i * hb - 1, 0), off + j))
    prm = lambda rows, off: pl.BlockSpec((rows, tc), lambda j, i: (0, off + j))
    acc = lambda rows: pl.BlockSpec((rows, tc), lambda j, i: (0, j))
    return pl.pallas_call(
        body, name="ffn_act_bwd",
        out_shape=[jax.ShapeDtypeStruct((s, f), BF16)] * 2 + [jax.ShapeDtypeStruct((1, f), F32)] * 2
        + [jax.ShapeDtypeStruct((FFN_CONV, f), F32)] * 2,
        grid=(nfc, s // tm),
        in_specs=[tile(0), tile(nfc), halo(0), halo(nfc), prm(FFN_CONV, 0), prm(FFN_CONV, nfc), prm(1, 0), prm(1, nfc), tile(0)],
        out_specs=[tile(0), tile(0), acc(1), acc(1), acc(FFN_CONV), acc(FFN_CONV)],
        compiler_params=_params("parallel", "arbitrary"),
    )(z, z, z, z, w_dw, w_dw, b_dw, b_dw, dact)


def _conv3_transpose(dug, w_dw, col0, tm=512):
    s, f = dug.shape
    tm = min(tm, s)
    tc = _ffn_cols(f)
    nfc = f // tc
    nrow = s // tm
    off = col0 // tc

    def body(d_ref, n_ref, w_ref, o_ref):
        last = pl.program_id(0) == nrow - 1
        dv = d_ref[...].astype(F32)
        nxt = n_ref[...].astype(F32) * jnp.where(last, 0.0, 1.0)
        n0, n1 = nxt[0:1, :], nxt[1:2, :]
        row = lax.broadcasted_iota(jnp.int32, dv.shape, 0)
        d1 = jnp.where(row == tm - 1, n0, pltpu.roll(dv, tm - 1, 0))
        d2 = jnp.where(row == tm - 1, n1, jnp.where(row == tm - 2, n0, pltpu.roll(dv, tm - 2, 0)))
        o_ref[...] = (w_ref[2:3, :] * dv + w_ref[1:2, :] * d1 + w_ref[0:1, :] * d2).astype(BF16)

    hb = tm // 8
    return pl.pallas_call(
        body, name="conv3_transpose", out_shape=jax.ShapeDtypeStruct((s, f), BF16), grid=(nrow, nfc),
        in_specs=[pl.BlockSpec((tm, tc), lambda i, j: (i, j)),
                  pl.BlockSpec((8, tc), lambda i, j: (jnp.minimum((i + 1) * hb, s // 8 - 1), j)),
                  pl.BlockSpec((FFN_CONV, tc), lambda i, j: (0, off + j))],
        out_specs=pl.BlockSpec((tm, tc), lambda i, j: (i, j)), compiler_params=_params("parallel", "parallel"),
    )(dug, dug, w_dw)


def _conv_mid_bwd(ag, du3, w_dw, b_dw, ln_g, ln_b, tm=256):
    s, c2 = ag.shape
    c = c2 // 2
    tm = min(tm, s)

    def body(ag_ref, halo_ref, du_ref, w_ref, b_ref, g_ref, bb_ref, o_ref, dlg_ref, dlb_ref, db_ref, dw_ref, ext_ref):
        @pl.when(pl.program_id(0) == 0)
        def _():
            for r_ in (dlg_ref, dlb_ref, db_ref, dw_ref):
                r_[...] = jnp.zeros_like(r_)

        u1 = _glu_conv(ag_ref, halo_ref, w_ref, ext_ref, pl.program_id(0) == 0, c) + b_ref[...]
        xh, rstd = _layernorm_stats(u1)
        u2 = xh * g_ref[...] + bb_ref[...]
        sg = _sigmoid(u2)
        du2 = du_ref[...] * (sg * (1.0 + u2 * (1.0 - sg)))
        dlg_ref[...] += jnp.sum(du2 * xh, axis=0, keepdims=True)
        dlb_ref[...] += jnp.sum(du2, axis=0, keepdims=True)
        dxh = du2 * g_ref[...]
        du1 = rstd * (dxh - jnp.mean(dxh, axis=-1, keepdims=True) - xh * jnp.mean(dxh * xh, axis=-1, keepdims=True))
        o_ref[...] = du1.astype(BF16)
        db_ref[...] += jnp.sum(du1, axis=0, keepdims=True)
        base = CONV_HALO - (CONV_KERNEL - 1)
        for j in range(CONV_KERNEL):
            dw_ref[j:j + 1, :] += jnp.sum(du1 * ext_ref[base + j:base + j + tm, :], axis=0, keepdims=True)

    hb = tm // CONV_HALO
    vec = _full((1, c))
    return pl.pallas_call(
        body, name="conv_mid_bwd",
        out_shape=[jax.ShapeDtypeStruct((s, c), BF16)] + [jax.ShapeDtypeStruct((1, c), F32)] * 3
        + [jax.ShapeDtypeStruct((CONV_HALO, c), F32)],
        grid=(s // tm,),
        in_specs=[_rows(tm, c2), pl.BlockSpec((CONV_HALO, c2), lambda i: (jnp.maximum(i * hb - 1, 0), 0)), _rows(tm, c),
                  _full((CONV_KERNEL, c)), vec, vec, vec],
        out_specs=[_rows(tm, c), vec, vec, vec, _full((CONV_HALO, c))],
        scratch_shapes=[pltpu.VMEM((CONV_HALO + tm, c), F32)], compiler_params=_params("arbitrary"),
    )(ag, ag, du3, w_dw, b_dw, ln_g, ln_b)


def _glu_conv_bwd(du1, ag, w_dw, tm=256):
    s, c = du1.shape
    tm = min(tm, s)
    nrow = s // tm

    def body(d_ref, n_ref, ag_ref, w_ref, o_ref, db_ref, ext_ref):
        @pl.when(pl.program_id(0) == 0)
        def _():
            db_ref[...] = jnp.zeros_like(db_ref)

        ext_ref[0:tm, :] = d_ref[...].astype(F32)
        ext_ref[tm:, :] = n_ref[...].astype(F32) * jnp.where(pl.program_id(0) == nrow - 1, 0.0, 1.0)
        top = CONV_KERNEL - 1
        du0 = w_ref[0:1, :] * ext_ref[top:top + tm, :]
        for j in range(1, CONV_KERNEL):
            du0 = du0 + w_ref[j:j + 1, :] * ext_ref[top - j:top - j + tm, :]
        ag = ag_ref[...].astype(F32)
        a, gt = ag[:, :c], ag[:, c:]
        sg = _sigmoid(gt)
        da = du0 * sg
        dgt = du0 * a * (sg * (1.0 - sg))
        o_ref[:, :c] = da.astype(BF16)
        o_ref[:, c:] = dgt.astype(BF16)
        db_ref[:, :c] += jnp.sum(da, axis=0, keepdims=True)
        db_ref[:, c:] += jnp.sum(dgt, axis=0, keepdims=True)

    hb = tm // CONV_HALO
    return pl.pallas_call(
        body, name="glu_conv_bwd",
        out_shape=[jax.ShapeDtypeStruct((s, 2 * c), BF16), jax.ShapeDtypeStruct((1, 2 * c), F32)], grid=(nrow,),
        in_specs=[_rows(tm, c), pl.BlockSpec((CONV_HALO, c), lambda i: (jnp.minimum((i + 1) * hb, s // CONV_HALO - 1), 0)),
                  _rows(tm, 2 * c), _full((CONV_KERNEL, c))],
        out_specs=[_rows(tm, 2 * c), _full((1, 2 * c))],
        scratch_shapes=[pltpu.VMEM((tm + CONV_HALO, c), F32)], compiler_params=_params("arbitrary"),
    )(du1, du1, ag, w_dw)


def _head_rows(v, mask):
    return jnp.max(jnp.where(mask, v, -jnp.inf), axis=-1, keepdims=True)


def _attn_bwd_dq(qkv, dmix, mixed, lse, rope, grp, dil):
    s, ncol = qkv.shape
    nblk = ncol // GROUP_WIDTH
    l = s // dil
    nb = l // SPAN
    view = lambda t: t.reshape(l, dil * t.shape[1])

    def body(q_ref, kp_ref, kc_ref, vp_ref, vc_ref, do_ref, mx_ref, l_ref, c_ref, su_ref, sd_ref, o_ref):
        b = pl.program_id(1)
        row = lax.broadcasted_iota(jnp.int32, (SPAN, 2 * SPAN), 0)
        col = lax.broadcasted_iota(jnp.int32, (SPAN, 2 * SPAN), 1)
        no_prev = jnp.where(b > 0, 0, 4 * SPAN)
        valid = ((col < SPAN) & (col >= row + no_prev)) | ((col >= SPAN) & (col - SPAN <= row))
        masks, keep = _head_masks()
        for p in range(GROUP_WIDTH // 128):
            sl = slice(p * 128, (p + 1) * 128)
            qp, dop = q_ref[:, sl], do_ref[:, sl]
            kk = jnp.concatenate([kp_ref[:, sl], kc_ref[:, sl]], axis=0)
            vv = jnp.concatenate([vp_ref[:, sl], vc_ref[:, sl]], axis=0)
            prod = dop.astype(F32) * mx_ref[:, sl].astype(F32)
            lsep = l_ref[:, sl]
            dqs = []
            for h in range(2):
                qh, doh = qp * keep[h], dop * keep[h]
                sc = lax.dot_general(qh, kk, (((1,), (1,)), ((), ())), preferred_element_type=F32) * (HEAD_DIM ** -0.5)
                pe = jnp.where(valid, jnp.exp(sc - _head_rows(lsep, masks[h])), 0.0)
                dp = lax.dot_general(doh, vv, (((1,), (1,)), ((), ())), preferred_element_type=F32)
                dbar = jnp.sum(jnp.where(masks[h], prod, 0.0), axis=-1, keepdims=True)
                ds = pe * (dp - dbar) * (HEAD_DIM ** -0.5)
                dqs.append(jnp.dot(ds.astype(BF16), kk, preferred_element_type=F32))
            dq = jnp.where(masks[0], dqs[0], dqs[1])
            o_ref[:, sl] = _rope_transpose(dq, c_ref[...], su_ref[...], sd_ref[...]).astype(BF16)

    blk = (SPAN, GROUP_WIDTH)
    cur = lambda t: pl.BlockSpec(blk, lambda r, b: (b, r * nblk + 3 * t + grp))
    prev = lambda t: pl.BlockSpec(blk, lambda r, b: (jnp.maximum(b - 1, 0), r * nblk + 3 * t + grp))
    own = pl.BlockSpec(blk, lambda r, b: (b, r))
    tab = pl.BlockSpec((SPAN, 128), lambda r, b: (b, r))
    qv = view(qkv)
    out = pl.pallas_call(
        body, name=f"attn_bwd_dq_g{grp}", out_shape=jax.ShapeDtypeStruct((l, dil * GROUP_WIDTH), BF16), grid=(dil, nb),
        in_specs=[cur(0), prev(1), cur(1), prev(2), cur(2), own, own, own, tab, tab, tab], out_specs=own,
        compiler_params=_params("parallel", "arbitrary"),
    )(qv, qv, qv, qv, qv, view(dmix), view(mixed), view(lse), *[view(t) for t in rope])
    return out.reshape(s, GROUP_WIDTH)


def _attn_bwd_dkv(qkv, dmix, mixed, lse, rope, grp, dil):
    s, ncol = qkv.shape
    nblk = ncol // GROUP_WIDTH
    l = s // dil
    nb = l // SPAN
    view = lambda t: t.reshape(l, dil * t.shape[1])

    def body(k_ref, v_ref, qc_ref, qn_ref, doc_ref, don_ref, mc_ref, mn_ref, lc_ref, ln_ref,
             c_ref, su_ref, sd_ref, o_ref):
        b = pl.program_id(1)
        row = lax.broadcasted_iota(jnp.int32, (2 * SPAN, SPAN), 0)
        col = lax.broadcasted_iota(jnp.int32, (2 * SPAN, SPAN), 1)
        no_next = jnp.where(b < nb - 1, 0, 4 * SPAN)
        valid = ((row < SPAN) & (col <= row)) | ((row >= SPAN) & (col >= row - SPAN + no_next))
        masks, keep = _head_masks()
        masks2, _ = _head_masks(2 * SPAN)
        for p in range(GROUP_WIDTH // 128):
            sl = slice(p * 128, (p + 1) * 128)
            kp, vp = k_ref[:, sl], v_ref[:, sl]
            qq = jnp.concatenate([qc_ref[:, sl], qn_ref[:, sl]], axis=0)
            doo = jnp.concatenate([doc_ref[:, sl], don_ref[:, sl]], axis=0)
            mm = jnp.concatenate([mc_ref[:, sl], mn_ref[:, sl]], axis=0)
            ll = jnp.concatenate([lc_ref[:, sl], ln_ref[:, sl]], axis=0)
            prod = doo.astype(F32) * mm.astype(F32)
            dks, dvs = [], []
            for h in range(2):
                qh, doh = qq * keep[h], doo * keep[h]
                sc = lax.dot_general(qh, kp, (((1,), (1,)), ((), ())), preferred_element_type=F32) * (HEAD_DIM ** -0.5)
                pe = jnp.where(valid, jnp.exp(sc - _head_rows(ll, masks2[h])), 0.0)
                dp = lax.dot_general(doh, vp, (((1,), (1,)), ((), ())), preferred_element_type=F32)
                dbar = jnp.sum(jnp.where(masks2[h], prod, 0.0), axis=-1, keepdims=True)
                ds = pe * (dp - dbar) * (HEAD_DIM ** -0.5)
                dvs.append(lax.dot_general(pe.astype(BF16), doo, (((0,), (0,)), ((), ())), preferred_element_type=F32))
                dks.append(lax.dot_general(ds.astype(BF16), qq, (((0,), (0,)), ((), ())), preferred_element_type=F32))
            dk = jnp.where(masks[0], dks[0], dks[1])
            o_ref[:, sl] = _rope_transpose(dk, c_ref[...], su_ref[...], sd_ref[...]).astype(BF16)
            o_ref[:, GROUP_WIDTH + p * 128:GROUP_WIDTH + (p + 1) * 128] = jnp.where(masks[0], dvs[0], dvs[1]).astype(BF16)

    blk = (SPAN, GROUP_WIDTH)
    nxt_b = lambda b: jnp.minimum(b + 1, nb - 1)
    col_of = lambda t: pl.BlockSpec(blk, lambda r, b: (b, r * nblk + 3 * t + grp))
    q_next = pl.BlockSpec(blk, lambda r, b: (nxt_b(b), r * nblk + grp))
    own = pl.BlockSpec(blk, lambda r, b: (b, r))
    own_next = pl.BlockSpec(blk, lambda r, b: (nxt_b(b), r))
    tab = pl.BlockSpec((SPAN, 128), lambda r, b: (b, r))
    qv, dv_, mv, lv = view(qkv), view(dmix), view(mixed), view(lse)
    out = pl.pallas_call(
        body, name=f"attn_bwd_dkv_g{grp}", out_shape=jax.ShapeDtypeStruct((l, dil * 2 * GROUP_WIDTH), BF16), grid=(dil, nb),
        in_specs=[col_of(1), col_of(2), col_of(0), q_next, own, own_next, own, own_next, own, own_next, tab, tab, tab],
        out_specs=pl.BlockSpec((SPAN, 2 * GROUP_WIDTH), lambda r, b: (b, r)),
        compiler_params=_params("parallel", "arbitrary"),
    )(qv, qv, qv, qv, dv_, dv_, mv, mv, lv, lv, *[view(t) for t in rope])
    return out.reshape(s, 2 * GROUP_WIDTH)


def _rope_freq_row():
    half = ROT_DIM // 2
    inv = (ROPE_THETA ** (-np.arange(half, dtype=np.float32) / half)).astype(np.float32)
    row = np.zeros((1, 128), np.float32)
    for head in range(128 // HEAD_DIM):
        row[0, head * HEAD_DIM:head * HEAD_DIM + half] = inv
        row[0, head * HEAD_DIM + half:head * HEAD_DIM + ROT_DIM] = inv
    return jnp.asarray(row)


def _ffn_fwd(x, g_pre, g_post, w_up, w_dw, b_dw, w_down):
    h, z = _norm_matmul(x, g_pre, w_up, tn=512 if w_up.shape[1] % 512 == 0 else 256, name="ffn_up")
    act = _ffn_act(z, w_dw, b_dw)
    y, xo = _matmul_resnorm(act, w_down, x, g_post, name="ffn_down")
    return xo, (x, h, z, act, y)


def _ffn_bwd(saved, dxo, g_pre, g_post, w_up, w_dw, b_dw, w_down):
    x, h, z, act, y = saved
    f = act.shape[1]
    dy, dg_post = _postnorm_bwd(y, g_post, dxo, name="ffn_post_bwd")
    dact = _matmul_nt(dy, w_down, name="ffn_dact", out_dtype=BF16)
    d_wdown = _matmul_tn(act, dy, name="ffn_dw_down", tn=512)
    dug_u, dug_g, db_u, db_g, dwd_u, dwd_g = _ffn_act_bwd(z, dact, w_dw, b_dw)
    dz_u = _conv3_transpose(dug_u, w_dw, 0)
    dz_g = _conv3_transpose(dug_g, w_dw, f)
    dx, dg_pre = _matmul_nt_prenorm_bwd([(dz_u, 0, f), (dz_g, f, f)], w_up, x, g_pre, dxo, name="ffn_dx")
    tn = 1408 if f % 1408 == 0 else 128
    d_wup = jnp.concatenate([_matmul_tn(h, dz_u, name="ffn_dw_up", tn=tn), _matmul_tn(h, dz_g, name="ffn_dw_up", tn=tn)], axis=1)
    grads = dict(w_up=d_wup, w_down=d_wdown, w_dw=jnp.concatenate([dwd_u, dwd_g], axis=1),
                 b_dw=jnp.concatenate([db_u, db_g], axis=1), g_pre=dg_pre, g_post=dg_post)
    return dx, grads


def _local_step(x, pos_col, target, p):
    ng = p["norm_g"]
    row = lambda r: ng[r:r + 1]
    rope = _rope_tables(pos_col, _rope_freq_row())

    h0, qkv = _norm_matmul(x, row(0), p["w_qkv"], tn=GROUP_WIDTH, name="attn_qkv", rope=rope, rope_blocks=6)
    os_, ls_ = zip(*[_attn_fwd(qkv, g_, d_) for g_, d_ in enumerate(DILATIONS)])
    y_a, x1, mixed, lse = _mix_wo(os_, ls_, p["w_o"], x, row(1))
    x2, ffn0 = _ffn_fwd(x1, row(2), row(3), p["w_up"][0], p["ffn_w_dw"][0], p["ffn_b_dw"][0], p["w_down"][0])
    h1, ag = _norm_matmul(x2, row(4), p["w_pw1"], tn=512, name="conv_pw1", bias=p["b_pw1"])
    u3 = _conv_mid(ag, p["conv_w_dw"], p["conv_b_dw"], p["ln_g"], p["ln_b"])
    y_c, x3 = _matmul_resnorm(u3, p["w_pw2"], x2, row(5), name="conv_pw2", bias=p["b_pw2"])
    x4, ffn1 = _ffn_fwd(x3, row(6), row(7), p["w_up"][1], p["ffn_w_dw"][1], p["ffn_b_dw"][1], p["w_down"][1])
    dx4, loss = _loss_grad(x4, target)

    dx3, gf1 = _ffn_bwd(ffn1, dx4, row(6), row(7), p["w_up"][1], p["ffn_w_dw"][1], p["ffn_b_dw"][1], p["w_down"][1])
    dy_c, dg5, db_pw2 = _postnorm_bwd(y_c, row(5), dx3, name="conv_post_bwd", with_bias_grad=True)
    du3 = _matmul_nt(dy_c, p["w_pw2"], name="conv_du3", out_dtype=F32)
    d_wpw2 = _matmul_tn(u3, dy_c, name="conv_dw_pw2", tn=512)
    du1, d_lng, d_lnb, d_cbdw, d_cwdw = _conv_mid_bwd(ag, du3, p["conv_w_dw"], p["conv_b_dw"], p["ln_g"], p["ln_b"])
    dag, db_pw1 = _glu_conv_bwd(du1, ag, p["conv_w_dw"])
    dx2, dg4 = _matmul_nt_prenorm_bwd([(dag, 0, dag.shape[1])], p["w_pw1"], x2, row(4), dx3, name="conv_dx")
    d_wpw1 = _matmul_tn(h1, dag, name="conv_dw_pw1", tn=512)
    dx1, gf0 = _ffn_bwd(ffn0, dx2, row(2), row(3), p["w_up"][0], p["ffn_w_dw"][0], p["ffn_b_dw"][0], p["w_down"][0])
    dy_a, dg1 = _postnorm_bwd(y_a, row(1), dx1, name="attn_post_bwd")
    dmix = _matmul_nt(dy_a, p["w_o"], name="attn_dmix", out_dtype=BF16)
    d_wo = _matmul_tn(mixed, dy_a, name="attn_dw_o", tn=512)
    pieces, d_wqkv = [], [None] * 9
    for g_, d_ in enumerate(DILATIONS):
        dq = _attn_bwd_dq(qkv, dmix, mixed, lse, rope, g_, d_)
        dkv = _attn_bwd_dkv(qkv, dmix, mixed, lse, rope, g_, d_)
        dk, dv = dkv[:, :GROUP_WIDTH], dkv[:, GROUP_WIDTH:]
        for t, arr in enumerate((dq, dk, dv)):
            pieces.append((arr, (3 * t + g_) * GROUP_WIDTH, GROUP_WIDTH))
            d_wqkv[3 * t + g_] = _matmul_tn(h0, arr, name="attn_dw_qkv", tn=GROUP_WIDTH)
    grad_x, dg0 = _matmul_nt_prenorm_bwd(pieces, p["w_qkv"], x, row(0), dx1, name="attn_dx")

    grads = dict(
        norm_g=jnp.concatenate([dg0, dg1, gf0["g_pre"], gf0["g_post"], dg4, dg5, gf1["g_pre"], gf1["g_post"]], axis=0),
        w_qkv=jnp.concatenate(d_wqkv, axis=1), w_o=d_wo, w_pw1=d_wpw1, b_pw1=db_pw1,
        conv_w_dw=d_cwdw[:CONV_KERNEL], conv_b_dw=d_cbdw, ln_g=d_lng, ln_b=d_lnb, w_pw2=d_wpw2, b_pw2=db_pw2,
        w_up=jnp.stack([gf0["w_up"], gf1["w_up"]]), ffn_w_dw=jnp.stack([gf0["w_dw"], gf1["w_dw"]]),
        ffn_b_dw=jnp.concatenate([gf0["b_dw"], gf1["b_dw"]], axis=0), w_down=jnp.stack([gf0["w_down"], gf1["w_down"]]))
    return loss, grad_x, grads


SHARD_AXIS = dict(norm_g=2, attn_w_qkv=2, attn_w_o=2, conv_w_pw1=2, conv_b_pw1=1, conv_w_dw=2, conv_b_dw=1,
                  conv_ln_g=1, conv_ln_b=1, conv_w_pw2=1, conv_b_pw2=1, ffn_w_up=2, ffn_w_dw=2, ffn_w_down=1)
MATMUL_WEIGHTS = ("attn_w_qkv", "attn_w_o", "conv_w_pw1", "conv_w_pw2", "ffn_w_up", "ffn_w_down")
SMALL_WEIGHTS = ("norm_g", "conv_b_pw1", "conv_w_dw", "conv_b_dw", "conv_ln_g", "conv_ln_b", "conv_b_pw2", "ffn_w_dw")
SHARDED = tuple(SHARD_AXIS)


def _pack(arrays, cols, row_multiple):
    flat = jnp.concatenate([a.reshape(-1) for a in arrays])
    rows = -(-flat.shape[0] // cols)
    rows = -(-rows // row_multiple) * row_multiple
    return jnp.pad(flat, (0, rows * cols - flat.shape[0])).reshape(rows, cols)


def _unpack(packed, shapes):
    flat = packed.reshape(packed.shape[:-2] + (-1,))
    out, off = [], 0
    for shp in shapes:
        n = math.prod(shp)
        out.append(flat[..., off:off + n].reshape(packed.shape[:-2] + tuple(shp)))
        off += n
    return out


def _join_shards(stacked, axis):
    moved = jnp.moveaxis(stacked, 0, axis)
    shp = moved.shape
    return moved.reshape(shp[:axis] + (shp[axis] * shp[axis + 1],) + shp[axis + 2:])


def _split_shards(whole, axis):
    shp = whole.shape
    cut = whole.reshape(shp[:axis] + (N_DEV, shp[axis] // N_DEV) + shp[axis + 1:])
    return jnp.moveaxis(cut, axis, 0)


def kernel(x, positions, norm_g, attn_w_qkv, attn_w_o, conv_w_pw1, conv_b_pw1, conv_w_dw, conv_b_dw, conv_ln_g, conv_ln_b, conv_w_pw2, conv_b_pw2, ffn_w_up, ffn_w_dw, ffn_b_dw, ffn_w_down, loss_target, m_norm_g, m_attn_w_qkv, m_attn_w_o, m_conv_w_pw1, m_conv_b_pw1, m_conv_w_dw, m_conv_b_dw, m_conv_ln_g, m_conv_ln_b, m_conv_w_pw2, m_conv_b_pw2, m_ffn_w_up, m_ffn_w_dw, m_ffn_b_dw, m_ffn_w_down, v_norm_g, v_attn_w_qkv, v_attn_w_o, v_conv_w_pw1, v_conv_b_pw1, v_conv_w_dw, v_conv_b_dw, v_conv_ln_g, v_conv_ln_b, v_conv_w_pw2, v_conv_b_pw2, v_ffn_w_up, v_ffn_w_dw, v_ffn_b_dw, v_ffn_w_down):
    w = dict(norm_g=norm_g, attn_w_qkv=attn_w_qkv, attn_w_o=attn_w_o, conv_w_pw1=conv_w_pw1, conv_b_pw1=conv_b_pw1,
             conv_w_dw=conv_w_dw, conv_b_dw=conv_b_dw, conv_ln_g=conv_ln_g, conv_ln_b=conv_ln_b, conv_w_pw2=conv_w_pw2,
             conv_b_pw2=conv_b_pw2, ffn_w_up=ffn_w_up, ffn_w_dw=ffn_w_dw, ffn_w_down=ffn_w_down)
    m = dict(norm_g=m_norm_g, attn_w_qkv=m_attn_w_qkv, attn_w_o=m_attn_w_o, conv_w_pw1=m_conv_w_pw1, conv_b_pw1=m_conv_b_pw1,
             conv_w_dw=m_conv_w_dw, conv_b_dw=m_conv_b_dw, conv_ln_g=m_conv_ln_g, conv_ln_b=m_conv_ln_b, conv_w_pw2=m_conv_w_pw2,
             conv_b_pw2=m_conv_b_pw2, ffn_w_up=m_ffn_w_up, ffn_w_dw=m_ffn_w_dw, ffn_w_down=m_ffn_w_down)
    v = dict(norm_g=v_norm_g, attn_w_qkv=v_attn_w_qkv, attn_w_o=v_attn_w_o, conv_w_pw1=v_conv_w_pw1, conv_b_pw1=v_conv_b_pw1,
             conv_w_dw=v_conv_w_dw, conv_b_dw=v_conv_b_dw, conv_ln_g=v_conv_ln_g, conv_ln_b=v_conv_ln_b, conv_w_pw2=v_conv_w_pw2,
             conv_b_pw2=v_conv_b_pw2, ffn_w_up=v_ffn_w_up, ffn_w_dw=v_ffn_w_dw, ffn_w_down=v_ffn_w_down)

    big = _all_gather(_pack([w[n].astype(BF16) for n in MATMUL_WEIGHTS], PACK_COLS, 16), "gather_matmul_weights")
    small = _all_gather(_pack([w[n] for n in SMALL_WEIGHTS], 128, 8), "gather_small_weights")
    whole = {}
    for names, packed in ((MATMUL_WEIGHTS, big), (SMALL_WEIGHTS, small)):
        for n, stacked in zip(names, _unpack(packed, [w[n].shape for n in names])):
            whole[n] = _join_shards(stacked, SHARD_AXIS[n])
    p = dict(norm_g=whole["norm_g"].reshape(-1, whole["norm_g"].shape[-1]), w_qkv=whole["attn_w_qkv"][0],
             w_o=whole["attn_w_o"][0], w_pw1=whole["conv_w_pw1"][0], b_pw1=whole["conv_b_pw1"],
             conv_w_dw=whole["conv_w_dw"][0], conv_b_dw=whole["conv_b_dw"], ln_g=whole["conv_ln_g"], ln_b=whole["conv_ln_b"],
             w_pw2=whole["conv_w_pw2"][0], b_pw2=whole["conv_b_pw2"], w_up=whole["ffn_w_up"], ffn_w_dw=whole["ffn_w_dw"],
             ffn_b_dw=[ffn_b_dw[0:1], ffn_b_dw[1:2]], w_down=whole["ffn_w_down"])

    loss, grad_x, g = _local_step(x[0], positions.reshape(-1, 1), loss_target[0], p)
    loss = lax.psum(loss[0, 0], ("x", "y", "c"))
    gw = dict(norm_g=g["norm_g"].reshape(norm_g.shape[0], 4, -1), attn_w_qkv=g["w_qkv"][None], attn_w_o=g["w_o"][None],
              conv_w_pw1=g["w_pw1"][None], conv_b_pw1=g["b_pw1"], conv_w_dw=g["conv_w_dw"][None], conv_b_dw=g["conv_b_dw"],
              conv_ln_g=g["ln_g"], conv_ln_b=g["ln_b"], conv_w_pw2=g["w_pw2"][None], conv_b_pw2=g["b_pw2"],
              ffn_w_up=g["w_up"], ffn_w_dw=g["ffn_w_dw"], ffn_w_down=g["w_down"])

    shapes = [w[n].shape for n in SHARDED]
    per_dev = jnp.concatenate([_split_shards(gw[n], SHARD_AXIS[n]).reshape(N_DEV, -1) for n in SHARDED], axis=1)
    width = per_dev.shape[1]
    rows = -(-width // (PACK_COLS * 128)) * 128
    per_dev = jnp.pad(per_dev, ((0, 0), (0, rows * PACK_COLS - width))).reshape(N_DEV, rows, PACK_COLS)
    core = lax.axis_index("c").astype(jnp.int32).reshape(1)
    chip_sums = _rs_chips(_rs_pair_add(per_dev, _rs_sibling(per_dev), core))
    packed = [_pack([t[n] for n in SHARDED], PACK_COLS, 128) for t in (w, m, v)]
    outs = _sum_adamw(chip_sums, *packed, name="sum_adamw")
    grad_s, delta_s, newm_s, newv_s = [dict(zip(SHARDED, _unpack(o, shapes))) for o in outs]

    bshape = ffn_b_dw.shape
    bparts = _all_gather(_pack([g["ffn_b_dw"]], 128, 8), "gather_bias_grads")
    bouts = _sum_adamw(bparts, *[_pack([t], 128, 8) for t in (ffn_b_dw, m_ffn_b_dw, v_ffn_b_dw)], name="sum_adamw_bias")
    for d_, o in zip((grad_s, delta_s, newm_s, newv_s), bouts):
        d_["ffn_b_dw"] = _unpack(o, [bshape])[0]

    order = ("norm_g", "attn_w_qkv", "attn_w_o", "conv_w_pw1", "conv_b_pw1", "conv_w_dw", "conv_b_dw", "conv_ln_g",
             "conv_ln_b", "conv_w_pw2", "conv_b_pw2", "ffn_w_up", "ffn_w_dw", "ffn_b_dw", "ffn_w_down")
    return (loss, grad_x[None], *[grad_s[n] for n in order], *[delta_s[n] for n in order],
            *[newm_s[n] for n in order], *[newv_s[n] for n in order])
```

```python
import functools
import math

import numpy as np
import jax
import jax.numpy as jnp
from jax import lax
from jax.experimental import pallas as pl
from jax.experimental.pallas import tpu as pltpu

F32 = jnp.float32
BF16 = jnp.bfloat16
EPS = 1e-6
N_DEV = 8
HEAD_DIM = 64
GROUP_WIDTH = 512
DILATIONS = (1, 4, 16)
SPAN = 128
ROT_DIM = 16
ROPE_THETA = 500000.0
CONV_KERNEL = 31
CONV_HALO = 32
FFN_CONV = 3
ADAM_LR, ADAM_B1, ADAM_B2, ADAM_EPS, ADAM_WD, ADAM_STEP = 0.001, 0.9, 0.999, 1e-08, 0.01, 10
VMEM_LIMIT_BYTES = 56 * 1024 * 1024
MESH = pl.DeviceIdType.MESH
ANY = pl.BlockSpec(memory_space=pl.ANY)
NT = (((1,), (1,)), ((), ()))
TN = (((0,), (0,)), ((), ()))


def _params(*sem):
    return pltpu.CompilerParams(dimension_semantics=sem, vmem_limit_bytes=VMEM_LIMIT_BYTES)


def _sigmoid(v):
    return 1.0 / (1.0 + jnp.exp(-v))


def _full(shape):
    return pl.BlockSpec(shape, lambda *_: (0,) * len(shape))


def _rows(tm, width):
    return pl.BlockSpec((tm, width), lambda i, *_: (i, 0))


def _tile(n, *multiples_of):
    for t in (1408, 1024, 512, 384, 256, 128):
        if n % t == 0 and all(o % t == 0 for o in multiples_of):
            return t
    raise ValueError((n, multiples_of))


def _all_gather(shard, name):
    r, c_ = shard.shape

    def body(x_ref, out_ref, send_sems, recv_sems, local_sem):
        x, y, c = lax.axis_index("x"), lax.axis_index("y"), lax.axis_index("c")
        me, sibling = (x, y, c), (x, y, 1 - c)
        chips = [(1 - x, y), (x, 1 - y), (1 - x, 1 - y)]

        def rows(px, py, pc):
            return out_ref.at[4 * px + 2 * py + pc]

        def copy(k, block, to, src=None):
            return pltpu.make_async_remote_copy(
                src_ref=rows(*block) if src is None else src, dst_ref=rows(*block),
                send_sem=send_sems.at[k], recv_sem=recv_sems.at[k], device_id=to, device_id_type=MESH)

        mine = pltpu.make_async_copy(x_ref, rows(*me), local_sem)
        mine.start()
        first = [copy(0, me, sibling, src=x_ref)]
        first += [copy(1 + j, me, (*chip, c), src=x_ref) for j, chip in enumerate(chips)]
        for cp in first:
            cp.start()
        passed = [copy(4 + j, (*chip, c), sibling) for j, chip in enumerate(chips)]
        for j, chip in enumerate(chips):
            copy(1 + j, (*chip, c), me).wait_recv()
            passed[j].start()
        copy(0, sibling, me).wait_recv()
        for j, chip in enumerate(chips):
            copy(4 + j, (*chip, 1 - c), me).wait_recv()
        for cp in first + passed:
            cp.wait_send()
        mine.wait()

    return pl.pallas_call(
        body, name=name, out_shape=jax.ShapeDtypeStruct((N_DEV, r, c_), shard.dtype),
        in_specs=[ANY], out_specs=ANY,
        scratch_shapes=[pltpu.SemaphoreType.DMA((7,)), pltpu.SemaphoreType.DMA((7,)), pltpu.SemaphoreType.DMA],
    )(shard)


def _with_rows(g, n):
    return jax.ShapeDtypeStruct((g.shape[0], n) + tuple(g.shape[2:]), g.dtype)


def _rs_sibling(gs):
    n = len(gs)

    def body(*refs):
        g_refs, o_refs, (send_sems, recv_sems) = refs[:n], refs[n:2 * n], refs[2 * n:]
        x, y, c = lax.axis_index("x"), lax.axis_index("y"), lax.axis_index("c")
        copies = [pltpu.make_async_remote_copy(
            src_ref=g_refs[w].at[:, 2 * q + (1 - c)], dst_ref=o_refs[w].at[:, q], send_sem=send_sems.at[4 * w + q],
            recv_sem=recv_sems.at[4 * w + q], device_id=(x, y, 1 - c), device_id_type=MESH)
            for w in range(n) for q in range(4)]
        for cp in copies:
            cp.start()
        for cp in copies:
            cp.wait_recv()
        for cp in copies:
            cp.wait_send()

    return pl.pallas_call(
        body, name="rs_sibling", out_shape=[_with_rows(g, 4) for g in gs],
        in_specs=[ANY] * n, out_specs=[ANY] * n,
        scratch_shapes=[pltpu.SemaphoreType.DMA((4 * n,)), pltpu.SemaphoreType.DMA((4 * n,))],
    )(*gs)


def _rs_pair_add(g, got, core):
    l, _, r, c_ = g.shape

    def body(core_ref, g_ref, got_ref, o_ref):
        o_ref[...] = g_ref[...] + got_ref[...]

    blk = (None, None, r, c_)
    return pl.pallas_call(
        body, name="rs_pair_add", out_shape=_with_rows(g, 4),
        grid_spec=pltpu.PrefetchScalarGridSpec(
            num_scalar_prefetch=1, grid=(l, 4),
            in_specs=[pl.BlockSpec(blk, lambda i, q, core_ref: (i, 2 * q + core_ref[0], 0, 0)),
                      pl.BlockSpec(blk, lambda i, q, core_ref: (i, q, 0, 0))],
            out_specs=pl.BlockSpec(blk, lambda i, q, core_ref: (i, q, 0, 0))),
        compiler_params=_params("parallel", "parallel"),
    )(core, g, got)


def _rs_chips(parts):
    n = len(parts)

    def body(*refs):
        p_refs, o_refs, (send_sems, recv_sems, local_sems) = refs[:n], refs[n:2 * n], refs[2 * n:]
        x, y, c = lax.axis_index("x"), lax.axis_index("y"), lax.axis_index("c")
        my_chip = 2 * x + y
        chips = [(1 - x, y), (x, 1 - y), (1 - x, 1 - y)]
        local = [pltpu.make_async_copy(p_refs[w].at[:, my_chip], o_refs[w].at[:, my_chip], local_sems.at[w]) for w in range(n)]
        for cp in local:
            cp.start()
        copies = [pltpu.make_async_remote_copy(
            src_ref=p_refs[w].at[:, 2 * qx + qy], dst_ref=o_refs[w].at[:, my_chip], send_sem=send_sems.at[3 * w + k],
            recv_sem=recv_sems.at[3 * w + k], device_id=(qx, qy, c), device_id_type=MESH)
            for w in range(n) for k, (qx, qy) in enumerate(chips)]
        for cp in copies:
            cp.start()
        for cp in copies:
            cp.wait_recv()
        for cp in copies:
            cp.wait_send()
        for cp in local:
            cp.wait()

    return pl.pallas_call(
        body, name="rs_chips", out_shape=[jax.ShapeDtypeStruct(p.shape, p.dtype) for p in parts],
        in_specs=[ANY] * n, out_specs=[ANY] * n,
        scratch_shapes=[pltpu.SemaphoreType.DMA((3 * n,)), pltpu.SemaphoreType.DMA((3 * n,)), pltpu.SemaphoreType.DMA((n,))],
    )(*parts)


def _sum_parts(parts, name):
    l, n, r, c_ = parts.shape

    def body(p_ref, o_ref):
        g = p_ref[0]
        for s in range(1, n):
            g = g + p_ref[s]
        o_ref[...] = g

    return pl.pallas_call(
        body, name=name, out_shape=jax.ShapeDtypeStruct((l, r, c_), F32), grid=(l,),
        in_specs=[pl.BlockSpec((None, n, r, c_), lambda i: (i, 0, 0, 0))],
        out_specs=pl.BlockSpec((None, r, c_), lambda i: (i, 0, 0)), compiler_params=_params("parallel"),
    )(parts)


def _adamw_math(w, g, m, v):
    m = ADAM_B1 * m + (1.0 - ADAM_B1) * g
    v = ADAM_B2 * v + (1.0 - ADAM_B2) * (g * g)
    m_hat = m / (1.0 - ADAM_B1 ** ADAM_STEP)
    v_hat = v / (1.0 - ADAM_B2 ** ADAM_STEP)
    delta = -ADAM_LR * (m_hat / (jnp.sqrt(v_hat) + ADAM_EPS) + ADAM_WD * w)
    return delta, m, v


def _adamw(g, w, m, v, name):
    l, k, n = w.shape
    tk = 256 if k % 256 == 0 else k

    def body(g_ref, w_ref, m_ref, v_ref, d_ref, nm_ref, nv_ref):
        d_ref[...], nm_ref[...], nv_ref[...] = _adamw_math(w_ref[...], g_ref[...], m_ref[...], v_ref[...])

    spec = pl.BlockSpec((None, tk, n), lambda i, j: (i, j, 0))
    return pl.pallas_call(
        body, name=name, out_shape=[jax.ShapeDtypeStruct((l, k, n), F32)] * 3, grid=(l, k // tk),
        in_specs=[spec] * 4, out_specs=[spec] * 3, compiler_params=_params("parallel", "parallel"),
    )(g, w, m, v)


def _sum_adamw(parts, w, m, v, name):
    n, r, c_ = parts.shape

    def body(p_ref, w_ref, m_ref, v_ref, g_ref, d_ref, nm_ref, nv_ref):
        g = p_ref[0]
        for s in range(1, n):
            g = g + p_ref[s]
        g_ref[...] = g
        d_ref[...], nm_ref[...], nv_ref[...] = _adamw_math(w_ref[...], g, m_ref[...], v_ref[...])

    return pl.pallas_call(
        body, name=name, out_shape=[jax.ShapeDtypeStruct((r, c_), F32)] * 4, grid=(1,),
        in_specs=[_full((n, r, c_))] + [_full((r, c_))] * 3, out_specs=[_full((r, c_))] * 4,
        compiler_params=_params("arbitrary"),
    )(parts, w, m, v)


def _rope_tables(pos_col, freq_row):
    s = pos_col.shape[0]
    tm = min(1024, s)

    def body(p_ref, f_ref, c_ref, su_ref, sd_ref):
        ang = p_ref[...].astype(F32) * f_ref[...]
        lane = lax.broadcasted_iota(jnp.int32, ang.shape, 1) & (HEAD_DIM - 1)
        cs, sn = jnp.cos(ang), jnp.sin(ang)
        c_ref[...] = jnp.where(lane < ROT_DIM, cs, 1.0)
        su_ref[...] = jnp.where((lane >= ROT_DIM // 2) & (lane < ROT_DIM), sn, 0.0)
        sd_ref[...] = jnp.where(lane < ROT_DIM // 2, -sn, 0.0)

    return pl.pallas_call(
        body, name="rope_tables", out_shape=[jax.ShapeDtypeStruct((s, 128), F32)] * 3, grid=(s // tm,),
        in_specs=[pl.BlockSpec((tm, 1), lambda i: (i, 0)), _full((1, 128))],
        out_specs=[_rows(tm, 128)] * 3, compiler_params=_params("parallel"),
    )(pos_col, freq_row)


def _rope_apply(t, cos, sin_up, sin_dn):
    w = t.shape[1]
    return t * cos + pltpu.roll(t, 8, 1) * sin_up + pltpu.roll(t, w - 8, 1) * sin_dn


def _rope_transpose(dr, cos, sin_up, sin_dn):
    w = dr.shape[1]
    return dr * cos + pltpu.roll(dr * sin_up, w - 8, 1) + pltpu.roll(dr * sin_dn, 8, 1)


def _norm_matmul(x, g, wt, *, tn, name, bias=None, rope=None, rope_blocks=0, tm=512):
    s, d = x.shape
    n = wt.shape[0]
    tm = min(tm, s)

    def body(*refs):
        x_ref, g_ref, w_ref = refs[:3]
        k = 3
        b_ref = None
        if bias is not None:
            b_ref = refs[k]
            k += 1
        if rope is not None:
            c_ref, su_ref, sd_ref = refs[k:k + 3]
            k += 3
        h_ref, o_ref = refs[k:k + 2]
        j = pl.program_id(1)

        @pl.when(j == 0)
        def _():
            xv = x_ref[...]
            r = lax.rsqrt(jnp.mean(xv * xv, axis=-1, keepdims=True) + EPS)
            h_ref[...] = (xv * r * g_ref[...]).astype(BF16)

        acc = lax.dot_general(h_ref[...], w_ref[...], NT, preferred_element_type=F32)
        if b_ref is not None:
            acc = acc + b_ref[...]
        if rope is None:
            o_ref[...] = acc.astype(BF16)
        else:
            @pl.when(j < rope_blocks)
            def _():
                reps = tn // 128
                o_ref[...] = _rope_apply(acc, jnp.tile(c_ref[...], (1, reps)), jnp.tile(su_ref[...], (1, reps)),
                                         jnp.tile(sd_ref[...], (1, reps))).astype(BF16)

            @pl.when(j >= rope_blocks)
            def _():
                o_ref[...] = acc.astype(BF16)

    in_specs = [_rows(tm, d), _full((1, d)), pl.BlockSpec((tn, d), lambda i, j: (j, 0))]
    args = [x, g, wt]
    if bias is not None:
        in_specs.append(pl.BlockSpec((1, tn), lambda i, j: (0, j)))
        args.append(bias)
    if rope is not None:
        in_specs += [_rows(tm, 128)] * 3
        args += list(rope)
    return pl.pallas_call(
        body, name=name,
        out_shape=[jax.ShapeDtypeStruct((s, d), BF16), jax.ShapeDtypeStruct((s, n), BF16)],
        grid=(s // tm, n // tn), in_specs=in_specs,
        out_specs=[_rows(tm, d), pl.BlockSpec((tm, tn), lambda i, j: (i, j))],
        compiler_params=_params("parallel", "arbitrary"),
    )(*args)


def _head_masks(rows=SPAN):
    lane = lax.broadcasted_iota(jnp.int32, (rows, 128), 1)
    masks = [lane < HEAD_DIM, lane >= HEAD_DIM]
    lane1 = lax.broadcasted_iota(jnp.int32, (1, 128), 1)
    keep = [jnp.where(lane1 < HEAD_DIM, 1.0, 0.0).astype(BF16), jnp.where(lane1 >= HEAD_DIM, 1.0, 0.0).astype(BF16)]
    return masks, keep


def _attn_fwd(qkv, grp, dil):
    s, ncol = qkv.shape
    nblk = ncol // GROUP_WIDTH
    l = s // dil
    nb = l // SPAN
    qv = qkv.reshape(l, dil * ncol)

    def body(q_ref, kp_ref, kc_ref, vp_ref, vc_ref, o_ref, l_ref):
        b = pl.program_id(1)
        row = lax.broadcasted_iota(jnp.int32, (SPAN, 2 * SPAN), 0)
        col = lax.broadcasted_iota(jnp.int32, (SPAN, 2 * SPAN), 1)
        no_prev = jnp.where(b > 0, 0, 4 * SPAN)
        valid = ((col < SPAN) & (col >= row + no_prev)) | ((col >= SPAN) & (col - SPAN <= row))
        masks, keep = _head_masks()
        for p in range(GROUP_WIDTH // 128):
            sl = slice(p * 128, (p + 1) * 128)
            qp = q_ref[:, sl]
            kk = jnp.concatenate([kp_ref[:, sl], kc_ref[:, sl]], axis=0)
            vv = jnp.concatenate([vp_ref[:, sl], vc_ref[:, sl]], axis=0)
            outs, lses = [], []
            for h in range(2):
                sc = lax.dot_general(qp * keep[h], kk, NT, preferred_element_type=F32) * (HEAD_DIM ** -0.5)
                sc = jnp.where(valid, sc, -1e30)
                mx = jnp.max(sc, axis=-1, keepdims=True)
                pe = jnp.exp(sc - mx)
                den = jnp.sum(pe, axis=-1, keepdims=True)
                pv = jnp.dot(pe.astype(BF16), vv, preferred_element_type=F32)
                outs.append(pv / den)
                lses.append(jnp.broadcast_to(mx + jnp.log(den), (SPAN, 128)))
            o_ref[:, sl] = jnp.where(masks[0], outs[0], outs[1])
            l_ref[:, sl] = jnp.where(masks[0], lses[0], lses[1])

    blk = (SPAN, GROUP_WIDTH)
    cur = lambda t: pl.BlockSpec(blk, lambda r, b: (b, r * nblk + 3 * t + grp))
    prev = lambda t: pl.BlockSpec(blk, lambda r, b: (jnp.maximum(b - 1, 0), r * nblk + 3 * t + grp))
    out = pl.BlockSpec(blk, lambda r, b: (b, r))
    o, lse = pl.pallas_call(
        body, name=f"attn_fwd_g{grp}", out_shape=[jax.ShapeDtypeStruct((l, dil * GROUP_WIDTH), F32)] * 2,
        grid=(dil, nb), in_specs=[cur(0), prev(1), cur(1), prev(2), cur(2)], out_specs=[out, out],
        compiler_params=_params("parallel", "arbitrary"),
    )(qv, qv, qv, qv, qv)
    return o.reshape(s, GROUP_WIDTH), lse.reshape(s, GROUP_WIDTH)


def _resnorm_store(y, x_ref, g_ref, y_ref, xo_ref):
    r = lax.rsqrt(jnp.mean(y * y, axis=-1, keepdims=True) + EPS)
    y_ref[...] = y
    xo_ref[...] = x_ref[...] + y * r * g_ref[...]


def _mix_wo(os_, ls_, wot, x, g, tm=256):
    s, d = x.shape
    gw = wot.shape[1]
    tm = min(tm, s)

    def body(o0, o1, o2, l0, l1, l2, w_ref, x_ref, g_ref, y_ref, xo_ref, mixed_ref, lse_ref):
        a0, a1, a2 = l0[...], l1[...], l2[...]
        mx = jnp.maximum(jnp.maximum(a0, a1), a2)
        e0, e1, e2 = jnp.exp(a0 - mx), jnp.exp(a1 - mx), jnp.exp(a2 - mx)
        den = e0 + e1 + e2
        mixed = (e0 / den) * o0[...] + (e1 / den) * o1[...] + (e2 / den) * o2[...]
        mixed_ref[...] = mixed.astype(BF16)
        lse_ref[...] = mx + jnp.log(den)
        y = lax.dot_general(mixed.astype(BF16), w_ref[...], NT, preferred_element_type=F32)
        _resnorm_store(y, x_ref, g_ref, y_ref, xo_ref)

    return pl.pallas_call(
        body, name="mix_wo",
        out_shape=[jax.ShapeDtypeStruct((s, d), F32), jax.ShapeDtypeStruct((s, d), F32),
                   jax.ShapeDtypeStruct((s, gw), BF16), jax.ShapeDtypeStruct((s, gw), F32)],
        grid=(s // tm,), in_specs=[_rows(tm, gw)] * 6 + [_full((d, gw)), _rows(tm, d), _full((1, d))],
        out_specs=[_rows(tm, d), _rows(tm, d), _rows(tm, gw), _rows(tm, gw)],
        compiler_params=_params("parallel"),
    )(*os_, *ls_, wot, x, g)


def _matmul_resnorm(a, w, x, g, *, name, bias=None, tm=512):
    s, k = a.shape
    d = w.shape[1]
    tm = min(tm, s)

    def body(*refs):
        a_ref, w_ref = refs[:2]
        b_ref = refs[2] if bias is not None else None
        x_ref, g_ref, y_ref, xo_ref = refs[-4:]
        y = jnp.dot(a_ref[...], w_ref[...], preferred_element_type=F32)
        if b_ref is not None:
            y = y + b_ref[...]
        _resnorm_store(y, x_ref, g_ref, y_ref, xo_ref)

    in_specs = [_rows(tm, k), _full((k, d))] + ([_full((1, d))] if bias is not None else []) + [_rows(tm, d), _full((1, d))]
    args = [a, w] + ([bias] if bias is not None else []) + [x, g]
    return pl.pallas_call(
        body, name=name, out_shape=[jax.ShapeDtypeStruct((s, d), F32)] * 2, grid=(s // tm,),
        in_specs=in_specs, out_specs=[_rows(tm, d)] * 2, compiler_params=_params("parallel"),
    )(*args)


def _conv3_taps(z, halo, first):
    row = lax.broadcasted_iota(jnp.int32, z.shape, 0)
    halo = halo * jnp.where(first, 0.0, 1.0)
    h6, h7 = halo[6:7, :], halo[7:8, :]
    z1 = jnp.where(row == 0, h7, pltpu.roll(z, 1, 0))
    z2 = jnp.where(row == 0, h6, jnp.where(row == 1, h7, pltpu.roll(z, 2, 0)))
    return z2, z1


def _ffn_cols(f):
    return 256 if f % 256 == 0 else 128


def _ffn_act(z, w_dw, b_dw, tm=512):
    s, f2 = z.shape
    f = f2 // 2
    tm = min(tm, s)
    tc = _ffn_cols(f)
    nfc = f // tc

    def body(zu, zg, hu, hg, wu, wg, bu, bg, o_ref):
        first = pl.program_id(0) == 0

        def conv(z_ref, h_ref, w_ref, b_ref):
            zc = z_ref[...].astype(F32)
            z2, z1 = _conv3_taps(zc, h_ref[...].astype(F32), first)
            return w_ref[0:1, :] * z2 + w_ref[1:2, :] * z1 + w_ref[2:3, :] * zc + b_ref[...]

        up, gate = conv(zu, hu, wu, bu), conv(zg, hg, wg, bg)
        o_ref[...] = (gate * _sigmoid(gate) * up).astype(BF16)

    hb = tm // 8
    tile = lambda off: pl.BlockSpec((tm, tc), lambda i, j: (i, off + j))
    halo = lambda off: pl.BlockSpec((8, tc), lambda i, j: (jnp.maximum(i * hb - 1, 0), off + j))
    prm = lambda rows, off: pl.BlockSpec((rows, tc), lambda i, j: (0, off + j))
    return pl.pallas_call(
        body, name="ffn_act", out_shape=jax.ShapeDtypeStruct((s, f), BF16), grid=(s // tm, nfc),
        in_specs=[tile(0), tile(nfc), halo(0), halo(nfc), prm(FFN_CONV, 0), prm(FFN_CONV, nfc), prm(1, 0), prm(1, nfc)],
        out_specs=pl.BlockSpec((tm, tc), lambda i, j: (i, j)), compiler_params=_params("parallel", "parallel"),
    )(z, z, z, z, w_dw, w_dw, b_dw, b_dw)


def _glu_conv(ag_ref, halo_ref, w_ref, ext_ref, first, c):
    tm = ag_ref.shape[0]
    hal = halo_ref[...].astype(F32)
    ext_ref[0:CONV_HALO, :] = hal[:, :c] * _sigmoid(hal[:, c:]) * jnp.where(first, 0.0, 1.0)
    ag = ag_ref[...].astype(F32)
    ext_ref[CONV_HALO:, :] = ag[:, :c] * _sigmoid(ag[:, c:])
    base = CONV_HALO - (CONV_KERNEL - 1)
    acc = w_ref[0:1, :] * ext_ref[base:base + tm, :]
    for j in range(1, CONV_KERNEL):
        acc = acc + w_ref[j:j + 1, :] * ext_ref[base + j:base + j + tm, :]
    return acc


def _layernorm_stats(u1):
    mu = jnp.mean(u1, axis=-1, keepdims=True)
    cen = u1 - mu
    rstd = lax.rsqrt(jnp.mean(cen * cen, axis=-1, keepdims=True) + EPS)
    return cen * rstd, rstd


def _conv_mid(ag, w_dw, b_dw, ln_g, ln_b, tm=256):
    s, c2 = ag.shape
    c = c2 // 2
    tm = min(tm, s)

    def body(ag_ref, halo_ref, w_ref, b_ref, g_ref, bb_ref, o_ref, ext_ref):
        u1 = _glu_conv(ag_ref, halo_ref, w_ref, ext_ref, pl.program_id(0) == 0, c) + b_ref[...]
        xh, _ = _layernorm_stats(u1)
        u2 = xh * g_ref[...] + bb_ref[...]
        o_ref[...] = (u2 * _sigmoid(u2)).astype(BF16)

    hb = tm // CONV_HALO
    return pl.pallas_call(
        body, name="conv_mid", out_shape=jax.ShapeDtypeStruct((s, c), BF16), grid=(s // tm,),
        in_specs=[_rows(tm, c2), pl.BlockSpec((CONV_HALO, c2), lambda i: (jnp.maximum(i * hb - 1, 0), 0)),
                  _full((CONV_KERNEL, c)), _full((1, c)), _full((1, c)), _full((1, c))],
        out_specs=_rows(tm, c), scratch_shapes=[pltpu.VMEM((CONV_HALO + tm, c), F32)],
        compiler_params=_params("arbitrary"),
    )(ag, ag, w_dw, b_dw, ln_g, ln_b)


def _loss_grad(xo, target, tm=512):
    s, d = xo.shape
    tm = min(tm, s)

    def body(x_ref, t_ref, dx_ref, loss_ref):
        @pl.when(pl.program_id(0) == 0)
        def _():
            loss_ref[...] = jnp.zeros_like(loss_ref)

        err = x_ref[...] - t_ref[...]
        dx_ref[...] = err * (1.0 / d)
        loss_ref[...] += 0.5 * jnp.sum(jnp.mean(err * err, axis=-1, keepdims=True))

    return pl.pallas_call(
        body, name="loss_grad", out_shape=[jax.ShapeDtypeStruct((s, d), F32), jax.ShapeDtypeStruct((1, 128), F32)],
        grid=(s // tm,), in_specs=[_rows(tm, d)] * 2, out_specs=[_rows(tm, d), _full((1, 128))],
        compiler_params=_params("arbitrary"),
    )(xo, target)


def _postnorm_bwd(y, g, dxo, *, name, with_bias_grad=False, tm=512):
    s, d = y.shape
    tm = min(tm, s)

    def body(y_ref, g_ref, dx_ref, dy_ref, dg_ref, *rest):
        @pl.when(pl.program_id(0) == 0)
        def _():
            dg_ref[...] = jnp.zeros_like(dg_ref)
            for r_ in rest:
                r_[...] = jnp.zeros_like(r_)

        yv, dxo_v = y_ref[...], dx_ref[...]
        r = lax.rsqrt(jnp.mean(yv * yv, axis=-1, keepdims=True) + EPS)
        yh = yv * r
        dyh = dxo_v * g_ref[...]
        dy = r * (dyh - yh * jnp.mean(dyh * yh, axis=-1, keepdims=True))
        dy_ref[...] = dy.astype(BF16)
        dg_ref[...] += jnp.sum(dxo_v * yh, axis=0, keepdims=True)
        for r_ in rest:
            r_[...] += jnp.sum(dy, axis=0, keepdims=True)

    nacc = 2 if with_bias_grad else 1
    return pl.pallas_call(
        body, name=name, out_shape=[jax.ShapeDtypeStruct((s, d), BF16)] + [jax.ShapeDtypeStruct((1, d), F32)] * nacc,
        grid=(s // tm,), in_specs=[_rows(tm, d), _full((1, d)), _rows(tm, d)],
        out_specs=[_rows(tm, d)] + [_full((1, d))] * nacc, compiler_params=_params("arbitrary"),
    )(y, g, dxo)


def _matmul(gmat, w, *, name, out_dtype, transposed_w, tm=512):
    s, k = gmat.shape
    n = w.shape[0] if transposed_w else w.shape[1]
    tm = min(tm, s)

    def body(g_ref, w_ref, o_ref):
        if transposed_w:
            acc = lax.dot_general(g_ref[...], w_ref[...], NT, preferred_element_type=F32)
        else:
            acc = jnp.dot(g_ref[...], w_ref[...], preferred_element_type=F32)
        o_ref[...] = acc.astype(out_dtype)

    return pl.pallas_call(
        body, name=name, out_shape=jax.ShapeDtypeStruct((s, n), out_dtype), grid=(s // tm,),
        in_specs=[_rows(tm, k), _full(w.shape)], out_specs=_rows(tm, n), compiler_params=_params("parallel"),
    )(gmat, w)


def _matmul_prenorm_bwd(pieces, wt, x, g, dres, *, name, tm=256):
    s, d = x.shape
    tm = min(tm, s)
    np_ = len(pieces)

    def body(*refs):
        p_refs, w_refs = refs[:np_], refs[np_:2 * np_]
        x_ref, g_ref, r_ref, dx_ref, dg_ref = refs[2 * np_:]

        @pl.when(pl.program_id(0) == 0)
        def _():
            dg_ref[...] = jnp.zeros_like(dg_ref)

        dh = None
        for p_ref, w_ref in zip(p_refs, w_refs):
            t = jnp.dot(p_ref[...], w_ref[...], preferred_element_type=F32)
            dh = t if dh is None else dh + t
        xv = x_ref[...]
        r = lax.rsqrt(jnp.mean(xv * xv, axis=-1, keepdims=True) + EPS)
        xh = xv * r
        dyh = dh * g_ref[...]
        dx_ref[...] = r_ref[...] + r * (dyh - xh * jnp.mean(dyh * xh, axis=-1, keepdims=True))
        dg_ref[...] += jnp.sum(dh * xh, axis=0, keepdims=True)

    in_specs = []
    for _, c0, kc, _ in pieces:
        assert c0 % kc == 0
        in_specs.append(pl.BlockSpec((tm, kc), lambda i, _b=c0 // kc: (i, _b)))
    for _, _, kc, r0 in pieces:
        assert r0 % kc == 0
        in_specs.append(pl.BlockSpec((kc, d), lambda i, _b=r0 // kc: (_b, 0)))
    in_specs += [_rows(tm, d), _full((1, d)), _rows(tm, d)]
    return pl.pallas_call(
        body, name=name, out_shape=[jax.ShapeDtypeStruct((s, d), F32), jax.ShapeDtypeStruct((1, d), F32)],
        grid=(s // tm,), in_specs=in_specs, out_specs=[_rows(tm, d), _full((1, d))],
        compiler_params=_params("arbitrary"),
    )(*[p[0] for p in pieces], *[wt] * np_, x, g, dres)


def _weight_grad(a, gmat, *, name, a_col0=0, ka=None, out=None, out_shape=None, layer=0, row0=0, ts=512):
    s = a.shape[0]
    ka = a.shape[1] if ka is None else ka
    n = gmat.shape[1]
    ts = min(ts, s)
    tka = _tile(ka, a_col0, row0)
    shape = out.shape if out is not None else out_shape

    def body(a_ref, g_ref, *rest):
        o_ref = rest[-1]

        @pl.when(pl.program_id(1) == 0)
        def _():
            o_ref[...] = jnp.zeros_like(o_ref)

        o_ref[...] += lax.dot_general(a_ref[...], g_ref[...], TN, preferred_element_type=F32)

    in_specs = [pl.BlockSpec((ts, tka), lambda k, i: (i, a_col0 // tka + k)), pl.BlockSpec((ts, n), lambda k, i: (i, 0))]
    args = [a, gmat]
    aliases = {}
    if out is not None:
        in_specs.append(ANY)
        args.append(out)
        aliases = {2: 0}
    return pl.pallas_call(
        body, name=name, out_shape=jax.ShapeDtypeStruct(shape, F32), grid=(ka // tka, s // ts), in_specs=in_specs,
        out_specs=pl.BlockSpec((None, tka, n), lambda k, i: (layer, row0 // tka + k, 0)),
        input_output_aliases=aliases, compiler_params=_params("parallel", "arbitrary"),
    )(*args)


def _ffn_act_bwd(z, dact, w_dw, b_dw, tm=512):
    s, f2 = z.shape
    f = f2 // 2
    tm = min(tm, s)
    tc = _ffn_cols(f)
    nfc = f // tc

    def body(zu, zg, hu, hg, wu, wg, bu, bg, da_ref, du_ref, dgt_ref, dbu_ref, dbg_ref, dwu_ref, dwg_ref):
        i = pl.program_id(1)

        @pl.when(i == 0)
        def _():
            for r_ in (dbu_ref, dbg_ref, dwu_ref, dwg_ref):
                r_[...] = jnp.zeros_like(r_)

        def conv(z_ref, h_ref, w_ref, b_ref):
            zc = z_ref[...].astype(F32)
            z2, z1 = _conv3_taps(zc, h_ref[...].astype(F32), i == 0)
            return (z2, z1, zc), w_ref[0:1, :] * z2 + w_ref[1:2, :] * z1 + w_ref[2:3, :] * zc + b_ref[...]

        taps_u, up = conv(zu, hu, wu, bu)
        taps_g, gate = conv(zg, hg, wg, bg)
        da = da_ref[...].astype(F32)
        sg = _sigmoid(gate)
        d_up = da * (gate * sg)
        d_gate = da * up * (sg * (1.0 + gate * (1.0 - sg)))
        du_ref[...] = d_up.astype(BF16)
        dgt_ref[...] = d_gate.astype(BF16)
        for dv, taps, db_ref, dw_ref in ((d_up, taps_u, dbu_ref, dwu_ref), (d_gate, taps_g, dbg_ref, dwg_ref)):
            db_ref[...] += jnp.sum(dv, axis=0, keepdims=True)
            for k_, tap in enumerate(taps):
                dw_ref[k_:k_ + 1, :] += jnp.sum(dv * tap, axis=0, keepdims=True)

    hb = tm // 8
    tile = lambda off: pl.BlockSpec((tm, tc), lambda j, i: (i, off + j))
    halo = lambda off: pl.BlockSpec((8, tc), lambda j, i: (jnp.maximum(i * hb - 1, 0), off + j))
    prm = lambda rows, off: pl.BlockSpec((rows, tc), lambda j, i: (0, off + j))
    acc = lambda rows: pl.BlockSpec((rows, tc), lambda j, i: (0, j))
    return pl.pallas_call(
        body, name="ffn_act_bwd",
        out_shape=[jax.ShapeDtypeStruct((s, f), BF16)] * 2 + [jax.ShapeDtypeStruct((1, f), F32)] * 2
        + [jax.ShapeDtypeStruct((FFN_CONV, f), F32)] * 2,
        grid=(nfc, s // tm),
        in_specs=[tile(0), tile(nfc), halo(0), halo(nfc), prm(FFN_CONV, 0), prm(FFN_CONV, nfc), prm(1, 0), prm(1, nfc), tile(0)],
        out_specs=[tile(0), tile(0), acc(1), acc(1), acc(FFN_CONV), acc(FFN_CONV)],
        compiler_params=_params("parallel", "arbitrary"),
    )(z, z, z, z, w_dw, w_dw, b_dw, b_dw, dact)


def _conv3_transpose(dug, w_dw, col0, tm=512):
    s, f = dug.shape
    tm = min(tm, s)
    tc = _ffn_cols(f)
    nfc = f // tc
    nrow = s // tm
    off = col0 // tc

    def body(d_ref, n_ref, w_ref, o_ref):
        last = pl.program_id(0) == nrow - 1
        dv = d_ref[...].astype(F32)
        nxt = n_ref[...].astype(F32) * jnp.where(last, 0.0, 1.0)
        n0, n1 = nxt[0:1, :], nxt[1:2, :]
        row = lax.broadcasted_iota(jnp.int32, dv.shape, 0)
        d1 = jnp.where(row == tm - 1, n0, pltpu.roll(dv, tm - 1, 0))
        d2 = jnp.where(row == tm - 1, n1, jnp.where(row == tm - 2, n0, pltpu.roll(dv, tm - 2, 0)))
        o_ref[...] = (w_ref[2:3, :] * dv + w_ref[1:2, :] * d1 + w_ref[0:1, :] * d2).astype(BF16)

    hb = tm // 8
    return pl.pallas_call(
        body, name="conv3_transpose", out_shape=jax.ShapeDtypeStruct((s, f), BF16), grid=(nrow, nfc),
        in_specs=[pl.BlockSpec((tm, tc), lambda i, j: (i, j)),
                  pl.BlockSpec((8, tc), lambda i, j: (jnp.minimum((i + 1) * hb, s // 8 - 1), j)),
                  pl.BlockSpec((FFN_CONV, tc), lambda i, j: (0, off + j))],
        out_specs=pl.BlockSpec((tm, tc), lambda i, j: (i, j)), compiler_params=_params("parallel", "parallel"),
    )(dug, dug, w_dw)


def _conv_mid_bwd(ag, du3, w_dw, b_dw, ln_g, ln_b, tm=256):
    s, c2 = ag.shape
    c = c2 // 2
    tm = min(tm, s)

    def body(ag_ref, halo_ref, du_ref, w_ref, b_ref, g_ref, bb_ref, o_ref, dlg_ref, dlb_ref, db_ref, dw_ref, ext_ref):
        @pl.when(pl.program_id(0) == 0)
        def _():
            for r_ in (dlg_ref, dlb_ref, db_ref, dw_ref):
                r_[...] = jnp.zeros_like(r_)

        u1 = _glu_conv(ag_ref, halo_ref, w_ref, ext_ref, pl.program_id(0) == 0, c) + b_ref[...]
        xh, rstd = _layernorm_stats(u1)
        u2 = xh * g_ref[...] + bb_ref[...]
        sg = _sigmoid(u2)
        du2 = du_ref[...] * (sg * (1.0 + u2 * (1.0 - sg)))
        dlg_ref[...] += jnp.sum(du2 * xh, axis=0, keepdims=True)
        dlb_ref[...] += jnp.sum(du2, axis=0, keepdims=True)
        dxh = du2 * g_ref[...]
        du1 = rstd * (dxh - jnp.mean(dxh, axis=-1, keepdims=True) - xh * jnp.mean(dxh * xh, axis=-1, keepdims=True))
        o_ref[...] = du1.astype(BF16)
        db_ref[...] += jnp.sum(du1, axis=0, keepdims=True)
        base = CONV_HALO - (CONV_KERNEL - 1)
        for j in range(CONV_KERNEL):
            dw_ref[j:j + 1, :] += jnp.sum(du1 * ext_ref[base + j:base + j + tm, :], axis=0, keepdims=True)

    hb = tm // CONV_HALO
    vec = _full((1, c))
    return pl.pallas_call(
        body, name="conv_mid_bwd",
        out_shape=[jax.ShapeDtypeStruct((s, c), BF16)] + [jax.ShapeDtypeStruct((1, c), F32)] * 3
        + [jax.ShapeDtypeStruct((CONV_HALO, c), F32)],
        grid=(s // tm,),
        in_specs=[_rows(tm, c2), pl.BlockSpec((CONV_HALO, c2), lambda i: (jnp.maximum(i * hb - 1, 0), 0)), _rows(tm, c),
                  _full((CONV_KERNEL, c)), vec, vec, vec],
        out_specs=[_rows(tm, c), vec, vec, vec, _full((CONV_HALO, c))],
        scratch_shapes=[pltpu.VMEM((CONV_HALO + tm, c), F32)], compiler_params=_params("arbitrary"),
    )(ag, ag, du3, w_dw, b_dw, ln_g, ln_b)


def _glu_conv_bwd(du1, ag, w_dw, tm=256):
    s, c = du1.shape
    tm = min(tm, s)
    nrow = s // tm

    def body(d_ref, n_ref, ag_ref, w_ref, o_ref, db_ref, ext_ref):
        @pl.when(pl.program_id(0) == 0)
        def _():
            db_ref[...] = jnp.zeros_like(db_ref)

        ext_ref[0:tm, :] = d_ref[...].astype(F32)
        ext_ref[tm:, :] = n_ref[...].astype(F32) * jnp.where(pl.program_id(0) == nrow - 1, 0.0, 1.0)
        top = CONV_KERNEL - 1
        du0 = w_ref[0:1, :] * ext_ref[top:top + tm, :]
        for j in range(1, CONV_KERNEL):
            du0 = du0 + w_ref[j:j + 1, :] * ext_ref[top - j:top - j + tm, :]
        ag = ag_ref[...].astype(F32)
        a, gt = ag[:, :c], ag[:, c:]
        sg = _sigmoid(gt)
        da = du0 * sg
        dgt = du0 * a * (sg * (1.0 - sg))
        o_ref[:, :c] = da.astype(BF16)
        o_ref[:, c:] = dgt.astype(BF16)
        db_ref[:, :c] += jnp.sum(da, axis=0, keepdims=True)
        db_ref[:, c:] += jnp.sum(dgt, axis=0, keepdims=True)

    hb = tm // CONV_HALO
    return pl.pallas_call(
        body, name="glu_conv_bwd",
        out_shape=[jax.ShapeDtypeStruct((s, 2 * c), BF16), jax.ShapeDtypeStruct((1, 2 * c), F32)], grid=(nrow,),
        in_specs=[_rows(tm, c), pl.BlockSpec((CONV_HALO, c), lambda i: (jnp.minimum((i + 1) * hb, s // CONV_HALO - 1), 0)),
                  _rows(tm, 2 * c), _full((CONV_KERNEL, c))],
        out_specs=[_rows(tm, 2 * c), _full((1, 2 * c))],
        scratch_shapes=[pltpu.VMEM((tm + CONV_HALO, c), F32)], compiler_params=_params("arbitrary"),
    )(du1, du1, ag, w_dw)


def _head_rows(v, mask):
    return jnp.max(jnp.where(mask, v, -jnp.inf), axis=-1, keepdims=True)


def _attn_bwd_dq(qkv, dmix, mixed, lse, rope, grp, dil):
    s, ncol = qkv.shape
    nblk = ncol // GROUP_WIDTH
    l = s // dil
    nb = l // SPAN
    view = lambda t: t.reshape(l, dil * t.shape[1])

    def body(q_ref, kp_ref, kc_ref, vp_ref, vc_ref, do_ref, mx_ref, l_ref, c_ref, su_ref, sd_ref, o_ref):
        b = pl.program_id(1)
        row = lax.broadcasted_iota(jnp.int32, (SPAN, 2 * SPAN), 0)
        col = lax.broadcasted_iota(jnp.int32, (SPAN, 2 * SPAN), 1)
        no_prev = jnp.where(b > 0, 0, 4 * SPAN)
        valid = ((col < SPAN) & (col >= row + no_prev)) | ((col >= SPAN) & (col - SPAN <= row))
        masks, keep = _head_masks()
        for p in range(GROUP_WIDTH // 128):
            sl = slice(p * 128, (p + 1) * 128)
            qp, dop = q_ref[:, sl], do_ref[:, sl]
            kk = jnp.concatenate([kp_ref[:, sl], kc_ref[:, sl]], axis=0)
            vv = jnp.concatenate([vp_ref[:, sl], vc_ref[:, sl]], axis=0)
            prod = dop.astype(F32) * mx_ref[:, sl].astype(F32)
            lsep = l_ref[:, sl]
            dqs = []
            for h in range(2):
                qh, doh = qp * keep[h], dop * keep[h]
                sc = lax.dot_general(qh, kk, NT, preferred_element_type=F32) * (HEAD_DIM ** -0.5)
                pe = jnp.where(valid, jnp.exp(sc - _head_rows(lsep, masks[h])), 0.0)
                dp = lax.dot_general(doh, vv, NT, preferred_element_type=F32)
                dbar = jnp.sum(jnp.where(masks[h], prod, 0.0), axis=-1, keepdims=True)
                ds = pe * (dp - dbar) * (HEAD_DIM ** -0.5)
                dqs.append(jnp.dot(ds.astype(BF16), kk, preferred_element_type=F32))
            dq = jnp.where(masks[0], dqs[0], dqs[1])
            o_ref[:, sl] = _rope_transpose(dq, c_ref[...], su_ref[...], sd_ref[...]).astype(BF16)

    blk = (SPAN, GROUP_WIDTH)
    cur = lambda t: pl.BlockSpec(blk, lambda r, b: (b, r * nblk + 3 * t + grp))
    prev = lambda t: pl.BlockSpec(blk, lambda r, b: (jnp.maximum(b - 1, 0), r * nblk + 3 * t + grp))
    own = pl.BlockSpec(blk, lambda r, b: (b, r))
    tab = pl.BlockSpec((SPAN, 128), lambda r, b: (b, r))
    qv = view(qkv)
    out = pl.pallas_call(
        body, name=f"attn_bwd_dq_g{grp}", out_shape=jax.ShapeDtypeStruct((l, dil * GROUP_WIDTH), BF16), grid=(dil, nb),
        in_specs=[cur(0), prev(1), cur(1), prev(2), cur(2), own, own, own, tab, tab, tab], out_specs=own,
        compiler_params=_params("parallel", "arbitrary"),
    )(qv, qv, qv, qv, qv, view(dmix), view(mixed), view(lse), *[view(t) for t in rope])
    return out.reshape(s, GROUP_WIDTH)


def _attn_bwd_dkv(qkv, dmix, mixed, lse, rope, grp, dil):
    s, ncol = qkv.shape
    nblk = ncol // GROUP_WIDTH
    l = s // dil
    nb = l // SPAN
    view = lambda t: t.reshape(l, dil * t.shape[1])

    def body(k_ref, v_ref, qc_ref, qn_ref, doc_ref, don_ref, mc_ref, mn_ref, lc_ref, ln_ref,
             c_ref, su_ref, sd_ref, o_ref):
        b = pl.program_id(1)
        row = lax.broadcasted_iota(jnp.int32, (2 * SPAN, SPAN), 0)
        col = lax.broadcasted_iota(jnp.int32, (2 * SPAN, SPAN), 1)
        no_next = jnp.where(b < nb - 1, 0, 4 * SPAN)
        valid = ((row < SPAN) & (col <= row)) | ((row >= SPAN) & (col >= row - SPAN + no_next))
        masks, keep = _head_masks()
        masks2, _ = _head_masks(2 * SPAN)
        for p in range(GROUP_WIDTH // 128):
            sl = slice(p * 128, (p + 1) * 128)
            kp, vp = k_ref[:, sl], v_ref[:, sl]
            qq = jnp.concatenate([qc_ref[:, sl], qn_ref[:, sl]], axis=0)
            doo = jnp.concatenate([doc_ref[:, sl], don_ref[:, sl]], axis=0)
            mm = jnp.concatenate([mc_ref[:, sl], mn_ref[:, sl]], axis=0)
            ll = jnp.concatenate([lc_ref[:, sl], ln_ref[:, sl]], axis=0)
            prod = doo.astype(F32) * mm.astype(F32)
            dks, dvs = [], []
            for h in range(2):
                qh, doh = qq * keep[h], doo * keep[h]
                sc = lax.dot_general(qh, kp, NT, preferred_element_type=F32) * (HEAD_DIM ** -0.5)
                pe = jnp.where(valid, jnp.exp(sc - _head_rows(ll, masks2[h])), 0.0)
                dp = lax.dot_general(doh, vp, NT, preferred_element_type=F32)
                dbar = jnp.sum(jnp.where(masks2[h], prod, 0.0), axis=-1, keepdims=True)
                ds = pe * (dp - dbar) * (HEAD_DIM ** -0.5)
                dvs.append(lax.dot_general(pe.astype(BF16), doo, TN, preferred_element_type=F32))
                dks.append(lax.dot_general(ds.astype(BF16), qq, TN, preferred_element_type=F32))
            dk = jnp.where(masks[0], dks[0], dks[1])
            o_ref[:, sl] = _rope_transpose(dk, c_ref[...], su_ref[...], sd_ref[...]).astype(BF16)
            o_ref[:, GROUP_WIDTH + p * 128:GROUP_WIDTH + (p + 1) * 128] = jnp.where(masks[0], dvs[0], dvs[1]).astype(BF16)

    blk = (SPAN, GROUP_WIDTH)
    nxt_b = lambda b: jnp.minimum(b + 1, nb - 1)
    col_of = lambda t: pl.BlockSpec(blk, lambda r, b: (b, r * nblk + 3 * t + grp))
    q_next = pl.BlockSpec(blk, lambda r, b: (nxt_b(b), r * nblk + grp))
    own = pl.BlockSpec(blk, lambda r, b: (b, r))
    own_next = pl.BlockSpec(blk, lambda r, b: (nxt_b(b), r))
    tab = pl.BlockSpec((SPAN, 128), lambda r, b: (b, r))
    qv, dv_, mv, lv = view(qkv), view(dmix), view(mixed), view(lse)
    out = pl.pallas_call(
        body, name=f"attn_bwd_dkv_g{grp}", out_shape=jax.ShapeDtypeStruct((l, dil * 2 * GROUP_WIDTH), BF16), grid=(dil, nb),
        in_specs=[col_of(1), col_of(2), col_of(0), q_next, own, own_next, own, own_next, own, own_next, tab, tab, tab],
        out_specs=pl.BlockSpec((SPAN, 2 * GROUP_WIDTH), lambda r, b: (b, r)),
        compiler_params=_params("parallel", "arbitrary"),
    )(qv, qv, qv, qv, dv_, dv_, mv, mv, lv, lv, *[view(t) for t in rope])
    return out.reshape(s, 2 * GROUP_WIDTH)


def _rope_freq_row():
    half = ROT_DIM // 2
    inv = (ROPE_THETA ** (-np.arange(half, dtype=np.float32) / half)).astype(np.float32)
    row = np.zeros((1, 128), np.float32)
    for head in range(128 // HEAD_DIM):
        row[0, head * HEAD_DIM:head * HEAD_DIM + half] = inv
        row[0, head * HEAD_DIM + half:head * HEAD_DIM + ROT_DIM] = inv
    return jnp.asarray(row)


def _ffn_fwd(x, g_pre, g_post, w_up_t, w_dw, b_dw, w_down):
    h, z = _norm_matmul(x, g_pre, w_up_t, tn=512 if w_up_t.shape[0] % 512 == 0 else 256, name="ffn_up")
    act = _ffn_act(z, w_dw, b_dw)
    y, xo = _matmul_resnorm(act, w_down, x, g_post, name="ffn_down")
    return xo, (x, h, z, act, y)


def _ffn_bwd(saved, dxo, g_pre, g_post, w_up_t, w_dw, b_dw, w_down, layer, d_up_t, d_down):
    x, h, z, act, y = saved
    f = act.shape[1]
    d = x.shape[1]
    dy, dg_post = _postnorm_bwd(y, g_post, dxo, name="ffn_post_bwd")
    dact = _matmul(dy, w_down, name="ffn_dact", out_dtype=BF16, transposed_w=True)
    d_down = _weight_grad(act, dy, name="ffn_dw_down", out=d_down, out_shape=(2, f, d), layer=layer)
    dug_u, dug_g, db_u, db_g, dwd_u, dwd_g = _ffn_act_bwd(z, dact, w_dw, b_dw)
    dz_u = _conv3_transpose(dug_u, w_dw, 0)
    dz_g = _conv3_transpose(dug_g, w_dw, f)
    dx, dg_pre = _matmul_prenorm_bwd([(dz_u, 0, f, 0), (dz_g, 0, f, f)], w_up_t, x, g_pre, dxo, name="ffn_dx")
    d_up_t = _weight_grad(dz_u, h, name="ffn_dw_up", out=d_up_t, out_shape=(2, 2 * f, d), layer=layer)
    d_up_t = _weight_grad(dz_g, h, name="ffn_dw_up", out=d_up_t, layer=layer, row0=f)
    grads = dict(w_dw=jnp.concatenate([dwd_u, dwd_g], axis=1), b_dw=jnp.concatenate([db_u, db_g], axis=1),
                 g_pre=dg_pre, g_post=dg_post)
    return dx, grads, d_up_t, d_down


def _local_step(x, pos_col, target, p):
    ng = p["norm_g"]
    row = lambda r: ng[r:r + 1]
    rope = _rope_tables(pos_col, _rope_freq_row())
    d = x.shape[1]

    h0, qkv = _norm_matmul(x, row(0), p["w_qkv_t"], tn=GROUP_WIDTH, name="attn_qkv", rope=rope, rope_blocks=6)
    os_, ls_ = zip(*[_attn_fwd(qkv, g_, d_) for g_, d_ in enumerate(DILATIONS)])
    y_a, x1, mixed, lse = _mix_wo(os_, ls_, p["w_o_t"], x, row(1))
    x2, ffn0 = _ffn_fwd(x1, row(2), row(3), p["w_up_t"][0], p["ffn_w_dw"][0], p["ffn_b_dw"][0], p["w_down"][0])
    h1, ag = _norm_matmul(x2, row(4), p["w_pw1_t"], tn=512, name="conv_pw1", bias=p["b_pw1"])
    u3 = _conv_mid(ag, p["conv_w_dw"], p["conv_b_dw"], p["ln_g"], p["ln_b"])
    y_c, x3 = _matmul_resnorm(u3, p["w_pw2"], x2, row(5), name="conv_pw2", bias=p["b_pw2"])
    x4, ffn1 = _ffn_fwd(x3, row(6), row(7), p["w_up_t"][1], p["ffn_w_dw"][1], p["ffn_b_dw"][1], p["w_down"][1])
    dx4, loss = _loss_grad(x4, target)

    dx3, gf1, d_up_t, d_down = _ffn_bwd(ffn1, dx4, row(6), row(7), p["w_up_t"][1], p["ffn_w_dw"][1], p["ffn_b_dw"][1],
                                        p["w_down"][1], 1, None, None)
    dy_c, dg5, db_pw2 = _postnorm_bwd(y_c, row(5), dx3, name="conv_post_bwd", with_bias_grad=True)
    du3 = _matmul(dy_c, p["w_pw2"], name="conv_du3", out_dtype=F32, transposed_w=True)
    d_wpw2 = _weight_grad(u3, dy_c, name="conv_dw_pw2", out_shape=(1, u3.shape[1], d))
    du1, d_lng, d_lnb, d_cbdw, d_cwdw = _conv_mid_bwd(ag, du3, p["conv_w_dw"], p["conv_b_dw"], p["ln_g"], p["ln_b"])
    dag, db_pw1 = _glu_conv_bwd(du1, ag, p["conv_w_dw"])
    dx2, dg4 = _matmul_prenorm_bwd([(dag, 0, dag.shape[1], 0)], p["w_pw1_t"], x2, row(4), dx3, name="conv_dx")
    d_wpw1_t = _weight_grad(dag, h1, name="conv_dw_pw1", out_shape=(1, dag.shape[1], d))
    dx1, gf0, d_up_t, d_down = _ffn_bwd(ffn0, dx2, row(2), row(3), p["w_up_t"][0], p["ffn_w_dw"][0], p["ffn_b_dw"][0],
                                        p["w_down"][0], 0, d_up_t, d_down)
    dy_a, dg1 = _postnorm_bwd(y_a, row(1), dx1, name="attn_post_bwd")
    dmix = _matmul(dy_a, p["w_o_t"], name="attn_dmix", out_dtype=BF16, transposed_w=False)
    d_wo_t = _weight_grad(dy_a, mixed, name="attn_dw_o", out_shape=(1, d, GROUP_WIDTH))
    pieces, d_wqkv_t = [], None
    for g_, d_ in enumerate(DILATIONS):
        dq = _attn_bwd_dq(qkv, dmix, mixed, lse, rope, g_, d_)
        dkv = _attn_bwd_dkv(qkv, dmix, mixed, lse, rope, g_, d_)
        for t, (arr, c0) in enumerate(((dq, 0), (dkv, 0), (dkv, GROUP_WIDTH))):
            r0 = (3 * t + g_) * GROUP_WIDTH
            pieces.append((arr, c0, GROUP_WIDTH, r0))
            d_wqkv_t = _weight_grad(arr, h0, name="attn_dw_qkv", a_col0=c0, ka=GROUP_WIDTH, out=d_wqkv_t,
                                    out_shape=(1, qkv.shape[1], d), row0=r0)
    grad_x, dg0 = _matmul_prenorm_bwd(pieces, p["w_qkv_t"], x, row(0), dx1, name="attn_dx")

    grads = dict(
        norm_g=jnp.concatenate([dg0, dg1, gf0["g_pre"], gf0["g_post"], dg4, dg5, gf1["g_pre"], gf1["g_post"]], axis=0),
        w_qkv_t=d_wqkv_t, w_o_t=d_wo_t, w_pw1_t=d_wpw1_t, b_pw1=db_pw1,
        conv_w_dw=d_cwdw[:CONV_KERNEL], conv_b_dw=d_cbdw, ln_g=d_lng, ln_b=d_lnb, w_pw2=d_wpw2, b_pw2=db_pw2,
        w_up_t=d_up_t, ffn_w_dw=jnp.stack([gf0["w_dw"], gf1["w_dw"]]),
        ffn_b_dw=jnp.concatenate([gf0["b_dw"], gf1["b_dw"]], axis=0), w_down=d_down)
    return loss, grad_x, grads


SMALL_AXIS = dict(norm_g=2, conv_b_pw1=1, conv_w_dw=2, conv_b_dw=1, conv_ln_g=1, conv_ln_b=1, conv_b_pw2=1, ffn_w_dw=2)
SMALL = tuple(SMALL_AXIS)
MATMUL_WEIGHTS = dict(attn_w_qkv=True, conv_w_pw1=True, ffn_w_up=True, conv_w_pw2=False, ffn_w_down=False)


def _pack(arrays, cols, row_multiple):
    flat = jnp.concatenate([a.reshape(-1) for a in arrays])
    rows = -(-flat.shape[0] // cols)
    rows = -(-rows // row_multiple) * row_multiple
    return jnp.pad(flat, (0, rows * cols - flat.shape[0])).reshape(rows, cols)


def _unpack(packed, shapes):
    flat = packed.reshape(packed.shape[:-2] + (-1,))
    out, off = [], 0
    for shp in shapes:
        n = math.prod(shp)
        out.append(flat[..., off:off + n].reshape(packed.shape[:-2] + tuple(shp)))
        off += n
    return out


def _join_shards(stacked, axis):
    moved = jnp.moveaxis(stacked, 0, axis)
    shp = moved.shape
    return moved.reshape(shp[:axis] + (shp[axis] * shp[axis + 1],) + shp[axis + 2:])


def _split_shards(whole, axis):
    shp = whole.shape
    cut = whole.reshape(shp[:axis] + (N_DEV, shp[axis] // N_DEV) + shp[axis + 1:])
    return jnp.moveaxis(cut, axis, 0)


def _row_shard(w, transposed):
    t = jnp.swapaxes(w, 1, 2) if transposed else w
    return t.astype(BF16).reshape(-1, t.shape[-1])


def kernel(x, positions, norm_g, attn_w_qkv, attn_w_o, conv_w_pw1, conv_b_pw1, conv_w_dw, conv_b_dw, conv_ln_g, conv_ln_b, conv_w_pw2, conv_b_pw2, ffn_w_up, ffn_w_dw, ffn_b_dw, ffn_w_down, loss_target, m_norm_g, m_attn_w_qkv, m_attn_w_o, m_conv_w_pw1, m_conv_b_pw1, m_conv_w_dw, m_conv_b_dw, m_conv_ln_g, m_conv_ln_b, m_conv_w_pw2, m_conv_b_pw2, m_ffn_w_up, m_ffn_w_dw, m_ffn_b_dw, m_ffn_w_down, v_norm_g, v_attn_w_qkv, v_attn_w_o, v_conv_w_pw1, v_conv_b_pw1, v_conv_w_dw, v_conv_b_dw, v_conv_ln_g, v_conv_ln_b, v_conv_w_pw2, v_conv_b_pw2, v_ffn_w_up, v_ffn_w_dw, v_ffn_b_dw, v_ffn_w_down):
    w = dict(norm_g=norm_g, attn_w_qkv=attn_w_qkv, attn_w_o=attn_w_o, conv_w_pw1=conv_w_pw1, conv_b_pw1=conv_b_pw1,
             conv_w_dw=conv_w_dw, conv_b_dw=conv_b_dw, conv_ln_g=conv_ln_g, conv_ln_b=conv_ln_b, conv_w_pw2=conv_w_pw2,
             conv_b_pw2=conv_b_pw2, ffn_w_up=ffn_w_up, ffn_w_dw=ffn_w_dw, ffn_w_down=ffn_w_down)
    m = dict(norm_g=m_norm_g, attn_w_qkv=m_attn_w_qkv, attn_w_o=m_attn_w_o, conv_w_pw1=m_conv_w_pw1, conv_b_pw1=m_conv_b_pw1,
             conv_w_dw=m_conv_w_dw, conv_b_dw=m_conv_b_dw, conv_ln_g=m_conv_ln_g, conv_ln_b=m_conv_ln_b, conv_w_pw2=m_conv_w_pw2,
             conv_b_pw2=m_conv_b_pw2, ffn_w_up=m_ffn_w_up, ffn_w_dw=m_ffn_w_dw, ffn_w_down=m_ffn_w_down)
    v = dict(norm_g=v_norm_g, attn_w_qkv=v_attn_w_qkv, attn_w_o=v_attn_w_o, conv_w_pw1=v_conv_w_pw1, conv_b_pw1=v_conv_b_pw1,
             conv_w_dw=v_conv_w_dw, conv_b_dw=v_conv_b_dw, conv_ln_g=v_conv_ln_g, conv_ln_b=v_conv_ln_b, conv_w_pw2=v_conv_w_pw2,
             conv_b_pw2=v_conv_b_pw2, ffn_w_up=v_ffn_w_up, ffn_w_dw=v_ffn_w_dw, ffn_w_down=v_ffn_w_down)
    d = x.shape[-1]

    shares = [_row_shard(w[n], t) for n, t in MATMUL_WEIGHTS.items()]
    rows = [s_.shape[0] for s_ in shares]
    big = _all_gather(jnp.concatenate(shares, axis=0), "gather_matmul_weights")
    whole, r0 = {}, 0
    for (n, _), nr in zip(MATMUL_WEIGHTS.items(), rows):
        layers = w[n].shape[0]
        seg = big[:, r0:r0 + nr].reshape(N_DEV, layers, nr // layers, d)
        whole[n] = [seg[:, l_].reshape(-1, d) for l_ in range(layers)]
        r0 += nr
    w_o_t = _all_gather(_row_shard(attn_w_o, True), "gather_w_o").reshape(d, -1)
    small = _all_gather(_pack([w[n] for n in SMALL], 128, 8), "gather_small_weights")
    sm = {n: _join_shards(stacked, SMALL_AXIS[n])
          for n, stacked in zip(SMALL, _unpack(small, [w[n].shape for n in SMALL]))}
    p = dict(norm_g=sm["norm_g"].reshape(-1, d), w_qkv_t=whole["attn_w_qkv"][0], w_o_t=w_o_t,
             w_pw1_t=whole["conv_w_pw1"][0], b_pw1=sm["conv_b_pw1"], conv_w_dw=sm["conv_w_dw"][0],
             conv_b_dw=sm["conv_b_dw"], ln_g=sm["conv_ln_g"], ln_b=sm["conv_ln_b"], w_pw2=whole["conv_w_pw2"][0],
             b_pw2=sm["conv_b_pw2"], w_up_t=whole["ffn_w_up"], ffn_w_dw=sm["ffn_w_dw"],
             ffn_b_dw=[ffn_b_dw[0:1], ffn_b_dw[1:2]], w_down=whole["ffn_w_down"])

    loss, grad_x, g = _local_step(x[0], positions.reshape(-1, 1), loss_target[0], p)
    loss = lax.psum(loss[0, 0], ("x", "y", "c"))

    gsmall = dict(norm_g=g["norm_g"].reshape(norm_g.shape[0], 4, -1), conv_b_pw1=g["b_pw1"], conv_w_dw=g["conv_w_dw"][None],
                  conv_b_dw=g["conv_b_dw"], conv_ln_g=g["ln_g"], conv_ln_b=g["ln_b"], conv_b_pw2=g["b_pw2"], ffn_w_dw=g["ffn_w_dw"])
    small_contrib = jnp.concatenate([_split_shards(gsmall[n], SMALL_AXIS[n]).reshape(N_DEV, -1) for n in SMALL], axis=1)
    srows = small.shape[1]
    small_contrib = jnp.pad(small_contrib, ((0, 0), (0, srows * 128 - small_contrib.shape[1]))).reshape(1, N_DEV, srows, 128)
    big_names = ("w_qkv_t", "w_o_t", "w_pw1_t", "w_up_t", "w_pw2", "w_down")
    contribs = [g[n].reshape(g[n].shape[0], N_DEV, g[n].shape[1] // N_DEV, g[n].shape[2]) for n in big_names] + [small_contrib]
    core = lax.axis_index("c").astype(jnp.int32).reshape(1)
    got = _rs_sibling(contribs)
    chip_sums = _rs_chips([_rs_pair_add(c_, g_, core) for c_, g_ in zip(contribs, got)])

    outs = {}
    for n, gname, transposed in (("attn_w_qkv", "w_qkv_t", True), ("attn_w_o", "w_o_t", True), ("conv_w_pw1", "w_pw1_t", True),
                                 ("ffn_w_up", "w_up_t", True), ("conv_w_pw2", "w_pw2", False), ("ffn_w_down", "w_down", False)):
        gsum = _sum_parts(chip_sums[big_names.index(gname)], "sum_chips")
        gsum = jnp.swapaxes(gsum, 1, 2) if transposed else gsum
        outs[n] = (gsum, *_adamw(gsum, w[n], m[n], v[n], "adamw"))
    sshapes = [w[n].shape for n in SMALL]
    souts = _sum_adamw(chip_sums[-1][0], *[_pack([t[n] for n in SMALL], 128, 8) for t in (w, m, v)], name="sum_adamw_small")
    for n, vals in zip(SMALL, zip(*[_unpack(o, sshapes) for o in souts])):
        outs[n] = vals
    bparts = _all_gather(_pack([g["ffn_b_dw"]], 128, 8), "gather_bias_grads")
    bouts = _sum_adamw(bparts, *[_pack([t], 128, 8) for t in (ffn_b_dw, m_ffn_b_dw, v_ffn_b_dw)], name="sum_adamw_bias")
    outs["ffn_b_dw"] = tuple(_unpack(o, [ffn_b_dw.shape])[0] for o in bouts)

    order = ("norm_g", "attn_w_qkv", "attn_w_o", "conv_w_pw1", "conv_b_pw1", "conv_w_dw", "conv_b_dw", "conv_ln_g",
             "conv_ln_b", "conv_w_pw2", "conv_b_pw2", "ffn_w_up", "ffn_w_dw", "ffn_b_dw", "ffn_w_down")
    return (loss, grad_x[None], *[outs[n][0] for n in order], *[outs[n][1] for n in order],
            *[outs[n][2] for n in order], *[outs[n][3] for n in order])
```

```python
import functools
import math

import numpy as np
import jax
import jax.numpy as jnp
from jax import lax
from jax.experimental import pallas as pl
from jax.experimental.pallas import tpu as pltpu

F32 = jnp.float32
BF16 = jnp.bfloat16
EPS = 1e-6
N_DEV = 8
HEAD_DIM = 64
GROUP_WIDTH = 512
DILATIONS = (1, 4, 16)
SPAN = 128
ROT_DIM = 16
ROPE_THETA = 500000.0
CONV_KERNEL = 31
CONV_HALO = 32
FFN_CONV = 3
ADAM_LR, ADAM_B1, ADAM_B2, ADAM_EPS, ADAM_WD, ADAM_STEP = 0.001, 0.9, 0.999, 1e-08, 0.01, 10
VMEM_LIMIT_BYTES = 56 * 1024 * 1024
MESH = pl.DeviceIdType.MESH
ANY = pl.BlockSpec(memory_space=pl.ANY)
NT = (((1,), (1,)), ((), ()))
TN = (((0,), (0,)), ((), ()))


def _params(*sem):
    return pltpu.CompilerParams(dimension_semantics=sem, vmem_limit_bytes=VMEM_LIMIT_BYTES)


def _sigmoid(v):
    return 1.0 / (1.0 + jnp.exp(-v))


def _full(shape):
    return pl.BlockSpec(shape, lambda *_: (0,) * len(shape))


def _rows(tm, width):
    return pl.BlockSpec((tm, width), lambda i, *_: (i, 0))


def _tile(n, *multiples_of):
    for t in (1408, 1024, 512, 384, 256, 128):
        if n % t == 0 and all(o % t == 0 for o in multiples_of):
            return t
    raise ValueError((n, multiples_of))


def _all_gather(shard, name):
    r, c_ = shard.shape

    def body(x_ref, out_ref, send_sems, recv_sems, local_sem):
        x, y, c = lax.axis_index("x"), lax.axis_index("y"), lax.axis_index("c")
        me, sibling = (x, y, c), (x, y, 1 - c)
        chips = [(1 - x, y), (x, 1 - y), (1 - x, 1 - y)]

        def rows(px, py, pc):
            return out_ref.at[4 * px + 2 * py + pc]

        def copy(k, block, to, src=None):
            return pltpu.make_async_remote_copy(
                src_ref=rows(*block) if src is None else src, dst_ref=rows(*block),
                send_sem=send_sems.at[k], recv_sem=recv_sems.at[k], device_id=to, device_id_type=MESH)

        mine = pltpu.make_async_copy(x_ref, rows(*me), local_sem)
        mine.start()
        first = [copy(0, me, sibling, src=x_ref)]
        first += [copy(1 + j, me, (*chip, c), src=x_ref) for j, chip in enumerate(chips)]
        for cp in first:
            cp.start()
        passed = [copy(4 + j, (*chip, c), sibling) for j, chip in enumerate(chips)]
        for j, chip in enumerate(chips):
            copy(1 + j, (*chip, c), me).wait_recv()
            passed[j].start()
        copy(0, sibling, me).wait_recv()
        for j, chip in enumerate(chips):
            copy(4 + j, (*chip, 1 - c), me).wait_recv()
        for cp in first + passed:
            cp.wait_send()
        mine.wait()

    return pl.pallas_call(
        body, name=name, out_shape=jax.ShapeDtypeStruct((N_DEV, r, c_), shard.dtype),
        in_specs=[ANY], out_specs=ANY,
        scratch_shapes=[pltpu.SemaphoreType.DMA((7,)), pltpu.SemaphoreType.DMA((7,)), pltpu.SemaphoreType.DMA],
    )(shard)


def _with_rows(g, n):
    return jax.ShapeDtypeStruct((g.shape[0], n) + tuple(g.shape[2:]), g.dtype)


def _rs_sibling(gs):
    n = len(gs)

    def body(*refs):
        g_refs, o_refs, (send_sems, recv_sems) = refs[:n], refs[n:2 * n], refs[2 * n:]
        x, y, c = lax.axis_index("x"), lax.axis_index("y"), lax.axis_index("c")
        copies = [pltpu.make_async_remote_copy(
            src_ref=g_refs[w].at[:, 2 * q + (1 - c)], dst_ref=o_refs[w].at[:, q], send_sem=send_sems.at[4 * w + q],
            recv_sem=recv_sems.at[4 * w + q], device_id=(x, y, 1 - c), device_id_type=MESH)
            for w in range(n) for q in range(4)]
        for cp in copies:
            cp.start()
        for cp in copies:
            cp.wait_recv()
        for cp in copies:
            cp.wait_send()

    return pl.pallas_call(
        body, name="rs_sibling", out_shape=[_with_rows(g, 4) for g in gs],
        in_specs=[ANY] * n, out_specs=[ANY] * n,
        scratch_shapes=[pltpu.SemaphoreType.DMA((4 * n,)), pltpu.SemaphoreType.DMA((4 * n,))],
    )(*gs)


def _rs_pair_add(g, got, core, out_dtype):
    l, _, r, c_ = g.shape

    def body(core_ref, g_ref, got_ref, o_ref):
        o_ref[...] = (g_ref[...] + got_ref[...]).astype(out_dtype)

    blk = (None, None, r, c_)
    return pl.pallas_call(
        body, name="rs_pair_add", out_shape=jax.ShapeDtypeStruct((l, 4, r, c_), out_dtype),
        grid_spec=pltpu.PrefetchScalarGridSpec(
            num_scalar_prefetch=1, grid=(l, 4),
            in_specs=[pl.BlockSpec(blk, lambda i, q, core_ref: (i, 2 * q + core_ref[0], 0, 0)),
                      pl.BlockSpec(blk, lambda i, q, core_ref: (i, q, 0, 0))],
            out_specs=pl.BlockSpec(blk, lambda i, q, core_ref: (i, q, 0, 0))),
        compiler_params=_params("parallel", "parallel"),
    )(core, g, got)


def _rs_chips(parts):
    n = len(parts)

    def body(*refs):
        p_refs, o_refs, (send_sems, recv_sems, local_sems) = refs[:n], refs[n:2 * n], refs[2 * n:]
        x, y, c = lax.axis_index("x"), lax.axis_index("y"), lax.axis_index("c")
        my_chip = 2 * x + y
        chips = [(1 - x, y), (x, 1 - y), (1 - x, 1 - y)]
        local = [pltpu.make_async_copy(p_refs[w].at[:, my_chip], o_refs[w].at[:, my_chip], local_sems.at[w]) for w in range(n)]
        for cp in local:
            cp.start()
        copies = [pltpu.make_async_remote_copy(
            src_ref=p_refs[w].at[:, 2 * qx + qy], dst_ref=o_refs[w].at[:, my_chip], send_sem=send_sems.at[3 * w + k],
            recv_sem=recv_sems.at[3 * w + k], device_id=(qx, qy, c), device_id_type=MESH)
            for w in range(n) for k, (qx, qy) in enumerate(chips)]
        for cp in copies:
            cp.start()
        for cp in copies:
            cp.wait_recv()
        for cp in copies:
            cp.wait_send()
        for cp in local:
            cp.wait()

    return pl.pallas_call(
        body, name="rs_chips", out_shape=[jax.ShapeDtypeStruct(p.shape, p.dtype) for p in parts],
        in_specs=[ANY] * n, out_specs=[ANY] * n,
        scratch_shapes=[pltpu.SemaphoreType.DMA((3 * n,)), pltpu.SemaphoreType.DMA((3 * n,)), pltpu.SemaphoreType.DMA((n,))],
    )(*parts)


def _sum_parts(parts, name):
    l, n, r, c_ = parts.shape

    def body(p_ref, o_ref):
        g = p_ref[0].astype(F32)
        for s in range(1, n):
            g = g + p_ref[s].astype(F32)
        o_ref[...] = g

    return pl.pallas_call(
        body, name=name, out_shape=jax.ShapeDtypeStruct((l, r, c_), F32), grid=(l,),
        in_specs=[pl.BlockSpec((None, n, r, c_), lambda i: (i, 0, 0, 0))],
        out_specs=pl.BlockSpec((None, r, c_), lambda i: (i, 0, 0)), compiler_params=_params("parallel"),
    )(parts)


def _adamw_math(w, g, m, v):
    m = ADAM_B1 * m + (1.0 - ADAM_B1) * g
    v = ADAM_B2 * v + (1.0 - ADAM_B2) * (g * g)
    m_hat = m / (1.0 - ADAM_B1 ** ADAM_STEP)
    v_hat = v / (1.0 - ADAM_B2 ** ADAM_STEP)
    delta = -ADAM_LR * (m_hat / (jnp.sqrt(v_hat) + ADAM_EPS) + ADAM_WD * w)
    return delta, m, v


def _adamw(g, w, m, v, name):
    l, k, n = w.shape
    tk = 256 if k % 256 == 0 else k

    def body(g_ref, w_ref, m_ref, v_ref, d_ref, nm_ref, nv_ref):
        d_ref[...], nm_ref[...], nv_ref[...] = _adamw_math(w_ref[...], g_ref[...], m_ref[...], v_ref[...])

    spec = pl.BlockSpec((None, tk, n), lambda i, j: (i, j, 0))
    return pl.pallas_call(
        body, name=name, out_shape=[jax.ShapeDtypeStruct((l, k, n), F32)] * 3, grid=(l, k // tk),
        in_specs=[spec] * 4, out_specs=[spec] * 3, compiler_params=_params("parallel", "parallel"),
    )(g, w, m, v)


def _sum_adamw(parts, w, m, v, name):
    n, r, c_ = parts.shape

    def body(p_ref, w_ref, m_ref, v_ref, g_ref, d_ref, nm_ref, nv_ref):
        g = p_ref[0]
        for s in range(1, n):
            g = g + p_ref[s]
        g_ref[...] = g
        d_ref[...], nm_ref[...], nv_ref[...] = _adamw_math(w_ref[...], g, m_ref[...], v_ref[...])

    return pl.pallas_call(
        body, name=name, out_shape=[jax.ShapeDtypeStruct((r, c_), F32)] * 4, grid=(1,),
        in_specs=[_full((n, r, c_))] + [_full((r, c_))] * 3, out_specs=[_full((r, c_))] * 4,
        compiler_params=_params("arbitrary"),
    )(parts, w, m, v)


def _rope_tables(pos_col, freq_row):
    s = pos_col.shape[0]
    tm = min(1024, s)

    def body(p_ref, f_ref, c_ref, su_ref, sd_ref):
        ang = p_ref[...].astype(F32) * f_ref[...]
        lane = lax.broadcasted_iota(jnp.int32, ang.shape, 1) & (HEAD_DIM - 1)
        cs, sn = jnp.cos(ang), jnp.sin(ang)
        c_ref[...] = jnp.where(lane < ROT_DIM, cs, 1.0)
        su_ref[...] = jnp.where((lane >= ROT_DIM // 2) & (lane < ROT_DIM), sn, 0.0)
        sd_ref[...] = jnp.where(lane < ROT_DIM // 2, -sn, 0.0)

    return pl.pallas_call(
        body, name="rope_tables", out_shape=[jax.ShapeDtypeStruct((s, 128), F32)] * 3, grid=(s // tm,),
        in_specs=[pl.BlockSpec((tm, 1), lambda i: (i, 0)), _full((1, 128))],
        out_specs=[_rows(tm, 128)] * 3, compiler_params=_params("parallel"),
    )(pos_col, freq_row)


def _rope_apply(t, cos, sin_up, sin_dn):
    w = t.shape[1]
    return t * cos + pltpu.roll(t, 8, 1) * sin_up + pltpu.roll(t, w - 8, 1) * sin_dn


def _rope_transpose(dr, cos, sin_up, sin_dn):
    w = dr.shape[1]
    return dr * cos + pltpu.roll(dr * sin_up, w - 8, 1) + pltpu.roll(dr * sin_dn, 8, 1)


def _norm_matmul(x, g, wt, *, tn, name, bias=None, rope=None, rope_blocks=0, tm=512):
    s, d = x.shape
    n = wt.shape[0]
    tm = min(tm, s)

    def body(*refs):
        x_ref, g_ref, w_ref = refs[:3]
        k = 3
        b_ref = None
        if bias is not None:
            b_ref = refs[k]
            k += 1
        if rope is not None:
            c_ref, su_ref, sd_ref = refs[k:k + 3]
            k += 3
        h_ref, o_ref = refs[k:k + 2]
        j = pl.program_id(1)

        @pl.when(j == 0)
        def _():
            xv = x_ref[...]
            r = lax.rsqrt(jnp.mean(xv * xv, axis=-1, keepdims=True) + EPS)
            h_ref[...] = (xv * r * g_ref[...]).astype(BF16)

        acc = lax.dot_general(h_ref[...], w_ref[...], NT, preferred_element_type=F32)
        if b_ref is not None:
            acc = acc + b_ref[...]
        if rope is None:
            o_ref[...] = acc.astype(BF16)
        else:
            @pl.when(j < rope_blocks)
            def _():
                reps = tn // 128
                o_ref[...] = _rope_apply(acc, jnp.tile(c_ref[...], (1, reps)), jnp.tile(su_ref[...], (1, reps)),
                                         jnp.tile(sd_ref[...], (1, reps))).astype(BF16)

            @pl.when(j >= rope_blocks)
            def _():
                o_ref[...] = acc.astype(BF16)

    in_specs = [_rows(tm, d), _full((1, d)), pl.BlockSpec((tn, d), lambda i, j: (j, 0))]
    args = [x, g, wt]
    if bias is not None:
        in_specs.append(pl.BlockSpec((1, tn), lambda i, j: (0, j)))
        args.append(bias)
    if rope is not None:
        in_specs += [_rows(tm, 128)] * 3
        args += list(rope)
    return pl.pallas_call(
        body, name=name,
        out_shape=[jax.ShapeDtypeStruct((s, d), BF16), jax.ShapeDtypeStruct((s, n), BF16)],
        grid=(s // tm, n // tn), in_specs=in_specs,
        out_specs=[_rows(tm, d), pl.BlockSpec((tm, tn), lambda i, j: (i, j))],
        compiler_params=_params("parallel", "arbitrary"),
    )(*args)


def _head_masks(rows=SPAN):
    lane = lax.broadcasted_iota(jnp.int32, (rows, 128), 1)
    masks = [lane < HEAD_DIM, lane >= HEAD_DIM]
    lane1 = lax.broadcasted_iota(jnp.int32, (1, 128), 1)
    keep = [jnp.where(lane1 < HEAD_DIM, 1.0, 0.0).astype(BF16), jnp.where(lane1 >= HEAD_DIM, 1.0, 0.0).astype(BF16)]
    return masks, keep


def _attn_fwd(qkv, grp, dil):
    s, ncol = qkv.shape
    nblk = ncol // GROUP_WIDTH
    l = s // dil
    nb = l // SPAN
    qv = qkv.reshape(l, dil * ncol)

    def body(q_ref, kp_ref, kc_ref, vp_ref, vc_ref, o_ref, l_ref):
        b = pl.program_id(1)
        row = lax.broadcasted_iota(jnp.int32, (SPAN, 2 * SPAN), 0)
        col = lax.broadcasted_iota(jnp.int32, (SPAN, 2 * SPAN), 1)
        no_prev = jnp.where(b > 0, 0, 4 * SPAN)
        valid = ((col < SPAN) & (col >= row + no_prev)) | ((col >= SPAN) & (col - SPAN <= row))
        masks, keep = _head_masks()
        for p in range(GROUP_WIDTH // 128):
            sl = slice(p * 128, (p + 1) * 128)
            qp = q_ref[:, sl]
            kk = jnp.concatenate([kp_ref[:, sl], kc_ref[:, sl]], axis=0)
            vv = jnp.concatenate([vp_ref[:, sl], vc_ref[:, sl]], axis=0)
            outs, lses = [], []
            for h in range(2):
                sc = lax.dot_general(qp * keep[h], kk, NT, preferred_element_type=F32) * (HEAD_DIM ** -0.5)
                sc = jnp.where(valid, sc, -1e30)
                mx = jnp.max(sc, axis=-1, keepdims=True)
                pe = jnp.exp(sc - mx)
                den = jnp.sum(pe, axis=-1, keepdims=True)
                pv = jnp.dot(pe.astype(BF16), vv, preferred_element_type=F32)
                outs.append(pv / den)
                lses.append(jnp.broadcast_to(mx + jnp.log(den), (SPAN, 128)))
            o_ref[:, sl] = jnp.where(masks[0], outs[0], outs[1])
            l_ref[:, sl] = jnp.where(masks[0], lses[0], lses[1])

    blk = (SPAN, GROUP_WIDTH)
    cur = lambda t: pl.BlockSpec(blk, lambda r, b: (b, r * nblk + 3 * t + grp))
    prev = lambda t: pl.BlockSpec(blk, lambda r, b: (jnp.maximum(b - 1, 0), r * nblk + 3 * t + grp))
    out = pl.BlockSpec(blk, lambda r, b: (b, r))
    o, lse = pl.pallas_call(
        body, name=f"attn_fwd_g{grp}", out_shape=[jax.ShapeDtypeStruct((l, dil * GROUP_WIDTH), F32)] * 2,
        grid=(dil, nb), in_specs=[cur(0), prev(1), cur(1), prev(2), cur(2)], out_specs=[out, out],
        compiler_params=_params("parallel", "arbitrary"),
    )(qv, qv, qv, qv, qv)
    return o.reshape(s, GROUP_WIDTH), lse.reshape(s, GROUP_WIDTH)


def _resnorm_store(y, x_ref, g_ref, y_ref, xo_ref):
    r = lax.rsqrt(jnp.mean(y * y, axis=-1, keepdims=True) + EPS)
    y_ref[...] = y
    xo_ref[...] = x_ref[...] + y * r * g_ref[...]


def _mix_wo(os_, ls_, wot, x, g, tm=256):
    s, d = x.shape
    gw = wot.shape[1]
    tm = min(tm, s)

    def body(o0, o1, o2, l0, l1, l2, w_ref, x_ref, g_ref, y_ref, xo_ref, mixed_ref, lse_ref):
        a0, a1, a2 = l0[...], l1[...], l2[...]
        mx = jnp.maximum(jnp.maximum(a0, a1), a2)
        e0, e1, e2 = jnp.exp(a0 - mx), jnp.exp(a1 - mx), jnp.exp(a2 - mx)
        den = e0 + e1 + e2
        mixed = (e0 / den) * o0[...] + (e1 / den) * o1[...] + (e2 / den) * o2[...]
        mixed_ref[...] = mixed.astype(BF16)
        lse_ref[...] = mx + jnp.log(den)
        y = lax.dot_general(mixed.astype(BF16), w_ref[...], NT, preferred_element_type=F32)
        _resnorm_store(y, x_ref, g_ref, y_ref, xo_ref)

    return pl.pallas_call(
        body, name="mix_wo",
        out_shape=[jax.ShapeDtypeStruct((s, d), F32), jax.ShapeDtypeStruct((s, d), F32),
                   jax.ShapeDtypeStruct((s, gw), BF16), jax.ShapeDtypeStruct((s, gw), F32)],
        grid=(s // tm,), in_specs=[_rows(tm, gw)] * 6 + [_full((d, gw)), _rows(tm, d), _full((1, d))],
        out_specs=[_rows(tm, d), _rows(tm, d), _rows(tm, gw), _rows(tm, gw)],
        compiler_params=_params("parallel"),
    )(*os_, *ls_, wot, x, g)


def _matmul_resnorm(a, w, x, g, *, name, bias=None, tm=512):
    s, k = a.shape
    d = w.shape[1]
    tm = min(tm, s)

    def body(*refs):
        a_ref, w_ref = refs[:2]
        b_ref = refs[2] if bias is not None else None
        x_ref, g_ref, y_ref, xo_ref = refs[-4:]
        y = jnp.dot(a_ref[...], w_ref[...], preferred_element_type=F32)
        if b_ref is not None:
            y = y + b_ref[...]
        _resnorm_store(y, x_ref, g_ref, y_ref, xo_ref)

    in_specs = [_rows(tm, k), _full((k, d))] + ([_full((1, d))] if bias is not None else []) + [_rows(tm, d), _full((1, d))]
    args = [a, w] + ([bias] if bias is not None else []) + [x, g]
    return pl.pallas_call(
        body, name=name, out_shape=[jax.ShapeDtypeStruct((s, d), F32)] * 2, grid=(s // tm,),
        in_specs=in_specs, out_specs=[_rows(tm, d)] * 2, compiler_params=_params("parallel"),
    )(*args)


def _conv3_taps(z, halo, first):
    row = lax.broadcasted_iota(jnp.int32, z.shape, 0)
    halo = halo * jnp.where(first, 0.0, 1.0)
    h6, h7 = halo[6:7, :], halo[7:8, :]
    z1 = jnp.where(row == 0, h7, pltpu.roll(z, 1, 0))
    z2 = jnp.where(row == 0, h6, jnp.where(row == 1, h7, pltpu.roll(z, 2, 0)))
    return z2, z1


def _ffn_cols(f):
    return _tile(f)


def _lane_chunks(width, fn):
    def step(k, carry):
        fn(pl.ds(pl.multiple_of(k * 128, 128), 128))
        return carry

    lax.fori_loop(0, width // 128, step, 0)


def _ffn_act(z, w_dw, b_dw, tm=256):
    s, f2 = z.shape
    f = f2 // 2
    tm = min(tm, s)
    tc = _ffn_cols(f)
    nfc = f // tc

    def body(zu, zg, hu, hg, wu, wg, bu, bg, o_ref):
        first = pl.program_id(0) == 0

        def chunk(cs):
            def conv(z_ref, h_ref, w_ref, b_ref):
                zc = z_ref[:, cs].astype(F32)
                z2, z1 = _conv3_taps(zc, h_ref[:, cs].astype(F32), first)
                return w_ref[0:1, cs] * z2 + w_ref[1:2, cs] * z1 + w_ref[2:3, cs] * zc + b_ref[:, cs]

            up, gate = conv(zu, hu, wu, bu), conv(zg, hg, wg, bg)
            o_ref[:, cs] = (gate * _sigmoid(gate) * up).astype(BF16)

        _lane_chunks(tc, chunk)

    hb = tm // 8
    tile = lambda off: pl.BlockSpec((tm, tc), lambda i, j: (i, off + j))
    halo = lambda off: pl.BlockSpec((8, tc), lambda i, j: (jnp.maximum(i * hb - 1, 0), off + j))
    prm = lambda rows, off: pl.BlockSpec((rows, tc), lambda i, j: (0, off + j))
    return pl.pallas_call(
        body, name="ffn_act", out_shape=jax.ShapeDtypeStruct((s, f), BF16), grid=(s // tm, nfc),
        in_specs=[tile(0), tile(nfc), halo(0), halo(nfc), prm(FFN_CONV, 0), prm(FFN_CONV, nfc), prm(1, 0), prm(1, nfc)],
        out_specs=pl.BlockSpec((tm, tc), lambda i, j: (i, j)), compiler_params=_params("parallel", "parallel"),
    )(z, z, z, z, w_dw, w_dw, b_dw, b_dw)


def _shifted_planes(ext_ref):
    rows = ext_ref.shape[1]
    for s in range(1, 8):
        ext_ref[s, 0:rows - 8, :] = ext_ref[0, s:s + rows - 8, :]


def _window(ext_ref, off, tm, cs):
    s = off % 8
    return ext_ref[s, off - s:off - s + tm, cs]


def _conv_taps(ext_ref, w_ref, offs, tm, out_ref):
    def chunk(cs):
        acc = w_ref[0:1, cs] * _window(ext_ref, offs[0], tm, cs)
        for j in range(1, len(offs)):
            acc = acc + w_ref[j:j + 1, cs] * _window(ext_ref, offs[j], tm, cs)
        out_ref[:, cs] = acc

    _lane_chunks(out_ref.shape[1], chunk)


def _glu_conv(ag_ref, halo_ref, w_ref, ext_ref, u1_ref, first, c):
    tm = ag_ref.shape[0]
    hal = halo_ref[...].astype(F32)
    ext_ref[0, 0:CONV_HALO, :] = hal[:, :c] * _sigmoid(hal[:, c:]) * jnp.where(first, 0.0, 1.0)
    ag = ag_ref[...].astype(F32)
    ext_ref[0, CONV_HALO:, :] = ag[:, :c] * _sigmoid(ag[:, c:])
    _shifted_planes(ext_ref)
    base = CONV_HALO - (CONV_KERNEL - 1)
    _conv_taps(ext_ref, w_ref, [base + j for j in range(CONV_KERNEL)], tm, u1_ref)


def _layernorm_stats(u1):
    mu = jnp.mean(u1, axis=-1, keepdims=True)
    cen = u1 - mu
    rstd = lax.rsqrt(jnp.mean(cen * cen, axis=-1, keepdims=True) + EPS)
    return cen * rstd, rstd


def _conv_mid(ag, w_dw, b_dw, ln_g, ln_b, tm=256):
    s, c2 = ag.shape
    c = c2 // 2
    tm = min(tm, s)

    def body(ag_ref, halo_ref, w_ref, b_ref, g_ref, bb_ref, o_ref, ext_ref, u1_ref):
        _glu_conv(ag_ref, halo_ref, w_ref, ext_ref, u1_ref, pl.program_id(0) == 0, c)
        xh, _ = _layernorm_stats(u1_ref[...] + b_ref[...])
        u2 = xh * g_ref[...] + bb_ref[...]
        o_ref[...] = (u2 * _sigmoid(u2)).astype(BF16)

    hb = tm // CONV_HALO
    return pl.pallas_call(
        body, name="conv_mid", out_shape=jax.ShapeDtypeStruct((s, c), BF16), grid=(s // tm,),
        in_specs=[_rows(tm, c2), pl.BlockSpec((CONV_HALO, c2), lambda i: (jnp.maximum(i * hb - 1, 0), 0)),
                  _full((CONV_KERNEL, c)), _full((1, c)), _full((1, c)), _full((1, c))],
        out_specs=_rows(tm, c), scratch_shapes=[pltpu.VMEM((8, CONV_HALO + tm, c), F32), pltpu.VMEM((tm, c), F32)],
        compiler_params=_params("arbitrary"),
    )(ag, ag, w_dw, b_dw, ln_g, ln_b)


def _loss_grad(xo, target, tm=512):
    s, d = xo.shape
    tm = min(tm, s)

    def body(x_ref, t_ref, dx_ref, loss_ref):
        @pl.when(pl.program_id(0) == 0)
        def _():
            loss_ref[...] = jnp.zeros_like(loss_ref)

        err = x_ref[...] - t_ref[...]
        dx_ref[...] = err * (1.0 / d)
        loss_ref[...] += 0.5 * jnp.sum(jnp.mean(err * err, axis=-1, keepdims=True))

    return pl.pallas_call(
        body, name="loss_grad", out_shape=[jax.ShapeDtypeStruct((s, d), F32), jax.ShapeDtypeStruct((1, 128), F32)],
        grid=(s // tm,), in_specs=[_rows(tm, d)] * 2, out_specs=[_rows(tm, d), _full((1, 128))],
        compiler_params=_params("arbitrary"),
    )(xo, target)


def _postnorm_bwd(y, g, dxo, *, name, with_bias_grad=False, tm=512):
    s, d = y.shape
    tm = min(tm, s)

    def body(y_ref, g_ref, dx_ref, dy_ref, dg_ref, *rest):
        @pl.when(pl.program_id(0) == 0)
        def _():
            dg_ref[...] = jnp.zeros_like(dg_ref)
            for r_ in rest:
                r_[...] = jnp.zeros_like(r_)

        yv, dxo_v = y_ref[...], dx_ref[...]
        r = lax.rsqrt(jnp.mean(yv * yv, axis=-1, keepdims=True) + EPS)
        yh = yv * r
        dyh = dxo_v * g_ref[...]
        dy = r * (dyh - yh * jnp.mean(dyh * yh, axis=-1, keepdims=True))
        dy_ref[...] = dy.astype(BF16)
        dg_ref[...] += jnp.sum(dxo_v * yh, axis=0, keepdims=True)
        for r_ in rest:
            r_[...] += jnp.sum(dy, axis=0, keepdims=True)

    nacc = 2 if with_bias_grad else 1
    return pl.pallas_call(
        body, name=name, out_shape=[jax.ShapeDtypeStruct((s, d), BF16)] + [jax.ShapeDtypeStruct((1, d), F32)] * nacc,
        grid=(s // tm,), in_specs=[_rows(tm, d), _full((1, d)), _rows(tm, d)],
        out_specs=[_rows(tm, d)] + [_full((1, d))] * nacc, compiler_params=_params("arbitrary"),
    )(y, g, dxo)


def _matmul(gmat, w, *, name, out_dtype, transposed_w, tm=512):
    s, k = gmat.shape
    n = w.shape[0] if transposed_w else w.shape[1]
    tm = min(tm, s)

    def body(g_ref, w_ref, o_ref):
        if transposed_w:
            acc = lax.dot_general(g_ref[...], w_ref[...], NT, preferred_element_type=F32)
        else:
            acc = jnp.dot(g_ref[...], w_ref[...], preferred_element_type=F32)
        o_ref[...] = acc.astype(out_dtype)

    return pl.pallas_call(
        body, name=name, out_shape=jax.ShapeDtypeStruct((s, n), out_dtype), grid=(s // tm,),
        in_specs=[_rows(tm, k), _full(w.shape)], out_specs=_rows(tm, n), compiler_params=_params("parallel"),
    )(gmat, w)


def _matmul_prenorm_bwd(pieces, wt, x, g, dres, *, name, tm=256):
    s, d = x.shape
    tm = min(tm, s)
    np_ = len(pieces)

    def body(*refs):
        p_refs, w_refs = refs[:np_], refs[np_:2 * np_]
        x_ref, g_ref, r_ref, dx_ref, dg_ref = refs[2 * np_:]

        @pl.when(pl.program_id(0) == 0)
        def _():
            dg_ref[...] = jnp.zeros_like(dg_ref)

        dh = None
        for p_ref, w_ref in zip(p_refs, w_refs):
            t = jnp.dot(p_ref[...], w_ref[...], preferred_element_type=F32)
            dh = t if dh is None else dh + t
        xv = x_ref[...]
        r = lax.rsqrt(jnp.mean(xv * xv, axis=-1, keepdims=True) + EPS)
        xh = xv * r
        dyh = dh * g_ref[...]
        dx_ref[...] = r_ref[...] + r * (dyh - xh * jnp.mean(dyh * xh, axis=-1, keepdims=True))
        dg_ref[...] += jnp.sum(dh * xh, axis=0, keepdims=True)

    in_specs = []
    for _, c0, kc, _ in pieces:
        assert c0 % kc == 0
        in_specs.append(pl.BlockSpec((tm, kc), lambda i, _b=c0 // kc: (i, _b)))
    for _, _, kc, r0 in pieces:
        assert r0 % kc == 0
        in_specs.append(pl.BlockSpec((kc, d), lambda i, _b=r0 // kc: (_b, 0)))
    in_specs += [_rows(tm, d), _full((1, d)), _rows(tm, d)]
    return pl.pallas_call(
        body, name=name, out_shape=[jax.ShapeDtypeStruct((s, d), F32), jax.ShapeDtypeStruct((1, d), F32)],
        grid=(s // tm,), in_specs=in_specs, out_specs=[_rows(tm, d), _full((1, d))],
        compiler_params=_params("arbitrary"),
    )(*[p[0] for p in pieces], *[wt] * np_, x, g, dres)


def _weight_grad(a, gmat, *, name, a_col0=0, ka=None, out=None, out_shape=None, layer=0, row0=0, ts=512):
    s = a.shape[0]
    ka = a.shape[1] if ka is None else ka
    n = gmat.shape[1]
    ts = min(ts, s)
    tka = _tile(ka, a_col0, row0)
    shape = out.shape if out is not None else out_shape

    def body(a_ref, g_ref, *rest):
        o_ref = rest[-1]

        @pl.when(pl.program_id(1) == 0)
        def _():
            o_ref[...] = jnp.zeros_like(o_ref)

        o_ref[...] += lax.dot_general(a_ref[...], g_ref[...], TN, preferred_element_type=F32)

    in_specs = [pl.BlockSpec((ts, tka), lambda k, i: (i, a_col0 // tka + k)), pl.BlockSpec((ts, n), lambda k, i: (i, 0))]
    args = [a, gmat]
    aliases = {}
    if out is not None:
        in_specs.append(ANY)
        args.append(out)
        aliases = {2: 0}
    return pl.pallas_call(
        body, name=name, out_shape=jax.ShapeDtypeStruct(shape, F32), grid=(ka // tka, s // ts), in_specs=in_specs,
        out_specs=pl.BlockSpec((None, tka, n), lambda k, i: (layer, row0 // tka + k, 0)),
        input_output_aliases=aliases, compiler_params=_params("parallel", "arbitrary"),
    )(*args)


def _ffn_act_bwd(z, dact, w_dw, b_dw, tm=256):
    s, f2 = z.shape
    f = f2 // 2
    tm = min(tm, s)
    tc = _ffn_cols(f)
    nfc = f // tc

    def body(zu, zg, hu, hg, wu, wg, bu, bg, da_ref, du_ref, dgt_ref, dbu_ref, dbg_ref, dwu_ref, dwg_ref):
        i = pl.program_id(1)

        @pl.when(i == 0)
        def _():
            for r_ in (dbu_ref, dbg_ref, dwu_ref, dwg_ref):
                r_[...] = jnp.zeros_like(r_)

        def chunk(cs):
            def conv(z_ref, h_ref, w_ref, b_ref):
                zc = z_ref[:, cs].astype(F32)
                z2, z1 = _conv3_taps(zc, h_ref[:, cs].astype(F32), i == 0)
                return (z2, z1, zc), w_ref[0:1, cs] * z2 + w_ref[1:2, cs] * z1 + w_ref[2:3, cs] * zc + b_ref[:, cs]

            taps_u, up = conv(zu, hu, wu, bu)
            taps_g, gate = conv(zg, hg, wg, bg)
            da = da_ref[:, cs].astype(F32)
            sg = _sigmoid(gate)
            d_up = da * (gate * sg)
            d_gate = da * up * (sg * (1.0 + gate * (1.0 - sg)))
            du_ref[:, cs] = d_up.astype(BF16)
            dgt_ref[:, cs] = d_gate.astype(BF16)
            for dv, taps, db_ref, dw_ref in ((d_up, taps_u, dbu_ref, dwu_ref), (d_gate, taps_g, dbg_ref, dwg_ref)):
                db_ref[:, cs] += jnp.sum(dv, axis=0, keepdims=True)
                for k_, tap in enumerate(taps):
                    dw_ref[k_:k_ + 1, cs] += jnp.sum(dv * tap, axis=0, keepdims=True)

        _lane_chunks(tc, chunk)

    hb = tm // 8
    tile = lambda off: pl.BlockSpec((tm, tc), lambda j, i: (i, off + j))
    halo = lambda off: pl.BlockSpec((8, tc), lambda j, i: (jnp.maximum(i * hb - 1, 0), off + j))
    prm = lambda rows, off: pl.BlockSpec((rows, tc), lambda j, i: (0, off + j))
    acc = lambda rows: pl.BlockSpec((rows, tc), lambda j, i: (0, j))
    return pl.pallas_call(
        body, name="ffn_act_bwd",
        out_shape=[jax.ShapeDtypeStruct((s, f), BF16)] * 2 + [jax.ShapeDtypeStruct((1, f), F32)] * 2
        + [jax.ShapeDtypeStruct((FFN_CONV, f), F32)] * 2,
        grid=(nfc, s // tm),
        in_specs=[tile(0), tile(nfc), halo(0), halo(nfc), prm(FFN_CONV, 0), prm(FFN_CONV, nfc), prm(1, 0), prm(1, nfc), tile(0)],
        out_specs=[tile(0), tile(0), acc(1), acc(1), acc(FFN_CONV), acc(FFN_CONV)],
        compiler_params=_params("parallel", "arbitrary"),
    )(z, z, z, z, w_dw, w_dw, b_dw, b_dw, dact)


def _conv3_transpose(dug, w_dw, col0, tm=256):
    s, f = dug.shape
    tm = min(tm, s)
    tc = _ffn_cols(f)
    nfc = f // tc
    nrow = s // tm
    off = col0 // tc

    def body(d_ref, n_ref, w_ref, o_ref):
        keep_next = jnp.where(pl.program_id(0) == nrow - 1, 0.0, 1.0)

        def chunk(cs):
            dv = d_ref[:, cs].astype(F32)
            nxt = n_ref[:, cs].astype(F32) * keep_next
            n0, n1 = nxt[0:1, :], nxt[1:2, :]
            row = lax.broadcasted_iota(jnp.int32, dv.shape, 0)
            d1 = jnp.where(row == tm - 1, n0, pltpu.roll(dv, tm - 1, 0))
            d2 = jnp.where(row == tm - 1, n1, jnp.where(row == tm - 2, n0, pltpu.roll(dv, tm - 2, 0)))
            o_ref[:, cs] = (w_ref[2:3, cs] * dv + w_ref[1:2, cs] * d1 + w_ref[0:1, cs] * d2).astype(BF16)

        _lane_chunks(tc, chunk)

    hb = tm // 8
    return pl.pallas_call(
        body, name="conv3_transpose", out_shape=jax.ShapeDtypeStruct((s, f), BF16), grid=(nrow, nfc),
        in_specs=[pl.BlockSpec((tm, tc), lambda i, j: (i, j)),
                  pl.BlockSpec((8, tc), lambda i, j: (jnp.minimum((i + 1) * hb, s // 8 - 1), j)),
                  pl.BlockSpec((FFN_CONV, tc), lambda i, j: (0, off + j))],
        out_specs=pl.BlockSpec((tm, tc), lambda i, j: (i, j)), compiler_params=_params("parallel", "parallel"),
    )(dug, dug, w_dw)


def _conv_mid_bwd(ag, du3, w_dw, b_dw, ln_g, ln_b, tm=256):
    s, c2 = ag.shape
    c = c2 // 2
    tm = min(tm, s)

    def body(ag_ref, halo_ref, du_ref, w_ref, b_ref, g_ref, bb_ref, o_ref, dlg_ref, dlb_ref, db_ref, dw_ref, ext_ref, u1_ref):
        @pl.when(pl.program_id(0) == 0)
        def _():
            for r_ in (dlg_ref, dlb_ref, db_ref, dw_ref):
                r_[...] = jnp.zeros_like(r_)

        _glu_conv(ag_ref, halo_ref, w_ref, ext_ref, u1_ref, pl.program_id(0) == 0, c)
        xh, rstd = _layernorm_stats(u1_ref[...] + b_ref[...])
        u2 = xh * g_ref[...] + bb_ref[...]
        sg = _sigmoid(u2)
        du2 = du_ref[...] * (sg * (1.0 + u2 * (1.0 - sg)))
        dlg_ref[...] += jnp.sum(du2 * xh, axis=0, keepdims=True)
        dlb_ref[...] += jnp.sum(du2, axis=0, keepdims=True)
        dxh = du2 * g_ref[...]
        du1 = rstd * (dxh - jnp.mean(dxh, axis=-1, keepdims=True) - xh * jnp.mean(dxh * xh, axis=-1, keepdims=True))
        o_ref[...] = du1.astype(BF16)
        db_ref[...] += jnp.sum(du1, axis=0, keepdims=True)
        u1_ref[...] = du1
        base = CONV_HALO - (CONV_KERNEL - 1)

        def chunk(cs):
            dc = u1_ref[:, cs]
            for j in range(CONV_KERNEL):
                dw_ref[j:j + 1, cs] += jnp.sum(dc * _window(ext_ref, base + j, tm, cs), axis=0, keepdims=True)

        _lane_chunks(c, chunk)

    hb = tm // CONV_HALO
    vec = _full((1, c))
    return pl.pallas_call(
        body, name="conv_mid_bwd",
        out_shape=[jax.ShapeDtypeStruct((s, c), BF16)] + [jax.ShapeDtypeStruct((1, c), F32)] * 3
        + [jax.ShapeDtypeStruct((CONV_HALO, c), F32)],
        grid=(s // tm,),
        in_specs=[_rows(tm, c2), pl.BlockSpec((CONV_HALO, c2), lambda i: (jnp.maximum(i * hb - 1, 0), 0)), _rows(tm, c),
                  _full((CONV_KERNEL, c)), vec, vec, vec],
        out_specs=[_rows(tm, c), vec, vec, vec, _full((CONV_HALO, c))],
        scratch_shapes=[pltpu.VMEM((8, CONV_HALO + tm, c), F32), pltpu.VMEM((tm, c), F32)],
        compiler_params=_params("arbitrary"),
    )(ag, ag, du3, w_dw, b_dw, ln_g, ln_b)


def _glu_conv_bwd(du1, ag, w_dw, tm=256):
    s, c = du1.shape
    tm = min(tm, s)
    nrow = s // tm

    def body(d_ref, n_ref, ag_ref, w_ref, o_ref, db_ref, ext_ref, du0_ref):
        @pl.when(pl.program_id(0) == 0)
        def _():
            db_ref[...] = jnp.zeros_like(db_ref)

        ext_ref[0, 0:tm, :] = d_ref[...].astype(F32)
        ext_ref[0, tm:, :] = n_ref[...].astype(F32) * jnp.where(pl.program_id(0) == nrow - 1, 0.0, 1.0)
        _shifted_planes(ext_ref)
        top = CONV_KERNEL - 1
        _conv_taps(ext_ref, w_ref, [top - j for j in range(CONV_KERNEL)], tm, du0_ref)
        du0 = du0_ref[...]
        ag = ag_ref[...].astype(F32)
        a, gt = ag[:, :c], ag[:, c:]
        sg = _sigmoid(gt)
        da = du0 * sg
        dgt = du0 * a * (sg * (1.0 - sg))
        o_ref[:, :c] = da.astype(BF16)
        o_ref[:, c:] = dgt.astype(BF16)
        db_ref[:, :c] += jnp.sum(da, axis=0, keepdims=True)
        db_ref[:, c:] += jnp.sum(dgt, axis=0, keepdims=True)

    hb = tm // CONV_HALO
    return pl.pallas_call(
        body, name="glu_conv_bwd",
        out_shape=[jax.ShapeDtypeStruct((s, 2 * c), BF16), jax.ShapeDtypeStruct((1, 2 * c), F32)], grid=(nrow,),
        in_specs=[_rows(tm, c), pl.BlockSpec((CONV_HALO, c), lambda i: (jnp.minimum((i + 1) * hb, s // CONV_HALO - 1), 0)),
                  _rows(tm, 2 * c), _full((CONV_KERNEL, c))],
        out_specs=[_rows(tm, 2 * c), _full((1, 2 * c))],
        scratch_shapes=[pltpu.VMEM((8, tm + CONV_HALO, c), F32), pltpu.VMEM((tm, c), F32)],
        compiler_params=_params("arbitrary"),
    )(du1, du1, ag, w_dw)


def _head_rows(v, mask):
    return jnp.max(jnp.where(mask, v, -jnp.inf), axis=-1, keepdims=True)


def _attn_bwd_dq(qkv, dmix, mixed, lse, rope, grp, dil):
    s, ncol = qkv.shape
    nblk = ncol // GROUP_WIDTH
    l = s // dil
    nb = l // SPAN
    view = lambda t: t.reshape(l, dil * t.shape[1])

    def body(q_ref, kp_ref, kc_ref, vp_ref, vc_ref, do_ref, mx_ref, l_ref, c_ref, su_ref, sd_ref, o_ref):
        b = pl.program_id(1)
        row = lax.broadcasted_iota(jnp.int32, (SPAN, 2 * SPAN), 0)
        col = lax.broadcasted_iota(jnp.int32, (SPAN, 2 * SPAN), 1)
        no_prev = jnp.where(b > 0, 0, 4 * SPAN)
        valid = ((col < SPAN) & (col >= row + no_prev)) | ((col >= SPAN) & (col - SPAN <= row))
        masks, keep = _head_masks()
        for p in range(GROUP_WIDTH // 128):
            sl = slice(p * 128, (p + 1) * 128)
            qp, dop = q_ref[:, sl], do_ref[:, sl]
            kk = jnp.concatenate([kp_ref[:, sl], kc_ref[:, sl]], axis=0)
            vv = jnp.concatenate([vp_ref[:, sl], vc_ref[:, sl]], axis=0)
            prod = dop.astype(F32) * mx_ref[:, sl].astype(F32)
            lsep = l_ref[:, sl]
            dqs = []
            for h in range(2):
                qh, doh = qp * keep[h], dop * keep[h]
                sc = lax.dot_general(qh, kk, NT, preferred_element_type=F32) * (HEAD_DIM ** -0.5)
                pe = jnp.where(valid, jnp.exp(sc - _head_rows(lsep, masks[h])), 0.0)
                dp = lax.dot_general(doh, vv, NT, preferred_element_type=F32)
                dbar = jnp.sum(jnp.where(masks[h], prod, 0.0), axis=-1, keepdims=True)
                ds = pe * (dp - dbar) * (HEAD_DIM ** -0.5)
                dqs.append(jnp.dot(ds.astype(BF16), kk, preferred_element_type=F32))
            dq = jnp.where(masks[0], dqs[0], dqs[1])
            o_ref[:, sl] = _rope_transpose(dq, c_ref[...], su_ref[...], sd_ref[...]).astype(BF16)

    blk = (SPAN, GROUP_WIDTH)
    cur = lambda t: pl.BlockSpec(blk, lambda r, b: (b, r * nblk + 3 * t + grp))
    prev = lambda t: pl.BlockSpec(blk, lambda r, b: (jnp.maximum(b - 1, 0), r * nblk + 3 * t + grp))
    own = pl.BlockSpec(blk, lambda r, b: (b, r))
    tab = pl.BlockSpec((SPAN, 128), lambda r, b: (b, r))
    qv = view(qkv)
    out = pl.pallas_call(
        body, name=f"attn_bwd_dq_g{grp}", out_shape=jax.ShapeDtypeStruct((l, dil * GROUP_WIDTH), BF16), grid=(dil, nb),
        in_specs=[cur(0), prev(1), cur(1), prev(2), cur(2), own, own, own, tab, tab, tab], out_specs=own,
        compiler_params=_params("parallel", "arbitrary"),
    )(qv, qv, qv, qv, qv, view(dmix), view(mixed), view(lse), *[view(t) for t in rope])
    return out.reshape(s, GROUP_WIDTH)


def _attn_bwd_dkv(qkv, dmix, mixed, lse, rope, grp, dil):
    s, ncol = qkv.shape
    nblk = ncol // GROUP_WIDTH
    l = s // dil
    nb = l // SPAN
    view = lambda t: t.reshape(l, dil * t.shape[1])

    def body(k_ref, v_ref, qc_ref, qn_ref, doc_ref, don_ref, mc_ref, mn_ref, lc_ref, ln_ref,
             c_ref, su_ref, sd_ref, o_ref):
        b = pl.program_id(1)
        row = lax.broadcasted_iota(jnp.int32, (2 * SPAN, SPAN), 0)
        col = lax.broadcasted_iota(jnp.int32, (2 * SPAN, SPAN), 1)
        no_next = jnp.where(b < nb - 1, 0, 4 * SPAN)
        valid = ((row < SPAN) & (col <= row)) | ((row >= SPAN) & (col >= row - SPAN + no_next))
        masks, keep = _head_masks()
        masks2, _ = _head_masks(2 * SPAN)
        for p in range(GROUP_WIDTH // 128):
            sl = slice(p * 128, (p + 1) * 128)
            kp, vp = k_ref[:, sl], v_ref[:, sl]
            qq = jnp.concatenate([qc_ref[:, sl], qn_ref[:, sl]], axis=0)
            doo = jnp.concatenate([doc_ref[:, sl], don_ref[:, sl]], axis=0)
            mm = jnp.concatenate([mc_ref[:, sl], mn_ref[:, sl]], axis=0)
            ll = jnp.concatenate([lc_ref[:, sl], ln_ref[:, sl]], axis=0)
            prod = doo.astype(F32) * mm.astype(F32)
            dks, dvs = [], []
            for h in range(2):
                qh, doh = qq * keep[h], doo * keep[h]
                sc = lax.dot_general(qh, kp, NT, preferred_element_type=F32) * (HEAD_DIM ** -0.5)
                pe = jnp.where(valid, jnp.exp(sc - _head_rows(ll, masks2[h])), 0.0)
                dp = lax.dot_general(doh, vp, NT, preferred_element_type=F32)
                dbar = jnp.sum(jnp.where(masks2[h], prod, 0.0), axis=-1, keepdims=True)
                ds = pe * (dp - dbar) * (HEAD_DIM ** -0.5)
                dvs.append(lax.dot_general(pe.astype(BF16), doo, TN, preferred_element_type=F32))
                dks.append(lax.dot_general(ds.astype(BF16), qq, TN, preferred_element_type=F32))
            dk = jnp.where(masks[0], dks[0], dks[1])
            o_ref[:, sl] = _rope_transpose(dk, c_ref[...], su_ref[...], sd_ref[...]).astype(BF16)
            o_ref[:, GROUP_WIDTH + p * 128:GROUP_WIDTH + (p + 1) * 128] = jnp.where(masks[0], dvs[0], dvs[1]).astype(BF16)

    blk = (SPAN, GROUP_WIDTH)
    nxt_b = lambda b: jnp.minimum(b + 1, nb - 1)
    col_of = lambda t: pl.BlockSpec(blk, lambda r, b: (b, r * nblk + 3 * t + grp))
    q_next = pl.BlockSpec(blk, lambda r, b: (nxt_b(b), r * nblk + grp))
    own = pl.BlockSpec(blk, lambda r, b: (b, r))
    own_next = pl.BlockSpec(blk, lambda r, b: (nxt_b(b), r))
    tab = pl.BlockSpec((SPAN, 128), lambda r, b: (b, r))
    qv, dv_, mv, lv = view(qkv), view(dmix), view(mixed), view(lse)
    out = pl.pallas_call(
        body, name=f"attn_bwd_dkv_g{grp}", out_shape=jax.ShapeDtypeStruct((l, dil * 2 * GROUP_WIDTH), BF16), grid=(dil, nb),
        in_specs=[col_of(1), col_of(2), col_of(0), q_next, own, own_next, own, own_next, own, own_next, tab, tab, tab],
        out_specs=pl.BlockSpec((SPAN, 2 * GROUP_WIDTH), lambda r, b: (b, r)),
        compiler_params=_params("parallel", "arbitrary"),
    )(qv, qv, qv, qv, dv_, dv_, mv, mv, lv, lv, *[view(t) for t in rope])
    return out.reshape(s, 2 * GROUP_WIDTH)


def _rope_freq_row():
    half = ROT_DIM // 2
    inv = (ROPE_THETA ** (-np.arange(half, dtype=np.float32) / half)).astype(np.float32)
    row = np.zeros((1, 128), np.float32)
    for head in range(128 // HEAD_DIM):
        row[0, head * HEAD_DIM:head * HEAD_DIM + half] = inv
        row[0, head * HEAD_DIM + half:head * HEAD_DIM + ROT_DIM] = inv
    return jnp.asarray(row)


def _ffn_fwd(x, g_pre, g_post, w_up_t, w_dw, b_dw, w_down):
    h, z = _norm_matmul(x, g_pre, w_up_t, tn=_tile(w_up_t.shape[0]), name="ffn_up")
    act = _ffn_act(z, w_dw, b_dw)
    y, xo = _matmul_resnorm(act, w_down, x, g_post, name="ffn_down")
    return xo, (x, h, z, act, y)


def _ffn_bwd(saved, dxo, g_pre, g_post, w_up_t, w_dw, b_dw, w_down, layer, d_up_t, d_down):
    x, h, z, act, y = saved
    f = act.shape[1]
    d = x.shape[1]
    dy, dg_post = _postnorm_bwd(y, g_post, dxo, name="ffn_post_bwd")
    dact = _matmul(dy, w_down, name="ffn_dact", out_dtype=BF16, transposed_w=True)
    d_down = _weight_grad(act, dy, name="ffn_dw_down", out=d_down, out_shape=(2, f, d), layer=layer)
    dug_u, dug_g, db_u, db_g, dwd_u, dwd_g = _ffn_act_bwd(z, dact, w_dw, b_dw)
    dz_u = _conv3_transpose(dug_u, w_dw, 0)
    dz_g = _conv3_transpose(dug_g, w_dw, f)
    dx, dg_pre = _matmul_prenorm_bwd([(dz_u, 0, f, 0), (dz_g, 0, f, f)], w_up_t, x, g_pre, dxo, name="ffn_dx")
    d_up_t = _weight_grad(dz_u, h, name="ffn_dw_up", out=d_up_t, out_shape=(2, 2 * f, d), layer=layer)
    d_up_t = _weight_grad(dz_g, h, name="ffn_dw_up", out=d_up_t, layer=layer, row0=f)
    grads = dict(w_dw=jnp.concatenate([dwd_u, dwd_g], axis=1), b_dw=jnp.concatenate([db_u, db_g], axis=1),
                 g_pre=dg_pre, g_post=dg_post)
    return dx, grads, d_up_t, d_down


def _local_step(x, pos_col, target, p):
    ng = p["norm_g"]
    row = lambda r: ng[r:r + 1]
    rope = _rope_tables(pos_col, _rope_freq_row())
    d = x.shape[1]

    h0, qkv = _norm_matmul(x, row(0), p["w_qkv_t"], tn=3 * GROUP_WIDTH, name="attn_qkv", rope=rope, rope_blocks=2)
    os_, ls_ = zip(*[_attn_fwd(qkv, g_, d_) for g_, d_ in enumerate(DILATIONS)])
    y_a, x1, mixed, lse = _mix_wo(os_, ls_, p["w_o_t"], x, row(1))
    x2, ffn0 = _ffn_fwd(x1, row(2), row(3), p["w_up_t"][0], p["ffn_w_dw"][0], p["ffn_b_dw"][0], p["w_down"][0])
    h1, ag = _norm_matmul(x2, row(4), p["w_pw1_t"], tn=_tile(p["w_pw1_t"].shape[0]), name="conv_pw1", bias=p["b_pw1"])
    u3 = _conv_mid(ag, p["conv_w_dw"], p["conv_b_dw"], p["ln_g"], p["ln_b"])
    y_c, x3 = _matmul_resnorm(u3, p["w_pw2"], x2, row(5), name="conv_pw2", bias=p["b_pw2"])
    x4, ffn1 = _ffn_fwd(x3, row(6), row(7), p["w_up_t"][1], p["ffn_w_dw"][1], p["ffn_b_dw"][1], p["w_down"][1])
    dx4, loss = _loss_grad(x4, target)

    dx3, gf1, d_up_t, d_down = _ffn_bwd(ffn1, dx4, row(6), row(7), p["w_up_t"][1], p["ffn_w_dw"][1], p["ffn_b_dw"][1],
                                        p["w_down"][1], 1, None, None)
    dy_c, dg5, db_pw2 = _postnorm_bwd(y_c, row(5), dx3, name="conv_post_bwd", with_bias_grad=True)
    du3 = _matmul(dy_c, p["w_pw2"], name="conv_du3", out_dtype=F32, transposed_w=True)
    d_wpw2 = _weight_grad(u3, dy_c, name="conv_dw_pw2", out_shape=(1, u3.shape[1], d))
    du1, d_lng, d_lnb, d_cbdw, d_cwdw = _conv_mid_bwd(ag, du3, p["conv_w_dw"], p["conv_b_dw"], p["ln_g"], p["ln_b"])
    dag, db_pw1 = _glu_conv_bwd(du1, ag, p["conv_w_dw"])
    dx2, dg4 = _matmul_prenorm_bwd([(dag, 0, dag.shape[1], 0)], p["w_pw1_t"], x2, row(4), dx3, name="conv_dx")
    d_wpw1_t = _weight_grad(dag, h1, name="conv_dw_pw1", out_shape=(1, dag.shape[1], d))
    dx1, gf0, d_up_t, d_down = _ffn_bwd(ffn0, dx2, row(2), row(3), p["w_up_t"][0], p["ffn_w_dw"][0], p["ffn_b_dw"][0],
                                        p["w_down"][0], 0, d_up_t, d_down)
    dy_a, dg1 = _postnorm_bwd(y_a, row(1), dx1, name="attn_post_bwd")
    dmix = _matmul(dy_a, p["w_o_t"], name="attn_dmix", out_dtype=BF16, transposed_w=False)
    d_wo_t = _weight_grad(dy_a, mixed, name="attn_dw_o", out_shape=(1, d, GROUP_WIDTH))
    pieces, d_wqkv_t = [], None
    for g_, d_ in enumerate(DILATIONS):
        dq = _attn_bwd_dq(qkv, dmix, mixed, lse, rope, g_, d_)
        dkv = _attn_bwd_dkv(qkv, dmix, mixed, lse, rope, g_, d_)
        for t, (arr, c0) in enumerate(((dq, 0), (dkv, 0), (dkv, GROUP_WIDTH))):
            r0 = (3 * t + g_) * GROUP_WIDTH
            pieces.append((arr, c0, GROUP_WIDTH, r0))
            d_wqkv_t = _weight_grad(arr, h0, name="attn_dw_qkv", a_col0=c0, ka=GROUP_WIDTH, out=d_wqkv_t,
                                    out_shape=(1, qkv.shape[1], d), row0=r0)
    grad_x, dg0 = _matmul_prenorm_bwd(pieces, p["w_qkv_t"], x, row(0), dx1, name="attn_dx")

    grads = dict(
        norm_g=jnp.concatenate([dg0, dg1, gf0["g_pre"], gf0["g_post"], dg4, dg5, gf1["g_pre"], gf1["g_post"]], axis=0),
        w_qkv_t=d_wqkv_t, w_o_t=d_wo_t, w_pw1_t=d_wpw1_t, b_pw1=db_pw1,
        conv_w_dw=d_cwdw[:CONV_KERNEL], conv_b_dw=d_cbdw, ln_g=d_lng, ln_b=d_lnb, w_pw2=d_wpw2, b_pw2=db_pw2,
        w_up_t=d_up_t, ffn_w_dw=jnp.stack([gf0["w_dw"], gf1["w_dw"]]),
        ffn_b_dw=jnp.concatenate([gf0["b_dw"], gf1["b_dw"]], axis=0), w_down=d_down)
    return loss, grad_x, grads


SMALL_AXIS = dict(norm_g=2, conv_b_pw1=1, conv_w_dw=2, conv_b_dw=1, conv_ln_g=1, conv_ln_b=1, conv_b_pw2=1, ffn_w_dw=2)
SMALL = tuple(SMALL_AXIS)
MATMUL_WEIGHTS = dict(attn_w_qkv=True, conv_w_pw1=True, ffn_w_up=True, conv_w_pw2=False, ffn_w_down=False)


def _pack(arrays, cols, row_multiple):
    flat = jnp.concatenate([a.reshape(-1) for a in arrays])
    rows = -(-flat.shape[0] // cols)
    rows = -(-rows // row_multiple) * row_multiple
    return jnp.pad(flat, (0, rows * cols - flat.shape[0])).reshape(rows, cols)


def _unpack(packed, shapes):
    flat = packed.reshape(packed.shape[:-2] + (-1,))
    out, off = [], 0
    for shp in shapes:
        n = math.prod(shp)
        out.append(flat[..., off:off + n].reshape(packed.shape[:-2] + tuple(shp)))
        off += n
    return out


def _join_shards(stacked, axis):
    moved = jnp.moveaxis(stacked, 0, axis)
    shp = moved.shape
    return moved.reshape(shp[:axis] + (shp[axis] * shp[axis + 1],) + shp[axis + 2:])


def _split_shards(whole, axis):
    shp = whole.shape
    cut = whole.reshape(shp[:axis] + (N_DEV, shp[axis] // N_DEV) + shp[axis + 1:])
    return jnp.moveaxis(cut, axis, 0)


def _row_shard(w, transposed):
    t = jnp.swapaxes(w, 1, 2) if transposed else w
    return t.astype(BF16).reshape(-1, t.shape[-1])


def kernel(x, positions, norm_g, attn_w_qkv, attn_w_o, conv_w_pw1, conv_b_pw1, conv_w_dw, conv_b_dw, conv_ln_g, conv_ln_b, conv_w_pw2, conv_b_pw2, ffn_w_up, ffn_w_dw, ffn_b_dw, ffn_w_down, loss_target, m_norm_g, m_attn_w_qkv, m_attn_w_o, m_conv_w_pw1, m_conv_b_pw1, m_conv_w_dw, m_conv_b_dw, m_conv_ln_g, m_conv_ln_b, m_conv_w_pw2, m_conv_b_pw2, m_ffn_w_up, m_ffn_w_dw, m_ffn_b_dw, m_ffn_w_down, v_norm_g, v_attn_w_qkv, v_attn_w_o, v_conv_w_pw1, v_conv_b_pw1, v_conv_w_dw, v_conv_b_dw, v_conv_ln_g, v_conv_ln_b, v_conv_w_pw2, v_conv_b_pw2, v_ffn_w_up, v_ffn_w_dw, v_ffn_b_dw, v_ffn_w_down):
    w = dict(norm_g=norm_g, attn_w_qkv=attn_w_qkv, attn_w_o=attn_w_o, conv_w_pw1=conv_w_pw1, conv_b_pw1=conv_b_pw1,
             conv_w_dw=conv_w_dw, conv_b_dw=conv_b_dw, conv_ln_g=conv_ln_g, conv_ln_b=conv_ln_b, conv_w_pw2=conv_w_pw2,
             conv_b_pw2=conv_b_pw2, ffn_w_up=ffn_w_up, ffn_w_dw=ffn_w_dw, ffn_w_down=ffn_w_down)
    m = dict(norm_g=m_norm_g, attn_w_qkv=m_attn_w_qkv, attn_w_o=m_attn_w_o, conv_w_pw1=m_conv_w_pw1, conv_b_pw1=m_conv_b_pw1,
             conv_w_dw=m_conv_w_dw, conv_b_dw=m_conv_b_dw, conv_ln_g=m_conv_ln_g, conv_ln_b=m_conv_ln_b, conv_w_pw2=m_conv_w_pw2,
             conv_b_pw2=m_conv_b_pw2, ffn_w_up=m_ffn_w_up, ffn_w_dw=m_ffn_w_dw, ffn_w_down=m_ffn_w_down)
    v = dict(norm_g=v_norm_g, attn_w_qkv=v_attn_w_qkv, attn_w_o=v_attn_w_o, conv_w_pw1=v_conv_w_pw1, conv_b_pw1=v_conv_b_pw1,
             conv_w_dw=v_conv_w_dw, conv_b_dw=v_conv_b_dw, conv_ln_g=v_conv_ln_g, conv_ln_b=v_conv_ln_b, conv_w_pw2=v_conv_w_pw2,
             conv_b_pw2=v_conv_b_pw2, ffn_w_up=v_ffn_w_up, ffn_w_dw=v_ffn_w_dw, ffn_w_down=v_ffn_w_down)
    d = x.shape[-1]

    shares = [_row_shard(w[n], t) for n, t in MATMUL_WEIGHTS.items()]
    rows = [s_.shape[0] for s_ in shares]
    big = _all_gather(jnp.concatenate(shares, axis=0), "gather_matmul_weights")
    whole, r0 = {}, 0
    for (n, _), nr in zip(MATMUL_WEIGHTS.items(), rows):
        layers = w[n].shape[0]
        seg = big[:, r0:r0 + nr].reshape(N_DEV, layers, nr // layers, d)
        whole[n] = [seg[:, l_].reshape(-1, d) for l_ in range(layers)]
        r0 += nr
    w_o_t = _all_gather(_row_shard(attn_w_o, True), "gather_w_o").reshape(d, -1)
    small = _all_gather(_pack([w[n] for n in SMALL], 128, 8), "gather_small_weights")
    sm = {n: _join_shards(stacked, SMALL_AXIS[n])
          for n, stacked in zip(SMALL, _unpack(small, [w[n].shape for n in SMALL]))}
    p = dict(norm_g=sm["norm_g"].reshape(-1, d), w_qkv_t=whole["attn_w_qkv"][0], w_o_t=w_o_t,
             w_pw1_t=whole["conv_w_pw1"][0], b_pw1=sm["conv_b_pw1"], conv_w_dw=sm["conv_w_dw"][0],
             conv_b_dw=sm["conv_b_dw"], ln_g=sm["conv_ln_g"], ln_b=sm["conv_ln_b"], w_pw2=whole["conv_w_pw2"][0],
             b_pw2=sm["conv_b_pw2"], w_up_t=whole["ffn_w_up"], ffn_w_dw=sm["ffn_w_dw"],
             ffn_b_dw=[ffn_b_dw[0:1], ffn_b_dw[1:2]], w_down=whole["ffn_w_down"])

    loss, grad_x, g = _local_step(x[0], positions.reshape(-1, 1), loss_target[0], p)
    loss = lax.psum(loss[0, 0], ("x", "y", "c"))

    gsmall = dict(norm_g=g["norm_g"].reshape(norm_g.shape[0], 4, -1), conv_b_pw1=g["b_pw1"], conv_w_dw=g["conv_w_dw"][None],
                  conv_b_dw=g["conv_b_dw"], conv_ln_g=g["ln_g"], conv_ln_b=g["ln_b"], conv_b_pw2=g["b_pw2"], ffn_w_dw=g["ffn_w_dw"])
    small_contrib = jnp.concatenate([_split_shards(gsmall[n], SMALL_AXIS[n]).reshape(N_DEV, -1) for n in SMALL], axis=1)
    srows = small.shape[1]
    small_contrib = jnp.pad(small_contrib, ((0, 0), (0, srows * 128 - small_contrib.shape[1]))).reshape(1, N_DEV, srows, 128)
    big_names = ("w_qkv_t", "w_o_t", "w_pw1_t", "w_up_t", "w_pw2", "w_down")
    contribs = [g[n].reshape(g[n].shape[0], N_DEV, g[n].shape[1] // N_DEV, g[n].shape[2]) for n in big_names] + [small_contrib]
    core = lax.axis_index("c").astype(jnp.int32).reshape(1)
    got = _rs_sibling(contribs)
    dtypes = [BF16] * len(big_names) + [F32]
    chip_sums = _rs_chips([_rs_pair_add(c_, g_, core, dt) for c_, g_, dt in zip(contribs, got, dtypes)])

    outs = {}
    for n, gname, transposed in (("attn_w_qkv", "w_qkv_t", True), ("attn_w_o", "w_o_t", True), ("conv_w_pw1", "w_pw1_t", True),
                                 ("ffn_w_up", "w_up_t", True), ("conv_w_pw2", "w_pw2", False), ("ffn_w_down", "w_down", False)):
        gsum = _sum_parts(chip_sums[big_names.index(gname)], "sum_chips")
        gsum = jnp.swapaxes(gsum, 1, 2) if transposed else gsum
        outs[n] = (gsum, *_adamw(gsum, w[n], m[n], v[n], "adamw"))
    sshapes = [w[n].shape for n in SMALL]
    souts = _sum_adamw(chip_sums[-1][0], *[_pack([t[n] for n in SMALL], 128, 8) for t in (w, m, v)], name="sum_adamw_small")
    for n, vals in zip(SMALL, zip(*[_unpack(o, sshapes) for o in souts])):
        outs[n] = vals
    bparts = _all_gather(_pack([g["ffn_b_dw"]], 128, 8), "gather_bias_grads")
    bouts = _sum_adamw(bparts, *[_pack([t], 128, 8) for t in (ffn_b_dw, m_ffn_b_dw, v_ffn_b_dw)], name="sum_adamw_bias")
    outs["ffn_b_dw"] = tuple(_unpack(o, [ffn_b_dw.shape])[0] for o in bouts)

    order = ("norm_g", "attn_w_qkv", "attn_w_o", "conv_w_pw1", "conv_b_pw1", "conv_w_dw", "conv_b_dw", "conv_ln_g",
             "conv_ln_b", "conv_w_pw2", "conv_b_pw2", "ffn_w_up", "ffn_w_dw", "ffn_b_dw", "ffn_w_down")
    return (loss, grad_x[None], *[outs[n][0] for n in order], *[outs[n][1] for n in order],
            *[outs[n][2] for n in order], *[outs[n][3] for n in order])
```

```python
import functools
import math

import numpy as np
import jax
import jax.numpy as jnp
from jax import lax
from jax.experimental import pallas as pl
from jax.experimental.pallas import tpu as pltpu

F32 = jnp.float32
BF16 = jnp.bfloat16
EPS = 1e-6
N_DEV = 8
HEAD_DIM = 64
GROUP_WIDTH = 512
DILATIONS = (1, 4, 16)
SPAN = 128
ROT_DIM = 16
ROPE_THETA = 500000.0
CONV_KERNEL = 31
CONV_HALO = 32
FFN_CONV = 3
ADAM_LR, ADAM_B1, ADAM_B2, ADAM_EPS, ADAM_WD, ADAM_STEP = 0.001, 0.9, 0.999, 1e-08, 0.01, 10
VMEM_LIMIT_BYTES = 56 * 1024 * 1024
MESH = pl.DeviceIdType.MESH
ANY = pl.BlockSpec(memory_space=pl.ANY)
NT = (((1,), (1,)), ((), ()))
TN = (((0,), (0,)), ((), ()))


def _params(*sem):
    return pltpu.CompilerParams(dimension_semantics=sem, vmem_limit_bytes=VMEM_LIMIT_BYTES)


def _sigmoid(v):
    return 1.0 / (1.0 + jnp.exp(-v))


def _full(shape):
    return pl.BlockSpec(shape, lambda *_: (0,) * len(shape))


def _rows(tm, width):
    return pl.BlockSpec((tm, width), lambda i, *_: (i, 0))


def _tile(n, *multiples_of):
    for t in (1408, 1024, 512, 384, 256, 128):
        if n % t == 0 and all(o % t == 0 for o in multiples_of):
            return t
    raise ValueError((n, multiples_of))


def _all_gather(shard, name):
    r, c_ = shard.shape

    def body(x_ref, out_ref, send_sems, recv_sems, local_sem):
        x, y, c = lax.axis_index("x"), lax.axis_index("y"), lax.axis_index("c")
        me, sibling = (x, y, c), (x, y, 1 - c)
        chips = [(1 - x, y), (x, 1 - y), (1 - x, 1 - y)]

        def rows(px, py, pc):
            return out_ref.at[4 * px + 2 * py + pc]

        def copy(k, block, to, src=None):
            return pltpu.make_async_remote_copy(
                src_ref=rows(*block) if src is None else src, dst_ref=rows(*block),
                send_sem=send_sems.at[k], recv_sem=recv_sems.at[k], device_id=to, device_id_type=MESH)

        mine = pltpu.make_async_copy(x_ref, rows(*me), local_sem)
        mine.start()
        first = [copy(0, me, sibling, src=x_ref)]
        first += [copy(1 + j, me, (*chip, c), src=x_ref) for j, chip in enumerate(chips)]
        for cp in first:
            cp.start()
        passed = [copy(4 + j, (*chip, c), sibling) for j, chip in enumerate(chips)]
        for j, chip in enumerate(chips):
            copy(1 + j, (*chip, c), me).wait_recv()
            passed[j].start()
        copy(0, sibling, me).wait_recv()
        for j, chip in enumerate(chips):
            copy(4 + j, (*chip, 1 - c), me).wait_recv()
        for cp in first + passed:
            cp.wait_send()
        mine.wait()

    return pl.pallas_call(
        body, name=name, out_shape=jax.ShapeDtypeStruct((N_DEV, r, c_), shard.dtype),
        in_specs=[ANY], out_specs=ANY,
        scratch_shapes=[pltpu.SemaphoreType.DMA((7,)), pltpu.SemaphoreType.DMA((7,)), pltpu.SemaphoreType.DMA],
    )(shard)


def _with_rows(g, n):
    return jax.ShapeDtypeStruct((g.shape[0], n) + tuple(g.shape[2:]), g.dtype)


def _rs_sibling(gs):
    n = len(gs)

    def body(*refs):
        g_refs, o_refs, (send_sems, recv_sems) = refs[:n], refs[n:2 * n], refs[2 * n:]
        x, y, c = lax.axis_index("x"), lax.axis_index("y"), lax.axis_index("c")
        copies = [pltpu.make_async_remote_copy(
            src_ref=g_refs[w].at[:, 2 * q + (1 - c)], dst_ref=o_refs[w].at[:, q], send_sem=send_sems.at[4 * w + q],
            recv_sem=recv_sems.at[4 * w + q], device_id=(x, y, 1 - c), device_id_type=MESH)
            for w in range(n) for q in range(4)]
        for cp in copies:
            cp.start()
        for cp in copies:
            cp.wait_recv()
        for cp in copies:
            cp.wait_send()

    return pl.pallas_call(
        body, name="rs_sibling", out_shape=[_with_rows(g, 4) for g in gs],
        in_specs=[ANY] * n, out_specs=[ANY] * n,
        scratch_shapes=[pltpu.SemaphoreType.DMA((4 * n,)), pltpu.SemaphoreType.DMA((4 * n,))],
    )(*gs)


def _rs_pair_add(g, got, core, out_dtype):
    l, _, r, c_ = g.shape

    def body(core_ref, g_ref, got_ref, o_ref):
        o_ref[...] = (g_ref[...].astype(F32) + got_ref[...].astype(F32)).astype(out_dtype)

    blk = (None, None, r, c_)
    return pl.pallas_call(
        body, name="rs_pair_add", out_shape=jax.ShapeDtypeStruct((l, 4, r, c_), out_dtype),
        grid_spec=pltpu.PrefetchScalarGridSpec(
            num_scalar_prefetch=1, grid=(l, 4),
            in_specs=[pl.BlockSpec(blk, lambda i, q, core_ref: (i, 2 * q + core_ref[0], 0, 0)),
                      pl.BlockSpec(blk, lambda i, q, core_ref: (i, q, 0, 0))],
            out_specs=pl.BlockSpec(blk, lambda i, q, core_ref: (i, q, 0, 0))),
        compiler_params=_params("parallel", "parallel"),
    )(core, g, got)


def _rs_chips(parts):
    n = len(parts)

    def body(*refs):
        p_refs, o_refs, (send_sems, recv_sems, local_sems) = refs[:n], refs[n:2 * n], refs[2 * n:]
        x, y, c = lax.axis_index("x"), lax.axis_index("y"), lax.axis_index("c")
        my_chip = 2 * x + y
        chips = [(1 - x, y), (x, 1 - y), (1 - x, 1 - y)]
        local = [pltpu.make_async_copy(p_refs[w].at[:, my_chip], o_refs[w].at[:, my_chip], local_sems.at[w]) for w in range(n)]
        for cp in local:
            cp.start()
        copies = [pltpu.make_async_remote_copy(
            src_ref=p_refs[w].at[:, 2 * qx + qy], dst_ref=o_refs[w].at[:, my_chip], send_sem=send_sems.at[3 * w + k],
            recv_sem=recv_sems.at[3 * w + k], device_id=(qx, qy, c), device_id_type=MESH)
            for w in range(n) for k, (qx, qy) in enumerate(chips)]
        for cp in copies:
            cp.start()
        for cp in copies:
            cp.wait_recv()
        for cp in copies:
            cp.wait_send()
        for cp in local:
            cp.wait()

    return pl.pallas_call(
        body, name="rs_chips", out_shape=[jax.ShapeDtypeStruct(p.shape, p.dtype) for p in parts],
        in_specs=[ANY] * n, out_specs=[ANY] * n,
        scratch_shapes=[pltpu.SemaphoreType.DMA((3 * n,)), pltpu.SemaphoreType.DMA((3 * n,)), pltpu.SemaphoreType.DMA((n,))],
    )(*parts)


def _sum_parts(parts, name):
    l, n, r, c_ = parts.shape

    def body(p_ref, o_ref):
        g = p_ref[0].astype(F32)
        for s in range(1, n):
            g = g + p_ref[s].astype(F32)
        o_ref[...] = g

    return pl.pallas_call(
        body, name=name, out_shape=jax.ShapeDtypeStruct((l, r, c_), F32), grid=(l,),
        in_specs=[pl.BlockSpec((None, n, r, c_), lambda i: (i, 0, 0, 0))],
        out_specs=pl.BlockSpec((None, r, c_), lambda i: (i, 0, 0)), compiler_params=_params("parallel"),
    )(parts)


def _adamw_math(w, g, m, v):
    m = ADAM_B1 * m + (1.0 - ADAM_B1) * g
    v = ADAM_B2 * v + (1.0 - ADAM_B2) * (g * g)
    m_hat = m / (1.0 - ADAM_B1 ** ADAM_STEP)
    v_hat = v / (1.0 - ADAM_B2 ** ADAM_STEP)
    delta = -ADAM_LR * (m_hat / (jnp.sqrt(v_hat) + ADAM_EPS) + ADAM_WD * w)
    return delta, m, v


def _adamw(g, w, m, v, name):
    l, k, n = w.shape
    tk = 256 if k % 256 == 0 else k

    def body(g_ref, w_ref, m_ref, v_ref, d_ref, nm_ref, nv_ref):
        d_ref[...], nm_ref[...], nv_ref[...] = _adamw_math(w_ref[...], g_ref[...], m_ref[...], v_ref[...])

    spec = pl.BlockSpec((None, tk, n), lambda i, j: (i, j, 0))
    return pl.pallas_call(
        body, name=name, out_shape=[jax.ShapeDtypeStruct((l, k, n), F32)] * 3, grid=(l, k // tk),
        in_specs=[spec] * 4, out_specs=[spec] * 3, compiler_params=_params("parallel", "parallel"),
    )(g, w, m, v)


def _sum_adamw(parts, w, m, v, name):
    n, r, c_ = parts.shape

    def body(p_ref, w_ref, m_ref, v_ref, g_ref, d_ref, nm_ref, nv_ref):
        g = p_ref[0]
        for s in range(1, n):
            g = g + p_ref[s]
        g_ref[...] = g
        d_ref[...], nm_ref[...], nv_ref[...] = _adamw_math(w_ref[...], g, m_ref[...], v_ref[...])

    return pl.pallas_call(
        body, name=name, out_shape=[jax.ShapeDtypeStruct((r, c_), F32)] * 4, grid=(1,),
        in_specs=[_full((n, r, c_))] + [_full((r, c_))] * 3, out_specs=[_full((r, c_))] * 4,
        compiler_params=_params("arbitrary"),
    )(parts, w, m, v)


def _rope_tables(pos_col, freq_row):
    s = pos_col.shape[0]
    tm = min(1024, s)

    def body(p_ref, f_ref, c_ref, su_ref, sd_ref):
        ang = p_ref[...].astype(F32) * f_ref[...]
        lane = lax.broadcasted_iota(jnp.int32, ang.shape, 1) & (HEAD_DIM - 1)
        cs, sn = jnp.cos(ang), jnp.sin(ang)
        c_ref[...] = jnp.where(lane < ROT_DIM, cs, 1.0)
        su_ref[...] = jnp.where((lane >= ROT_DIM // 2) & (lane < ROT_DIM), sn, 0.0)
        sd_ref[...] = jnp.where(lane < ROT_DIM // 2, -sn, 0.0)

    return pl.pallas_call(
        body, name="rope_tables", out_shape=[jax.ShapeDtypeStruct((s, 128), F32)] * 3, grid=(s // tm,),
        in_specs=[pl.BlockSpec((tm, 1), lambda i: (i, 0)), _full((1, 128))],
        out_specs=[_rows(tm, 128)] * 3, compiler_params=_params("parallel"),
    )(pos_col, freq_row)


def _rope_apply(t, cos, sin_up, sin_dn):
    w = t.shape[1]
    return t * cos + pltpu.roll(t, 8, 1) * sin_up + pltpu.roll(t, w - 8, 1) * sin_dn


def _rope_transpose(dr, cos, sin_up, sin_dn):
    w = dr.shape[1]
    return dr * cos + pltpu.roll(dr * sin_up, w - 8, 1) + pltpu.roll(dr * sin_dn, 8, 1)


def _norm_matmul(x, g, wt, *, tn, name, bias=None, rope=None, rope_blocks=0, tm=512):
    s, d = x.shape
    n = wt.shape[0]
    tm = min(tm, s)

    def body(*refs):
        x_ref, g_ref, w_ref = refs[:3]
        k = 3
        b_ref = None
        if bias is not None:
            b_ref = refs[k]
            k += 1
        if rope is not None:
            c_ref, su_ref, sd_ref = refs[k:k + 3]
            k += 3
        h_ref, o_ref = refs[k:k + 2]
        j = pl.program_id(1)

        @pl.when(j == 0)
        def _():
            xv = x_ref[...]
            r = lax.rsqrt(jnp.mean(xv * xv, axis=-1, keepdims=True) + EPS)
            h_ref[...] = (xv * r * g_ref[...]).astype(BF16)

        acc = lax.dot_general(h_ref[...], w_ref[...], NT, preferred_element_type=F32)
        if b_ref is not None:
            acc = acc + b_ref[...]
        if rope is None:
            o_ref[...] = acc.astype(BF16)
        else:
            @pl.when(j < rope_blocks)
            def _():
                reps = tn // 128
                o_ref[...] = _rope_apply(acc, jnp.tile(c_ref[...], (1, reps)), jnp.tile(su_ref[...], (1, reps)),
                                         jnp.tile(sd_ref[...], (1, reps))).astype(BF16)

            @pl.when(j >= rope_blocks)
            def _():
                o_ref[...] = acc.astype(BF16)

    in_specs = [_rows(tm, d), _full((1, d)), pl.BlockSpec((tn, d), lambda i, j: (j, 0))]
    args = [x, g, wt]
    if bias is not None:
        in_specs.append(pl.BlockSpec((1, tn), lambda i, j: (0, j)))
        args.append(bias)
    if rope is not None:
        in_specs += [_rows(tm, 128)] * 3
        args += list(rope)
    return pl.pallas_call(
        body, name=name,
        out_shape=[jax.ShapeDtypeStruct((s, d), BF16), jax.ShapeDtypeStruct((s, n), BF16)],
        grid=(s // tm, n // tn), in_specs=in_specs,
        out_specs=[_rows(tm, d), pl.BlockSpec((tm, tn), lambda i, j: (i, j))],
        compiler_params=_params("parallel", "arbitrary"),
    )(*args)


def _class_major(tm, dil):
    p = np.zeros((tm, tm), np.float32)
    per = tm // dil
    for r in range(dil):
        for j in range(per):
            p[r * per + j, j * dil + r] = 1.0
    return jnp.asarray(p, dtype=BF16)


def _qkv_proj(x, g, wt, rope, tm=512):
    s, d = x.shape
    n = wt.shape[0]
    gw3 = 3 * GROUP_WIDTH
    tm = min(tm, s)
    assert n == 3 * gw3

    def body(x_ref, g_ref, w_ref, c_ref, su_ref, sd_ref, p1_ref, p2_ref, h_ref, o0_ref, o1_ref, o2_ref):
        j = pl.program_id(1)

        @pl.when(j == 0)
        def _():
            xv = x_ref[...]
            r = lax.rsqrt(jnp.mean(xv * xv, axis=-1, keepdims=True) + EPS)
            h_ref[...] = (xv * r * g_ref[...]).astype(BF16)

        acc = lax.dot_general(h_ref[...], w_ref[...], NT, preferred_element_type=F32)

        def store(y):
            yb = y.astype(BF16)
            o0_ref[:, pl.ds(pl.multiple_of(j * GROUP_WIDTH, GROUP_WIDTH), GROUP_WIDTH)] = yb[:, :GROUP_WIDTH]
            for grp, o_ref, p_ref in ((1, o1_ref, p1_ref), (2, o2_ref, p2_ref)):
                dil = DILATIONS[grp]
                per = tm // dil
                yp = jnp.dot(p_ref[...], yb[:, grp * GROUP_WIDTH:(grp + 1) * GROUP_WIDTH],
                             preferred_element_type=F32).astype(BF16)
                for r in range(dil):
                    col = pl.multiple_of(r * gw3 + j * GROUP_WIDTH, GROUP_WIDTH)
                    o_ref[:, pl.ds(col, GROUP_WIDTH)] = yp[r * per:(r + 1) * per, :]

        @pl.when(j < 2)
        def _():
            reps = gw3 // 128
            store(_rope_apply(acc, jnp.tile(c_ref[...], (1, reps)), jnp.tile(su_ref[...], (1, reps)),
                              jnp.tile(sd_ref[...], (1, reps))))

        @pl.when(j == 2)
        def _():
            store(acc)

    outs = [jax.ShapeDtypeStruct((s, d), BF16)] + [jax.ShapeDtypeStruct((s // dl, dl * gw3), BF16) for dl in DILATIONS]
    out_specs = [_rows(tm, d)] + [_rows(tm // dl, dl * gw3) for dl in DILATIONS]
    return pl.pallas_call(
        body, name="attn_qkv", out_shape=outs, grid=(s // tm, 3),
        in_specs=[_rows(tm, d), _full((1, d)), pl.BlockSpec((gw3, d), lambda i, j: (j, 0))] + [_rows(tm, 128)] * 3
        + [_full((tm, tm))] * 2,
        out_specs=out_specs, compiler_params=_params("parallel", "arbitrary"),
    )(x, g, wt, *rope, _class_major(tm, DILATIONS[1]), _class_major(tm, DILATIONS[2]))


def _head_masks(rows=SPAN):
    lane = lax.broadcasted_iota(jnp.int32, (rows, 128), 1)
    masks = [lane < HEAD_DIM, lane >= HEAD_DIM]
    lane1 = lax.broadcasted_iota(jnp.int32, (1, 128), 1)
    keep = [jnp.where(lane1 < HEAD_DIM, 1.0, 0.0).astype(BF16), jnp.where(lane1 >= HEAD_DIM, 1.0, 0.0).astype(BF16)]
    return masks, keep


def _attn_fwd(qv, grp, dil):
    l = qv.shape[0]
    s = l * dil
    nb = l // SPAN

    def body(q_ref, kp_ref, kc_ref, vp_ref, vc_ref, o_ref, l_ref):
        b = pl.program_id(1)
        row = lax.broadcasted_iota(jnp.int32, (SPAN, 2 * SPAN), 0)
        col = lax.broadcasted_iota(jnp.int32, (SPAN, 2 * SPAN), 1)
        no_prev = jnp.where(b > 0, 0, 4 * SPAN)
        valid = ((col < SPAN) & (col >= row + no_prev)) | ((col >= SPAN) & (col - SPAN <= row))
        masks, keep = _head_masks()
        for p in range(GROUP_WIDTH // 128):
            sl = slice(p * 128, (p + 1) * 128)
            qp = q_ref[:, sl]
            kk = jnp.concatenate([kp_ref[:, sl], kc_ref[:, sl]], axis=0)
            vv = jnp.concatenate([vp_ref[:, sl], vc_ref[:, sl]], axis=0)
            outs, lses = [], []
            for h in range(2):
                sc = lax.dot_general(qp * keep[h], kk, NT, preferred_element_type=F32) * (HEAD_DIM ** -0.5)
                sc = jnp.where(valid, sc, -1e30)
                mx = jnp.max(sc, axis=-1, keepdims=True)
                pe = jnp.exp(sc - mx)
                den = jnp.sum(pe, axis=-1, keepdims=True)
                pv = jnp.dot(pe.astype(BF16), vv, preferred_element_type=F32)
                outs.append(pv / den)
                lses.append(jnp.broadcast_to(mx + jnp.log(den), (SPAN, 128)))
            o_ref[:, sl] = jnp.where(masks[0], outs[0], outs[1])
            l_ref[:, sl] = jnp.where(masks[0], lses[0], lses[1])

    blk = (SPAN, GROUP_WIDTH)
    cur = lambda t: pl.BlockSpec(blk, lambda r, b: (b, r * 3 + t))
    prev = lambda t: pl.BlockSpec(blk, lambda r, b: (jnp.maximum(b - 1, 0), r * 3 + t))
    out = pl.BlockSpec(blk, lambda r, b: (b, r))
    o, lse = pl.pallas_call(
        body, name=f"attn_fwd_g{grp}", out_shape=[jax.ShapeDtypeStruct((l, dil * GROUP_WIDTH), F32)] * 2,
        grid=(dil, nb), in_specs=[cur(0), prev(1), cur(1), prev(2), cur(2)], out_specs=[out, out],
        compiler_params=_params("parallel", "arbitrary"),
    )(qv, qv, qv, qv, qv)
    return o.reshape(s, GROUP_WIDTH), lse.reshape(s, GROUP_WIDTH)


def _resnorm_store(y, x_ref, g_ref, y_ref, xo_ref):
    r = lax.rsqrt(jnp.mean(y * y, axis=-1, keepdims=True) + EPS)
    y_ref[...] = y
    xo_ref[...] = x_ref[...] + y * r * g_ref[...]


def _mix_wo(os_, ls_, wot, x, g, tm=256):
    s, d = x.shape
    gw = wot.shape[1]
    tm = min(tm, s)

    def body(o0, o1, o2, l0, l1, l2, w_ref, x_ref, g_ref, y_ref, xo_ref, mixed_ref, lse_ref):
        a0, a1, a2 = l0[...], l1[...], l2[...]
        mx = jnp.maximum(jnp.maximum(a0, a1), a2)
        e0, e1, e2 = jnp.exp(a0 - mx), jnp.exp(a1 - mx), jnp.exp(a2 - mx)
        den = e0 + e1 + e2
        mixed = (e0 / den) * o0[...] + (e1 / den) * o1[...] + (e2 / den) * o2[...]
        mixed_ref[...] = mixed.astype(BF16)
        lse_ref[...] = mx + jnp.log(den)
        y = lax.dot_general(mixed.astype(BF16), w_ref[...], NT, preferred_element_type=F32)
        _resnorm_store(y, x_ref, g_ref, y_ref, xo_ref)

    return pl.pallas_call(
        body, name="mix_wo",
        out_shape=[jax.ShapeDtypeStruct((s, d), F32), jax.ShapeDtypeStruct((s, d), F32),
                   jax.ShapeDtypeStruct((s, gw), BF16), jax.ShapeDtypeStruct((s, gw), F32)],
        grid=(s // tm,), in_specs=[_rows(tm, gw)] * 6 + [_full((d, gw)), _rows(tm, d), _full((1, d))],
        out_specs=[_rows(tm, d), _rows(tm, d), _rows(tm, gw), _rows(tm, gw)],
        compiler_params=_params("parallel"),
    )(*os_, *ls_, wot, x, g)


def _matmul_resnorm(a, w, x, g, *, name, bias=None, tm=512):
    s, k = a.shape
    d = w.shape[1]
    tm = min(tm, s)

    def body(*refs):
        a_ref, w_ref = refs[:2]
        b_ref = refs[2] if bias is not None else None
        x_ref, g_ref, y_ref, xo_ref = refs[-4:]
        y = jnp.dot(a_ref[...], w_ref[...], preferred_element_type=F32)
        if b_ref is not None:
            y = y + b_ref[...]
        _resnorm_store(y, x_ref, g_ref, y_ref, xo_ref)

    in_specs = [_rows(tm, k), _full((k, d))] + ([_full((1, d))] if bias is not None else []) + [_rows(tm, d), _full((1, d))]
    args = [a, w] + ([bias] if bias is not None else []) + [x, g]
    return pl.pallas_call(
        body, name=name, out_shape=[jax.ShapeDtypeStruct((s, d), F32)] * 2, grid=(s // tm,),
        in_specs=in_specs, out_specs=[_rows(tm, d)] * 2, compiler_params=_params("parallel"),
    )(*args)


def _conv3_taps(z, halo, first):
    row = lax.broadcasted_iota(jnp.int32, z.shape, 0)
    halo = halo * jnp.where(first, 0.0, 1.0)
    h6, h7 = halo[6:7, :], halo[7:8, :]
    z1 = jnp.where(row == 0, h7, pltpu.roll(z, 1, 0))
    z2 = jnp.where(row == 0, h6, jnp.where(row == 1, h7, pltpu.roll(z, 2, 0)))
    return z2, z1


def _ffn_cols(f):
    return _tile(f)


def _lane_chunks(width, fn):
    def step(k, carry):
        fn(pl.ds(pl.multiple_of(k * 128, 128), 128))
        return carry

    lax.fori_loop(0, width // 128, step, 0)


def _ffn_act(z, w_dw, b_dw, tm=256):
    s, f2 = z.shape
    f = f2 // 2
    tm = min(tm, s)
    tc = _ffn_cols(f)
    nfc = f // tc

    def body(zu, zg, hu, hg, wu, wg, bu, bg, o_ref):
        first = pl.program_id(0) == 0

        def chunk(cs):
            def conv(z_ref, h_ref, w_ref, b_ref):
                zc = z_ref[:, cs].astype(F32)
                z2, z1 = _conv3_taps(zc, h_ref[:, cs].astype(F32), first)
                return w_ref[0:1, cs] * z2 + w_ref[1:2, cs] * z1 + w_ref[2:3, cs] * zc + b_ref[:, cs]

            up, gate = conv(zu, hu, wu, bu), conv(zg, hg, wg, bg)
            o_ref[:, cs] = (gate * _sigmoid(gate) * up).astype(BF16)

        _lane_chunks(tc, chunk)

    hb = tm // 8
    tile = lambda off: pl.BlockSpec((tm, tc), lambda i, j: (i, off + j))
    halo = lambda off: pl.BlockSpec((8, tc), lambda i, j: (jnp.maximum(i * hb - 1, 0), off + j))
    prm = lambda rows, off: pl.BlockSpec((rows, tc), lambda i, j: (0, off + j))
    return pl.pallas_call(
        body, name="ffn_act", out_shape=jax.ShapeDtypeStruct((s, f), BF16), grid=(s // tm, nfc),
        in_specs=[tile(0), tile(nfc), halo(0), halo(nfc), prm(FFN_CONV, 0), prm(FFN_CONV, nfc), prm(1, 0), prm(1, nfc)],
        out_specs=pl.BlockSpec((tm, tc), lambda i, j: (i, j)), compiler_params=_params("parallel", "parallel"),
    )(z, z, z, z, w_dw, w_dw, b_dw, b_dw)


def _shifted_planes(ext_ref):
    rows = ext_ref.shape[1]
    for s in range(1, 8):
        ext_ref[s, 0:rows - 8, :] = ext_ref[0, s:s + rows - 8, :]


def _window(ext_ref, off, tm, cs):
    s = off % 8
    return ext_ref[s, off - s:off - s + tm, cs]


def _conv_taps(ext_ref, w_ref, offs, tm, out_ref):
    def chunk(cs):
        acc = w_ref[0:1, cs] * _window(ext_ref, offs[0], tm, cs)
        for j in range(1, len(offs)):
            acc = acc + w_ref[j:j + 1, cs] * _window(ext_ref, offs[j], tm, cs)
        out_ref[:, cs] = acc

    _lane_chunks(out_ref.shape[1], chunk)


def _glu_planes(ag_ref, halo_ref, ext_ref, first, c):
    hal = halo_ref[...].astype(F32)
    ext_ref[0, 0:CONV_HALO, :] = hal[:, :c] * _sigmoid(hal[:, c:]) * jnp.where(first, 0.0, 1.0)
    ag = ag_ref[...].astype(F32)
    ext_ref[0, CONV_HALO:, :] = ag[:, :c] * _sigmoid(ag[:, c:])
    _shifted_planes(ext_ref)


def _layernorm_stats(u1):
    mu = jnp.mean(u1, axis=-1, keepdims=True)
    cen = u1 - mu
    rstd = lax.rsqrt(jnp.mean(cen * cen, axis=-1, keepdims=True) + EPS)
    return cen * rstd, rstd


def _conv_mid(ag, w_dw, b_dw, ln_g, ln_b, tm=256):
    s, c2 = ag.shape
    c = c2 // 2
    tm = min(tm, s)

    def body(ag_ref, halo_ref, w_ref, b_ref, g_ref, bb_ref, o_ref, u1_ref, ext_ref):
        _glu_planes(ag_ref, halo_ref, ext_ref, pl.program_id(0) == 0, c)
        base = CONV_HALO - (CONV_KERNEL - 1)
        _conv_taps(ext_ref, w_ref, [base + j for j in range(CONV_KERNEL)], tm, u1_ref)
        xh, _ = _layernorm_stats(u1_ref[...] + b_ref[...])
        u2 = xh * g_ref[...] + bb_ref[...]
        o_ref[...] = (u2 * _sigmoid(u2)).astype(BF16)

    hb = tm // CONV_HALO
    return pl.pallas_call(
        body, name="conv_mid", out_shape=[jax.ShapeDtypeStruct((s, c), BF16), jax.ShapeDtypeStruct((s, c), F32)], grid=(s // tm,),
        in_specs=[_rows(tm, c2), pl.BlockSpec((CONV_HALO, c2), lambda i: (jnp.maximum(i * hb - 1, 0), 0)),
                  _full((CONV_KERNEL, c)), _full((1, c)), _full((1, c)), _full((1, c))],
        out_specs=[_rows(tm, c), _rows(tm, c)], scratch_shapes=[pltpu.VMEM((8, CONV_HALO + tm, c), F32)],
        compiler_params=_params("arbitrary"),
    )(ag, ag, w_dw, b_dw, ln_g, ln_b)


def _loss_grad(xo, target, tm=512):
    s, d = xo.shape
    tm = min(tm, s)

    def body(x_ref, t_ref, dx_ref, loss_ref):
        @pl.when(pl.program_id(0) == 0)
        def _():
            loss_ref[...] = jnp.zeros_like(loss_ref)

        err = x_ref[...] - t_ref[...]
        dx_ref[...] = err * (1.0 / d)
        loss_ref[...] += 0.5 * jnp.sum(jnp.mean(err * err, axis=-1, keepdims=True))

    return pl.pallas_call(
        body, name="loss_grad", out_shape=[jax.ShapeDtypeStruct((s, d), F32), jax.ShapeDtypeStruct((1, 128), F32)],
        grid=(s // tm,), in_specs=[_rows(tm, d)] * 2, out_specs=[_rows(tm, d), _full((1, 128))],
        compiler_params=_params("arbitrary"),
    )(xo, target)


def _postnorm_bwd(y, g, dxo, *, name, with_bias_grad=False, tm=512):
    s, d = y.shape
    tm = min(tm, s)

    def body(y_ref, g_ref, dx_ref, dy_ref, dg_ref, *rest):
        @pl.when(pl.program_id(0) == 0)
        def _():
            dg_ref[...] = jnp.zeros_like(dg_ref)
            for r_ in rest:
                r_[...] = jnp.zeros_like(r_)

        yv, dxo_v = y_ref[...], dx_ref[...]
        r = lax.rsqrt(jnp.mean(yv * yv, axis=-1, keepdims=True) + EPS)
        yh = yv * r
        dyh = dxo_v * g_ref[...]
        dy = r * (dyh - yh * jnp.mean(dyh * yh, axis=-1, keepdims=True))
        dy_ref[...] = dy.astype(BF16)
        dg_ref[...] += jnp.sum(dxo_v * yh, axis=0, keepdims=True)
        for r_ in rest:
            r_[...] += jnp.sum(dy, axis=0, keepdims=True)

    nacc = 2 if with_bias_grad else 1
    return pl.pallas_call(
        body, name=name, out_shape=[jax.ShapeDtypeStruct((s, d), BF16)] + [jax.ShapeDtypeStruct((1, d), F32)] * nacc,
        grid=(s // tm,), in_specs=[_rows(tm, d), _full((1, d)), _rows(tm, d)],
        out_specs=[_rows(tm, d)] + [_full((1, d))] * nacc, compiler_params=_params("arbitrary"),
    )(y, g, dxo)


def _matmul(gmat, w, *, name, out_dtype, transposed_w, tm=512):
    s, k = gmat.shape
    n = w.shape[0] if transposed_w else w.shape[1]
    tm = min(tm, s)

    def body(g_ref, w_ref, o_ref):
        if transposed_w:
            acc = lax.dot_general(g_ref[...], w_ref[...], NT, preferred_element_type=F32)
        else:
            acc = jnp.dot(g_ref[...], w_ref[...], preferred_element_type=F32)
        o_ref[...] = acc.astype(out_dtype)

    return pl.pallas_call(
        body, name=name, out_shape=jax.ShapeDtypeStruct((s, n), out_dtype), grid=(s // tm,),
        in_specs=[_rows(tm, k), _full(w.shape)], out_specs=_rows(tm, n), compiler_params=_params("parallel"),
    )(gmat, w)


def _matmul_prenorm_bwd(pieces, wt, x, g, dres, *, name, tm=256):
    s, d = x.shape
    tm = min(tm, s)
    np_ = len(pieces)

    def body(*refs):
        p_refs, w_refs = refs[:np_], refs[np_:2 * np_]
        x_ref, g_ref, r_ref, dx_ref, dg_ref = refs[2 * np_:]

        @pl.when(pl.program_id(0) == 0)
        def _():
            dg_ref[...] = jnp.zeros_like(dg_ref)

        dh = None
        for p_ref, w_ref in zip(p_refs, w_refs):
            t = jnp.dot(p_ref[...], w_ref[...], preferred_element_type=F32)
            dh = t if dh is None else dh + t
        xv = x_ref[...]
        r = lax.rsqrt(jnp.mean(xv * xv, axis=-1, keepdims=True) + EPS)
        xh = xv * r
        dyh = dh * g_ref[...]
        dx_ref[...] = r_ref[...] + r * (dyh - xh * jnp.mean(dyh * xh, axis=-1, keepdims=True))
        dg_ref[...] += jnp.sum(dh * xh, axis=0, keepdims=True)

    in_specs = []
    for _, c0, kc, _ in pieces:
        assert c0 % kc == 0
        in_specs.append(pl.BlockSpec((tm, kc), lambda i, _b=c0 // kc: (i, _b)))
    for _, _, kc, r0 in pieces:
        assert r0 % kc == 0
        in_specs.append(pl.BlockSpec((kc, d), lambda i, _b=r0 // kc: (_b, 0)))
    in_specs += [_rows(tm, d), _full((1, d)), _rows(tm, d)]
    return pl.pallas_call(
        body, name=name, out_shape=[jax.ShapeDtypeStruct((s, d), F32), jax.ShapeDtypeStruct((1, d), F32)],
        grid=(s // tm,), in_specs=in_specs, out_specs=[_rows(tm, d), _full((1, d))],
        compiler_params=_params("arbitrary"),
    )(*[p[0] for p in pieces], *[wt] * np_, x, g, dres)


def _weight_grad(a, gmat, *, name, a_col0=0, ka=None, out=None, out_shape=None, layer=0, row0=0, ts=1024):
    s = a.shape[0]
    ka = a.shape[1] if ka is None else ka
    n = gmat.shape[1]
    ts = min(ts, s)
    tka = _tile(ka, a_col0, row0)
    shape = out.shape if out is not None else out_shape
    nsteps = s // ts

    def body(a_ref, g_ref, *rest):
        o_ref, acc_ref = rest[-2:]
        i = pl.program_id(1)

        @pl.when(i == 0)
        def _():
            acc_ref[...] = jnp.zeros_like(acc_ref)

        acc_ref[...] += lax.dot_general(a_ref[...], g_ref[...], TN, preferred_element_type=F32)

        @pl.when(i == nsteps - 1)
        def _():
            o_ref[...] = acc_ref[...].astype(BF16)

    in_specs = [pl.BlockSpec((ts, tka), lambda k, i: (i, a_col0 // tka + k)), pl.BlockSpec((ts, n), lambda k, i: (i, 0))]
    args = [a, gmat]
    aliases = {}
    if out is not None:
        in_specs.append(ANY)
        args.append(out)
        aliases = {2: 0}
    return pl.pallas_call(
        body, name=name, out_shape=jax.ShapeDtypeStruct(shape, BF16), grid=(ka // tka, nsteps), in_specs=in_specs,
        out_specs=pl.BlockSpec((None, tka, n), lambda k, i: (layer, row0 // tka + k, 0)),
        scratch_shapes=[pltpu.VMEM((tka, n), F32)],
        input_output_aliases=aliases, compiler_params=_params("parallel", "arbitrary"),
    )(*args)


def _ffn_act_bwd(z, dact, w_dw, b_dw, tm=256):
    s, f2 = z.shape
    f = f2 // 2
    tm = min(tm, s)
    tc = _ffn_cols(f)
    nfc = f // tc

    def body(zu, zg, hu, hg, wu, wg, bu, bg, da_ref, du_ref, dgt_ref, dbu_ref, dbg_ref, dwu_ref, dwg_ref):
        i = pl.program_id(1)

        @pl.when(i == 0)
        def _():
            for r_ in (dbu_ref, dbg_ref, dwu_ref, dwg_ref):
                r_[...] = jnp.zeros_like(r_)

        def chunk(cs):
            def conv(z_ref, h_ref, w_ref, b_ref):
                zc = z_ref[:, cs].astype(F32)
                z2, z1 = _conv3_taps(zc, h_ref[:, cs].astype(F32), i == 0)
                return (z2, z1, zc), w_ref[0:1, cs] * z2 + w_ref[1:2, cs] * z1 + w_ref[2:3, cs] * zc + b_ref[:, cs]

            taps_u, up = conv(zu, hu, wu, bu)
            taps_g, gate = conv(zg, hg, wg, bg)
            da = da_ref[:, cs].astype(F32)
            sg = _sigmoid(gate)
            d_up = da * (gate * sg)
            d_gate = da * up * (sg * (1.0 + gate * (1.0 - sg)))
            du_ref[:, cs] = d_up.astype(BF16)
            dgt_ref[:, cs] = d_gate.astype(BF16)
            for dv, taps, db_ref, dw_ref in ((d_up, taps_u, dbu_ref, dwu_ref), (d_gate, taps_g, dbg_ref, dwg_ref)):
                db_ref[:, cs] += jnp.sum(dv, axis=0, keepdims=True)
                for k_, tap in enumerate(taps):
                    dw_ref[k_:k_ + 1, cs] += jnp.sum(dv * tap, axis=0, keepdims=True)

        _lane_chunks(tc, chunk)

    hb = tm // 8
    tile = lambda off: pl.BlockSpec((tm, tc), lambda j, i: (i, off + j))
    halo = lambda off: pl.BlockSpec((8, tc), lambda j, i: (jnp.maximum(i * hb - 1, 0), off + j))
    prm = lambda rows, off: pl.BlockSpec((rows, tc), lambda j, i: (0, off + j))
    acc = lambda rows: pl.BlockSpec((rows, tc), lambda j, i: (0, j))
    return pl.pallas_call(
        body, name="ffn_act_bwd",
        out_shape=[jax.ShapeDtypeStruct((s, f), BF16)] * 2 + [jax.ShapeDtypeStruct((1, f), F32)] * 2
        + [jax.ShapeDtypeStruct((FFN_CONV, f), F32)] * 2,
        grid=(nfc, s // tm),
        in_specs=[tile(0), tile(nfc), halo(0), halo(nfc), prm(FFN_CONV, 0), prm(FFN_CONV, nfc), prm(1, 0), prm(1, nfc), tile(0)],
        out_specs=[tile(0), tile(0), acc(1), acc(1), acc(FFN_CONV), acc(FFN_CONV)],
        compiler_params=_params("parallel", "arbitrary"),
    )(z, z, z, z, w_dw, w_dw, b_dw, b_dw, dact)


def _conv3_transpose(dug, w_dw, col0, tm=256):
    s, f = dug.shape
    tm = min(tm, s)
    tc = _ffn_cols(f)
    nfc = f // tc
    nrow = s // tm
    off = col0 // tc

    def body(d_ref, n_ref, w_ref, o_ref):
        keep_next = jnp.where(pl.program_id(0) == nrow - 1, 0.0, 1.0)

        def chunk(cs):
            dv = d_ref[:, cs].astype(F32)
            nxt = n_ref[:, cs].astype(F32) * keep_next
            n0, n1 = nxt[0:1, :], nxt[1:2, :]
            row = lax.broadcasted_iota(jnp.int32, dv.shape, 0)
            d1 = jnp.where(row == tm - 1, n0, pltpu.roll(dv, tm - 1, 0))
            d2 = jnp.where(row == tm - 1, n1, jnp.where(row == tm - 2, n0, pltpu.roll(dv, tm - 2, 0)))
            o_ref[:, cs] = (w_ref[2:3, cs] * dv + w_ref[1:2, cs] * d1 + w_ref[0:1, cs] * d2).astype(BF16)

        _lane_chunks(tc, chunk)

    hb = tm // 8
    return pl.pallas_call(
        body, name="conv3_transpose", out_shape=jax.ShapeDtypeStruct((s, f), BF16), grid=(nrow, nfc),
        in_specs=[pl.BlockSpec((tm, tc), lambda i, j: (i, j)),
                  pl.BlockSpec((8, tc), lambda i, j: (jnp.minimum((i + 1) * hb, s // 8 - 1), j)),
                  pl.BlockSpec((FFN_CONV, tc), lambda i, j: (0, off + j))],
        out_specs=pl.BlockSpec((tm, tc), lambda i, j: (i, j)), compiler_params=_params("parallel", "parallel"),
    )(dug, dug, w_dw)


def _conv_mid_bwd(ag, u1, du3, b_dw, ln_g, ln_b, tm=256):
    s, c2 = ag.shape
    c = c2 // 2
    tm = min(tm, s)

    def body(ag_ref, halo_ref, u1in_ref, du_ref, b_ref, g_ref, bb_ref, o_ref, dlg_ref, dlb_ref, db_ref, dw_ref, ext_ref, u1_ref):
        @pl.when(pl.program_id(0) == 0)
        def _():
            for r_ in (dlg_ref, dlb_ref, db_ref, dw_ref):
                r_[...] = jnp.zeros_like(r_)

        _glu_planes(ag_ref, halo_ref, ext_ref, pl.program_id(0) == 0, c)
        xh, rstd = _layernorm_stats(u1in_ref[...] + b_ref[...])
        u2 = xh * g_ref[...] + bb_ref[...]
        sg = _sigmoid(u2)
        du2 = du_ref[...] * (sg * (1.0 + u2 * (1.0 - sg)))
        dlg_ref[...] += jnp.sum(du2 * xh, axis=0, keepdims=True)
        dlb_ref[...] += jnp.sum(du2, axis=0, keepdims=True)
        dxh = du2 * g_ref[...]
        du1 = rstd * (dxh - jnp.mean(dxh, axis=-1, keepdims=True) - xh * jnp.mean(dxh * xh, axis=-1, keepdims=True))
        o_ref[...] = du1.astype(BF16)
        db_ref[...] += jnp.sum(du1, axis=0, keepdims=True)
        u1_ref[...] = du1
        base = CONV_HALO - (CONV_KERNEL - 1)

        def chunk(cs):
            dc = u1_ref[:, cs]
            for j in range(CONV_KERNEL):
                dw_ref[j:j + 1, cs] += jnp.sum(dc * _window(ext_ref, base + j, tm, cs), axis=0, keepdims=True)

        _lane_chunks(c, chunk)

    hb = tm // CONV_HALO
    vec = _full((1, c))
    return pl.pallas_call(
        body, name="conv_mid_bwd",
        out_shape=[jax.ShapeDtypeStruct((s, c), BF16)] + [jax.ShapeDtypeStruct((1, c), F32)] * 3
        + [jax.ShapeDtypeStruct((CONV_HALO, c), F32)],
        grid=(s // tm,),
        in_specs=[_rows(tm, c2), pl.BlockSpec((CONV_HALO, c2), lambda i: (jnp.maximum(i * hb - 1, 0), 0)), _rows(tm, c),
                  _rows(tm, c), vec, vec, vec],
        out_specs=[_rows(tm, c), vec, vec, vec, _full((CONV_HALO, c))],
        scratch_shapes=[pltpu.VMEM((8, CONV_HALO + tm, c), F32), pltpu.VMEM((tm, c), F32)],
        compiler_params=_params("arbitrary"),
    )(ag, ag, u1, du3, b_dw, ln_g, ln_b)


def _glu_conv_bwd(du1, ag, w_dw, tm=256):
    s, c = du1.shape
    tm = min(tm, s)
    nrow = s // tm

    def body(d_ref, n_ref, ag_ref, w_ref, o_ref, db_ref, ext_ref, du0_ref):
        @pl.when(pl.program_id(0) == 0)
        def _():
            db_ref[...] = jnp.zeros_like(db_ref)

        ext_ref[0, 0:tm, :] = d_ref[...].astype(F32)
        ext_ref[0, tm:, :] = n_ref[...].astype(F32) * jnp.where(pl.program_id(0) == nrow - 1, 0.0, 1.0)
        _shifted_planes(ext_ref)
        top = CONV_KERNEL - 1
        _conv_taps(ext_ref, w_ref, [top - j for j in range(CONV_KERNEL)], tm, du0_ref)
        du0 = du0_ref[...]
        ag = ag_ref[...].astype(F32)
        a, gt = ag[:, :c], ag[:, c:]
        sg = _sigmoid(gt)
        da = du0 * sg
        dgt = du0 * a * (sg * (1.0 - sg))
        o_ref[:, :c] = da.astype(BF16)
        o_ref[:, c:] = dgt.astype(BF16)
        db_ref[:, :c] += jnp.sum(da, axis=0, keepdims=True)
        db_ref[:, c:] += jnp.sum(dgt, axis=0, keepdims=True)

    hb = tm // CONV_HALO
    return pl.pallas_call(
        body, name="glu_conv_bwd",
        out_shape=[jax.ShapeDtypeStruct((s, 2 * c), BF16), jax.ShapeDtypeStruct((1, 2 * c), F32)], grid=(nrow,),
        in_specs=[_rows(tm, c), pl.BlockSpec((CONV_HALO, c), lambda i: (jnp.minimum((i + 1) * hb, s // CONV_HALO - 1), 0)),
                  _rows(tm, 2 * c), _full((CONV_KERNEL, c))],
        out_specs=[_rows(tm, 2 * c), _full((1, 2 * c))],
        scratch_shapes=[pltpu.VMEM((8, tm + CONV_HALO, c), F32), pltpu.VMEM((tm, c), F32)],
        compiler_params=_params("arbitrary"),
    )(du1, du1, ag, w_dw)


def _head_rows(v, mask):
    return jnp.max(jnp.where(mask, v, -jnp.inf), axis=-1, keepdims=True)


def _attn_bwd_dq(qv, dmix, mixed, lse, rope, grp, dil):
    l = qv.shape[0]
    s = l * dil
    nb = l // SPAN
    view = lambda t: t.reshape(l, dil * t.shape[1])

    def body(q_ref, kp_ref, kc_ref, vp_ref, vc_ref, do_ref, mx_ref, l_ref, c_ref, su_ref, sd_ref, o_ref):
        b = pl.program_id(1)
        row = lax.broadcasted_iota(jnp.int32, (SPAN, 2 * SPAN), 0)
        col = lax.broadcasted_iota(jnp.int32, (SPAN, 2 * SPAN), 1)
        no_prev = jnp.where(b > 0, 0, 4 * SPAN)
        valid = ((col < SPAN) & (col >= row + no_prev)) | ((col >= SPAN) & (col - SPAN <= row))
        masks, keep = _head_masks()
        for p in range(GROUP_WIDTH // 128):
            sl = slice(p * 128, (p + 1) * 128)
            qp, dop = q_ref[:, sl], do_ref[:, sl]
            kk = jnp.concatenate([kp_ref[:, sl], kc_ref[:, sl]], axis=0)
            vv = jnp.concatenate([vp_ref[:, sl], vc_ref[:, sl]], axis=0)
            prod = dop.astype(F32) * mx_ref[:, sl].astype(F32)
            lsep = l_ref[:, sl]
            dqs = []
            for h in range(2):
                qh, doh = qp * keep[h], dop * keep[h]
                sc = lax.dot_general(qh, kk, NT, preferred_element_type=F32) * (HEAD_DIM ** -0.5)
                pe = jnp.where(valid, jnp.exp(sc - _head_rows(lsep, masks[h])), 0.0)
                dp = lax.dot_general(doh, vv, NT, preferred_element_type=F32)
                dbar = jnp.sum(jnp.where(masks[h], prod, 0.0), axis=-1, keepdims=True)
                ds = pe * (dp - dbar) * (HEAD_DIM ** -0.5)
                dqs.append(jnp.dot(ds.astype(BF16), kk, preferred_element_type=F32))
            dq = jnp.where(masks[0], dqs[0], dqs[1])
            o_ref[:, sl] = _rope_transpose(dq, c_ref[...], su_ref[...], sd_ref[...]).astype(BF16)

    blk = (SPAN, GROUP_WIDTH)
    cur = lambda t: pl.BlockSpec(blk, lambda r, b: (b, r * 3 + t))
    prev = lambda t: pl.BlockSpec(blk, lambda r, b: (jnp.maximum(b - 1, 0), r * 3 + t))
    own = pl.BlockSpec(blk, lambda r, b: (b, r))
    tab = pl.BlockSpec((SPAN, 128), lambda r, b: (b, r))
    out = pl.pallas_call(
        body, name=f"attn_bwd_dq_g{grp}", out_shape=jax.ShapeDtypeStruct((l, dil * GROUP_WIDTH), BF16), grid=(dil, nb),
        in_specs=[cur(0), prev(1), cur(1), prev(2), cur(2), own, own, own, tab, tab, tab], out_specs=own,
        compiler_params=_params("parallel", "arbitrary"),
    )(qv, qv, qv, qv, qv, view(dmix), view(mixed), view(lse), *[view(t) for t in rope])
    return out.reshape(s, GROUP_WIDTH)


def _attn_bwd_dkv(qv, dmix, mixed, lse, rope, grp, dil):
    l = qv.shape[0]
    s = l * dil
    nb = l // SPAN
    view = lambda t: t.reshape(l, dil * t.shape[1])

    def body(k_ref, v_ref, qc_ref, qn_ref, doc_ref, don_ref, mc_ref, mn_ref, lc_ref, ln_ref,
             c_ref, su_ref, sd_ref, o_ref):
        b = pl.program_id(1)
        row = lax.broadcasted_iota(jnp.int32, (2 * SPAN, SPAN), 0)
        col = lax.broadcasted_iota(jnp.int32, (2 * SPAN, SPAN), 1)
        no_next = jnp.where(b < nb - 1, 0, 4 * SPAN)
        valid = ((row < SPAN) & (col <= row)) | ((row >= SPAN) & (col >= row - SPAN + no_next))
        masks, keep = _head_masks()
        masks2, _ = _head_masks(2 * SPAN)
        for p in range(GROUP_WIDTH // 128):
            sl = slice(p * 128, (p + 1) * 128)
            kp, vp = k_ref[:, sl], v_ref[:, sl]
            qq = jnp.concatenate([qc_ref[:, sl], qn_ref[:, sl]], axis=0)
            doo = jnp.concatenate([doc_ref[:, sl], don_ref[:, sl]], axis=0)
            mm = jnp.concatenate([mc_ref[:, sl], mn_ref[:, sl]], axis=0)
            ll = jnp.concatenate([lc_ref[:, sl], ln_ref[:, sl]], axis=0)
            prod = doo.astype(F32) * mm.astype(F32)
            dks, dvs = [], []
            for h in range(2):
                qh, doh = qq * keep[h], doo * keep[h]
                sc = lax.dot_general(qh, kp, NT, preferred_element_type=F32) * (HEAD_DIM ** -0.5)
                pe = jnp.where(valid, jnp.exp(sc - _head_rows(ll, masks2[h])), 0.0)
                dp = lax.dot_general(doh, vp, NT, preferred_element_type=F32)
                dbar = jnp.sum(jnp.where(masks2[h], prod, 0.0), axis=-1, keepdims=True)
                ds = pe * (dp - dbar) * (HEAD_DIM ** -0.5)
                dvs.append(lax.dot_general(pe.astype(BF16), doo, TN, preferred_element_type=F32))
                dks.append(lax.dot_general(ds.astype(BF16), qq, TN, preferred_element_type=F32))
            dk = jnp.where(masks[0], dks[0], dks[1])
            o_ref[:, sl] = _rope_transpose(dk, c_ref[...], su_ref[...], sd_ref[...]).astype(BF16)
            o_ref[:, GROUP_WIDTH + p * 128:GROUP_WIDTH + (p + 1) * 128] = jnp.where(masks[0], dvs[0], dvs[1]).astype(BF16)

    blk = (SPAN, GROUP_WIDTH)
    nxt_b = lambda b: jnp.minimum(b + 1, nb - 1)
    col_of = lambda t: pl.BlockSpec(blk, lambda r, b: (b, r * 3 + t))
    q_next = pl.BlockSpec(blk, lambda r, b: (nxt_b(b), r * 3))
    own = pl.BlockSpec(blk, lambda r, b: (b, r))
    own_next = pl.BlockSpec(blk, lambda r, b: (nxt_b(b), r))
    tab = pl.BlockSpec((SPAN, 128), lambda r, b: (b, r))
    dv_, mv, lv = view(dmix), view(mixed), view(lse)
    out = pl.pallas_call(
        body, name=f"attn_bwd_dkv_g{grp}", out_shape=jax.ShapeDtypeStruct((l, dil * 2 * GROUP_WIDTH), BF16), grid=(dil, nb),
        in_specs=[col_of(1), col_of(2), col_of(0), q_next, own, own_next, own, own_next, own, own_next, tab, tab, tab],
        out_specs=pl.BlockSpec((SPAN, 2 * GROUP_WIDTH), lambda r, b: (b, r)),
        compiler_params=_params("parallel", "arbitrary"),
    )(qv, qv, qv, qv, dv_, dv_, mv, mv, lv, lv, *[view(t) for t in rope])
    return out.reshape(s, 2 * GROUP_WIDTH)


def _rope_freq_row():
    half = ROT_DIM // 2
    inv = (ROPE_THETA ** (-np.arange(half, dtype=np.float32) / half)).astype(np.float32)
    row = np.zeros((1, 128), np.float32)
    for head in range(128 // HEAD_DIM):
        row[0, head * HEAD_DIM:head * HEAD_DIM + half] = inv
        row[0, head * HEAD_DIM + half:head * HEAD_DIM + ROT_DIM] = inv
    return jnp.asarray(row)


def _ffn_fwd(x, g_pre, g_post, w_up_t, w_dw, b_dw, w_down):
    h, z = _norm_matmul(x, g_pre, w_up_t, tn=_tile(w_up_t.shape[0]), name="ffn_up")
    act = _ffn_act(z, w_dw, b_dw)
    y, xo = _matmul_resnorm(act, w_down, x, g_post, name="ffn_down")
    return xo, (x, h, z, act, y)


def _ffn_bwd(saved, dxo, g_pre, g_post, w_up_t, w_dw, b_dw, w_down, layer, d_up_t, d_down):
    x, h, z, act, y = saved
    f = act.shape[1]
    d = x.shape[1]
    dy, dg_post = _postnorm_bwd(y, g_post, dxo, name="ffn_post_bwd")
    dact = _matmul(dy, w_down, name="ffn_dact", out_dtype=BF16, transposed_w=True)
    d_down = _weight_grad(act, dy, name="ffn_dw_down", out=d_down, out_shape=(2, f, d), layer=layer)
    dug_u, dug_g, db_u, db_g, dwd_u, dwd_g = _ffn_act_bwd(z, dact, w_dw, b_dw)
    dz_u = _conv3_transpose(dug_u, w_dw, 0)
    dz_g = _conv3_transpose(dug_g, w_dw, f)
    dx, dg_pre = _matmul_prenorm_bwd([(dz_u, 0, f, 0), (dz_g, 0, f, f)], w_up_t, x, g_pre, dxo, name="ffn_dx")
    d_up_t = _weight_grad(dz_u, h, name="ffn_dw_up", out=d_up_t, out_shape=(2, 2 * f, d), layer=layer)
    d_up_t = _weight_grad(dz_g, h, name="ffn_dw_up", out=d_up_t, layer=layer, row0=f)
    grads = dict(w_dw=jnp.concatenate([dwd_u, dwd_g], axis=1), b_dw=jnp.concatenate([db_u, db_g], axis=1),
                 g_pre=dg_pre, g_post=dg_post)
    return dx, grads, d_up_t, d_down


def _local_step(x, pos_col, target, p):
    ng = p["norm_g"]
    row = lambda r: ng[r:r + 1]
    rope = _rope_tables(pos_col, _rope_freq_row())
    d = x.shape[1]

    h0, *qkv = _qkv_proj(x, row(0), p["w_qkv_t"], rope)
    os_, ls_ = zip(*[_attn_fwd(qkv[g_], g_, d_) for g_, d_ in enumerate(DILATIONS)])
    y_a, x1, mixed, lse = _mix_wo(os_, ls_, p["w_o_t"], x, row(1))
    x2, ffn0 = _ffn_fwd(x1, row(2), row(3), p["w_up_t"][0], p["ffn_w_dw"][0], p["ffn_b_dw"][0], p["w_down"][0])
    h1, ag = _norm_matmul(x2, row(4), p["w_pw1_t"], tn=_tile(p["w_pw1_t"].shape[0]), name="conv_pw1", bias=p["b_pw1"])
    u3, u1 = _conv_mid(ag, p["conv_w_dw"], p["conv_b_dw"], p["ln_g"], p["ln_b"])
    y_c, x3 = _matmul_resnorm(u3, p["w_pw2"], x2, row(5), name="conv_pw2", bias=p["b_pw2"])
    x4, ffn1 = _ffn_fwd(x3, row(6), row(7), p["w_up_t"][1], p["ffn_w_dw"][1], p["ffn_b_dw"][1], p["w_down"][1])
    dx4, loss = _loss_grad(x4, target)

    dx3, gf1, d_up_t, d_down = _ffn_bwd(ffn1, dx4, row(6), row(7), p["w_up_t"][1], p["ffn_w_dw"][1], p["ffn_b_dw"][1],
                                        p["w_down"][1], 1, None, None)
    dy_c, dg5, db_pw2 = _postnorm_bwd(y_c, row(5), dx3, name="conv_post_bwd", with_bias_grad=True)
    du3 = _matmul(dy_c, p["w_pw2"], name="conv_du3", out_dtype=F32, transposed_w=True)
    d_wpw2 = _weight_grad(u3, dy_c, name="conv_dw_pw2", out_shape=(1, u3.shape[1], d))
    du1, d_lng, d_lnb, d_cbdw, d_cwdw = _conv_mid_bwd(ag, u1, du3, p["conv_b_dw"], p["ln_g"], p["ln_b"])
    dag, db_pw1 = _glu_conv_bwd(du1, ag, p["conv_w_dw"])
    dx2, dg4 = _matmul_prenorm_bwd([(dag, 0, dag.shape[1], 0)], p["w_pw1_t"], x2, row(4), dx3, name="conv_dx")
    d_wpw1_t = _weight_grad(dag, h1, name="conv_dw_pw1", out_shape=(1, dag.shape[1], d))
    dx1, gf0, d_up_t, d_down = _ffn_bwd(ffn0, dx2, row(2), row(3), p["w_up_t"][0], p["ffn_w_dw"][0], p["ffn_b_dw"][0],
                                        p["w_down"][0], 0, d_up_t, d_down)
    dy_a, dg1 = _postnorm_bwd(y_a, row(1), dx1, name="attn_post_bwd")
    dmix = _matmul(dy_a, p["w_o_t"], name="attn_dmix", out_dtype=BF16, transposed_w=False)
    d_wo_t = _weight_grad(dy_a, mixed, name="attn_dw_o", out_shape=(1, d, GROUP_WIDTH))
    pieces, d_wqkv_t = [], None
    for g_, d_ in enumerate(DILATIONS):
        dq = _attn_bwd_dq(qkv[g_], dmix, mixed, lse, rope, g_, d_)
        dkv = _attn_bwd_dkv(qkv[g_], dmix, mixed, lse, rope, g_, d_)
        for t, (arr, c0) in enumerate(((dq, 0), (dkv, 0), (dkv, GROUP_WIDTH))):
            r0 = (3 * t + g_) * GROUP_WIDTH
            pieces.append((arr, c0, GROUP_WIDTH, r0))
            d_wqkv_t = _weight_grad(arr, h0, name="attn_dw_qkv", a_col0=c0, ka=GROUP_WIDTH, out=d_wqkv_t,
                                    out_shape=(1, p["w_qkv_t"].shape[0], d), row0=r0)
    grad_x, dg0 = _matmul_prenorm_bwd(pieces, p["w_qkv_t"], x, row(0), dx1, name="attn_dx")

    grads = dict(
        norm_g=jnp.concatenate([dg0, dg1, gf0["g_pre"], gf0["g_post"], dg4, dg5, gf1["g_pre"], gf1["g_post"]], axis=0),
        w_qkv_t=d_wqkv_t, w_o_t=d_wo_t, w_pw1_t=d_wpw1_t, b_pw1=db_pw1,
        conv_w_dw=d_cwdw[:CONV_KERNEL], conv_b_dw=d_cbdw, ln_g=d_lng, ln_b=d_lnb, w_pw2=d_wpw2, b_pw2=db_pw2,
        w_up_t=d_up_t, ffn_w_dw=jnp.stack([gf0["w_dw"], gf1["w_dw"]]),
        ffn_b_dw=jnp.concatenate([gf0["b_dw"], gf1["b_dw"]], axis=0), w_down=d_down)
    return loss, grad_x, grads


SMALL_AXIS = dict(norm_g=2, conv_b_pw1=1, conv_w_dw=2, conv_b_dw=1, conv_ln_g=1, conv_ln_b=1, conv_b_pw2=1, ffn_w_dw=2)
SMALL = tuple(SMALL_AXIS)
MATMUL_WEIGHTS = dict(attn_w_qkv=True, conv_w_pw1=True, ffn_w_up=True, conv_w_pw2=False, ffn_w_down=False)


def _pack(arrays, cols, row_multiple):
    flat = jnp.concatenate([a.reshape(-1) for a in arrays])
    rows = -(-flat.shape[0] // cols)
    rows = -(-rows // row_multiple) * row_multiple
    return jnp.pad(flat, (0, rows * cols - flat.shape[0])).reshape(rows, cols)


def _unpack(packed, shapes):
    flat = packed.reshape(packed.shape[:-2] + (-1,))
    out, off = [], 0
    for shp in shapes:
        n = math.prod(shp)
        out.append(flat[..., off:off + n].reshape(packed.shape[:-2] + tuple(shp)))
        off += n
    return out


def _join_shards(stacked, axis):
    moved = jnp.moveaxis(stacked, 0, axis)
    shp = moved.shape
    return moved.reshape(shp[:axis] + (shp[axis] * shp[axis + 1],) + shp[axis + 2:])


def _split_shards(whole, axis):
    shp = whole.shape
    cut = whole.reshape(shp[:axis] + (N_DEV, shp[axis] // N_DEV) + shp[axis + 1:])
    return jnp.moveaxis(cut, axis, 0)


def _row_shard(w, transposed):
    t = jnp.swapaxes(w, 1, 2) if transposed else w
    return t.astype(BF16).reshape(-1, t.shape[-1])


def kernel(x, positions, norm_g, attn_w_qkv, attn_w_o, conv_w_pw1, conv_b_pw1, conv_w_dw, conv_b_dw, conv_ln_g, conv_ln_b, conv_w_pw2, conv_b_pw2, ffn_w_up, ffn_w_dw, ffn_b_dw, ffn_w_down, loss_target, m_norm_g, m_attn_w_qkv, m_attn_w_o, m_conv_w_pw1, m_conv_b_pw1, m_conv_w_dw, m_conv_b_dw, m_conv_ln_g, m_conv_ln_b, m_conv_w_pw2, m_conv_b_pw2, m_ffn_w_up, m_ffn_w_dw, m_ffn_b_dw, m_ffn_w_down, v_norm_g, v_attn_w_qkv, v_attn_w_o, v_conv_w_pw1, v_conv_b_pw1, v_conv_w_dw, v_conv_b_dw, v_conv_ln_g, v_conv_ln_b, v_conv_w_pw2, v_conv_b_pw2, v_ffn_w_up, v_ffn_w_dw, v_ffn_b_dw, v_ffn_w_down):
    w = dict(norm_g=norm_g, attn_w_qkv=attn_w_qkv, attn_w_o=attn_w_o, conv_w_pw1=conv_w_pw1, conv_b_pw1=conv_b_pw1,
             conv_w_dw=conv_w_dw, conv_b_dw=conv_b_dw, conv_ln_g=conv_ln_g, conv_ln_b=conv_ln_b, conv_w_pw2=conv_w_pw2,
             conv_b_pw2=conv_b_pw2, ffn_w_up=ffn_w_up, ffn_w_dw=ffn_w_dw, ffn_w_down=ffn_w_down)
    m = dict(norm_g=m_norm_g, attn_w_qkv=m_attn_w_qkv, attn_w_o=m_attn_w_o, conv_w_pw1=m_conv_w_pw1, conv_b_pw1=m_conv_b_pw1,
             conv_w_dw=m_conv_w_dw, conv_b_dw=m_conv_b_dw, conv_ln_g=m_conv_ln_g, conv_ln_b=m_conv_ln_b, conv_w_pw2=m_conv_w_pw2,
             conv_b_pw2=m_conv_b_pw2, ffn_w_up=m_ffn_w_up, ffn_w_dw=m_ffn_w_dw, ffn_w_down=m_ffn_w_down)
    v = dict(norm_g=v_norm_g, attn_w_qkv=v_attn_w_qkv, attn_w_o=v_attn_w_o, conv_w_pw1=v_conv_w_pw1, conv_b_pw1=v_conv_b_pw1,
             conv_w_dw=v_conv_w_dw, conv_b_dw=v_conv_b_dw, conv_ln_g=v_conv_ln_g, conv_ln_b=v_conv_ln_b, conv_w_pw2=v_conv_w_pw2,
             conv_b_pw2=v_conv_b_pw2, ffn_w_up=v_ffn_w_up, ffn_w_dw=v_ffn_w_dw, ffn_w_down=v_ffn_w_down)
    d = x.shape[-1]

    shares = [_row_shard(w[n], t) for n, t in MATMUL_WEIGHTS.items()]
    rows = [s_.shape[0] for s_ in shares]
    big = _all_gather(jnp.concatenate(shares, axis=0), "gather_matmul_weights")
    whole, r0 = {}, 0
    for (n, _), nr in zip(MATMUL_WEIGHTS.items(), rows):
        layers = w[n].shape[0]
        seg = big[:, r0:r0 + nr].reshape(N_DEV, layers, nr // layers, d)
        whole[n] = [seg[:, l_].reshape(-1, d) for l_ in range(layers)]
        r0 += nr
    w_o_t = _all_gather(_row_shard(attn_w_o, True), "gather_w_o").reshape(d, -1)
    small = _all_gather(_pack([w[n] for n in SMALL], 128, 8), "gather_small_weights")
    sm = {n: _join_shards(stacked, SMALL_AXIS[n])
          for n, stacked in zip(SMALL, _unpack(small, [w[n].shape for n in SMALL]))}
    p = dict(norm_g=sm["norm_g"].reshape(-1, d), w_qkv_t=whole["attn_w_qkv"][0], w_o_t=w_o_t,
             w_pw1_t=whole["conv_w_pw1"][0], b_pw1=sm["conv_b_pw1"], conv_w_dw=sm["conv_w_dw"][0],
             conv_b_dw=sm["conv_b_dw"], ln_g=sm["conv_ln_g"], ln_b=sm["conv_ln_b"], w_pw2=whole["conv_w_pw2"][0],
             b_pw2=sm["conv_b_pw2"], w_up_t=whole["ffn_w_up"], ffn_w_dw=sm["ffn_w_dw"],
             ffn_b_dw=[ffn_b_dw[0:1], ffn_b_dw[1:2]], w_down=whole["ffn_w_down"])

    loss, grad_x, g = _local_step(x[0], positions.reshape(-1, 1), loss_target[0], p)
    loss = lax.psum(loss[0, 0], ("x", "y", "c"))

    gsmall = dict(norm_g=g["norm_g"].reshape(norm_g.shape[0], 4, -1), conv_b_pw1=g["b_pw1"], conv_w_dw=g["conv_w_dw"][None],
                  conv_b_dw=g["conv_b_dw"], conv_ln_g=g["ln_g"], conv_ln_b=g["ln_b"], conv_b_pw2=g["b_pw2"], ffn_w_dw=g["ffn_w_dw"])
    small_contrib = jnp.concatenate([_split_shards(gsmall[n], SMALL_AXIS[n]).reshape(N_DEV, -1) for n in SMALL], axis=1)
    srows = small.shape[1]
    small_contrib = jnp.pad(small_contrib, ((0, 0), (0, srows * 128 - small_contrib.shape[1]))).reshape(1, N_DEV, srows, 128)
    big_names = ("w_qkv_t", "w_o_t", "w_pw1_t", "w_up_t", "w_pw2", "w_down")
    contribs = [g[n].reshape(g[n].shape[0], N_DEV, g[n].shape[1] // N_DEV, g[n].shape[2]) for n in big_names] + [small_contrib]
    core = lax.axis_index("c").astype(jnp.int32).reshape(1)
    got = _rs_sibling(contribs)
    dtypes = [BF16] * len(big_names) + [F32]
    chip_sums = _rs_chips([_rs_pair_add(c_, g_, core, dt) for c_, g_, dt in zip(contribs, got, dtypes)])

    outs = {}
    for n, gname, transposed in (("attn_w_qkv", "w_qkv_t", True), ("attn_w_o", "w_o_t", True), ("conv_w_pw1", "w_pw1_t", True),
                                 ("ffn_w_up", "w_up_t", True), ("conv_w_pw2", "w_pw2", False), ("ffn_w_down", "w_down", False)):
        gsum = _sum_parts(chip_sums[big_names.index(gname)], "sum_chips")
        gsum = jnp.swapaxes(gsum, 1, 2) if transposed else gsum
        outs[n] = (gsum, *_adamw(gsum, w[n], m[n], v[n], "adamw"))
    sshapes = [w[n].shape for n in SMALL]
    souts = _sum_adamw(chip_sums[-1][0], *[_pack([t[n] for n in SMALL], 128, 8) for t in (w, m, v)], name="sum_adamw_small")
    for n, vals in zip(SMALL, zip(*[_unpack(o, sshapes) for o in souts])):
        outs[n] = vals
    bparts = _all_gather(_pack([g["ffn_b_dw"]], 128, 8), "gather_bias_grads")
    bouts = _sum_adamw(bparts, *[_pack([t], 128, 8) for t in (ffn_b_dw, m_ffn_b_dw, v_ffn_b_dw)], name="sum_adamw_bias")
    outs["ffn_b_dw"] = tuple(_unpack(o, [ffn_b_dw.shape])[0] for o in bouts)

    order = ("norm_g", "attn_w_qkv", "attn_w_o", "conv_w_pw1", "conv_b_pw1", "conv_w_dw", "conv_b_dw", "conv_ln_g",
             "conv_ln_b", "conv_w_pw2", "conv_b_pw2", "ffn_w_up", "ffn_w_dw", "ffn_b_dw", "ffn_w_down")
    return (loss, grad_x[None], *[outs[n][0] for n in order], *[outs[n][1] for n in order],
            *[outs[n][2] for n in order], *[outs[n][3] for n in order])
```

```python
import functools
import math

import numpy as np
import jax
import jax.numpy as jnp
from jax import lax
from jax.experimental import pallas as pl
from jax.experimental.pallas import tpu as pltpu

F32 = jnp.float32
BF16 = jnp.bfloat16
EPS = 1e-6
N_DEV = 8
HEAD_DIM = 64
GROUP_WIDTH = 512
DILATIONS = (1, 4, 16)
SPAN = 128
ROT_DIM = 16
ROPE_THETA = 500000.0
CONV_KERNEL = 31
CONV_HALO = 32
FFN_CONV = 3
ADAM_LR, ADAM_B1, ADAM_B2, ADAM_EPS, ADAM_WD, ADAM_STEP = 0.001, 0.9, 0.999, 1e-08, 0.01, 10
VMEM_LIMIT_BYTES = 56 * 1024 * 1024
MESH = pl.DeviceIdType.MESH
ANY = pl.BlockSpec(memory_space=pl.ANY)
NT = (((1,), (1,)), ((), ()))
TN = (((0,), (0,)), ((), ()))


def _params(*sem):
    return pltpu.CompilerParams(dimension_semantics=sem, vmem_limit_bytes=VMEM_LIMIT_BYTES)


def _sigmoid(v):
    return 1.0 / (1.0 + jnp.exp(-v))


def _full(shape):
    return pl.BlockSpec(shape, lambda *_: (0,) * len(shape))


def _rows(tm, width):
    return pl.BlockSpec((tm, width), lambda i, *_: (i, 0))


def _tile(n, *multiples_of):
    for t in (1408, 1024, 512, 384, 256, 128):
        if n % t == 0 and all(o % t == 0 for o in multiples_of):
            return t
    raise ValueError((n, multiples_of))


def _all_gather(shard, name):
    r, c_ = shard.shape

    def body(x_ref, out_ref, send_sems, recv_sems, local_sem):
        x, y, c = lax.axis_index("x"), lax.axis_index("y"), lax.axis_index("c")
        me, sibling = (x, y, c), (x, y, 1 - c)
        chips = [(1 - x, y), (x, 1 - y), (1 - x, 1 - y)]

        def rows(px, py, pc):
            return out_ref.at[4 * px + 2 * py + pc]

        def copy(k, block, to, src=None):
            return pltpu.make_async_remote_copy(
                src_ref=rows(*block) if src is None else src, dst_ref=rows(*block),
                send_sem=send_sems.at[k], recv_sem=recv_sems.at[k], device_id=to, device_id_type=MESH)

        mine = pltpu.make_async_copy(x_ref, rows(*me), local_sem)
        mine.start()
        first = [copy(0, me, sibling, src=x_ref)]
        first += [copy(1 + j, me, (*chip, c), src=x_ref) for j, chip in enumerate(chips)]
        for cp in first:
            cp.start()
        passed = [copy(4 + j, (*chip, c), sibling) for j, chip in enumerate(chips)]
        for j, chip in enumerate(chips):
            copy(1 + j, (*chip, c), me).wait_recv()
            passed[j].start()
        copy(0, sibling, me).wait_recv()
        for j, chip in enumerate(chips):
            copy(4 + j, (*chip, 1 - c), me).wait_recv()
        for cp in first + passed:
            cp.wait_send()
        mine.wait()

    return pl.pallas_call(
        body, name=name, out_shape=jax.ShapeDtypeStruct((N_DEV, r, c_), shard.dtype),
        in_specs=[ANY], out_specs=ANY,
        scratch_shapes=[pltpu.SemaphoreType.DMA((7,)), pltpu.SemaphoreType.DMA((7,)), pltpu.SemaphoreType.DMA],
    )(shard)


HBM = pl.BlockSpec(memory_space=pltpu.HBM)
SEM = pl.BlockSpec(memory_space=pltpu.SEMAPHORE)
SIDE_EFFECT = pltpu.CompilerParams(has_side_effects=pltpu.SideEffectType.DATAFLOW_SIDE_EFFECTING)


def _gather_start(shard):
    r, c_ = shard.shape

    def body(x_ref, land_ref, send_sems, recv_sems, x_thru, land_thru, token):
        x, y, c = lax.axis_index("x"), lax.axis_index("y"), lax.axis_index("c")
        me = 4 * x + 2 * y + c
        for k in range(1, N_DEV):
            peer = (1 - x if k & 4 else x, 1 - y if k & 2 else y, 1 - c if k & 1 else c)
            pltpu.make_async_remote_copy(src_ref=x_ref, dst_ref=land_ref.at[me], send_sem=send_sems.at[k - 1],
                                         recv_sem=recv_sems.at[k - 1], device_id=peer, device_id_type=MESH).start()
        token[...] = jnp.zeros_like(token)

    land = pltpu.with_memory_space_constraint(lax.empty((N_DEV, r, c_), shard.dtype), pltpu.HBM)
    return pl.pallas_call(
        body, name="gather_late_weights_start",
        out_shape=(pltpu.SemaphoreType.DMA((N_DEV - 1,)), pltpu.SemaphoreType.DMA((N_DEV - 1,)),
                   pltpu.HBM(shard.shape, shard.dtype), pltpu.HBM((N_DEV, r, c_), shard.dtype),
                   jax.ShapeDtypeStruct((8, 128), F32)),
        in_specs=(HBM, HBM), out_specs=(SEM, SEM, HBM, HBM, pl.BlockSpec(memory_space=pltpu.VMEM)),
        input_output_aliases={0: 2, 1: 3}, compiler_params=SIDE_EFFECT,
    )(pltpu.with_memory_space_constraint(shard, pltpu.HBM), land)


def _gather_wait(send_sems, recv_sems, shard_thru, land_thru, after):
    def body(x_ref, land_ref, send_sems, recv_sems, after_ref, x_dead, got_ref):
        x, y, c = lax.axis_index("x"), lax.axis_index("y"), lax.axis_index("c")
        for k in range(N_DEV - 1):
            copy = pltpu.make_async_remote_copy(src_ref=x_ref, dst_ref=land_ref.at[0], send_sem=send_sems.at[k],
                                                recv_sem=recv_sems.at[k], device_id=(x, y, c), device_id_type=MESH)
            copy.wait_send()
            copy.wait_recv()

    return pl.pallas_call(
        body, name="gather_late_weights_wait",
        out_shape=(pltpu.HBM(shard_thru.shape, shard_thru.dtype), pltpu.HBM(land_thru.shape, land_thru.dtype)),
        in_specs=(HBM, HBM, SEM, SEM, ANY), out_specs=(HBM, HBM), input_output_aliases={0: 0, 1: 1},
        compiler_params=SIDE_EFFECT,
    )(shard_thru, land_thru, send_sems, recv_sems, after)[1]


def _with_rows(g, n):
    return jax.ShapeDtypeStruct((g.shape[0], n) + tuple(g.shape[2:]), g.dtype)


def _rs_sibling(gs):
    n = len(gs)

    def body(*refs):
        g_refs, o_refs, (send_sems, recv_sems) = refs[:n], refs[n:2 * n], refs[2 * n:]
        x, y, c = lax.axis_index("x"), lax.axis_index("y"), lax.axis_index("c")
        copies = [pltpu.make_async_remote_copy(
            src_ref=g_refs[w].at[:, 2 * q + (1 - c)], dst_ref=o_refs[w].at[:, q], send_sem=send_sems.at[4 * w + q],
            recv_sem=recv_sems.at[4 * w + q], device_id=(x, y, 1 - c), device_id_type=MESH)
            for w in range(n) for q in range(4)]
        for cp in copies:
            cp.start()
        for cp in copies:
            cp.wait_recv()
        for cp in copies:
            cp.wait_send()

    return pl.pallas_call(
        body, name="rs_sibling", out_shape=[_with_rows(g, 4) for g in gs],
        in_specs=[ANY] * n, out_specs=[ANY] * n,
        scratch_shapes=[pltpu.SemaphoreType.DMA((4 * n,)), pltpu.SemaphoreType.DMA((4 * n,))],
    )(*gs)


def _rs_pair_add(g, got, core, out_dtype):
    l, _, r, c_ = g.shape

    def body(core_ref, g_ref, got_ref, o_ref):
        o_ref[...] = (g_ref[...].astype(F32) + got_ref[...].astype(F32)).astype(out_dtype)

    blk = (None, None, r, c_)
    return pl.pallas_call(
        body, name="rs_pair_add", out_shape=jax.ShapeDtypeStruct((l, 4, r, c_), out_dtype),
        grid_spec=pltpu.PrefetchScalarGridSpec(
            num_scalar_prefetch=1, grid=(l, 4),
            in_specs=[pl.BlockSpec(blk, lambda i, q, core_ref: (i, 2 * q + core_ref[0], 0, 0)),
                      pl.BlockSpec(blk, lambda i, q, core_ref: (i, q, 0, 0))],
            out_specs=pl.BlockSpec(blk, lambda i, q, core_ref: (i, q, 0, 0))),
        compiler_params=_params("parallel", "parallel"),
    )(core, g, got)


def _rs_chips(parts):
    n = len(parts)

    def body(*refs):
        p_refs, o_refs, (send_sems, recv_sems, local_sems) = refs[:n], refs[n:2 * n], refs[2 * n:]
        x, y, c = lax.axis_index("x"), lax.axis_index("y"), lax.axis_index("c")
        my_chip = 2 * x + y
        chips = [(1 - x, y), (x, 1 - y), (1 - x, 1 - y)]
        local = [pltpu.make_async_copy(p_refs[w].at[:, my_chip], o_refs[w].at[:, my_chip], local_sems.at[w]) for w in range(n)]
        for cp in local:
            cp.start()
        copies = [pltpu.make_async_remote_copy(
            src_ref=p_refs[w].at[:, 2 * qx + qy], dst_ref=o_refs[w].at[:, my_chip], send_sem=send_sems.at[3 * w + k],
            recv_sem=recv_sems.at[3 * w + k], device_id=(qx, qy, c), device_id_type=MESH)
            for w in range(n) for k, (qx, qy) in enumerate(chips)]
        for cp in copies:
            cp.start()
        for cp in copies:
            cp.wait_recv()
        for cp in copies:
            cp.wait_send()
        for cp in local:
            cp.wait()

    return pl.pallas_call(
        body, name="rs_chips", out_shape=[jax.ShapeDtypeStruct(p.shape, p.dtype) for p in parts],
        in_specs=[ANY] * n, out_specs=[ANY] * n,
        scratch_shapes=[pltpu.SemaphoreType.DMA((3 * n,)), pltpu.SemaphoreType.DMA((3 * n,)), pltpu.SemaphoreType.DMA((n,))],
    )(*parts)


def _sum_parts(parts, name):
    l, n, r, c_ = parts.shape

    def body(p_ref, o_ref):
        g = p_ref[0].astype(F32)
        for s in range(1, n):
            g = g + p_ref[s].astype(F32)
        o_ref[...] = g

    return pl.pallas_call(
        body, name=name, out_shape=jax.ShapeDtypeStruct((l, r, c_), F32), grid=(l,),
        in_specs=[pl.BlockSpec((None, n, r, c_), lambda i: (i, 0, 0, 0))],
        out_specs=pl.BlockSpec((None, r, c_), lambda i: (i, 0, 0)), compiler_params=_params("parallel"),
    )(parts)


def _adamw_math(w, g, m, v):
    m = ADAM_B1 * m + (1.0 - ADAM_B1) * g
    v = ADAM_B2 * v + (1.0 - ADAM_B2) * (g * g)
    m_hat = m / (1.0 - ADAM_B1 ** ADAM_STEP)
    v_hat = v / (1.0 - ADAM_B2 ** ADAM_STEP)
    delta = -ADAM_LR * (m_hat / (jnp.sqrt(v_hat) + ADAM_EPS) + ADAM_WD * w)
    return delta, m, v


def _adamw(g, w, m, v, name):
    l, k, n = w.shape
    tk = 256 if k % 256 == 0 else k

    def body(g_ref, w_ref, m_ref, v_ref, d_ref, nm_ref, nv_ref):
        d_ref[...], nm_ref[...], nv_ref[...] = _adamw_math(w_ref[...], g_ref[...], m_ref[...], v_ref[...])

    spec = pl.BlockSpec((None, tk, n), lambda i, j: (i, j, 0))
    return pl.pallas_call(
        body, name=name, out_shape=[jax.ShapeDtypeStruct((l, k, n), F32)] * 3, grid=(l, k // tk),
        in_specs=[spec] * 4, out_specs=[spec] * 3, compiler_params=_params("parallel", "parallel"),
    )(g, w, m, v)


def _sum_adamw(parts, w, m, v, name):
    n, r, c_ = parts.shape

    def body(p_ref, w_ref, m_ref, v_ref, g_ref, d_ref, nm_ref, nv_ref):
        g = p_ref[0]
        for s in range(1, n):
            g = g + p_ref[s]
        g_ref[...] = g
        d_ref[...], nm_ref[...], nv_ref[...] = _adamw_math(w_ref[...], g, m_ref[...], v_ref[...])

    return pl.pallas_call(
        body, name=name, out_shape=[jax.ShapeDtypeStruct((r, c_), F32)] * 4, grid=(1,),
        in_specs=[_full((n, r, c_))] + [_full((r, c_))] * 3, out_specs=[_full((r, c_))] * 4,
        compiler_params=_params("arbitrary"),
    )(parts, w, m, v)


def _rope_tables(pos_col, freq_row):
    s = pos_col.shape[0]
    tm = min(1024, s)

    def body(p_ref, f_ref, c_ref, su_ref, sd_ref):
        ang = p_ref[...].astype(F32) * f_ref[...]
        lane = lax.broadcasted_iota(jnp.int32, ang.shape, 1) & (HEAD_DIM - 1)
        cs, sn = jnp.cos(ang), jnp.sin(ang)
        c_ref[...] = jnp.where(lane < ROT_DIM, cs, 1.0)
        su_ref[...] = jnp.where((lane >= ROT_DIM // 2) & (lane < ROT_DIM), sn, 0.0)
        sd_ref[...] = jnp.where(lane < ROT_DIM // 2, -sn, 0.0)

    return pl.pallas_call(
        body, name="rope_tables", out_shape=[jax.ShapeDtypeStruct((s, 128), F32)] * 3, grid=(s // tm,),
        in_specs=[pl.BlockSpec((tm, 1), lambda i: (i, 0)), _full((1, 128))],
        out_specs=[_rows(tm, 128)] * 3, compiler_params=_params("parallel"),
    )(pos_col, freq_row)


def _rope_apply(t, cos, sin_up, sin_dn):
    w = t.shape[1]
    return t * cos + pltpu.roll(t, 8, 1) * sin_up + pltpu.roll(t, w - 8, 1) * sin_dn


def _rope_transpose(dr, cos, sin_up, sin_dn):
    w = dr.shape[1]
    return dr * cos + pltpu.roll(dr * sin_up, w - 8, 1) + pltpu.roll(dr * sin_dn, 8, 1)


def _norm_matmul(x, g, wt, *, tn, name, bias=None, rope=None, rope_blocks=0, tm=512):
    s, d = x.shape
    n = wt.shape[0]
    tm = min(tm, s)

    def body(*refs):
        x_ref, g_ref, w_ref = refs[:3]
        k = 3
        b_ref = None
        if bias is not None:
            b_ref = refs[k]
            k += 1
        if rope is not None:
            c_ref, su_ref, sd_ref = refs[k:k + 3]
            k += 3
        h_ref, o_ref = refs[k:k + 2]
        j = pl.program_id(1)

        @pl.when(j == 0)
        def _():
            xv = x_ref[...]
            r = lax.rsqrt(jnp.mean(xv * xv, axis=-1, keepdims=True) + EPS)
            h_ref[...] = (xv * r * g_ref[...]).astype(BF16)

        acc = lax.dot_general(h_ref[...], w_ref[...], NT, preferred_element_type=F32)
        if b_ref is not None:
            acc = acc + b_ref[...]
        if rope is None:
            o_ref[...] = acc.astype(BF16)
        else:
            @pl.when(j < rope_blocks)
            def _():
                reps = tn // 128
                o_ref[...] = _rope_apply(acc, jnp.tile(c_ref[...], (1, reps)), jnp.tile(su_ref[...], (1, reps)),
                                         jnp.tile(sd_ref[...], (1, reps))).astype(BF16)

            @pl.when(j >= rope_blocks)
            def _():
                o_ref[...] = acc.astype(BF16)

    in_specs = [_rows(tm, d), _full((1, d)), pl.BlockSpec((tn, d), lambda i, j: (j, 0))]
    args = [x, g, wt]
    if bias is not None:
        in_specs.append(pl.BlockSpec((1, tn), lambda i, j: (0, j)))
        args.append(bias)
    if rope is not None:
        in_specs += [_rows(tm, 128)] * 3
        args += list(rope)
    return pl.pallas_call(
        body, name=name,
        out_shape=[jax.ShapeDtypeStruct((s, d), BF16), jax.ShapeDtypeStruct((s, n), BF16)],
        grid=(s // tm, n // tn), in_specs=in_specs,
        out_specs=[_rows(tm, d), pl.BlockSpec((tm, tn), lambda i, j: (i, j))],
        compiler_params=_params("parallel", "arbitrary"),
    )(*args)


def _class_major(tm, dil):
    p = np.zeros((tm, tm), np.float32)
    per = tm // dil
    for r in range(dil):
        for j in range(per):
            p[r * per + j, j * dil + r] = 1.0
    return jnp.asarray(p, dtype=BF16)


def _qkv_proj(x, g, wt, rope, tm=512):
    s, d = x.shape
    n = wt.shape[0]
    gw3 = 3 * GROUP_WIDTH
    tm = min(tm, s)
    assert n == 3 * gw3

    def body(x_ref, g_ref, w_ref, c_ref, su_ref, sd_ref, p1_ref, p2_ref, h_ref, o0_ref, o1_ref, o2_ref):
        j = pl.program_id(1)

        @pl.when(j == 0)
        def _():
            xv = x_ref[...]
            r = lax.rsqrt(jnp.mean(xv * xv, axis=-1, keepdims=True) + EPS)
            h_ref[...] = (xv * r * g_ref[...]).astype(BF16)

        acc = lax.dot_general(h_ref[...], w_ref[...], NT, preferred_element_type=F32)

        def store(y):
            yb = y.astype(BF16)
            o0_ref[:, pl.ds(pl.multiple_of(j * GROUP_WIDTH, GROUP_WIDTH), GROUP_WIDTH)] = yb[:, :GROUP_WIDTH]
            for grp, o_ref, p_ref in ((1, o1_ref, p1_ref), (2, o2_ref, p2_ref)):
                dil = DILATIONS[grp]
                per = tm // dil
                yp = jnp.dot(p_ref[...], yb[:, grp * GROUP_WIDTH:(grp + 1) * GROUP_WIDTH],
                             preferred_element_type=F32).astype(BF16)
                for r in range(dil):
                    col = pl.multiple_of(r * gw3 + j * GROUP_WIDTH, GROUP_WIDTH)
                    o_ref[:, pl.ds(col, GROUP_WIDTH)] = yp[r * per:(r + 1) * per, :]

        @pl.when(j < 2)
        def _():
            reps = gw3 // 128
            store(_rope_apply(acc, jnp.tile(c_ref[...], (1, reps)), jnp.tile(su_ref[...], (1, reps)),
                              jnp.tile(sd_ref[...], (1, reps))))

        @pl.when(j == 2)
        def _():
            store(acc)

    outs = [jax.ShapeDtypeStruct((s, d), BF16)] + [jax.ShapeDtypeStruct((s // dl, dl * gw3), BF16) for dl in DILATIONS]
    out_specs = [_rows(tm, d)] + [_rows(tm // dl, dl * gw3) for dl in DILATIONS]
    return pl.pallas_call(
        body, name="attn_qkv", out_shape=outs, grid=(s // tm, 3),
        in_specs=[_rows(tm, d), _full((1, d)), pl.BlockSpec((gw3, d), lambda i, j: (j, 0))] + [_rows(tm, 128)] * 3
        + [_full((tm, tm))] * 2,
        out_specs=out_specs, compiler_params=_params("parallel", "arbitrary"),
    )(x, g, wt, *rope, _class_major(tm, DILATIONS[1]), _class_major(tm, DILATIONS[2]))


def _head_masks(rows=SPAN):
    lane = lax.broadcasted_iota(jnp.int32, (rows, 128), 1)
    masks = [lane < HEAD_DIM, lane >= HEAD_DIM]
    lane1 = lax.broadcasted_iota(jnp.int32, (1, 128), 1)
    keep = [jnp.where(lane1 < HEAD_DIM, 1.0, 0.0).astype(BF16), jnp.where(lane1 >= HEAD_DIM, 1.0, 0.0).astype(BF16)]
    return masks, keep


def _attn_fwd(qv, grp, dil):
    l = qv.shape[0]
    s = l * dil
    nb = l // SPAN

    def body(q_ref, kp_ref, kc_ref, vp_ref, vc_ref, o_ref, l_ref):
        b = pl.program_id(1)
        row = lax.broadcasted_iota(jnp.int32, (SPAN, 2 * SPAN), 0)
        col = lax.broadcasted_iota(jnp.int32, (SPAN, 2 * SPAN), 1)
        no_prev = jnp.where(b > 0, 0, 4 * SPAN)
        valid = ((col < SPAN) & (col >= row + no_prev)) | ((col >= SPAN) & (col - SPAN <= row))
        masks, keep = _head_masks()
        for p in range(GROUP_WIDTH // 128):
            sl = slice(p * 128, (p + 1) * 128)
            qp = q_ref[:, sl]
            kk = jnp.concatenate([kp_ref[:, sl], kc_ref[:, sl]], axis=0)
            vv = jnp.concatenate([vp_ref[:, sl], vc_ref[:, sl]], axis=0)
            outs, lses = [], []
            for h in range(2):
                sc = lax.dot_general(qp * keep[h], kk, NT, preferred_element_type=F32) * (HEAD_DIM ** -0.5)
                sc = jnp.where(valid, sc, -1e30)
                mx = jnp.max(sc, axis=-1, keepdims=True)
                pe = jnp.exp(sc - mx)
                den = jnp.sum(pe, axis=-1, keepdims=True)
                pv = jnp.dot(pe.astype(BF16), vv, preferred_element_type=F32)
                outs.append(pv / den)
                lses.append(jnp.broadcast_to(mx + jnp.log(den), (SPAN, 128)))
            o_ref[:, sl] = jnp.where(masks[0], outs[0], outs[1])
            l_ref[:, sl] = jnp.where(masks[0], lses[0], lses[1])

    blk = (SPAN, GROUP_WIDTH)
    cur = lambda t: pl.BlockSpec(blk, lambda r, b: (b, r * 3 + t))
    prev = lambda t: pl.BlockSpec(blk, lambda r, b: (jnp.maximum(b - 1, 0), r * 3 + t))
    out = pl.BlockSpec(blk, lambda r, b: (b, r))
    o, lse = pl.pallas_call(
        body, name=f"attn_fwd_g{grp}", out_shape=[jax.ShapeDtypeStruct((l, dil * GROUP_WIDTH), F32)] * 2,
        grid=(dil, nb), in_specs=[cur(0), prev(1), cur(1), prev(2), cur(2)], out_specs=[out, out],
        compiler_params=_params("parallel", "arbitrary"),
    )(qv, qv, qv, qv, qv)
    return o.reshape(s, GROUP_WIDTH), lse.reshape(s, GROUP_WIDTH)


def _resnorm_store(y, x_ref, g_ref, y_ref, xo_ref):
    r = lax.rsqrt(jnp.mean(y * y, axis=-1, keepdims=True) + EPS)
    y_ref[...] = y
    xo_ref[...] = x_ref[...] + y * r * g_ref[...]


def _mix_wo(os_, ls_, wot, x, g, tm=256):
    s, d = x.shape
    gw = wot.shape[1]
    tm = min(tm, s)

    def body(o0, o1, o2, l0, l1, l2, w_ref, x_ref, g_ref, y_ref, xo_ref, mixed_ref, lse_ref):
        a0, a1, a2 = l0[...], l1[...], l2[...]
        mx = jnp.maximum(jnp.maximum(a0, a1), a2)
        e0, e1, e2 = jnp.exp(a0 - mx), jnp.exp(a1 - mx), jnp.exp(a2 - mx)
        den = e0 + e1 + e2
        mixed = (e0 / den) * o0[...] + (e1 / den) * o1[...] + (e2 / den) * o2[...]
        mixed_ref[...] = mixed.astype(BF16)
        lse_ref[...] = mx + jnp.log(den)
        y = lax.dot_general(mixed.astype(BF16), w_ref[...], NT, preferred_element_type=F32)
        _resnorm_store(y, x_ref, g_ref, y_ref, xo_ref)

    return pl.pallas_call(
        body, name="mix_wo",
        out_shape=[jax.ShapeDtypeStruct((s, d), F32), jax.ShapeDtypeStruct((s, d), F32),
                   jax.ShapeDtypeStruct((s, gw), BF16), jax.ShapeDtypeStruct((s, gw), F32)],
        grid=(s // tm,), in_specs=[_rows(tm, gw)] * 6 + [_full((d, gw)), _rows(tm, d), _full((1, d))],
        out_specs=[_rows(tm, d), _rows(tm, d), _rows(tm, gw), _rows(tm, gw)],
        compiler_params=_params("parallel"),
    )(*os_, *ls_, wot, x, g)


def _matmul_resnorm(a, w, x, g, *, name, bias=None, tm=512):
    s, k = a.shape
    d = w.shape[1]
    tm = min(tm, s)

    def body(*refs):
        a_ref, w_ref = refs[:2]
        b_ref = refs[2] if bias is not None else None
        x_ref, g_ref, y_ref, xo_ref = refs[-4:]
        y = jnp.dot(a_ref[...], w_ref[...], preferred_element_type=F32)
        if b_ref is not None:
            y = y + b_ref[...]
        _resnorm_store(y, x_ref, g_ref, y_ref, xo_ref)

    in_specs = [_rows(tm, k), _full((k, d))] + ([_full((1, d))] if bias is not None else []) + [_rows(tm, d), _full((1, d))]
    args = [a, w] + ([bias] if bias is not None else []) + [x, g]
    return pl.pallas_call(
        body, name=name, out_shape=[jax.ShapeDtypeStruct((s, d), F32)] * 2, grid=(s // tm,),
        in_specs=in_specs, out_specs=[_rows(tm, d)] * 2, compiler_params=_params("parallel"),
    )(*args)


def _conv3_taps(z, halo, first):
    row = lax.broadcasted_iota(jnp.int32, z.shape, 0)
    halo = halo * jnp.where(first, 0.0, 1.0)
    h6, h7 = halo[6:7, :], halo[7:8, :]
    z1 = jnp.where(row == 0, h7, pltpu.roll(z, 1, 0))
    z2 = jnp.where(row == 0, h6, jnp.where(row == 1, h7, pltpu.roll(z, 2, 0)))
    return z2, z1


def _ffn_cols(f):
    return _tile(f)


def _lane_chunks(width, fn):
    def step(k, carry):
        fn(pl.ds(pl.multiple_of(k * 128, 128), 128))
        return carry

    lax.fori_loop(0, width // 128, step, 0)


def _ffn_act(z, w_dw, b_dw, tm=256):
    s, f2 = z.shape
    f = f2 // 2
    tm = min(tm, s)
    tc = _ffn_cols(f)
    nfc = f // tc

    def body(zu, zg, hu, hg, wu, wg, bu, bg, o_ref):
        first = pl.program_id(0) == 0

        def chunk(cs):
            def conv(z_ref, h_ref, w_ref, b_ref):
                zc = z_ref[:, cs].astype(F32)
                z2, z1 = _conv3_taps(zc, h_ref[:, cs].astype(F32), first)
                return w_ref[0:1, cs] * z2 + w_ref[1:2, cs] * z1 + w_ref[2:3, cs] * zc + b_ref[:, cs]

            up, gate = conv(zu, hu, wu, bu), conv(zg, hg, wg, bg)
            o_ref[:, cs] = (gate * _sigmoid(gate) * up).astype(BF16)

        _lane_chunks(tc, chunk)

    hb = tm // 8
    tile = lambda off: pl.BlockSpec((tm, tc), lambda i, j: (i, off + j))
    halo = lambda off: pl.BlockSpec((8, tc), lambda i, j: (jnp.maximum(i * hb - 1, 0), off + j))
    prm = lambda rows, off: pl.BlockSpec((rows, tc), lambda i, j: (0, off + j))
    return pl.pallas_call(
        body, name="ffn_act", out_shape=jax.ShapeDtypeStruct((s, f), BF16), grid=(s // tm, nfc),
        in_specs=[tile(0), tile(nfc), halo(0), halo(nfc), prm(FFN_CONV, 0), prm(FFN_CONV, nfc), prm(1, 0), prm(1, nfc)],
        out_specs=pl.BlockSpec((tm, tc), lambda i, j: (i, j)), compiler_params=_params("parallel", "parallel"),
    )(z, z, z, z, w_dw, w_dw, b_dw, b_dw)


def _shifted_planes(ext_ref):
    rows = ext_ref.shape[1]
    for s in range(1, 8):
        ext_ref[s, 0:rows - 8, :] = ext_ref[0, s:s + rows - 8, :]


def _window(ext_ref, off, tm, cs):
    s = off % 8
    return ext_ref[s, off - s:off - s + tm, cs]


def _conv_taps(ext_ref, w_ref, offs, tm, out_ref):
    def chunk(cs):
        acc = w_ref[0:1, cs] * _window(ext_ref, offs[0], tm, cs)
        for j in range(1, len(offs)):
            acc = acc + w_ref[j:j + 1, cs] * _window(ext_ref, offs[j], tm, cs)
        out_ref[:, cs] = acc

    _lane_chunks(out_ref.shape[1], chunk)


def _glu_planes(ag_ref, halo_ref, ext_ref, first, c):
    hal = halo_ref[...].astype(F32)
    ext_ref[0, 0:CONV_HALO, :] = hal[:, :c] * _sigmoid(hal[:, c:]) * jnp.where(first, 0.0, 1.0)
    ag = ag_ref[...].astype(F32)
    ext_ref[0, CONV_HALO:, :] = ag[:, :c] * _sigmoid(ag[:, c:])
    _shifted_planes(ext_ref)


def _layernorm_stats(u1):
    mu = jnp.mean(u1, axis=-1, keepdims=True)
    cen = u1 - mu
    rstd = lax.rsqrt(jnp.mean(cen * cen, axis=-1, keepdims=True) + EPS)
    return cen * rstd, rstd


def _conv_mid(ag, w_dw, b_dw, ln_g, ln_b, tm=256):
    s, c2 = ag.shape
    c = c2 // 2
    tm = min(tm, s)

    def body(ag_ref, halo_ref, w_ref, b_ref, g_ref, bb_ref, o_ref, u1_ref, ext_ref):
        _glu_planes(ag_ref, halo_ref, ext_ref, pl.program_id(0) == 0, c)
        base = CONV_HALO - (CONV_KERNEL - 1)
        _conv_taps(ext_ref, w_ref, [base + j for j in range(CONV_KERNEL)], tm, u1_ref)
        xh, _ = _layernorm_stats(u1_ref[...] + b_ref[...])
        u2 = xh * g_ref[...] + bb_ref[...]
        o_ref[...] = (u2 * _sigmoid(u2)).astype(BF16)

    hb = tm // CONV_HALO
    return pl.pallas_call(
        body, name="conv_mid", out_shape=[jax.ShapeDtypeStruct((s, c), BF16), jax.ShapeDtypeStruct((s, c), F32)], grid=(s // tm,),
        in_specs=[_rows(tm, c2), pl.BlockSpec((CONV_HALO, c2), lambda i: (jnp.maximum(i * hb - 1, 0), 0)),
                  _full((CONV_KERNEL, c)), _full((1, c)), _full((1, c)), _full((1, c))],
        out_specs=[_rows(tm, c), _rows(tm, c)], scratch_shapes=[pltpu.VMEM((8, CONV_HALO + tm, c), F32)],
        compiler_params=_params("arbitrary"),
    )(ag, ag, w_dw, b_dw, ln_g, ln_b)


def _loss_grad(xo, target, tm=512):
    s, d = xo.shape
    tm = min(tm, s)

    def body(x_ref, t_ref, dx_ref, loss_ref):
        @pl.when(pl.program_id(0) == 0)
        def _():
            loss_ref[...] = jnp.zeros_like(loss_ref)

        err = x_ref[...] - t_ref[...]
        dx_ref[...] = err * (1.0 / d)
        loss_ref[...] += 0.5 * jnp.sum(jnp.mean(err * err, axis=-1, keepdims=True))

    return pl.pallas_call(
        body, name="loss_grad", out_shape=[jax.ShapeDtypeStruct((s, d), F32), jax.ShapeDtypeStruct((1, 128), F32)],
        grid=(s // tm,), in_specs=[_rows(tm, d)] * 2, out_specs=[_rows(tm, d), _full((1, 128))],
        compiler_params=_params("arbitrary"),
    )(xo, target)


def _postnorm_bwd(y, g, dxo, *, name, with_bias_grad=False, tm=512):
    s, d = y.shape
    tm = min(tm, s)

    def body(y_ref, g_ref, dx_ref, dy_ref, dg_ref, *rest):
        @pl.when(pl.program_id(0) == 0)
        def _():
            dg_ref[...] = jnp.zeros_like(dg_ref)
            for r_ in rest:
                r_[...] = jnp.zeros_like(r_)

        yv, dxo_v = y_ref[...], dx_ref[...]
        r = lax.rsqrt(jnp.mean(yv * yv, axis=-1, keepdims=True) + EPS)
        yh = yv * r
        dyh = dxo_v * g_ref[...]
        dy = r * (dyh - yh * jnp.mean(dyh * yh, axis=-1, keepdims=True))
        dy_ref[...] = dy.astype(BF16)
        dg_ref[...] += jnp.sum(dxo_v * yh, axis=0, keepdims=True)
        for r_ in rest:
            r_[...] += jnp.sum(dy, axis=0, keepdims=True)

    nacc = 2 if with_bias_grad else 1
    return pl.pallas_call(
        body, name=name, out_shape=[jax.ShapeDtypeStruct((s, d), BF16)] + [jax.ShapeDtypeStruct((1, d), F32)] * nacc,
        grid=(s // tm,), in_specs=[_rows(tm, d), _full((1, d)), _rows(tm, d)],
        out_specs=[_rows(tm, d)] + [_full((1, d))] * nacc, compiler_params=_params("arbitrary"),
    )(y, g, dxo)


def _matmul(gmat, w, *, name, out_dtype, transposed_w, tm=512):
    s, k = gmat.shape
    n = w.shape[0] if transposed_w else w.shape[1]
    tm = min(tm, s)

    def body(g_ref, w_ref, o_ref):
        if transposed_w:
            acc = lax.dot_general(g_ref[...], w_ref[...], NT, preferred_element_type=F32)
        else:
            acc = jnp.dot(g_ref[...], w_ref[...], preferred_element_type=F32)
        o_ref[...] = acc.astype(out_dtype)

    return pl.pallas_call(
        body, name=name, out_shape=jax.ShapeDtypeStruct((s, n), out_dtype), grid=(s // tm,),
        in_specs=[_rows(tm, k), _full(w.shape)], out_specs=_rows(tm, n), compiler_params=_params("parallel"),
    )(gmat, w)


def _matmul_prenorm_bwd(pieces, wt, x, g, dres, *, name, tm=256):
    s, d = x.shape
    tm = min(tm, s)
    np_ = len(pieces)

    def body(*refs):
        p_refs, w_refs = refs[:np_], refs[np_:2 * np_]
        x_ref, g_ref, r_ref, dx_ref, dg_ref = refs[2 * np_:]

        @pl.when(pl.program_id(0) == 0)
        def _():
            dg_ref[...] = jnp.zeros_like(dg_ref)

        dh = None
        for p_ref, w_ref in zip(p_refs, w_refs):
            t = jnp.dot(p_ref[...], w_ref[...], preferred_element_type=F32)
            dh = t if dh is None else dh + t
        xv = x_ref[...]
        r = lax.rsqrt(jnp.mean(xv * xv, axis=-1, keepdims=True) + EPS)
        xh = xv * r
        dyh = dh * g_ref[...]
        dx_ref[...] = r_ref[...] + r * (dyh - xh * jnp.mean(dyh * xh, axis=-1, keepdims=True))
        dg_ref[...] += jnp.sum(dh * xh, axis=0, keepdims=True)

    in_specs = []
    for _, c0, kc, _ in pieces:
        assert c0 % kc == 0
        in_specs.append(pl.BlockSpec((tm, kc), lambda i, _b=c0 // kc: (i, _b)))
    for _, _, kc, r0 in pieces:
        assert r0 % kc == 0
        in_specs.append(pl.BlockSpec((kc, d), lambda i, _b=r0 // kc: (_b, 0)))
    in_specs += [_rows(tm, d), _full((1, d)), _rows(tm, d)]
    return pl.pallas_call(
        body, name=name, out_shape=[jax.ShapeDtypeStruct((s, d), F32), jax.ShapeDtypeStruct((1, d), F32)],
        grid=(s // tm,), in_specs=in_specs, out_specs=[_rows(tm, d), _full((1, d))],
        compiler_params=_params("arbitrary"),
    )(*[p[0] for p in pieces], *[wt] * np_, x, g, dres)


def _weight_grad(a, gmat, *, name, a_col0=0, ka=None, out=None, out_shape=None, layer=0, row0=0, ts=1024):
    s = a.shape[0]
    ka = a.shape[1] if ka is None else ka
    n = gmat.shape[1]
    ts = min(ts, s)
    tka = _tile(ka, a_col0, row0)
    shape = out.shape if out is not None else out_shape
    nsteps = s // ts

    def body(a_ref, g_ref, *rest):
        o_ref, acc_ref = rest[-2:]
        i = pl.program_id(1)

        @pl.when(i == 0)
        def _():
            acc_ref[...] = jnp.zeros_like(acc_ref)

        acc_ref[...] += lax.dot_general(a_ref[...], g_ref[...], TN, preferred_element_type=F32)

        @pl.when(i == nsteps - 1)
        def _():
            o_ref[...] = acc_ref[...].astype(BF16)

    in_specs = [pl.BlockSpec((ts, tka), lambda k, i: (i, a_col0 // tka + k)), pl.BlockSpec((ts, n), lambda k, i: (i, 0))]
    args = [a, gmat]
    aliases = {}
    if out is not None:
        in_specs.append(ANY)
        args.append(out)
        aliases = {2: 0}
    return pl.pallas_call(
        body, name=name, out_shape=jax.ShapeDtypeStruct(shape, BF16), grid=(ka // tka, nsteps), in_specs=in_specs,
        out_specs=pl.BlockSpec((None, tka, n), lambda k, i: (layer, row0 // tka + k, 0)),
        scratch_shapes=[pltpu.VMEM((tka, n), F32)],
        input_output_aliases=aliases, compiler_params=_params("parallel", "arbitrary"),
    )(*args)


def _ffn_act_bwd(z, dact, w_dw, b_dw, tm=256):
    s, f2 = z.shape
    f = f2 // 2
    tm = min(tm, s)
    tc = _ffn_cols(f)
    nfc = f // tc

    def body(zu, zg, hu, hg, wu, wg, bu, bg, da_ref, du_ref, dgt_ref, dbu_ref, dbg_ref, dwu_ref, dwg_ref):
        i = pl.program_id(1)

        @pl.when(i == 0)
        def _():
            for r_ in (dbu_ref, dbg_ref, dwu_ref, dwg_ref):
                r_[...] = jnp.zeros_like(r_)

        def chunk(cs):
            def conv(z_ref, h_ref, w_ref, b_ref):
                zc = z_ref[:, cs].astype(F32)
                z2, z1 = _conv3_taps(zc, h_ref[:, cs].astype(F32), i == 0)
                return (z2, z1, zc), w_ref[0:1, cs] * z2 + w_ref[1:2, cs] * z1 + w_ref[2:3, cs] * zc + b_ref[:, cs]

            taps_u, up = conv(zu, hu, wu, bu)
            taps_g, gate = conv(zg, hg, wg, bg)
            da = da_ref[:, cs].astype(F32)
            sg = _sigmoid(gate)
            d_up = da * (gate * sg)
            d_gate = da * up * (sg * (1.0 + gate * (1.0 - sg)))
            du_ref[:, cs] = d_up.astype(BF16)
            dgt_ref[:, cs] = d_gate.astype(BF16)
            for dv, taps, db_ref, dw_ref in ((d_up, taps_u, dbu_ref, dwu_ref), (d_gate, taps_g, dbg_ref, dwg_ref)):
                db_ref[:, cs] += jnp.sum(dv, axis=0, keepdims=True)
                for k_, tap in enumerate(taps):
                    dw_ref[k_:k_ + 1, cs] += jnp.sum(dv * tap, axis=0, keepdims=True)

        _lane_chunks(tc, chunk)

    hb = tm // 8
    tile = lambda off: pl.BlockSpec((tm, tc), lambda j, i: (i, off + j))
    halo = lambda off: pl.BlockSpec((8, tc), lambda j, i: (jnp.maximum(i * hb - 1, 0), off + j))
    prm = lambda rows, off: pl.BlockSpec((rows, tc), lambda j, i: (0, off + j))
    acc = lambda rows: pl.BlockSpec((rows, tc), lambda j, i: (0, j))
    return pl.pallas_call(
        body, name="ffn_act_bwd",
        out_shape=[jax.ShapeDtypeStruct((s, f), BF16)] * 2 + [jax.ShapeDtypeStruct((1, f), F32)] * 2
        + [jax.ShapeDtypeStruct((FFN_CONV, f), F32)] * 2,
        grid=(nfc, s // tm),
        in_specs=[tile(0), tile(nfc), halo(0), halo(nfc), prm(FFN_CONV, 0), prm(FFN_CONV, nfc), prm(1, 0), prm(1, nfc), tile(0)],
        out_specs=[tile(0), tile(0), acc(1), acc(1), acc(FFN_CONV), acc(FFN_CONV)],
        compiler_params=_params("parallel", "arbitrary"),
    )(z, z, z, z, w_dw, w_dw, b_dw, b_dw, dact)


def _conv3_transpose(dug, w_dw, col0, tm=256):
    s, f = dug.shape
    tm = min(tm, s)
    tc = _ffn_cols(f)
    nfc = f // tc
    nrow = s // tm
    off = col0 // tc

    def body(d_ref, n_ref, w_ref, o_ref):
        keep_next = jnp.where(pl.program_id(0) == nrow - 1, 0.0, 1.0)

        def chunk(cs):
            dv = d_ref[:, cs].astype(F32)
            nxt = n_ref[:, cs].astype(F32) * keep_next
            n0, n1 = nxt[0:1, :], nxt[1:2, :]
            row = lax.broadcasted_iota(jnp.int32, dv.shape, 0)
            d1 = jnp.where(row == tm - 1, n0, pltpu.roll(dv, tm - 1, 0))
            d2 = jnp.where(row == tm - 1, n1, jnp.where(row == tm - 2, n0, pltpu.roll(dv, tm - 2, 0)))
            o_ref[:, cs] = (w_ref[2:3, cs] * dv + w_ref[1:2, cs] * d1 + w_ref[0:1, cs] * d2).astype(BF16)

        _lane_chunks(tc, chunk)

    hb = tm // 8
    return pl.pallas_call(
        body, name="conv3_transpose", out_shape=jax.ShapeDtypeStruct((s, f), BF16), grid=(nrow, nfc),
        in_specs=[pl.BlockSpec((tm, tc), lambda i, j: (i, j)),
                  pl.BlockSpec((8, tc), lambda i, j: (jnp.minimum((i + 1) * hb, s // 8 - 1), j)),
                  pl.BlockSpec((FFN_CONV, tc), lambda i, j: (0, off + j))],
        out_specs=pl.BlockSpec((tm, tc), lambda i, j: (i, j)), compiler_params=_params("parallel", "parallel"),
    )(dug, dug, w_dw)


def _conv_mid_bwd(ag, u1, du3, b_dw, ln_g, ln_b, tm=256):
    s, c2 = ag.shape
    c = c2 // 2
    tm = min(tm, s)

    def body(ag_ref, halo_ref, u1in_ref, du_ref, b_ref, g_ref, bb_ref, o_ref, dlg_ref, dlb_ref, db_ref, dw_ref, ext_ref, u1_ref):
        @pl.when(pl.program_id(0) == 0)
        def _():
            for r_ in (dlg_ref, dlb_ref, db_ref, dw_ref):
                r_[...] = jnp.zeros_like(r_)

        _glu_planes(ag_ref, halo_ref, ext_ref, pl.program_id(0) == 0, c)
        xh, rstd = _layernorm_stats(u1in_ref[...] + b_ref[...])
        u2 = xh * g_ref[...] + bb_ref[...]
        sg = _sigmoid(u2)
        du2 = du_ref[...] * (sg * (1.0 + u2 * (1.0 - sg)))
        dlg_ref[...] += jnp.sum(du2 * xh, axis=0, keepdims=True)
        dlb_ref[...] += jnp.sum(du2, axis=0, keepdims=True)
        dxh = du2 * g_ref[...]
        du1 = rstd * (dxh - jnp.mean(dxh, axis=-1, keepdims=True) - xh * jnp.mean(dxh * xh, axis=-1, keepdims=True))
        o_ref[...] = du1.astype(BF16)
        db_ref[...] += jnp.sum(du1, axis=0, keepdims=True)
        u1_ref[...] = du1
        base = CONV_HALO - (CONV_KERNEL - 1)

        def chunk(cs):
            dc = u1_ref[:, cs]
            for j in range(CONV_KERNEL):
                dw_ref[j:j + 1, cs] += jnp.sum(dc * _window(ext_ref, base + j, tm, cs), axis=0, keepdims=True)

        _lane_chunks(c, chunk)

    hb = tm // CONV_HALO
    vec = _full((1, c))
    return pl.pallas_call(
        body, name="conv_mid_bwd",
        out_shape=[jax.ShapeDtypeStruct((s, c), BF16)] + [jax.ShapeDtypeStruct((1, c), F32)] * 3
        + [jax.ShapeDtypeStruct((CONV_HALO, c), F32)],
        grid=(s // tm,),
        in_specs=[_rows(tm, c2), pl.BlockSpec((CONV_HALO, c2), lambda i: (jnp.maximum(i * hb - 1, 0), 0)), _rows(tm, c),
                  _rows(tm, c), vec, vec, vec],
        out_specs=[_rows(tm, c), vec, vec, vec, _full((CONV_HALO, c))],
        scratch_shapes=[pltpu.VMEM((8, CONV_HALO + tm, c), F32), pltpu.VMEM((tm, c), F32)],
        compiler_params=_params("arbitrary"),
    )(ag, ag, u1, du3, b_dw, ln_g, ln_b)


def _glu_conv_bwd(du1, ag, w_dw, tm=256):
    s, c = du1.shape
    tm = min(tm, s)
    nrow = s // tm

    def body(d_ref, n_ref, ag_ref, w_ref, o_ref, db_ref, ext_ref, du0_ref):
        @pl.when(pl.program_id(0) == 0)
        def _():
            db_ref[...] = jnp.zeros_like(db_ref)

        ext_ref[0, 0:tm, :] = d_ref[...].astype(F32)
        ext_ref[0, tm:, :] = n_ref[...].astype(F32) * jnp.where(pl.program_id(0) == nrow - 1, 0.0, 1.0)
        _shifted_planes(ext_ref)
        top = CONV_KERNEL - 1
        _conv_taps(ext_ref, w_ref, [top - j for j in range(CONV_KERNEL)], tm, du0_ref)
        du0 = du0_ref[...]
        ag = ag_ref[...].astype(F32)
        a, gt = ag[:, :c], ag[:, c:]
        sg = _sigmoid(gt)
        da = du0 * sg
        dgt = du0 * a * (sg * (1.0 - sg))
        o_ref[:, :c] = da.astype(BF16)
        o_ref[:, c:] = dgt.astype(BF16)
        db_ref[:, :c] += jnp.sum(da, axis=0, keepdims=True)
        db_ref[:, c:] += jnp.sum(dgt, axis=0, keepdims=True)

    hb = tm // CONV_HALO
    return pl.pallas_call(
        body, name="glu_conv_bwd",
        out_shape=[jax.ShapeDtypeStruct((s, 2 * c), BF16), jax.ShapeDtypeStruct((1, 2 * c), F32)], grid=(nrow,),
        in_specs=[_rows(tm, c), pl.BlockSpec((CONV_HALO, c), lambda i: (jnp.minimum((i + 1) * hb, s // CONV_HALO - 1), 0)),
                  _rows(tm, 2 * c), _full((CONV_KERNEL, c))],
        out_specs=[_rows(tm, 2 * c), _full((1, 2 * c))],
        scratch_shapes=[pltpu.VMEM((8, tm + CONV_HALO, c), F32), pltpu.VMEM((tm, c), F32)],
        compiler_params=_params("arbitrary"),
    )(du1, du1, ag, w_dw)


def _head_rows(v, mask):
    return jnp.max(jnp.where(mask, v, -jnp.inf), axis=-1, keepdims=True)


def _attn_bwd_dq(qv, dmix, mixed, lse, rope, grp, dil):
    l = qv.shape[0]
    s = l * dil
    nb = l // SPAN
    view = lambda t: t.reshape(l, dil * t.shape[1])

    def body(q_ref, kp_ref, kc_ref, vp_ref, vc_ref, do_ref, mx_ref, l_ref, c_ref, su_ref, sd_ref, o_ref):
        b = pl.program_id(1)
        row = lax.broadcasted_iota(jnp.int32, (SPAN, 2 * SPAN), 0)
        col = lax.broadcasted_iota(jnp.int32, (SPAN, 2 * SPAN), 1)
        no_prev = jnp.where(b > 0, 0, 4 * SPAN)
        valid = ((col < SPAN) & (col >= row + no_prev)) | ((col >= SPAN) & (col - SPAN <= row))
        masks, keep = _head_masks()
        for p in range(GROUP_WIDTH // 128):
            sl = slice(p * 128, (p + 1) * 128)
            qp, dop = q_ref[:, sl], do_ref[:, sl]
            kk = jnp.concatenate([kp_ref[:, sl], kc_ref[:, sl]], axis=0)
            vv = jnp.concatenate([vp_ref[:, sl], vc_ref[:, sl]], axis=0)
            prod = dop.astype(F32) * mx_ref[:, sl].astype(F32)
            lsep = l_ref[:, sl]
            dqs = []
            for h in range(2):
                qh, doh = qp * keep[h], dop * keep[h]
                sc = lax.dot_general(qh, kk, NT, preferred_element_type=F32) * (HEAD_DIM ** -0.5)
                pe = jnp.where(valid, jnp.exp(sc - _head_rows(lsep, masks[h])), 0.0)
                dp = lax.dot_general(doh, vv, NT, preferred_element_type=F32)
                dbar = jnp.sum(jnp.where(masks[h], prod, 0.0), axis=-1, keepdims=True)
                ds = pe * (dp - dbar) * (HEAD_DIM ** -0.5)
                dqs.append(jnp.dot(ds.astype(BF16), kk, preferred_element_type=F32))
            dq = jnp.where(masks[0], dqs[0], dqs[1])
            o_ref[:, sl] = _rope_transpose(dq, c_ref[...], su_ref[...], sd_ref[...]).astype(BF16)

    blk = (SPAN, GROUP_WIDTH)
    cur = lambda t: pl.BlockSpec(blk, lambda r, b: (b, r * 3 + t))
    prev = lambda t: pl.BlockSpec(blk, lambda r, b: (jnp.maximum(b - 1, 0), r * 3 + t))
    own = pl.BlockSpec(blk, lambda r, b: (b, r))
    tab = pl.BlockSpec((SPAN, 128), lambda r, b: (b, r))
    out = pl.pallas_call(
        body, name=f"attn_bwd_dq_g{grp}", out_shape=jax.ShapeDtypeStruct((l, dil * GROUP_WIDTH), BF16), grid=(dil, nb),
        in_specs=[cur(0), prev(1), cur(1), prev(2), cur(2), own, own, own, tab, tab, tab], out_specs=own,
        compiler_params=_params("parallel", "arbitrary"),
    )(qv, qv, qv, qv, qv, view(dmix), view(mixed), view(lse), *[view(t) for t in rope])
    return out.reshape(s, GROUP_WIDTH)


def _attn_bwd_dkv(qv, dmix, mixed, lse, rope, grp, dil):
    l = qv.shape[0]
    s = l * dil
    nb = l // SPAN
    view = lambda t: t.reshape(l, dil * t.shape[1])

    def body(k_ref, v_ref, qc_ref, qn_ref, doc_ref, don_ref, mc_ref, mn_ref, lc_ref, ln_ref,
             c_ref, su_ref, sd_ref, o_ref):
        b = pl.program_id(1)
        row = lax.broadcasted_iota(jnp.int32, (2 * SPAN, SPAN), 0)
        col = lax.broadcasted_iota(jnp.int32, (2 * SPAN, SPAN), 1)
        no_next = jnp.where(b < nb - 1, 0, 4 * SPAN)
        valid = ((row < SPAN) & (col <= row)) | ((row >= SPAN) & (col >= row - SPAN + no_next))
        masks, keep = _head_masks()
        masks2, _ = _head_masks(2 * SPAN)
        for p in range(GROUP_WIDTH // 128):
            sl = slice(p * 128, (p + 1) * 128)
            kp, vp = k_ref[:, sl], v_ref[:, sl]
            qq = jnp.concatenate([qc_ref[:, sl], qn_ref[:, sl]], axis=0)
            doo = jnp.concatenate([doc_ref[:, sl], don_ref[:, sl]], axis=0)
            mm = jnp.concatenate([mc_ref[:, sl], mn_ref[:, sl]], axis=0)
            ll = jnp.concatenate([lc_ref[:, sl], ln_ref[:, sl]], axis=0)
            prod = doo.astype(F32) * mm.astype(F32)
            dks, dvs = [], []
            for h in range(2):
                qh, doh = qq * keep[h], doo * keep[h]
                sc = lax.dot_general(qh, kp, NT, preferred_element_type=F32) * (HEAD_DIM ** -0.5)
                pe = jnp.where(valid, jnp.exp(sc - _head_rows(ll, masks2[h])), 0.0)
                dp = lax.dot_general(doh, vp, NT, preferred_element_type=F32)
                dbar = jnp.sum(jnp.where(masks2[h], prod, 0.0), axis=-1, keepdims=True)
                ds = pe * (dp - dbar) * (HEAD_DIM ** -0.5)
                dvs.append(lax.dot_general(pe.astype(BF16), doo, TN, preferred_element_type=F32))
                dks.append(lax.dot_general(ds.astype(BF16), qq, TN, preferred_element_type=F32))
            dk = jnp.where(masks[0], dks[0], dks[1])
            o_ref[:, sl] = _rope_transpose(dk, c_ref[...], su_ref[...], sd_ref[...]).astype(BF16)
            o_ref[:, GROUP_WIDTH + p * 128:GROUP_WIDTH + (p + 1) * 128] = jnp.where(masks[0], dvs[0], dvs[1]).astype(BF16)

    blk = (SPAN, GROUP_WIDTH)
    nxt_b = lambda b: jnp.minimum(b + 1, nb - 1)
    col_of = lambda t: pl.BlockSpec(blk, lambda r, b: (b, r * 3 + t))
    q_next = pl.BlockSpec(blk, lambda r, b: (nxt_b(b), r * 3))
    own = pl.BlockSpec(blk, lambda r, b: (b, r))
    own_next = pl.BlockSpec(blk, lambda r, b: (nxt_b(b), r))
    tab = pl.BlockSpec((SPAN, 128), lambda r, b: (b, r))
    dv_, mv, lv = view(dmix), view(mixed), view(lse)
    out = pl.pallas_call(
        body, name=f"attn_bwd_dkv_g{grp}", out_shape=jax.ShapeDtypeStruct((l, dil * 2 * GROUP_WIDTH), BF16), grid=(dil, nb),
        in_specs=[col_of(1), col_of(2), col_of(0), q_next, own, own_next, own, own_next, own, own_next, tab, tab, tab],
        out_specs=pl.BlockSpec((SPAN, 2 * GROUP_WIDTH), lambda r, b: (b, r)),
        compiler_params=_params("parallel", "arbitrary"),
    )(qv, qv, qv, qv, dv_, dv_, mv, mv, lv, lv, *[view(t) for t in rope])
    return out.reshape(s, 2 * GROUP_WIDTH)


def _rope_freq_row():
    half = ROT_DIM // 2
    inv = (ROPE_THETA ** (-np.arange(half, dtype=np.float32) / half)).astype(np.float32)
    row = np.zeros((1, 128), np.float32)
    for head in range(128 // HEAD_DIM):
        row[0, head * HEAD_DIM:head * HEAD_DIM + half] = inv
        row[0, head * HEAD_DIM + half:head * HEAD_DIM + ROT_DIM] = inv
    return jnp.asarray(row)


def _ffn_fwd(x, g_pre, g_post, w_up_t, w_dw, b_dw, w_down):
    h, z = _norm_matmul(x, g_pre, w_up_t, tn=_tile(w_up_t.shape[0]), name="ffn_up")
    act = _ffn_act(z, w_dw, b_dw)
    y, xo = _matmul_resnorm(act, w_down, x, g_post, name="ffn_down")
    return xo, (x, h, z, act, y)


def _ffn_bwd(saved, dxo, g_pre, g_post, w_up_t, w_dw, b_dw, w_down, layer, d_up_t, d_down):
    x, h, z, act, y = saved
    f = act.shape[1]
    d = x.shape[1]
    dy, dg_post = _postnorm_bwd(y, g_post, dxo, name="ffn_post_bwd")
    dact = _matmul(dy, w_down, name="ffn_dact", out_dtype=BF16, transposed_w=True)
    d_down = _weight_grad(act, dy, name="ffn_dw_down", out=d_down, out_shape=(2, f, d), layer=layer)
    dug_u, dug_g, db_u, db_g, dwd_u, dwd_g = _ffn_act_bwd(z, dact, w_dw, b_dw)
    dz_u = _conv3_transpose(dug_u, w_dw, 0)
    dz_g = _conv3_transpose(dug_g, w_dw, f)
    dx, dg_pre = _matmul_prenorm_bwd([(dz_u, 0, f, 0), (dz_g, 0, f, f)], w_up_t, x, g_pre, dxo, name="ffn_dx")
    d_up_t = _weight_grad(dz_u, h, name="ffn_dw_up", out=d_up_t, out_shape=(2, 2 * f, d), layer=layer)
    d_up_t = _weight_grad(dz_g, h, name="ffn_dw_up", out=d_up_t, layer=layer, row0=f)
    grads = dict(w_dw=jnp.concatenate([dwd_u, dwd_g], axis=1), b_dw=jnp.concatenate([db_u, db_g], axis=1),
                 g_pre=dg_pre, g_post=dg_post)
    return dx, grads, d_up_t, d_down


def _local_step(x, pos_col, target, p, tie=None, late_weights=None):
    ng = p["norm_g"]
    row = lambda r: ng[r:r + 1]
    freq = _rope_freq_row()
    rope = _rope_tables(pos_col, freq if tie is None else freq + tie[0:1])
    d = x.shape[1]

    h0, *qkv = _qkv_proj(x, row(0), p["w_qkv_t"], rope)
    os_, ls_ = zip(*[_attn_fwd(qkv[g_], g_, d_) for g_, d_ in enumerate(DILATIONS)])
    y_a, x1, mixed, lse = _mix_wo(os_, ls_, p["w_o_t"], x, row(1))
    if late_weights is not None:
        p = {**p, **late_weights(x1)}
    x2, ffn0 = _ffn_fwd(x1, row(2), row(3), p["w_up_t"][0], p["ffn_w_dw"][0], p["ffn_b_dw"][0], p["w_down"][0])
    h1, ag = _norm_matmul(x2, row(4), p["w_pw1_t"], tn=_tile(p["w_pw1_t"].shape[0]), name="conv_pw1", bias=p["b_pw1"])
    u3, u1 = _conv_mid(ag, p["conv_w_dw"], p["conv_b_dw"], p["ln_g"], p["ln_b"])
    y_c, x3 = _matmul_resnorm(u3, p["w_pw2"], x2, row(5), name="conv_pw2", bias=p["b_pw2"])
    x4, ffn1 = _ffn_fwd(x3, row(6), row(7), p["w_up_t"][1], p["ffn_w_dw"][1], p["ffn_b_dw"][1], p["w_down"][1])
    dx4, loss = _loss_grad(x4, target)

    dx3, gf1, d_up_t, d_down = _ffn_bwd(ffn1, dx4, row(6), row(7), p["w_up_t"][1], p["ffn_w_dw"][1], p["ffn_b_dw"][1],
                                        p["w_down"][1], 1, None, None)
    dy_c, dg5, db_pw2 = _postnorm_bwd(y_c, row(5), dx3, name="conv_post_bwd", with_bias_grad=True)
    du3 = _matmul(dy_c, p["w_pw2"], name="conv_du3", out_dtype=F32, transposed_w=True)
    d_wpw2 = _weight_grad(u3, dy_c, name="conv_dw_pw2", out_shape=(1, u3.shape[1], d))
    du1, d_lng, d_lnb, d_cbdw, d_cwdw = _conv_mid_bwd(ag, u1, du3, p["conv_b_dw"], p["ln_g"], p["ln_b"])
    dag, db_pw1 = _glu_conv_bwd(du1, ag, p["conv_w_dw"])
    dx2, dg4 = _matmul_prenorm_bwd([(dag, 0, dag.shape[1], 0)], p["w_pw1_t"], x2, row(4), dx3, name="conv_dx")
    d_wpw1_t = _weight_grad(dag, h1, name="conv_dw_pw1", out_shape=(1, dag.shape[1], d))
    dx1, gf0, d_up_t, d_down = _ffn_bwd(ffn0, dx2, row(2), row(3), p["w_up_t"][0], p["ffn_w_dw"][0], p["ffn_b_dw"][0],
                                        p["w_down"][0], 0, d_up_t, d_down)
    dy_a, dg1 = _postnorm_bwd(y_a, row(1), dx1, name="attn_post_bwd")
    dmix = _matmul(dy_a, p["w_o_t"], name="attn_dmix", out_dtype=BF16, transposed_w=False)
    d_wo_t = _weight_grad(dy_a, mixed, name="attn_dw_o", out_shape=(1, d, GROUP_WIDTH))
    pieces, d_wqkv_t = [], None
    for g_, d_ in enumerate(DILATIONS):
        dq = _attn_bwd_dq(qkv[g_], dmix, mixed, lse, rope, g_, d_)
        dkv = _attn_bwd_dkv(qkv[g_], dmix, mixed, lse, rope, g_, d_)
        for t, (arr, c0) in enumerate(((dq, 0), (dkv, 0), (dkv, GROUP_WIDTH))):
            r0 = (3 * t + g_) * GROUP_WIDTH
            pieces.append((arr, c0, GROUP_WIDTH, r0))
            d_wqkv_t = _weight_grad(arr, h0, name="attn_dw_qkv", a_col0=c0, ka=GROUP_WIDTH, out=d_wqkv_t,
                                    out_shape=(1, p["w_qkv_t"].shape[0], d), row0=r0)
    grad_x, dg0 = _matmul_prenorm_bwd(pieces, p["w_qkv_t"], x, row(0), dx1, name="attn_dx")

    grads = dict(
        norm_g=jnp.concatenate([dg0, dg1, gf0["g_pre"], gf0["g_post"], dg4, dg5, gf1["g_pre"], gf1["g_post"]], axis=0),
        w_qkv_t=d_wqkv_t, w_o_t=d_wo_t, w_pw1_t=d_wpw1_t, b_pw1=db_pw1,
        conv_w_dw=d_cwdw[:CONV_KERNEL], conv_b_dw=d_cbdw, ln_g=d_lng, ln_b=d_lnb, w_pw2=d_wpw2, b_pw2=db_pw2,
        w_up_t=d_up_t, ffn_w_dw=jnp.stack([gf0["w_dw"], gf1["w_dw"]]),
        ffn_b_dw=jnp.concatenate([gf0["b_dw"], gf1["b_dw"]], axis=0), w_down=d_down)
    return loss, grad_x, grads


SMALL_AXIS = dict(norm_g=2, conv_b_pw1=1, conv_w_dw=2, conv_b_dw=1, conv_ln_g=1, conv_ln_b=1, conv_b_pw2=1, ffn_w_dw=2)
SMALL = tuple(SMALL_AXIS)
MATMUL_WEIGHTS = dict(attn_w_qkv=True, conv_w_pw1=True, ffn_w_up=True, conv_w_pw2=False, ffn_w_down=False)


def _pack(arrays, cols, row_multiple):
    flat = jnp.concatenate([a.reshape(-1) for a in arrays])
    rows = -(-flat.shape[0] // cols)
    rows = -(-rows // row_multiple) * row_multiple
    return jnp.pad(flat, (0, rows * cols - flat.shape[0])).reshape(rows, cols)


def _unpack(packed, shapes):
    flat = packed.reshape(packed.shape[:-2] + (-1,))
    out, off = [], 0
    for shp in shapes:
        n = math.prod(shp)
        out.append(flat[..., off:off + n].reshape(packed.shape[:-2] + tuple(shp)))
        off += n
    return out


def _join_shards(stacked, axis):
    moved = jnp.moveaxis(stacked, 0, axis)
    shp = moved.shape
    return moved.reshape(shp[:axis] + (shp[axis] * shp[axis + 1],) + shp[axis + 2:])


def _split_shards(whole, axis):
    shp = whole.shape
    cut = whole.reshape(shp[:axis] + (N_DEV, shp[axis] // N_DEV) + shp[axis + 1:])
    return jnp.moveaxis(cut, axis, 0)


def _row_shard(w, transposed):
    t = jnp.swapaxes(w, 1, 2) if transposed else w
    return t.astype(BF16).reshape(-1, t.shape[-1])


def kernel(x, positions, norm_g, attn_w_qkv, attn_w_o, conv_w_pw1, conv_b_pw1, conv_w_dw, conv_b_dw, conv_ln_g, conv_ln_b, conv_w_pw2, conv_b_pw2, ffn_w_up, ffn_w_dw, ffn_b_dw, ffn_w_down, loss_target, m_norm_g, m_attn_w_qkv, m_attn_w_o, m_conv_w_pw1, m_conv_b_pw1, m_conv_w_dw, m_conv_b_dw, m_conv_ln_g, m_conv_ln_b, m_conv_w_pw2, m_conv_b_pw2, m_ffn_w_up, m_ffn_w_dw, m_ffn_b_dw, m_ffn_w_down, v_norm_g, v_attn_w_qkv, v_attn_w_o, v_conv_w_pw1, v_conv_b_pw1, v_conv_w_dw, v_conv_b_dw, v_conv_ln_g, v_conv_ln_b, v_conv_w_pw2, v_conv_b_pw2, v_ffn_w_up, v_ffn_w_dw, v_ffn_b_dw, v_ffn_w_down):
    w = dict(norm_g=norm_g, attn_w_qkv=attn_w_qkv, attn_w_o=attn_w_o, conv_w_pw1=conv_w_pw1, conv_b_pw1=conv_b_pw1,
             conv_w_dw=conv_w_dw, conv_b_dw=conv_b_dw, conv_ln_g=conv_ln_g, conv_ln_b=conv_ln_b, conv_w_pw2=conv_w_pw2,
             conv_b_pw2=conv_b_pw2, ffn_w_up=ffn_w_up, ffn_w_dw=ffn_w_dw, ffn_w_down=ffn_w_down)
    m = dict(norm_g=m_norm_g, attn_w_qkv=m_attn_w_qkv, attn_w_o=m_attn_w_o, conv_w_pw1=m_conv_w_pw1, conv_b_pw1=m_conv_b_pw1,
             conv_w_dw=m_conv_w_dw, conv_b_dw=m_conv_b_dw, conv_ln_g=m_conv_ln_g, conv_ln_b=m_conv_ln_b, conv_w_pw2=m_conv_w_pw2,
             conv_b_pw2=m_conv_b_pw2, ffn_w_up=m_ffn_w_up, ffn_w_dw=m_ffn_w_dw, ffn_w_down=m_ffn_w_down)
    v = dict(norm_g=v_norm_g, attn_w_qkv=v_attn_w_qkv, attn_w_o=v_attn_w_o, conv_w_pw1=v_conv_w_pw1, conv_b_pw1=v_conv_b_pw1,
             conv_w_dw=v_conv_w_dw, conv_b_dw=v_conv_b_dw, conv_ln_g=v_conv_ln_g, conv_ln_b=v_conv_ln_b, conv_w_pw2=v_conv_w_pw2,
             conv_b_pw2=v_conv_b_pw2, ffn_w_up=v_ffn_w_up, ffn_w_dw=v_ffn_w_dw, ffn_w_down=v_ffn_w_down)
    d = x.shape[-1]

    w_qkv_t = _all_gather(_row_shard(attn_w_qkv, True), "gather_w_qkv").reshape(-1, d)
    w_o_t = _all_gather(_row_shard(attn_w_o, True), "gather_w_o").reshape(d, -1)
    small = _all_gather(_pack([w[n] for n in SMALL], 128, 8), "gather_small_weights")
    sm = {n: _join_shards(stacked, SMALL_AXIS[n])
          for n, stacked in zip(SMALL, _unpack(small, [w[n].shape for n in SMALL]))}
    late = {n: t for n, t in MATMUL_WEIGHTS.items() if n != "attn_w_qkv"}
    shares = [_row_shard(w[n], t) for n, t in late.items()]
    rows = [s_.shape[0] for s_ in shares]
    late_share = jnp.concatenate(shares, axis=0)
    send_sems, recv_sems, share_thru, land_thru, tie = _gather_start(late_share)
    me = 4 * lax.axis_index("x") + 2 * lax.axis_index("y") + lax.axis_index("c")

    def late_weights(after):
        big = _gather_wait(send_sems, recv_sems, share_thru, land_thru, after)
        big = lax.dynamic_update_slice(big, late_share[None], (me, 0, 0))
        whole, r0 = {}, 0
        for n, nr in zip(late, rows):
            layers = w[n].shape[0]
            seg = big[:, r0:r0 + nr].reshape(N_DEV, layers, nr // layers, d)
            whole[n] = [seg[:, l_].reshape(-1, d) for l_ in range(layers)]
            r0 += nr
        return dict(w_pw1_t=whole["conv_w_pw1"][0], w_pw2=whole["conv_w_pw2"][0], w_up_t=whole["ffn_w_up"],
                    w_down=whole["ffn_w_down"])

    p = dict(norm_g=sm["norm_g"].reshape(-1, d), w_qkv_t=w_qkv_t, w_o_t=w_o_t, b_pw1=sm["conv_b_pw1"],
             conv_w_dw=sm["conv_w_dw"][0], conv_b_dw=sm["conv_b_dw"], ln_g=sm["conv_ln_g"], ln_b=sm["conv_ln_b"],
             b_pw2=sm["conv_b_pw2"], ffn_w_dw=sm["ffn_w_dw"], ffn_b_dw=[ffn_b_dw[0:1], ffn_b_dw[1:2]])

    loss, grad_x, g = _local_step(x[0], positions.reshape(-1, 1), loss_target[0], p, tie, late_weights)
    loss = lax.psum(loss[0, 0], ("x", "y", "c"))

    gsmall = dict(norm_g=g["norm_g"].reshape(norm_g.shape[0], 4, -1), conv_b_pw1=g["b_pw1"], conv_w_dw=g["conv_w_dw"][None],
                  conv_b_dw=g["conv_b_dw"], conv_ln_g=g["ln_g"], conv_ln_b=g["ln_b"], conv_b_pw2=g["b_pw2"], ffn_w_dw=g["ffn_w_dw"])
    small_contrib = jnp.concatenate([_split_shards(gsmall[n], SMALL_AXIS[n]).reshape(N_DEV, -1) for n in SMALL], axis=1)
    srows = small.shape[1]
    small_contrib = jnp.pad(small_contrib, ((0, 0), (0, srows * 128 - small_contrib.shape[1]))).reshape(1, N_DEV, srows, 128)
    big_names = ("w_qkv_t", "w_o_t", "w_pw1_t", "w_up_t", "w_pw2", "w_down")
    contribs = [g[n].reshape(g[n].shape[0], N_DEV, g[n].shape[1] // N_DEV, g[n].shape[2]) for n in big_names] + [small_contrib]
    core = lax.axis_index("c").astype(jnp.int32).reshape(1)
    got = _rs_sibling(contribs)
    dtypes = [BF16] * len(big_names) + [F32]
    chip_sums = _rs_chips([_rs_pair_add(c_, g_, core, dt) for c_, g_, dt in zip(contribs, got, dtypes)])

    outs = {}
    for n, gname, transposed in (("attn_w_qkv", "w_qkv_t", True), ("attn_w_o", "w_o_t", True), ("conv_w_pw1", "w_pw1_t", True),
                                 ("ffn_w_up", "w_up_t", True), ("conv_w_pw2", "w_pw2", False), ("ffn_w_down", "w_down", False)):
        gsum = _sum_parts(chip_sums[big_names.index(gname)], "sum_chips")
        gsum = jnp.swapaxes(gsum, 1, 2) if transposed else gsum
        outs[n] = (gsum, *_adamw(gsum, w[n], m[n], v[n], "adamw"))
    sshapes = [w[n].shape for n in SMALL]
    souts = _sum_adamw(chip_sums[-1][0], *[_pack([t[n] for n in SMALL], 128, 8) for t in (w, m, v)], name="sum_adamw_small")
    for n, vals in zip(SMALL, zip(*[_unpack(o, sshapes) for o in souts])):
        outs[n] = vals
    bparts = _all_gather(_pack([g["ffn_b_dw"]], 128, 8), "gather_bias_grads")
    bouts = _sum_adamw(bparts, *[_pack([t], 128, 8) for t in (ffn_b_dw, m_ffn_b_dw, v_ffn_b_dw)], name="sum_adamw_bias")
    outs["ffn_b_dw"] = tuple(_unpack(o, [ffn_b_dw.shape])[0] for o in bouts)

    order = ("norm_g", "attn_w_qkv", "attn_w_o", "conv_w_pw1", "conv_b_pw1", "conv_w_dw", "conv_b_dw", "conv_ln_g",
             "conv_ln_b", "conv_w_pw2", "conv_b_pw2", "ffn_w_up", "ffn_w_dw", "ffn_b_dw", "ffn_w_down")
    return (loss, grad_x[None], *[outs[n][0] for n in order], *[outs[n][1] for n in order],
            *[outs[n][2] for n in order], *[outs[n][3] for n in order])
```

```python
import functools
import math

import numpy as np
import jax
import jax.numpy as jnp
from jax import lax
from jax.experimental import pallas as pl
from jax.experimental.pallas import tpu as pltpu

F32 = jnp.float32
BF16 = jnp.bfloat16
EPS = 1e-6
N_DEV = 8
HEAD_DIM = 64
GROUP_WIDTH = 512
DILATIONS = (1, 4, 16)
SPAN = 128
ROT_DIM = 16
ROPE_THETA = 500000.0
CONV_KERNEL = 31
CONV_HALO = 32
FFN_CONV = 3
ADAM_LR, ADAM_B1, ADAM_B2, ADAM_EPS, ADAM_WD, ADAM_STEP = 0.001, 0.9, 0.999, 1e-08, 0.01, 10
VMEM_LIMIT_BYTES = 56 * 1024 * 1024
MESH = pl.DeviceIdType.MESH
ANY = pl.BlockSpec(memory_space=pl.ANY)
NT = (((1,), (1,)), ((), ()))
TN = (((0,), (0,)), ((), ()))


def _params(*sem):
    return pltpu.CompilerParams(dimension_semantics=sem, vmem_limit_bytes=VMEM_LIMIT_BYTES)


def _sigmoid(v):
    return 1.0 / (1.0 + jnp.exp(-v))


def _full(shape):
    return pl.BlockSpec(shape, lambda *_: (0,) * len(shape))


def _rows(tm, width):
    return pl.BlockSpec((tm, width), lambda i, *_: (i, 0))


def _tile(n, *multiples_of):
    for t in (1408, 1024, 512, 384, 256, 128):
        if n % t == 0 and all(o % t == 0 for o in multiples_of):
            return t
    raise ValueError((n, multiples_of))


def _all_gather(shard, name):
    r, c_ = shard.shape

    def body(x_ref, out_ref, send_sems, recv_sems, local_sem):
        x, y, c = lax.axis_index("x"), lax.axis_index("y"), lax.axis_index("c")
        me, sibling = (x, y, c), (x, y, 1 - c)
        chips = [(1 - x, y), (x, 1 - y), (1 - x, 1 - y)]

        def rows(px, py, pc):
            return out_ref.at[4 * px + 2 * py + pc]

        def copy(k, block, to, src=None):
            return pltpu.make_async_remote_copy(
                src_ref=rows(*block) if src is None else src, dst_ref=rows(*block),
                send_sem=send_sems.at[k], recv_sem=recv_sems.at[k], device_id=to, device_id_type=MESH)

        mine = pltpu.make_async_copy(x_ref, rows(*me), local_sem)
        mine.start()
        first = [copy(0, me, sibling, src=x_ref)]
        first += [copy(1 + j, me, (*chip, c), src=x_ref) for j, chip in enumerate(chips)]
        for cp in first:
            cp.start()
        passed = [copy(4 + j, (*chip, c), sibling) for j, chip in enumerate(chips)]
        for j, chip in enumerate(chips):
            copy(1 + j, (*chip, c), me).wait_recv()
            passed[j].start()
        copy(0, sibling, me).wait_recv()
        for j, chip in enumerate(chips):
            copy(4 + j, (*chip, 1 - c), me).wait_recv()
        for cp in first + passed:
            cp.wait_send()
        mine.wait()

    return pl.pallas_call(
        body, name=name, out_shape=jax.ShapeDtypeStruct((N_DEV, r, c_), shard.dtype),
        in_specs=[ANY], out_specs=ANY,
        scratch_shapes=[pltpu.SemaphoreType.DMA((7,)), pltpu.SemaphoreType.DMA((7,)), pltpu.SemaphoreType.DMA],
    )(shard)


HBM = pl.BlockSpec(memory_space=pltpu.HBM)
SEM = pl.BlockSpec(memory_space=pltpu.SEMAPHORE)
SIDE_EFFECT = pltpu.CompilerParams(has_side_effects=pltpu.SideEffectType.DATAFLOW_SIDE_EFFECTING)


def _gather_start(shard):
    r, c_ = shard.shape

    def body(x_ref, land_ref, send_sems, recv_sems, x_thru, land_thru, token):
        x, y, c = lax.axis_index("x"), lax.axis_index("y"), lax.axis_index("c")
        me = 4 * x + 2 * y + c
        for k in range(1, N_DEV):
            peer = (1 - x if k & 4 else x, 1 - y if k & 2 else y, 1 - c if k & 1 else c)
            pltpu.make_async_remote_copy(src_ref=x_ref, dst_ref=land_ref.at[me], send_sem=send_sems.at[k - 1],
                                         recv_sem=recv_sems.at[k - 1], device_id=peer, device_id_type=MESH).start()
        token[...] = jnp.zeros_like(token)

    land = pltpu.with_memory_space_constraint(lax.empty((N_DEV, r, c_), shard.dtype), pltpu.HBM)
    return pl.pallas_call(
        body, name="gather_late_weights_start",
        out_shape=(pltpu.SemaphoreType.DMA((N_DEV - 1,)), pltpu.SemaphoreType.DMA((N_DEV - 1,)),
                   pltpu.HBM(shard.shape, shard.dtype), pltpu.HBM((N_DEV, r, c_), shard.dtype),
                   jax.ShapeDtypeStruct((8, 128), F32)),
        in_specs=(HBM, HBM), out_specs=(SEM, SEM, HBM, HBM, pl.BlockSpec(memory_space=pltpu.VMEM)),
        input_output_aliases={0: 2, 1: 3}, compiler_params=SIDE_EFFECT,
    )(pltpu.with_memory_space_constraint(shard, pltpu.HBM), land)


def _gather_wait(send_sems, recv_sems, shard_thru, land_thru, after):
    def body(x_ref, land_ref, send_sems, recv_sems, after_ref, x_dead, got_ref):
        x, y, c = lax.axis_index("x"), lax.axis_index("y"), lax.axis_index("c")
        for k in range(N_DEV - 1):
            copy = pltpu.make_async_remote_copy(src_ref=x_ref, dst_ref=land_ref.at[0], send_sem=send_sems.at[k],
                                                recv_sem=recv_sems.at[k], device_id=(x, y, c), device_id_type=MESH)
            copy.wait_send()
            copy.wait_recv()

    return pl.pallas_call(
        body, name="gather_late_weights_wait",
        out_shape=(pltpu.HBM(shard_thru.shape, shard_thru.dtype), pltpu.HBM(land_thru.shape, land_thru.dtype)),
        in_specs=(HBM, HBM, SEM, SEM, ANY), out_specs=(HBM, HBM), input_output_aliases={0: 0, 1: 1},
        compiler_params=SIDE_EFFECT,
    )(shard_thru, land_thru, send_sems, recv_sems, after)[1]


def _hbm(a):
    return pltpu.with_memory_space_constraint(a, pltpu.HBM)


def _exchange_start(name, arrays, lands, plan, ncopies):
    n = len(arrays)

    def body(*refs):
        send_sems, recv_sems, token = refs[2 * n], refs[2 * n + 1], refs[-1]
        x, y, c = lax.axis_index("x"), lax.axis_index("y"), lax.axis_index("c")
        for k, (src, dst, peer) in enumerate(plan(x, y, c, refs[:n], refs[n:2 * n])):
            pltpu.make_async_remote_copy(src_ref=src, dst_ref=dst, send_sem=send_sems.at[k], recv_sem=recv_sems.at[k],
                                         device_id=peer, device_id_type=MESH).start()
        token[...] = jnp.zeros_like(token)

    both = list(arrays) + list(lands)
    outs = pl.pallas_call(
        body, name=name,
        out_shape=(pltpu.SemaphoreType.DMA((ncopies,)), pltpu.SemaphoreType.DMA((ncopies,)),
                   *[pltpu.HBM(a.shape, a.dtype) for a in both], jax.ShapeDtypeStruct((8, 128), F32)),
        in_specs=(HBM,) * (2 * n), out_specs=(SEM, SEM) + (HBM,) * (2 * n) + (pl.BlockSpec(memory_space=pltpu.VMEM),),
        input_output_aliases={i: 2 + i for i in range(2 * n)}, compiler_params=SIDE_EFFECT,
    )(*[_hbm(a) for a in both])
    return outs[0], outs[1], list(outs[2:2 + n]), list(outs[2 + n:2 + 2 * n]), outs[-1]


def _exchange_wait(name, send_sems, recv_sems, arrays, lands, plan, after):
    n = len(arrays)

    def body(*refs):
        send_sems, recv_sems = refs[2 * n], refs[2 * n + 1]
        x, y, c = lax.axis_index("x"), lax.axis_index("y"), lax.axis_index("c")
        for k, (src, dst, peer) in enumerate(plan(x, y, c, refs[:n], refs[n:2 * n])):
            copy = pltpu.make_async_remote_copy(src_ref=src, dst_ref=dst, send_sem=send_sems.at[k], recv_sem=recv_sems.at[k],
                                                device_id=peer, device_id_type=MESH)
            copy.wait_send()
            copy.wait_recv()

    both = list(arrays) + list(lands)
    outs = pl.pallas_call(
        body, name=name, out_shape=tuple(pltpu.HBM(a.shape, a.dtype) for a in both),
        in_specs=(HBM,) * (2 * n) + (SEM, SEM, ANY), out_specs=(HBM,) * (2 * n),
        input_output_aliases={i: i for i in range(2 * n)}, compiler_params=SIDE_EFFECT,
    )(*both, send_sems, recv_sems, after)
    return list(outs[:n]), list(outs[n:])


def _sibling_plan(x, y, c, g_refs, land_refs):
    return [(g.at[:, 2 * q + (1 - c)], o.at[:, q], (x, y, 1 - c)) for g, o in zip(g_refs, land_refs) for q in range(4)]


def _chips_plan(x, y, c, p_refs, land_refs):
    chips = [(1 - x, y), (x, 1 - y), (1 - x, 1 - y)]
    return [(p_.at[:, 2 * qx + qy], o.at[:, 2 * x + y], (qx, qy, c)) for p_, o in zip(p_refs, land_refs) for qx, qy in chips]


class _GradExchange:
    def __init__(self):
        self.core = lax.axis_index("c").astype(jnp.int32).reshape(1)
        self.chip = 2 * lax.axis_index("x") + lax.axis_index("y")
        self.groups = []

    def submit(self, tag, arrays, dtypes):
        arrays = [a.reshape(a.shape[0], N_DEV, a.shape[1] // N_DEV, a.shape[2]) for a in arrays]
        lands = [lax.empty((a.shape[0], 4) + a.shape[2:], a.dtype) for a in arrays]
        send, recv, arrays, lands, token = _exchange_start(f"rs_pair_start_{tag}", arrays, lands, _sibling_plan, 4 * len(arrays))
        self.groups.append(dict(tag=tag, stage=1, sems=(send, recv), arrays=arrays, lands=lands, dtypes=dtypes))
        return token

    def advance(self, after):
        token = None
        for g in self.groups:
            if g["stage"] == 1:
                arrays, got = _exchange_wait(f"rs_pair_wait_{g['tag']}", *g["sems"], g["arrays"], g["lands"], _sibling_plan, after)
                parts = [_rs_pair_add(a, b, self.core, dt) for a, b, dt in zip(arrays, got, g["dtypes"])]
                lands = [lax.empty(p_.shape, p_.dtype) for p_ in parts]
                send, recv, parts, lands, tok = _exchange_start(f"rs_chip_start_{g['tag']}", parts, lands, _chips_plan, 3 * len(parts))
                g.update(stage=2, sems=(send, recv), arrays=parts, lands=lands)
                token = tok if token is None else token + tok
            elif g["stage"] == 2:
                parts, lands = _exchange_wait(f"rs_chip_wait_{g['tag']}", *g["sems"], g["arrays"], g["lands"], _chips_plan, after)
                sums = []
                for p_, land in zip(parts, lands):
                    l, _, r, c_ = p_.shape
                    own = lax.dynamic_slice(p_, (0, self.chip, 0, 0), (l, 1, r, c_))
                    sums.append(_sum_parts(lax.dynamic_update_slice(land, own, (0, self.chip, 0, 0)), "sum_chips"))
                g.update(stage=3, sums=sums)
        return token

    def results(self):
        return [g["sums"] for g in self.groups]


def _with_rows(g, n):
    return jax.ShapeDtypeStruct((g.shape[0], n) + tuple(g.shape[2:]), g.dtype)


def _rs_sibling(gs):
    n = len(gs)

    def body(*refs):
        g_refs, o_refs, (send_sems, recv_sems) = refs[:n], refs[n:2 * n], refs[2 * n:]
        x, y, c = lax.axis_index("x"), lax.axis_index("y"), lax.axis_index("c")
        copies = [pltpu.make_async_remote_copy(
            src_ref=g_refs[w].at[:, 2 * q + (1 - c)], dst_ref=o_refs[w].at[:, q], send_sem=send_sems.at[4 * w + q],
            recv_sem=recv_sems.at[4 * w + q], device_id=(x, y, 1 - c), device_id_type=MESH)
            for w in range(n) for q in range(4)]
        for cp in copies:
            cp.start()
        for cp in copies:
            cp.wait_recv()
        for cp in copies:
            cp.wait_send()

    return pl.pallas_call(
        body, name="rs_sibling", out_shape=[_with_rows(g, 4) for g in gs],
        in_specs=[ANY] * n, out_specs=[ANY] * n,
        scratch_shapes=[pltpu.SemaphoreType.DMA((4 * n,)), pltpu.SemaphoreType.DMA((4 * n,))],
    )(*gs)


def _rs_pair_add(g, got, core, out_dtype):
    l, _, r, c_ = g.shape

    def body(core_ref, g_ref, got_ref, o_ref):
        o_ref[...] = (g_ref[...].astype(F32) + got_ref[...].astype(F32)).astype(out_dtype)

    blk = (None, None, r, c_)
    return pl.pallas_call(
        body, name="rs_pair_add", out_shape=jax.ShapeDtypeStruct((l, 4, r, c_), out_dtype),
        grid_spec=pltpu.PrefetchScalarGridSpec(
            num_scalar_prefetch=1, grid=(l, 4),
            in_specs=[pl.BlockSpec(blk, lambda i, q, core_ref: (i, 2 * q + core_ref[0], 0, 0)),
                      pl.BlockSpec(blk, lambda i, q, core_ref: (i, q, 0, 0))],
            out_specs=pl.BlockSpec(blk, lambda i, q, core_ref: (i, q, 0, 0))),
        compiler_params=_params("parallel", "parallel"),
    )(core, g, got)


def _rs_chips(parts):
    n = len(parts)

    def body(*refs):
        p_refs, o_refs, (send_sems, recv_sems, local_sems) = refs[:n], refs[n:2 * n], refs[2 * n:]
        x, y, c = lax.axis_index("x"), lax.axis_index("y"), lax.axis_index("c")
        my_chip = 2 * x + y
        chips = [(1 - x, y), (x, 1 - y), (1 - x, 1 - y)]
        local = [pltpu.make_async_copy(p_refs[w].at[:, my_chip], o_refs[w].at[:, my_chip], local_sems.at[w]) for w in range(n)]
        for cp in local:
            cp.start()
        copies = [pltpu.make_async_remote_copy(
            src_ref=p_refs[w].at[:, 2 * qx + qy], dst_ref=o_refs[w].at[:, my_chip], send_sem=send_sems.at[3 * w + k],
            recv_sem=recv_sems.at[3 * w + k], device_id=(qx, qy, c), device_id_type=MESH)
            for w in range(n) for k, (qx, qy) in enumerate(chips)]
        for cp in copies:
            cp.start()
        for cp in copies:
            cp.wait_recv()
        for cp in copies:
            cp.wait_send()
        for cp in local:
            cp.wait()

    return pl.pallas_call(
        body, name="rs_chips", out_shape=[jax.ShapeDtypeStruct(p.shape, p.dtype) for p in parts],
        in_specs=[ANY] * n, out_specs=[ANY] * n,
        scratch_shapes=[pltpu.SemaphoreType.DMA((3 * n,)), pltpu.SemaphoreType.DMA((3 * n,)), pltpu.SemaphoreType.DMA((n,))],
    )(*parts)


def _sum_parts(parts, name):
    l, n, r, c_ = parts.shape

    def body(p_ref, o_ref):
        g = p_ref[0].astype(F32)
        for s in range(1, n):
            g = g + p_ref[s].astype(F32)
        o_ref[...] = g

    return pl.pallas_call(
        body, name=name, out_shape=jax.ShapeDtypeStruct((l, r, c_), F32), grid=(l,),
        in_specs=[pl.BlockSpec((None, n, r, c_), lambda i: (i, 0, 0, 0))],
        out_specs=pl.BlockSpec((None, r, c_), lambda i: (i, 0, 0)), compiler_params=_params("parallel"),
    )(parts)


def _adamw_math(w, g, m, v):
    m = ADAM_B1 * m + (1.0 - ADAM_B1) * g
    v = ADAM_B2 * v + (1.0 - ADAM_B2) * (g * g)
    m_hat = m / (1.0 - ADAM_B1 ** ADAM_STEP)
    v_hat = v / (1.0 - ADAM_B2 ** ADAM_STEP)
    delta = -ADAM_LR * (m_hat / (jnp.sqrt(v_hat) + ADAM_EPS) + ADAM_WD * w)
    return delta, m, v


def _adamw(g, w, m, v, name):
    l, k, n = w.shape
    tk = 256 if k % 256 == 0 else k

    def body(g_ref, w_ref, m_ref, v_ref, d_ref, nm_ref, nv_ref):
        d_ref[...], nm_ref[...], nv_ref[...] = _adamw_math(w_ref[...], g_ref[...], m_ref[...], v_ref[...])

    spec = pl.BlockSpec((None, tk, n), lambda i, j: (i, j, 0))
    return pl.pallas_call(
        body, name=name, out_shape=[jax.ShapeDtypeStruct((l, k, n), F32)] * 3, grid=(l, k // tk),
        in_specs=[spec] * 4, out_specs=[spec] * 3, compiler_params=_params("parallel", "parallel"),
    )(g, w, m, v)


def _sum_adamw(parts, w, m, v, name):
    n, r, c_ = parts.shape

    def body(p_ref, w_ref, m_ref, v_ref, g_ref, d_ref, nm_ref, nv_ref):
        g = p_ref[0]
        for s in range(1, n):
            g = g + p_ref[s]
        g_ref[...] = g
        d_ref[...], nm_ref[...], nv_ref[...] = _adamw_math(w_ref[...], g, m_ref[...], v_ref[...])

    return pl.pallas_call(
        body, name=name, out_shape=[jax.ShapeDtypeStruct((r, c_), F32)] * 4, grid=(1,),
        in_specs=[_full((n, r, c_))] + [_full((r, c_))] * 3, out_specs=[_full((r, c_))] * 4,
        compiler_params=_params("arbitrary"),
    )(parts, w, m, v)


def _rope_tables(pos_col, freq_row):
    s = pos_col.shape[0]
    tm = min(1024, s)

    def body(p_ref, f_ref, c_ref, su_ref, sd_ref):
        ang = p_ref[...].astype(F32) * f_ref[...]
        lane = lax.broadcasted_iota(jnp.int32, ang.shape, 1) & (HEAD_DIM - 1)
        cs, sn = jnp.cos(ang), jnp.sin(ang)
        c_ref[...] = jnp.where(lane < ROT_DIM, cs, 1.0)
        su_ref[...] = jnp.where((lane >= ROT_DIM // 2) & (lane < ROT_DIM), sn, 0.0)
        sd_ref[...] = jnp.where(lane < ROT_DIM // 2, -sn, 0.0)

    return pl.pallas_call(
        body, name="rope_tables", out_shape=[jax.ShapeDtypeStruct((s, 128), F32)] * 3, grid=(s // tm,),
        in_specs=[pl.BlockSpec((tm, 1), lambda i: (i, 0)), _full((1, 128))],
        out_specs=[_rows(tm, 128)] * 3, compiler_params=_params("parallel"),
    )(pos_col, freq_row)


def _rope_apply(t, cos, sin_up, sin_dn):
    w = t.shape[1]
    return t * cos + pltpu.roll(t, 8, 1) * sin_up + pltpu.roll(t, w - 8, 1) * sin_dn


def _rope_transpose(dr, cos, sin_up, sin_dn):
    w = dr.shape[1]
    return dr * cos + pltpu.roll(dr * sin_up, w - 8, 1) + pltpu.roll(dr * sin_dn, 8, 1)


def _norm_matmul(x, g, wt, *, tn, name, bias=None, rope=None, rope_blocks=0, tm=512):
    s, d = x.shape
    n = wt.shape[0]
    tm = min(tm, s)

    def body(*refs):
        x_ref, g_ref, w_ref = refs[:3]
        k = 3
        b_ref = None
        if bias is not None:
            b_ref = refs[k]
            k += 1
        if rope is not None:
            c_ref, su_ref, sd_ref = refs[k:k + 3]
            k += 3
        h_ref, o_ref = refs[k:k + 2]
        j = pl.program_id(1)

        @pl.when(j == 0)
        def _():
            xv = x_ref[...]
            r = lax.rsqrt(jnp.mean(xv * xv, axis=-1, keepdims=True) + EPS)
            h_ref[...] = (xv * r * g_ref[...]).astype(BF16)

        acc = lax.dot_general(h_ref[...], w_ref[...], NT, preferred_element_type=F32)
        if b_ref is not None:
            acc = acc + b_ref[...]
        if rope is None:
            o_ref[...] = acc.astype(BF16)
        else:
            @pl.when(j < rope_blocks)
            def _():
                reps = tn // 128
                o_ref[...] = _rope_apply(acc, jnp.tile(c_ref[...], (1, reps)), jnp.tile(su_ref[...], (1, reps)),
                                         jnp.tile(sd_ref[...], (1, reps))).astype(BF16)

            @pl.when(j >= rope_blocks)
            def _():
                o_ref[...] = acc.astype(BF16)

    in_specs = [_rows(tm, d), _full((1, d)), pl.BlockSpec((tn, d), lambda i, j: (j, 0))]
    args = [x, g, wt]
    if bias is not None:
        in_specs.append(pl.BlockSpec((1, tn), lambda i, j: (0, j)))
        args.append(bias)
    if rope is not None:
        in_specs += [_rows(tm, 128)] * 3
        args += list(rope)
    return pl.pallas_call(
        body, name=name,
        out_shape=[jax.ShapeDtypeStruct((s, d), BF16), jax.ShapeDtypeStruct((s, n), BF16)],
        grid=(s // tm, n // tn), in_specs=in_specs,
        out_specs=[_rows(tm, d), pl.BlockSpec((tm, tn), lambda i, j: (i, j))],
        compiler_params=_params("parallel", "arbitrary"),
    )(*args)


def _class_major(tm, dil):
    p = np.zeros((tm, tm), np.float32)
    per = tm // dil
    for r in range(dil):
        for j in range(per):
            p[r * per + j, j * dil + r] = 1.0
    return jnp.asarray(p, dtype=BF16)


def _qkv_proj(x, g, wt, rope, tm=512):
    s, d = x.shape
    n = wt.shape[0]
    gw3 = 3 * GROUP_WIDTH
    tm = min(tm, s)
    assert n == 3 * gw3

    def body(x_ref, g_ref, w_ref, c_ref, su_ref, sd_ref, p1_ref, p2_ref, h_ref, o0_ref, o1_ref, o2_ref):
        j = pl.program_id(1)

        @pl.when(j == 0)
        def _():
            xv = x_ref[...]
            r = lax.rsqrt(jnp.mean(xv * xv, axis=-1, keepdims=True) + EPS)
            h_ref[...] = (xv * r * g_ref[...]).astype(BF16)

        acc = lax.dot_general(h_ref[...], w_ref[...], NT, preferred_element_type=F32)

        def store(y):
            yb = y.astype(BF16)
            o0_ref[:, pl.ds(pl.multiple_of(j * GROUP_WIDTH, GROUP_WIDTH), GROUP_WIDTH)] = yb[:, :GROUP_WIDTH]
            for grp, o_ref, p_ref in ((1, o1_ref, p1_ref), (2, o2_ref, p2_ref)):
                dil = DILATIONS[grp]
                per = tm // dil
                yp = jnp.dot(p_ref[...], yb[:, grp * GROUP_WIDTH:(grp + 1) * GROUP_WIDTH],
                             preferred_element_type=F32).astype(BF16)
                for r in range(dil):
                    col = pl.multiple_of(r * gw3 + j * GROUP_WIDTH, GROUP_WIDTH)
                    o_ref[:, pl.ds(col, GROUP_WIDTH)] = yp[r * per:(r + 1) * per, :]

        @pl.when(j < 2)
        def _():
            reps = gw3 // 128
            store(_rope_apply(acc, jnp.tile(c_ref[...], (1, reps)), jnp.tile(su_ref[...], (1, reps)),
                              jnp.tile(sd_ref[...], (1, reps))))

        @pl.when(j == 2)
        def _():
            store(acc)

    outs = [jax.ShapeDtypeStruct((s, d), BF16)] + [jax.ShapeDtypeStruct((s // dl, dl * gw3), BF16) for dl in DILATIONS]
    out_specs = [_rows(tm, d)] + [_rows(tm // dl, dl * gw3) for dl in DILATIONS]
    return pl.pallas_call(
        body, name="attn_qkv", out_shape=outs, grid=(s // tm, 3),
        in_specs=[_rows(tm, d), _full((1, d)), pl.BlockSpec((gw3, d), lambda i, j: (j, 0))] + [_rows(tm, 128)] * 3
        + [_full((tm, tm))] * 2,
        out_specs=out_specs, compiler_params=_params("parallel", "arbitrary"),
    )(x, g, wt, *rope, _class_major(tm, DILATIONS[1]), _class_major(tm, DILATIONS[2]))


def _head_masks(rows=SPAN):
    lane = lax.broadcasted_iota(jnp.int32, (rows, 128), 1)
    masks = [lane < HEAD_DIM, lane >= HEAD_DIM]
    lane1 = lax.broadcasted_iota(jnp.int32, (1, 128), 1)
    keep = [jnp.where(lane1 < HEAD_DIM, 1.0, 0.0).astype(BF16), jnp.where(lane1 >= HEAD_DIM, 1.0, 0.0).astype(BF16)]
    return masks, keep


def _attn_fwd(qv, grp, dil):
    l = qv.shape[0]
    s = l * dil
    nb = l // SPAN

    def body(q_ref, kp_ref, kc_ref, vp_ref, vc_ref, o_ref, l_ref):
        b = pl.program_id(1)
        row = lax.broadcasted_iota(jnp.int32, (SPAN, 2 * SPAN), 0)
        col = lax.broadcasted_iota(jnp.int32, (SPAN, 2 * SPAN), 1)
        no_prev = jnp.where(b > 0, 0, 4 * SPAN)
        valid = ((col < SPAN) & (col >= row + no_prev)) | ((col >= SPAN) & (col - SPAN <= row))
        masks, keep = _head_masks()
        for p in range(GROUP_WIDTH // 128):
            sl = slice(p * 128, (p + 1) * 128)
            qp = q_ref[:, sl]
            kk = jnp.concatenate([kp_ref[:, sl], kc_ref[:, sl]], axis=0)
            vv = jnp.concatenate([vp_ref[:, sl], vc_ref[:, sl]], axis=0)
            outs, lses = [], []
            for h in range(2):
                sc = lax.dot_general(qp * keep[h], kk, NT, preferred_element_type=F32) * (HEAD_DIM ** -0.5)
                sc = jnp.where(valid, sc, -1e30)
                mx = jnp.max(sc, axis=-1, keepdims=True)
                pe = jnp.exp(sc - mx)
                den = jnp.sum(pe, axis=-1, keepdims=True)
                pv = jnp.dot(pe.astype(BF16), vv, preferred_element_type=F32)
                outs.append(pv / den)
                lses.append(jnp.broadcast_to(mx + jnp.log(den), (SPAN, 128)))
            o_ref[:, sl] = jnp.where(masks[0], outs[0], outs[1])
            l_ref[:, sl] = jnp.where(masks[0], lses[0], lses[1])

    blk = (SPAN, GROUP_WIDTH)
    cur = lambda t: pl.BlockSpec(blk, lambda r, b: (b, r * 3 + t))
    prev = lambda t: pl.BlockSpec(blk, lambda r, b: (jnp.maximum(b - 1, 0), r * 3 + t))
    out = pl.BlockSpec(blk, lambda r, b: (b, r))
    o, lse = pl.pallas_call(
        body, name=f"attn_fwd_g{grp}", out_shape=[jax.ShapeDtypeStruct((l, dil * GROUP_WIDTH), F32)] * 2,
        grid=(dil, nb), in_specs=[cur(0), prev(1), cur(1), prev(2), cur(2)], out_specs=[out, out],
        compiler_params=_params("parallel", "arbitrary"),
    )(qv, qv, qv, qv, qv)
    return o.reshape(s, GROUP_WIDTH), lse.reshape(s, GROUP_WIDTH)


def _resnorm_store(y, x_ref, g_ref, y_ref, xo_ref):
    r = lax.rsqrt(jnp.mean(y * y, axis=-1, keepdims=True) + EPS)
    y_ref[...] = y
    xo_ref[...] = x_ref[...] + y * r * g_ref[...]


def _mix_wo(os_, ls_, wot, x, g, tm=256):
    s, d = x.shape
    gw = wot.shape[1]
    tm = min(tm, s)

    def body(o0, o1, o2, l0, l1, l2, w_ref, x_ref, g_ref, y_ref, xo_ref, mixed_ref, lse_ref):
        a0, a1, a2 = l0[...], l1[...], l2[...]
        mx = jnp.maximum(jnp.maximum(a0, a1), a2)
        e0, e1, e2 = jnp.exp(a0 - mx), jnp.exp(a1 - mx), jnp.exp(a2 - mx)
        den = e0 + e1 + e2
        mixed = (e0 / den) * o0[...] + (e1 / den) * o1[...] + (e2 / den) * o2[...]
        mixed_ref[...] = mixed.astype(BF16)
        lse_ref[...] = mx + jnp.log(den)
        y = lax.dot_general(mixed.astype(BF16), w_ref[...], NT, preferred_element_type=F32)
        _resnorm_store(y, x_ref, g_ref, y_ref, xo_ref)

    return pl.pallas_call(
        body, name="mix_wo",
        out_shape=[jax.ShapeDtypeStruct((s, d), F32), jax.ShapeDtypeStruct((s, d), F32),
                   jax.ShapeDtypeStruct((s, gw), BF16), jax.ShapeDtypeStruct((s, gw), F32)],
        grid=(s // tm,), in_specs=[_rows(tm, gw)] * 6 + [_full((d, gw)), _rows(tm, d), _full((1, d))],
        out_specs=[_rows(tm, d), _rows(tm, d), _rows(tm, gw), _rows(tm, gw)],
        compiler_params=_params("parallel"),
    )(*os_, *ls_, wot, x, g)


def _matmul_resnorm(a, w, x, g, *, name, bias=None, tm=512):
    s, k = a.shape
    d = w.shape[1]
    tm = min(tm, s)

    def body(*refs):
        a_ref, w_ref = refs[:2]
        b_ref = refs[2] if bias is not None else None
        x_ref, g_ref, y_ref, xo_ref = refs[-4:]
        y = jnp.dot(a_ref[...], w_ref[...], preferred_element_type=F32)
        if b_ref is not None:
            y = y + b_ref[...]
        _resnorm_store(y, x_ref, g_ref, y_ref, xo_ref)

    in_specs = [_rows(tm, k), _full((k, d))] + ([_full((1, d))] if bias is not None else []) + [_rows(tm, d), _full((1, d))]
    args = [a, w] + ([bias] if bias is not None else []) + [x, g]
    return pl.pallas_call(
        body, name=name, out_shape=[jax.ShapeDtypeStruct((s, d), F32)] * 2, grid=(s // tm,),
        in_specs=in_specs, out_specs=[_rows(tm, d)] * 2, compiler_params=_params("parallel"),
    )(*args)


def _conv3_taps(z, halo, first):
    row = lax.broadcasted_iota(jnp.int32, z.shape, 0)
    halo = halo * jnp.where(first, 0.0, 1.0)
    h6, h7 = halo[6:7, :], halo[7:8, :]
    z1 = jnp.where(row == 0, h7, pltpu.roll(z, 1, 0))
    z2 = jnp.where(row == 0, h6, jnp.where(row == 1, h7, pltpu.roll(z, 2, 0)))
    return z2, z1


def _ffn_cols(f):
    return _tile(f)


def _lane_chunks(width, fn):
    def step(k, carry):
        fn(pl.ds(pl.multiple_of(k * 128, 128), 128))
        return carry

    lax.fori_loop(0, width // 128, step, 0)


def _ffn_act(z, w_dw, b_dw, tm=256):
    s, f2 = z.shape
    f = f2 // 2
    tm = min(tm, s)
    tc = _ffn_cols(f)
    nfc = f // tc

    def body(zu, zg, hu, hg, wu, wg, bu, bg, o_ref):
        first = pl.program_id(0) == 0

        def chunk(cs):
            def conv(z_ref, h_ref, w_ref, b_ref):
                zc = z_ref[:, cs].astype(F32)
                z2, z1 = _conv3_taps(zc, h_ref[:, cs].astype(F32), first)
                return w_ref[0:1, cs] * z2 + w_ref[1:2, cs] * z1 + w_ref[2:3, cs] * zc + b_ref[:, cs]

            up, gate = conv(zu, hu, wu, bu), conv(zg, hg, wg, bg)
            o_ref[:, cs] = (gate * _sigmoid(gate) * up).astype(BF16)

        _lane_chunks(tc, chunk)

    hb = tm // 8
    tile = lambda off: pl.BlockSpec((tm, tc), lambda i, j: (i, off + j))
    halo = lambda off: pl.BlockSpec((8, tc), lambda i, j: (jnp.maximum(i * hb - 1, 0), off + j))
    prm = lambda rows, off: pl.BlockSpec((rows, tc), lambda i, j: (0, off + j))
    return pl.pallas_call(
        body, name="ffn_act", out_shape=jax.ShapeDtypeStruct((s, f), BF16), grid=(s // tm, nfc),
        in_specs=[tile(0), tile(nfc), halo(0), halo(nfc), prm(FFN_CONV, 0), prm(FFN_CONV, nfc), prm(1, 0), prm(1, nfc)],
        out_specs=pl.BlockSpec((tm, tc), lambda i, j: (i, j)), compiler_params=_params("parallel", "parallel"),
    )(z, z, z, z, w_dw, w_dw, b_dw, b_dw)


def _shifted_planes(ext_ref):
    rows = ext_ref.shape[1]
    for s in range(1, 8):
        ext_ref[s, 0:rows - 8, :] = ext_ref[0, s:s + rows - 8, :]


def _window(ext_ref, off, tm, cs):
    s = off % 8
    return ext_ref[s, off - s:off - s + tm, cs]


def _conv_taps(ext_ref, w_ref, offs, tm, out_ref):
    def chunk(cs):
        acc = w_ref[0:1, cs] * _window(ext_ref, offs[0], tm, cs)
        for j in range(1, len(offs)):
            acc = acc + w_ref[j:j + 1, cs] * _window(ext_ref, offs[j], tm, cs)
        out_ref[:, cs] = acc

    _lane_chunks(out_ref.shape[1], chunk)


def _glu_planes(ag_ref, halo_ref, ext_ref, first, c):
    hal = halo_ref[...].astype(F32)
    ext_ref[0, 0:CONV_HALO, :] = hal[:, :c] * _sigmoid(hal[:, c:]) * jnp.where(first, 0.0, 1.0)
    ag = ag_ref[...].astype(F32)
    ext_ref[0, CONV_HALO:, :] = ag[:, :c] * _sigmoid(ag[:, c:])
    _shifted_planes(ext_ref)


def _layernorm_stats(u1):
    mu = jnp.mean(u1, axis=-1, keepdims=True)
    cen = u1 - mu
    rstd = lax.rsqrt(jnp.mean(cen * cen, axis=-1, keepdims=True) + EPS)
    return cen * rstd, rstd


def _conv_mid(ag, w_dw, b_dw, ln_g, ln_b, tm=256):
    s, c2 = ag.shape
    c = c2 // 2
    tm = min(tm, s)

    def body(ag_ref, halo_ref, w_ref, b_ref, g_ref, bb_ref, o_ref, u1_ref, ext_ref):
        _glu_planes(ag_ref, halo_ref, ext_ref, pl.program_id(0) == 0, c)
        base = CONV_HALO - (CONV_KERNEL - 1)
        _conv_taps(ext_ref, w_ref, [base + j for j in range(CONV_KERNEL)], tm, u1_ref)
        xh, _ = _layernorm_stats(u1_ref[...] + b_ref[...])
        u2 = xh * g_ref[...] + bb_ref[...]
        o_ref[...] = (u2 * _sigmoid(u2)).astype(BF16)

    hb = tm // CONV_HALO
    return pl.pallas_call(
        body, name="conv_mid", out_shape=[jax.ShapeDtypeStruct((s, c), BF16), jax.ShapeDtypeStruct((s, c), F32)], grid=(s // tm,),
        in_specs=[_rows(tm, c2), pl.BlockSpec((CONV_HALO, c2), lambda i: (jnp.maximum(i * hb - 1, 0), 0)),
                  _full((CONV_KERNEL, c)), _full((1, c)), _full((1, c)), _full((1, c))],
        out_specs=[_rows(tm, c), _rows(tm, c)], scratch_shapes=[pltpu.VMEM((8, CONV_HALO + tm, c), F32)],
        compiler_params=_params("arbitrary"),
    )(ag, ag, w_dw, b_dw, ln_g, ln_b)


def _loss_grad(xo, target, tm=512):
    s, d = xo.shape
    tm = min(tm, s)

    def body(x_ref, t_ref, dx_ref, loss_ref):
        @pl.when(pl.program_id(0) == 0)
        def _():
            loss_ref[...] = jnp.zeros_like(loss_ref)

        err = x_ref[...] - t_ref[...]
        dx_ref[...] = err * (1.0 / d)
        loss_ref[...] += 0.5 * jnp.sum(jnp.mean(err * err, axis=-1, keepdims=True))

    return pl.pallas_call(
        body, name="loss_grad", out_shape=[jax.ShapeDtypeStruct((s, d), F32), jax.ShapeDtypeStruct((1, 128), F32)],
        grid=(s // tm,), in_specs=[_rows(tm, d)] * 2, out_specs=[_rows(tm, d), _full((1, 128))],
        compiler_params=_params("arbitrary"),
    )(xo, target)


def _postnorm_bwd(y, g, dxo, *, name, with_bias_grad=False, tm=512):
    s, d = y.shape
    tm = min(tm, s)

    def body(y_ref, g_ref, dx_ref, dy_ref, dg_ref, *rest):
        @pl.when(pl.program_id(0) == 0)
        def _():
            dg_ref[...] = jnp.zeros_like(dg_ref)
            for r_ in rest:
                r_[...] = jnp.zeros_like(r_)

        yv, dxo_v = y_ref[...], dx_ref[...]
        r = lax.rsqrt(jnp.mean(yv * yv, axis=-1, keepdims=True) + EPS)
        yh = yv * r
        dyh = dxo_v * g_ref[...]
        dy = r * (dyh - yh * jnp.mean(dyh * yh, axis=-1, keepdims=True))
        dy_ref[...] = dy.astype(BF16)
        dg_ref[...] += jnp.sum(dxo_v * yh, axis=0, keepdims=True)
        for r_ in rest:
            r_[...] += jnp.sum(dy, axis=0, keepdims=True)

    nacc = 2 if with_bias_grad else 1
    return pl.pallas_call(
        body, name=name, out_shape=[jax.ShapeDtypeStruct((s, d), BF16)] + [jax.ShapeDtypeStruct((1, d), F32)] * nacc,
        grid=(s // tm,), in_specs=[_rows(tm, d), _full((1, d)), _rows(tm, d)],
        out_specs=[_rows(tm, d)] + [_full((1, d))] * nacc, compiler_params=_params("arbitrary"),
    )(y, g, dxo)


def _matmul(gmat, w, *, name, out_dtype, transposed_w, tm=512):
    s, k = gmat.shape
    n = w.shape[0] if transposed_w else w.shape[1]
    tm = min(tm, s)

    def body(g_ref, w_ref, o_ref):
        if transposed_w:
            acc = lax.dot_general(g_ref[...], w_ref[...], NT, preferred_element_type=F32)
        else:
            acc = jnp.dot(g_ref[...], w_ref[...], preferred_element_type=F32)
        o_ref[...] = acc.astype(out_dtype)

    return pl.pallas_call(
        body, name=name, out_shape=jax.ShapeDtypeStruct((s, n), out_dtype), grid=(s // tm,),
        in_specs=[_rows(tm, k), _full(w.shape)], out_specs=_rows(tm, n), compiler_params=_params("parallel"),
    )(gmat, w)


def _matmul_prenorm_bwd(pieces, wt, x, g, dres, *, name, tm=256):
    s, d = x.shape
    tm = min(tm, s)
    np_ = len(pieces)

    def body(*refs):
        p_refs, w_refs = refs[:np_], refs[np_:2 * np_]
        x_ref, g_ref, r_ref, dx_ref, dg_ref = refs[2 * np_:]

        @pl.when(pl.program_id(0) == 0)
        def _():
            dg_ref[...] = jnp.zeros_like(dg_ref)

        dh = None
        for p_ref, w_ref in zip(p_refs, w_refs):
            t = jnp.dot(p_ref[...], w_ref[...], preferred_element_type=F32)
            dh = t if dh is None else dh + t
        xv = x_ref[...]
        r = lax.rsqrt(jnp.mean(xv * xv, axis=-1, keepdims=True) + EPS)
        xh = xv * r
        dyh = dh * g_ref[...]
        dx_ref[...] = r_ref[...] + r * (dyh - xh * jnp.mean(dyh * xh, axis=-1, keepdims=True))
        dg_ref[...] += jnp.sum(dh * xh, axis=0, keepdims=True)

    in_specs = []
    for _, c0, kc, _ in pieces:
        assert c0 % kc == 0
        in_specs.append(pl.BlockSpec((tm, kc), lambda i, _b=c0 // kc: (i, _b)))
    for _, _, kc, r0 in pieces:
        assert r0 % kc == 0
        in_specs.append(pl.BlockSpec((kc, d), lambda i, _b=r0 // kc: (_b, 0)))
    in_specs += [_rows(tm, d), _full((1, d)), _rows(tm, d)]
    return pl.pallas_call(
        body, name=name, out_shape=[jax.ShapeDtypeStruct((s, d), F32), jax.ShapeDtypeStruct((1, d), F32)],
        grid=(s // tm,), in_specs=in_specs, out_specs=[_rows(tm, d), _full((1, d))],
        compiler_params=_params("arbitrary"),
    )(*[p[0] for p in pieces], *[wt] * np_, x, g, dres)


def _weight_grad(a, gmat, *, name, a_col0=0, ka=None, out=None, out_shape=None, layer=0, row0=0, ts=1024):
    s = a.shape[0]
    ka = a.shape[1] if ka is None else ka
    n = gmat.shape[1]
    ts = min(ts, s)
    tka = _tile(ka, a_col0, row0)
    shape = out.shape if out is not None else out_shape
    nsteps = s // ts

    def body(a_ref, g_ref, *rest):
        o_ref, acc_ref = rest[-2:]
        i = pl.program_id(1)

        @pl.when(i == 0)
        def _():
            acc_ref[...] = jnp.zeros_like(acc_ref)

        acc_ref[...] += lax.dot_general(a_ref[...], g_ref[...], TN, preferred_element_type=F32)

        @pl.when(i == nsteps - 1)
        def _():
            o_ref[...] = acc_ref[...].astype(BF16)

    in_specs = [pl.BlockSpec((ts, tka), lambda k, i: (i, a_col0 // tka + k)), pl.BlockSpec((ts, n), lambda k, i: (i, 0))]
    args = [a, gmat]
    aliases = {}
    if out is not None:
        in_specs.append(ANY)
        args.append(out)
        aliases = {2: 0}
    return pl.pallas_call(
        body, name=name, out_shape=jax.ShapeDtypeStruct(shape, BF16), grid=(ka // tka, nsteps), in_specs=in_specs,
        out_specs=pl.BlockSpec((None, tka, n), lambda k, i: (layer, row0 // tka + k, 0)),
        scratch_shapes=[pltpu.VMEM((tka, n), F32)],
        input_output_aliases=aliases, compiler_params=_params("parallel", "arbitrary"),
    )(*args)


def _ffn_act_bwd(z, dact, w_dw, b_dw, tm=256):
    s, f2 = z.shape
    f = f2 // 2
    tm = min(tm, s)
    tc = _ffn_cols(f)
    nfc = f // tc

    def body(zu, zg, hu, hg, wu, wg, bu, bg, da_ref, du_ref, dgt_ref, dbu_ref, dbg_ref, dwu_ref, dwg_ref):
        i = pl.program_id(1)

        @pl.when(i == 0)
        def _():
            for r_ in (dbu_ref, dbg_ref, dwu_ref, dwg_ref):
                r_[...] = jnp.zeros_like(r_)

        def chunk(cs):
            def conv(z_ref, h_ref, w_ref, b_ref):
                zc = z_ref[:, cs].astype(F32)
                z2, z1 = _conv3_taps(zc, h_ref[:, cs].astype(F32), i == 0)
                return (z2, z1, zc), w_ref[0:1, cs] * z2 + w_ref[1:2, cs] * z1 + w_ref[2:3, cs] * zc + b_ref[:, cs]

            taps_u, up = conv(zu, hu, wu, bu)
            taps_g, gate = conv(zg, hg, wg, bg)
            da = da_ref[:, cs].astype(F32)
            sg = _sigmoid(gate)
            d_up = da * (gate * sg)
            d_gate = da * up * (sg * (1.0 + gate * (1.0 - sg)))
            du_ref[:, cs] = d_up.astype(BF16)
            dgt_ref[:, cs] = d_gate.astype(BF16)
            for dv, taps, db_ref, dw_ref in ((d_up, taps_u, dbu_ref, dwu_ref), (d_gate, taps_g, dbg_ref, dwg_ref)):
                db_ref[:, cs] += jnp.sum(dv, axis=0, keepdims=True)
                for k_, tap in enumerate(taps):
                    dw_ref[k_:k_ + 1, cs] += jnp.sum(dv * tap, axis=0, keepdims=True)

        _lane_chunks(tc, chunk)

    hb = tm // 8
    tile = lambda off: pl.BlockSpec((tm, tc), lambda j, i: (i, off + j))
    halo = lambda off: pl.BlockSpec((8, tc), lambda j, i: (jnp.maximum(i * hb - 1, 0), off + j))
    prm = lambda rows, off: pl.BlockSpec((rows, tc), lambda j, i: (0, off + j))
    acc = lambda rows: pl.BlockSpec((rows, tc), lambda j, i: (0, j))
    return pl.pallas_call(
        body, name="ffn_act_bwd",
        out_shape=[jax.ShapeDtypeStruct((s, f), BF16)] * 2 + [jax.ShapeDtypeStruct((1, f), F32)] * 2
        + [jax.ShapeDtypeStruct((FFN_CONV, f), F32)] * 2,
        grid=(nfc, s // tm),
        in_specs=[tile(0), tile(nfc), halo(0), halo(nfc), prm(FFN_CONV, 0), prm(FFN_CONV, nfc), prm(1, 0), prm(1, nfc), tile(0)],
        out_specs=[tile(0), tile(0), acc(1), acc(1), acc(FFN_CONV), acc(FFN_CONV)],
        compiler_params=_params("parallel", "arbitrary"),
    )(z, z, z, z, w_dw, w_dw, b_dw, b_dw, dact)


def _conv3_transpose(dug, w_dw, col0, tm=256):
    s, f = dug.shape
    tm = min(tm, s)
    tc = _ffn_cols(f)
    nfc = f // tc
    nrow = s // tm
    off = col0 // tc

    def body(d_ref, n_ref, w_ref, o_ref):
        keep_next = jnp.where(pl.program_id(0) == nrow - 1, 0.0, 1.0)

        def chunk(cs):
            dv = d_ref[:, cs].astype(F32)
            nxt = n_ref[:, cs].astype(F32) * keep_next
            n0, n1 = nxt[0:1, :], nxt[1:2, :]
            row = lax.broadcasted_iota(jnp.int32, dv.shape, 0)
            d1 = jnp.where(row == tm - 1, n0, pltpu.roll(dv, tm - 1, 0))
            d2 = jnp.where(row == tm - 1, n1, jnp.where(row == tm - 2, n0, pltpu.roll(dv, tm - 2, 0)))
            o_ref[:, cs] = (w_ref[2:3, cs] * dv + w_ref[1:2, cs] * d1 + w_ref[0:1, cs] * d2).astype(BF16)

        _lane_chunks(tc, chunk)

    hb = tm // 8
    return pl.pallas_call(
        body, name="conv3_transpose", out_shape=jax.ShapeDtypeStruct((s, f), BF16), grid=(nrow, nfc),
        in_specs=[pl.BlockSpec((tm, tc), lambda i, j: (i, j)),
                  pl.BlockSpec((8, tc), lambda i, j: (jnp.minimum((i + 1) * hb, s // 8 - 1), j)),
                  pl.BlockSpec((FFN_CONV, tc), lambda i, j: (0, off + j))],
        out_specs=pl.BlockSpec((tm, tc), lambda i, j: (i, j)), compiler_params=_params("parallel", "parallel"),
    )(dug, dug, w_dw)


def _conv_mid_bwd(ag, u1, du3, b_dw, ln_g, ln_b, tm=256):
    s, c2 = ag.shape
    c = c2 // 2
    tm = min(tm, s)

    def body(ag_ref, halo_ref, u1in_ref, du_ref, b_ref, g_ref, bb_ref, o_ref, dlg_ref, dlb_ref, db_ref, dw_ref, ext_ref, u1_ref):
        @pl.when(pl.program_id(0) == 0)
        def _():
            for r_ in (dlg_ref, dlb_ref, db_ref, dw_ref):
                r_[...] = jnp.zeros_like(r_)

        _glu_planes(ag_ref, halo_ref, ext_ref, pl.program_id(0) == 0, c)
        xh, rstd = _layernorm_stats(u1in_ref[...] + b_ref[...])
        u2 = xh * g_ref[...] + bb_ref[...]
        sg = _sigmoid(u2)
        du2 = du_ref[...] * (sg * (1.0 + u2 * (1.0 - sg)))
        dlg_ref[...] += jnp.sum(du2 * xh, axis=0, keepdims=True)
        dlb_ref[...] += jnp.sum(du2, axis=0, keepdims=True)
        dxh = du2 * g_ref[...]
        du1 = rstd * (dxh - jnp.mean(dxh, axis=-1, keepdims=True) - xh * jnp.mean(dxh * xh, axis=-1, keepdims=True))
        o_ref[...] = du1.astype(BF16)
        db_ref[...] += jnp.sum(du1, axis=0, keepdims=True)
        u1_ref[...] = du1
        base = CONV_HALO - (CONV_KERNEL - 1)

        def chunk(cs):
            dc = u1_ref[:, cs]
            for j in range(CONV_KERNEL):
                dw_ref[j:j + 1, cs] += jnp.sum(dc * _window(ext_ref, base + j, tm, cs), axis=0, keepdims=True)

        _lane_chunks(c, chunk)

    hb = tm // CONV_HALO
    vec = _full((1, c))
    return pl.pallas_call(
        body, name="conv_mid_bwd",
        out_shape=[jax.ShapeDtypeStruct((s, c), BF16)] + [jax.ShapeDtypeStruct((1, c), F32)] * 3
        + [jax.ShapeDtypeStruct((CONV_HALO, c), F32)],
        grid=(s // tm,),
        in_specs=[_rows(tm, c2), pl.BlockSpec((CONV_HALO, c2), lambda i: (jnp.maximum(i * hb - 1, 0), 0)), _rows(tm, c),
                  _rows(tm, c), vec, vec, vec],
        out_specs=[_rows(tm, c), vec, vec, vec, _full((CONV_HALO, c))],
        scratch_shapes=[pltpu.VMEM((8, CONV_HALO + tm, c), F32), pltpu.VMEM((tm, c), F32)],
        compiler_params=_params("arbitrary"),
    )(ag, ag, u1, du3, b_dw, ln_g, ln_b)


def _glu_conv_bwd(du1, ag, w_dw, tm=256):
    s, c = du1.shape
    tm = min(tm, s)
    nrow = s // tm

    def body(d_ref, n_ref, ag_ref, w_ref, o_ref, db_ref, ext_ref, du0_ref):
        @pl.when(pl.program_id(0) == 0)
        def _():
            db_ref[...] = jnp.zeros_like(db_ref)

        ext_ref[0, 0:tm, :] = d_ref[...].astype(F32)
        ext_ref[0, tm:, :] = n_ref[...].astype(F32) * jnp.where(pl.program_id(0) == nrow - 1, 0.0, 1.0)
        _shifted_planes(ext_ref)
        top = CONV_KERNEL - 1
        _conv_taps(ext_ref, w_ref, [top - j for j in range(CONV_KERNEL)], tm, du0_ref)
        du0 = du0_ref[...]
        ag = ag_ref[...].astype(F32)
        a, gt = ag[:, :c], ag[:, c:]
        sg = _sigmoid(gt)
        da = du0 * sg
        dgt = du0 * a * (sg * (1.0 - sg))
        o_ref[:, :c] = da.astype(BF16)
        o_ref[:, c:] = dgt.astype(BF16)
        db_ref[:, :c] += jnp.sum(da, axis=0, keepdims=True)
        db_ref[:, c:] += jnp.sum(dgt, axis=0, keepdims=True)

    hb = tm // CONV_HALO
    return pl.pallas_call(
        body, name="glu_conv_bwd",
        out_shape=[jax.ShapeDtypeStruct((s, 2 * c), BF16), jax.ShapeDtypeStruct((1, 2 * c), F32)], grid=(nrow,),
        in_specs=[_rows(tm, c), pl.BlockSpec((CONV_HALO, c), lambda i: (jnp.minimum((i + 1) * hb, s // CONV_HALO - 1), 0)),
                  _rows(tm, 2 * c), _full((CONV_KERNEL, c))],
        out_specs=[_rows(tm, 2 * c), _full((1, 2 * c))],
        scratch_shapes=[pltpu.VMEM((8, tm + CONV_HALO, c), F32), pltpu.VMEM((tm, c), F32)],
        compiler_params=_params("arbitrary"),
    )(du1, du1, ag, w_dw)


def _head_rows(v, mask):
    return jnp.max(jnp.where(mask, v, -jnp.inf), axis=-1, keepdims=True)


def _attn_bwd_dq(qv, dmix, mixed, lse, rope, grp, dil):
    l = qv.shape[0]
    s = l * dil
    nb = l // SPAN
    view = lambda t: t.reshape(l, dil * t.shape[1])

    def body(q_ref, kp_ref, kc_ref, vp_ref, vc_ref, do_ref, mx_ref, l_ref, c_ref, su_ref, sd_ref, o_ref):
        b = pl.program_id(1)
        row = lax.broadcasted_iota(jnp.int32, (SPAN, 2 * SPAN), 0)
        col = lax.broadcasted_iota(jnp.int32, (SPAN, 2 * SPAN), 1)
        no_prev = jnp.where(b > 0, 0, 4 * SPAN)
        valid = ((col < SPAN) & (col >= row + no_prev)) | ((col >= SPAN) & (col - SPAN <= row))
        masks, keep = _head_masks()
        for p in range(GROUP_WIDTH // 128):
            sl = slice(p * 128, (p + 1) * 128)
            qp, dop = q_ref[:, sl], do_ref[:, sl]
            kk = jnp.concatenate([kp_ref[:, sl], kc_ref[:, sl]], axis=0)
            vv = jnp.concatenate([vp_ref[:, sl], vc_ref[:, sl]], axis=0)
            prod = dop.astype(F32) * mx_ref[:, sl].astype(F32)
            lsep = l_ref[:, sl]
            dqs = []
            for h in range(2):
                qh, doh = qp * keep[h], dop * keep[h]
                sc = lax.dot_general(qh, kk, NT, preferred_element_type=F32) * (HEAD_DIM ** -0.5)
                pe = jnp.where(valid, jnp.exp(sc - _head_rows(lsep, masks[h])), 0.0)
                dp = lax.dot_general(doh, vv, NT, preferred_element_type=F32)
                dbar = jnp.sum(jnp.where(masks[h], prod, 0.0), axis=-1, keepdims=True)
                ds = pe * (dp - dbar) * (HEAD_DIM ** -0.5)
                dqs.append(jnp.dot(ds.astype(BF16), kk, preferred_element_type=F32))
            dq = jnp.where(masks[0], dqs[0], dqs[1])
            o_ref[:, sl] = _rope_transpose(dq, c_ref[...], su_ref[...], sd_ref[...]).astype(BF16)

    blk = (SPAN, GROUP_WIDTH)
    cur = lambda t: pl.BlockSpec(blk, lambda r, b: (b, r * 3 + t))
    prev = lambda t: pl.BlockSpec(blk, lambda r, b: (jnp.maximum(b - 1, 0), r * 3 + t))
    own = pl.BlockSpec(blk, lambda r, b: (b, r))
    tab = pl.BlockSpec((SPAN, 128), lambda r, b: (b, r))
    out = pl.pallas_call(
        body, name=f"attn_bwd_dq_g{grp}", out_shape=jax.ShapeDtypeStruct((l, dil * GROUP_WIDTH), BF16), grid=(dil, nb),
        in_specs=[cur(0), prev(1), cur(1), prev(2), cur(2), own, own, own, tab, tab, tab], out_specs=own,
        compiler_params=_params("parallel", "arbitrary"),
    )(qv, qv, qv, qv, qv, view(dmix), view(mixed), view(lse), *[view(t) for t in rope])
    return out.reshape(s, GROUP_WIDTH)


def _attn_bwd_dkv(qv, dmix, mixed, lse, rope, grp, dil):
    l = qv.shape[0]
    s = l * dil
    nb = l // SPAN
    view = lambda t: t.reshape(l, dil * t.shape[1])

    def body(k_ref, v_ref, qc_ref, qn_ref, doc_ref, don_ref, mc_ref, mn_ref, lc_ref, ln_ref,
             c_ref, su_ref, sd_ref, o_ref):
        b = pl.program_id(1)
        row = lax.broadcasted_iota(jnp.int32, (2 * SPAN, SPAN), 0)
        col = lax.broadcasted_iota(jnp.int32, (2 * SPAN, SPAN), 1)
        no_next = jnp.where(b < nb - 1, 0, 4 * SPAN)
        valid = ((row < SPAN) & (col <= row)) | ((row >= SPAN) & (col >= row - SPAN + no_next))
        masks, keep = _head_masks()
        masks2, _ = _head_masks(2 * SPAN)
        for p in range(GROUP_WIDTH // 128):
            sl = slice(p * 128, (p + 1) * 128)
            kp, vp = k_ref[:, sl], v_ref[:, sl]
            qq = jnp.concatenate([qc_ref[:, sl], qn_ref[:, sl]], axis=0)
            doo = jnp.concatenate([doc_ref[:, sl], don_ref[:, sl]], axis=0)
            mm = jnp.concatenate([mc_ref[:, sl], mn_ref[:, sl]], axis=0)
            ll = jnp.concatenate([lc_ref[:, sl], ln_ref[:, sl]], axis=0)
            prod = doo.astype(F32) * mm.astype(F32)
            dks, dvs = [], []
            for h in range(2):
                qh, doh = qq * keep[h], doo * keep[h]
                sc = lax.dot_general(qh, kp, NT, preferred_element_type=F32) * (HEAD_DIM ** -0.5)
                pe = jnp.where(valid, jnp.exp(sc - _head_rows(ll, masks2[h])), 0.0)
                dp = lax.dot_general(doh, vp, NT, preferred_element_type=F32)
                dbar = jnp.sum(jnp.where(masks2[h], prod, 0.0), axis=-1, keepdims=True)
                ds = pe * (dp - dbar) * (HEAD_DIM ** -0.5)
                dvs.append(lax.dot_general(pe.astype(BF16), doo, TN, preferred_element_type=F32))
                dks.append(lax.dot_general(ds.astype(BF16), qq, TN, preferred_element_type=F32))
            dk = jnp.where(masks[0], dks[0], dks[1])
            o_ref[:, sl] = _rope_transpose(dk, c_ref[...], su_ref[...], sd_ref[...]).astype(BF16)
            o_ref[:, GROUP_WIDTH + p * 128:GROUP_WIDTH + (p + 1) * 128] = jnp.where(masks[0], dvs[0], dvs[1]).astype(BF16)

    blk = (SPAN, GROUP_WIDTH)
    nxt_b = lambda b: jnp.minimum(b + 1, nb - 1)
    col_of = lambda t: pl.BlockSpec(blk, lambda r, b: (b, r * 3 + t))
    q_next = pl.BlockSpec(blk, lambda r, b: (nxt_b(b), r * 3))
    own = pl.BlockSpec(blk, lambda r, b: (b, r))
    own_next = pl.BlockSpec(blk, lambda r, b: (nxt_b(b), r))
    tab = pl.BlockSpec((SPAN, 128), lambda r, b: (b, r))
    dv_, mv, lv = view(dmix), view(mixed), view(lse)
    out = pl.pallas_call(
        body, name=f"attn_bwd_dkv_g{grp}", out_shape=jax.ShapeDtypeStruct((l, dil * 2 * GROUP_WIDTH), BF16), grid=(dil, nb),
        in_specs=[col_of(1), col_of(2), col_of(0), q_next, own, own_next, own, own_next, own, own_next, tab, tab, tab],
        out_specs=pl.BlockSpec((SPAN, 2 * GROUP_WIDTH), lambda r, b: (b, r)),
        compiler_params=_params("parallel", "arbitrary"),
    )(qv, qv, qv, qv, dv_, dv_, mv, mv, lv, lv, *[view(t) for t in rope])
    return out.reshape(s, 2 * GROUP_WIDTH)


def _rope_freq_row():
    half = ROT_DIM // 2
    inv = (ROPE_THETA ** (-np.arange(half, dtype=np.float32) / half)).astype(np.float32)
    row = np.zeros((1, 128), np.float32)
    for head in range(128 // HEAD_DIM):
        row[0, head * HEAD_DIM:head * HEAD_DIM + half] = inv
        row[0, head * HEAD_DIM + half:head * HEAD_DIM + ROT_DIM] = inv
    return jnp.asarray(row)


def _ffn_fwd(x, g_pre, g_post, w_up_t, w_dw, b_dw, w_down):
    h, z = _norm_matmul(x, g_pre, w_up_t, tn=_tile(w_up_t.shape[0]), name="ffn_up")
    act = _ffn_act(z, w_dw, b_dw)
    y, xo = _matmul_resnorm(act, w_down, x, g_post, name="ffn_down")
    return xo, (x, h, z, act, y)


def _ffn_bwd(saved, dxo, g_pre, g_post, w_up_t, w_dw, b_dw, w_down):
    x, h, z, act, y = saved
    f = act.shape[1]
    d = x.shape[1]
    dy, dg_post = _postnorm_bwd(y, g_post, dxo, name="ffn_post_bwd")
    dact = _matmul(dy, w_down, name="ffn_dact", out_dtype=BF16, transposed_w=True)
    d_down = _weight_grad(act, dy, name="ffn_dw_down", out_shape=(1, f, d))
    dug_u, dug_g, db_u, db_g, dwd_u, dwd_g = _ffn_act_bwd(z, dact, w_dw, b_dw)
    dz_u = _conv3_transpose(dug_u, w_dw, 0)
    dz_g = _conv3_transpose(dug_g, w_dw, f)
    dx, dg_pre = _matmul_prenorm_bwd([(dz_u, 0, f, 0), (dz_g, 0, f, f)], w_up_t, x, g_pre, dxo, name="ffn_dx")
    d_up_t = _weight_grad(dz_u, h, name="ffn_dw_up", out_shape=(1, 2 * f, d))
    d_up_t = _weight_grad(dz_g, h, name="ffn_dw_up", out=d_up_t, row0=f)
    grads = dict(w_dw=jnp.concatenate([dwd_u, dwd_g], axis=1), b_dw=jnp.concatenate([db_u, db_g], axis=1),
                 g_pre=dg_pre, g_post=dg_post)
    return dx, grads, d_up_t, d_down


def _local_step(x, pos_col, target, p, tie=None, late_weights=None, exchange=None):
    ng = p["norm_g"]
    row = lambda r: ng[r:r + 1]
    freq = _rope_freq_row()
    rope = _rope_tables(pos_col, freq if tie is None else freq + tie[0:1])
    d = x.shape[1]

    h0, *qkv = _qkv_proj(x, row(0), p["w_qkv_t"], rope)
    os_, ls_ = zip(*[_attn_fwd(qkv[g_], g_, d_) for g_, d_ in enumerate(DILATIONS)])
    y_a, x1, mixed, lse = _mix_wo(os_, ls_, p["w_o_t"], x, row(1))
    if late_weights is not None:
        p = {**p, **late_weights(x1)}
    x2, ffn0 = _ffn_fwd(x1, row(2), row(3), p["w_up_t"][0], p["ffn_w_dw"][0], p["ffn_b_dw"][0], p["w_down"][0])
    h1, ag = _norm_matmul(x2, row(4), p["w_pw1_t"], tn=_tile(p["w_pw1_t"].shape[0]), name="conv_pw1", bias=p["b_pw1"])
    u3, u1 = _conv_mid(ag, p["conv_w_dw"], p["conv_b_dw"], p["ln_g"], p["ln_b"])
    y_c, x3 = _matmul_resnorm(u3, p["w_pw2"], x2, row(5), name="conv_pw2", bias=p["b_pw2"])
    x4, ffn1 = _ffn_fwd(x3, row(6), row(7), p["w_up_t"][1], p["ffn_w_dw"][1], p["ffn_b_dw"][1], p["w_down"][1])
    dx4, loss = _loss_grad(x4, target)

    big = [BF16, BF16]

    def tied(r, *tokens):
        tokens = [t for t in tokens if t is not None]
        return row(r) if not tokens else row(r) + jnp.tile(sum(tokens)[0:1], (1, d // 128))

    dx3, gf1, d_up1, d_down1 = _ffn_bwd(ffn1, dx4, row(6), row(7), p["w_up_t"][1], p["ffn_w_dw"][1], p["ffn_b_dw"][1],
                                        p["w_down"][1])
    t0 = exchange.submit("ffn1", [d_up1, d_down1], big) if exchange else None
    dy_c, dg5, db_pw2 = _postnorm_bwd(y_c, tied(5, t0), dx3, name="conv_post_bwd", with_bias_grad=True)
    du3 = _matmul(dy_c, p["w_pw2"], name="conv_du3", out_dtype=F32, transposed_w=True)
    d_wpw2 = _weight_grad(u3, dy_c, name="conv_dw_pw2", out_shape=(1, u3.shape[1], d))
    du1, d_lng, d_lnb, d_cbdw, d_cwdw = _conv_mid_bwd(ag, u1, du3, p["conv_b_dw"], p["ln_g"], p["ln_b"])
    dag, db_pw1 = _glu_conv_bwd(du1, ag, p["conv_w_dw"])
    dx2, dg4 = _matmul_prenorm_bwd([(dag, 0, dag.shape[1], 0)], p["w_pw1_t"], x2, row(4), dx3, name="conv_dx")
    d_wpw1_t = _weight_grad(dag, h1, name="conv_dw_pw1", out_shape=(1, dag.shape[1], d))
    t0 = exchange.advance(dx2) if exchange else None
    t1 = exchange.submit("conv", [d_wpw1_t, d_wpw2], big) if exchange else None
    dx1, gf0, d_up0, d_down0 = _ffn_bwd(ffn0, dx2, row(2), tied(3, t0, t1), p["w_up_t"][0], p["ffn_w_dw"][0], p["ffn_b_dw"][0],
                                        p["w_down"][0])
    t0 = exchange.advance(dx1) if exchange else None
    t1 = exchange.submit("ffn0", [d_up0, d_down0], big) if exchange else None
    dy_a, dg1 = _postnorm_bwd(y_a, tied(1, t0, t1), dx1, name="attn_post_bwd")
    dmix = _matmul(dy_a, p["w_o_t"], name="attn_dmix", out_dtype=BF16, transposed_w=False)
    d_wo_t = _weight_grad(dy_a, mixed, name="attn_dw_o", out_shape=(1, d, GROUP_WIDTH))
    pieces, d_wqkv_t = [], None
    for g_, d_ in enumerate(DILATIONS):
        dq = _attn_bwd_dq(qkv[g_], dmix, mixed, lse, rope, g_, d_)
        dkv = _attn_bwd_dkv(qkv[g_], dmix, mixed, lse, rope, g_, d_)
        for t, (arr, c0) in enumerate(((dq, 0), (dkv, 0), (dkv, GROUP_WIDTH))):
            r0 = (3 * t + g_) * GROUP_WIDTH
            pieces.append((arr, c0, GROUP_WIDTH, r0))
            d_wqkv_t = _weight_grad(arr, h0, name="attn_dw_qkv", a_col0=c0, ka=GROUP_WIDTH, out=d_wqkv_t,
                                    out_shape=(1, p["w_qkv_t"].shape[0], d), row0=r0)
    grad_x, dg0 = _matmul_prenorm_bwd(pieces, p["w_qkv_t"], x, row(0), dx1, name="attn_dx")

    grads = dict(
        norm_g=jnp.concatenate([dg0, dg1, gf0["g_pre"], gf0["g_post"], dg4, dg5, gf1["g_pre"], gf1["g_post"]], axis=0),
        w_qkv_t=d_wqkv_t, w_o_t=d_wo_t, w_pw1_t=d_wpw1_t, b_pw1=db_pw1,
        conv_w_dw=d_cwdw[:CONV_KERNEL], conv_b_dw=d_cbdw, ln_g=d_lng, ln_b=d_lnb, w_pw2=d_wpw2, b_pw2=db_pw2,
        w_up_t=[d_up0, d_up1], ffn_w_dw=jnp.stack([gf0["w_dw"], gf1["w_dw"]]),
        ffn_b_dw=jnp.concatenate([gf0["b_dw"], gf1["b_dw"]], axis=0), w_down=[d_down0, d_down1])
    return loss, grad_x, grads


SMALL_AXIS = dict(norm_g=2, conv_b_pw1=1, conv_w_dw=2, conv_b_dw=1, conv_ln_g=1, conv_ln_b=1, conv_b_pw2=1, ffn_w_dw=2)
SMALL = tuple(SMALL_AXIS)
MATMUL_WEIGHTS = dict(attn_w_qkv=True, conv_w_pw1=True, ffn_w_up=True, conv_w_pw2=False, ffn_w_down=False)


def _pack(arrays, cols, row_multiple):
    flat = jnp.concatenate([a.reshape(-1) for a in arrays])
    rows = -(-flat.shape[0] // cols)
    rows = -(-rows // row_multiple) * row_multiple
    return jnp.pad(flat, (0, rows * cols - flat.shape[0])).reshape(rows, cols)


def _unpack(packed, shapes):
    flat = packed.reshape(packed.shape[:-2] + (-1,))
    out, off = [], 0
    for shp in shapes:
        n = math.prod(shp)
        out.append(flat[..., off:off + n].reshape(packed.shape[:-2] + tuple(shp)))
        off += n
    return out


def _join_shards(stacked, axis):
    moved = jnp.moveaxis(stacked, 0, axis)
    shp = moved.shape
    return moved.reshape(shp[:axis] + (shp[axis] * shp[axis + 1],) + shp[axis + 2:])


def _split_shards(whole, axis):
    shp = whole.shape
    cut = whole.reshape(shp[:axis] + (N_DEV, shp[axis] // N_DEV) + shp[axis + 1:])
    return jnp.moveaxis(cut, axis, 0)


def _row_shard(w, transposed):
    t = jnp.swapaxes(w, 1, 2) if transposed else w
    return t.astype(BF16).reshape(-1, t.shape[-1])


def kernel(x, positions, norm_g, attn_w_qkv, attn_w_o, conv_w_pw1, conv_b_pw1, conv_w_dw, conv_b_dw, conv_ln_g, conv_ln_b, conv_w_pw2, conv_b_pw2, ffn_w_up, ffn_w_dw, ffn_b_dw, ffn_w_down, loss_target, m_norm_g, m_attn_w_qkv, m_attn_w_o, m_conv_w_pw1, m_conv_b_pw1, m_conv_w_dw, m_conv_b_dw, m_conv_ln_g, m_conv_ln_b, m_conv_w_pw2, m_conv_b_pw2, m_ffn_w_up, m_ffn_w_dw, m_ffn_b_dw, m_ffn_w_down, v_norm_g, v_attn_w_qkv, v_attn_w_o, v_conv_w_pw1, v_conv_b_pw1, v_conv_w_dw, v_conv_b_dw, v_conv_ln_g, v_conv_ln_b, v_conv_w_pw2, v_conv_b_pw2, v_ffn_w_up, v_ffn_w_dw, v_ffn_b_dw, v_ffn_w_down):
    w = dict(norm_g=norm_g, attn_w_qkv=attn_w_qkv, attn_w_o=attn_w_o, conv_w_pw1=conv_w_pw1, conv_b_pw1=conv_b_pw1,
             conv_w_dw=conv_w_dw, conv_b_dw=conv_b_dw, conv_ln_g=conv_ln_g, conv_ln_b=conv_ln_b, conv_w_pw2=conv_w_pw2,
             conv_b_pw2=conv_b_pw2, ffn_w_up=ffn_w_up, ffn_w_dw=ffn_w_dw, ffn_w_down=ffn_w_down)
    m = dict(norm_g=m_norm_g, attn_w_qkv=m_attn_w_qkv, attn_w_o=m_attn_w_o, conv_w_pw1=m_conv_w_pw1, conv_b_pw1=m_conv_b_pw1,
             conv_w_dw=m_conv_w_dw, conv_b_dw=m_conv_b_dw, conv_ln_g=m_conv_ln_g, conv_ln_b=m_conv_ln_b, conv_w_pw2=m_conv_w_pw2,
             conv_b_pw2=m_conv_b_pw2, ffn_w_up=m_ffn_w_up, ffn_w_dw=m_ffn_w_dw, ffn_w_down=m_ffn_w_down)
    v = dict(norm_g=v_norm_g, attn_w_qkv=v_attn_w_qkv, attn_w_o=v_attn_w_o, conv_w_pw1=v_conv_w_pw1, conv_b_pw1=v_conv_b_pw1,
             conv_w_dw=v_conv_w_dw, conv_b_dw=v_conv_b_dw, conv_ln_g=v_conv_ln_g, conv_ln_b=v_conv_ln_b, conv_w_pw2=v_conv_w_pw2,
             conv_b_pw2=v_conv_b_pw2, ffn_w_up=v_ffn_w_up, ffn_w_dw=v_ffn_w_dw, ffn_w_down=v_ffn_w_down)
    d = x.shape[-1]

    w_qkv_t = _all_gather(_row_shard(attn_w_qkv, True), "gather_w_qkv").reshape(-1, d)
    w_o_t = _all_gather(_row_shard(attn_w_o, True), "gather_w_o").reshape(d, -1)
    small = _all_gather(_pack([w[n] for n in SMALL], 128, 8), "gather_small_weights")
    sm = {n: _join_shards(stacked, SMALL_AXIS[n])
          for n, stacked in zip(SMALL, _unpack(small, [w[n].shape for n in SMALL]))}
    late = {n: t for n, t in MATMUL_WEIGHTS.items() if n != "attn_w_qkv"}
    shares = [_row_shard(w[n], t) for n, t in late.items()]
    rows = [s_.shape[0] for s_ in shares]
    late_share = jnp.concatenate(shares, axis=0)
    send_sems, recv_sems, share_thru, land_thru, tie = _gather_start(late_share)
    me = 4 * lax.axis_index("x") + 2 * lax.axis_index("y") + lax.axis_index("c")

    def late_weights(after):
        big = _gather_wait(send_sems, recv_sems, share_thru, land_thru, after)
        big = lax.dynamic_update_slice(big, late_share[None], (me, 0, 0))
        whole, r0 = {}, 0
        for n, nr in zip(late, rows):
            layers = w[n].shape[0]
            seg = big[:, r0:r0 + nr].reshape(N_DEV, layers, nr // layers, d)
            whole[n] = [seg[:, l_].reshape(-1, d) for l_ in range(layers)]
            r0 += nr
        return dict(w_pw1_t=whole["conv_w_pw1"][0], w_pw2=whole["conv_w_pw2"][0], w_up_t=whole["ffn_w_up"],
                    w_down=whole["ffn_w_down"])

    p = dict(norm_g=sm["norm_g"].reshape(-1, d), w_qkv_t=w_qkv_t, w_o_t=w_o_t, b_pw1=sm["conv_b_pw1"],
             conv_w_dw=sm["conv_w_dw"][0], conv_b_dw=sm["conv_b_dw"], ln_g=sm["conv_ln_g"], ln_b=sm["conv_ln_b"],
             b_pw2=sm["conv_b_pw2"], ffn_w_dw=sm["ffn_w_dw"], ffn_b_dw=[ffn_b_dw[0:1], ffn_b_dw[1:2]])

    exchange = _GradExchange()
    loss, grad_x, g = _local_step(x[0], positions.reshape(-1, 1), loss_target[0], p, tie, late_weights, exchange)
    loss = lax.psum(loss[0, 0], ("x", "y", "c"))
    gsmall = dict(norm_g=g["norm_g"].reshape(norm_g.shape[0], 4, -1), conv_b_pw1=g["b_pw1"], conv_w_dw=g["conv_w_dw"][None],
                  conv_b_dw=g["conv_b_dw"], conv_ln_g=g["ln_g"], conv_ln_b=g["ln_b"], conv_b_pw2=g["b_pw2"], ffn_w_dw=g["ffn_w_dw"])
    small_contrib = jnp.concatenate([_split_shards(gsmall[n], SMALL_AXIS[n]).reshape(N_DEV, -1) for n in SMALL], axis=1)
    srows = small.shape[1]
    small_contrib = jnp.pad(small_contrib, ((0, 0), (0, srows * 128 - small_contrib.shape[1]))).reshape(1, N_DEV * srows, 128)
    exchange.advance(grad_x)
    exchange.submit("attn", [g["w_qkv_t"], g["w_o_t"], small_contrib], [BF16, BF16, F32])
    exchange.advance(grad_x)
    exchange.advance(grad_x)
    (s_up1, s_down1), (s_pw1, s_pw2), (s_up0, s_down0), (s_qkv, s_wo, s_small) = exchange.results()
    reduced = dict(attn_w_qkv=s_qkv, attn_w_o=s_wo, conv_w_pw1=s_pw1, conv_w_pw2=s_pw2,
                   ffn_w_up=jnp.concatenate([s_up0, s_up1], axis=0), ffn_w_down=jnp.concatenate([s_down0, s_down1], axis=0))

    outs = {}
    for n, transposed in (("attn_w_o", True), *MATMUL_WEIGHTS.items()):
        gsum = jnp.swapaxes(reduced[n], 1, 2) if transposed else reduced[n]
        outs[n] = (gsum, *_adamw(gsum, w[n], m[n], v[n], "adamw"))
    sshapes = [w[n].shape for n in SMALL]
    souts = _sum_adamw(s_small, *[_pack([t[n] for n in SMALL], 128, 8) for t in (w, m, v)], name="sum_adamw_small")
    for n, vals in zip(SMALL, zip(*[_unpack(o, sshapes) for o in souts])):
        outs[n] = vals
    bparts = _all_gather(_pack([g["ffn_b_dw"]], 128, 8), "gather_bias_grads")
    bouts = _sum_adamw(bparts, *[_pack([t], 128, 8) for t in (ffn_b_dw, m_ffn_b_dw, v_ffn_b_dw)], name="sum_adamw_bias")
    outs["ffn_b_dw"] = tuple(_unpack(o, [ffn_b_dw.shape])[0] for o in bouts)

    order = ("norm_g", "attn_w_qkv", "attn_w_o", "conv_w_pw1", "conv_b_pw1", "conv_w_dw", "conv_b_dw", "conv_ln_g",
             "conv_ln_b", "conv_w_pw2", "conv_b_pw2", "ffn_w_up", "ffn_w_dw", "ffn_b_dw", "ffn_w_down")
    return (loss, grad_x[None], *[outs[n][0] for n in order], *[outs[n][1] for n in order],
            *[outs[n][2] for n in order], *[outs[n][3] for n in order])
```

```python
import functools
import math

import numpy as np
import jax
import jax.numpy as jnp
from jax import lax
from jax.experimental import pallas as pl
from jax.experimental.pallas import tpu as pltpu

F32 = jnp.float32
BF16 = jnp.bfloat16
EPS = 1e-6
N_DEV = 8
HEAD_DIM = 64
GROUP_WIDTH = 512
DILATIONS = (1, 4, 16)
SPAN = 128
ROT_DIM = 16
ROPE_THETA = 500000.0
CONV_KERNEL = 31
CONV_HALO = 32
FFN_CONV = 3
ADAM_LR, ADAM_B1, ADAM_B2, ADAM_EPS, ADAM_WD, ADAM_STEP = 0.001, 0.9, 0.999, 1e-08, 0.01, 10
VMEM_LIMIT_BYTES = 56 * 1024 * 1024
MESH = pl.DeviceIdType.MESH
ANY = pl.BlockSpec(memory_space=pl.ANY)
NT = (((1,), (1,)), ((), ()))
TN = (((0,), (0,)), ((), ()))


def _params(*sem):
    return pltpu.CompilerParams(dimension_semantics=sem, vmem_limit_bytes=VMEM_LIMIT_BYTES)


def _sigmoid(v):
    return 1.0 / (1.0 + jnp.exp(-v))


def _full(shape):
    return pl.BlockSpec(shape, lambda *_: (0,) * len(shape))


def _rows(tm, width):
    return pl.BlockSpec((tm, width), lambda i, *_: (i, 0))


def _tile(n, *multiples_of):
    for t in (1408, 1024, 512, 384, 256, 128):
        if n % t == 0 and all(o % t == 0 for o in multiples_of):
            return t
    raise ValueError((n, multiples_of))


def _all_gather(shard, name):
    r, c_ = shard.shape

    def body(x_ref, out_ref, send_sems, recv_sems, local_sem):
        x, y, c = lax.axis_index("x"), lax.axis_index("y"), lax.axis_index("c")
        me, sibling = (x, y, c), (x, y, 1 - c)
        chips = [(1 - x, y), (x, 1 - y), (1 - x, 1 - y)]

        def rows(px, py, pc):
            return out_ref.at[4 * px + 2 * py + pc]

        def copy(k, block, to, src=None):
            return pltpu.make_async_remote_copy(
                src_ref=rows(*block) if src is None else src, dst_ref=rows(*block),
                send_sem=send_sems.at[k], recv_sem=recv_sems.at[k], device_id=to, device_id_type=MESH)

        mine = pltpu.make_async_copy(x_ref, rows(*me), local_sem)
        mine.start()
        first = [copy(0, me, sibling, src=x_ref)]
        first += [copy(1 + j, me, (*chip, c), src=x_ref) for j, chip in enumerate(chips)]
        for cp in first:
            cp.start()
        passed = [copy(4 + j, (*chip, c), sibling) for j, chip in enumerate(chips)]
        for j, chip in enumerate(chips):
            copy(1 + j, (*chip, c), me).wait_recv()
            passed[j].start()
        copy(0, sibling, me).wait_recv()
        for j, chip in enumerate(chips):
            copy(4 + j, (*chip, 1 - c), me).wait_recv()
        for cp in first + passed:
            cp.wait_send()
        mine.wait()

    return pl.pallas_call(
        body, name=name, out_shape=jax.ShapeDtypeStruct((N_DEV, r, c_), shard.dtype),
        in_specs=[ANY], out_specs=ANY,
        scratch_shapes=[pltpu.SemaphoreType.DMA((7,)), pltpu.SemaphoreType.DMA((7,)), pltpu.SemaphoreType.DMA],
    )(shard)


HBM = pl.BlockSpec(memory_space=pltpu.HBM)
SEM = pl.BlockSpec(memory_space=pltpu.SEMAPHORE)
SIDE_EFFECT = pltpu.CompilerParams(has_side_effects=pltpu.SideEffectType.DATAFLOW_SIDE_EFFECTING)


def _gather_start(shard):
    r, c_ = shard.shape

    def body(x_ref, land_ref, send_sems, recv_sems, x_thru, land_thru, token):
        x, y, c = lax.axis_index("x"), lax.axis_index("y"), lax.axis_index("c")
        me = 4 * x + 2 * y + c
        for k in range(1, N_DEV):
            peer = (1 - x if k & 4 else x, 1 - y if k & 2 else y, 1 - c if k & 1 else c)
            pltpu.make_async_remote_copy(src_ref=x_ref, dst_ref=land_ref.at[me], send_sem=send_sems.at[k - 1],
                                         recv_sem=recv_sems.at[k - 1], device_id=peer, device_id_type=MESH).start()
        token[...] = jnp.zeros_like(token)

    land = pltpu.with_memory_space_constraint(lax.empty((N_DEV, r, c_), shard.dtype), pltpu.HBM)
    return pl.pallas_call(
        body, name="gather_late_weights_start",
        out_shape=(pltpu.SemaphoreType.DMA((N_DEV - 1,)), pltpu.SemaphoreType.DMA((N_DEV - 1,)),
                   pltpu.HBM(shard.shape, shard.dtype), pltpu.HBM((N_DEV, r, c_), shard.dtype),
                   jax.ShapeDtypeStruct((8, 128), F32)),
        in_specs=(HBM, HBM), out_specs=(SEM, SEM, HBM, HBM, pl.BlockSpec(memory_space=pltpu.VMEM)),
        input_output_aliases={0: 2, 1: 3}, compiler_params=SIDE_EFFECT,
    )(pltpu.with_memory_space_constraint(shard, pltpu.HBM), land)


def _gather_wait(send_sems, recv_sems, shard_thru, land_thru, after):
    def body(x_ref, land_ref, send_sems, recv_sems, after_ref, x_dead, got_ref):
        x, y, c = lax.axis_index("x"), lax.axis_index("y"), lax.axis_index("c")
        for k in range(N_DEV - 1):
            copy = pltpu.make_async_remote_copy(src_ref=x_ref, dst_ref=land_ref.at[0], send_sem=send_sems.at[k],
                                                recv_sem=recv_sems.at[k], device_id=(x, y, c), device_id_type=MESH)
            copy.wait_send()
            copy.wait_recv()

    return pl.pallas_call(
        body, name="gather_late_weights_wait",
        out_shape=(pltpu.HBM(shard_thru.shape, shard_thru.dtype), pltpu.HBM(land_thru.shape, land_thru.dtype)),
        in_specs=(HBM, HBM, SEM, SEM, ANY), out_specs=(HBM, HBM), input_output_aliases={0: 0, 1: 1},
        compiler_params=SIDE_EFFECT,
    )(shard_thru, land_thru, send_sems, recv_sems, after)[1]


def _hbm(a):
    return pltpu.with_memory_space_constraint(a, pltpu.HBM)


def _exchange_start(name, arrays, lands, plan, ncopies):
    n = len(arrays)

    def body(*refs):
        send_sems, recv_sems, token = refs[2 * n], refs[2 * n + 1], refs[-1]
        x, y, c = lax.axis_index("x"), lax.axis_index("y"), lax.axis_index("c")
        for k, (src, dst, peer) in enumerate(plan(x, y, c, refs[:n], refs[n:2 * n])):
            pltpu.make_async_remote_copy(src_ref=src, dst_ref=dst, send_sem=send_sems.at[k], recv_sem=recv_sems.at[k],
                                         device_id=peer, device_id_type=MESH).start()
        token[...] = jnp.zeros_like(token)

    both = list(arrays) + list(lands)
    outs = pl.pallas_call(
        body, name=name,
        out_shape=(pltpu.SemaphoreType.DMA((ncopies,)), pltpu.SemaphoreType.DMA((ncopies,)),
                   *[pltpu.HBM(a.shape, a.dtype) for a in both], jax.ShapeDtypeStruct((8, 128), F32)),
        in_specs=(HBM,) * (2 * n), out_specs=(SEM, SEM) + (HBM,) * (2 * n) + (pl.BlockSpec(memory_space=pltpu.VMEM),),
        input_output_aliases={i: 2 + i for i in range(2 * n)}, compiler_params=SIDE_EFFECT,
    )(*[_hbm(a) for a in both])
    return outs[0], outs[1], list(outs[2:2 + n]), list(outs[2 + n:2 + 2 * n]), outs[-1]


def _exchange_wait(name, send_sems, recv_sems, arrays, lands, plan, after):
    n = len(arrays)

    def body(*refs):
        send_sems, recv_sems = refs[2 * n], refs[2 * n + 1]
        x, y, c = lax.axis_index("x"), lax.axis_index("y"), lax.axis_index("c")
        for k, (src, dst, peer) in enumerate(plan(x, y, c, refs[:n], refs[n:2 * n])):
            copy = pltpu.make_async_remote_copy(src_ref=src, dst_ref=dst, send_sem=send_sems.at[k], recv_sem=recv_sems.at[k],
                                                device_id=peer, device_id_type=MESH)
            copy.wait_send()
            copy.wait_recv()

    both = list(arrays) + list(lands)
    outs = pl.pallas_call(
        body, name=name, out_shape=tuple(pltpu.HBM(a.shape, a.dtype) for a in both),
        in_specs=(HBM,) * (2 * n) + (SEM, SEM, ANY), out_specs=(HBM,) * (2 * n),
        input_output_aliases={i: i for i in range(2 * n)}, compiler_params=SIDE_EFFECT,
    )(*both, send_sems, recv_sems, after)
    return list(outs[:n]), list(outs[n:])


def _sibling_plan(x, y, c, g_refs, land_refs):
    return [(g.at[:, 2 * q + (1 - c)], o.at[:, q], (x, y, 1 - c)) for g, o in zip(g_refs, land_refs) for q in range(4)]


def _chips_plan(x, y, c, p_refs, land_refs):
    chips = [(1 - x, y), (x, 1 - y), (1 - x, 1 - y)]
    return [(p_.at[:, 2 * qx + qy], o.at[:, 2 * x + y], (qx, qy, c)) for p_, o in zip(p_refs, land_refs) for qx, qy in chips]


class _GradExchange:
    def __init__(self):
        self.core = lax.axis_index("c").astype(jnp.int32).reshape(1)
        self.chip = 2 * lax.axis_index("x") + lax.axis_index("y")
        self.groups = []

    def submit(self, tag, arrays, dtypes):
        arrays = [a.reshape(a.shape[0], N_DEV, a.shape[1] // N_DEV, a.shape[2]) for a in arrays]
        lands = [lax.empty((a.shape[0], 4) + a.shape[2:], a.dtype) for a in arrays]
        send, recv, arrays, lands, token = _exchange_start(f"rs_pair_start_{tag}", arrays, lands, _sibling_plan, 4 * len(arrays))
        self.groups.append(dict(tag=tag, stage=1, sems=(send, recv), arrays=arrays, lands=lands, dtypes=dtypes))
        return token

    def advance(self, after):
        token = None
        for g in self.groups:
            if g["stage"] == 1:
                arrays, got = _exchange_wait(f"rs_pair_wait_{g['tag']}", *g["sems"], g["arrays"], g["lands"], _sibling_plan, after)
                parts = [_rs_pair_add(a, b, self.core, dt) for a, b, dt in zip(arrays, got, g["dtypes"])]
                lands = [lax.empty(p_.shape, p_.dtype) for p_ in parts]
                send, recv, parts, lands, tok = _exchange_start(f"rs_chip_start_{g['tag']}", parts, lands, _chips_plan, 3 * len(parts))
                g.update(stage=2, sems=(send, recv), arrays=parts, lands=lands)
                token = tok if token is None else token + tok
            elif g["stage"] == 2:
                parts, lands = _exchange_wait(f"rs_chip_wait_{g['tag']}", *g["sems"], g["arrays"], g["lands"], _chips_plan, after)
                sums = []
                for p_, land in zip(parts, lands):
                    l, _, r, c_ = p_.shape
                    own = lax.dynamic_slice(p_, (0, self.chip, 0, 0), (l, 1, r, c_))
                    sums.append(_sum_parts(lax.dynamic_update_slice(land, own, (0, self.chip, 0, 0)), "sum_chips"))
                g.update(stage=3, sums=sums)
        return token

    def results(self):
        return [g.get("sums") for g in self.groups]


def _with_rows(g, n):
    return jax.ShapeDtypeStruct((g.shape[0], n) + tuple(g.shape[2:]), g.dtype)


def _rs_sibling(gs):
    n = len(gs)

    def body(*refs):
        g_refs, o_refs, (send_sems, recv_sems) = refs[:n], refs[n:2 * n], refs[2 * n:]
        x, y, c = lax.axis_index("x"), lax.axis_index("y"), lax.axis_index("c")
        copies = [pltpu.make_async_remote_copy(
            src_ref=g_refs[w].at[:, 2 * q + (1 - c)], dst_ref=o_refs[w].at[:, q], send_sem=send_sems.at[4 * w + q],
            recv_sem=recv_sems.at[4 * w + q], device_id=(x, y, 1 - c), device_id_type=MESH)
            for w in range(n) for q in range(4)]
        for cp in copies:
            cp.start()
        for cp in copies:
            cp.wait_recv()
        for cp in copies:
            cp.wait_send()

    return pl.pallas_call(
        body, name="rs_sibling", out_shape=[_with_rows(g, 4) for g in gs],
        in_specs=[ANY] * n, out_specs=[ANY] * n,
        scratch_shapes=[pltpu.SemaphoreType.DMA((4 * n,)), pltpu.SemaphoreType.DMA((4 * n,))],
    )(*gs)


def _rs_pair_add(g, got, core, out_dtype):
    l, _, r, c_ = g.shape

    def body(core_ref, g_ref, got_ref, o_ref):
        o_ref[...] = (g_ref[...].astype(F32) + got_ref[...].astype(F32)).astype(out_dtype)

    blk = (None, None, r, c_)
    return pl.pallas_call(
        body, name="rs_pair_add", out_shape=jax.ShapeDtypeStruct((l, 4, r, c_), out_dtype),
        grid_spec=pltpu.PrefetchScalarGridSpec(
            num_scalar_prefetch=1, grid=(l, 4),
            in_specs=[pl.BlockSpec(blk, lambda i, q, core_ref: (i, 2 * q + core_ref[0], 0, 0)),
                      pl.BlockSpec(blk, lambda i, q, core_ref: (i, q, 0, 0))],
            out_specs=pl.BlockSpec(blk, lambda i, q, core_ref: (i, q, 0, 0))),
        compiler_params=_params("parallel", "parallel"),
    )(core, g, got)


def _rs_chips(parts):
    n = len(parts)

    def body(*refs):
        p_refs, o_refs, (send_sems, recv_sems, local_sems) = refs[:n], refs[n:2 * n], refs[2 * n:]
        x, y, c = lax.axis_index("x"), lax.axis_index("y"), lax.axis_index("c")
        my_chip = 2 * x + y
        chips = [(1 - x, y), (x, 1 - y), (1 - x, 1 - y)]
        local = [pltpu.make_async_copy(p_refs[w].at[:, my_chip], o_refs[w].at[:, my_chip], local_sems.at[w]) for w in range(n)]
        for cp in local:
            cp.start()
        copies = [pltpu.make_async_remote_copy(
            src_ref=p_refs[w].at[:, 2 * qx + qy], dst_ref=o_refs[w].at[:, my_chip], send_sem=send_sems.at[3 * w + k],
            recv_sem=recv_sems.at[3 * w + k], device_id=(qx, qy, c), device_id_type=MESH)
            for w in range(n) for k, (qx, qy) in enumerate(chips)]
        for cp in copies:
            cp.start()
        for cp in copies:
            cp.wait_recv()
        for cp in copies:
            cp.wait_send()
        for cp in local:
            cp.wait()

    return pl.pallas_call(
        body, name="rs_chips", out_shape=[jax.ShapeDtypeStruct(p.shape, p.dtype) for p in parts],
        in_specs=[ANY] * n, out_specs=[ANY] * n,
        scratch_shapes=[pltpu.SemaphoreType.DMA((3 * n,)), pltpu.SemaphoreType.DMA((3 * n,)), pltpu.SemaphoreType.DMA((n,))],
    )(*parts)


def _sum_parts(parts, name):
    l, n, r, c_ = parts.shape

    def body(p_ref, o_ref):
        g = p_ref[0].astype(F32)
        for s in range(1, n):
            g = g + p_ref[s].astype(F32)
        o_ref[...] = g

    return pl.pallas_call(
        body, name=name, out_shape=jax.ShapeDtypeStruct((l, r, c_), F32), grid=(l,),
        in_specs=[pl.BlockSpec((None, n, r, c_), lambda i: (i, 0, 0, 0))],
        out_specs=pl.BlockSpec((None, r, c_), lambda i: (i, 0, 0)), compiler_params=_params("parallel"),
    )(parts)


def _adamw_math(w, g, m, v):
    m = ADAM_B1 * m + (1.0 - ADAM_B1) * g
    v = ADAM_B2 * v + (1.0 - ADAM_B2) * (g * g)
    m_hat = m / (1.0 - ADAM_B1 ** ADAM_STEP)
    v_hat = v / (1.0 - ADAM_B2 ** ADAM_STEP)
    delta = -ADAM_LR * (m_hat / (jnp.sqrt(v_hat) + ADAM_EPS) + ADAM_WD * w)
    return delta, m, v


def _adamw(g, w, m, v, name):
    l, k, n = w.shape
    tk = 256 if k % 256 == 0 else k

    def body(g_ref, w_ref, m_ref, v_ref, d_ref, nm_ref, nv_ref):
        d_ref[...], nm_ref[...], nv_ref[...] = _adamw_math(w_ref[...], g_ref[...], m_ref[...], v_ref[...])

    spec = pl.BlockSpec((None, tk, n), lambda i, j: (i, j, 0))
    return pl.pallas_call(
        body, name=name, out_shape=[jax.ShapeDtypeStruct((l, k, n), F32)] * 3, grid=(l, k // tk),
        in_specs=[spec] * 4, out_specs=[spec] * 3, compiler_params=_params("parallel", "parallel"),
    )(g, w, m, v)


def _sum_adamw(parts, w, m, v, name):
    n, r, c_ = parts.shape

    def body(p_ref, w_ref, m_ref, v_ref, g_ref, d_ref, nm_ref, nv_ref):
        g = p_ref[0]
        for s in range(1, n):
            g = g + p_ref[s]
        g_ref[...] = g
        d_ref[...], nm_ref[...], nv_ref[...] = _adamw_math(w_ref[...], g, m_ref[...], v_ref[...])

    return pl.pallas_call(
        body, name=name, out_shape=[jax.ShapeDtypeStruct((r, c_), F32)] * 4, grid=(1,),
        in_specs=[_full((n, r, c_))] + [_full((r, c_))] * 3, out_specs=[_full((r, c_))] * 4,
        compiler_params=_params("arbitrary"),
    )(parts, w, m, v)


def _rope_tables(pos_col, freq_row):
    s = pos_col.shape[0]
    tm = min(1024, s)

    def body(p_ref, f_ref, c_ref, su_ref, sd_ref):
        ang = p_ref[...].astype(F32) * f_ref[...]
        lane = lax.broadcasted_iota(jnp.int32, ang.shape, 1) & (HEAD_DIM - 1)
        cs, sn = jnp.cos(ang), jnp.sin(ang)
        c_ref[...] = jnp.where(lane < ROT_DIM, cs, 1.0)
        su_ref[...] = jnp.where((lane >= ROT_DIM // 2) & (lane < ROT_DIM), sn, 0.0)
        sd_ref[...] = jnp.where(lane < ROT_DIM // 2, -sn, 0.0)

    return pl.pallas_call(
        body, name="rope_tables", out_shape=[jax.ShapeDtypeStruct((s, 128), F32)] * 3, grid=(s // tm,),
        in_specs=[pl.BlockSpec((tm, 1), lambda i: (i, 0)), _full((1, 128))],
        out_specs=[_rows(tm, 128)] * 3, compiler_params=_params("parallel"),
    )(pos_col, freq_row)


def _rope_apply(t, cos, sin_up, sin_dn):
    w = t.shape[1]
    return t * cos + pltpu.roll(t, 8, 1) * sin_up + pltpu.roll(t, w - 8, 1) * sin_dn


def _rope_transpose(dr, cos, sin_up, sin_dn):
    w = dr.shape[1]
    return dr * cos + pltpu.roll(dr * sin_up, w - 8, 1) + pltpu.roll(dr * sin_dn, 8, 1)


def _norm_matmul(x, g, wt, *, tn, name, bias=None, rope=None, rope_blocks=0, tm=512):
    s, d = x.shape
    n = wt.shape[0]
    tm = min(tm, s)

    def body(*refs):
        x_ref, g_ref, w_ref = refs[:3]
        k = 3
        b_ref = None
        if bias is not None:
            b_ref = refs[k]
            k += 1
        if rope is not None:
            c_ref, su_ref, sd_ref = refs[k:k + 3]
            k += 3
        h_ref, o_ref = refs[k:k + 2]
        j = pl.program_id(1)

        @pl.when(j == 0)
        def _():
            xv = x_ref[...]
            r = lax.rsqrt(jnp.mean(xv * xv, axis=-1, keepdims=True) + EPS)
            h_ref[...] = (xv * r * g_ref[...]).astype(BF16)

        acc = lax.dot_general(h_ref[...], w_ref[...], NT, preferred_element_type=F32)
        if b_ref is not None:
            acc = acc + b_ref[...]
        if rope is None:
            o_ref[...] = acc.astype(BF16)
        else:
            @pl.when(j < rope_blocks)
            def _():
                reps = tn // 128
                o_ref[...] = _rope_apply(acc, jnp.tile(c_ref[...], (1, reps)), jnp.tile(su_ref[...], (1, reps)),
                                         jnp.tile(sd_ref[...], (1, reps))).astype(BF16)

            @pl.when(j >= rope_blocks)
            def _():
                o_ref[...] = acc.astype(BF16)

    in_specs = [_rows(tm, d), _full((1, d)), pl.BlockSpec((tn, d), lambda i, j: (j, 0))]
    args = [x, g, wt]
    if bias is not None:
        in_specs.append(pl.BlockSpec((1, tn), lambda i, j: (0, j)))
        args.append(bias)
    if rope is not None:
        in_specs += [_rows(tm, 128)] * 3
        args += list(rope)
    return pl.pallas_call(
        body, name=name,
        out_shape=[jax.ShapeDtypeStruct((s, d), BF16), jax.ShapeDtypeStruct((s, n), BF16)],
        grid=(s // tm, n // tn), in_specs=in_specs,
        out_specs=[_rows(tm, d), pl.BlockSpec((tm, tn), lambda i, j: (i, j))],
        compiler_params=_params("parallel", "arbitrary"),
    )(*args)


def _class_major(tm, dil):
    p = np.zeros((tm, tm), np.float32)
    per = tm // dil
    for r in range(dil):
        for j in range(per):
            p[r * per + j, j * dil + r] = 1.0
    return jnp.asarray(p, dtype=BF16)


def _qkv_proj(x, g, wt, rope, tm=512):
    s, d = x.shape
    n = wt.shape[0]
    gw3 = 3 * GROUP_WIDTH
    tm = min(tm, s)
    assert n == 3 * gw3

    def body(x_ref, g_ref, w_ref, c_ref, su_ref, sd_ref, p1_ref, p2_ref, h_ref, o0_ref, o1_ref, o2_ref):
        j = pl.program_id(1)

        @pl.when(j == 0)
        def _():
            xv = x_ref[...]
            r = lax.rsqrt(jnp.mean(xv * xv, axis=-1, keepdims=True) + EPS)
            h_ref[...] = (xv * r * g_ref[...]).astype(BF16)

        acc = lax.dot_general(h_ref[...], w_ref[...], NT, preferred_element_type=F32)

        def store(y):
            yb = y.astype(BF16)
            o0_ref[:, pl.ds(pl.multiple_of(j * GROUP_WIDTH, GROUP_WIDTH), GROUP_WIDTH)] = yb[:, :GROUP_WIDTH]
            for grp, o_ref, p_ref in ((1, o1_ref, p1_ref), (2, o2_ref, p2_ref)):
                dil = DILATIONS[grp]
                per = tm // dil
                yp = jnp.dot(p_ref[...], yb[:, grp * GROUP_WIDTH:(grp + 1) * GROUP_WIDTH],
                             preferred_element_type=F32).astype(BF16)
                for r in range(dil):
                    col = pl.multiple_of(r * gw3 + j * GROUP_WIDTH, GROUP_WIDTH)
                    o_ref[:, pl.ds(col, GROUP_WIDTH)] = yp[r * per:(r + 1) * per, :]

        @pl.when(j < 2)
        def _():
            reps = gw3 // 128
            store(_rope_apply(acc, jnp.tile(c_ref[...], (1, reps)), jnp.tile(su_ref[...], (1, reps)),
                              jnp.tile(sd_ref[...], (1, reps))))

        @pl.when(j == 2)
        def _():
            store(acc)

    outs = [jax.ShapeDtypeStruct((s, d), BF16)] + [jax.ShapeDtypeStruct((s // dl, dl * gw3), BF16) for dl in DILATIONS]
    out_specs = [_rows(tm, d)] + [_rows(tm // dl, dl * gw3) for dl in DILATIONS]
    return pl.pallas_call(
        body, name="attn_qkv", out_shape=outs, grid=(s // tm, 3),
        in_specs=[_rows(tm, d), _full((1, d)), pl.BlockSpec((gw3, d), lambda i, j: (j, 0))] + [_rows(tm, 128)] * 3
        + [_full((tm, tm))] * 2,
        out_specs=out_specs, compiler_params=_params("parallel", "arbitrary"),
    )(x, g, wt, *rope, _class_major(tm, DILATIONS[1]), _class_major(tm, DILATIONS[2]))


def _head_masks(rows=SPAN):
    lane = lax.broadcasted_iota(jnp.int32, (rows, 128), 1)
    masks = [lane < HEAD_DIM, lane >= HEAD_DIM]
    lane1 = lax.broadcasted_iota(jnp.int32, (1, 128), 1)
    keep = [jnp.where(lane1 < HEAD_DIM, 1.0, 0.0).astype(BF16), jnp.where(lane1 >= HEAD_DIM, 1.0, 0.0).astype(BF16)]
    return masks, keep


def _attn_fwd(qv, grp, dil):
    l = qv.shape[0]
    s = l * dil
    nb = l // SPAN

    def body(q_ref, kp_ref, kc_ref, vp_ref, vc_ref, o_ref, l_ref):
        b = pl.program_id(1)
        row = lax.broadcasted_iota(jnp.int32, (SPAN, 2 * SPAN), 0)
        col = lax.broadcasted_iota(jnp.int32, (SPAN, 2 * SPAN), 1)
        no_prev = jnp.where(b > 0, 0, 4 * SPAN)
        valid = ((col < SPAN) & (col >= row + no_prev)) | ((col >= SPAN) & (col - SPAN <= row))
        masks, keep = _head_masks()
        for p in range(GROUP_WIDTH // 128):
            sl = slice(p * 128, (p + 1) * 128)
            qp = q_ref[:, sl]
            kk = jnp.concatenate([kp_ref[:, sl], kc_ref[:, sl]], axis=0)
            vv = jnp.concatenate([vp_ref[:, sl], vc_ref[:, sl]], axis=0)
            outs, lses = [], []
            for h in range(2):
                sc = lax.dot_general(qp * keep[h], kk, NT, preferred_element_type=F32) * (HEAD_DIM ** -0.5)
                sc = jnp.where(valid, sc, -1e30)
                mx = jnp.max(sc, axis=-1, keepdims=True)
                pe = jnp.exp(sc - mx)
                den = jnp.sum(pe, axis=-1, keepdims=True)
                pv = jnp.dot(pe.astype(BF16), vv, preferred_element_type=F32)
                outs.append(pv / den)
                lses.append(jnp.broadcast_to(mx + jnp.log(den), (SPAN, 128)))
            o_ref[:, sl] = jnp.where(masks[0], outs[0], outs[1])
            l_ref[:, sl] = jnp.where(masks[0], lses[0], lses[1])

    blk = (SPAN, GROUP_WIDTH)
    cur = lambda t: pl.BlockSpec(blk, lambda r, b: (b, r * 3 + t))
    prev = lambda t: pl.BlockSpec(blk, lambda r, b: (jnp.maximum(b - 1, 0), r * 3 + t))
    out = pl.BlockSpec(blk, lambda r, b: (b, r))
    o, lse = pl.pallas_call(
        body, name=f"attn_fwd_g{grp}", out_shape=[jax.ShapeDtypeStruct((l, dil * GROUP_WIDTH), F32)] * 2,
        grid=(dil, nb), in_specs=[cur(0), prev(1), cur(1), prev(2), cur(2)], out_specs=[out, out],
        compiler_params=_params("parallel", "arbitrary"),
    )(qv, qv, qv, qv, qv)
    return o.reshape(s, GROUP_WIDTH), lse.reshape(s, GROUP_WIDTH)


def _resnorm_store(y, x_ref, g_ref, y_ref, xo_ref):
    r = lax.rsqrt(jnp.mean(y * y, axis=-1, keepdims=True) + EPS)
    y_ref[...] = y
    xo_ref[...] = x_ref[...] + y * r * g_ref[...]


def _mix_wo(os_, ls_, wot, x, g, tm=256):
    s, d = x.shape
    gw = wot.shape[1]
    tm = min(tm, s)

    def body(o0, o1, o2, l0, l1, l2, w_ref, x_ref, g_ref, y_ref, xo_ref, mixed_ref, lse_ref):
        a0, a1, a2 = l0[...], l1[...], l2[...]
        mx = jnp.maximum(jnp.maximum(a0, a1), a2)
        e0, e1, e2 = jnp.exp(a0 - mx), jnp.exp(a1 - mx), jnp.exp(a2 - mx)
        den = e0 + e1 + e2
        mixed = (e0 / den) * o0[...] + (e1 / den) * o1[...] + (e2 / den) * o2[...]
        mixed_ref[...] = mixed.astype(BF16)
        lse_ref[...] = mx + jnp.log(den)
        y = lax.dot_general(mixed.astype(BF16), w_ref[...], NT, preferred_element_type=F32)
        _resnorm_store(y, x_ref, g_ref, y_ref, xo_ref)

    return pl.pallas_call(
        body, name="mix_wo",
        out_shape=[jax.ShapeDtypeStruct((s, d), F32), jax.ShapeDtypeStruct((s, d), F32),
                   jax.ShapeDtypeStruct((s, gw), BF16), jax.ShapeDtypeStruct((s, gw), F32)],
        grid=(s // tm,), in_specs=[_rows(tm, gw)] * 6 + [_full((d, gw)), _rows(tm, d), _full((1, d))],
        out_specs=[_rows(tm, d), _rows(tm, d), _rows(tm, gw), _rows(tm, gw)],
        compiler_params=_params("parallel"),
    )(*os_, *ls_, wot, x, g)


def _matmul_resnorm(a, w, x, g, *, name, bias=None, tm=512):
    s, k = a.shape
    d = w.shape[1]
    tm = min(tm, s)

    def body(*refs):
        a_ref, w_ref = refs[:2]
        b_ref = refs[2] if bias is not None else None
        x_ref, g_ref, y_ref, xo_ref = refs[-4:]
        y = jnp.dot(a_ref[...], w_ref[...], preferred_element_type=F32)
        if b_ref is not None:
            y = y + b_ref[...]
        _resnorm_store(y, x_ref, g_ref, y_ref, xo_ref)

    in_specs = [_rows(tm, k), _full((k, d))] + ([_full((1, d))] if bias is not None else []) + [_rows(tm, d), _full((1, d))]
    args = [a, w] + ([bias] if bias is not None else []) + [x, g]
    return pl.pallas_call(
        body, name=name, out_shape=[jax.ShapeDtypeStruct((s, d), F32)] * 2, grid=(s // tm,),
        in_specs=in_specs, out_specs=[_rows(tm, d)] * 2, compiler_params=_params("parallel"),
    )(*args)


def _conv3_taps(z, halo, first):
    row = lax.broadcasted_iota(jnp.int32, z.shape, 0)
    halo = halo * jnp.where(first, 0.0, 1.0)
    h6, h7 = halo[6:7, :], halo[7:8, :]
    z1 = jnp.where(row == 0, h7, pltpu.roll(z, 1, 0))
    z2 = jnp.where(row == 0, h6, jnp.where(row == 1, h7, pltpu.roll(z, 2, 0)))
    return z2, z1


def _ffn_cols(f):
    return _tile(f)


def _lane_chunks(width, fn):
    def step(k, carry):
        fn(pl.ds(pl.multiple_of(k * 128, 128), 128))
        return carry

    lax.fori_loop(0, width // 128, step, 0)


def _ffn_act(z, w_dw, b_dw, tm=256):
    s, f2 = z.shape
    f = f2 // 2
    tm = min(tm, s)
    tc = _ffn_cols(f)
    nfc = f // tc

    def body(zu, zg, hu, hg, wu, wg, bu, bg, o_ref):
        first = pl.program_id(0) == 0

        def chunk(cs):
            def conv(z_ref, h_ref, w_ref, b_ref):
                zc = z_ref[:, cs].astype(F32)
                z2, z1 = _conv3_taps(zc, h_ref[:, cs].astype(F32), first)
                return w_ref[0:1, cs] * z2 + w_ref[1:2, cs] * z1 + w_ref[2:3, cs] * zc + b_ref[:, cs]

            up, gate = conv(zu, hu, wu, bu), conv(zg, hg, wg, bg)
            o_ref[:, cs] = (gate * _sigmoid(gate) * up).astype(BF16)

        _lane_chunks(tc, chunk)

    hb = tm // 8
    tile = lambda off: pl.BlockSpec((tm, tc), lambda i, j: (i, off + j))
    halo = lambda off: pl.BlockSpec((8, tc), lambda i, j: (jnp.maximum(i * hb - 1, 0), off + j))
    prm = lambda rows, off: pl.BlockSpec((rows, tc), lambda i, j: (0, off + j))
    return pl.pallas_call(
        body, name="ffn_act", out_shape=jax.ShapeDtypeStruct((s, f), BF16), grid=(s // tm, nfc),
        in_specs=[tile(0), tile(nfc), halo(0), halo(nfc), prm(FFN_CONV, 0), prm(FFN_CONV, nfc), prm(1, 0), prm(1, nfc)],
        out_specs=pl.BlockSpec((tm, tc), lambda i, j: (i, j)), compiler_params=_params("parallel", "parallel"),
    )(z, z, z, z, w_dw, w_dw, b_dw, b_dw)


def _shifted_planes(ext_ref):
    rows = ext_ref.shape[1]
    for s in range(1, 8):
        ext_ref[s, 0:rows - 8, :] = ext_ref[0, s:s + rows - 8, :]


def _window(ext_ref, off, tm, cs):
    s = off % 8
    return ext_ref[s, off - s:off - s + tm, cs]


def _conv_taps(ext_ref, w_ref, offs, tm, out_ref):
    def chunk(cs):
        acc = w_ref[0:1, cs] * _window(ext_ref, offs[0], tm, cs)
        for j in range(1, len(offs)):
            acc = acc + w_ref[j:j + 1, cs] * _window(ext_ref, offs[j], tm, cs)
        out_ref[:, cs] = acc

    _lane_chunks(out_ref.shape[1], chunk)


def _glu_planes(ag_ref, halo_ref, ext_ref, first, c):
    hal = halo_ref[...].astype(F32)
    ext_ref[0, 0:CONV_HALO, :] = hal[:, :c] * _sigmoid(hal[:, c:]) * jnp.where(first, 0.0, 1.0)
    ag = ag_ref[...].astype(F32)
    ext_ref[0, CONV_HALO:, :] = ag[:, :c] * _sigmoid(ag[:, c:])
    _shifted_planes(ext_ref)


def _layernorm_stats(u1):
    mu = jnp.mean(u1, axis=-1, keepdims=True)
    cen = u1 - mu
    rstd = lax.rsqrt(jnp.mean(cen * cen, axis=-1, keepdims=True) + EPS)
    return cen * rstd, rstd


def _conv_mid(ag, w_dw, b_dw, ln_g, ln_b, tm=256):
    s, c2 = ag.shape
    c = c2 // 2
    tm = min(tm, s)

    def body(ag_ref, halo_ref, w_ref, b_ref, g_ref, bb_ref, o_ref, u1_ref, ext_ref):
        _glu_planes(ag_ref, halo_ref, ext_ref, pl.program_id(0) == 0, c)
        base = CONV_HALO - (CONV_KERNEL - 1)
        _conv_taps(ext_ref, w_ref, [base + j for j in range(CONV_KERNEL)], tm, u1_ref)
        xh, _ = _layernorm_stats(u1_ref[...] + b_ref[...])
        u2 = xh * g_ref[...] + bb_ref[...]
        o_ref[...] = (u2 * _sigmoid(u2)).astype(BF16)

    hb = tm // CONV_HALO
    return pl.pallas_call(
        body, name="conv_mid", out_shape=[jax.ShapeDtypeStruct((s, c), BF16), jax.ShapeDtypeStruct((s, c), F32)], grid=(s // tm,),
        in_specs=[_rows(tm, c2), pl.BlockSpec((CONV_HALO, c2), lambda i: (jnp.maximum(i * hb - 1, 0), 0)),
                  _full((CONV_KERNEL, c)), _full((1, c)), _full((1, c)), _full((1, c))],
        out_specs=[_rows(tm, c), _rows(tm, c)], scratch_shapes=[pltpu.VMEM((8, CONV_HALO + tm, c), F32)],
        compiler_params=_params("arbitrary"),
    )(ag, ag, w_dw, b_dw, ln_g, ln_b)


def _loss_grad(xo, target, tm=512):
    s, d = xo.shape
    tm = min(tm, s)

    def body(x_ref, t_ref, dx_ref, loss_ref):
        @pl.when(pl.program_id(0) == 0)
        def _():
            loss_ref[...] = jnp.zeros_like(loss_ref)

        err = x_ref[...] - t_ref[...]
        dx_ref[...] = err * (1.0 / d)
        loss_ref[...] += 0.5 * jnp.sum(jnp.mean(err * err, axis=-1, keepdims=True))

    return pl.pallas_call(
        body, name="loss_grad", out_shape=[jax.ShapeDtypeStruct((s, d), F32), jax.ShapeDtypeStruct((1, 128), F32)],
        grid=(s // tm,), in_specs=[_rows(tm, d)] * 2, out_specs=[_rows(tm, d), _full((1, 128))],
        compiler_params=_params("arbitrary"),
    )(xo, target)


def _postnorm_bwd(y, g, dxo, *, name, with_bias_grad=False, tm=512):
    s, d = y.shape
    tm = min(tm, s)

    def body(y_ref, g_ref, dx_ref, dy_ref, dg_ref, *rest):
        @pl.when(pl.program_id(0) == 0)
        def _():
            dg_ref[...] = jnp.zeros_like(dg_ref)
            for r_ in rest:
                r_[...] = jnp.zeros_like(r_)

        yv, dxo_v = y_ref[...], dx_ref[...]
        r = lax.rsqrt(jnp.mean(yv * yv, axis=-1, keepdims=True) + EPS)
        yh = yv * r
        dyh = dxo_v * g_ref[...]
        dy = r * (dyh - yh * jnp.mean(dyh * yh, axis=-1, keepdims=True))
        dy_ref[...] = dy.astype(BF16)
        dg_ref[...] += jnp.sum(dxo_v * yh, axis=0, keepdims=True)
        for r_ in rest:
            r_[...] += jnp.sum(dy, axis=0, keepdims=True)

    nacc = 2 if with_bias_grad else 1
    return pl.pallas_call(
        body, name=name, out_shape=[jax.ShapeDtypeStruct((s, d), BF16)] + [jax.ShapeDtypeStruct((1, d), F32)] * nacc,
        grid=(s // tm,), in_specs=[_rows(tm, d), _full((1, d)), _rows(tm, d)],
        out_specs=[_rows(tm, d)] + [_full((1, d))] * nacc, compiler_params=_params("arbitrary"),
    )(y, g, dxo)


def _matmul(gmat, w, *, name, out_dtype, transposed_w, tm=512):
    s, k = gmat.shape
    n = w.shape[0] if transposed_w else w.shape[1]
    tm = min(tm, s)

    def body(g_ref, w_ref, o_ref):
        if transposed_w:
            acc = lax.dot_general(g_ref[...], w_ref[...], NT, preferred_element_type=F32)
        else:
            acc = jnp.dot(g_ref[...], w_ref[...], preferred_element_type=F32)
        o_ref[...] = acc.astype(out_dtype)

    return pl.pallas_call(
        body, name=name, out_shape=jax.ShapeDtypeStruct((s, n), out_dtype), grid=(s // tm,),
        in_specs=[_rows(tm, k), _full(w.shape)], out_specs=_rows(tm, n), compiler_params=_params("parallel"),
    )(gmat, w)


def _matmul_prenorm_bwd(pieces, wt, x, g, dres, *, name, tm=256):
    s, d = x.shape
    tm = min(tm, s)
    np_ = len(pieces)

    def body(*refs):
        p_refs, w_refs = refs[:np_], refs[np_:2 * np_]
        x_ref, g_ref, r_ref, dx_ref, dg_ref = refs[2 * np_:]

        @pl.when(pl.program_id(0) == 0)
        def _():
            dg_ref[...] = jnp.zeros_like(dg_ref)

        dh = None
        for p_ref, w_ref in zip(p_refs, w_refs):
            t = jnp.dot(p_ref[...], w_ref[...], preferred_element_type=F32)
            dh = t if dh is None else dh + t
        xv = x_ref[...]
        r = lax.rsqrt(jnp.mean(xv * xv, axis=-1, keepdims=True) + EPS)
        xh = xv * r
        dyh = dh * g_ref[...]
        dx_ref[...] = r_ref[...] + r * (dyh - xh * jnp.mean(dyh * xh, axis=-1, keepdims=True))
        dg_ref[...] += jnp.sum(dh * xh, axis=0, keepdims=True)

    in_specs = []
    for _, c0, kc, _ in pieces:
        assert c0 % kc == 0
        in_specs.append(pl.BlockSpec((tm, kc), lambda i, _b=c0 // kc: (i, _b)))
    for _, _, kc, r0 in pieces:
        assert r0 % kc == 0
        in_specs.append(pl.BlockSpec((kc, d), lambda i, _b=r0 // kc: (_b, 0)))
    in_specs += [_rows(tm, d), _full((1, d)), _rows(tm, d)]
    return pl.pallas_call(
        body, name=name, out_shape=[jax.ShapeDtypeStruct((s, d), F32), jax.ShapeDtypeStruct((1, d), F32)],
        grid=(s // tm,), in_specs=in_specs, out_specs=[_rows(tm, d), _full((1, d))],
        compiler_params=_params("arbitrary"),
    )(*[p[0] for p in pieces], *[wt] * np_, x, g, dres)


def _weight_grad(a, gmat, *, name, a_col0=0, ka=None, out=None, out_shape=None, layer=0, row0=0, ts=1024):
    s = a.shape[0]
    ka = a.shape[1] if ka is None else ka
    n = gmat.shape[1]
    ts = min(ts, s)
    tka = _tile(ka, a_col0, row0)
    shape = out.shape if out is not None else out_shape
    nsteps = s // ts

    def body(a_ref, g_ref, *rest):
        o_ref, acc_ref = rest[-2:]
        i = pl.program_id(1)

        @pl.when(i == 0)
        def _():
            acc_ref[...] = jnp.zeros_like(acc_ref)

        acc_ref[...] += lax.dot_general(a_ref[...], g_ref[...], TN, preferred_element_type=F32)

        @pl.when(i == nsteps - 1)
        def _():
            o_ref[...] = acc_ref[...].astype(BF16)

    in_specs = [pl.BlockSpec((ts, tka), lambda k, i: (i, a_col0 // tka + k)), pl.BlockSpec((ts, n), lambda k, i: (i, 0))]
    args = [a, gmat]
    aliases = {}
    if out is not None:
        in_specs.append(ANY)
        args.append(out)
        aliases = {2: 0}
    return pl.pallas_call(
        body, name=name, out_shape=jax.ShapeDtypeStruct(shape, BF16), grid=(ka // tka, nsteps), in_specs=in_specs,
        out_specs=pl.BlockSpec((None, tka, n), lambda k, i: (layer, row0 // tka + k, 0)),
        scratch_shapes=[pltpu.VMEM((tka, n), F32)],
        input_output_aliases=aliases, compiler_params=_params("parallel", "arbitrary"),
    )(*args)


def _ffn_act_bwd(z, dact, w_dw, b_dw, tm=256):
    s, f2 = z.shape
    f = f2 // 2
    tm = min(tm, s)
    tc = _ffn_cols(f)
    nfc = f // tc

    def body(zu, zg, hu, hg, wu, wg, bu, bg, da_ref, du_ref, dgt_ref, dbu_ref, dbg_ref, dwu_ref, dwg_ref):
        i = pl.program_id(1)

        @pl.when(i == 0)
        def _():
            for r_ in (dbu_ref, dbg_ref, dwu_ref, dwg_ref):
                r_[...] = jnp.zeros_like(r_)

        def chunk(cs):
            def conv(z_ref, h_ref, w_ref, b_ref):
                zc = z_ref[:, cs].astype(F32)
                z2, z1 = _conv3_taps(zc, h_ref[:, cs].astype(F32), i == 0)
                return (z2, z1, zc), w_ref[0:1, cs] * z2 + w_ref[1:2, cs] * z1 + w_ref[2:3, cs] * zc + b_ref[:, cs]

            taps_u, up = conv(zu, hu, wu, bu)
            taps_g, gate = conv(zg, hg, wg, bg)
            da = da_ref[:, cs].astype(F32)
            sg = _sigmoid(gate)
            d_up = da * (gate * sg)
            d_gate = da * up * (sg * (1.0 + gate * (1.0 - sg)))
            du_ref[:, cs] = d_up.astype(BF16)
            dgt_ref[:, cs] = d_gate.astype(BF16)
            for dv, taps, db_ref, dw_ref in ((d_up, taps_u, dbu_ref, dwu_ref), (d_gate, taps_g, dbg_ref, dwg_ref)):
                db_ref[:, cs] += jnp.sum(dv, axis=0, keepdims=True)
                for k_, tap in enumerate(taps):
                    dw_ref[k_:k_ + 1, cs] += jnp.sum(dv * tap, axis=0, keepdims=True)

        _lane_chunks(tc, chunk)

    hb = tm // 8
    tile = lambda off: pl.BlockSpec((tm, tc), lambda j, i: (i, off + j))
    halo = lambda off: pl.BlockSpec((8, tc), lambda j, i: (jnp.maximum(i * hb - 1, 0), off + j))
    prm = lambda rows, off: pl.BlockSpec((rows, tc), lambda j, i: (0, off + j))
    acc = lambda rows: pl.BlockSpec((rows, tc), lambda j, i: (0, j))
    return pl.pallas_call(
        body, name="ffn_act_bwd",
        out_shape=[jax.ShapeDtypeStruct((s, f), BF16)] * 2 + [jax.ShapeDtypeStruct((1, f), F32)] * 2
        + [jax.ShapeDtypeStruct((FFN_CONV, f), F32)] * 2,
        grid=(nfc, s // tm),
        in_specs=[tile(0), tile(nfc), halo(0), halo(nfc), prm(FFN_CONV, 0), prm(FFN_CONV, nfc), prm(1, 0), prm(1, nfc), tile(0)],
        out_specs=[tile(0), tile(0), acc(1), acc(1), acc(FFN_CONV), acc(FFN_CONV)],
        compiler_params=_params("parallel", "arbitrary"),
    )(z, z, z, z, w_dw, w_dw, b_dw, b_dw, dact)


def _conv3_transpose(dug, w_dw, col0, tm=256):
    s, f = dug.shape
    tm = min(tm, s)
    tc = _ffn_cols(f)
    nfc = f // tc
    nrow = s // tm
    off = col0 // tc

    def body(d_ref, n_ref, w_ref, o_ref):
        keep_next = jnp.where(pl.program_id(0) == nrow - 1, 0.0, 1.0)

        def chunk(cs):
            dv = d_ref[:, cs].astype(F32)
            nxt = n_ref[:, cs].astype(F32) * keep_next
            n0, n1 = nxt[0:1, :], nxt[1:2, :]
            row = lax.broadcasted_iota(jnp.int32, dv.shape, 0)
            d1 = jnp.where(row == tm - 1, n0, pltpu.roll(dv, tm - 1, 0))
            d2 = jnp.where(row == tm - 1, n1, jnp.where(row == tm - 2, n0, pltpu.roll(dv, tm - 2, 0)))
            o_ref[:, cs] = (w_ref[2:3, cs] * dv + w_ref[1:2, cs] * d1 + w_ref[0:1, cs] * d2).astype(BF16)

        _lane_chunks(tc, chunk)

    hb = tm // 8
    return pl.pallas_call(
        body, name="conv3_transpose", out_shape=jax.ShapeDtypeStruct((s, f), BF16), grid=(nrow, nfc),
        in_specs=[pl.BlockSpec((tm, tc), lambda i, j: (i, j)),
                  pl.BlockSpec((8, tc), lambda i, j: (jnp.minimum((i + 1) * hb, s // 8 - 1), j)),
                  pl.BlockSpec((FFN_CONV, tc), lambda i, j: (0, off + j))],
        out_specs=pl.BlockSpec((tm, tc), lambda i, j: (i, j)), compiler_params=_params("parallel", "parallel"),
    )(dug, dug, w_dw)


def _conv_mid_bwd(ag, u1, du3, b_dw, ln_g, ln_b, tm=256):
    s, c2 = ag.shape
    c = c2 // 2
    tm = min(tm, s)

    def body(ag_ref, halo_ref, u1in_ref, du_ref, b_ref, g_ref, bb_ref, o_ref, dlg_ref, dlb_ref, db_ref, dw_ref, ext_ref, u1_ref):
        @pl.when(pl.program_id(0) == 0)
        def _():
            for r_ in (dlg_ref, dlb_ref, db_ref, dw_ref):
                r_[...] = jnp.zeros_like(r_)

        _glu_planes(ag_ref, halo_ref, ext_ref, pl.program_id(0) == 0, c)
        xh, rstd = _layernorm_stats(u1in_ref[...] + b_ref[...])
        u2 = xh * g_ref[...] + bb_ref[...]
        sg = _sigmoid(u2)
        du2 = du_ref[...] * (sg * (1.0 + u2 * (1.0 - sg)))
        dlg_ref[...] += jnp.sum(du2 * xh, axis=0, keepdims=True)
        dlb_ref[...] += jnp.sum(du2, axis=0, keepdims=True)
        dxh = du2 * g_ref[...]
        du1 = rstd * (dxh - jnp.mean(dxh, axis=-1, keepdims=True) - xh * jnp.mean(dxh * xh, axis=-1, keepdims=True))
        o_ref[...] = du1.astype(BF16)
        db_ref[...] += jnp.sum(du1, axis=0, keepdims=True)
        u1_ref[...] = du1
        base = CONV_HALO - (CONV_KERNEL - 1)

        def chunk(cs):
            dc = u1_ref[:, cs]
            for j in range(CONV_KERNEL):
                dw_ref[j:j + 1, cs] += jnp.sum(dc * _window(ext_ref, base + j, tm, cs), axis=0, keepdims=True)

        _lane_chunks(c, chunk)

    hb = tm // CONV_HALO
    vec = _full((1, c))
    return pl.pallas_call(
        body, name="conv_mid_bwd",
        out_shape=[jax.ShapeDtypeStruct((s, c), BF16)] + [jax.ShapeDtypeStruct((1, c), F32)] * 3
        + [jax.ShapeDtypeStruct((CONV_HALO, c), F32)],
        grid=(s // tm,),
        in_specs=[_rows(tm, c2), pl.BlockSpec((CONV_HALO, c2), lambda i: (jnp.maximum(i * hb - 1, 0), 0)), _rows(tm, c),
                  _rows(tm, c), vec, vec, vec],
        out_specs=[_rows(tm, c), vec, vec, vec, _full((CONV_HALO, c))],
        scratch_shapes=[pltpu.VMEM((8, CONV_HALO + tm, c), F32), pltpu.VMEM((tm, c), F32)],
        compiler_params=_params("arbitrary"),
    )(ag, ag, u1, du3, b_dw, ln_g, ln_b)


def _glu_conv_bwd(du1, ag, w_dw, tm=256):
    s, c = du1.shape
    tm = min(tm, s)
    nrow = s // tm

    def body(d_ref, n_ref, ag_ref, w_ref, o_ref, db_ref, ext_ref, du0_ref):
        @pl.when(pl.program_id(0) == 0)
        def _():
            db_ref[...] = jnp.zeros_like(db_ref)

        ext_ref[0, 0:tm, :] = d_ref[...].astype(F32)
        ext_ref[0, tm:, :] = n_ref[...].astype(F32) * jnp.where(pl.program_id(0) == nrow - 1, 0.0, 1.0)
        _shifted_planes(ext_ref)
        top = CONV_KERNEL - 1
        _conv_taps(ext_ref, w_ref, [top - j for j in range(CONV_KERNEL)], tm, du0_ref)
        du0 = du0_ref[...]
        ag = ag_ref[...].astype(F32)
        a, gt = ag[:, :c], ag[:, c:]
        sg = _sigmoid(gt)
        da = du0 * sg
        dgt = du0 * a * (sg * (1.0 - sg))
        o_ref[:, :c] = da.astype(BF16)
        o_ref[:, c:] = dgt.astype(BF16)
        db_ref[:, :c] += jnp.sum(da, axis=0, keepdims=True)
        db_ref[:, c:] += jnp.sum(dgt, axis=0, keepdims=True)

    hb = tm // CONV_HALO
    return pl.pallas_call(
        body, name="glu_conv_bwd",
        out_shape=[jax.ShapeDtypeStruct((s, 2 * c), BF16), jax.ShapeDtypeStruct((1, 2 * c), F32)], grid=(nrow,),
        in_specs=[_rows(tm, c), pl.BlockSpec((CONV_HALO, c), lambda i: (jnp.minimum((i + 1) * hb, s // CONV_HALO - 1), 0)),
                  _rows(tm, 2 * c), _full((CONV_KERNEL, c))],
        out_specs=[_rows(tm, 2 * c), _full((1, 2 * c))],
        scratch_shapes=[pltpu.VMEM((8, tm + CONV_HALO, c), F32), pltpu.VMEM((tm, c), F32)],
        compiler_params=_params("arbitrary"),
    )(du1, du1, ag, w_dw)


def _head_rows(v, mask):
    return jnp.max(jnp.where(mask, v, -jnp.inf), axis=-1, keepdims=True)


def _attn_bwd_dq(qv, dmix, mixed, lse, rope, grp, dil):
    l = qv.shape[0]
    s = l * dil
    nb = l // SPAN
    view = lambda t: t.reshape(l, dil * t.shape[1])

    def body(q_ref, kp_ref, kc_ref, vp_ref, vc_ref, do_ref, mx_ref, l_ref, c_ref, su_ref, sd_ref, o_ref):
        b = pl.program_id(1)
        row = lax.broadcasted_iota(jnp.int32, (SPAN, 2 * SPAN), 0)
        col = lax.broadcasted_iota(jnp.int32, (SPAN, 2 * SPAN), 1)
        no_prev = jnp.where(b > 0, 0, 4 * SPAN)
        valid = ((col < SPAN) & (col >= row + no_prev)) | ((col >= SPAN) & (col - SPAN <= row))
        masks, keep = _head_masks()
        for p in range(GROUP_WIDTH // 128):
            sl = slice(p * 128, (p + 1) * 128)
            qp, dop = q_ref[:, sl], do_ref[:, sl]
            kk = jnp.concatenate([kp_ref[:, sl], kc_ref[:, sl]], axis=0)
            vv = jnp.concatenate([vp_ref[:, sl], vc_ref[:, sl]], axis=0)
            prod = dop.astype(F32) * mx_ref[:, sl].astype(F32)
            lsep = l_ref[:, sl]
            dqs = []
            for h in range(2):
                qh, doh = qp * keep[h], dop * keep[h]
                sc = lax.dot_general(qh, kk, NT, preferred_element_type=F32) * (HEAD_DIM ** -0.5)
                pe = jnp.where(valid, jnp.exp(sc - _head_rows(lsep, masks[h])), 0.0)
                dp = lax.dot_general(doh, vv, NT, preferred_element_type=F32)
                dbar = jnp.sum(jnp.where(masks[h], prod, 0.0), axis=-1, keepdims=True)
                ds = pe * (dp - dbar) * (HEAD_DIM ** -0.5)
                dqs.append(jnp.dot(ds.astype(BF16), kk, preferred_element_type=F32))
            dq = jnp.where(masks[0], dqs[0], dqs[1])
            o_ref[:, sl] = _rope_transpose(dq, c_ref[...], su_ref[...], sd_ref[...]).astype(BF16)

    blk = (SPAN, GROUP_WIDTH)
    cur = lambda t: pl.BlockSpec(blk, lambda r, b: (b, r * 3 + t))
    prev = lambda t: pl.BlockSpec(blk, lambda r, b: (jnp.maximum(b - 1, 0), r * 3 + t))
    own = pl.BlockSpec(blk, lambda r, b: (b, r))
    tab = pl.BlockSpec((SPAN, 128), lambda r, b: (b, r))
    out = pl.pallas_call(
        body, name=f"attn_bwd_dq_g{grp}", out_shape=jax.ShapeDtypeStruct((l, dil * GROUP_WIDTH), BF16), grid=(dil, nb),
        in_specs=[cur(0), prev(1), cur(1), prev(2), cur(2), own, own, own, tab, tab, tab], out_specs=own,
        compiler_params=_params("parallel", "arbitrary"),
    )(qv, qv, qv, qv, qv, view(dmix), view(mixed), view(lse), *[view(t) for t in rope])
    return out.reshape(s, GROUP_WIDTH)


def _attn_bwd_dkv(qv, dmix, mixed, lse, rope, grp, dil):
    l = qv.shape[0]
    s = l * dil
    nb = l // SPAN
    view = lambda t: t.reshape(l, dil * t.shape[1])

    def body(k_ref, v_ref, qc_ref, qn_ref, doc_ref, don_ref, mc_ref, mn_ref, lc_ref, ln_ref,
             c_ref, su_ref, sd_ref, o_ref):
        b = pl.program_id(1)
        row = lax.broadcasted_iota(jnp.int32, (2 * SPAN, SPAN), 0)
        col = lax.broadcasted_iota(jnp.int32, (2 * SPAN, SPAN), 1)
        no_next = jnp.where(b < nb - 1, 0, 4 * SPAN)
        valid = ((row < SPAN) & (col <= row)) | ((row >= SPAN) & (col >= row - SPAN + no_next))
        masks, keep = _head_masks()
        masks2, _ = _head_masks(2 * SPAN)
        for p in range(GROUP_WIDTH // 128):
            sl = slice(p * 128, (p + 1) * 128)
            kp, vp = k_ref[:, sl], v_ref[:, sl]
            qq = jnp.concatenate([qc_ref[:, sl], qn_ref[:, sl]], axis=0)
            doo = jnp.concatenate([doc_ref[:, sl], don_ref[:, sl]], axis=0)
            mm = jnp.concatenate([mc_ref[:, sl], mn_ref[:, sl]], axis=0)
            ll = jnp.concatenate([lc_ref[:, sl], ln_ref[:, sl]], axis=0)
            prod = doo.astype(F32) * mm.astype(F32)
            dks, dvs = [], []
            for h in range(2):
                qh, doh = qq * keep[h], doo * keep[h]
                sc = lax.dot_general(qh, kp, NT, preferred_element_type=F32) * (HEAD_DIM ** -0.5)
                pe = jnp.where(valid, jnp.exp(sc - _head_rows(ll, masks2[h])), 0.0)
                dp = lax.dot_general(doh, vp, NT, preferred_element_type=F32)
                dbar = jnp.sum(jnp.where(masks2[h], prod, 0.0), axis=-1, keepdims=True)
                ds = pe * (dp - dbar) * (HEAD_DIM ** -0.5)
                dvs.append(lax.dot_general(pe.astype(BF16), doo, TN, preferred_element_type=F32))
                dks.append(lax.dot_general(ds.astype(BF16), qq, TN, preferred_element_type=F32))
            dk = jnp.where(masks[0], dks[0], dks[1])
            o_ref[:, sl] = _rope_transpose(dk, c_ref[...], su_ref[...], sd_ref[...]).astype(BF16)
            o_ref[:, GROUP_WIDTH + p * 128:GROUP_WIDTH + (p + 1) * 128] = jnp.where(masks[0], dvs[0], dvs[1]).astype(BF16)

    blk = (SPAN, GROUP_WIDTH)
    nxt_b = lambda b: jnp.minimum(b + 1, nb - 1)
    col_of = lambda t: pl.BlockSpec(blk, lambda r, b: (b, r * 3 + t))
    q_next = pl.BlockSpec(blk, lambda r, b: (nxt_b(b), r * 3))
    own = pl.BlockSpec(blk, lambda r, b: (b, r))
    own_next = pl.BlockSpec(blk, lambda r, b: (nxt_b(b), r))
    tab = pl.BlockSpec((SPAN, 128), lambda r, b: (b, r))
    dv_, mv, lv = view(dmix), view(mixed), view(lse)
    out = pl.pallas_call(
        body, name=f"attn_bwd_dkv_g{grp}", out_shape=jax.ShapeDtypeStruct((l, dil * 2 * GROUP_WIDTH), BF16), grid=(dil, nb),
        in_specs=[col_of(1), col_of(2), col_of(0), q_next, own, own_next, own, own_next, own, own_next, tab, tab, tab],
        out_specs=pl.BlockSpec((SPAN, 2 * GROUP_WIDTH), lambda r, b: (b, r)),
        compiler_params=_params("parallel", "arbitrary"),
    )(qv, qv, qv, qv, dv_, dv_, mv, mv, lv, lv, *[view(t) for t in rope])
    return out.reshape(s, 2 * GROUP_WIDTH)


def _rope_freq_row():
    half = ROT_DIM // 2
    inv = (ROPE_THETA ** (-np.arange(half, dtype=np.float32) / half)).astype(np.float32)
    row = np.zeros((1, 128), np.float32)
    for head in range(128 // HEAD_DIM):
        row[0, head * HEAD_DIM:head * HEAD_DIM + half] = inv
        row[0, head * HEAD_DIM + half:head * HEAD_DIM + ROT_DIM] = inv
    return jnp.asarray(row)


def _ffn_fwd(x, g_pre, g_post, w_up_t, w_dw, b_dw, w_down):
    h, z = _norm_matmul(x, g_pre, w_up_t, tn=_tile(w_up_t.shape[0]), name="ffn_up")
    act = _ffn_act(z, w_dw, b_dw)
    y, xo = _matmul_resnorm(act, w_down, x, g_post, name="ffn_down")
    return xo, (x, h, z, act, y)


def _ffn_bwd(saved, dxo, g_pre, g_post, w_up_t, w_dw, b_dw, w_down):
    x, h, z, act, y = saved
    f = act.shape[1]
    d = x.shape[1]
    dy, dg_post = _postnorm_bwd(y, g_post, dxo, name="ffn_post_bwd")
    dact = _matmul(dy, w_down, name="ffn_dact", out_dtype=BF16, transposed_w=True)
    d_down = _weight_grad(act, dy, name="ffn_dw_down", out_shape=(1, f, d))
    dug_u, dug_g, db_u, db_g, dwd_u, dwd_g = _ffn_act_bwd(z, dact, w_dw, b_dw)
    dz_u = _conv3_transpose(dug_u, w_dw, 0)
    dz_g = _conv3_transpose(dug_g, w_dw, f)
    dx, dg_pre = _matmul_prenorm_bwd([(dz_u, 0, f, 0), (dz_g, 0, f, f)], w_up_t, x, g_pre, dxo, name="ffn_dx")
    d_up_t = _weight_grad(dz_u, h, name="ffn_dw_up", out_shape=(1, 2 * f, d))
    d_up_t = _weight_grad(dz_g, h, name="ffn_dw_up", out=d_up_t, row0=f)
    grads = dict(w_dw=jnp.concatenate([dwd_u, dwd_g], axis=1), b_dw=jnp.concatenate([db_u, db_g], axis=1),
                 g_pre=dg_pre, g_post=dg_post)
    return dx, grads, d_up_t, d_down


def _local_step(x, pos_col, target, p, tie=None, late_weights=None, exchange=None):
    ng = p["norm_g"]
    row = lambda r: ng[r:r + 1]
    freq = _rope_freq_row()
    rope = _rope_tables(pos_col, freq if tie is None else freq + tie[0:1])
    d = x.shape[1]

    h0, *qkv = _qkv_proj(x, row(0), p["w_qkv_t"], rope)
    os_, ls_ = zip(*[_attn_fwd(qkv[g_], g_, d_) for g_, d_ in enumerate(DILATIONS)])
    y_a, x1, mixed, lse = _mix_wo(os_, ls_, p["w_o_t"], x, row(1))
    if late_weights is not None:
        p = {**p, **late_weights(x1)}
    x2, ffn0 = _ffn_fwd(x1, row(2), row(3), p["w_up_t"][0], p["ffn_w_dw"][0], p["ffn_b_dw"][0], p["w_down"][0])
    h1, ag = _norm_matmul(x2, row(4), p["w_pw1_t"], tn=_tile(p["w_pw1_t"].shape[0]), name="conv_pw1", bias=p["b_pw1"])
    u3, u1 = _conv_mid(ag, p["conv_w_dw"], p["conv_b_dw"], p["ln_g"], p["ln_b"])
    y_c, x3 = _matmul_resnorm(u3, p["w_pw2"], x2, row(5), name="conv_pw2", bias=p["b_pw2"])
    x4, ffn1 = _ffn_fwd(x3, row(6), row(7), p["w_up_t"][1], p["ffn_w_dw"][1], p["ffn_b_dw"][1], p["w_down"][1])
    dx4, loss = _loss_grad(x4, target)

    big = [BF16, BF16]

    def tied(r, *tokens):
        tokens = [t for t in tokens if t is not None]
        return row(r) if not tokens else row(r) + jnp.tile(sum(tokens)[0:1], (1, d // 128))

    dx3, gf1, d_up1, d_down1 = _ffn_bwd(ffn1, dx4, row(6), row(7), p["w_up_t"][1], p["ffn_w_dw"][1], p["ffn_b_dw"][1],
                                        p["w_down"][1])
    t0 = exchange.submit("ffn1", [d_up1, d_down1], big) if exchange else None
    dy_c, dg5, db_pw2 = _postnorm_bwd(y_c, tied(5, t0), dx3, name="conv_post_bwd", with_bias_grad=True)
    du3 = _matmul(dy_c, p["w_pw2"], name="conv_du3", out_dtype=F32, transposed_w=True)
    d_wpw2 = _weight_grad(u3, dy_c, name="conv_dw_pw2", out_shape=(1, u3.shape[1], d))
    du1, d_lng, d_lnb, d_cbdw, d_cwdw = _conv_mid_bwd(ag, u1, du3, p["conv_b_dw"], p["ln_g"], p["ln_b"])
    dag, db_pw1 = _glu_conv_bwd(du1, ag, p["conv_w_dw"])
    dx2, dg4 = _matmul_prenorm_bwd([(dag, 0, dag.shape[1], 0)], p["w_pw1_t"], x2, row(4), dx3, name="conv_dx")
    d_wpw1_t = _weight_grad(dag, h1, name="conv_dw_pw1", out_shape=(1, dag.shape[1], d))
    t0 = exchange.advance(dx2) if exchange else None
    t1 = exchange.submit("conv", [d_wpw1_t, d_wpw2], big) if exchange else None
    dx1, gf0, d_up0, d_down0 = _ffn_bwd(ffn0, dx2, row(2), tied(3, t0, t1), p["w_up_t"][0], p["ffn_w_dw"][0], p["ffn_b_dw"][0],
                                        p["w_down"][0])
    t0 = exchange.advance(dx1) if exchange else None
    t1 = exchange.submit("ffn0", [d_up0, d_down0], big) if exchange else None
    dy_a, dg1 = _postnorm_bwd(y_a, tied(1, t0, t1), dx1, name="attn_post_bwd")
    dmix = _matmul(dy_a, p["w_o_t"], name="attn_dmix", out_dtype=BF16, transposed_w=False)
    d_wo_t = _weight_grad(dy_a, mixed, name="attn_dw_o", out_shape=(1, d, GROUP_WIDTH))
    pieces, d_wqkv_t = [], None
    for g_, d_ in enumerate(DILATIONS):
        if exchange and g_ > 0:
            tok = exchange.advance(dkv)
            if tok is not None:
                rope = (rope[0] + tok[0:1], rope[1], rope[2])
        dq = _attn_bwd_dq(qkv[g_], dmix, mixed, lse, rope, g_, d_)
        dkv = _attn_bwd_dkv(qkv[g_], dmix, mixed, lse, rope, g_, d_)
        for t, (arr, c0) in enumerate(((dq, 0), (dkv, 0), (dkv, GROUP_WIDTH))):
            r0 = (3 * t + g_) * GROUP_WIDTH
            pieces.append((arr, c0, GROUP_WIDTH, r0))
            d_wqkv_t = _weight_grad(arr, h0, name="attn_dw_qkv", a_col0=c0, ka=GROUP_WIDTH, out=d_wqkv_t,
                                    out_shape=(1, p["w_qkv_t"].shape[0], d), row0=r0)
    t0 = exchange.advance(dkv) if exchange else None
    t1 = exchange.submit("attn", [d_wqkv_t, d_wo_t], big) if exchange else None
    grad_x, dg0 = _matmul_prenorm_bwd(pieces, p["w_qkv_t"], x, tied(0, t0, t1), dx1, name="attn_dx")

    grads = dict(
        norm_g=jnp.concatenate([dg0, dg1, gf0["g_pre"], gf0["g_post"], dg4, dg5, gf1["g_pre"], gf1["g_post"]], axis=0),
        w_qkv_t=d_wqkv_t, w_o_t=d_wo_t, w_pw1_t=d_wpw1_t, b_pw1=db_pw1,
        conv_w_dw=d_cwdw[:CONV_KERNEL], conv_b_dw=d_cbdw, ln_g=d_lng, ln_b=d_lnb, w_pw2=d_wpw2, b_pw2=db_pw2,
        w_up_t=[d_up0, d_up1], ffn_w_dw=jnp.stack([gf0["w_dw"], gf1["w_dw"]]),
        ffn_b_dw=jnp.concatenate([gf0["b_dw"], gf1["b_dw"]], axis=0), w_down=[d_down0, d_down1])
    return loss, grad_x, grads


SMALL_AXIS = dict(norm_g=2, conv_b_pw1=1, conv_w_dw=2, conv_b_dw=1, conv_ln_g=1, conv_ln_b=1, conv_b_pw2=1, ffn_w_dw=2)
SMALL = tuple(SMALL_AXIS)
MATMUL_WEIGHTS = dict(attn_w_qkv=True, conv_w_pw1=True, ffn_w_up=True, conv_w_pw2=False, ffn_w_down=False)


def _pack(arrays, cols, row_multiple):
    flat = jnp.concatenate([a.reshape(-1) for a in arrays])
    rows = -(-flat.shape[0] // cols)
    rows = -(-rows // row_multiple) * row_multiple
    return jnp.pad(flat, (0, rows * cols - flat.shape[0])).reshape(rows, cols)


def _unpack(packed, shapes):
    flat = packed.reshape(packed.shape[:-2] + (-1,))
    out, off = [], 0
    for shp in shapes:
        n = math.prod(shp)
        out.append(flat[..., off:off + n].reshape(packed.shape[:-2] + tuple(shp)))
        off += n
    return out


def _join_shards(stacked, axis):
    moved = jnp.moveaxis(stacked, 0, axis)
    shp = moved.shape
    return moved.reshape(shp[:axis] + (shp[axis] * shp[axis + 1],) + shp[axis + 2:])


def _split_shards(whole, axis):
    shp = whole.shape
    cut = whole.reshape(shp[:axis] + (N_DEV, shp[axis] // N_DEV) + shp[axis + 1:])
    return jnp.moveaxis(cut, axis, 0)


def _row_shard(w, transposed):
    t = jnp.swapaxes(w, 1, 2) if transposed else w
    return t.astype(BF16).reshape(-1, t.shape[-1])


def kernel(x, positions, norm_g, attn_w_qkv, attn_w_o, conv_w_pw1, conv_b_pw1, conv_w_dw, conv_b_dw, conv_ln_g, conv_ln_b, conv_w_pw2, conv_b_pw2, ffn_w_up, ffn_w_dw, ffn_b_dw, ffn_w_down, loss_target, m_norm_g, m_attn_w_qkv, m_attn_w_o, m_conv_w_pw1, m_conv_b_pw1, m_conv_w_dw, m_conv_b_dw, m_conv_ln_g, m_conv_ln_b, m_conv_w_pw2, m_conv_b_pw2, m_ffn_w_up, m_ffn_w_dw, m_ffn_b_dw, m_ffn_w_down, v_norm_g, v_attn_w_qkv, v_attn_w_o, v_conv_w_pw1, v_conv_b_pw1, v_conv_w_dw, v_conv_b_dw, v_conv_ln_g, v_conv_ln_b, v_conv_w_pw2, v_conv_b_pw2, v_ffn_w_up, v_ffn_w_dw, v_ffn_b_dw, v_ffn_w_down):
    w = dict(norm_g=norm_g, attn_w_qkv=attn_w_qkv, attn_w_o=attn_w_o, conv_w_pw1=conv_w_pw1, conv_b_pw1=conv_b_pw1,
             conv_w_dw=conv_w_dw, conv_b_dw=conv_b_dw, conv_ln_g=conv_ln_g, conv_ln_b=conv_ln_b, conv_w_pw2=conv_w_pw2,
             conv_b_pw2=conv_b_pw2, ffn_w_up=ffn_w_up, ffn_w_dw=ffn_w_dw, ffn_w_down=ffn_w_down)
    m = dict(norm_g=m_norm_g, attn_w_qkv=m_attn_w_qkv, attn_w_o=m_attn_w_o, conv_w_pw1=m_conv_w_pw1, conv_b_pw1=m_conv_b_pw1,
             conv_w_dw=m_conv_w_dw, conv_b_dw=m_conv_b_dw, conv_ln_g=m_conv_ln_g, conv_ln_b=m_conv_ln_b, conv_w_pw2=m_conv_w_pw2,
             conv_b_pw2=m_conv_b_pw2, ffn_w_up=m_ffn_w_up, ffn_w_dw=m_ffn_w_dw, ffn_w_down=m_ffn_w_down)
    v = dict(norm_g=v_norm_g, attn_w_qkv=v_attn_w_qkv, attn_w_o=v_attn_w_o, conv_w_pw1=v_conv_w_pw1, conv_b_pw1=v_conv_b_pw1,
             conv_w_dw=v_conv_w_dw, conv_b_dw=v_conv_b_dw, conv_ln_g=v_conv_ln_g, conv_ln_b=v_conv_ln_b, conv_w_pw2=v_conv_w_pw2,
             conv_b_pw2=v_conv_b_pw2, ffn_w_up=v_ffn_w_up, ffn_w_dw=v_ffn_w_dw, ffn_w_down=v_ffn_w_down)
    d = x.shape[-1]

    w_qkv_t = _all_gather(_row_shard(attn_w_qkv, True), "gather_w_qkv").reshape(-1, d)
    w_o_t = _all_gather(_row_shard(attn_w_o, True), "gather_w_o").reshape(d, -1)
    small = _all_gather(_pack([w[n] for n in SMALL], 128, 8), "gather_small_weights")
    sm = {n: _join_shards(stacked, SMALL_AXIS[n])
          for n, stacked in zip(SMALL, _unpack(small, [w[n].shape for n in SMALL]))}
    late = {n: t for n, t in MATMUL_WEIGHTS.items() if n != "attn_w_qkv"}
    shares = [_row_shard(w[n], t) for n, t in late.items()]
    rows = [s_.shape[0] for s_ in shares]
    late_share = jnp.concatenate(shares, axis=0)
    send_sems, recv_sems, share_thru, land_thru, tie = _gather_start(late_share)
    me = 4 * lax.axis_index("x") + 2 * lax.axis_index("y") + lax.axis_index("c")

    def late_weights(after):
        big = _gather_wait(send_sems, recv_sems, share_thru, land_thru, after)
        big = lax.dynamic_update_slice(big, late_share[None], (me, 0, 0))
        whole, r0 = {}, 0
        for n, nr in zip(late, rows):
            layers = w[n].shape[0]
            seg = big[:, r0:r0 + nr].reshape(N_DEV, layers, nr // layers, d)
            whole[n] = [seg[:, l_].reshape(-1, d) for l_ in range(layers)]
            r0 += nr
        return dict(w_pw1_t=whole["conv_w_pw1"][0], w_pw2=whole["conv_w_pw2"][0], w_up_t=whole["ffn_w_up"],
                    w_down=whole["ffn_w_down"])

    p = dict(norm_g=sm["norm_g"].reshape(-1, d), w_qkv_t=w_qkv_t, w_o_t=w_o_t, b_pw1=sm["conv_b_pw1"],
             conv_w_dw=sm["conv_w_dw"][0], conv_b_dw=sm["conv_b_dw"], ln_g=sm["conv_ln_g"], ln_b=sm["conv_ln_b"],
             b_pw2=sm["conv_b_pw2"], ffn_w_dw=sm["ffn_w_dw"], ffn_b_dw=[ffn_b_dw[0:1], ffn_b_dw[1:2]])

    exchange = _GradExchange()
    loss, grad_x, g = _local_step(x[0], positions.reshape(-1, 1), loss_target[0], p, tie, late_weights, exchange)
    loss = lax.psum(loss[0, 0], ("x", "y", "c"))
    gsmall = dict(norm_g=g["norm_g"].reshape(norm_g.shape[0], 4, -1), conv_b_pw1=g["b_pw1"], conv_w_dw=g["conv_w_dw"][None],
                  conv_b_dw=g["conv_b_dw"], conv_ln_g=g["ln_g"], conv_ln_b=g["ln_b"], conv_b_pw2=g["b_pw2"], ffn_w_dw=g["ffn_w_dw"])
    small_contrib = jnp.concatenate([_split_shards(gsmall[n], SMALL_AXIS[n]).reshape(N_DEV, -1) for n in SMALL], axis=1)
    srows = small.shape[1]
    small_contrib = jnp.pad(small_contrib, ((0, 0), (0, srows * 128 - small_contrib.shape[1]))).reshape(1, N_DEV, srows, 128)
    exchange.advance(grad_x)
    small_sums = _rs_chips([_rs_pair_add(small_contrib, _rs_sibling([small_contrib])[0], exchange.core, F32)])[0]

    outs = {}

    def update(n, reduced):
        gsum = jnp.swapaxes(reduced, 1, 2) if n == "attn_w_o" or MATMUL_WEIGHTS.get(n) else reduced
        outs[n] = (gsum, *_adamw(gsum, w[n], m[n], v[n], "adamw"))

    (s_up1, s_down1), (s_pw1, s_pw2), (s_up0, s_down0) = exchange.results()[:3]
    update("conv_w_pw1", s_pw1)
    update("conv_w_pw2", s_pw2)
    update("ffn_w_up", jnp.concatenate([s_up0, s_up1], axis=0))
    update("ffn_w_down", jnp.concatenate([s_down0, s_down1], axis=0))
    exchange.advance(outs["ffn_w_down"][1])
    s_qkv, s_wo = exchange.results()[3]
    update("attn_w_qkv", s_qkv)
    update("attn_w_o", s_wo)
    sshapes = [w[n].shape for n in SMALL]
    souts = _sum_adamw(small_sums[0], *[_pack([t[n] for n in SMALL], 128, 8) for t in (w, m, v)], name="sum_adamw_small")
    for n, vals in zip(SMALL, zip(*[_unpack(o, sshapes) for o in souts])):
        outs[n] = vals
    bparts = _all_gather(_pack([g["ffn_b_dw"]], 128, 8), "gather_bias_grads")
    bouts = _sum_adamw(bparts, *[_pack([t], 128, 8) for t in (ffn_b_dw, m_ffn_b_dw, v_ffn_b_dw)], name="sum_adamw_bias")
    outs["ffn_b_dw"] = tuple(_unpack(o, [ffn_b_dw.shape])[0] for o in bouts)

    order = ("norm_g", "attn_w_qkv", "attn_w_o", "conv_w_pw1", "conv_b_pw1", "conv_w_dw", "conv_b_dw", "conv_ln_g",
             "conv_ln_b", "conv_w_pw2", "conv_b_pw2", "ffn_w_up", "ffn_w_dw", "ffn_b_dw", "ffn_w_down")
    return (loss, grad_x[None], *[outs[n][0] for n in order], *[outs[n][1] for n in order],
            *[outs[n][2] for n in order], *[outs[n][3] for n in order])
```

```python
import functools
import math

import numpy as np
import jax
import jax.numpy as jnp
from jax import lax
from jax.experimental import pallas as pl
from jax.experimental.pallas import tpu as pltpu

F32 = jnp.float32
BF16 = jnp.bfloat16
EPS = 1e-6
N_DEV = 8
HEAD_DIM = 64
GROUP_WIDTH = 512
DILATIONS = (1, 4, 16)
SPAN = 128
ROT_DIM = 16
ROPE_THETA = 500000.0
CONV_KERNEL = 31
CONV_HALO = 32
FFN_CONV = 3
ADAM_LR, ADAM_B1, ADAM_B2, ADAM_EPS, ADAM_WD, ADAM_STEP = 0.001, 0.9, 0.999, 1e-08, 0.01, 10
VMEM_LIMIT_BYTES = 56 * 1024 * 1024
MESH = pl.DeviceIdType.MESH
ANY = pl.BlockSpec(memory_space=pl.ANY)
NT = (((1,), (1,)), ((), ()))
TN = (((0,), (0,)), ((), ()))


def _params(*sem):
    return pltpu.CompilerParams(dimension_semantics=sem, vmem_limit_bytes=VMEM_LIMIT_BYTES)


def _sigmoid(v):
    return 1.0 / (1.0 + jnp.exp(-v))


def _full(shape):
    return pl.BlockSpec(shape, lambda *_: (0,) * len(shape))


def _rows(tm, width):
    return pl.BlockSpec((tm, width), lambda i, *_: (i, 0))


def _tile(n, *multiples_of):
    for t in (1408, 1024, 512, 384, 256, 128):
        if n % t == 0 and all(o % t == 0 for o in multiples_of):
            return t
    raise ValueError((n, multiples_of))


def _all_gather(shard, name):
    r, c_ = shard.shape

    def body(x_ref, out_ref, send_sems, recv_sems, local_sem):
        x, y, c = lax.axis_index("x"), lax.axis_index("y"), lax.axis_index("c")
        me, sibling = (x, y, c), (x, y, 1 - c)
        chips = [(1 - x, y), (x, 1 - y), (1 - x, 1 - y)]

        def rows(px, py, pc):
            return out_ref.at[4 * px + 2 * py + pc]

        def copy(k, block, to, src=None):
            return pltpu.make_async_remote_copy(
                src_ref=rows(*block) if src is None else src, dst_ref=rows(*block),
                send_sem=send_sems.at[k], recv_sem=recv_sems.at[k], device_id=to, device_id_type=MESH)

        mine = pltpu.make_async_copy(x_ref, rows(*me), local_sem)
        mine.start()
        first = [copy(0, me, sibling, src=x_ref)]
        first += [copy(1 + j, me, (*chip, c), src=x_ref) for j, chip in enumerate(chips)]
        for cp in first:
            cp.start()
        passed = [copy(4 + j, (*chip, c), sibling) for j, chip in enumerate(chips)]
        for j, chip in enumerate(chips):
            copy(1 + j, (*chip, c), me).wait_recv()
            passed[j].start()
        copy(0, sibling, me).wait_recv()
        for j, chip in enumerate(chips):
            copy(4 + j, (*chip, 1 - c), me).wait_recv()
        for cp in first + passed:
            cp.wait_send()
        mine.wait()

    return pl.pallas_call(
        body, name=name, out_shape=jax.ShapeDtypeStruct((N_DEV, r, c_), shard.dtype),
        in_specs=[ANY], out_specs=ANY,
        scratch_shapes=[pltpu.SemaphoreType.DMA((7,)), pltpu.SemaphoreType.DMA((7,)), pltpu.SemaphoreType.DMA],
    )(shard)


HBM = pl.BlockSpec(memory_space=pltpu.HBM)
SEM = pl.BlockSpec(memory_space=pltpu.SEMAPHORE)
SIDE_EFFECT = pltpu.CompilerParams(has_side_effects=pltpu.SideEffectType.DATAFLOW_SIDE_EFFECTING)


def _gather_start(shard):
    r, c_ = shard.shape

    def body(x_ref, land_ref, send_sems, recv_sems, x_thru, land_thru, token):
        x, y, c = lax.axis_index("x"), lax.axis_index("y"), lax.axis_index("c")
        me = 4 * x + 2 * y + c
        for k in range(1, N_DEV):
            peer = (1 - x if k & 4 else x, 1 - y if k & 2 else y, 1 - c if k & 1 else c)
            pltpu.make_async_remote_copy(src_ref=x_ref, dst_ref=land_ref.at[me], send_sem=send_sems.at[k - 1],
                                         recv_sem=recv_sems.at[k - 1], device_id=peer, device_id_type=MESH).start()
        token[...] = jnp.zeros_like(token)

    land = pltpu.with_memory_space_constraint(lax.empty((N_DEV, r, c_), shard.dtype), pltpu.HBM)
    return pl.pallas_call(
        body, name="gather_late_weights_start",
        out_shape=(pltpu.SemaphoreType.DMA((N_DEV - 1,)), pltpu.SemaphoreType.DMA((N_DEV - 1,)),
                   pltpu.HBM(shard.shape, shard.dtype), pltpu.HBM((N_DEV, r, c_), shard.dtype),
                   jax.ShapeDtypeStruct((8, 128), F32)),
        in_specs=(HBM, HBM), out_specs=(SEM, SEM, HBM, HBM, pl.BlockSpec(memory_space=pltpu.VMEM)),
        input_output_aliases={0: 2, 1: 3}, compiler_params=SIDE_EFFECT,
    )(pltpu.with_memory_space_constraint(shard, pltpu.HBM), land)


def _gather_wait(send_sems, recv_sems, shard_thru, land_thru, after):
    def body(x_ref, land_ref, send_sems, recv_sems, after_ref, x_dead, got_ref):
        x, y, c = lax.axis_index("x"), lax.axis_index("y"), lax.axis_index("c")
        for k in range(N_DEV - 1):
            copy = pltpu.make_async_remote_copy(src_ref=x_ref, dst_ref=land_ref.at[0], send_sem=send_sems.at[k],
                                                recv_sem=recv_sems.at[k], device_id=(x, y, c), device_id_type=MESH)
            copy.wait_send()
            copy.wait_recv()

    return pl.pallas_call(
        body, name="gather_late_weights_wait",
        out_shape=(pltpu.HBM(shard_thru.shape, shard_thru.dtype), pltpu.HBM(land_thru.shape, land_thru.dtype)),
        in_specs=(HBM, HBM, SEM, SEM, ANY), out_specs=(HBM, HBM), input_output_aliases={0: 0, 1: 1},
        compiler_params=SIDE_EFFECT,
    )(shard_thru, land_thru, send_sems, recv_sems, after)[1]


def _hbm(a):
    return pltpu.with_memory_space_constraint(a, pltpu.HBM)


def _exchange_start(name, arrays, lands, plan, ncopies):
    n = len(arrays)

    def body(*refs):
        send_sems, recv_sems, token = refs[2 * n], refs[2 * n + 1], refs[-1]
        x, y, c = lax.axis_index("x"), lax.axis_index("y"), lax.axis_index("c")
        for k, (src, dst, peer) in enumerate(plan(x, y, c, refs[:n], refs[n:2 * n])):
            pltpu.make_async_remote_copy(src_ref=src, dst_ref=dst, send_sem=send_sems.at[k], recv_sem=recv_sems.at[k],
                                         device_id=peer, device_id_type=MESH).start()
        token[...] = jnp.zeros_like(token)

    both = list(arrays) + list(lands)
    outs = pl.pallas_call(
        body, name=name,
        out_shape=(pltpu.SemaphoreType.DMA((ncopies,)), pltpu.SemaphoreType.DMA((ncopies,)),
                   *[pltpu.HBM(a.shape, a.dtype) for a in both], jax.ShapeDtypeStruct((8, 128), F32)),
        in_specs=(HBM,) * (2 * n), out_specs=(SEM, SEM) + (HBM,) * (2 * n) + (pl.BlockSpec(memory_space=pltpu.VMEM),),
        input_output_aliases={i: 2 + i for i in range(2 * n)}, compiler_params=SIDE_EFFECT,
    )(*[_hbm(a) for a in both])
    return outs[0], outs[1], list(outs[2:2 + n]), list(outs[2 + n:2 + 2 * n]), outs[-1]


def _exchange_wait(name, send_sems, recv_sems, arrays, lands, plan, after):
    n = len(arrays)

    def body(*refs):
        send_sems, recv_sems = refs[2 * n], refs[2 * n + 1]
        x, y, c = lax.axis_index("x"), lax.axis_index("y"), lax.axis_index("c")
        for k, (src, dst, peer) in enumerate(plan(x, y, c, refs[:n], refs[n:2 * n])):
            copy = pltpu.make_async_remote_copy(src_ref=src, dst_ref=dst, send_sem=send_sems.at[k], recv_sem=recv_sems.at[k],
                                                device_id=peer, device_id_type=MESH)
            copy.wait_send()
            copy.wait_recv()

    both = list(arrays) + list(lands)
    outs = pl.pallas_call(
        body, name=name, out_shape=tuple(pltpu.HBM(a.shape, a.dtype) for a in both),
        in_specs=(HBM,) * (2 * n) + (SEM, SEM, ANY), out_specs=(HBM,) * (2 * n),
        input_output_aliases={i: i for i in range(2 * n)}, compiler_params=SIDE_EFFECT,
    )(*both, send_sems, recv_sems, after)
    return list(outs[:n]), list(outs[n:])


def _sibling_plan(x, y, c, g_refs, land_refs):
    return [(g.at[:, 2 * q + (1 - c)], o.at[:, q], (x, y, 1 - c)) for g, o in zip(g_refs, land_refs) for q in range(4)]


def _chips_plan(x, y, c, p_refs, land_refs):
    chips = [(1 - x, y), (x, 1 - y), (1 - x, 1 - y)]
    return [(p_.at[:, 2 * qx + qy], o.at[:, 2 * x + y], (qx, qy, c)) for p_, o in zip(p_refs, land_refs) for qx, qy in chips]


class _GradExchange:
    def __init__(self):
        self.core = lax.axis_index("c").astype(jnp.int32).reshape(1)
        self.chip = 2 * lax.axis_index("x") + lax.axis_index("y")
        self.groups = []

    def submit(self, tag, arrays, dtypes):
        arrays = [a.reshape(a.shape[0], N_DEV, a.shape[1] // N_DEV, a.shape[2]) for a in arrays]
        lands = [lax.empty((a.shape[0], 4) + a.shape[2:], a.dtype) for a in arrays]
        send, recv, arrays, lands, token = _exchange_start(f"rs_pair_start_{tag}", arrays, lands, _sibling_plan, 4 * len(arrays))
        self.groups.append(dict(tag=tag, stage=1, sems=(send, recv), arrays=arrays, lands=lands, dtypes=dtypes))
        return token

    def advance(self, after):
        token = None
        for g in self.groups:
            if g["stage"] == 1:
                arrays, got = _exchange_wait(f"rs_pair_wait_{g['tag']}", *g["sems"], g["arrays"], g["lands"], _sibling_plan, after)
                parts = [_rs_pair_add(a, b, self.core, dt) for a, b, dt in zip(arrays, got, g["dtypes"])]
                lands = [lax.empty(p_.shape, p_.dtype) for p_ in parts]
                send, recv, parts, lands, tok = _exchange_start(f"rs_chip_start_{g['tag']}", parts, lands, _chips_plan, 3 * len(parts))
                g.update(stage=2, sems=(send, recv), arrays=parts, lands=lands)
                token = tok if token is None else token + tok
            elif g["stage"] == 2:
                parts, lands = _exchange_wait(f"rs_chip_wait_{g['tag']}", *g["sems"], g["arrays"], g["lands"], _chips_plan, after)
                sums = []
                for p_, land in zip(parts, lands):
                    l, _, r, c_ = p_.shape
                    own = lax.dynamic_slice(p_, (0, self.chip, 0, 0), (l, 1, r, c_))
                    sums.append(_sum_parts(lax.dynamic_update_slice(land, own, (0, self.chip, 0, 0)), "sum_chips"))
                g.update(stage=3, sums=sums)
        return token

    def results(self):
        return [g.get("sums") for g in self.groups]


def _with_rows(g, n):
    return jax.ShapeDtypeStruct((g.shape[0], n) + tuple(g.shape[2:]), g.dtype)


def _rs_sibling(gs):
    n = len(gs)

    def body(*refs):
        g_refs, o_refs, (send_sems, recv_sems) = refs[:n], refs[n:2 * n], refs[2 * n:]
        x, y, c = lax.axis_index("x"), lax.axis_index("y"), lax.axis_index("c")
        copies = [pltpu.make_async_remote_copy(
            src_ref=g_refs[w].at[:, 2 * q + (1 - c)], dst_ref=o_refs[w].at[:, q], send_sem=send_sems.at[4 * w + q],
            recv_sem=recv_sems.at[4 * w + q], device_id=(x, y, 1 - c), device_id_type=MESH)
            for w in range(n) for q in range(4)]
        for cp in copies:
            cp.start()
        for cp in copies:
            cp.wait_recv()
        for cp in copies:
            cp.wait_send()

    return pl.pallas_call(
        body, name="rs_sibling", out_shape=[_with_rows(g, 4) for g in gs],
        in_specs=[ANY] * n, out_specs=[ANY] * n,
        scratch_shapes=[pltpu.SemaphoreType.DMA((4 * n,)), pltpu.SemaphoreType.DMA((4 * n,))],
    )(*gs)


def _rs_pair_add(g, got, core, out_dtype):
    l, _, r, c_ = g.shape

    def body(core_ref, g_ref, got_ref, o_ref):
        o_ref[...] = (g_ref[...].astype(F32) + got_ref[...].astype(F32)).astype(out_dtype)

    blk = (None, None, r, c_)
    return pl.pallas_call(
        body, name="rs_pair_add", out_shape=jax.ShapeDtypeStruct((l, 4, r, c_), out_dtype),
        grid_spec=pltpu.PrefetchScalarGridSpec(
            num_scalar_prefetch=1, grid=(l, 4),
            in_specs=[pl.BlockSpec(blk, lambda i, q, core_ref: (i, 2 * q + core_ref[0], 0, 0)),
                      pl.BlockSpec(blk, lambda i, q, core_ref: (i, q, 0, 0))],
            out_specs=pl.BlockSpec(blk, lambda i, q, core_ref: (i, q, 0, 0))),
        compiler_params=_params("parallel", "parallel"),
    )(core, g, got)


def _rs_chips(parts):
    n = len(parts)

    def body(*refs):
        p_refs, o_refs, (send_sems, recv_sems, local_sems) = refs[:n], refs[n:2 * n], refs[2 * n:]
        x, y, c = lax.axis_index("x"), lax.axis_index("y"), lax.axis_index("c")
        my_chip = 2 * x + y
        chips = [(1 - x, y), (x, 1 - y), (1 - x, 1 - y)]
        local = [pltpu.make_async_copy(p_refs[w].at[:, my_chip], o_refs[w].at[:, my_chip], local_sems.at[w]) for w in range(n)]
        for cp in local:
            cp.start()
        copies = [pltpu.make_async_remote_copy(
            src_ref=p_refs[w].at[:, 2 * qx + qy], dst_ref=o_refs[w].at[:, my_chip], send_sem=send_sems.at[3 * w + k],
            recv_sem=recv_sems.at[3 * w + k], device_id=(qx, qy, c), device_id_type=MESH)
            for w in range(n) for k, (qx, qy) in enumerate(chips)]
        for cp in copies:
            cp.start()
        for cp in copies:
            cp.wait_recv()
        for cp in copies:
            cp.wait_send()
        for cp in local:
            cp.wait()

    return pl.pallas_call(
        body, name="rs_chips", out_shape=[jax.ShapeDtypeStruct(p.shape, p.dtype) for p in parts],
        in_specs=[ANY] * n, out_specs=[ANY] * n,
        scratch_shapes=[pltpu.SemaphoreType.DMA((3 * n,)), pltpu.SemaphoreType.DMA((3 * n,)), pltpu.SemaphoreType.DMA((n,))],
    )(*parts)


def _sum_parts(parts, name):
    l, n, r, c_ = parts.shape

    def body(p_ref, o_ref):
        g = p_ref[0].astype(F32)
        for s in range(1, n):
            g = g + p_ref[s].astype(F32)
        o_ref[...] = g

    return pl.pallas_call(
        body, name=name, out_shape=jax.ShapeDtypeStruct((l, r, c_), F32), grid=(l,),
        in_specs=[pl.BlockSpec((None, n, r, c_), lambda i: (i, 0, 0, 0))],
        out_specs=pl.BlockSpec((None, r, c_), lambda i: (i, 0, 0)), compiler_params=_params("parallel"),
    )(parts)


def _adamw_math(w, g, m, v):
    m = ADAM_B1 * m + (1.0 - ADAM_B1) * g
    v = ADAM_B2 * v + (1.0 - ADAM_B2) * (g * g)
    m_hat = m / (1.0 - ADAM_B1 ** ADAM_STEP)
    v_hat = v / (1.0 - ADAM_B2 ** ADAM_STEP)
    delta = -ADAM_LR * (m_hat / (jnp.sqrt(v_hat) + ADAM_EPS) + ADAM_WD * w)
    return delta, m, v


def _adamw(g, w, m, v, name):
    l, k, n = w.shape
    tk = 256 if k % 256 == 0 else k

    def body(g_ref, w_ref, m_ref, v_ref, d_ref, nm_ref, nv_ref):
        d_ref[...], nm_ref[...], nv_ref[...] = _adamw_math(w_ref[...], g_ref[...], m_ref[...], v_ref[...])

    spec = pl.BlockSpec((None, tk, n), lambda i, j: (i, j, 0))
    return pl.pallas_call(
        body, name=name, out_shape=[jax.ShapeDtypeStruct((l, k, n), F32)] * 3, grid=(l, k // tk),
        in_specs=[spec] * 4, out_specs=[spec] * 3, compiler_params=_params("parallel", "parallel"),
    )(g, w, m, v)


def _sum_adamw(parts, w, m, v, name):
    n, r, c_ = parts.shape

    def body(p_ref, w_ref, m_ref, v_ref, g_ref, d_ref, nm_ref, nv_ref):
        g = p_ref[0]
        for s in range(1, n):
            g = g + p_ref[s]
        g_ref[...] = g
        d_ref[...], nm_ref[...], nv_ref[...] = _adamw_math(w_ref[...], g, m_ref[...], v_ref[...])

    return pl.pallas_call(
        body, name=name, out_shape=[jax.ShapeDtypeStruct((r, c_), F32)] * 4, grid=(1,),
        in_specs=[_full((n, r, c_))] + [_full((r, c_))] * 3, out_specs=[_full((r, c_))] * 4,
        compiler_params=_params("arbitrary"),
    )(parts, w, m, v)


def _rope_tables(pos_col, freq_row):
    s = pos_col.shape[0]
    tm = min(1024, s)

    def body(p_ref, f_ref, c_ref, su_ref, sd_ref):
        ang = p_ref[...].astype(F32) * f_ref[...]
        lane = lax.broadcasted_iota(jnp.int32, ang.shape, 1) & (HEAD_DIM - 1)
        cs, sn = jnp.cos(ang), jnp.sin(ang)
        c_ref[...] = jnp.where(lane < ROT_DIM, cs, 1.0)
        su_ref[...] = jnp.where((lane >= ROT_DIM // 2) & (lane < ROT_DIM), sn, 0.0)
        sd_ref[...] = jnp.where(lane < ROT_DIM // 2, -sn, 0.0)

    return pl.pallas_call(
        body, name="rope_tables", out_shape=[jax.ShapeDtypeStruct((s, 128), F32)] * 3, grid=(s // tm,),
        in_specs=[pl.BlockSpec((tm, 1), lambda i: (i, 0)), _full((1, 128))],
        out_specs=[_rows(tm, 128)] * 3, compiler_params=_params("parallel"),
    )(pos_col, freq_row)


def _rope_apply(t, cos, sin_up, sin_dn):
    w = t.shape[1]
    return t * cos + pltpu.roll(t, 8, 1) * sin_up + pltpu.roll(t, w - 8, 1) * sin_dn


def _rope_transpose(dr, cos, sin_up, sin_dn):
    w = dr.shape[1]
    return dr * cos + pltpu.roll(dr * sin_up, w - 8, 1) + pltpu.roll(dr * sin_dn, 8, 1)


def _norm_matmul(x, g, wt, *, tn, name, bias=None, rope=None, rope_blocks=0, tm=512):
    s, d = x.shape
    n = wt.shape[0]
    tm = min(tm, s)

    def body(*refs):
        x_ref, g_ref, w_ref = refs[:3]
        k = 3
        b_ref = None
        if bias is not None:
            b_ref = refs[k]
            k += 1
        if rope is not None:
            c_ref, su_ref, sd_ref = refs[k:k + 3]
            k += 3
        h_ref, o_ref = refs[k:k + 2]
        j = pl.program_id(1)

        @pl.when(j == 0)
        def _():
            xv = x_ref[...]
            r = lax.rsqrt(jnp.mean(xv * xv, axis=-1, keepdims=True) + EPS)
            h_ref[...] = (xv * r * g_ref[...]).astype(BF16)

        acc = lax.dot_general(h_ref[...], w_ref[...], NT, preferred_element_type=F32)
        if b_ref is not None:
            acc = acc + b_ref[...]
        if rope is None:
            o_ref[...] = acc.astype(BF16)
        else:
            @pl.when(j < rope_blocks)
            def _():
                reps = tn // 128
                o_ref[...] = _rope_apply(acc, jnp.tile(c_ref[...], (1, reps)), jnp.tile(su_ref[...], (1, reps)),
                                         jnp.tile(sd_ref[...], (1, reps))).astype(BF16)

            @pl.when(j >= rope_blocks)
            def _():
                o_ref[...] = acc.astype(BF16)

    in_specs = [_rows(tm, d), _full((1, d)), pl.BlockSpec((tn, d), lambda i, j: (j, 0))]
    args = [x, g, wt]
    if bias is not None:
        in_specs.append(pl.BlockSpec((1, tn), lambda i, j: (0, j)))
        args.append(bias)
    if rope is not None:
        in_specs += [_rows(tm, 128)] * 3
        args += list(rope)
    return pl.pallas_call(
        body, name=name,
        out_shape=[jax.ShapeDtypeStruct((s, d), BF16), jax.ShapeDtypeStruct((s, n), BF16)],
        grid=(s // tm, n // tn), in_specs=in_specs,
        out_specs=[_rows(tm, d), pl.BlockSpec((tm, tn), lambda i, j: (i, j))],
        compiler_params=_params("parallel", "arbitrary"),
    )(*args)


def _class_major(tm, dil):
    p = np.zeros((tm, tm), np.float32)
    per = tm // dil
    for r in range(dil):
        for j in range(per):
            p[r * per + j, j * dil + r] = 1.0
    return jnp.asarray(p, dtype=BF16)


def _qkv_proj(x, g, wt, rope, tm=512):
    s, d = x.shape
    n = wt.shape[0]
    gw3 = 3 * GROUP_WIDTH
    tm = min(tm, s)
    assert n == 3 * gw3

    def body(x_ref, g_ref, w_ref, c_ref, su_ref, sd_ref, p1_ref, p2_ref, h_ref, o0_ref, o1_ref, o2_ref):
        j = pl.program_id(1)

        @pl.when(j == 0)
        def _():
            xv = x_ref[...]
            r = lax.rsqrt(jnp.mean(xv * xv, axis=-1, keepdims=True) + EPS)
            h_ref[...] = (xv * r * g_ref[...]).astype(BF16)

        acc = lax.dot_general(h_ref[...], w_ref[...], NT, preferred_element_type=F32)

        def store(y):
            yb = y.astype(BF16)
            o0_ref[:, pl.ds(pl.multiple_of(j * GROUP_WIDTH, GROUP_WIDTH), GROUP_WIDTH)] = yb[:, :GROUP_WIDTH]
            for grp, o_ref, p_ref in ((1, o1_ref, p1_ref), (2, o2_ref, p2_ref)):
                dil = DILATIONS[grp]
                per = tm // dil
                yp = jnp.dot(p_ref[...], yb[:, grp * GROUP_WIDTH:(grp + 1) * GROUP_WIDTH],
                             preferred_element_type=F32).astype(BF16)
                for r in range(dil):
                    col = pl.multiple_of(r * gw3 + j * GROUP_WIDTH, GROUP_WIDTH)
                    o_ref[:, pl.ds(col, GROUP_WIDTH)] = yp[r * per:(r + 1) * per, :]

        @pl.when(j < 2)
        def _():
            reps = gw3 // 128
            store(_rope_apply(acc, jnp.tile(c_ref[...], (1, reps)), jnp.tile(su_ref[...], (1, reps)),
                              jnp.tile(sd_ref[...], (1, reps))))

        @pl.when(j == 2)
        def _():
            store(acc)

    outs = [jax.ShapeDtypeStruct((s, d), BF16)] + [jax.ShapeDtypeStruct((s // dl, dl * gw3), BF16) for dl in DILATIONS]
    out_specs = [_rows(tm, d)] + [_rows(tm // dl, dl * gw3) for dl in DILATIONS]
    return pl.pallas_call(
        body, name="attn_qkv", out_shape=outs, grid=(s // tm, 3),
        in_specs=[_rows(tm, d), _full((1, d)), pl.BlockSpec((gw3, d), lambda i, j: (j, 0))] + [_rows(tm, 128)] * 3
        + [_full((tm, tm))] * 2,
        out_specs=out_specs, compiler_params=_params("parallel", "arbitrary"),
    )(x, g, wt, *rope, _class_major(tm, DILATIONS[1]), _class_major(tm, DILATIONS[2]))


def _head_masks(rows=SPAN):
    lane = lax.broadcasted_iota(jnp.int32, (rows, 128), 1)
    masks = [lane < HEAD_DIM, lane >= HEAD_DIM]
    lane1 = lax.broadcasted_iota(jnp.int32, (1, 128), 1)
    keep = [jnp.where(lane1 < HEAD_DIM, 1.0, 0.0).astype(BF16), jnp.where(lane1 >= HEAD_DIM, 1.0, 0.0).astype(BF16)]
    return masks, keep


def _attn_fwd(qv, grp, dil):
    l = qv.shape[0]
    s = l * dil
    nb = l // SPAN

    def body(q_ref, kp_ref, kc_ref, vp_ref, vc_ref, o_ref, l_ref):
        b = pl.program_id(1)
        row = lax.broadcasted_iota(jnp.int32, (SPAN, 2 * SPAN), 0)
        col = lax.broadcasted_iota(jnp.int32, (SPAN, 2 * SPAN), 1)
        no_prev = jnp.where(b > 0, 0, 4 * SPAN)
        valid = ((col < SPAN) & (col >= row + no_prev)) | ((col >= SPAN) & (col - SPAN <= row))
        masks, keep = _head_masks()
        for p in range(GROUP_WIDTH // 128):
            sl = slice(p * 128, (p + 1) * 128)
            qp = q_ref[:, sl]
            kk = jnp.concatenate([kp_ref[:, sl], kc_ref[:, sl]], axis=0)
            vv = jnp.concatenate([vp_ref[:, sl], vc_ref[:, sl]], axis=0)
            outs, lses = [], []
            for h in range(2):
                sc = lax.dot_general(qp * keep[h], kk, NT, preferred_element_type=F32) * (HEAD_DIM ** -0.5)
                sc = jnp.where(valid, sc, -1e30)
                mx = jnp.max(sc, axis=-1, keepdims=True)
                pe = jnp.exp(sc - mx)
                den = jnp.sum(pe, axis=-1, keepdims=True)
                pv = jnp.dot(pe.astype(BF16), vv, preferred_element_type=F32)
                outs.append(pv / den)
                lses.append(jnp.broadcast_to(mx + jnp.log(den), (SPAN, 128)))
            o_ref[:, sl] = jnp.where(masks[0], outs[0], outs[1])
            l_ref[:, sl] = jnp.where(masks[0], lses[0], lses[1])

    blk = (SPAN, GROUP_WIDTH)
    cur = lambda t: pl.BlockSpec(blk, lambda r, b: (b, r * 3 + t))
    prev = lambda t: pl.BlockSpec(blk, lambda r, b: (jnp.maximum(b - 1, 0), r * 3 + t))
    out = pl.BlockSpec(blk, lambda r, b: (b, r))
    o, lse = pl.pallas_call(
        body, name=f"attn_fwd_g{grp}", out_shape=[jax.ShapeDtypeStruct((l, dil * GROUP_WIDTH), F32)] * 2,
        grid=(dil, nb), in_specs=[cur(0), prev(1), cur(1), prev(2), cur(2)], out_specs=[out, out],
        compiler_params=_params("parallel", "arbitrary"),
    )(qv, qv, qv, qv, qv)
    return o.reshape(s, GROUP_WIDTH), lse.reshape(s, GROUP_WIDTH)


def _resnorm_store(y, x_ref, g_ref, y_ref, xo_ref):
    r = lax.rsqrt(jnp.mean(y * y, axis=-1, keepdims=True) + EPS)
    y_ref[...] = y
    xo_ref[...] = x_ref[...] + y * r * g_ref[...]


def _mix_wo(os_, ls_, wot, x, g, tm=256):
    s, d = x.shape
    gw = wot.shape[1]
    tm = min(tm, s)

    def body(o0, o1, o2, l0, l1, l2, w_ref, x_ref, g_ref, y_ref, xo_ref, mixed_ref, lse_ref):
        a0, a1, a2 = l0[...], l1[...], l2[...]
        mx = jnp.maximum(jnp.maximum(a0, a1), a2)
        e0, e1, e2 = jnp.exp(a0 - mx), jnp.exp(a1 - mx), jnp.exp(a2 - mx)
        den = e0 + e1 + e2
        mixed = (e0 / den) * o0[...] + (e1 / den) * o1[...] + (e2 / den) * o2[...]
        mixed_ref[...] = mixed.astype(BF16)
        lse_ref[...] = mx + jnp.log(den)
        y = lax.dot_general(mixed.astype(BF16), w_ref[...], NT, preferred_element_type=F32)
        _resnorm_store(y, x_ref, g_ref, y_ref, xo_ref)

    return pl.pallas_call(
        body, name="mix_wo",
        out_shape=[jax.ShapeDtypeStruct((s, d), F32), jax.ShapeDtypeStruct((s, d), F32),
                   jax.ShapeDtypeStruct((s, gw), BF16), jax.ShapeDtypeStruct((s, gw), F32)],
        grid=(s // tm,), in_specs=[_rows(tm, gw)] * 6 + [_full((d, gw)), _rows(tm, d), _full((1, d))],
        out_specs=[_rows(tm, d), _rows(tm, d), _rows(tm, gw), _rows(tm, gw)],
        compiler_params=_params("parallel"),
    )(*os_, *ls_, wot, x, g)


def _matmul_resnorm(a, w, x, g, *, name, bias=None, tm=512):
    s, k = a.shape
    d = w.shape[1]
    tm = min(tm, s)

    def body(*refs):
        a_ref, w_ref = refs[:2]
        b_ref = refs[2] if bias is not None else None
        x_ref, g_ref, y_ref, xo_ref = refs[-4:]
        y = jnp.dot(a_ref[...], w_ref[...], preferred_element_type=F32)
        if b_ref is not None:
            y = y + b_ref[...]
        _resnorm_store(y, x_ref, g_ref, y_ref, xo_ref)

    in_specs = [_rows(tm, k), _full((k, d))] + ([_full((1, d))] if bias is not None else []) + [_rows(tm, d), _full((1, d))]
    args = [a, w] + ([bias] if bias is not None else []) + [x, g]
    return pl.pallas_call(
        body, name=name, out_shape=[jax.ShapeDtypeStruct((s, d), F32)] * 2, grid=(s // tm,),
        in_specs=in_specs, out_specs=[_rows(tm, d)] * 2, compiler_params=_params("parallel"),
    )(*args)


def _conv3_taps(z, halo, first):
    row = lax.broadcasted_iota(jnp.int32, z.shape, 0)
    halo = halo * jnp.where(first, 0.0, 1.0)
    h6, h7 = halo[6:7, :], halo[7:8, :]
    z1 = jnp.where(row == 0, h7, pltpu.roll(z, 1, 0))
    z2 = jnp.where(row == 0, h6, jnp.where(row == 1, h7, pltpu.roll(z, 2, 0)))
    return z2, z1


def _ffn_cols(f):
    return _tile(f)


def _lane_chunks(width, fn):
    def step(k, carry):
        fn(pl.ds(pl.multiple_of(k * 128, 128), 128))
        return carry

    lax.fori_loop(0, width // 128, step, 0)


def _ffn_act(z, w_dw, b_dw, tm=256):
    s, f2 = z.shape
    f = f2 // 2
    tm = min(tm, s)
    tc = _ffn_cols(f)
    nfc = f // tc

    def body(zu, zg, hu, hg, wu, wg, bu, bg, o_ref):
        first = pl.program_id(0) == 0

        def chunk(cs):
            def conv(z_ref, h_ref, w_ref, b_ref):
                zc = z_ref[:, cs].astype(F32)
                z2, z1 = _conv3_taps(zc, h_ref[:, cs].astype(F32), first)
                return w_ref[0:1, cs] * z2 + w_ref[1:2, cs] * z1 + w_ref[2:3, cs] * zc + b_ref[:, cs]

            up, gate = conv(zu, hu, wu, bu), conv(zg, hg, wg, bg)
            o_ref[:, cs] = (gate * _sigmoid(gate) * up).astype(BF16)

        _lane_chunks(tc, chunk)

    hb = tm // 8
    tile = lambda off: pl.BlockSpec((tm, tc), lambda i, j: (i, off + j))
    halo = lambda off: pl.BlockSpec((8, tc), lambda i, j: (jnp.maximum(i * hb - 1, 0), off + j))
    prm = lambda rows, off: pl.BlockSpec((rows, tc), lambda i, j: (0, off + j))
    return pl.pallas_call(
        body, name="ffn_act", out_shape=jax.ShapeDtypeStruct((s, f), BF16), grid=(s // tm, nfc),
        in_specs=[tile(0), tile(nfc), halo(0), halo(nfc), prm(FFN_CONV, 0), prm(FFN_CONV, nfc), prm(1, 0), prm(1, nfc)],
        out_specs=pl.BlockSpec((tm, tc), lambda i, j: (i, j)), compiler_params=_params("parallel", "parallel"),
    )(z, z, z, z, w_dw, w_dw, b_dw, b_dw)


def _shifted_planes(ext_ref):
    rows = ext_ref.shape[1]
    for s in range(1, 8):
        ext_ref[s, 0:rows - 8, :] = ext_ref[0, s:s + rows - 8, :]


def _window(ext_ref, off, tm, cs):
    s = off % 8
    return ext_ref[s, off - s:off - s + tm, cs]


def _conv_taps(ext_ref, w_ref, offs, tm, out_ref):
    def chunk(cs):
        acc = w_ref[0:1, cs] * _window(ext_ref, offs[0], tm, cs)
        for j in range(1, len(offs)):
            acc = acc + w_ref[j:j + 1, cs] * _window(ext_ref, offs[j], tm, cs)
        out_ref[:, cs] = acc

    _lane_chunks(out_ref.shape[1], chunk)


def _glu_planes(ag_ref, halo_ref, ext_ref, first, c):
    hal = halo_ref[...].astype(F32)
    ext_ref[0, 0:CONV_HALO, :] = hal[:, :c] * _sigmoid(hal[:, c:]) * jnp.where(first, 0.0, 1.0)
    ag = ag_ref[...].astype(F32)
    ext_ref[0, CONV_HALO:, :] = ag[:, :c] * _sigmoid(ag[:, c:])
    _shifted_planes(ext_ref)


def _layernorm_stats(u1):
    mu = jnp.mean(u1, axis=-1, keepdims=True)
    cen = u1 - mu
    rstd = lax.rsqrt(jnp.mean(cen * cen, axis=-1, keepdims=True) + EPS)
    return cen * rstd, rstd


def _conv_mid(ag, w_dw, b_dw, ln_g, ln_b, tm=256):
    s, c2 = ag.shape
    c = c2 // 2
    tm = min(tm, s)

    def body(ag_ref, halo_ref, w_ref, b_ref, g_ref, bb_ref, o_ref, u1_ref, ext_ref):
        _glu_planes(ag_ref, halo_ref, ext_ref, pl.program_id(0) == 0, c)
        base = CONV_HALO - (CONV_KERNEL - 1)
        _conv_taps(ext_ref, w_ref, [base + j for j in range(CONV_KERNEL)], tm, u1_ref)
        xh, _ = _layernorm_stats(u1_ref[...] + b_ref[...])
        u2 = xh * g_ref[...] + bb_ref[...]
        o_ref[...] = (u2 * _sigmoid(u2)).astype(BF16)

    hb = tm // CONV_HALO
    return pl.pallas_call(
        body, name="conv_mid", out_shape=[jax.ShapeDtypeStruct((s, c), BF16), jax.ShapeDtypeStruct((s, c), F32)], grid=(s // tm,),
        in_specs=[_rows(tm, c2), pl.BlockSpec((CONV_HALO, c2), lambda i: (jnp.maximum(i * hb - 1, 0), 0)),
                  _full((CONV_KERNEL, c)), _full((1, c)), _full((1, c)), _full((1, c))],
        out_specs=[_rows(tm, c), _rows(tm, c)], scratch_shapes=[pltpu.VMEM((8, CONV_HALO + tm, c), F32)],
        compiler_params=_params("arbitrary"),
    )(ag, ag, w_dw, b_dw, ln_g, ln_b)


def _loss_grad(xo, target, tm=512):
    s, d = xo.shape
    tm = min(tm, s)

    def body(x_ref, t_ref, dx_ref, loss_ref):
        @pl.when(pl.program_id(0) == 0)
        def _():
            loss_ref[...] = jnp.zeros_like(loss_ref)

        err = x_ref[...] - t_ref[...]
        dx_ref[...] = err * (1.0 / d)
        loss_ref[...] += 0.5 * jnp.sum(jnp.mean(err * err, axis=-1, keepdims=True))

    return pl.pallas_call(
        body, name="loss_grad", out_shape=[jax.ShapeDtypeStruct((s, d), F32), jax.ShapeDtypeStruct((1, 128), F32)],
        grid=(s // tm,), in_specs=[_rows(tm, d)] * 2, out_specs=[_rows(tm, d), _full((1, 128))],
        compiler_params=_params("arbitrary"),
    )(xo, target)


def _postnorm_bwd(y, g, dxo, *, name, with_bias_grad=False, tm=512):
    s, d = y.shape
    tm = min(tm, s)

    def body(y_ref, g_ref, dx_ref, dy_ref, dg_ref, *rest):
        @pl.when(pl.program_id(0) == 0)
        def _():
            dg_ref[...] = jnp.zeros_like(dg_ref)
            for r_ in rest:
                r_[...] = jnp.zeros_like(r_)

        yv, dxo_v = y_ref[...], dx_ref[...]
        r = lax.rsqrt(jnp.mean(yv * yv, axis=-1, keepdims=True) + EPS)
        yh = yv * r
        dyh = dxo_v * g_ref[...]
        dy = r * (dyh - yh * jnp.mean(dyh * yh, axis=-1, keepdims=True))
        dy_ref[...] = dy.astype(BF16)
        dg_ref[...] += jnp.sum(dxo_v * yh, axis=0, keepdims=True)
        for r_ in rest:
            r_[...] += jnp.sum(dy, axis=0, keepdims=True)

    nacc = 2 if with_bias_grad else 1
    return pl.pallas_call(
        body, name=name, out_shape=[jax.ShapeDtypeStruct((s, d), BF16)] + [jax.ShapeDtypeStruct((1, d), F32)] * nacc,
        grid=(s // tm,), in_specs=[_rows(tm, d), _full((1, d)), _rows(tm, d)],
        out_specs=[_rows(tm, d)] + [_full((1, d))] * nacc, compiler_params=_params("arbitrary"),
    )(y, g, dxo)


def _matmul(gmat, w, *, name, out_dtype, transposed_w, tm=512):
    s, k = gmat.shape
    n = w.shape[0] if transposed_w else w.shape[1]
    tm = min(tm, s)

    def body(g_ref, w_ref, o_ref):
        if transposed_w:
            acc = lax.dot_general(g_ref[...], w_ref[...], NT, preferred_element_type=F32)
        else:
            acc = jnp.dot(g_ref[...], w_ref[...], preferred_element_type=F32)
        o_ref[...] = acc.astype(out_dtype)

    return pl.pallas_call(
        body, name=name, out_shape=jax.ShapeDtypeStruct((s, n), out_dtype), grid=(s // tm,),
        in_specs=[_rows(tm, k), _full(w.shape)], out_specs=_rows(tm, n), compiler_params=_params("parallel"),
    )(gmat, w)


def _matmul_prenorm_bwd(pieces, wt, x, g, dres, *, name, tm=256):
    s, d = x.shape
    tm = min(tm, s)
    np_ = len(pieces)

    def body(*refs):
        p_refs, w_refs = refs[:np_], refs[np_:2 * np_]
        x_ref, g_ref, r_ref, dx_ref, dg_ref = refs[2 * np_:]

        @pl.when(pl.program_id(0) == 0)
        def _():
            dg_ref[...] = jnp.zeros_like(dg_ref)

        dh = None
        for p_ref, w_ref in zip(p_refs, w_refs):
            t = jnp.dot(p_ref[...], w_ref[...], preferred_element_type=F32)
            dh = t if dh is None else dh + t
        xv = x_ref[...]
        r = lax.rsqrt(jnp.mean(xv * xv, axis=-1, keepdims=True) + EPS)
        xh = xv * r
        dyh = dh * g_ref[...]
        dx_ref[...] = r_ref[...] + r * (dyh - xh * jnp.mean(dyh * xh, axis=-1, keepdims=True))
        dg_ref[...] += jnp.sum(dh * xh, axis=0, keepdims=True)

    in_specs = []
    for _, c0, kc, _ in pieces:
        assert c0 % kc == 0
        in_specs.append(pl.BlockSpec((tm, kc), lambda i, _b=c0 // kc: (i, _b)))
    for _, _, kc, r0 in pieces:
        assert r0 % kc == 0
        in_specs.append(pl.BlockSpec((kc, d), lambda i, _b=r0 // kc: (_b, 0)))
    in_specs += [_rows(tm, d), _full((1, d)), _rows(tm, d)]
    return pl.pallas_call(
        body, name=name, out_shape=[jax.ShapeDtypeStruct((s, d), F32), jax.ShapeDtypeStruct((1, d), F32)],
        grid=(s // tm,), in_specs=in_specs, out_specs=[_rows(tm, d), _full((1, d))],
        compiler_params=_params("arbitrary"),
    )(*[p[0] for p in pieces], *[wt] * np_, x, g, dres)


def _weight_grad(a, gmat, *, name, a_col0=0, ka=None, out=None, out_shape=None, layer=0, row0=0, ts=1024):
    s = a.shape[0]
    ka = a.shape[1] if ka is None else ka
    n = gmat.shape[1]
    ts = min(ts, s)
    tka = _tile(ka, a_col0, row0)
    shape = out.shape if out is not None else out_shape
    nsteps = s // ts

    def body(a_ref, g_ref, *rest):
        o_ref, acc_ref = rest[-2:]
        i = pl.program_id(1)

        @pl.when(i == 0)
        def _():
            acc_ref[...] = jnp.zeros_like(acc_ref)

        acc_ref[...] += lax.dot_general(a_ref[...], g_ref[...], TN, preferred_element_type=F32)

        @pl.when(i == nsteps - 1)
        def _():
            o_ref[...] = acc_ref[...].astype(BF16)

    in_specs = [pl.BlockSpec((ts, tka), lambda k, i: (i, a_col0 // tka + k)), pl.BlockSpec((ts, n), lambda k, i: (i, 0))]
    args = [a, gmat]
    aliases = {}
    if out is not None:
        in_specs.append(ANY)
        args.append(out)
        aliases = {2: 0}
    return pl.pallas_call(
        body, name=name, out_shape=jax.ShapeDtypeStruct(shape, BF16), grid=(ka // tka, nsteps), in_specs=in_specs,
        out_specs=pl.BlockSpec((None, tka, n), lambda k, i: (layer, row0 // tka + k, 0)),
        scratch_shapes=[pltpu.VMEM((tka, n), F32)],
        input_output_aliases=aliases, compiler_params=_params("parallel", "arbitrary"),
    )(*args)


def _ffn_act_bwd(z, dact, w_dw, b_dw, tm=256):
    s, f2 = z.shape
    f = f2 // 2
    tm = min(tm, s)
    tc = _ffn_cols(f)
    nfc = f // tc

    def body(zu, zg, hu, hg, wu, wg, bu, bg, da_ref, du_ref, dgt_ref, dbu_ref, dbg_ref, dwu_ref, dwg_ref):
        i = pl.program_id(1)

        @pl.when(i == 0)
        def _():
            for r_ in (dbu_ref, dbg_ref, dwu_ref, dwg_ref):
                r_[...] = jnp.zeros_like(r_)

        def chunk(cs):
            def conv(z_ref, h_ref, w_ref, b_ref):
                zc = z_ref[:, cs].astype(F32)
                z2, z1 = _conv3_taps(zc, h_ref[:, cs].astype(F32), i == 0)
                return (z2, z1, zc), w_ref[0:1, cs] * z2 + w_ref[1:2, cs] * z1 + w_ref[2:3, cs] * zc + b_ref[:, cs]

            taps_u, up = conv(zu, hu, wu, bu)
            taps_g, gate = conv(zg, hg, wg, bg)
            da = da_ref[:, cs].astype(F32)
            sg = _sigmoid(gate)
            d_up = da * (gate * sg)
            d_gate = da * up * (sg * (1.0 + gate * (1.0 - sg)))
            du_ref[:, cs] = d_up.astype(BF16)
            dgt_ref[:, cs] = d_gate.astype(BF16)
            for dv, taps, db_ref, dw_ref in ((d_up, taps_u, dbu_ref, dwu_ref), (d_gate, taps_g, dbg_ref, dwg_ref)):
                db_ref[:, cs] += jnp.sum(dv, axis=0, keepdims=True)
                for k_, tap in enumerate(taps):
                    dw_ref[k_:k_ + 1, cs] += jnp.sum(dv * tap, axis=0, keepdims=True)

        _lane_chunks(tc, chunk)

    hb = tm // 8
    tile = lambda off: pl.BlockSpec((tm, tc), lambda j, i: (i, off + j))
    halo = lambda off: pl.BlockSpec((8, tc), lambda j, i: (jnp.maximum(i * hb - 1, 0), off + j))
    prm = lambda rows, off: pl.BlockSpec((rows, tc), lambda j, i: (0, off + j))
    acc = lambda rows: pl.BlockSpec((rows, tc), lambda j, i: (0, j))
    return pl.pallas_call(
        body, name="ffn_act_bwd",
        out_shape=[jax.ShapeDtypeStruct((s, f), BF16)] * 2 + [jax.ShapeDtypeStruct((1, f), F32)] * 2
        + [jax.ShapeDtypeStruct((FFN_CONV, f), F32)] * 2,
        grid=(nfc, s // tm),
        in_specs=[tile(0), tile(nfc), halo(0), halo(nfc), prm(FFN_CONV, 0), prm(FFN_CONV, nfc), prm(1, 0), prm(1, nfc), tile(0)],
        out_specs=[tile(0), tile(0), acc(1), acc(1), acc(FFN_CONV), acc(FFN_CONV)],
        compiler_params=_params("parallel", "arbitrary"),
    )(z, z, z, z, w_dw, w_dw, b_dw, b_dw, dact)


def _conv3_transpose(dug, w_dw, col0, tm=256):
    s, f = dug.shape
    tm = min(tm, s)
    tc = _ffn_cols(f)
    nfc = f // tc
    nrow = s // tm
    off = col0 // tc

    def body(d_ref, n_ref, w_ref, o_ref):
        keep_next = jnp.where(pl.program_id(0) == nrow - 1, 0.0, 1.0)

        def chunk(cs):
            dv = d_ref[:, cs].astype(F32)
            nxt = n_ref[:, cs].astype(F32) * keep_next
            n0, n1 = nxt[0:1, :], nxt[1:2, :]
            row = lax.broadcasted_iota(jnp.int32, dv.shape, 0)
            d1 = jnp.where(row == tm - 1, n0, pltpu.roll(dv, tm - 1, 0))
            d2 = jnp.where(row == tm - 1, n1, jnp.where(row == tm - 2, n0, pltpu.roll(dv, tm - 2, 0)))
            o_ref[:, cs] = (w_ref[2:3, cs] * dv + w_ref[1:2, cs] * d1 + w_ref[0:1, cs] * d2).astype(BF16)

        _lane_chunks(tc, chunk)

    hb = tm // 8
    return pl.pallas_call(
        body, name="conv3_transpose", out_shape=jax.ShapeDtypeStruct((s, f), BF16), grid=(nrow, nfc),
        in_specs=[pl.BlockSpec((tm, tc), lambda i, j: (i, j)),
                  pl.BlockSpec((8, tc), lambda i, j: (jnp.minimum((i + 1) * hb, s // 8 - 1), j)),
                  pl.BlockSpec((FFN_CONV, tc), lambda i, j: (0, off + j))],
        out_specs=pl.BlockSpec((tm, tc), lambda i, j: (i, j)), compiler_params=_params("parallel", "parallel"),
    )(dug, dug, w_dw)


def _conv_mid_bwd(ag, u1, du3, b_dw, ln_g, ln_b, tm=256):
    s, c2 = ag.shape
    c = c2 // 2
    tm = min(tm, s)

    def body(ag_ref, halo_ref, u1in_ref, du_ref, b_ref, g_ref, bb_ref, o_ref, dlg_ref, dlb_ref, db_ref, dw_ref, ext_ref, u1_ref):
        @pl.when(pl.program_id(0) == 0)
        def _():
            for r_ in (dlg_ref, dlb_ref, db_ref, dw_ref):
                r_[...] = jnp.zeros_like(r_)

        _glu_planes(ag_ref, halo_ref, ext_ref, pl.program_id(0) == 0, c)
        xh, rstd = _layernorm_stats(u1in_ref[...] + b_ref[...])
        u2 = xh * g_ref[...] + bb_ref[...]
        sg = _sigmoid(u2)
        du2 = du_ref[...] * (sg * (1.0 + u2 * (1.0 - sg)))
        dlg_ref[...] += jnp.sum(du2 * xh, axis=0, keepdims=True)
        dlb_ref[...] += jnp.sum(du2, axis=0, keepdims=True)
        dxh = du2 * g_ref[...]
        du1 = rstd * (dxh - jnp.mean(dxh, axis=-1, keepdims=True) - xh * jnp.mean(dxh * xh, axis=-1, keepdims=True))
        o_ref[...] = du1.astype(BF16)
        db_ref[...] += jnp.sum(du1, axis=0, keepdims=True)
        u1_ref[...] = du1
        base = CONV_HALO - (CONV_KERNEL - 1)

        def chunk(cs):
            dc = u1_ref[:, cs]
            for j in range(CONV_KERNEL):
                dw_ref[j:j + 1, cs] += jnp.sum(dc * _window(ext_ref, base + j, tm, cs), axis=0, keepdims=True)

        _lane_chunks(c, chunk)

    hb = tm // CONV_HALO
    vec = _full((1, c))
    return pl.pallas_call(
        body, name="conv_mid_bwd",
        out_shape=[jax.ShapeDtypeStruct((s, c), BF16)] + [jax.ShapeDtypeStruct((1, c), F32)] * 3
        + [jax.ShapeDtypeStruct((CONV_HALO, c), F32)],
        grid=(s // tm,),
        in_specs=[_rows(tm, c2), pl.BlockSpec((CONV_HALO, c2), lambda i: (jnp.maximum(i * hb - 1, 0), 0)), _rows(tm, c),
                  _rows(tm, c), vec, vec, vec],
        out_specs=[_rows(tm, c), vec, vec, vec, _full((CONV_HALO, c))],
        scratch_shapes=[pltpu.VMEM((8, CONV_HALO + tm, c), F32), pltpu.VMEM((tm, c), F32)],
        compiler_params=_params("arbitrary"),
    )(ag, ag, u1, du3, b_dw, ln_g, ln_b)


def _glu_conv_bwd(du1, ag, w_dw, tm=256):
    s, c = du1.shape
    tm = min(tm, s)
    nrow = s // tm

    def body(d_ref, n_ref, ag_ref, w_ref, o_ref, db_ref, ext_ref, du0_ref):
        @pl.when(pl.program_id(0) == 0)
        def _():
            db_ref[...] = jnp.zeros_like(db_ref)

        ext_ref[0, 0:tm, :] = d_ref[...].astype(F32)
        ext_ref[0, tm:, :] = n_ref[...].astype(F32) * jnp.where(pl.program_id(0) == nrow - 1, 0.0, 1.0)
        _shifted_planes(ext_ref)
        top = CONV_KERNEL - 1
        _conv_taps(ext_ref, w_ref, [top - j for j in range(CONV_KERNEL)], tm, du0_ref)
        du0 = du0_ref[...]
        ag = ag_ref[...].astype(F32)
        a, gt = ag[:, :c], ag[:, c:]
        sg = _sigmoid(gt)
        da = du0 * sg
        dgt = du0 * a * (sg * (1.0 - sg))
        o_ref[:, :c] = da.astype(BF16)
        o_ref[:, c:] = dgt.astype(BF16)
        db_ref[:, :c] += jnp.sum(da, axis=0, keepdims=True)
        db_ref[:, c:] += jnp.sum(dgt, axis=0, keepdims=True)

    hb = tm // CONV_HALO
    return pl.pallas_call(
        body, name="glu_conv_bwd",
        out_shape=[jax.ShapeDtypeStruct((s, 2 * c), BF16), jax.ShapeDtypeStruct((1, 2 * c), F32)], grid=(nrow,),
        in_specs=[_rows(tm, c), pl.BlockSpec((CONV_HALO, c), lambda i: (jnp.minimum((i + 1) * hb, s // CONV_HALO - 1), 0)),
                  _rows(tm, 2 * c), _full((CONV_KERNEL, c))],
        out_specs=[_rows(tm, 2 * c), _full((1, 2 * c))],
        scratch_shapes=[pltpu.VMEM((8, tm + CONV_HALO, c), F32), pltpu.VMEM((tm, c), F32)],
        compiler_params=_params("arbitrary"),
    )(du1, du1, ag, w_dw)


def _head_rows(v, mask):
    return jnp.max(jnp.where(mask, v, -jnp.inf), axis=-1, keepdims=True)


def _attn_bwd(qv, dmix, mixed, lse, rope, grp, dil):
    l = qv.shape[0]
    s = l * dil
    nb = l // SPAN
    view = lambda t: t.reshape(l, dil * t.shape[1])
    scale = HEAD_DIM ** -0.5
    gw = GROUP_WIDTH

    def body(q_ref, kp_ref, kc_ref, vp_ref, vc_ref, do_ref, mx_ref, l_ref, c_ref, su_ref, sd_ref, cp_ref, sup_ref, sdp_ref,
             dq_ref, dkv_ref, carry_ref):
        b = pl.program_id(1)
        prev_tabs = (cp_ref, sup_ref, sdp_ref)

        @pl.when(b < nb)
        def _():
            row = lax.broadcasted_iota(jnp.int32, (SPAN, 2 * SPAN), 0)
            col = lax.broadcasted_iota(jnp.int32, (SPAN, 2 * SPAN), 1)
            no_prev = jnp.where(b > 0, 0, 4 * SPAN)
            valid = ((col < SPAN) & (col >= row + no_prev)) | ((col >= SPAN) & (col - SPAN <= row))
            masks, keep = _head_masks()
            masks2, _ = _head_masks(2 * SPAN)
            for p in range(gw // 128):
                sl = slice(p * 128, (p + 1) * 128)
                sl_v = slice(gw + p * 128, gw + (p + 1) * 128)
                qp, dop = q_ref[:, sl], do_ref[:, sl]
                kk = jnp.concatenate([kp_ref[:, sl], kc_ref[:, sl]], axis=0)
                vv = jnp.concatenate([vp_ref[:, sl], vc_ref[:, sl]], axis=0)
                prod = dop.astype(F32) * mx_ref[:, sl].astype(F32)
                lsep = l_ref[:, sl]
                dqs, dks, dvs = [], [], []
                for h in range(2):
                    qh, doh = qp * keep[h], dop * keep[h]
                    sc = lax.dot_general(qh, kk, NT, preferred_element_type=F32) * scale
                    pe = jnp.where(valid, jnp.exp(sc - _head_rows(lsep, masks[h])), 0.0)
                    dp = lax.dot_general(doh, vv, NT, preferred_element_type=F32)
                    dbar = jnp.sum(jnp.where(masks[h], prod, 0.0), axis=-1, keepdims=True)
                    ds = (pe * (dp - dbar) * scale).astype(BF16)
                    dqs.append(jnp.dot(ds, kk, preferred_element_type=F32))
                    dks.append(lax.dot_general(ds, qp, TN, preferred_element_type=F32))
                    dvs.append(lax.dot_general(pe.astype(BF16), dop, TN, preferred_element_type=F32))
                dq = jnp.where(masks[0], dqs[0], dqs[1])
                dq_ref[:, sl] = _rope_transpose(dq, c_ref[...], su_ref[...], sd_ref[...]).astype(BF16)
                dk = jnp.where(masks2[0], dks[0], dks[1])
                dv = jnp.where(masks2[0], dvs[0], dvs[1])

                @pl.when(b > 0)
                def _():
                    dk_prev = carry_ref[:, sl] + dk[:SPAN]
                    dkv_ref[:, sl] = _rope_transpose(dk_prev, *[t[...] for t in prev_tabs]).astype(BF16)
                    dkv_ref[:, sl_v] = (carry_ref[:, sl_v] + dv[:SPAN]).astype(BF16)

                carry_ref[:, sl] = dk[SPAN:]
                carry_ref[:, sl_v] = dv[SPAN:]

        @pl.when(b == nb)
        def _():
            for p in range(gw // 128):
                sl = slice(p * 128, (p + 1) * 128)
                sl_v = slice(gw + p * 128, gw + (p + 1) * 128)
                dkv_ref[:, sl] = _rope_transpose(carry_ref[:, sl], *[t[...] for t in prev_tabs]).astype(BF16)
                dkv_ref[:, sl_v] = carry_ref[:, sl_v].astype(BF16)

    blk = (SPAN, gw)
    cb = lambda b: jnp.minimum(b, nb - 1)
    cur = lambda t: pl.BlockSpec(blk, lambda r, b: (cb(b), r * 3 + t))
    prev = lambda t: pl.BlockSpec(blk, lambda r, b: (jnp.maximum(cb(b) - 1, 0), r * 3 + t))
    own = pl.BlockSpec(blk, lambda r, b: (cb(b), r))
    tab = pl.BlockSpec((SPAN, 128), lambda r, b: (cb(b), r))
    tab_prev = pl.BlockSpec((SPAN, 128), lambda r, b: (jnp.maximum(b - 1, 0), r))
    tabs = [view(t) for t in rope]
    dq, dkv = pl.pallas_call(
        body, name=f"attn_bwd_g{grp}",
        out_shape=[jax.ShapeDtypeStruct((l, dil * gw), BF16), jax.ShapeDtypeStruct((l, dil * 2 * gw), BF16)],
        grid=(dil, nb + 1),
        in_specs=[cur(0), prev(1), cur(1), prev(2), cur(2), own, own, own, tab, tab, tab, tab_prev, tab_prev, tab_prev],
        out_specs=[own, pl.BlockSpec((SPAN, 2 * gw), lambda r, b: (jnp.maximum(b - 1, 0), r))],
        scratch_shapes=[pltpu.VMEM((SPAN, 2 * gw), F32)], compiler_params=_params("parallel", "arbitrary"),
    )(qv, qv, qv, qv, qv, view(dmix), view(mixed), view(lse), *tabs, *tabs)
    return dq.reshape(s, gw), dkv.reshape(s, 2 * gw)


def _attn_bwd_dq(qv, dmix, mixed, lse, rope, grp, dil):
    l = qv.shape[0]
    s = l * dil
    nb = l // SPAN
    view = lambda t: t.reshape(l, dil * t.shape[1])

    def body(q_ref, kp_ref, kc_ref, vp_ref, vc_ref, do_ref, mx_ref, l_ref, c_ref, su_ref, sd_ref, o_ref):
        b = pl.program_id(1)
        row = lax.broadcasted_iota(jnp.int32, (SPAN, 2 * SPAN), 0)
        col = lax.broadcasted_iota(jnp.int32, (SPAN, 2 * SPAN), 1)
        no_prev = jnp.where(b > 0, 0, 4 * SPAN)
        valid = ((col < SPAN) & (col >= row + no_prev)) | ((col >= SPAN) & (col - SPAN <= row))
        masks, keep = _head_masks()
        for p in range(GROUP_WIDTH // 128):
            sl = slice(p * 128, (p + 1) * 128)
            qp, dop = q_ref[:, sl], do_ref[:, sl]
            kk = jnp.concatenate([kp_ref[:, sl], kc_ref[:, sl]], axis=0)
            vv = jnp.concatenate([vp_ref[:, sl], vc_ref[:, sl]], axis=0)
            prod = dop.astype(F32) * mx_ref[:, sl].astype(F32)
            lsep = l_ref[:, sl]
            dqs = []
            for h in range(2):
                qh, doh = qp * keep[h], dop * keep[h]
                sc = lax.dot_general(qh, kk, NT, preferred_element_type=F32) * (HEAD_DIM ** -0.5)
                pe = jnp.where(valid, jnp.exp(sc - _head_rows(lsep, masks[h])), 0.0)
                dp = lax.dot_general(doh, vv, NT, preferred_element_type=F32)
                dbar = jnp.sum(jnp.where(masks[h], prod, 0.0), axis=-1, keepdims=True)
                ds = pe * (dp - dbar) * (HEAD_DIM ** -0.5)
                dqs.append(jnp.dot(ds.astype(BF16), kk, preferred_element_type=F32))
            dq = jnp.where(masks[0], dqs[0], dqs[1])
            o_ref[:, sl] = _rope_transpose(dq, c_ref[...], su_ref[...], sd_ref[...]).astype(BF16)

    blk = (SPAN, GROUP_WIDTH)
    cur = lambda t: pl.BlockSpec(blk, lambda r, b: (b, r * 3 + t))
    prev = lambda t: pl.BlockSpec(blk, lambda r, b: (jnp.maximum(b - 1, 0), r * 3 + t))
    own = pl.BlockSpec(blk, lambda r, b: (b, r))
    tab = pl.BlockSpec((SPAN, 128), lambda r, b: (b, r))
    out = pl.pallas_call(
        body, name=f"attn_bwd_dq_g{grp}", out_shape=jax.ShapeDtypeStruct((l, dil * GROUP_WIDTH), BF16), grid=(dil, nb),
        in_specs=[cur(0), prev(1), cur(1), prev(2), cur(2), own, own, own, tab, tab, tab], out_specs=own,
        compiler_params=_params("parallel", "arbitrary"),
    )(qv, qv, qv, qv, qv, view(dmix), view(mixed), view(lse), *[view(t) for t in rope])
    return out.reshape(s, GROUP_WIDTH)


def _attn_bwd_dkv(qv, dmix, mixed, lse, rope, grp, dil):
    l = qv.shape[0]
    s = l * dil
    nb = l // SPAN
    view = lambda t: t.reshape(l, dil * t.shape[1])

    def body(k_ref, v_ref, qc_ref, qn_ref, doc_ref, don_ref, mc_ref, mn_ref, lc_ref, ln_ref,
             c_ref, su_ref, sd_ref, o_ref):
        b = pl.program_id(1)
        row = lax.broadcasted_iota(jnp.int32, (2 * SPAN, SPAN), 0)
        col = lax.broadcasted_iota(jnp.int32, (2 * SPAN, SPAN), 1)
        no_next = jnp.where(b < nb - 1, 0, 4 * SPAN)
        valid = ((row < SPAN) & (col <= row)) | ((row >= SPAN) & (col >= row - SPAN + no_next))
        masks, keep = _head_masks()
        masks2, _ = _head_masks(2 * SPAN)
        for p in range(GROUP_WIDTH // 128):
            sl = slice(p * 128, (p + 1) * 128)
            kp, vp = k_ref[:, sl], v_ref[:, sl]
            qq = jnp.concatenate([qc_ref[:, sl], qn_ref[:, sl]], axis=0)
            doo = jnp.concatenate([doc_ref[:, sl], don_ref[:, sl]], axis=0)
            mm = jnp.concatenate([mc_ref[:, sl], mn_ref[:, sl]], axis=0)
            ll = jnp.concatenate([lc_ref[:, sl], ln_ref[:, sl]], axis=0)
            prod = doo.astype(F32) * mm.astype(F32)
            dks, dvs = [], []
            for h in range(2):
                qh, doh = qq * keep[h], doo * keep[h]
                sc = lax.dot_general(qh, kp, NT, preferred_element_type=F32) * (HEAD_DIM ** -0.5)
                pe = jnp.where(valid, jnp.exp(sc - _head_rows(ll, masks2[h])), 0.0)
                dp = lax.dot_general(doh, vp, NT, preferred_element_type=F32)
                dbar = jnp.sum(jnp.where(masks2[h], prod, 0.0), axis=-1, keepdims=True)
                ds = pe * (dp - dbar) * (HEAD_DIM ** -0.5)
                dvs.append(lax.dot_general(pe.astype(BF16), doo, TN, preferred_element_type=F32))
                dks.append(lax.dot_general(ds.astype(BF16), qq, TN, preferred_element_type=F32))
            dk = jnp.where(masks[0], dks[0], dks[1])
            o_ref[:, sl] = _rope_transpose(dk, c_ref[...], su_ref[...], sd_ref[...]).astype(BF16)
            o_ref[:, GROUP_WIDTH + p * 128:GROUP_WIDTH + (p + 1) * 128] = jnp.where(masks[0], dvs[0], dvs[1]).astype(BF16)

    blk = (SPAN, GROUP_WIDTH)
    nxt_b = lambda b: jnp.minimum(b + 1, nb - 1)
    col_of = lambda t: pl.BlockSpec(blk, lambda r, b: (b, r * 3 + t))
    q_next = pl.BlockSpec(blk, lambda r, b: (nxt_b(b), r * 3))
    own = pl.BlockSpec(blk, lambda r, b: (b, r))
    own_next = pl.BlockSpec(blk, lambda r, b: (nxt_b(b), r))
    tab = pl.BlockSpec((SPAN, 128), lambda r, b: (b, r))
    dv_, mv, lv = view(dmix), view(mixed), view(lse)
    out = pl.pallas_call(
        body, name=f"attn_bwd_dkv_g{grp}", out_shape=jax.ShapeDtypeStruct((l, dil * 2 * GROUP_WIDTH), BF16), grid=(dil, nb),
        in_specs=[col_of(1), col_of(2), col_of(0), q_next, own, own_next, own, own_next, own, own_next, tab, tab, tab],
        out_specs=pl.BlockSpec((SPAN, 2 * GROUP_WIDTH), lambda r, b: (b, r)),
        compiler_params=_params("parallel", "arbitrary"),
    )(qv, qv, qv, qv, dv_, dv_, mv, mv, lv, lv, *[view(t) for t in rope])
    return out.reshape(s, 2 * GROUP_WIDTH)


def _rope_freq_row():
    half = ROT_DIM // 2
    inv = (ROPE_THETA ** (-np.arange(half, dtype=np.float32) / half)).astype(np.float32)
    row = np.zeros((1, 128), np.float32)
    for head in range(128 // HEAD_DIM):
        row[0, head * HEAD_DIM:head * HEAD_DIM + half] = inv
        row[0, head * HEAD_DIM + half:head * HEAD_DIM + ROT_DIM] = inv
    return jnp.asarray(row)


def _ffn_fwd(x, g_pre, g_post, w_up_t, w_dw, b_dw, w_down):
    h, z = _norm_matmul(x, g_pre, w_up_t, tn=_tile(w_up_t.shape[0]), name="ffn_up", tm=1024)
    act = _ffn_act(z, w_dw, b_dw)
    y, xo = _matmul_resnorm(act, w_down, x, g_post, name="ffn_down")
    return xo, (x, h, z, act, y)


def _ffn_bwd(saved, dxo, g_pre, g_post, w_up_t, w_dw, b_dw, w_down):
    x, h, z, act, y = saved
    f = act.shape[1]
    d = x.shape[1]
    dy, dg_post = _postnorm_bwd(y, g_post, dxo, name="ffn_post_bwd")
    dact = _matmul(dy, w_down, name="ffn_dact", out_dtype=BF16, transposed_w=True)
    d_down = _weight_grad(act, dy, name="ffn_dw_down", out_shape=(1, f, d))
    dug_u, dug_g, db_u, db_g, dwd_u, dwd_g = _ffn_act_bwd(z, dact, w_dw, b_dw)
    dz_u = _conv3_transpose(dug_u, w_dw, 0)
    dz_g = _conv3_transpose(dug_g, w_dw, f)
    dx, dg_pre = _matmul_prenorm_bwd([(dz_u, 0, f, 0), (dz_g, 0, f, f)], w_up_t, x, g_pre, dxo, name="ffn_dx")
    d_up_t = _weight_grad(dz_u, h, name="ffn_dw_up", out_shape=(1, 2 * f, d))
    d_up_t = _weight_grad(dz_g, h, name="ffn_dw_up", out=d_up_t, row0=f)
    grads = dict(w_dw=jnp.concatenate([dwd_u, dwd_g], axis=1), b_dw=jnp.concatenate([db_u, db_g], axis=1),
                 g_pre=dg_pre, g_post=dg_post)
    return dx, grads, d_up_t, d_down


def _local_step(x, pos_col, target, p, tie=None, late_weights=None, exchange=None):
    ng = p["norm_g"]
    row = lambda r: ng[r:r + 1]
    freq = _rope_freq_row()
    rope = _rope_tables(pos_col, freq if tie is None else freq + tie[0:1])
    d = x.shape[1]

    h0, *qkv = _qkv_proj(x, row(0), p["w_qkv_t"], rope)
    os_, ls_ = zip(*[_attn_fwd(qkv[g_], g_, d_) for g_, d_ in enumerate(DILATIONS)])
    y_a, x1, mixed, lse = _mix_wo(os_, ls_, p["w_o_t"], x, row(1))
    if late_weights is not None:
        p = {**p, **late_weights(x1)}
    x2, ffn0 = _ffn_fwd(x1, row(2), row(3), p["w_up_t"][0], p["ffn_w_dw"][0], p["ffn_b_dw"][0], p["w_down"][0])
    h1, ag = _norm_matmul(x2, row(4), p["w_pw1_t"], tn=_tile(p["w_pw1_t"].shape[0]), name="conv_pw1", bias=p["b_pw1"])
    u3, u1 = _conv_mid(ag, p["conv_w_dw"], p["conv_b_dw"], p["ln_g"], p["ln_b"])
    y_c, x3 = _matmul_resnorm(u3, p["w_pw2"], x2, row(5), name="conv_pw2", bias=p["b_pw2"])
    x4, ffn1 = _ffn_fwd(x3, row(6), row(7), p["w_up_t"][1], p["ffn_w_dw"][1], p["ffn_b_dw"][1], p["w_down"][1])
    dx4, loss = _loss_grad(x4, target)

    big = [BF16, BF16]

    def tied(r, *tokens):
        tokens = [t for t in tokens if t is not None]
        return row(r) if not tokens else row(r) + jnp.tile(sum(tokens)[0:1], (1, d // 128))

    dx3, gf1, d_up1, d_down1 = _ffn_bwd(ffn1, dx4, row(6), row(7), p["w_up_t"][1], p["ffn_w_dw"][1], p["ffn_b_dw"][1],
                                        p["w_down"][1])
    t0 = exchange.submit("ffn1", [d_up1, d_down1], big) if exchange else None
    dy_c, dg5, db_pw2 = _postnorm_bwd(y_c, tied(5, t0), dx3, name="conv_post_bwd", with_bias_grad=True)
    du3 = _matmul(dy_c, p["w_pw2"], name="conv_du3", out_dtype=F32, transposed_w=True)
    d_wpw2 = _weight_grad(u3, dy_c, name="conv_dw_pw2", out_shape=(1, u3.shape[1], d))
    du1, d_lng, d_lnb, d_cbdw, d_cwdw = _conv_mid_bwd(ag, u1, du3, p["conv_b_dw"], p["ln_g"], p["ln_b"])
    dag, db_pw1 = _glu_conv_bwd(du1, ag, p["conv_w_dw"])
    dx2, dg4 = _matmul_prenorm_bwd([(dag, 0, dag.shape[1], 0)], p["w_pw1_t"], x2, row(4), dx3, name="conv_dx")
    d_wpw1_t = _weight_grad(dag, h1, name="conv_dw_pw1", out_shape=(1, dag.shape[1], d))
    t0 = exchange.advance(dx2) if exchange else None
    t1 = exchange.submit("conv", [d_wpw1_t, d_wpw2], big) if exchange else None
    dx1, gf0, d_up0, d_down0 = _ffn_bwd(ffn0, dx2, row(2), tied(3, t0, t1), p["w_up_t"][0], p["ffn_w_dw"][0], p["ffn_b_dw"][0],
                                        p["w_down"][0])
    t0 = exchange.advance(dx1) if exchange else None
    t1 = exchange.submit("ffn0", [d_up0, d_down0], big) if exchange else None
    dy_a, dg1 = _postnorm_bwd(y_a, tied(1, t0, t1), dx1, name="attn_post_bwd")
    dmix = _matmul(dy_a, p["w_o_t"], name="attn_dmix", out_dtype=BF16, transposed_w=False)
    d_wo_t = _weight_grad(dy_a, mixed, name="attn_dw_o", out_shape=(1, d, GROUP_WIDTH))
    pieces, d_wqkv_t = [], None
    for g_, d_ in enumerate(DILATIONS):
        if exchange and g_ > 0:
            tok = exchange.advance(dkv)
            if tok is not None:
                rope = (rope[0] + tok[0:1], rope[1], rope[2])
        dq, dkv = _attn_bwd(qkv[g_], dmix, mixed, lse, rope, g_, d_)
        for t, (arr, c0) in enumerate(((dq, 0), (dkv, 0), (dkv, GROUP_WIDTH))):
            r0 = (3 * t + g_) * GROUP_WIDTH
            pieces.append((arr, c0, GROUP_WIDTH, r0))
            d_wqkv_t = _weight_grad(arr, h0, name="attn_dw_qkv", a_col0=c0, ka=GROUP_WIDTH, out=d_wqkv_t,
                                    out_shape=(1, p["w_qkv_t"].shape[0], d), row0=r0)
    t0 = exchange.advance(dkv) if exchange else None
    t1 = exchange.submit("attn", [d_wqkv_t, d_wo_t], big) if exchange else None
    grad_x, dg0 = _matmul_prenorm_bwd(pieces, p["w_qkv_t"], x, tied(0, t0, t1), dx1, name="attn_dx")

    grads = dict(
        norm_g=jnp.concatenate([dg0, dg1, gf0["g_pre"], gf0["g_post"], dg4, dg5, gf1["g_pre"], gf1["g_post"]], axis=0),
        w_qkv_t=d_wqkv_t, w_o_t=d_wo_t, w_pw1_t=d_wpw1_t, b_pw1=db_pw1,
        conv_w_dw=d_cwdw[:CONV_KERNEL], conv_b_dw=d_cbdw, ln_g=d_lng, ln_b=d_lnb, w_pw2=d_wpw2, b_pw2=db_pw2,
        w_up_t=[d_up0, d_up1], ffn_w_dw=jnp.stack([gf0["w_dw"], gf1["w_dw"]]),
        ffn_b_dw=jnp.concatenate([gf0["b_dw"], gf1["b_dw"]], axis=0), w_down=[d_down0, d_down1])
    return loss, grad_x, grads


SMALL_AXIS = dict(norm_g=2, conv_b_pw1=1, conv_w_dw=2, conv_b_dw=1, conv_ln_g=1, conv_ln_b=1, conv_b_pw2=1, ffn_w_dw=2)
SMALL = tuple(SMALL_AXIS)
MATMUL_WEIGHTS = dict(attn_w_qkv=True, conv_w_pw1=True, ffn_w_up=True, conv_w_pw2=False, ffn_w_down=False)


def _pack(arrays, cols, row_multiple):
    flat = jnp.concatenate([a.reshape(-1) for a in arrays])
    rows = -(-flat.shape[0] // cols)
    rows = -(-rows // row_multiple) * row_multiple
    return jnp.pad(flat, (0, rows * cols - flat.shape[0])).reshape(rows, cols)


def _unpack(packed, shapes):
    flat = packed.reshape(packed.shape[:-2] + (-1,))
    out, off = [], 0
    for shp in shapes:
        n = math.prod(shp)
        out.append(flat[..., off:off + n].reshape(packed.shape[:-2] + tuple(shp)))
        off += n
    return out


def _join_shards(stacked, axis):
    moved = jnp.moveaxis(stacked, 0, axis)
    shp = moved.shape
    return moved.reshape(shp[:axis] + (shp[axis] * shp[axis + 1],) + shp[axis + 2:])


def _split_shards(whole, axis):
    shp = whole.shape
    cut = whole.reshape(shp[:axis] + (N_DEV, shp[axis] // N_DEV) + shp[axis + 1:])
    return jnp.moveaxis(cut, axis, 0)


def _row_shard(w, transposed):
    t = jnp.swapaxes(w, 1, 2) if transposed else w
    return t.astype(BF16).reshape(-1, t.shape[-1])


def kernel(x, positions, norm_g, attn_w_qkv, attn_w_o, conv_w_pw1, conv_b_pw1, conv_w_dw, conv_b_dw, conv_ln_g, conv_ln_b, conv_w_pw2, conv_b_pw2, ffn_w_up, ffn_w_dw, ffn_b_dw, ffn_w_down, loss_target, m_norm_g, m_attn_w_qkv, m_attn_w_o, m_conv_w_pw1, m_conv_b_pw1, m_conv_w_dw, m_conv_b_dw, m_conv_ln_g, m_conv_ln_b, m_conv_w_pw2, m_conv_b_pw2, m_ffn_w_up, m_ffn_w_dw, m_ffn_b_dw, m_ffn_w_down, v_norm_g, v_attn_w_qkv, v_attn_w_o, v_conv_w_pw1, v_conv_b_pw1, v_conv_w_dw, v_conv_b_dw, v_conv_ln_g, v_conv_ln_b, v_conv_w_pw2, v_conv_b_pw2, v_ffn_w_up, v_ffn_w_dw, v_ffn_b_dw, v_ffn_w_down):
    w = dict(norm_g=norm_g, attn_w_qkv=attn_w_qkv, attn_w_o=attn_w_o, conv_w_pw1=conv_w_pw1, conv_b_pw1=conv_b_pw1,
             conv_w_dw=conv_w_dw, conv_b_dw=conv_b_dw, conv_ln_g=conv_ln_g, conv_ln_b=conv_ln_b, conv_w_pw2=conv_w_pw2,
             conv_b_pw2=conv_b_pw2, ffn_w_up=ffn_w_up, ffn_w_dw=ffn_w_dw, ffn_w_down=ffn_w_down)
    m = dict(norm_g=m_norm_g, attn_w_qkv=m_attn_w_qkv, attn_w_o=m_attn_w_o, conv_w_pw1=m_conv_w_pw1, conv_b_pw1=m_conv_b_pw1,
             conv_w_dw=m_conv_w_dw, conv_b_dw=m_conv_b_dw, conv_ln_g=m_conv_ln_g, conv_ln_b=m_conv_ln_b, conv_w_pw2=m_conv_w_pw2,
             conv_b_pw2=m_conv_b_pw2, ffn_w_up=m_ffn_w_up, ffn_w_dw=m_ffn_w_dw, ffn_w_down=m_ffn_w_down)
    v = dict(norm_g=v_norm_g, attn_w_qkv=v_attn_w_qkv, attn_w_o=v_attn_w_o, conv_w_pw1=v_conv_w_pw1, conv_b_pw1=v_conv_b_pw1,
             conv_w_dw=v_conv_w_dw, conv_b_dw=v_conv_b_dw, conv_ln_g=v_conv_ln_g, conv_ln_b=v_conv_ln_b, conv_w_pw2=v_conv_w_pw2,
             conv_b_pw2=v_conv_b_pw2, ffn_w_up=v_ffn_w_up, ffn_w_dw=v_ffn_w_dw, ffn_w_down=v_ffn_w_down)
    d = x.shape[-1]

    w_qkv_t = _all_gather(_row_shard(attn_w_qkv, True), "gather_w_qkv").reshape(-1, d)
    w_o_t = _all_gather(_row_shard(attn_w_o, True), "gather_w_o").reshape(d, -1)
    small = _all_gather(_pack([w[n] for n in SMALL], 128, 8), "gather_small_weights")
    sm = {n: _join_shards(stacked, SMALL_AXIS[n])
          for n, stacked in zip(SMALL, _unpack(small, [w[n].shape for n in SMALL]))}
    late = {n: t for n, t in MATMUL_WEIGHTS.items() if n != "attn_w_qkv"}
    shares = [_row_shard(w[n], t) for n, t in late.items()]
    rows = [s_.shape[0] for s_ in shares]
    late_share = jnp.concatenate(shares, axis=0)
    send_sems, recv_sems, share_thru, land_thru, tie = _gather_start(late_share)
    me = 4 * lax.axis_index("x") + 2 * lax.axis_index("y") + lax.axis_index("c")

    def late_weights(after):
        big = _gather_wait(send_sems, recv_sems, share_thru, land_thru, after)
        big = lax.dynamic_update_slice(big, late_share[None], (me, 0, 0))
        whole, r0 = {}, 0
        for n, nr in zip(late, rows):
            layers = w[n].shape[0]
            seg = big[:, r0:r0 + nr].reshape(N_DEV, layers, nr // layers, d)
            whole[n] = [seg[:, l_].reshape(-1, d) for l_ in range(layers)]
            r0 += nr
        return dict(w_pw1_t=whole["conv_w_pw1"][0], w_pw2=whole["conv_w_pw2"][0], w_up_t=whole["ffn_w_up"],
                    w_down=whole["ffn_w_down"])

    p = dict(norm_g=sm["norm_g"].reshape(-1, d), w_qkv_t=w_qkv_t, w_o_t=w_o_t, b_pw1=sm["conv_b_pw1"],
             conv_w_dw=sm["conv_w_dw"][0], conv_b_dw=sm["conv_b_dw"], ln_g=sm["conv_ln_g"], ln_b=sm["conv_ln_b"],
             b_pw2=sm["conv_b_pw2"], ffn_w_dw=sm["ffn_w_dw"], ffn_b_dw=[ffn_b_dw[0:1], ffn_b_dw[1:2]])

    exchange = _GradExchange()
    loss, grad_x, g = _local_step(x[0], positions.reshape(-1, 1), loss_target[0], p, tie, late_weights, exchange)
    loss = lax.psum(loss[0, 0], ("x", "y", "c"))
    gsmall = dict(norm_g=g["norm_g"].reshape(norm_g.shape[0], 4, -1), conv_b_pw1=g["b_pw1"], conv_w_dw=g["conv_w_dw"][None],
                  conv_b_dw=g["conv_b_dw"], conv_ln_g=g["ln_g"], conv_ln_b=g["ln_b"], conv_b_pw2=g["b_pw2"], ffn_w_dw=g["ffn_w_dw"])
    small_contrib = jnp.concatenate([_split_shards(gsmall[n], SMALL_AXIS[n]).reshape(N_DEV, -1) for n in SMALL], axis=1)
    srows = small.shape[1]
    small_contrib = jnp.pad(small_contrib, ((0, 0), (0, srows * 128 - small_contrib.shape[1]))).reshape(1, N_DEV, srows, 128)
    exchange.advance(grad_x)
    small_sums = _rs_chips([_rs_pair_add(small_contrib, _rs_sibling([small_contrib])[0], exchange.core, F32)])[0]

    outs = {}

    def update(n, reduced):
        gsum = jnp.swapaxes(reduced, 1, 2) if n == "attn_w_o" or MATMUL_WEIGHTS.get(n) else reduced
        outs[n] = (gsum, *_adamw(gsum, w[n], m[n], v[n], "adamw"))

    (s_up1, s_down1), (s_pw1, s_pw2), (s_up0, s_down0) = exchange.results()[:3]
    update("conv_w_pw1", s_pw1)
    update("conv_w_pw2", s_pw2)
    update("ffn_w_up", jnp.concatenate([s_up0, s_up1], axis=0))
    update("ffn_w_down", jnp.concatenate([s_down0, s_down1], axis=0))
    sshapes = [w[n].shape for n in SMALL]
    souts = _sum_adamw(small_sums[0], *[_pack([t[n] for n in SMALL], 128, 8) for t in (w, m, v)], name="sum_adamw_small")
    for n, vals in zip(SMALL, zip(*[_unpack(o, sshapes) for o in souts])):
        outs[n] = vals
    bparts = _all_gather(_pack([g["ffn_b_dw"]], 128, 8), "gather_bias_grads")
    bouts = _sum_adamw(bparts, *[_pack([t], 128, 8) for t in (ffn_b_dw, m_ffn_b_dw, v_ffn_b_dw)], name="sum_adamw_bias")
    outs["ffn_b_dw"] = tuple(_unpack(o, [ffn_b_dw.shape])[0] for o in bouts)
    exchange.advance(outs["ffn_w_down"][1][0, :8, :128] + bouts[1][:8] + souts[1][:8])
    s_qkv, s_wo = exchange.results()[3]
    update("attn_w_qkv", s_qkv)
    update("attn_w_o", s_wo)

    order = ("norm_g", "attn_w_qkv", "attn_w_o", "conv_w_pw1", "conv_b_pw1", "conv_w_dw", "conv_b_dw", "conv_ln_g",
             "conv_ln_b", "conv_w_pw2", "conv_b_pw2", "ffn_w_up", "ffn_w_dw", "ffn_b_dw", "ffn_w_down")
    return (loss, grad_x[None], *[outs[n][0] for n in order], *[outs[n][1] for n in order],
            *[outs[n][2] for n in order], *[outs[n][3] for n in order])
```

```python
import functools
import math

import numpy as np
import jax
import jax.numpy as jnp
from jax import lax
from jax.experimental import pallas as pl
from jax.experimental.pallas import tpu as pltpu

F32 = jnp.float32
BF16 = jnp.bfloat16
EPS = 1e-6
N_DEV = 8
HEAD_DIM = 64
GROUP_WIDTH = 512
DILATIONS = (1, 4, 16)
SPAN = 128
ROT_DIM = 16
ROPE_THETA = 500000.0
CONV_KERNEL = 31
CONV_HALO = 32
FFN_CONV = 3
ADAM_LR, ADAM_B1, ADAM_B2, ADAM_EPS, ADAM_WD, ADAM_STEP = 0.001, 0.9, 0.999, 1e-08, 0.01, 10
VMEM_LIMIT_BYTES = 56 * 1024 * 1024
MESH = pl.DeviceIdType.MESH
ANY = pl.BlockSpec(memory_space=pl.ANY)
NT = (((1,), (1,)), ((), ()))
TN = (((0,), (0,)), ((), ()))


def _params(*sem):
    return pltpu.CompilerParams(dimension_semantics=sem, vmem_limit_bytes=VMEM_LIMIT_BYTES)


def _sigmoid(v):
    return 1.0 / (1.0 + jnp.exp(-v))


def _full(shape):
    return pl.BlockSpec(shape, lambda *_: (0,) * len(shape))


def _rows(tm, width):
    return pl.BlockSpec((tm, width), lambda i, *_: (i, 0))


def _tile(n, *multiples_of):
    for t in (1408, 1024, 512, 384, 256, 128):
        if n % t == 0 and all(o % t == 0 for o in multiples_of):
            return t
    raise ValueError((n, multiples_of))


def _all_gather(shard, name):
    r, c_ = shard.shape

    def body(x_ref, out_ref, send_sems, recv_sems, local_sem):
        x, y, c = lax.axis_index("x"), lax.axis_index("y"), lax.axis_index("c")
        me, sibling = (x, y, c), (x, y, 1 - c)
        chips = [(1 - x, y), (x, 1 - y), (1 - x, 1 - y)]

        def rows(px, py, pc):
            return out_ref.at[4 * px + 2 * py + pc]

        def copy(k, block, to, src=None):
            return pltpu.make_async_remote_copy(
                src_ref=rows(*block) if src is None else src, dst_ref=rows(*block),
                send_sem=send_sems.at[k], recv_sem=recv_sems.at[k], device_id=to, device_id_type=MESH)

        mine = pltpu.make_async_copy(x_ref, rows(*me), local_sem)
        mine.start()
        first = [copy(0, me, sibling, src=x_ref)]
        first += [copy(1 + j, me, (*chip, c), src=x_ref) for j, chip in enumerate(chips)]
        for cp in first:
            cp.start()
        passed = [copy(4 + j, (*chip, c), sibling) for j, chip in enumerate(chips)]
        for j, chip in enumerate(chips):
            copy(1 + j, (*chip, c), me).wait_recv()
            passed[j].start()
        copy(0, sibling, me).wait_recv()
        for j, chip in enumerate(chips):
            copy(4 + j, (*chip, 1 - c), me).wait_recv()
        for cp in first + passed:
            cp.wait_send()
        mine.wait()

    return pl.pallas_call(
        body, name=name, out_shape=jax.ShapeDtypeStruct((N_DEV, r, c_), shard.dtype),
        in_specs=[ANY], out_specs=ANY,
        scratch_shapes=[pltpu.SemaphoreType.DMA((7,)), pltpu.SemaphoreType.DMA((7,)), pltpu.SemaphoreType.DMA],
    )(shard)


HBM = pl.BlockSpec(memory_space=pltpu.HBM)
SEM = pl.BlockSpec(memory_space=pltpu.SEMAPHORE)
SIDE_EFFECT = pltpu.CompilerParams(has_side_effects=pltpu.SideEffectType.DATAFLOW_SIDE_EFFECTING)


def _gather_start(shard):
    r, c_ = shard.shape

    def body(x_ref, land_ref, send_sems, recv_sems, x_thru, land_thru, token):
        x, y, c = lax.axis_index("x"), lax.axis_index("y"), lax.axis_index("c")
        me = 4 * x + 2 * y + c
        for k in range(1, N_DEV):
            peer = (1 - x if k & 4 else x, 1 - y if k & 2 else y, 1 - c if k & 1 else c)
            pltpu.make_async_remote_copy(src_ref=x_ref, dst_ref=land_ref.at[me], send_sem=send_sems.at[k - 1],
                                         recv_sem=recv_sems.at[k - 1], device_id=peer, device_id_type=MESH).start()
        token[...] = jnp.zeros_like(token)

    land = pltpu.with_memory_space_constraint(lax.empty((N_DEV, r, c_), shard.dtype), pltpu.HBM)
    return pl.pallas_call(
        body, name="gather_late_weights_start",
        out_shape=(pltpu.SemaphoreType.DMA((N_DEV - 1,)), pltpu.SemaphoreType.DMA((N_DEV - 1,)),
                   pltpu.HBM(shard.shape, shard.dtype), pltpu.HBM((N_DEV, r, c_), shard.dtype),
                   jax.ShapeDtypeStruct((8, 128), F32)),
        in_specs=(HBM, HBM), out_specs=(SEM, SEM, HBM, HBM, pl.BlockSpec(memory_space=pltpu.VMEM)),
        input_output_aliases={0: 2, 1: 3}, compiler_params=SIDE_EFFECT,
    )(pltpu.with_memory_space_constraint(shard, pltpu.HBM), land)


def _gather_wait(send_sems, recv_sems, shard_thru, land_thru, after):
    def body(x_ref, land_ref, send_sems, recv_sems, after_ref, x_dead, got_ref):
        x, y, c = lax.axis_index("x"), lax.axis_index("y"), lax.axis_index("c")
        for k in range(N_DEV - 1):
            copy = pltpu.make_async_remote_copy(src_ref=x_ref, dst_ref=land_ref.at[0], send_sem=send_sems.at[k],
                                                recv_sem=recv_sems.at[k], device_id=(x, y, c), device_id_type=MESH)
            copy.wait_send()
            copy.wait_recv()

    return pl.pallas_call(
        body, name="gather_late_weights_wait",
        out_shape=(pltpu.HBM(shard_thru.shape, shard_thru.dtype), pltpu.HBM(land_thru.shape, land_thru.dtype)),
        in_specs=(HBM, HBM, SEM, SEM, ANY), out_specs=(HBM, HBM), input_output_aliases={0: 0, 1: 1},
        compiler_params=SIDE_EFFECT,
    )(shard_thru, land_thru, send_sems, recv_sems, after)[1]


def _hbm(a):
    return pltpu.with_memory_space_constraint(a, pltpu.HBM)


def _exchange_start(name, arrays, lands, plan, ncopies):
    n = len(arrays)

    def body(*refs):
        send_sems, recv_sems, token = refs[2 * n], refs[2 * n + 1], refs[-1]
        x, y, c = lax.axis_index("x"), lax.axis_index("y"), lax.axis_index("c")
        for k, (src, dst, peer) in enumerate(plan(x, y, c, refs[:n], refs[n:2 * n])):
            pltpu.make_async_remote_copy(src_ref=src, dst_ref=dst, send_sem=send_sems.at[k], recv_sem=recv_sems.at[k],
                                         device_id=peer, device_id_type=MESH).start()
        token[...] = jnp.zeros_like(token)

    both = list(arrays) + list(lands)
    outs = pl.pallas_call(
        body, name=name,
        out_shape=(pltpu.SemaphoreType.DMA((ncopies,)), pltpu.SemaphoreType.DMA((ncopies,)),
                   *[pltpu.HBM(a.shape, a.dtype) for a in both], jax.ShapeDtypeStruct((8, 128), F32)),
        in_specs=(HBM,) * (2 * n), out_specs=(SEM, SEM) + (HBM,) * (2 * n) + (pl.BlockSpec(memory_space=pltpu.VMEM),),
        input_output_aliases={i: 2 + i for i in range(2 * n)}, compiler_params=SIDE_EFFECT,
    )(*[_hbm(a) for a in both])
    return outs[0], outs[1], list(outs[2:2 + n]), list(outs[2 + n:2 + 2 * n]), outs[-1]


def _exchange_wait(name, send_sems, recv_sems, arrays, lands, plan, after):
    n = len(arrays)

    def body(*refs):
        send_sems, recv_sems = refs[2 * n], refs[2 * n + 1]
        x, y, c = lax.axis_index("x"), lax.axis_index("y"), lax.axis_index("c")
        for k, (src, dst, peer) in enumerate(plan(x, y, c, refs[:n], refs[n:2 * n])):
            copy = pltpu.make_async_remote_copy(src_ref=src, dst_ref=dst, send_sem=send_sems.at[k], recv_sem=recv_sems.at[k],
                                                device_id=peer, device_id_type=MESH)
            copy.wait_send()
            copy.wait_recv()

    both = list(arrays) + list(lands)
    outs = pl.pallas_call(
        body, name=name, out_shape=tuple(pltpu.HBM(a.shape, a.dtype) for a in both),
        in_specs=(HBM,) * (2 * n) + (SEM, SEM, ANY), out_specs=(HBM,) * (2 * n),
        input_output_aliases={i: i for i in range(2 * n)}, compiler_params=SIDE_EFFECT,
    )(*both, send_sems, recv_sems, after)
    return list(outs[:n]), list(outs[n:])


def _sibling_plan(x, y, c, g_refs, land_refs):
    return [(g.at[:, 2 * q + (1 - c)], o.at[:, q], (x, y, 1 - c)) for g, o in zip(g_refs, land_refs) for q in range(4)]


def _chips_plan(x, y, c, p_refs, land_refs):
    chips = [(1 - x, y), (x, 1 - y), (1 - x, 1 - y)]
    return [(p_.at[:, 2 * qx + qy], o.at[:, 2 * x + y], (qx, qy, c)) for p_, o in zip(p_refs, land_refs) for qx, qy in chips]


class _GradExchange:
    def __init__(self):
        self.core = lax.axis_index("c").astype(jnp.int32).reshape(1)
        self.chip = 2 * lax.axis_index("x") + lax.axis_index("y")
        self.groups = []

    def submit(self, tag, arrays, dtypes):
        arrays = [a.reshape(a.shape[0], N_DEV, a.shape[1] // N_DEV, a.shape[2]) for a in arrays]
        lands = [lax.empty((a.shape[0], 4) + a.shape[2:], a.dtype) for a in arrays]
        send, recv, arrays, lands, token = _exchange_start(f"rs_pair_start_{tag}", arrays, lands, _sibling_plan, 4 * len(arrays))
        self.groups.append(dict(tag=tag, stage=1, sems=(send, recv), arrays=arrays, lands=lands, dtypes=dtypes))
        return token

    def advance(self, after):
        token = None
        for g in self.groups:
            if g["stage"] == 1:
                arrays, got = _exchange_wait(f"rs_pair_wait_{g['tag']}", *g["sems"], g["arrays"], g["lands"], _sibling_plan, after)
                parts = [_rs_pair_add(a, b, self.core, dt) for a, b, dt in zip(arrays, got, g["dtypes"])]
                lands = [lax.empty(p_.shape, p_.dtype) for p_ in parts]
                send, recv, parts, lands, tok = _exchange_start(f"rs_chip_start_{g['tag']}", parts, lands, _chips_plan, 3 * len(parts))
                g.update(stage=2, sems=(send, recv), arrays=parts, lands=lands)
                token = tok if token is None else token + tok
            elif g["stage"] == 2:
                parts, lands = _exchange_wait(f"rs_chip_wait_{g['tag']}", *g["sems"], g["arrays"], g["lands"], _chips_plan, after)
                sums = []
                for p_, land in zip(parts, lands):
                    l, _, r, c_ = p_.shape
                    own = lax.dynamic_slice(p_, (0, self.chip, 0, 0), (l, 1, r, c_))
                    sums.append(_sum_parts(lax.dynamic_update_slice(land, own, (0, self.chip, 0, 0)), "sum_chips"))
                g.update(stage=3, sums=sums)
        return token

    def results(self):
        return [g.get("sums") for g in self.groups]


def _with_rows(g, n):
    return jax.ShapeDtypeStruct((g.shape[0], n) + tuple(g.shape[2:]), g.dtype)


def _rs_sibling(gs):
    n = len(gs)

    def body(*refs):
        g_refs, o_refs, (send_sems, recv_sems) = refs[:n], refs[n:2 * n], refs[2 * n:]
        x, y, c = lax.axis_index("x"), lax.axis_index("y"), lax.axis_index("c")
        copies = [pltpu.make_async_remote_copy(
            src_ref=g_refs[w].at[:, 2 * q + (1 - c)], dst_ref=o_refs[w].at[:, q], send_sem=send_sems.at[4 * w + q],
            recv_sem=recv_sems.at[4 * w + q], device_id=(x, y, 1 - c), device_id_type=MESH)
            for w in range(n) for q in range(4)]
        for cp in copies:
            cp.start()
        for cp in copies:
            cp.wait_recv()
        for cp in copies:
            cp.wait_send()

    return pl.pallas_call(
        body, name="rs_sibling", out_shape=[_with_rows(g, 4) for g in gs],
        in_specs=[ANY] * n, out_specs=[ANY] * n,
        scratch_shapes=[pltpu.SemaphoreType.DMA((4 * n,)), pltpu.SemaphoreType.DMA((4 * n,))],
    )(*gs)


def _rs_pair_add(g, got, core, out_dtype):
    l, _, r, c_ = g.shape

    def body(core_ref, g_ref, got_ref, o_ref):
        o_ref[...] = (g_ref[...].astype(F32) + got_ref[...].astype(F32)).astype(out_dtype)

    blk = (None, None, r, c_)
    return pl.pallas_call(
        body, name="rs_pair_add", out_shape=jax.ShapeDtypeStruct((l, 4, r, c_), out_dtype),
        grid_spec=pltpu.PrefetchScalarGridSpec(
            num_scalar_prefetch=1, grid=(l, 4),
            in_specs=[pl.BlockSpec(blk, lambda i, q, core_ref: (i, 2 * q + core_ref[0], 0, 0)),
                      pl.BlockSpec(blk, lambda i, q, core_ref: (i, q, 0, 0))],
            out_specs=pl.BlockSpec(blk, lambda i, q, core_ref: (i, q, 0, 0))),
        compiler_params=_params("parallel", "parallel"),
    )(core, g, got)


def _rs_chips(parts):
    n = len(parts)

    def body(*refs):
        p_refs, o_refs, (send_sems, recv_sems, local_sems) = refs[:n], refs[n:2 * n], refs[2 * n:]
        x, y, c = lax.axis_index("x"), lax.axis_index("y"), lax.axis_index("c")
        my_chip = 2 * x + y
        chips = [(1 - x, y), (x, 1 - y), (1 - x, 1 - y)]
        local = [pltpu.make_async_copy(p_refs[w].at[:, my_chip], o_refs[w].at[:, my_chip], local_sems.at[w]) for w in range(n)]
        for cp in local:
            cp.start()
        copies = [pltpu.make_async_remote_copy(
            src_ref=p_refs[w].at[:, 2 * qx + qy], dst_ref=o_refs[w].at[:, my_chip], send_sem=send_sems.at[3 * w + k],
            recv_sem=recv_sems.at[3 * w + k], device_id=(qx, qy, c), device_id_type=MESH)
            for w in range(n) for k, (qx, qy) in enumerate(chips)]
        for cp in copies:
            cp.start()
        for cp in copies:
            cp.wait_recv()
        for cp in copies:
            cp.wait_send()
        for cp in local:
            cp.wait()

    return pl.pallas_call(
        body, name="rs_chips", out_shape=[jax.ShapeDtypeStruct(p.shape, p.dtype) for p in parts],
        in_specs=[ANY] * n, out_specs=[ANY] * n,
        scratch_shapes=[pltpu.SemaphoreType.DMA((3 * n,)), pltpu.SemaphoreType.DMA((3 * n,)), pltpu.SemaphoreType.DMA((n,))],
    )(*parts)


def _sum_parts(parts, name):
    l, n, r, c_ = parts.shape

    def body(p_ref, o_ref):
        g = p_ref[0].astype(F32)
        for s in range(1, n):
            g = g + p_ref[s].astype(F32)
        o_ref[...] = g

    return pl.pallas_call(
        body, name=name, out_shape=jax.ShapeDtypeStruct((l, r, c_), F32), grid=(l,),
        in_specs=[pl.BlockSpec((None, n, r, c_), lambda i: (i, 0, 0, 0))],
        out_specs=pl.BlockSpec((None, r, c_), lambda i: (i, 0, 0)), compiler_params=_params("parallel"),
    )(parts)


def _adamw_math(w, g, m, v):
    m = ADAM_B1 * m + (1.0 - ADAM_B1) * g
    v = ADAM_B2 * v + (1.0 - ADAM_B2) * (g * g)
    m_hat = m / (1.0 - ADAM_B1 ** ADAM_STEP)
    v_hat = v / (1.0 - ADAM_B2 ** ADAM_STEP)
    delta = -ADAM_LR * (m_hat / (jnp.sqrt(v_hat) + ADAM_EPS) + ADAM_WD * w)
    return delta, m, v


def _adamw(g, w, m, v, name):
    l, k, n = w.shape
    tk = 256 if k % 256 == 0 else k

    def body(g_ref, w_ref, m_ref, v_ref, d_ref, nm_ref, nv_ref):
        d_ref[...], nm_ref[...], nv_ref[...] = _adamw_math(w_ref[...], g_ref[...], m_ref[...], v_ref[...])

    spec = pl.BlockSpec((None, tk, n), lambda i, j: (i, j, 0))
    return pl.pallas_call(
        body, name=name, out_shape=[jax.ShapeDtypeStruct((l, k, n), F32)] * 3, grid=(l, k // tk),
        in_specs=[spec] * 4, out_specs=[spec] * 3, compiler_params=_params("parallel", "parallel"),
    )(g, w, m, v)


def _sum_adamw(parts, w, m, v, name):
    n, r, c_ = parts.shape

    def body(p_ref, w_ref, m_ref, v_ref, g_ref, d_ref, nm_ref, nv_ref):
        g = p_ref[0]
        for s in range(1, n):
            g = g + p_ref[s]
        g_ref[...] = g
        d_ref[...], nm_ref[...], nv_ref[...] = _adamw_math(w_ref[...], g, m_ref[...], v_ref[...])

    return pl.pallas_call(
        body, name=name, out_shape=[jax.ShapeDtypeStruct((r, c_), F32)] * 4, grid=(1,),
        in_specs=[_full((n, r, c_))] + [_full((r, c_))] * 3, out_specs=[_full((r, c_))] * 4,
        compiler_params=_params("arbitrary"),
    )(parts, w, m, v)


def _rope_tables(pos_col, freq_row):
    s = pos_col.shape[0]
    tm = min(1024, s)

    def body(p_ref, f_ref, c_ref, su_ref, sd_ref):
        ang = p_ref[...].astype(F32) * f_ref[...]
        lane = lax.broadcasted_iota(jnp.int32, ang.shape, 1) & (HEAD_DIM - 1)
        cs, sn = jnp.cos(ang), jnp.sin(ang)
        c_ref[...] = jnp.where(lane < ROT_DIM, cs, 1.0)
        su_ref[...] = jnp.where((lane >= ROT_DIM // 2) & (lane < ROT_DIM), sn, 0.0)
        sd_ref[...] = jnp.where(lane < ROT_DIM // 2, -sn, 0.0)

    return pl.pallas_call(
        body, name="rope_tables", out_shape=[jax.ShapeDtypeStruct((s, 128), F32)] * 3, grid=(s // tm,),
        in_specs=[pl.BlockSpec((tm, 1), lambda i: (i, 0)), _full((1, 128))],
        out_specs=[_rows(tm, 128)] * 3, compiler_params=_params("parallel"),
    )(pos_col, freq_row)


def _rope_apply(t, cos, sin_up, sin_dn):
    w = t.shape[1]
    return t * cos + pltpu.roll(t, 8, 1) * sin_up + pltpu.roll(t, w - 8, 1) * sin_dn


def _rope_transpose(dr, cos, sin_up, sin_dn):
    w = dr.shape[1]
    return dr * cos + pltpu.roll(dr * sin_up, w - 8, 1) + pltpu.roll(dr * sin_dn, 8, 1)


def _norm_matmul(x, g, wt, *, tn, name, bias=None, rope=None, rope_blocks=0, tm=512):
    s, d = x.shape
    n = wt.shape[0]
    tm = min(tm, s)

    def body(*refs):
        x_ref, g_ref, w_ref = refs[:3]
        k = 3
        b_ref = None
        if bias is not None:
            b_ref = refs[k]
            k += 1
        if rope is not None:
            c_ref, su_ref, sd_ref = refs[k:k + 3]
            k += 3
        h_ref, o_ref = refs[k:k + 2]
        j = pl.program_id(1)

        @pl.when(j == 0)
        def _():
            xv = x_ref[...]
            r = lax.rsqrt(jnp.mean(xv * xv, axis=-1, keepdims=True) + EPS)
            h_ref[...] = (xv * r * g_ref[...]).astype(BF16)

        acc = lax.dot_general(h_ref[...], w_ref[...], NT, preferred_element_type=F32)
        if b_ref is not None:
            acc = acc + b_ref[...]
        if rope is None:
            o_ref[...] = acc.astype(BF16)
        else:
            @pl.when(j < rope_blocks)
            def _():
                reps = tn // 128
                o_ref[...] = _rope_apply(acc, jnp.tile(c_ref[...], (1, reps)), jnp.tile(su_ref[...], (1, reps)),
                                         jnp.tile(sd_ref[...], (1, reps))).astype(BF16)

            @pl.when(j >= rope_blocks)
            def _():
                o_ref[...] = acc.astype(BF16)

    in_specs = [_rows(tm, d), _full((1, d)), pl.BlockSpec((tn, d), lambda i, j: (j, 0))]
    args = [x, g, wt]
    if bias is not None:
        in_specs.append(pl.BlockSpec((1, tn), lambda i, j: (0, j)))
        args.append(bias)
    if rope is not None:
        in_specs += [_rows(tm, 128)] * 3
        args += list(rope)
    return pl.pallas_call(
        body, name=name,
        out_shape=[jax.ShapeDtypeStruct((s, d), BF16), jax.ShapeDtypeStruct((s, n), BF16)],
        grid=(s // tm, n // tn), in_specs=in_specs,
        out_specs=[_rows(tm, d), pl.BlockSpec((tm, tn), lambda i, j: (i, j))],
        compiler_params=_params("parallel", "arbitrary"),
    )(*args)


def _class_major(tm, dil):
    p = np.zeros((tm, tm), np.float32)
    per = tm // dil
    for r in range(dil):
        for j in range(per):
            p[r * per + j, j * dil + r] = 1.0
    return jnp.asarray(p, dtype=BF16)


def _qkv_proj(x, g, wt, rope, tm=512):
    s, d = x.shape
    n = wt.shape[0]
    gw3 = 3 * GROUP_WIDTH
    tm = min(tm, s)
    assert n == 3 * gw3

    def body(x_ref, g_ref, w_ref, c_ref, su_ref, sd_ref, p1_ref, p2_ref, h_ref, o0_ref, o1_ref, o2_ref):
        j = pl.program_id(1)

        @pl.when(j == 0)
        def _():
            xv = x_ref[...]
            r = lax.rsqrt(jnp.mean(xv * xv, axis=-1, keepdims=True) + EPS)
            h_ref[...] = (xv * r * g_ref[...]).astype(BF16)

        acc = lax.dot_general(h_ref[...], w_ref[...], NT, preferred_element_type=F32)

        def store(y):
            yb = y.astype(BF16)
            o0_ref[:, pl.ds(pl.multiple_of(j * GROUP_WIDTH, GROUP_WIDTH), GROUP_WIDTH)] = yb[:, :GROUP_WIDTH]
            for grp, o_ref, p_ref in ((1, o1_ref, p1_ref), (2, o2_ref, p2_ref)):
                dil = DILATIONS[grp]
                per = tm // dil
                yp = jnp.dot(p_ref[...], yb[:, grp * GROUP_WIDTH:(grp + 1) * GROUP_WIDTH],
                             preferred_element_type=F32).astype(BF16)
                for r in range(dil):
                    col = pl.multiple_of(r * gw3 + j * GROUP_WIDTH, GROUP_WIDTH)
                    o_ref[:, pl.ds(col, GROUP_WIDTH)] = yp[r * per:(r + 1) * per, :]

        @pl.when(j < 2)
        def _():
            reps = gw3 // 128
            store(_rope_apply(acc, jnp.tile(c_ref[...], (1, reps)), jnp.tile(su_ref[...], (1, reps)),
                              jnp.tile(sd_ref[...], (1, reps))))

        @pl.when(j == 2)
        def _():
            store(acc)

    outs = [jax.ShapeDtypeStruct((s, d), BF16)] + [jax.ShapeDtypeStruct((s // dl, dl * gw3), BF16) for dl in DILATIONS]
    out_specs = [_rows(tm, d)] + [_rows(tm // dl, dl * gw3) for dl in DILATIONS]
    return pl.pallas_call(
        body, name="attn_qkv", out_shape=outs, grid=(s // tm, 3),
        in_specs=[_rows(tm, d), _full((1, d)), pl.BlockSpec((gw3, d), lambda i, j: (j, 0))] + [_rows(tm, 128)] * 3
        + [_full((tm, tm))] * 2,
        out_specs=out_specs, compiler_params=_params("parallel", "arbitrary"),
    )(x, g, wt, *rope, _class_major(tm, DILATIONS[1]), _class_major(tm, DILATIONS[2]))


def _head_masks(rows=SPAN):
    lane = lax.broadcasted_iota(jnp.int32, (rows, 128), 1)
    masks = [lane < HEAD_DIM, lane >= HEAD_DIM]
    lane1 = lax.broadcasted_iota(jnp.int32, (1, 128), 1)
    keep = [jnp.where(lane1 < HEAD_DIM, 1.0, 0.0).astype(BF16), jnp.where(lane1 >= HEAD_DIM, 1.0, 0.0).astype(BF16)]
    return masks, keep


def _attn_fwd(qv, grp, dil):
    l = qv.shape[0]
    s = l * dil
    nb = l // SPAN

    def body(q_ref, kp_ref, kc_ref, vp_ref, vc_ref, o_ref, l_ref):
        b = pl.program_id(1)
        row = lax.broadcasted_iota(jnp.int32, (SPAN, 2 * SPAN), 0)
        col = lax.broadcasted_iota(jnp.int32, (SPAN, 2 * SPAN), 1)
        no_prev = jnp.where(b > 0, 0, 4 * SPAN)
        valid = ((col < SPAN) & (col >= row + no_prev)) | ((col >= SPAN) & (col - SPAN <= row))
        masks, keep = _head_masks()
        for p in range(GROUP_WIDTH // 128):
            sl = slice(p * 128, (p + 1) * 128)
            qp = q_ref[:, sl]
            kk = jnp.concatenate([kp_ref[:, sl], kc_ref[:, sl]], axis=0)
            vv = jnp.concatenate([vp_ref[:, sl], vc_ref[:, sl]], axis=0)
            outs, lses = [], []
            for h in range(2):
                sc = lax.dot_general(qp * keep[h], kk, NT, preferred_element_type=F32) * (HEAD_DIM ** -0.5)
                sc = jnp.where(valid, sc, -1e30)
                mx = jnp.max(sc, axis=-1, keepdims=True)
                pe = jnp.exp(sc - mx)
                den = jnp.sum(pe, axis=-1, keepdims=True)
                pv = jnp.dot(pe.astype(BF16), vv, preferred_element_type=F32)
                outs.append(pv / den)
                lses.append(jnp.broadcast_to(mx + jnp.log(den), (SPAN, 128)))
            o_ref[:, sl] = jnp.where(masks[0], outs[0], outs[1])
            l_ref[:, sl] = jnp.where(masks[0], lses[0], lses[1])

    blk = (SPAN, GROUP_WIDTH)
    cur = lambda t: pl.BlockSpec(blk, lambda r, b: (b, r * 3 + t))
    prev = lambda t: pl.BlockSpec(blk, lambda r, b: (jnp.maximum(b - 1, 0), r * 3 + t))
    out = pl.BlockSpec(blk, lambda r, b: (b, r))
    o, lse = pl.pallas_call(
        body, name=f"attn_fwd_g{grp}", out_shape=[jax.ShapeDtypeStruct((l, dil * GROUP_WIDTH), F32)] * 2,
        grid=(dil, nb), in_specs=[cur(0), prev(1), cur(1), prev(2), cur(2)], out_specs=[out, out],
        compiler_params=_params("parallel", "arbitrary"),
    )(qv, qv, qv, qv, qv)
    return o.reshape(s, GROUP_WIDTH), lse.reshape(s, GROUP_WIDTH)


def _resnorm_store(y, x_ref, g_ref, y_ref, xo_ref):
    r = lax.rsqrt(jnp.mean(y * y, axis=-1, keepdims=True) + EPS)
    y_ref[...] = y
    xo_ref[...] = x_ref[...] + y * r * g_ref[...]


def _mix_wo(os_, ls_, wot, x, g, tm=256):
    s, d = x.shape
    gw = wot.shape[1]
    tm = min(tm, s)

    def body(o0, o1, o2, l0, l1, l2, w_ref, x_ref, g_ref, y_ref, xo_ref, mixed_ref, lse_ref):
        a0, a1, a2 = l0[...], l1[...], l2[...]
        mx = jnp.maximum(jnp.maximum(a0, a1), a2)
        e0, e1, e2 = jnp.exp(a0 - mx), jnp.exp(a1 - mx), jnp.exp(a2 - mx)
        den = e0 + e1 + e2
        mixed = (e0 / den) * o0[...] + (e1 / den) * o1[...] + (e2 / den) * o2[...]
        mixed_ref[...] = mixed.astype(BF16)
        lse_ref[...] = mx + jnp.log(den)
        y = lax.dot_general(mixed.astype(BF16), w_ref[...], NT, preferred_element_type=F32)
        _resnorm_store(y, x_ref, g_ref, y_ref, xo_ref)

    return pl.pallas_call(
        body, name="mix_wo",
        out_shape=[jax.ShapeDtypeStruct((s, d), F32), jax.ShapeDtypeStruct((s, d), F32),
                   jax.ShapeDtypeStruct((s, gw), BF16), jax.ShapeDtypeStruct((s, gw), F32)],
        grid=(s // tm,), in_specs=[_rows(tm, gw)] * 6 + [_full((d, gw)), _rows(tm, d), _full((1, d))],
        out_specs=[_rows(tm, d), _rows(tm, d), _rows(tm, gw), _rows(tm, gw)],
        compiler_params=_params("parallel"),
    )(*os_, *ls_, wot, x, g)


def _matmul_resnorm(a, w, x, g, *, name, bias=None, tm=512):
    s, k = a.shape
    d = w.shape[1]
    tm = min(tm, s)

    def body(*refs):
        a_ref, w_ref = refs[:2]
        b_ref = refs[2] if bias is not None else None
        x_ref, g_ref, y_ref, xo_ref = refs[-4:]
        y = jnp.dot(a_ref[...], w_ref[...], preferred_element_type=F32)
        if b_ref is not None:
            y = y + b_ref[...]
        _resnorm_store(y, x_ref, g_ref, y_ref, xo_ref)

    in_specs = [_rows(tm, k), _full((k, d))] + ([_full((1, d))] if bias is not None else []) + [_rows(tm, d), _full((1, d))]
    args = [a, w] + ([bias] if bias is not None else []) + [x, g]
    return pl.pallas_call(
        body, name=name, out_shape=[jax.ShapeDtypeStruct((s, d), F32)] * 2, grid=(s // tm,),
        in_specs=in_specs, out_specs=[_rows(tm, d)] * 2, compiler_params=_params("parallel"),
    )(*args)


def _conv3_taps(z, halo, first):
    row = lax.broadcasted_iota(jnp.int32, (8, z.shape[1]), 0)
    halo = halo * jnp.where(first, 0.0, 1.0)
    h6, h7 = halo[6:7, :], halo[7:8, :]
    r1, r2 = pltpu.roll(z, 1, 0), pltpu.roll(z, 2, 0)
    z1 = jnp.concatenate([jnp.where(row == 0, h7, r1[0:8]), r1[8:]], axis=0)
    z2 = jnp.concatenate([jnp.where(row == 0, h6, jnp.where(row == 1, h7, r2[0:8])), r2[8:]], axis=0)
    return z2, z1


def _ffn_cols(f):
    return _tile(f)


def _lane_chunks(width, fn):
    def step(k, carry):
        fn(pl.ds(pl.multiple_of(k * 128, 128), 128))
        return carry

    lax.fori_loop(0, width // 128, step, 0)


def _ffn_act(z, w_dw, b_dw, tm=256):
    s, f2 = z.shape
    f = f2 // 2
    tm = min(tm, s)
    tc = _ffn_cols(f)
    nfc = f // tc

    def body(zu, zg, hu, hg, wu, wg, bu, bg, o_ref):
        first = pl.program_id(0) == 0

        def chunk(cs):
            def conv(z_ref, h_ref, w_ref, b_ref):
                zc = z_ref[:, cs].astype(F32)
                z2, z1 = _conv3_taps(zc, h_ref[:, cs].astype(F32), first)
                return w_ref[0:1, cs] * z2 + w_ref[1:2, cs] * z1 + w_ref[2:3, cs] * zc + b_ref[:, cs]

            up, gate = conv(zu, hu, wu, bu), conv(zg, hg, wg, bg)
            o_ref[:, cs] = (gate * _sigmoid(gate) * up).astype(BF16)

        _lane_chunks(tc, chunk)

    hb = tm // 8
    tile = lambda off: pl.BlockSpec((tm, tc), lambda i, j: (i, off + j))
    halo = lambda off: pl.BlockSpec((8, tc), lambda i, j: (jnp.maximum(i * hb - 1, 0), off + j))
    prm = lambda rows, off: pl.BlockSpec((rows, tc), lambda i, j: (0, off + j))
    return pl.pallas_call(
        body, name="ffn_act", out_shape=jax.ShapeDtypeStruct((s, f), BF16), grid=(s // tm, nfc),
        in_specs=[tile(0), tile(nfc), halo(0), halo(nfc), prm(FFN_CONV, 0), prm(FFN_CONV, nfc), prm(1, 0), prm(1, nfc)],
        out_specs=pl.BlockSpec((tm, tc), lambda i, j: (i, j)), compiler_params=_params("parallel", "parallel"),
    )(z, z, z, z, w_dw, w_dw, b_dw, b_dw)


def _shifted_planes(ext_ref):
    rows = ext_ref.shape[1]
    for s in range(1, 8):
        ext_ref[s, 0:rows - 8, :] = ext_ref[0, s:s + rows - 8, :]


def _window(ext_ref, off, tm, cs):
    s = off % 8
    return ext_ref[s, off - s:off - s + tm, cs]


def _conv_taps(ext_ref, w_ref, offs, tm, out_ref):
    def chunk(cs):
        acc = w_ref[0:1, cs] * _window(ext_ref, offs[0], tm, cs)
        for j in range(1, len(offs)):
            acc = acc + w_ref[j:j + 1, cs] * _window(ext_ref, offs[j], tm, cs)
        out_ref[:, cs] = acc

    _lane_chunks(out_ref.shape[1], chunk)


def _glu_planes(ag_ref, halo_ref, ext_ref, first, c):
    hal = halo_ref[...].astype(F32)
    ext_ref[0, 0:CONV_HALO, :] = hal[:, :c] * _sigmoid(hal[:, c:]) * jnp.where(first, 0.0, 1.0)
    ag = ag_ref[...].astype(F32)
    ext_ref[0, CONV_HALO:, :] = ag[:, :c] * _sigmoid(ag[:, c:])
    _shifted_planes(ext_ref)


def _layernorm_stats(u1):
    mu = jnp.mean(u1, axis=-1, keepdims=True)
    cen = u1 - mu
    rstd = lax.rsqrt(jnp.mean(cen * cen, axis=-1, keepdims=True) + EPS)
    return cen * rstd, rstd


def _conv_mid(ag, w_dw, b_dw, ln_g, ln_b, tm=256):
    s, c2 = ag.shape
    c = c2 // 2
    tm = min(tm, s)

    def body(ag_ref, halo_ref, w_ref, b_ref, g_ref, bb_ref, o_ref, u1_ref, ext_ref):
        _glu_planes(ag_ref, halo_ref, ext_ref, pl.program_id(0) == 0, c)
        base = CONV_HALO - (CONV_KERNEL - 1)
        _conv_taps(ext_ref, w_ref, [base + j for j in range(CONV_KERNEL)], tm, u1_ref)
        xh, _ = _layernorm_stats(u1_ref[...] + b_ref[...])
        u2 = xh * g_ref[...] + bb_ref[...]
        o_ref[...] = (u2 * _sigmoid(u2)).astype(BF16)

    hb = tm // CONV_HALO
    return pl.pallas_call(
        body, name="conv_mid", out_shape=[jax.ShapeDtypeStruct((s, c), BF16), jax.ShapeDtypeStruct((s, c), F32)], grid=(s // tm,),
        in_specs=[_rows(tm, c2), pl.BlockSpec((CONV_HALO, c2), lambda i: (jnp.maximum(i * hb - 1, 0), 0)),
                  _full((CONV_KERNEL, c)), _full((1, c)), _full((1, c)), _full((1, c))],
        out_specs=[_rows(tm, c), _rows(tm, c)], scratch_shapes=[pltpu.VMEM((8, CONV_HALO + tm, c), F32)],
        compiler_params=_params("arbitrary"),
    )(ag, ag, w_dw, b_dw, ln_g, ln_b)


def _loss_grad(xo, target, tm=512):
    s, d = xo.shape
    tm = min(tm, s)

    def body(x_ref, t_ref, dx_ref, loss_ref):
        @pl.when(pl.program_id(0) == 0)
        def _():
            loss_ref[...] = jnp.zeros_like(loss_ref)

        err = x_ref[...] - t_ref[...]
        dx_ref[...] = err * (1.0 / d)
        loss_ref[...] += 0.5 * jnp.sum(jnp.mean(err * err, axis=-1, keepdims=True))

    return pl.pallas_call(
        body, name="loss_grad", out_shape=[jax.ShapeDtypeStruct((s, d), F32), jax.ShapeDtypeStruct((1, 128), F32)],
        grid=(s // tm,), in_specs=[_rows(tm, d)] * 2, out_specs=[_rows(tm, d), _full((1, 128))],
        compiler_params=_params("arbitrary"),
    )(xo, target)


def _postnorm_bwd(y, g, dxo, *, name, with_bias_grad=False, tm=512):
    s, d = y.shape
    tm = min(tm, s)

    def body(y_ref, g_ref, dx_ref, dy_ref, dg_ref, *rest):
        @pl.when(pl.program_id(0) == 0)
        def _():
            dg_ref[...] = jnp.zeros_like(dg_ref)
            for r_ in rest:
                r_[...] = jnp.zeros_like(r_)

        yv, dxo_v = y_ref[...], dx_ref[...]
        r = lax.rsqrt(jnp.mean(yv * yv, axis=-1, keepdims=True) + EPS)
        yh = yv * r
        dyh = dxo_v * g_ref[...]
        dy = r * (dyh - yh * jnp.mean(dyh * yh, axis=-1, keepdims=True))
        dy_ref[...] = dy.astype(BF16)
        dg_ref[...] += jnp.sum(dxo_v * yh, axis=0, keepdims=True)
        for r_ in rest:
            r_[...] += jnp.sum(dy, axis=0, keepdims=True)

    nacc = 2 if with_bias_grad else 1
    return pl.pallas_call(
        body, name=name, out_shape=[jax.ShapeDtypeStruct((s, d), BF16)] + [jax.ShapeDtypeStruct((1, d), F32)] * nacc,
        grid=(s // tm,), in_specs=[_rows(tm, d), _full((1, d)), _rows(tm, d)],
        out_specs=[_rows(tm, d)] + [_full((1, d))] * nacc, compiler_params=_params("arbitrary"),
    )(y, g, dxo)


def _matmul(gmat, w, *, name, out_dtype, transposed_w, tm=512):
    s, k = gmat.shape
    n = w.shape[0] if transposed_w else w.shape[1]
    tm = min(tm, s)

    def body(g_ref, w_ref, o_ref):
        if transposed_w:
            acc = lax.dot_general(g_ref[...], w_ref[...], NT, preferred_element_type=F32)
        else:
            acc = jnp.dot(g_ref[...], w_ref[...], preferred_element_type=F32)
        o_ref[...] = acc.astype(out_dtype)

    return pl.pallas_call(
        body, name=name, out_shape=jax.ShapeDtypeStruct((s, n), out_dtype), grid=(s // tm,),
        in_specs=[_rows(tm, k), _full(w.shape)], out_specs=_rows(tm, n), compiler_params=_params("parallel"),
    )(gmat, w)


def _matmul_prenorm_bwd(pieces, wt, x, g, dres, *, name, tm=256):
    s, d = x.shape
    tm = min(tm, s)
    np_ = len(pieces)

    def body(*refs):
        p_refs, w_refs = refs[:np_], refs[np_:2 * np_]
        x_ref, g_ref, r_ref, dx_ref, dg_ref = refs[2 * np_:]

        @pl.when(pl.program_id(0) == 0)
        def _():
            dg_ref[...] = jnp.zeros_like(dg_ref)

        dh = None
        for p_ref, w_ref in zip(p_refs, w_refs):
            t = jnp.dot(p_ref[...], w_ref[...], preferred_element_type=F32)
            dh = t if dh is None else dh + t
        xv = x_ref[...]
        r = lax.rsqrt(jnp.mean(xv * xv, axis=-1, keepdims=True) + EPS)
        xh = xv * r
        dyh = dh * g_ref[...]
        dx_ref[...] = r_ref[...] + r * (dyh - xh * jnp.mean(dyh * xh, axis=-1, keepdims=True))
        dg_ref[...] += jnp.sum(dh * xh, axis=0, keepdims=True)

    in_specs = []
    for _, c0, kc, _ in pieces:
        assert c0 % kc == 0
        in_specs.append(pl.BlockSpec((tm, kc), lambda i, _b=c0 // kc: (i, _b)))
    for _, _, kc, r0 in pieces:
        assert r0 % kc == 0
        in_specs.append(pl.BlockSpec((kc, d), lambda i, _b=r0 // kc: (_b, 0)))
    in_specs += [_rows(tm, d), _full((1, d)), _rows(tm, d)]
    return pl.pallas_call(
        body, name=name, out_shape=[jax.ShapeDtypeStruct((s, d), F32), jax.ShapeDtypeStruct((1, d), F32)],
        grid=(s // tm,), in_specs=in_specs, out_specs=[_rows(tm, d), _full((1, d))],
        compiler_params=_params("arbitrary"),
    )(*[p[0] for p in pieces], *[wt] * np_, x, g, dres)


def _weight_grad(a, gmat, *, name, a_col0=0, ka=None, out=None, out_shape=None, layer=0, row0=0, ts=1024):
    s = a.shape[0]
    ka = a.shape[1] if ka is None else ka
    n = gmat.shape[1]
    ts = min(ts, s)
    tka = _tile(ka, a_col0, row0)
    shape = out.shape if out is not None else out_shape
    nsteps = s // ts

    def body(a_ref, g_ref, *rest):
        o_ref, acc_ref = rest[-2:]
        i = pl.program_id(1)

        @pl.when(i == 0)
        def _():
            acc_ref[...] = jnp.zeros_like(acc_ref)

        acc_ref[...] += lax.dot_general(a_ref[...], g_ref[...], TN, preferred_element_type=F32)

        @pl.when(i == nsteps - 1)
        def _():
            o_ref[...] = acc_ref[...].astype(BF16)

    in_specs = [pl.BlockSpec((ts, tka), lambda k, i: (i, a_col0 // tka + k)), pl.BlockSpec((ts, n), lambda k, i: (i, 0))]
    args = [a, gmat]
    aliases = {}
    if out is not None:
        in_specs.append(ANY)
        args.append(out)
        aliases = {2: 0}
    return pl.pallas_call(
        body, name=name, out_shape=jax.ShapeDtypeStruct(shape, BF16), grid=(ka // tka, nsteps), in_specs=in_specs,
        out_specs=pl.BlockSpec((None, tka, n), lambda k, i: (layer, row0 // tka + k, 0)),
        scratch_shapes=[pltpu.VMEM((tka, n), F32)],
        input_output_aliases=aliases, compiler_params=_params("parallel", "arbitrary"),
    )(*args)


def _ffn_act_bwd(z, dact, w_dw, b_dw, tm=256):
    s, f2 = z.shape
    f = f2 // 2
    tm = min(tm, s)
    tc = _ffn_cols(f)
    nfc = f // tc

    def body(zu, zg, hu, hg, wu, wg, bu, bg, da_ref, du_ref, dgt_ref, dbu_ref, dbg_ref, dwu_ref, dwg_ref):
        i = pl.program_id(1)

        @pl.when(i == 0)
        def _():
            for r_ in (dbu_ref, dbg_ref, dwu_ref, dwg_ref):
                r_[...] = jnp.zeros_like(r_)

        def chunk(cs):
            def conv(z_ref, h_ref, w_ref, b_ref):
                zc = z_ref[:, cs].astype(F32)
                z2, z1 = _conv3_taps(zc, h_ref[:, cs].astype(F32), i == 0)
                return (z2, z1, zc), w_ref[0:1, cs] * z2 + w_ref[1:2, cs] * z1 + w_ref[2:3, cs] * zc + b_ref[:, cs]

            taps_u, up = conv(zu, hu, wu, bu)
            taps_g, gate = conv(zg, hg, wg, bg)
            da = da_ref[:, cs].astype(F32)
            sg = _sigmoid(gate)
            d_up = da * (gate * sg)
            d_gate = da * up * (sg * (1.0 + gate * (1.0 - sg)))
            du_ref[:, cs] = d_up.astype(BF16)
            dgt_ref[:, cs] = d_gate.astype(BF16)
            for dv, taps, db_ref, dw_ref in ((d_up, taps_u, dbu_ref, dwu_ref), (d_gate, taps_g, dbg_ref, dwg_ref)):
                db_ref[:, cs] += jnp.sum(dv, axis=0, keepdims=True)
                for k_, tap in enumerate(taps):
                    dw_ref[k_:k_ + 1, cs] += jnp.sum(dv * tap, axis=0, keepdims=True)

        _lane_chunks(tc, chunk)

    hb = tm // 8
    tile = lambda off: pl.BlockSpec((tm, tc), lambda j, i: (i, off + j))
    halo = lambda off: pl.BlockSpec((8, tc), lambda j, i: (jnp.maximum(i * hb - 1, 0), off + j))
    prm = lambda rows, off: pl.BlockSpec((rows, tc), lambda j, i: (0, off + j))
    acc = lambda rows: pl.BlockSpec((rows, tc), lambda j, i: (0, j))
    return pl.pallas_call(
        body, name="ffn_act_bwd",
        out_shape=[jax.ShapeDtypeStruct((s, f), BF16)] * 2 + [jax.ShapeDtypeStruct((1, f), F32)] * 2
        + [jax.ShapeDtypeStruct((FFN_CONV, f), F32)] * 2,
        grid=(nfc, s // tm),
        in_specs=[tile(0), tile(nfc), halo(0), halo(nfc), prm(FFN_CONV, 0), prm(FFN_CONV, nfc), prm(1, 0), prm(1, nfc), tile(0)],
        out_specs=[tile(0), tile(0), acc(1), acc(1), acc(FFN_CONV), acc(FFN_CONV)],
        compiler_params=_params("parallel", "arbitrary"),
    )(z, z, z, z, w_dw, w_dw, b_dw, b_dw, dact)


def _conv3_transpose(dug, w_dw, col0, tm=256):
    s, f = dug.shape
    tm = min(tm, s)
    tc = _ffn_cols(f)
    nfc = f // tc
    nrow = s // tm
    off = col0 // tc

    def body(d_ref, n_ref, w_ref, o_ref):
        keep_next = jnp.where(pl.program_id(0) == nrow - 1, 0.0, 1.0)

        def chunk(cs):
            dv = d_ref[:, cs].astype(F32)
            nxt = n_ref[:, cs].astype(F32) * keep_next
            n0, n1 = nxt[0:1, :], nxt[1:2, :]
            row = lax.broadcasted_iota(jnp.int32, (8, dv.shape[1]), 0)
            r1, r2 = pltpu.roll(dv, tm - 1, 0), pltpu.roll(dv, tm - 2, 0)
            d1 = jnp.concatenate([r1[:tm - 8], jnp.where(row == 7, n0, r1[tm - 8:])], axis=0)
            d2 = jnp.concatenate([r2[:tm - 8], jnp.where(row == 7, n1, jnp.where(row == 6, n0, r2[tm - 8:]))], axis=0)
            o_ref[:, cs] = (w_ref[2:3, cs] * dv + w_ref[1:2, cs] * d1 + w_ref[0:1, cs] * d2).astype(BF16)

        _lane_chunks(tc, chunk)

    hb = tm // 8
    return pl.pallas_call(
        body, name="conv3_transpose", out_shape=jax.ShapeDtypeStruct((s, f), BF16), grid=(nrow, nfc),
        in_specs=[pl.BlockSpec((tm, tc), lambda i, j: (i, j)),
                  pl.BlockSpec((8, tc), lambda i, j: (jnp.minimum((i + 1) * hb, s // 8 - 1), j)),
                  pl.BlockSpec((FFN_CONV, tc), lambda i, j: (0, off + j))],
        out_specs=pl.BlockSpec((tm, tc), lambda i, j: (i, j)), compiler_params=_params("parallel", "parallel"),
    )(dug, dug, w_dw)


def _conv_mid_bwd(ag, u1, du3, b_dw, ln_g, ln_b, tm=256):
    s, c2 = ag.shape
    c = c2 // 2
    tm = min(tm, s)

    def body(ag_ref, halo_ref, u1in_ref, du_ref, b_ref, g_ref, bb_ref, o_ref, dlg_ref, dlb_ref, db_ref, dw_ref, ext_ref, u1_ref):
        @pl.when(pl.program_id(0) == 0)
        def _():
            for r_ in (dlg_ref, dlb_ref, db_ref, dw_ref):
                r_[...] = jnp.zeros_like(r_)

        _glu_planes(ag_ref, halo_ref, ext_ref, pl.program_id(0) == 0, c)
        xh, rstd = _layernorm_stats(u1in_ref[...] + b_ref[...])
        u2 = xh * g_ref[...] + bb_ref[...]
        sg = _sigmoid(u2)
        du2 = du_ref[...] * (sg * (1.0 + u2 * (1.0 - sg)))
        dlg_ref[...] += jnp.sum(du2 * xh, axis=0, keepdims=True)
        dlb_ref[...] += jnp.sum(du2, axis=0, keepdims=True)
        dxh = du2 * g_ref[...]
        du1 = rstd * (dxh - jnp.mean(dxh, axis=-1, keepdims=True) - xh * jnp.mean(dxh * xh, axis=-1, keepdims=True))
        o_ref[...] = du1.astype(BF16)
        db_ref[...] += jnp.sum(du1, axis=0, keepdims=True)
        u1_ref[...] = du1
        base = CONV_HALO - (CONV_KERNEL - 1)

        def chunk(cs):
            dc = u1_ref[:, cs]
            for j in range(CONV_KERNEL):
                dw_ref[j:j + 1, cs] += jnp.sum(dc * _window(ext_ref, base + j, tm, cs), axis=0, keepdims=True)

        _lane_chunks(c, chunk)

    hb = tm // CONV_HALO
    vec = _full((1, c))
    return pl.pallas_call(
        body, name="conv_mid_bwd",
        out_shape=[jax.ShapeDtypeStruct((s, c), BF16)] + [jax.ShapeDtypeStruct((1, c), F32)] * 3
        + [jax.ShapeDtypeStruct((CONV_HALO, c), F32)],
        grid=(s // tm,),
        in_specs=[_rows(tm, c2), pl.BlockSpec((CONV_HALO, c2), lambda i: (jnp.maximum(i * hb - 1, 0), 0)), _rows(tm, c),
                  _rows(tm, c), vec, vec, vec],
        out_specs=[_rows(tm, c), vec, vec, vec, _full((CONV_HALO, c))],
        scratch_shapes=[pltpu.VMEM((8, CONV_HALO + tm, c), F32), pltpu.VMEM((tm, c), F32)],
        compiler_params=_params("arbitrary"),
    )(ag, ag, u1, du3, b_dw, ln_g, ln_b)


def _glu_conv_bwd(du1, ag, w_dw, tm=256):
    s, c = du1.shape
    tm = min(tm, s)
    nrow = s // tm

    def body(d_ref, n_ref, ag_ref, w_ref, o_ref, db_ref, ext_ref, du0_ref):
        @pl.when(pl.program_id(0) == 0)
        def _():
            db_ref[...] = jnp.zeros_like(db_ref)

        ext_ref[0, 0:tm, :] = d_ref[...].astype(F32)
        ext_ref[0, tm:, :] = n_ref[...].astype(F32) * jnp.where(pl.program_id(0) == nrow - 1, 0.0, 1.0)
        _shifted_planes(ext_ref)
        top = CONV_KERNEL - 1
        _conv_taps(ext_ref, w_ref, [top - j for j in range(CONV_KERNEL)], tm, du0_ref)
        du0 = du0_ref[...]
        ag = ag_ref[...].astype(F32)
        a, gt = ag[:, :c], ag[:, c:]
        sg = _sigmoid(gt)
        da = du0 * sg
        dgt = du0 * a * (sg * (1.0 - sg))
        o_ref[:, :c] = da.astype(BF16)
        o_ref[:, c:] = dgt.astype(BF16)
        db_ref[:, :c] += jnp.sum(da, axis=0, keepdims=True)
        db_ref[:, c:] += jnp.sum(dgt, axis=0, keepdims=True)

    hb = tm // CONV_HALO
    return pl.pallas_call(
        body, name="glu_conv_bwd",
        out_shape=[jax.ShapeDtypeStruct((s, 2 * c), BF16), jax.ShapeDtypeStruct((1, 2 * c), F32)], grid=(nrow,),
        in_specs=[_rows(tm, c), pl.BlockSpec((CONV_HALO, c), lambda i: (jnp.minimum((i + 1) * hb, s // CONV_HALO - 1), 0)),
                  _rows(tm, 2 * c), _full((CONV_KERNEL, c))],
        out_specs=[_rows(tm, 2 * c), _full((1, 2 * c))],
        scratch_shapes=[pltpu.VMEM((8, tm + CONV_HALO, c), F32), pltpu.VMEM((tm, c), F32)],
        compiler_params=_params("arbitrary"),
    )(du1, du1, ag, w_dw)


def _head_rows(v, mask):
    return jnp.max(jnp.where(mask, v, -jnp.inf), axis=-1, keepdims=True)


def _attn_bwd(qv, dmix, mixed, lse, rope, grp, dil):
    l = qv.shape[0]
    s = l * dil
    nb = l // SPAN
    view = lambda t: t.reshape(l, dil * t.shape[1])
    scale = HEAD_DIM ** -0.5
    gw = GROUP_WIDTH

    def body(q_ref, kp_ref, kc_ref, vp_ref, vc_ref, do_ref, mx_ref, l_ref, c_ref, su_ref, sd_ref, cp_ref, sup_ref, sdp_ref,
             dq_ref, dkv_ref, carry_ref):
        b = pl.program_id(1)
        prev_tabs = (cp_ref, sup_ref, sdp_ref)

        @pl.when(b < nb)
        def _():
            row = lax.broadcasted_iota(jnp.int32, (SPAN, 2 * SPAN), 0)
            col = lax.broadcasted_iota(jnp.int32, (SPAN, 2 * SPAN), 1)
            no_prev = jnp.where(b > 0, 0, 4 * SPAN)
            valid = ((col < SPAN) & (col >= row + no_prev)) | ((col >= SPAN) & (col - SPAN <= row))
            masks, keep = _head_masks()
            masks2, _ = _head_masks(2 * SPAN)
            for p in range(gw // 128):
                sl = slice(p * 128, (p + 1) * 128)
                sl_v = slice(gw + p * 128, gw + (p + 1) * 128)
                qp, dop = q_ref[:, sl], do_ref[:, sl]
                kk = jnp.concatenate([kp_ref[:, sl], kc_ref[:, sl]], axis=0)
                vv = jnp.concatenate([vp_ref[:, sl], vc_ref[:, sl]], axis=0)
                prod = dop.astype(F32) * mx_ref[:, sl].astype(F32)
                lsep = l_ref[:, sl]
                dqs, dks, dvs = [], [], []
                for h in range(2):
                    qh, doh = qp * keep[h], dop * keep[h]
                    sc = lax.dot_general(qh, kk, NT, preferred_element_type=F32) * scale
                    pe = jnp.where(valid, jnp.exp(sc - _head_rows(lsep, masks[h])), 0.0)
                    dp = lax.dot_general(doh, vv, NT, preferred_element_type=F32)
                    dbar = jnp.sum(jnp.where(masks[h], prod, 0.0), axis=-1, keepdims=True)
                    ds = (pe * (dp - dbar) * scale).astype(BF16)
                    dqs.append(jnp.dot(ds, kk, preferred_element_type=F32))
                    dks.append(lax.dot_general(ds, qp, TN, preferred_element_type=F32))
                    dvs.append(lax.dot_general(pe.astype(BF16), dop, TN, preferred_element_type=F32))
                dq = jnp.where(masks[0], dqs[0], dqs[1])
                dq_ref[:, sl] = _rope_transpose(dq, c_ref[...], su_ref[...], sd_ref[...]).astype(BF16)
                dk = jnp.where(masks2[0], dks[0], dks[1])
                dv = jnp.where(masks2[0], dvs[0], dvs[1])

                @pl.when(b > 0)
                def _():
                    dk_prev = carry_ref[:, sl] + dk[:SPAN]
                    dkv_ref[:, sl] = _rope_transpose(dk_prev, *[t[...] for t in prev_tabs]).astype(BF16)
                    dkv_ref[:, sl_v] = (carry_ref[:, sl_v] + dv[:SPAN]).astype(BF16)

                carry_ref[:, sl] = dk[SPAN:]
                carry_ref[:, sl_v] = dv[SPAN:]

        @pl.when(b == nb)
        def _():
            for p in range(gw // 128):
                sl = slice(p * 128, (p + 1) * 128)
                sl_v = slice(gw + p * 128, gw + (p + 1) * 128)
                dkv_ref[:, sl] = _rope_transpose(carry_ref[:, sl], *[t[...] for t in prev_tabs]).astype(BF16)
                dkv_ref[:, sl_v] = carry_ref[:, sl_v].astype(BF16)

    blk = (SPAN, gw)
    cb = lambda b: jnp.minimum(b, nb - 1)
    cur = lambda t: pl.BlockSpec(blk, lambda r, b: (cb(b), r * 3 + t))
    prev = lambda t: pl.BlockSpec(blk, lambda r, b: (jnp.maximum(cb(b) - 1, 0), r * 3 + t))
    own = pl.BlockSpec(blk, lambda r, b: (cb(b), r))
    tab = pl.BlockSpec((SPAN, 128), lambda r, b: (cb(b), r))
    tab_prev = pl.BlockSpec((SPAN, 128), lambda r, b: (jnp.maximum(b - 1, 0), r))
    tabs = [view(t) for t in rope]
    dq, dkv = pl.pallas_call(
        body, name=f"attn_bwd_g{grp}",
        out_shape=[jax.ShapeDtypeStruct((l, dil * gw), BF16), jax.ShapeDtypeStruct((l, dil * 2 * gw), BF16)],
        grid=(dil, nb + 1),
        in_specs=[cur(0), prev(1), cur(1), prev(2), cur(2), own, own, own, tab, tab, tab, tab_prev, tab_prev, tab_prev],
        out_specs=[own, pl.BlockSpec((SPAN, 2 * gw), lambda r, b: (jnp.maximum(b - 1, 0), r))],
        scratch_shapes=[pltpu.VMEM((SPAN, 2 * gw), F32)], compiler_params=_params("parallel", "arbitrary"),
    )(qv, qv, qv, qv, qv, view(dmix), view(mixed), view(lse), *tabs, *tabs)
    return dq.reshape(s, gw), dkv.reshape(s, 2 * gw)


def _attn_bwd_dq(qv, dmix, mixed, lse, rope, grp, dil):
    l = qv.shape[0]
    s = l * dil
    nb = l // SPAN
    view = lambda t: t.reshape(l, dil * t.shape[1])

    def body(q_ref, kp_ref, kc_ref, vp_ref, vc_ref, do_ref, mx_ref, l_ref, c_ref, su_ref, sd_ref, o_ref):
        b = pl.program_id(1)
        row = lax.broadcasted_iota(jnp.int32, (SPAN, 2 * SPAN), 0)
        col = lax.broadcasted_iota(jnp.int32, (SPAN, 2 * SPAN), 1)
        no_prev = jnp.where(b > 0, 0, 4 * SPAN)
        valid = ((col < SPAN) & (col >= row + no_prev)) | ((col >= SPAN) & (col - SPAN <= row))
        masks, keep = _head_masks()
        for p in range(GROUP_WIDTH // 128):
            sl = slice(p * 128, (p + 1) * 128)
            qp, dop = q_ref[:, sl], do_ref[:, sl]
            kk = jnp.concatenate([kp_ref[:, sl], kc_ref[:, sl]], axis=0)
            vv = jnp.concatenate([vp_ref[:, sl], vc_ref[:, sl]], axis=0)
            prod = dop.astype(F32) * mx_ref[:, sl].astype(F32)
            lsep = l_ref[:, sl]
            dqs = []
            for h in range(2):
                qh, doh = qp * keep[h], dop * keep[h]
                sc = lax.dot_general(qh, kk, NT, preferred_element_type=F32) * (HEAD_DIM ** -0.5)
                pe = jnp.where(valid, jnp.exp(sc - _head_rows(lsep, masks[h])), 0.0)
                dp = lax.dot_general(doh, vv, NT, preferred_element_type=F32)
                dbar = jnp.sum(jnp.where(masks[h], prod, 0.0), axis=-1, keepdims=True)
                ds = pe * (dp - dbar) * (HEAD_DIM ** -0.5)
                dqs.append(jnp.dot(ds.astype(BF16), kk, preferred_element_type=F32))
            dq = jnp.where(masks[0], dqs[0], dqs[1])
            o_ref[:, sl] = _rope_transpose(dq, c_ref[...], su_ref[...], sd_ref[...]).astype(BF16)

    blk = (SPAN, GROUP_WIDTH)
    cur = lambda t: pl.BlockSpec(blk, lambda r, b: (b, r * 3 + t))
    prev = lambda t: pl.BlockSpec(blk, lambda r, b: (jnp.maximum(b - 1, 0), r * 3 + t))
    own = pl.BlockSpec(blk, lambda r, b: (b, r))
    tab = pl.BlockSpec((SPAN, 128), lambda r, b: (b, r))
    out = pl.pallas_call(
        body, name=f"attn_bwd_dq_g{grp}", out_shape=jax.ShapeDtypeStruct((l, dil * GROUP_WIDTH), BF16), grid=(dil, nb),
        in_specs=[cur(0), prev(1), cur(1), prev(2), cur(2), own, own, own, tab, tab, tab], out_specs=own,
        compiler_params=_params("parallel", "arbitrary"),
    )(qv, qv, qv, qv, qv, view(dmix), view(mixed), view(lse), *[view(t) for t in rope])
    return out.reshape(s, GROUP_WIDTH)


def _attn_bwd_dkv(qv, dmix, mixed, lse, rope, grp, dil):
    l = qv.shape[0]
    s = l * dil
    nb = l // SPAN
    view = lambda t: t.reshape(l, dil * t.shape[1])

    def body(k_ref, v_ref, qc_ref, qn_ref, doc_ref, don_ref, mc_ref, mn_ref, lc_ref, ln_ref,
             c_ref, su_ref, sd_ref, o_ref):
        b = pl.program_id(1)
        row = lax.broadcasted_iota(jnp.int32, (2 * SPAN, SPAN), 0)
        col = lax.broadcasted_iota(jnp.int32, (2 * SPAN, SPAN), 1)
        no_next = jnp.where(b < nb - 1, 0, 4 * SPAN)
        valid = ((row < SPAN) & (col <= row)) | ((row >= SPAN) & (col >= row - SPAN + no_next))
        masks, keep = _head_masks()
        masks2, _ = _head_masks(2 * SPAN)
        for p in range(GROUP_WIDTH // 128):
            sl = slice(p * 128, (p + 1) * 128)
            kp, vp = k_ref[:, sl], v_ref[:, sl]
            qq = jnp.concatenate([qc_ref[:, sl], qn_ref[:, sl]], axis=0)
            doo = jnp.concatenate([doc_ref[:, sl], don_ref[:, sl]], axis=0)
            mm = jnp.concatenate([mc_ref[:, sl], mn_ref[:, sl]], axis=0)
            ll = jnp.concatenate([lc_ref[:, sl], ln_ref[:, sl]], axis=0)
            prod = doo.astype(F32) * mm.astype(F32)
            dks, dvs = [], []
            for h in range(2):
                qh, doh = qq * keep[h], doo * keep[h]
                sc = lax.dot_general(qh, kp, NT, preferred_element_type=F32) * (HEAD_DIM ** -0.5)
                pe = jnp.where(valid, jnp.exp(sc - _head_rows(ll, masks2[h])), 0.0)
                dp = lax.dot_general(doh, vp, NT, preferred_element_type=F32)
                dbar = jnp.sum(jnp.where(masks2[h], prod, 0.0), axis=-1, keepdims=True)
                ds = pe * (dp - dbar) * (HEAD_DIM ** -0.5)
                dvs.append(lax.dot_general(pe.astype(BF16), doo, TN, preferred_element_type=F32))
                dks.append(lax.dot_general(ds.astype(BF16), qq, TN, preferred_element_type=F32))
            dk = jnp.where(masks[0], dks[0], dks[1])
            o_ref[:, sl] = _rope_transpose(dk, c_ref[...], su_ref[...], sd_ref[...]).astype(BF16)
            o_ref[:, GROUP_WIDTH + p * 128:GROUP_WIDTH + (p + 1) * 128] = jnp.where(masks[0], dvs[0], dvs[1]).astype(BF16)

    blk = (SPAN, GROUP_WIDTH)
    nxt_b = lambda b: jnp.minimum(b + 1, nb - 1)
    col_of = lambda t: pl.BlockSpec(blk, lambda r, b: (b, r * 3 + t))
    q_next = pl.BlockSpec(blk, lambda r, b: (nxt_b(b), r * 3))
    own = pl.BlockSpec(blk, lambda r, b: (b, r))
    own_next = pl.BlockSpec(blk, lambda r, b: (nxt_b(b), r))
    tab = pl.BlockSpec((SPAN, 128), lambda r, b: (b, r))
    dv_, mv, lv = view(dmix), view(mixed), view(lse)
    out = pl.pallas_call(
        body, name=f"attn_bwd_dkv_g{grp}", out_shape=jax.ShapeDtypeStruct((l, dil * 2 * GROUP_WIDTH), BF16), grid=(dil, nb),
        in_specs=[col_of(1), col_of(2), col_of(0), q_next, own, own_next, own, own_next, own, own_next, tab, tab, tab],
        out_specs=pl.BlockSpec((SPAN, 2 * GROUP_WIDTH), lambda r, b: (b, r)),
        compiler_params=_params("parallel", "arbitrary"),
    )(qv, qv, qv, qv, dv_, dv_, mv, mv, lv, lv, *[view(t) for t in rope])
    return out.reshape(s, 2 * GROUP_WIDTH)


def _rope_freq_row():
    half = ROT_DIM // 2
    inv = (ROPE_THETA ** (-np.arange(half, dtype=np.float32) / half)).astype(np.float32)
    row = np.zeros((1, 128), np.float32)
    for head in range(128 // HEAD_DIM):
        row[0, head * HEAD_DIM:head * HEAD_DIM + half] = inv
        row[0, head * HEAD_DIM + half:head * HEAD_DIM + ROT_DIM] = inv
    return jnp.asarray(row)


def _ffn_fwd(x, g_pre, g_post, w_up_t, w_dw, b_dw, w_down):
    h, z = _norm_matmul(x, g_pre, w_up_t, tn=_tile(w_up_t.shape[0]), name="ffn_up", tm=1024)
    act = _ffn_act(z, w_dw, b_dw)
    y, xo = _matmul_resnorm(act, w_down, x, g_post, name="ffn_down")
    return xo, (x, h, z, act, y)


def _ffn_bwd(saved, dxo, g_pre, g_post, w_up_t, w_dw, b_dw, w_down):
    x, h, z, act, y = saved
    f = act.shape[1]
    d = x.shape[1]
    dy, dg_post = _postnorm_bwd(y, g_post, dxo, name="ffn_post_bwd")
    dact = _matmul(dy, w_down, name="ffn_dact", out_dtype=BF16, transposed_w=True)
    d_down = _weight_grad(act, dy, name="ffn_dw_down", out_shape=(1, f, d))
    dug_u, dug_g, db_u, db_g, dwd_u, dwd_g = _ffn_act_bwd(z, dact, w_dw, b_dw)
    dz_u = _conv3_transpose(dug_u, w_dw, 0)
    dz_g = _conv3_transpose(dug_g, w_dw, f)
    dx, dg_pre = _matmul_prenorm_bwd([(dz_u, 0, f, 0), (dz_g, 0, f, f)], w_up_t, x, g_pre, dxo, name="ffn_dx")
    d_up_t = _weight_grad(dz_u, h, name="ffn_dw_up", out_shape=(1, 2 * f, d))
    d_up_t = _weight_grad(dz_g, h, name="ffn_dw_up", out=d_up_t, row0=f)
    grads = dict(w_dw=jnp.concatenate([dwd_u, dwd_g], axis=1), b_dw=jnp.concatenate([db_u, db_g], axis=1),
                 g_pre=dg_pre, g_post=dg_post)
    return dx, grads, d_up_t, d_down


def _local_step(x, pos_col, target, p, tie=None, late_weights=None, exchange=None):
    ng = p["norm_g"]
    row = lambda r: ng[r:r + 1]
    freq = _rope_freq_row()
    rope = _rope_tables(pos_col, freq if tie is None else freq + tie[0:1])
    d = x.shape[1]

    h0, *qkv = _qkv_proj(x, row(0), p["w_qkv_t"], rope)
    os_, ls_ = zip(*[_attn_fwd(qkv[g_], g_, d_) for g_, d_ in enumerate(DILATIONS)])
    y_a, x1, mixed, lse = _mix_wo(os_, ls_, p["w_o_t"], x, row(1))
    if late_weights is not None:
        p = {**p, **late_weights(x1)}
    x2, ffn0 = _ffn_fwd(x1, row(2), row(3), p["w_up_t"][0], p["ffn_w_dw"][0], p["ffn_b_dw"][0], p["w_down"][0])
    h1, ag = _norm_matmul(x2, row(4), p["w_pw1_t"], tn=_tile(p["w_pw1_t"].shape[0]), name="conv_pw1", bias=p["b_pw1"], tm=1024)
    u3, u1 = _conv_mid(ag, p["conv_w_dw"], p["conv_b_dw"], p["ln_g"], p["ln_b"])
    y_c, x3 = _matmul_resnorm(u3, p["w_pw2"], x2, row(5), name="conv_pw2", bias=p["b_pw2"])
    x4, ffn1 = _ffn_fwd(x3, row(6), row(7), p["w_up_t"][1], p["ffn_w_dw"][1], p["ffn_b_dw"][1], p["w_down"][1])
    dx4, loss = _loss_grad(x4, target)

    big = [BF16, BF16]

    def tied(r, *tokens):
        tokens = [t for t in tokens if t is not None]
        return row(r) if not tokens else row(r) + jnp.tile(sum(tokens)[0:1], (1, d // 128))

    dx3, gf1, d_up1, d_down1 = _ffn_bwd(ffn1, dx4, row(6), row(7), p["w_up_t"][1], p["ffn_w_dw"][1], p["ffn_b_dw"][1],
                                        p["w_down"][1])
    t0 = exchange.submit("ffn1", [d_up1, d_down1], big) if exchange else None
    dy_c, dg5, db_pw2 = _postnorm_bwd(y_c, tied(5, t0), dx3, name="conv_post_bwd", with_bias_grad=True)
    du3 = _matmul(dy_c, p["w_pw2"], name="conv_du3", out_dtype=F32, transposed_w=True)
    d_wpw2 = _weight_grad(u3, dy_c, name="conv_dw_pw2", out_shape=(1, u3.shape[1], d))
    du1, d_lng, d_lnb, d_cbdw, d_cwdw = _conv_mid_bwd(ag, u1, du3, p["conv_b_dw"], p["ln_g"], p["ln_b"])
    dag, db_pw1 = _glu_conv_bwd(du1, ag, p["conv_w_dw"])
    dx2, dg4 = _matmul_prenorm_bwd([(dag, 0, dag.shape[1], 0)], p["w_pw1_t"], x2, row(4), dx3, name="conv_dx")
    d_wpw1_t = _weight_grad(dag, h1, name="conv_dw_pw1", out_shape=(1, dag.shape[1], d))
    t0 = exchange.advance(dx2) if exchange else None
    t1 = exchange.submit("conv", [d_wpw1_t, d_wpw2], big) if exchange else None
    dx1, gf0, d_up0, d_down0 = _ffn_bwd(ffn0, dx2, row(2), tied(3, t0, t1), p["w_up_t"][0], p["ffn_w_dw"][0], p["ffn_b_dw"][0],
                                        p["w_down"][0])
    t0 = exchange.advance(dx1) if exchange else None
    t1 = exchange.submit("ffn0", [d_up0, d_down0], big) if exchange else None
    dy_a, dg1 = _postnorm_bwd(y_a, tied(1, t0, t1), dx1, name="attn_post_bwd")
    dmix = _matmul(dy_a, p["w_o_t"], name="attn_dmix", out_dtype=BF16, transposed_w=False)
    d_wo_t = _weight_grad(dy_a, mixed, name="attn_dw_o", out_shape=(1, d, GROUP_WIDTH))
    pieces, d_wqkv_t = [], None
    for g_, d_ in enumerate(DILATIONS):
        if exchange and g_ > 0:
            tok = exchange.advance(dkv)
            if tok is not None:
                rope = (rope[0] + tok[0:1], rope[1], rope[2])
        dq, dkv = _attn_bwd(qkv[g_], dmix, mixed, lse, rope, g_, d_)
        for t, (arr, c0) in enumerate(((dq, 0), (dkv, 0), (dkv, GROUP_WIDTH))):
            r0 = (3 * t + g_) * GROUP_WIDTH
            pieces.append((arr, c0, GROUP_WIDTH, r0))
            d_wqkv_t = _weight_grad(arr, h0, name="attn_dw_qkv", a_col0=c0, ka=GROUP_WIDTH, out=d_wqkv_t,
                                    out_shape=(1, p["w_qkv_t"].shape[0], d), row0=r0)
    t0 = exchange.advance(dkv) if exchange else None
    t1 = exchange.submit("attn", [d_wqkv_t, d_wo_t], big) if exchange else None
    grad_x, dg0 = _matmul_prenorm_bwd(pieces, p["w_qkv_t"], x, tied(0, t0, t1), dx1, name="attn_dx")

    grads = dict(
        norm_g=jnp.concatenate([dg0, dg1, gf0["g_pre"], gf0["g_post"], dg4, dg5, gf1["g_pre"], gf1["g_post"]], axis=0),
        w_qkv_t=d_wqkv_t, w_o_t=d_wo_t, w_pw1_t=d_wpw1_t, b_pw1=db_pw1,
        conv_w_dw=d_cwdw[:CONV_KERNEL], conv_b_dw=d_cbdw, ln_g=d_lng, ln_b=d_lnb, w_pw2=d_wpw2, b_pw2=db_pw2,
        w_up_t=[d_up0, d_up1], ffn_w_dw=jnp.stack([gf0["w_dw"], gf1["w_dw"]]),
        ffn_b_dw=jnp.concatenate([gf0["b_dw"], gf1["b_dw"]], axis=0), w_down=[d_down0, d_down1])
    return loss, grad_x, grads


SMALL_AXIS = dict(norm_g=2, conv_b_pw1=1, conv_w_dw=2, conv_b_dw=1, conv_ln_g=1, conv_ln_b=1, conv_b_pw2=1, ffn_w_dw=2)
SMALL = tuple(SMALL_AXIS)
MATMUL_WEIGHTS = dict(attn_w_qkv=True, conv_w_pw1=True, ffn_w_up=True, conv_w_pw2=False, ffn_w_down=False)


def _pack(arrays, cols, row_multiple):
    flat = jnp.concatenate([a.reshape(-1) for a in arrays])
    rows = -(-flat.shape[0] // cols)
    rows = -(-rows // row_multiple) * row_multiple
    return jnp.pad(flat, (0, rows * cols - flat.shape[0])).reshape(rows, cols)


def _unpack(packed, shapes):
    flat = packed.reshape(packed.shape[:-2] + (-1,))
    out, off = [], 0
    for shp in shapes:
        n = math.prod(shp)
        out.append(flat[..., off:off + n].reshape(packed.shape[:-2] + tuple(shp)))
        off += n
    return out


def _join_shards(stacked, axis):
    moved = jnp.moveaxis(stacked, 0, axis)
    shp = moved.shape
    return moved.reshape(shp[:axis] + (shp[axis] * shp[axis + 1],) + shp[axis + 2:])


def _split_shards(whole, axis):
    shp = whole.shape
    cut = whole.reshape(shp[:axis] + (N_DEV, shp[axis] // N_DEV) + shp[axis + 1:])
    return jnp.moveaxis(cut, axis, 0)


def _row_shard(w, transposed):
    t = jnp.swapaxes(w, 1, 2) if transposed else w
    return t.astype(BF16).reshape(-1, t.shape[-1])


def kernel(x, positions, norm_g, attn_w_qkv, attn_w_o, conv_w_pw1, conv_b_pw1, conv_w_dw, conv_b_dw, conv_ln_g, conv_ln_b, conv_w_pw2, conv_b_pw2, ffn_w_up, ffn_w_dw, ffn_b_dw, ffn_w_down, loss_target, m_norm_g, m_attn_w_qkv, m_attn_w_o, m_conv_w_pw1, m_conv_b_pw1, m_conv_w_dw, m_conv_b_dw, m_conv_ln_g, m_conv_ln_b, m_conv_w_pw2, m_conv_b_pw2, m_ffn_w_up, m_ffn_w_dw, m_ffn_b_dw, m_ffn_w_down, v_norm_g, v_attn_w_qkv, v_attn_w_o, v_conv_w_pw1, v_conv_b_pw1, v_conv_w_dw, v_conv_b_dw, v_conv_ln_g, v_conv_ln_b, v_conv_w_pw2, v_conv_b_pw2, v_ffn_w_up, v_ffn_w_dw, v_ffn_b_dw, v_ffn_w_down):
    w = dict(norm_g=norm_g, attn_w_qkv=attn_w_qkv, attn_w_o=attn_w_o, conv_w_pw1=conv_w_pw1, conv_b_pw1=conv_b_pw1,
             conv_w_dw=conv_w_dw, conv_b_dw=conv_b_dw, conv_ln_g=conv_ln_g, conv_ln_b=conv_ln_b, conv_w_pw2=conv_w_pw2,
             conv_b_pw2=conv_b_pw2, ffn_w_up=ffn_w_up, ffn_w_dw=ffn_w_dw, ffn_w_down=ffn_w_down)
    m = dict(norm_g=m_norm_g, attn_w_qkv=m_attn_w_qkv, attn_w_o=m_attn_w_o, conv_w_pw1=m_conv_w_pw1, conv_b_pw1=m_conv_b_pw1,
             conv_w_dw=m_conv_w_dw, conv_b_dw=m_conv_b_dw, conv_ln_g=m_conv_ln_g, conv_ln_b=m_conv_ln_b, conv_w_pw2=m_conv_w_pw2,
             conv_b_pw2=m_conv_b_pw2, ffn_w_up=m_ffn_w_up, ffn_w_dw=m_ffn_w_dw, ffn_w_down=m_ffn_w_down)
    v = dict(norm_g=v_norm_g, attn_w_qkv=v_attn_w_qkv, attn_w_o=v_attn_w_o, conv_w_pw1=v_conv_w_pw1, conv_b_pw1=v_conv_b_pw1,
             conv_w_dw=v_conv_w_dw, conv_b_dw=v_conv_b_dw, conv_ln_g=v_conv_ln_g, conv_ln_b=v_conv_ln_b, conv_w_pw2=v_conv_w_pw2,
             conv_b_pw2=v_conv_b_pw2, ffn_w_up=v_ffn_w_up, ffn_w_dw=v_ffn_w_dw, ffn_w_down=v_ffn_w_down)
    d = x.shape[-1]

    w_qkv_t = _all_gather(_row_shard(attn_w_qkv, True), "gather_w_qkv").reshape(-1, d)
    w_o_t = _all_gather(_row_shard(attn_w_o, True), "gather_w_o").reshape(d, -1)
    small = _all_gather(_pack([w[n] for n in SMALL], 128, 8), "gather_small_weights")
    sm = {n: _join_shards(stacked, SMALL_AXIS[n])
          for n, stacked in zip(SMALL, _unpack(small, [w[n].shape for n in SMALL]))}
    late = {n: t for n, t in MATMUL_WEIGHTS.items() if n != "attn_w_qkv"}
    shares = [_row_shard(w[n], t) for n, t in late.items()]
    rows = [s_.shape[0] for s_ in shares]
    late_share = jnp.concatenate(shares, axis=0)
    send_sems, recv_sems, share_thru, land_thru, tie = _gather_start(late_share)
    me = 4 * lax.axis_index("x") + 2 * lax.axis_index("y") + lax.axis_index("c")

    def late_weights(after):
        big = _gather_wait(send_sems, recv_sems, share_thru, land_thru, after)
        big = lax.dynamic_update_slice(big, late_share[None], (me, 0, 0))
        whole, r0 = {}, 0
        for n, nr in zip(late, rows):
            layers = w[n].shape[0]
            seg = big[:, r0:r0 + nr].reshape(N_DEV, layers, nr // layers, d)
            whole[n] = [seg[:, l_].reshape(-1, d) for l_ in range(layers)]
            r0 += nr
        return dict(w_pw1_t=whole["conv_w_pw1"][0], w_pw2=whole["conv_w_pw2"][0], w_up_t=whole["ffn_w_up"],
                    w_down=whole["ffn_w_down"])

    p = dict(norm_g=sm["norm_g"].reshape(-1, d), w_qkv_t=w_qkv_t, w_o_t=w_o_t, b_pw1=sm["conv_b_pw1"],
             conv_w_dw=sm["conv_w_dw"][0], conv_b_dw=sm["conv_b_dw"], ln_g=sm["conv_ln_g"], ln_b=sm["conv_ln_b"],
             b_pw2=sm["conv_b_pw2"], ffn_w_dw=sm["ffn_w_dw"], ffn_b_dw=[ffn_b_dw[0:1], ffn_b_dw[1:2]])

    exchange = _GradExchange()
    loss, grad_x, g = _local_step(x[0], positions.reshape(-1, 1), loss_target[0], p, tie, late_weights, exchange)
    loss = lax.psum(loss[0, 0], ("x", "y", "c"))
    gsmall = dict(norm_g=g["norm_g"].reshape(norm_g.shape[0], 4, -1), conv_b_pw1=g["b_pw1"], conv_w_dw=g["conv_w_dw"][None],
                  conv_b_dw=g["conv_b_dw"], conv_ln_g=g["ln_g"], conv_ln_b=g["ln_b"], conv_b_pw2=g["b_pw2"], ffn_w_dw=g["ffn_w_dw"])
    small_contrib = jnp.concatenate([_split_shards(gsmall[n], SMALL_AXIS[n]).reshape(N_DEV, -1) for n in SMALL], axis=1)
    srows = small.shape[1]
    small_contrib = jnp.pad(small_contrib, ((0, 0), (0, srows * 128 - small_contrib.shape[1]))).reshape(1, N_DEV, srows, 128)
    exchange.advance(grad_x)
    small_sums = _rs_chips([_rs_pair_add(small_contrib, _rs_sibling([small_contrib])[0], exchange.core, F32)])[0]

    outs = {}

    def update(n, reduced):
        gsum = jnp.swapaxes(reduced, 1, 2) if n == "attn_w_o" or MATMUL_WEIGHTS.get(n) else reduced
        outs[n] = (gsum, *_adamw(gsum, w[n], m[n], v[n], "adamw"))

    (s_up1, s_down1), (s_pw1, s_pw2), (s_up0, s_down0) = exchange.results()[:3]
    update("conv_w_pw1", s_pw1)
    update("conv_w_pw2", s_pw2)
    update("ffn_w_up", jnp.concatenate([s_up0, s_up1], axis=0))
    update("ffn_w_down", jnp.concatenate([s_down0, s_down1], axis=0))
    sshapes = [w[n].shape for n in SMALL]
    souts = _sum_adamw(small_sums[0], *[_pack([t[n] for n in SMALL], 128, 8) for t in (w, m, v)], name="sum_adamw_small")
    for n, vals in zip(SMALL, zip(*[_unpack(o, sshapes) for o in souts])):
        outs[n] = vals
    bparts = _all_gather(_pack([g["ffn_b_dw"]], 128, 8), "gather_bias_grads")
    bouts = _sum_adamw(bparts, *[_pack([t], 128, 8) for t in (ffn_b_dw, m_ffn_b_dw, v_ffn_b_dw)], name="sum_adamw_bias")
    outs["ffn_b_dw"] = tuple(_unpack(o, [ffn_b_dw.shape])[0] for o in bouts)
    done = [outs[n][1][0, :8, :128] for n in ("conv_w_pw1", "conv_w_pw2", "ffn_w_up", "ffn_w_down")]
    exchange.advance(sum(done) + bouts[1][:8] + souts[1][:8])
    s_qkv, s_wo = exchange.results()[3]
    update("attn_w_qkv", s_qkv)
    update("attn_w_o", s_wo)

    order = ("norm_g", "attn_w_qkv", "attn_w_o", "conv_w_pw1", "conv_b_pw1", "conv_w_dw", "conv_b_dw", "conv_ln_g",
             "conv_ln_b", "conv_w_pw2", "conv_b_pw2", "ffn_w_up", "ffn_w_dw", "ffn_b_dw", "ffn_w_down")
    return (loss, grad_x[None], *[outs[n][0] for n in order], *[outs[n][1] for n in order],
            *[outs[n][2] for n in order], *[outs[n][3] for n in order])
```

```python
import functools
import math

import numpy as np
import jax
import jax.numpy as jnp
from jax import lax
from jax.experimental import pallas as pl
from jax.experimental.pallas import tpu as pltpu

F32 = jnp.float32
BF16 = jnp.bfloat16
EPS = 1e-6
N_DEV = 8
HEAD_DIM = 64
GROUP_WIDTH = 512
DILATIONS = (1, 4, 16)
SPAN = 128
ROT_DIM = 16
ROPE_THETA = 500000.0
CONV_KERNEL = 31
CONV_HALO = 32
FFN_CONV = 3
ADAM_LR, ADAM_B1, ADAM_B2, ADAM_EPS, ADAM_WD, ADAM_STEP = 0.001, 0.9, 0.999, 1e-08, 0.01, 10
VMEM_LIMIT_BYTES = 56 * 1024 * 1024
MESH = pl.DeviceIdType.MESH
ANY = pl.BlockSpec(memory_space=pl.ANY)
NT = (((1,), (1,)), ((), ()))
TN = (((0,), (0,)), ((), ()))


def _params(*sem):
    return pltpu.CompilerParams(dimension_semantics=sem, vmem_limit_bytes=VMEM_LIMIT_BYTES)


def _sigmoid(v):
    return 1.0 / (1.0 + jnp.exp(-v))


def _full(shape):
    return pl.BlockSpec(shape, lambda *_: (0,) * len(shape))


def _rows(tm, width):
    return pl.BlockSpec((tm, width), lambda i, *_: (i, 0))


def _tile(n, *multiples_of):
    for t in (1408, 1024, 512, 384, 256, 128):
        if n % t == 0 and all(o % t == 0 for o in multiples_of):
            return t
    raise ValueError((n, multiples_of))


def _all_gather(shard, name):
    r, c_ = shard.shape

    def body(x_ref, out_ref, send_sems, recv_sems, local_sem):
        x, y, c = lax.axis_index("x"), lax.axis_index("y"), lax.axis_index("c")
        me, sibling = (x, y, c), (x, y, 1 - c)
        chips = [(1 - x, y), (x, 1 - y), (1 - x, 1 - y)]

        def rows(px, py, pc):
            return out_ref.at[4 * px + 2 * py + pc]

        def copy(k, block, to, src=None):
            return pltpu.make_async_remote_copy(
                src_ref=rows(*block) if src is None else src, dst_ref=rows(*block),
                send_sem=send_sems.at[k], recv_sem=recv_sems.at[k], device_id=to, device_id_type=MESH)

        mine = pltpu.make_async_copy(x_ref, rows(*me), local_sem)
        mine.start()
        first = [copy(0, me, sibling, src=x_ref)]
        first += [copy(1 + j, me, (*chip, c), src=x_ref) for j, chip in enumerate(chips)]
        for cp in first:
            cp.start()
        passed = [copy(4 + j, (*chip, c), sibling) for j, chip in enumerate(chips)]
        for j, chip in enumerate(chips):
            copy(1 + j, (*chip, c), me).wait_recv()
            passed[j].start()
        copy(0, sibling, me).wait_recv()
        for j, chip in enumerate(chips):
            copy(4 + j, (*chip, 1 - c), me).wait_recv()
        for cp in first + passed:
            cp.wait_send()
        mine.wait()

    return pl.pallas_call(
        body, name=name, out_shape=jax.ShapeDtypeStruct((N_DEV, r, c_), shard.dtype),
        in_specs=[ANY], out_specs=ANY,
        scratch_shapes=[pltpu.SemaphoreType.DMA((7,)), pltpu.SemaphoreType.DMA((7,)), pltpu.SemaphoreType.DMA],
    )(shard)


HBM = pl.BlockSpec(memory_space=pltpu.HBM)
SEM = pl.BlockSpec(memory_space=pltpu.SEMAPHORE)
SIDE_EFFECT = pltpu.CompilerParams(has_side_effects=pltpu.SideEffectType.DATAFLOW_SIDE_EFFECTING)


def _gather_start(shard):
    r, c_ = shard.shape

    def body(x_ref, land_ref, send_sems, recv_sems, x_thru, land_thru, token):
        x, y, c = lax.axis_index("x"), lax.axis_index("y"), lax.axis_index("c")
        me = 4 * x + 2 * y + c
        for k in range(1, N_DEV):
            peer = (1 - x if k & 4 else x, 1 - y if k & 2 else y, 1 - c if k & 1 else c)
            pltpu.make_async_remote_copy(src_ref=x_ref, dst_ref=land_ref.at[me], send_sem=send_sems.at[k - 1],
                                         recv_sem=recv_sems.at[k - 1], device_id=peer, device_id_type=MESH).start()
        token[...] = jnp.zeros_like(token)

    land = pltpu.with_memory_space_constraint(lax.empty((N_DEV, r, c_), shard.dtype), pltpu.HBM)
    return pl.pallas_call(
        body, name="gather_late_weights_start",
        out_shape=(pltpu.SemaphoreType.DMA((N_DEV - 1,)), pltpu.SemaphoreType.DMA((N_DEV - 1,)),
                   pltpu.HBM(shard.shape, shard.dtype), pltpu.HBM((N_DEV, r, c_), shard.dtype),
                   jax.ShapeDtypeStruct((8, 128), F32)),
        in_specs=(HBM, HBM), out_specs=(SEM, SEM, HBM, HBM, pl.BlockSpec(memory_space=pltpu.VMEM)),
        input_output_aliases={0: 2, 1: 3}, compiler_params=SIDE_EFFECT,
    )(pltpu.with_memory_space_constraint(shard, pltpu.HBM), land)


def _gather_wait(send_sems, recv_sems, shard_thru, land_thru, after):
    def body(x_ref, land_ref, send_sems, recv_sems, after_ref, x_dead, got_ref):
        x, y, c = lax.axis_index("x"), lax.axis_index("y"), lax.axis_index("c")
        for k in range(N_DEV - 1):
            copy = pltpu.make_async_remote_copy(src_ref=x_ref, dst_ref=land_ref.at[0], send_sem=send_sems.at[k],
                                                recv_sem=recv_sems.at[k], device_id=(x, y, c), device_id_type=MESH)
            copy.wait_send()
            copy.wait_recv()

    return pl.pallas_call(
        body, name="gather_late_weights_wait",
        out_shape=(pltpu.HBM(shard_thru.shape, shard_thru.dtype), pltpu.HBM(land_thru.shape, land_thru.dtype)),
        in_specs=(HBM, HBM, SEM, SEM, ANY), out_specs=(HBM, HBM), input_output_aliases={0: 0, 1: 1},
        compiler_params=SIDE_EFFECT,
    )(shard_thru, land_thru, send_sems, recv_sems, after)[1]


def _hbm(a):
    return pltpu.with_memory_space_constraint(a, pltpu.HBM)


def _exchange_start(name, arrays, lands, plan, ncopies):
    n = len(arrays)

    def body(*refs):
        send_sems, recv_sems, token = refs[2 * n], refs[2 * n + 1], refs[-1]
        x, y, c = lax.axis_index("x"), lax.axis_index("y"), lax.axis_index("c")
        for k, (src, dst, peer) in enumerate(plan(x, y, c, refs[:n], refs[n:2 * n])):
            pltpu.make_async_remote_copy(src_ref=src, dst_ref=dst, send_sem=send_sems.at[k], recv_sem=recv_sems.at[k],
                                         device_id=peer, device_id_type=MESH).start()
        token[...] = jnp.zeros_like(token)

    both = list(arrays) + list(lands)
    outs = pl.pallas_call(
        body, name=name,
        out_shape=(pltpu.SemaphoreType.DMA((ncopies,)), pltpu.SemaphoreType.DMA((ncopies,)),
                   *[pltpu.HBM(a.shape, a.dtype) for a in both], jax.ShapeDtypeStruct((8, 128), F32)),
        in_specs=(HBM,) * (2 * n), out_specs=(SEM, SEM) + (HBM,) * (2 * n) + (pl.BlockSpec(memory_space=pltpu.VMEM),),
        input_output_aliases={i: 2 + i for i in range(2 * n)}, compiler_params=SIDE_EFFECT,
    )(*[_hbm(a) for a in both])
    return outs[0], outs[1], list(outs[2:2 + n]), list(outs[2 + n:2 + 2 * n]), outs[-1]


def _exchange_wait(name, send_sems, recv_sems, arrays, lands, plan, after):
    n = len(arrays)

    def body(*refs):
        send_sems, recv_sems = refs[2 * n], refs[2 * n + 1]
        x, y, c = lax.axis_index("x"), lax.axis_index("y"), lax.axis_index("c")
        for k, (src, dst, peer) in enumerate(plan(x, y, c, refs[:n], refs[n:2 * n])):
            copy = pltpu.make_async_remote_copy(src_ref=src, dst_ref=dst, send_sem=send_sems.at[k], recv_sem=recv_sems.at[k],
                                                device_id=peer, device_id_type=MESH)
            copy.wait_send()
            copy.wait_recv()

    both = list(arrays) + list(lands)
    outs = pl.pallas_call(
        body, name=name, out_shape=tuple(pltpu.HBM(a.shape, a.dtype) for a in both),
        in_specs=(HBM,) * (2 * n) + (SEM, SEM, ANY), out_specs=(HBM,) * (2 * n),
        input_output_aliases={i: i for i in range(2 * n)}, compiler_params=SIDE_EFFECT,
    )(*both, send_sems, recv_sems, after)
    return list(outs[:n]), list(outs[n:])


def _sibling_plan(x, y, c, g_refs, land_refs):
    return [(g.at[:, 2 * q + (1 - c)], o.at[:, q], (x, y, 1 - c)) for g, o in zip(g_refs, land_refs) for q in range(4)]


def _chips_plan(x, y, c, p_refs, land_refs):
    chips = [(1 - x, y), (x, 1 - y), (1 - x, 1 - y)]
    return [(p_.at[:, 2 * qx + qy], o.at[:, 2 * x + y], (qx, qy, c)) for p_, o in zip(p_refs, land_refs) for qx, qy in chips]


class _GradExchange:
    def __init__(self):
        self.core = lax.axis_index("c").astype(jnp.int32).reshape(1)
        self.chip = 2 * lax.axis_index("x") + lax.axis_index("y")
        self.groups = []

    def submit(self, tag, arrays, dtypes):
        arrays = [a.reshape(a.shape[0], N_DEV, a.shape[1] // N_DEV, a.shape[2]) for a in arrays]
        lands = [lax.empty((a.shape[0], 4) + a.shape[2:], a.dtype) for a in arrays]
        send, recv, arrays, lands, token = _exchange_start(f"rs_pair_start_{tag}", arrays, lands, _sibling_plan, 4 * len(arrays))
        self.groups.append(dict(tag=tag, stage=1, sems=(send, recv), arrays=arrays, lands=lands, dtypes=dtypes))
        return token

    def advance(self, after):
        token = None
        for g in self.groups:
            if g["stage"] == 1:
                arrays, got = _exchange_wait(f"rs_pair_wait_{g['tag']}", *g["sems"], g["arrays"], g["lands"], _sibling_plan, after)
                parts = [_rs_pair_add(a, b, self.core, dt) for a, b, dt in zip(arrays, got, g["dtypes"])]
                lands = [lax.empty(p_.shape, p_.dtype) for p_ in parts]
                send, recv, parts, lands, tok = _exchange_start(f"rs_chip_start_{g['tag']}", parts, lands, _chips_plan, 3 * len(parts))
                g.update(stage=2, sems=(send, recv), arrays=parts, lands=lands)
                token = tok if token is None else token + tok
            elif g["stage"] == 2:
                parts, lands = _exchange_wait(f"rs_chip_wait_{g['tag']}", *g["sems"], g["arrays"], g["lands"], _chips_plan, after)
                sums = []
                for p_, land in zip(parts, lands):
                    l, _, r, c_ = p_.shape
                    own = lax.dynamic_slice(p_, (0, self.chip, 0, 0), (l, 1, r, c_))
                    sums.append(_sum_parts(lax.dynamic_update_slice(land, own, (0, self.chip, 0, 0)), "sum_chips"))
                g.update(stage=3, sums=sums)
        return token

    def results(self):
        return [g.get("sums") for g in self.groups]


def _with_rows(g, n):
    return jax.ShapeDtypeStruct((g.shape[0], n) + tuple(g.shape[2:]), g.dtype)


def _rs_sibling(gs):
    n = len(gs)

    def body(*refs):
        g_refs, o_refs, (send_sems, recv_sems) = refs[:n], refs[n:2 * n], refs[2 * n:]
        x, y, c = lax.axis_index("x"), lax.axis_index("y"), lax.axis_index("c")
        copies = [pltpu.make_async_remote_copy(
            src_ref=g_refs[w].at[:, 2 * q + (1 - c)], dst_ref=o_refs[w].at[:, q], send_sem=send_sems.at[4 * w + q],
            recv_sem=recv_sems.at[4 * w + q], device_id=(x, y, 1 - c), device_id_type=MESH)
            for w in range(n) for q in range(4)]
        for cp in copies:
            cp.start()
        for cp in copies:
            cp.wait_recv()
        for cp in copies:
            cp.wait_send()

    return pl.pallas_call(
        body, name="rs_sibling", out_shape=[_with_rows(g, 4) for g in gs],
        in_specs=[ANY] * n, out_specs=[ANY] * n,
        scratch_shapes=[pltpu.SemaphoreType.DMA((4 * n,)), pltpu.SemaphoreType.DMA((4 * n,))],
    )(*gs)


def _rs_pair_add(g, got, core, out_dtype):
    l, _, r, c_ = g.shape

    def body(core_ref, g_ref, got_ref, o_ref):
        o_ref[...] = (g_ref[...].astype(F32) + got_ref[...].astype(F32)).astype(out_dtype)

    blk = (None, None, r, c_)
    return pl.pallas_call(
        body, name="rs_pair_add", out_shape=jax.ShapeDtypeStruct((l, 4, r, c_), out_dtype),
        grid_spec=pltpu.PrefetchScalarGridSpec(
            num_scalar_prefetch=1, grid=(l, 4),
            in_specs=[pl.BlockSpec(blk, lambda i, q, core_ref: (i, 2 * q + core_ref[0], 0, 0)),
                      pl.BlockSpec(blk, lambda i, q, core_ref: (i, q, 0, 0))],
            out_specs=pl.BlockSpec(blk, lambda i, q, core_ref: (i, q, 0, 0))),
        compiler_params=_params("parallel", "parallel"),
    )(core, g, got)


def _rs_chips(parts):
    n = len(parts)

    def body(*refs):
        p_refs, o_refs, (send_sems, recv_sems, local_sems) = refs[:n], refs[n:2 * n], refs[2 * n:]
        x, y, c = lax.axis_index("x"), lax.axis_index("y"), lax.axis_index("c")
        my_chip = 2 * x + y
        chips = [(1 - x, y), (x, 1 - y), (1 - x, 1 - y)]
        local = [pltpu.make_async_copy(p_refs[w].at[:, my_chip], o_refs[w].at[:, my_chip], local_sems.at[w]) for w in range(n)]
        for cp in local:
            cp.start()
        copies = [pltpu.make_async_remote_copy(
            src_ref=p_refs[w].at[:, 2 * qx + qy], dst_ref=o_refs[w].at[:, my_chip], send_sem=send_sems.at[3 * w + k],
            recv_sem=recv_sems.at[3 * w + k], device_id=(qx, qy, c), device_id_type=MESH)
            for w in range(n) for k, (qx, qy) in enumerate(chips)]
        for cp in copies:
            cp.start()
        for cp in copies:
            cp.wait_recv()
        for cp in copies:
            cp.wait_send()
        for cp in local:
            cp.wait()

    return pl.pallas_call(
        body, name="rs_chips", out_shape=[jax.ShapeDtypeStruct(p.shape, p.dtype) for p in parts],
        in_specs=[ANY] * n, out_specs=[ANY] * n,
        scratch_shapes=[pltpu.SemaphoreType.DMA((3 * n,)), pltpu.SemaphoreType.DMA((3 * n,)), pltpu.SemaphoreType.DMA((n,))],
    )(*parts)


def _sum_parts(parts, name):
    l, n, r, c_ = parts.shape

    def body(p_ref, o_ref):
        g = p_ref[0].astype(F32)
        for s in range(1, n):
            g = g + p_ref[s].astype(F32)
        o_ref[...] = g

    return pl.pallas_call(
        body, name=name, out_shape=jax.ShapeDtypeStruct((l, r, c_), F32), grid=(l,),
        in_specs=[pl.BlockSpec((None, n, r, c_), lambda i: (i, 0, 0, 0))],
        out_specs=pl.BlockSpec((None, r, c_), lambda i: (i, 0, 0)), compiler_params=_params("parallel"),
    )(parts)


def _adamw_math(w, g, m, v):
    m = ADAM_B1 * m + (1.0 - ADAM_B1) * g
    v = ADAM_B2 * v + (1.0 - ADAM_B2) * (g * g)
    m_hat = m / (1.0 - ADAM_B1 ** ADAM_STEP)
    v_hat = v / (1.0 - ADAM_B2 ** ADAM_STEP)
    delta = -ADAM_LR * (m_hat / (jnp.sqrt(v_hat) + ADAM_EPS) + ADAM_WD * w)
    return delta, m, v


def _adamw(g, w, m, v, name):
    l, k, n = w.shape
    tk = 256 if k % 256 == 0 else k

    def body(g_ref, w_ref, m_ref, v_ref, d_ref, nm_ref, nv_ref):
        d_ref[...], nm_ref[...], nv_ref[...] = _adamw_math(w_ref[...], g_ref[...], m_ref[...], v_ref[...])

    spec = pl.BlockSpec((None, tk, n), lambda i, j: (i, j, 0))
    return pl.pallas_call(
        body, name=name, out_shape=[jax.ShapeDtypeStruct((l, k, n), F32)] * 3, grid=(l, k // tk),
        in_specs=[spec] * 4, out_specs=[spec] * 3, compiler_params=_params("parallel", "parallel"),
    )(g, w, m, v)


def _sum_adamw(parts, w, m, v, name):
    n, r, c_ = parts.shape

    def body(p_ref, w_ref, m_ref, v_ref, g_ref, d_ref, nm_ref, nv_ref):
        g = p_ref[0]
        for s in range(1, n):
            g = g + p_ref[s]
        g_ref[...] = g
        d_ref[...], nm_ref[...], nv_ref[...] = _adamw_math(w_ref[...], g, m_ref[...], v_ref[...])

    return pl.pallas_call(
        body, name=name, out_shape=[jax.ShapeDtypeStruct((r, c_), F32)] * 4, grid=(1,),
        in_specs=[_full((n, r, c_))] + [_full((r, c_))] * 3, out_specs=[_full((r, c_))] * 4,
        compiler_params=_params("arbitrary"),
    )(parts, w, m, v)


def _rope_tables(pos_col, freq_row):
    s = pos_col.shape[0]
    tm = min(1024, s)

    def body(p_ref, f_ref, c_ref, su_ref, sd_ref):
        ang = p_ref[...].astype(F32) * f_ref[...]
        lane = lax.broadcasted_iota(jnp.int32, ang.shape, 1) & (HEAD_DIM - 1)
        cs, sn = jnp.cos(ang), jnp.sin(ang)
        c_ref[...] = jnp.where(lane < ROT_DIM, cs, 1.0)
        su_ref[...] = jnp.where((lane >= ROT_DIM // 2) & (lane < ROT_DIM), sn, 0.0)
        sd_ref[...] = jnp.where(lane < ROT_DIM // 2, -sn, 0.0)

    return pl.pallas_call(
        body, name="rope_tables", out_shape=[jax.ShapeDtypeStruct((s, 128), F32)] * 3, grid=(s // tm,),
        in_specs=[pl.BlockSpec((tm, 1), lambda i: (i, 0)), _full((1, 128))],
        out_specs=[_rows(tm, 128)] * 3, compiler_params=_params("parallel"),
    )(pos_col, freq_row)


def _rope_apply(t, cos, sin_up, sin_dn):
    w = t.shape[1]
    return t * cos + pltpu.roll(t, 8, 1) * sin_up + pltpu.roll(t, w - 8, 1) * sin_dn


def _rope_transpose(dr, cos, sin_up, sin_dn):
    w = dr.shape[1]
    return dr * cos + pltpu.roll(dr * sin_up, w - 8, 1) + pltpu.roll(dr * sin_dn, 8, 1)


def _norm_matmul(x, g, wt, *, tn, name, bias=None, rope=None, rope_blocks=0, tm=512):
    s, d = x.shape
    n = wt.shape[0]
    tm = min(tm, s)

    def body(*refs):
        x_ref, g_ref, w_ref = refs[:3]
        k = 3
        b_ref = None
        if bias is not None:
            b_ref = refs[k]
            k += 1
        if rope is not None:
            c_ref, su_ref, sd_ref = refs[k:k + 3]
            k += 3
        h_ref, o_ref = refs[k:k + 2]
        j = pl.program_id(1)

        @pl.when(j == 0)
        def _():
            xv = x_ref[...]
            r = lax.rsqrt(jnp.mean(xv * xv, axis=-1, keepdims=True) + EPS)
            h_ref[...] = (xv * r * g_ref[...]).astype(BF16)

        acc = lax.dot_general(h_ref[...], w_ref[...], NT, preferred_element_type=F32)
        if b_ref is not None:
            acc = acc + b_ref[...]
        if rope is None:
            o_ref[...] = acc.astype(BF16)
        else:
            @pl.when(j < rope_blocks)
            def _():
                reps = tn // 128
                o_ref[...] = _rope_apply(acc, jnp.tile(c_ref[...], (1, reps)), jnp.tile(su_ref[...], (1, reps)),
                                         jnp.tile(sd_ref[...], (1, reps))).astype(BF16)

            @pl.when(j >= rope_blocks)
            def _():
                o_ref[...] = acc.astype(BF16)

    in_specs = [_rows(tm, d), _full((1, d)), pl.BlockSpec((tn, d), lambda i, j: (j, 0))]
    args = [x, g, wt]
    if bias is not None:
        in_specs.append(pl.BlockSpec((1, tn), lambda i, j: (0, j)))
        args.append(bias)
    if rope is not None:
        in_specs += [_rows(tm, 128)] * 3
        args += list(rope)
    return pl.pallas_call(
        body, name=name,
        out_shape=[jax.ShapeDtypeStruct((s, d), BF16), jax.ShapeDtypeStruct((s, n), BF16)],
        grid=(s // tm, n // tn), in_specs=in_specs,
        out_specs=[_rows(tm, d), pl.BlockSpec((tm, tn), lambda i, j: (i, j))],
        compiler_params=_params("parallel", "arbitrary"),
    )(*args)


def _class_major(tm, dil):
    p = np.zeros((tm, tm), np.float32)
    per = tm // dil
    for r in range(dil):
        for j in range(per):
            p[r * per + j, j * dil + r] = 1.0
    return jnp.asarray(p, dtype=BF16)


def _qkv_proj(x, g, wt, rope, tm=512):
    s, d = x.shape
    n = wt.shape[0]
    gw3 = 3 * GROUP_WIDTH
    tm = min(tm, s)
    assert n == 3 * gw3

    def body(x_ref, g_ref, w_ref, c_ref, su_ref, sd_ref, p1_ref, p2_ref, h_ref, o0_ref, o1_ref, o2_ref):
        j = pl.program_id(1)

        @pl.when(j == 0)
        def _():
            xv = x_ref[...]
            r = lax.rsqrt(jnp.mean(xv * xv, axis=-1, keepdims=True) + EPS)
            h_ref[...] = (xv * r * g_ref[...]).astype(BF16)

        acc = lax.dot_general(h_ref[...], w_ref[...], NT, preferred_element_type=F32)

        def store(y):
            yb = y.astype(BF16)
            o0_ref[:, pl.ds(pl.multiple_of(j * GROUP_WIDTH, GROUP_WIDTH), GROUP_WIDTH)] = yb[:, :GROUP_WIDTH]
            for grp, o_ref, p_ref in ((1, o1_ref, p1_ref), (2, o2_ref, p2_ref)):
                dil = DILATIONS[grp]
                per = tm // dil
                yp = jnp.dot(p_ref[...], yb[:, grp * GROUP_WIDTH:(grp + 1) * GROUP_WIDTH],
                             preferred_element_type=F32).astype(BF16)
                for r in range(dil):
                    col = pl.multiple_of(r * gw3 + j * GROUP_WIDTH, GROUP_WIDTH)
                    o_ref[:, pl.ds(col, GROUP_WIDTH)] = yp[r * per:(r + 1) * per, :]

        @pl.when(j < 2)
        def _():
            reps = gw3 // 128
            store(_rope_apply(acc, jnp.tile(c_ref[...], (1, reps)), jnp.tile(su_ref[...], (1, reps)),
                              jnp.tile(sd_ref[...], (1, reps))))

        @pl.when(j == 2)
        def _():
            store(acc)

    outs = [jax.ShapeDtypeStruct((s, d), BF16)] + [jax.ShapeDtypeStruct((s // dl, dl * gw3), BF16) for dl in DILATIONS]
    out_specs = [_rows(tm, d)] + [_rows(tm // dl, dl * gw3) for dl in DILATIONS]
    return pl.pallas_call(
        body, name="attn_qkv", out_shape=outs, grid=(s // tm, 3),
        in_specs=[_rows(tm, d), _full((1, d)), pl.BlockSpec((gw3, d), lambda i, j: (j, 0))] + [_rows(tm, 128)] * 3
        + [_full((tm, tm))] * 2,
        out_specs=out_specs, compiler_params=_params("parallel", "arbitrary"),
    )(x, g, wt, *rope, _class_major(tm, DILATIONS[1]), _class_major(tm, DILATIONS[2]))


def _head_masks(rows=SPAN):
    lane = lax.broadcasted_iota(jnp.int32, (rows, 128), 1)
    masks = [lane < HEAD_DIM, lane >= HEAD_DIM]
    lane1 = lax.broadcasted_iota(jnp.int32, (1, 128), 1)
    keep = [jnp.where(lane1 < HEAD_DIM, 1.0, 0.0).astype(BF16), jnp.where(lane1 >= HEAD_DIM, 1.0, 0.0).astype(BF16)]
    return masks, keep


def _attn_fwd(qv, grp, dil):
    l = qv.shape[0]
    s = l * dil
    nb = l // SPAN

    def body(q_ref, kp_ref, kc_ref, vp_ref, vc_ref, o_ref, l_ref):
        b = pl.program_id(1)
        row = lax.broadcasted_iota(jnp.int32, (SPAN, 2 * SPAN), 0)
        col = lax.broadcasted_iota(jnp.int32, (SPAN, 2 * SPAN), 1)
        no_prev = jnp.where(b > 0, 0, 4 * SPAN)
        valid = ((col < SPAN) & (col >= row + no_prev)) | ((col >= SPAN) & (col - SPAN <= row))
        masks, keep = _head_masks()
        for p in range(GROUP_WIDTH // 128):
            sl = slice(p * 128, (p + 1) * 128)
            qp = q_ref[:, sl]
            kk = jnp.concatenate([kp_ref[:, sl], kc_ref[:, sl]], axis=0)
            vv = jnp.concatenate([vp_ref[:, sl], vc_ref[:, sl]], axis=0)
            outs, lses = [], []
            for h in range(2):
                sc = lax.dot_general(qp * keep[h], kk, NT, preferred_element_type=F32) * (HEAD_DIM ** -0.5)
                sc = jnp.where(valid, sc, -1e30)
                mx = jnp.max(sc, axis=-1, keepdims=True)
                pe = jnp.exp(sc - mx)
                den = jnp.sum(pe, axis=-1, keepdims=True)
                pv = jnp.dot(pe.astype(BF16), vv, preferred_element_type=F32)
                outs.append(pv / den)
                lses.append(jnp.broadcast_to(mx + jnp.log(den), (SPAN, 128)))
            o_ref[:, sl] = jnp.where(masks[0], outs[0], outs[1])
            l_ref[:, sl] = jnp.where(masks[0], lses[0], lses[1])

    blk = (SPAN, GROUP_WIDTH)
    cur = lambda t: pl.BlockSpec(blk, lambda r, b: (b, r * 3 + t))
    prev = lambda t: pl.BlockSpec(blk, lambda r, b: (jnp.maximum(b - 1, 0), r * 3 + t))
    out = pl.BlockSpec(blk, lambda r, b: (b, r))
    o, lse = pl.pallas_call(
        body, name=f"attn_fwd_g{grp}", out_shape=[jax.ShapeDtypeStruct((l, dil * GROUP_WIDTH), F32)] * 2,
        grid=(dil, nb), in_specs=[cur(0), prev(1), cur(1), prev(2), cur(2)], out_specs=[out, out],
        compiler_params=_params("parallel", "arbitrary"),
    )(qv, qv, qv, qv, qv)
    return o.reshape(s, GROUP_WIDTH), lse.reshape(s, GROUP_WIDTH)


def _resnorm_store(y, x_ref, g_ref, y_ref, xo_ref):
    r = lax.rsqrt(jnp.mean(y * y, axis=-1, keepdims=True) + EPS)
    y_ref[...] = y
    xo_ref[...] = x_ref[...] + y * r * g_ref[...]


def _mix_wo(os_, ls_, wot, x, g, tm=256):
    s, d = x.shape
    gw = wot.shape[1]
    tm = min(tm, s)

    def body(o0, o1, o2, l0, l1, l2, w_ref, x_ref, g_ref, y_ref, xo_ref, mixed_ref, lse_ref):
        a0, a1, a2 = l0[...], l1[...], l2[...]
        mx = jnp.maximum(jnp.maximum(a0, a1), a2)
        e0, e1, e2 = jnp.exp(a0 - mx), jnp.exp(a1 - mx), jnp.exp(a2 - mx)
        den = e0 + e1 + e2
        mixed = (e0 / den) * o0[...] + (e1 / den) * o1[...] + (e2 / den) * o2[...]
        mixed_ref[...] = mixed.astype(BF16)
        lse_ref[...] = mx + jnp.log(den)
        y = lax.dot_general(mixed.astype(BF16), w_ref[...], NT, preferred_element_type=F32)
        _resnorm_store(y, x_ref, g_ref, y_ref, xo_ref)

    return pl.pallas_call(
        body, name="mix_wo",
        out_shape=[jax.ShapeDtypeStruct((s, d), F32), jax.ShapeDtypeStruct((s, d), F32),
                   jax.ShapeDtypeStruct((s, gw), BF16), jax.ShapeDtypeStruct((s, gw), F32)],
        grid=(s // tm,), in_specs=[_rows(tm, gw)] * 6 + [_full((d, gw)), _rows(tm, d), _full((1, d))],
        out_specs=[_rows(tm, d), _rows(tm, d), _rows(tm, gw), _rows(tm, gw)],
        compiler_params=_params("parallel"),
    )(*os_, *ls_, wot, x, g)


def _matmul_resnorm(a, w, x, g, *, name, bias=None, tm=512):
    s, k = a.shape
    d = w.shape[1]
    tm = min(tm, s)

    def body(*refs):
        a_ref, w_ref = refs[:2]
        b_ref = refs[2] if bias is not None else None
        x_ref, g_ref, y_ref, xo_ref = refs[-4:]
        y = jnp.dot(a_ref[...], w_ref[...], preferred_element_type=F32)
        if b_ref is not None:
            y = y + b_ref[...]
        _resnorm_store(y, x_ref, g_ref, y_ref, xo_ref)

    in_specs = [_rows(tm, k), _full((k, d))] + ([_full((1, d))] if bias is not None else []) + [_rows(tm, d), _full((1, d))]
    args = [a, w] + ([bias] if bias is not None else []) + [x, g]
    return pl.pallas_call(
        body, name=name, out_shape=[jax.ShapeDtypeStruct((s, d), F32)] * 2, grid=(s // tm,),
        in_specs=in_specs, out_specs=[_rows(tm, d)] * 2, compiler_params=_params("parallel"),
    )(*args)


FFN_SUB = 256


def _conv3_rows(z_ref, halo_ref, rb, sub, cs, first):
    zc = z_ref[rb * sub:(rb + 1) * sub, cs].astype(F32)
    if rb == 0:
        halo = halo_ref[:, cs].astype(F32) * jnp.where(first, 0.0, 1.0)
    else:
        halo = z_ref[rb * sub - 16:rb * sub, cs].astype(F32)[8:]
    z2, z1 = _conv3_taps(zc, halo)
    return z2, z1, zc


def _conv3_taps(z, halo):
    row = lax.broadcasted_iota(jnp.int32, (8, z.shape[1]), 0)
    h6, h7 = halo[6:7, :], halo[7:8, :]
    r1, r2 = pltpu.roll(z, 1, 0), pltpu.roll(z, 2, 0)
    z1 = jnp.concatenate([jnp.where(row == 0, h7, r1[0:8]), r1[8:]], axis=0)
    z2 = jnp.concatenate([jnp.where(row == 0, h6, jnp.where(row == 1, h7, r2[0:8])), r2[8:]], axis=0)
    return z2, z1


def _ffn_cols(f):
    return _tile(f)


def _lane_chunks(width, fn):
    def step(k, carry):
        fn(pl.ds(pl.multiple_of(k * 128, 128), 128))
        return carry

    lax.fori_loop(0, width // 128, step, 0)


def _ffn_act(z, w_dw, b_dw, tm=1024):
    s, f2 = z.shape
    f = f2 // 2
    tm = min(tm, s)
    sub = min(FFN_SUB, tm)
    tc = _ffn_cols(f)
    nfc = f // tc

    def body(zu, zg, hu, hg, wu, wg, bu, bg, o_ref):
        first = pl.program_id(0) == 0

        def chunk(cs):
            for rb in range(tm // sub):
                def conv(z_ref, h_ref, w_ref, b_ref):
                    z2, z1, zc = _conv3_rows(z_ref, h_ref, rb, sub, cs, first)
                    return w_ref[0:1, cs] * z2 + w_ref[1:2, cs] * z1 + w_ref[2:3, cs] * zc + b_ref[:, cs]

                up, gate = conv(zu, hu, wu, bu), conv(zg, hg, wg, bg)
                o_ref[rb * sub:(rb + 1) * sub, cs] = (gate * _sigmoid(gate) * up).astype(BF16)

        _lane_chunks(tc, chunk)

    hb = tm // 8
    tile = lambda off: pl.BlockSpec((tm, tc), lambda i, j: (i, off + j))
    halo = lambda off: pl.BlockSpec((8, tc), lambda i, j: (jnp.maximum(i * hb - 1, 0), off + j))
    prm = lambda rows, off: pl.BlockSpec((rows, tc), lambda i, j: (0, off + j))
    return pl.pallas_call(
        body, name="ffn_act", out_shape=jax.ShapeDtypeStruct((s, f), BF16), grid=(s // tm, nfc),
        in_specs=[tile(0), tile(nfc), halo(0), halo(nfc), prm(FFN_CONV, 0), prm(FFN_CONV, nfc), prm(1, 0), prm(1, nfc)],
        out_specs=pl.BlockSpec((tm, tc), lambda i, j: (i, j)), compiler_params=_params("parallel", "parallel"),
    )(z, z, z, z, w_dw, w_dw, b_dw, b_dw)


def _shifted_planes(ext_ref):
    rows = ext_ref.shape[1]
    for s in range(1, 8):
        ext_ref[s, 0:rows - 8, :] = ext_ref[0, s:s + rows - 8, :]


def _window(ext_ref, off, tm, cs):
    s = off % 8
    return ext_ref[s, off - s:off - s + tm, cs]


def _conv_taps(ext_ref, w_ref, offs, tm, out_ref):
    def chunk(cs):
        acc = w_ref[0:1, cs] * _window(ext_ref, offs[0], tm, cs)
        for j in range(1, len(offs)):
            acc = acc + w_ref[j:j + 1, cs] * _window(ext_ref, offs[j], tm, cs)
        out_ref[:, cs] = acc

    _lane_chunks(out_ref.shape[1], chunk)


def _glu_planes(ag_ref, halo_ref, ext_ref, first, c):
    hal = halo_ref[...].astype(F32)
    ext_ref[0, 0:CONV_HALO, :] = hal[:, :c] * _sigmoid(hal[:, c:]) * jnp.where(first, 0.0, 1.0)
    ag = ag_ref[...].astype(F32)
    ext_ref[0, CONV_HALO:, :] = ag[:, :c] * _sigmoid(ag[:, c:])
    _shifted_planes(ext_ref)


def _layernorm_stats(u1):
    mu = jnp.mean(u1, axis=-1, keepdims=True)
    cen = u1 - mu
    rstd = lax.rsqrt(jnp.mean(cen * cen, axis=-1, keepdims=True) + EPS)
    return cen * rstd, rstd


def _conv_mid(ag, w_dw, b_dw, ln_g, ln_b, tm=256):
    s, c2 = ag.shape
    c = c2 // 2
    tm = min(tm, s)

    def body(ag_ref, halo_ref, w_ref, b_ref, g_ref, bb_ref, o_ref, u1_ref, ext_ref):
        _glu_planes(ag_ref, halo_ref, ext_ref, pl.program_id(0) == 0, c)
        base = CONV_HALO - (CONV_KERNEL - 1)
        _conv_taps(ext_ref, w_ref, [base + j for j in range(CONV_KERNEL)], tm, u1_ref)
        xh, _ = _layernorm_stats(u1_ref[...] + b_ref[...])
        u2 = xh * g_ref[...] + bb_ref[...]
        o_ref[...] = (u2 * _sigmoid(u2)).astype(BF16)

    hb = tm // CONV_HALO
    return pl.pallas_call(
        body, name="conv_mid", out_shape=[jax.ShapeDtypeStruct((s, c), BF16), jax.ShapeDtypeStruct((s, c), F32)], grid=(s // tm,),
        in_specs=[_rows(tm, c2), pl.BlockSpec((CONV_HALO, c2), lambda i: (jnp.maximum(i * hb - 1, 0), 0)),
                  _full((CONV_KERNEL, c)), _full((1, c)), _full((1, c)), _full((1, c))],
        out_specs=[_rows(tm, c), _rows(tm, c)], scratch_shapes=[pltpu.VMEM((8, CONV_HALO + tm, c), F32)],
        compiler_params=_params("arbitrary"),
    )(ag, ag, w_dw, b_dw, ln_g, ln_b)


def _loss_grad(xo, target, tm=512):
    s, d = xo.shape
    tm = min(tm, s)

    def body(x_ref, t_ref, dx_ref, loss_ref):
        @pl.when(pl.program_id(0) == 0)
        def _():
            loss_ref[...] = jnp.zeros_like(loss_ref)

        err = x_ref[...] - t_ref[...]
        dx_ref[...] = err * (1.0 / d)
        loss_ref[...] += 0.5 * jnp.sum(jnp.mean(err * err, axis=-1, keepdims=True))

    return pl.pallas_call(
        body, name="loss_grad", out_shape=[jax.ShapeDtypeStruct((s, d), F32), jax.ShapeDtypeStruct((1, 128), F32)],
        grid=(s // tm,), in_specs=[_rows(tm, d)] * 2, out_specs=[_rows(tm, d), _full((1, 128))],
        compiler_params=_params("arbitrary"),
    )(xo, target)


def _postnorm_bwd(y, g, dxo, *, name, with_bias_grad=False, tm=512):
    s, d = y.shape
    tm = min(tm, s)

    def body(y_ref, g_ref, dx_ref, dy_ref, dg_ref, *rest):
        @pl.when(pl.program_id(0) == 0)
        def _():
            dg_ref[...] = jnp.zeros_like(dg_ref)
            for r_ in rest:
                r_[...] = jnp.zeros_like(r_)

        yv, dxo_v = y_ref[...], dx_ref[...]
        r = lax.rsqrt(jnp.mean(yv * yv, axis=-1, keepdims=True) + EPS)
        yh = yv * r
        dyh = dxo_v * g_ref[...]
        dy = r * (dyh - yh * jnp.mean(dyh * yh, axis=-1, keepdims=True))
        dy_ref[...] = dy.astype(BF16)
        dg_ref[...] += jnp.sum(dxo_v * yh, axis=0, keepdims=True)
        for r_ in rest:
            r_[...] += jnp.sum(dy, axis=0, keepdims=True)

    nacc = 2 if with_bias_grad else 1
    return pl.pallas_call(
        body, name=name, out_shape=[jax.ShapeDtypeStruct((s, d), BF16)] + [jax.ShapeDtypeStruct((1, d), F32)] * nacc,
        grid=(s // tm,), in_specs=[_rows(tm, d), _full((1, d)), _rows(tm, d)],
        out_specs=[_rows(tm, d)] + [_full((1, d))] * nacc, compiler_params=_params("arbitrary"),
    )(y, g, dxo)


def _matmul(gmat, w, *, name, out_dtype, transposed_w, tm=512):
    s, k = gmat.shape
    n = w.shape[0] if transposed_w else w.shape[1]
    tm = min(tm, s)

    def body(g_ref, w_ref, o_ref):
        if transposed_w:
            acc = lax.dot_general(g_ref[...], w_ref[...], NT, preferred_element_type=F32)
        else:
            acc = jnp.dot(g_ref[...], w_ref[...], preferred_element_type=F32)
        o_ref[...] = acc.astype(out_dtype)

    return pl.pallas_call(
        body, name=name, out_shape=jax.ShapeDtypeStruct((s, n), out_dtype), grid=(s // tm,),
        in_specs=[_rows(tm, k), _full(w.shape)], out_specs=_rows(tm, n), compiler_params=_params("parallel"),
    )(gmat, w)


def _matmul_prenorm_bwd(pieces, wt, x, g, dres, *, name, tm=256):
    s, d = x.shape
    tm = min(tm, s)
    np_ = len(pieces)

    def body(*refs):
        p_refs, w_refs = refs[:np_], refs[np_:2 * np_]
        x_ref, g_ref, r_ref, dx_ref, dg_ref = refs[2 * np_:]

        @pl.when(pl.program_id(0) == 0)
        def _():
            dg_ref[...] = jnp.zeros_like(dg_ref)

        dh = None
        for p_ref, w_ref in zip(p_refs, w_refs):
            t = jnp.dot(p_ref[...], w_ref[...], preferred_element_type=F32)
            dh = t if dh is None else dh + t
        xv = x_ref[...]
        r = lax.rsqrt(jnp.mean(xv * xv, axis=-1, keepdims=True) + EPS)
        xh = xv * r
        dyh = dh * g_ref[...]
        dx_ref[...] = r_ref[...] + r * (dyh - xh * jnp.mean(dyh * xh, axis=-1, keepdims=True))
        dg_ref[...] += jnp.sum(dh * xh, axis=0, keepdims=True)

    in_specs = []
    for _, c0, kc, _ in pieces:
        assert c0 % kc == 0
        in_specs.append(pl.BlockSpec((tm, kc), lambda i, _b=c0 // kc: (i, _b)))
    for _, _, kc, r0 in pieces:
        assert r0 % kc == 0
        in_specs.append(pl.BlockSpec((kc, d), lambda i, _b=r0 // kc: (_b, 0)))
    in_specs += [_rows(tm, d), _full((1, d)), _rows(tm, d)]
    return pl.pallas_call(
        body, name=name, out_shape=[jax.ShapeDtypeStruct((s, d), F32), jax.ShapeDtypeStruct((1, d), F32)],
        grid=(s // tm,), in_specs=in_specs, out_specs=[_rows(tm, d), _full((1, d))],
        compiler_params=_params("arbitrary"),
    )(*[p[0] for p in pieces], *[wt] * np_, x, g, dres)


def _weight_grad(a, gmat, *, name, a_col0=0, ka=None, out=None, out_shape=None, layer=0, row0=0, ts=1024):
    s = a.shape[0]
    ka = a.shape[1] if ka is None else ka
    n = gmat.shape[1]
    ts = min(ts, s)
    tka = _tile(ka, a_col0, row0)
    shape = out.shape if out is not None else out_shape
    nsteps = s // ts

    def body(a_ref, g_ref, *rest):
        o_ref, acc_ref = rest[-2:]
        i = pl.program_id(1)

        @pl.when(i == 0)
        def _():
            acc_ref[...] = jnp.zeros_like(acc_ref)

        acc_ref[...] += lax.dot_general(a_ref[...], g_ref[...], TN, preferred_element_type=F32)

        @pl.when(i == nsteps - 1)
        def _():
            o_ref[...] = acc_ref[...].astype(BF16)

    in_specs = [pl.BlockSpec((ts, tka), lambda k, i: (i, a_col0 // tka + k)), pl.BlockSpec((ts, n), lambda k, i: (i, 0))]
    args = [a, gmat]
    aliases = {}
    if out is not None:
        in_specs.append(ANY)
        args.append(out)
        aliases = {2: 0}
    return pl.pallas_call(
        body, name=name, out_shape=jax.ShapeDtypeStruct(shape, BF16), grid=(ka // tka, nsteps), in_specs=in_specs,
        out_specs=pl.BlockSpec((None, tka, n), lambda k, i: (layer, row0 // tka + k, 0)),
        scratch_shapes=[pltpu.VMEM((tka, n), F32)],
        input_output_aliases=aliases, compiler_params=_params("parallel", "arbitrary"),
    )(*args)


def _ffn_act_bwd(z, dact, w_dw, b_dw, tm=1024):
    s, f2 = z.shape
    f = f2 // 2
    tm = min(tm, s)
    sub = min(FFN_SUB, tm)
    tc = _ffn_cols(f)
    nfc = f // tc

    def body(zu, zg, hu, hg, wu, wg, bu, bg, da_ref, du_ref, dgt_ref, dbu_ref, dbg_ref, dwu_ref, dwg_ref):
        i = pl.program_id(1)

        @pl.when(i == 0)
        def _():
            for r_ in (dbu_ref, dbg_ref, dwu_ref, dwg_ref):
                r_[...] = jnp.zeros_like(r_)

        def chunk(cs):
            for rb in range(tm // sub):
                rows = slice(rb * sub, (rb + 1) * sub)

                def conv(z_ref, h_ref, w_ref, b_ref):
                    taps = _conv3_rows(z_ref, h_ref, rb, sub, cs, i == 0)
                    return taps, w_ref[0:1, cs] * taps[0] + w_ref[1:2, cs] * taps[1] + w_ref[2:3, cs] * taps[2] + b_ref[:, cs]

                taps_u, up = conv(zu, hu, wu, bu)
                taps_g, gate = conv(zg, hg, wg, bg)
                da = da_ref[rows, cs].astype(F32)
                sg = _sigmoid(gate)
                d_up = da * (gate * sg)
                d_gate = da * up * (sg * (1.0 + gate * (1.0 - sg)))
                du_ref[rows, cs] = d_up.astype(BF16)
                dgt_ref[rows, cs] = d_gate.astype(BF16)
                for dv, taps, db_ref, dw_ref in ((d_up, taps_u, dbu_ref, dwu_ref), (d_gate, taps_g, dbg_ref, dwg_ref)):
                    db_ref[:, cs] += jnp.sum(dv, axis=0, keepdims=True)
                    for k_, tap in enumerate(taps):
                        dw_ref[k_:k_ + 1, cs] += jnp.sum(dv * tap, axis=0, keepdims=True)

        _lane_chunks(tc, chunk)

    hb = tm // 8
    tile = lambda off: pl.BlockSpec((tm, tc), lambda j, i: (i, off + j))
    halo = lambda off: pl.BlockSpec((8, tc), lambda j, i: (jnp.maximum(i * hb - 1, 0), off + j))
    prm = lambda rows, off: pl.BlockSpec((rows, tc), lambda j, i: (0, off + j))
    acc = lambda rows: pl.BlockSpec((rows, tc), lambda j, i: (0, j))
    return pl.pallas_call(
        body, name="ffn_act_bwd",
        out_shape=[jax.ShapeDtypeStruct((s, f), BF16)] * 2 + [jax.ShapeDtypeStruct((1, f), F32)] * 2
        + [jax.ShapeDtypeStruct((FFN_CONV, f), F32)] * 2,
        grid=(nfc, s // tm),
        in_specs=[tile(0), tile(nfc), halo(0), halo(nfc), prm(FFN_CONV, 0), prm(FFN_CONV, nfc), prm(1, 0), prm(1, nfc), tile(0)],
        out_specs=[tile(0), tile(0), acc(1), acc(1), acc(FFN_CONV), acc(FFN_CONV)],
        compiler_params=_params("parallel", "arbitrary"),
    )(z, z, z, z, w_dw, w_dw, b_dw, b_dw, dact)


def _conv3_transpose(dug, w_dw, col0, tm=1024):
    s, f = dug.shape
    tm = min(tm, s)
    sub = min(FFN_SUB, tm)
    nsub = tm // sub
    tc = _ffn_cols(f)
    nfc = f // tc
    nrow = s // tm
    off = col0 // tc

    def body(d_ref, n_ref, w_ref, o_ref):
        keep_next = jnp.where(pl.program_id(0) == nrow - 1, 0.0, 1.0)

        def chunk(cs):
            for rb in range(nsub):
                rows = slice(rb * sub, (rb + 1) * sub)
                dv = d_ref[rows, cs].astype(F32)
                if rb == nsub - 1:
                    nxt = n_ref[:, cs].astype(F32) * keep_next
                else:
                    nxt = d_ref[(rb + 1) * sub:(rb + 1) * sub + 16, cs].astype(F32)[:8]
                n0, n1 = nxt[0:1, :], nxt[1:2, :]
                row = lax.broadcasted_iota(jnp.int32, (8, dv.shape[1]), 0)
                r1, r2 = pltpu.roll(dv, sub - 1, 0), pltpu.roll(dv, sub - 2, 0)
                d1 = jnp.concatenate([r1[:sub - 8], jnp.where(row == 7, n0, r1[sub - 8:])], axis=0)
                d2 = jnp.concatenate([r2[:sub - 8], jnp.where(row == 7, n1, jnp.where(row == 6, n0, r2[sub - 8:]))], axis=0)
                o_ref[rows, cs] = (w_ref[2:3, cs] * dv + w_ref[1:2, cs] * d1 + w_ref[0:1, cs] * d2).astype(BF16)

        _lane_chunks(tc, chunk)

    hb = tm // 8
    return pl.pallas_call(
        body, name="conv3_transpose", out_shape=jax.ShapeDtypeStruct((s, f), BF16), grid=(nrow, nfc),
        in_specs=[pl.BlockSpec((tm, tc), lambda i, j: (i, j)),
                  pl.BlockSpec((8, tc), lambda i, j: (jnp.minimum((i + 1) * hb, s // 8 - 1), j)),
                  pl.BlockSpec((FFN_CONV, tc), lambda i, j: (0, off + j))],
        out_specs=pl.BlockSpec((tm, tc), lambda i, j: (i, j)), compiler_params=_params("parallel", "parallel"),
    )(dug, dug, w_dw)


def _conv_mid_bwd(ag, u1, du3, b_dw, ln_g, ln_b, tm=256):
    s, c2 = ag.shape
    c = c2 // 2
    tm = min(tm, s)

    def body(ag_ref, halo_ref, u1in_ref, du_ref, b_ref, g_ref, bb_ref, o_ref, dlg_ref, dlb_ref, db_ref, dw_ref, ext_ref, u1_ref):
        @pl.when(pl.program_id(0) == 0)
        def _():
            for r_ in (dlg_ref, dlb_ref, db_ref, dw_ref):
                r_[...] = jnp.zeros_like(r_)

        _glu_planes(ag_ref, halo_ref, ext_ref, pl.program_id(0) == 0, c)
        xh, rstd = _layernorm_stats(u1in_ref[...] + b_ref[...])
        u2 = xh * g_ref[...] + bb_ref[...]
        sg = _sigmoid(u2)
        du2 = du_ref[...] * (sg * (1.0 + u2 * (1.0 - sg)))
        dlg_ref[...] += jnp.sum(du2 * xh, axis=0, keepdims=True)
        dlb_ref[...] += jnp.sum(du2, axis=0, keepdims=True)
        dxh = du2 * g_ref[...]
        du1 = rstd * (dxh - jnp.mean(dxh, axis=-1, keepdims=True) - xh * jnp.mean(dxh * xh, axis=-1, keepdims=True))
        o_ref[...] = du1.astype(BF16)
        db_ref[...] += jnp.sum(du1, axis=0, keepdims=True)
        u1_ref[...] = du1
        base = CONV_HALO - (CONV_KERNEL - 1)

        def chunk(cs):
            dc = u1_ref[:, cs]
            for j in range(CONV_KERNEL):
                dw_ref[j:j + 1, cs] += jnp.sum(dc * _window(ext_ref, base + j, tm, cs), axis=0, keepdims=True)

        _lane_chunks(c, chunk)

    hb = tm // CONV_HALO
    vec = _full((1, c))
    return pl.pallas_call(
        body, name="conv_mid_bwd",
        out_shape=[jax.ShapeDtypeStruct((s, c), BF16)] + [jax.ShapeDtypeStruct((1, c), F32)] * 3
        + [jax.ShapeDtypeStruct((CONV_HALO, c), F32)],
        grid=(s // tm,),
        in_specs=[_rows(tm, c2), pl.BlockSpec((CONV_HALO, c2), lambda i: (jnp.maximum(i * hb - 1, 0), 0)), _rows(tm, c),
                  _rows(tm, c), vec, vec, vec],
        out_specs=[_rows(tm, c), vec, vec, vec, _full((CONV_HALO, c))],
        scratch_shapes=[pltpu.VMEM((8, CONV_HALO + tm, c), F32), pltpu.VMEM((tm, c), F32)],
        compiler_params=_params("arbitrary"),
    )(ag, ag, u1, du3, b_dw, ln_g, ln_b)


def _glu_conv_bwd(du1, ag, w_dw, tm=256):
    s, c = du1.shape
    tm = min(tm, s)
    nrow = s // tm

    def body(d_ref, n_ref, ag_ref, w_ref, o_ref, db_ref, ext_ref, du0_ref):
        @pl.when(pl.program_id(0) == 0)
        def _():
            db_ref[...] = jnp.zeros_like(db_ref)

        ext_ref[0, 0:tm, :] = d_ref[...].astype(F32)
        ext_ref[0, tm:, :] = n_ref[...].astype(F32) * jnp.where(pl.program_id(0) == nrow - 1, 0.0, 1.0)
        _shifted_planes(ext_ref)
        top = CONV_KERNEL - 1
        _conv_taps(ext_ref, w_ref, [top - j for j in range(CONV_KERNEL)], tm, du0_ref)
        du0 = du0_ref[...]
        ag = ag_ref[...].astype(F32)
        a, gt = ag[:, :c], ag[:, c:]
        sg = _sigmoid(gt)
        da = du0 * sg
        dgt = du0 * a * (sg * (1.0 - sg))
        o_ref[:, :c] = da.astype(BF16)
        o_ref[:, c:] = dgt.astype(BF16)
        db_ref[:, :c] += jnp.sum(da, axis=0, keepdims=True)
        db_ref[:, c:] += jnp.sum(dgt, axis=0, keepdims=True)

    hb = tm // CONV_HALO
    return pl.pallas_call(
        body, name="glu_conv_bwd",
        out_shape=[jax.ShapeDtypeStruct((s, 2 * c), BF16), jax.ShapeDtypeStruct((1, 2 * c), F32)], grid=(nrow,),
        in_specs=[_rows(tm, c), pl.BlockSpec((CONV_HALO, c), lambda i: (jnp.minimum((i + 1) * hb, s // CONV_HALO - 1), 0)),
                  _rows(tm, 2 * c), _full((CONV_KERNEL, c))],
        out_specs=[_rows(tm, 2 * c), _full((1, 2 * c))],
        scratch_shapes=[pltpu.VMEM((8, tm + CONV_HALO, c), F32), pltpu.VMEM((tm, c), F32)],
        compiler_params=_params("arbitrary"),
    )(du1, du1, ag, w_dw)


def _head_rows(v, mask):
    return jnp.max(jnp.where(mask, v, -jnp.inf), axis=-1, keepdims=True)


def _attn_bwd(qv, dmix, mixed, lse, rope, grp, dil):
    l = qv.shape[0]
    s = l * dil
    nb = l // SPAN
    view = lambda t: t.reshape(l, dil * t.shape[1])
    scale = HEAD_DIM ** -0.5
    gw = GROUP_WIDTH

    def body(q_ref, kp_ref, kc_ref, vp_ref, vc_ref, do_ref, mx_ref, l_ref, c_ref, su_ref, sd_ref, cp_ref, sup_ref, sdp_ref,
             dq_ref, dkv_ref, carry_ref):
        b = pl.program_id(1)
        prev_tabs = (cp_ref, sup_ref, sdp_ref)

        @pl.when(b < nb)
        def _():
            row = lax.broadcasted_iota(jnp.int32, (SPAN, 2 * SPAN), 0)
            col = lax.broadcasted_iota(jnp.int32, (SPAN, 2 * SPAN), 1)
            no_prev = jnp.where(b > 0, 0, 4 * SPAN)
            valid = ((col < SPAN) & (col >= row + no_prev)) | ((col >= SPAN) & (col - SPAN <= row))
            masks, keep = _head_masks()
            masks2, _ = _head_masks(2 * SPAN)
            for p in range(gw // 128):
                sl = slice(p * 128, (p + 1) * 128)
                sl_v = slice(gw + p * 128, gw + (p + 1) * 128)
                qp, dop = q_ref[:, sl], do_ref[:, sl]
                kk = jnp.concatenate([kp_ref[:, sl], kc_ref[:, sl]], axis=0)
                vv = jnp.concatenate([vp_ref[:, sl], vc_ref[:, sl]], axis=0)
                prod = dop.astype(F32) * mx_ref[:, sl].astype(F32)
                lsep = l_ref[:, sl]
                dqs, dks, dvs = [], [], []
                for h in range(2):
                    qh, doh = qp * keep[h], dop * keep[h]
                    sc = lax.dot_general(qh, kk, NT, preferred_element_type=F32) * scale
                    pe = jnp.where(valid, jnp.exp(sc - _head_rows(lsep, masks[h])), 0.0)
                    dp = lax.dot_general(doh, vv, NT, preferred_element_type=F32)
                    dbar = jnp.sum(jnp.where(masks[h], prod, 0.0), axis=-1, keepdims=True)
                    ds = (pe * (dp - dbar) * scale).astype(BF16)
                    dqs.append(jnp.dot(ds, kk, preferred_element_type=F32))
                    dks.append(lax.dot_general(ds, qp, TN, preferred_element_type=F32))
                    dvs.append(lax.dot_general(pe.astype(BF16), dop, TN, preferred_element_type=F32))
                dq = jnp.where(masks[0], dqs[0], dqs[1])
                dq_ref[:, sl] = _rope_transpose(dq, c_ref[...], su_ref[...], sd_ref[...]).astype(BF16)
                dk = jnp.where(masks2[0], dks[0], dks[1])
                dv = jnp.where(masks2[0], dvs[0], dvs[1])

                @pl.when(b > 0)
                def _():
                    dk_prev = carry_ref[:, sl] + dk[:SPAN]
                    dkv_ref[:, sl] = _rope_transpose(dk_prev, *[t[...] for t in prev_tabs]).astype(BF16)
                    dkv_ref[:, sl_v] = (carry_ref[:, sl_v] + dv[:SPAN]).astype(BF16)

                carry_ref[:, sl] = dk[SPAN:]
                carry_ref[:, sl_v] = dv[SPAN:]

        @pl.when(b == nb)
        def _():
            for p in range(gw // 128):
                sl = slice(p * 128, (p + 1) * 128)
                sl_v = slice(gw + p * 128, gw + (p + 1) * 128)
                dkv_ref[:, sl] = _rope_transpose(carry_ref[:, sl], *[t[...] for t in prev_tabs]).astype(BF16)
                dkv_ref[:, sl_v] = carry_ref[:, sl_v].astype(BF16)

    blk = (SPAN, gw)
    cb = lambda b: jnp.minimum(b, nb - 1)
    cur = lambda t: pl.BlockSpec(blk, lambda r, b: (cb(b), r * 3 + t))
    prev = lambda t: pl.BlockSpec(blk, lambda r, b: (jnp.maximum(cb(b) - 1, 0), r * 3 + t))
    own = pl.BlockSpec(blk, lambda r, b: (cb(b), r))
    tab = pl.BlockSpec((SPAN, 128), lambda r, b: (cb(b), r))
    tab_prev = pl.BlockSpec((SPAN, 128), lambda r, b: (jnp.maximum(b - 1, 0), r))
    tabs = [view(t) for t in rope]
    dq, dkv = pl.pallas_call(
        body, name=f"attn_bwd_g{grp}",
        out_shape=[jax.ShapeDtypeStruct((l, dil * gw), BF16), jax.ShapeDtypeStruct((l, dil * 2 * gw), BF16)],
        grid=(dil, nb + 1),
        in_specs=[cur(0), prev(1), cur(1), prev(2), cur(2), own, own, own, tab, tab, tab, tab_prev, tab_prev, tab_prev],
        out_specs=[own, pl.BlockSpec((SPAN, 2 * gw), lambda r, b: (jnp.maximum(b - 1, 0), r))],
        scratch_shapes=[pltpu.VMEM((SPAN, 2 * gw), F32)], compiler_params=_params("parallel", "arbitrary"),
    )(qv, qv, qv, qv, qv, view(dmix), view(mixed), view(lse), *tabs, *tabs)
    return dq.reshape(s, gw), dkv.reshape(s, 2 * gw)


def _attn_bwd_dq(qv, dmix, mixed, lse, rope, grp, dil):
    l = qv.shape[0]
    s = l * dil
    nb = l // SPAN
    view = lambda t: t.reshape(l, dil * t.shape[1])

    def body(q_ref, kp_ref, kc_ref, vp_ref, vc_ref, do_ref, mx_ref, l_ref, c_ref, su_ref, sd_ref, o_ref):
        b = pl.program_id(1)
        row = lax.broadcasted_iota(jnp.int32, (SPAN, 2 * SPAN), 0)
        col = lax.broadcasted_iota(jnp.int32, (SPAN, 2 * SPAN), 1)
        no_prev = jnp.where(b > 0, 0, 4 * SPAN)
        valid = ((col < SPAN) & (col >= row + no_prev)) | ((col >= SPAN) & (col - SPAN <= row))
        masks, keep = _head_masks()
        for p in range(GROUP_WIDTH // 128):
            sl = slice(p * 128, (p + 1) * 128)
            qp, dop = q_ref[:, sl], do_ref[:, sl]
            kk = jnp.concatenate([kp_ref[:, sl], kc_ref[:, sl]], axis=0)
            vv = jnp.concatenate([vp_ref[:, sl], vc_ref[:, sl]], axis=0)
            prod = dop.astype(F32) * mx_ref[:, sl].astype(F32)
            lsep = l_ref[:, sl]
            dqs = []
            for h in range(2):
                qh, doh = qp * keep[h], dop * keep[h]
                sc = lax.dot_general(qh, kk, NT, preferred_element_type=F32) * (HEAD_DIM ** -0.5)
                pe = jnp.where(valid, jnp.exp(sc - _head_rows(lsep, masks[h])), 0.0)
                dp = lax.dot_general(doh, vv, NT, preferred_element_type=F32)
                dbar = jnp.sum(jnp.where(masks[h], prod, 0.0), axis=-1, keepdims=True)
                ds = pe * (dp - dbar) * (HEAD_DIM ** -0.5)
                dqs.append(jnp.dot(ds.astype(BF16), kk, preferred_element_type=F32))
            dq = jnp.where(masks[0], dqs[0], dqs[1])
            o_ref[:, sl] = _rope_transpose(dq, c_ref[...], su_ref[...], sd_ref[...]).astype(BF16)

    blk = (SPAN, GROUP_WIDTH)
    cur = lambda t: pl.BlockSpec(blk, lambda r, b: (b, r * 3 + t))
    prev = lambda t: pl.BlockSpec(blk, lambda r, b: (jnp.maximum(b - 1, 0), r * 3 + t))
    own = pl.BlockSpec(blk, lambda r, b: (b, r))
    tab = pl.BlockSpec((SPAN, 128), lambda r, b: (b, r))
    out = pl.pallas_call(
        body, name=f"attn_bwd_dq_g{grp}", out_shape=jax.ShapeDtypeStruct((l, dil * GROUP_WIDTH), BF16), grid=(dil, nb),
        in_specs=[cur(0), prev(1), cur(1), prev(2), cur(2), own, own, own, tab, tab, tab], out_specs=own,
        compiler_params=_params("parallel", "arbitrary"),
    )(qv, qv, qv, qv, qv, view(dmix), view(mixed), view(lse), *[view(t) for t in rope])
    return out.reshape(s, GROUP_WIDTH)


def _attn_bwd_dkv(qv, dmix, mixed, lse, rope, grp, dil):
    l = qv.shape[0]
    s = l * dil
    nb = l // SPAN
    view = lambda t: t.reshape(l, dil * t.shape[1])

    def body(k_ref, v_ref, qc_ref, qn_ref, doc_ref, don_ref, mc_ref, mn_ref, lc_ref, ln_ref,
             c_ref, su_ref, sd_ref, o_ref):
        b = pl.program_id(1)
        row = lax.broadcasted_iota(jnp.int32, (2 * SPAN, SPAN), 0)
        col = lax.broadcasted_iota(jnp.int32, (2 * SPAN, SPAN), 1)
        no_next = jnp.where(b < nb - 1, 0, 4 * SPAN)
        valid = ((row < SPAN) & (col <= row)) | ((row >= SPAN) & (col >= row - SPAN + no_next))
        masks, keep = _head_masks()
        masks2, _ = _head_masks(2 * SPAN)
        for p in range(GROUP_WIDTH // 128):
            sl = slice(p * 128, (p + 1) * 128)
            kp, vp = k_ref[:, sl], v_ref[:, sl]
            qq = jnp.concatenate([qc_ref[:, sl], qn_ref[:, sl]], axis=0)
            doo = jnp.concatenate([doc_ref[:, sl], don_ref[:, sl]], axis=0)
            mm = jnp.concatenate([mc_ref[:, sl], mn_ref[:, sl]], axis=0)
            ll = jnp.concatenate([lc_ref[:, sl], ln_ref[:, sl]], axis=0)
            prod = doo.astype(F32) * mm.astype(F32)
            dks, dvs = [], []
            for h in range(2):
                qh, doh = qq * keep[h], doo * keep[h]
                sc = lax.dot_general(qh, kp, NT, preferred_element_type=F32) * (HEAD_DIM ** -0.5)
                pe = jnp.where(valid, jnp.exp(sc - _head_rows(ll, masks2[h])), 0.0)
                dp = lax.dot_general(doh, vp, NT, preferred_element_type=F32)
                dbar = jnp.sum(jnp.where(masks2[h], prod, 0.0), axis=-1, keepdims=True)
                ds = pe * (dp - dbar) * (HEAD_DIM ** -0.5)
                dvs.append(lax.dot_general(pe.astype(BF16), doo, TN, preferred_element_type=F32))
                dks.append(lax.dot_general(ds.astype(BF16), qq, TN, preferred_element_type=F32))
            dk = jnp.where(masks[0], dks[0], dks[1])
            o_ref[:, sl] = _rope_transpose(dk, c_ref[...], su_ref[...], sd_ref[...]).astype(BF16)
            o_ref[:, GROUP_WIDTH + p * 128:GROUP_WIDTH + (p + 1) * 128] = jnp.where(masks[0], dvs[0], dvs[1]).astype(BF16)

    blk = (SPAN, GROUP_WIDTH)
    nxt_b = lambda b: jnp.minimum(b + 1, nb - 1)
    col_of = lambda t: pl.BlockSpec(blk, lambda r, b: (b, r * 3 + t))
    q_next = pl.BlockSpec(blk, lambda r, b: (nxt_b(b), r * 3))
    own = pl.BlockSpec(blk, lambda r, b: (b, r))
    own_next = pl.BlockSpec(blk, lambda r, b: (nxt_b(b), r))
    tab = pl.BlockSpec((SPAN, 128), lambda r, b: (b, r))
    dv_, mv, lv = view(dmix), view(mixed), view(lse)
    out = pl.pallas_call(
        body, name=f"attn_bwd_dkv_g{grp}", out_shape=jax.ShapeDtypeStruct((l, dil * 2 * GROUP_WIDTH), BF16), grid=(dil, nb),
        in_specs=[col_of(1), col_of(2), col_of(0), q_next, own, own_next, own, own_next, own, own_next, tab, tab, tab],
        out_specs=pl.BlockSpec((SPAN, 2 * GROUP_WIDTH), lambda r, b: (b, r)),
        compiler_params=_params("parallel", "arbitrary"),
    )(qv, qv, qv, qv, dv_, dv_, mv, mv, lv, lv, *[view(t) for t in rope])
    return out.reshape(s, 2 * GROUP_WIDTH)


def _rope_freq_row():
    half = ROT_DIM // 2
    inv = (ROPE_THETA ** (-np.arange(half, dtype=np.float32) / half)).astype(np.float32)
    row = np.zeros((1, 128), np.float32)
    for head in range(128 // HEAD_DIM):
        row[0, head * HEAD_DIM:head * HEAD_DIM + half] = inv
        row[0, head * HEAD_DIM + half:head * HEAD_DIM + ROT_DIM] = inv
    return jnp.asarray(row)


def _ffn_fwd(x, g_pre, g_post, w_up_t, w_dw, b_dw, w_down):
    h, z = _norm_matmul(x, g_pre, w_up_t, tn=_tile(w_up_t.shape[0]), name="ffn_up", tm=1024)
    act = _ffn_act(z, w_dw, b_dw)
    y, xo = _matmul_resnorm(act, w_down, x, g_post, name="ffn_down")
    return xo, (x, h, z, act, y)


def _ffn_bwd(saved, dxo, g_pre, g_post, w_up_t, w_dw, b_dw, w_down):
    x, h, z, act, y = saved
    f = act.shape[1]
    d = x.shape[1]
    dy, dg_post = _postnorm_bwd(y, g_post, dxo, name="ffn_post_bwd")
    dact = _matmul(dy, w_down, name="ffn_dact", out_dtype=BF16, transposed_w=True)
    d_down = _weight_grad(act, dy, name="ffn_dw_down", out_shape=(1, f, d))
    dug_u, dug_g, db_u, db_g, dwd_u, dwd_g = _ffn_act_bwd(z, dact, w_dw, b_dw)
    dz_u = _conv3_transpose(dug_u, w_dw, 0)
    dz_g = _conv3_transpose(dug_g, w_dw, f)
    dx, dg_pre = _matmul_prenorm_bwd([(dz_u, 0, f, 0), (dz_g, 0, f, f)], w_up_t, x, g_pre, dxo, name="ffn_dx")
    d_up_t = _weight_grad(dz_u, h, name="ffn_dw_up", out_shape=(1, 2 * f, d))
    d_up_t = _weight_grad(dz_g, h, name="ffn_dw_up", out=d_up_t, row0=f)
    grads = dict(w_dw=jnp.concatenate([dwd_u, dwd_g], axis=1), b_dw=jnp.concatenate([db_u, db_g], axis=1),
                 g_pre=dg_pre, g_post=dg_post)
    return dx, grads, d_up_t, d_down


def _local_step(x, pos_col, target, p, tie=None, late_weights=None, exchange=None):
    ng = p["norm_g"]
    row = lambda r: ng[r:r + 1]
    freq = _rope_freq_row()
    rope = _rope_tables(pos_col, freq if tie is None else freq + tie[0:1])
    d = x.shape[1]

    h0, *qkv = _qkv_proj(x, row(0), p["w_qkv_t"], rope)
    os_, ls_ = zip(*[_attn_fwd(qkv[g_], g_, d_) for g_, d_ in enumerate(DILATIONS)])
    y_a, x1, mixed, lse = _mix_wo(os_, ls_, p["w_o_t"], x, row(1))
    if late_weights is not None:
        p = {**p, **late_weights(x1)}
    x2, ffn0 = _ffn_fwd(x1, row(2), row(3), p["w_up_t"][0], p["ffn_w_dw"][0], p["ffn_b_dw"][0], p["w_down"][0])
    h1, ag = _norm_matmul(x2, row(4), p["w_pw1_t"], tn=_tile(p["w_pw1_t"].shape[0]), name="conv_pw1", bias=p["b_pw1"], tm=1024)
    u3, u1 = _conv_mid(ag, p["conv_w_dw"], p["conv_b_dw"], p["ln_g"], p["ln_b"])
    y_c, x3 = _matmul_resnorm(u3, p["w_pw2"], x2, row(5), name="conv_pw2", bias=p["b_pw2"])
    x4, ffn1 = _ffn_fwd(x3, row(6), row(7), p["w_up_t"][1], p["ffn_w_dw"][1], p["ffn_b_dw"][1], p["w_down"][1])
    dx4, loss = _loss_grad(x4, target)

    big = [BF16, BF16]

    def tied(r, *tokens):
        tokens = [t for t in tokens if t is not None]
        return row(r) if not tokens else row(r) + jnp.tile(sum(tokens)[0:1], (1, d // 128))

    dx3, gf1, d_up1, d_down1 = _ffn_bwd(ffn1, dx4, row(6), row(7), p["w_up_t"][1], p["ffn_w_dw"][1], p["ffn_b_dw"][1],
                                        p["w_down"][1])
    t0 = exchange.submit("ffn1", [d_up1, d_down1], big) if exchange else None
    dy_c, dg5, db_pw2 = _postnorm_bwd(y_c, tied(5, t0), dx3, name="conv_post_bwd", with_bias_grad=True)
    du3 = _matmul(dy_c, p["w_pw2"], name="conv_du3", out_dtype=F32, transposed_w=True)
    d_wpw2 = _weight_grad(u3, dy_c, name="conv_dw_pw2", out_shape=(1, u3.shape[1], d))
    du1, d_lng, d_lnb, d_cbdw, d_cwdw = _conv_mid_bwd(ag, u1, du3, p["conv_b_dw"], p["ln_g"], p["ln_b"])
    dag, db_pw1 = _glu_conv_bwd(du1, ag, p["conv_w_dw"])
    dx2, dg4 = _matmul_prenorm_bwd([(dag, 0, dag.shape[1], 0)], p["w_pw1_t"], x2, row(4), dx3, name="conv_dx")
    d_wpw1_t = _weight_grad(dag, h1, name="conv_dw_pw1", out_shape=(1, dag.shape[1], d))
    t0 = exchange.advance(dx2) if exchange else None
    t1 = exchange.submit("conv", [d_wpw1_t, d_wpw2], big) if exchange else None
    dx1, gf0, d_up0, d_down0 = _ffn_bwd(ffn0, dx2, row(2), tied(3, t0, t1), p["w_up_t"][0], p["ffn_w_dw"][0], p["ffn_b_dw"][0],
                                        p["w_down"][0])
    t0 = exchange.advance(dx1) if exchange else None
    t1 = exchange.submit("ffn0", [d_up0, d_down0], big) if exchange else None
    dy_a, dg1 = _postnorm_bwd(y_a, tied(1, t0, t1), dx1, name="attn_post_bwd")
    dmix = _matmul(dy_a, p["w_o_t"], name="attn_dmix", out_dtype=BF16, transposed_w=False)
    d_wo_t = _weight_grad(dy_a, mixed, name="attn_dw_o", out_shape=(1, d, GROUP_WIDTH))
    pieces, d_wqkv_t = [], None
    for g_, d_ in enumerate(DILATIONS):
        if exchange and g_ > 0:
            tok = exchange.advance(dkv)
            if tok is not None:
                rope = (rope[0] + tok[0:1], rope[1], rope[2])
        dq, dkv = _attn_bwd(qkv[g_], dmix, mixed, lse, rope, g_, d_)
        for t, (arr, c0) in enumerate(((dq, 0), (dkv, 0), (dkv, GROUP_WIDTH))):
            r0 = (3 * t + g_) * GROUP_WIDTH
            pieces.append((arr, c0, GROUP_WIDTH, r0))
            d_wqkv_t = _weight_grad(arr, h0, name="attn_dw_qkv", a_col0=c0, ka=GROUP_WIDTH, out=d_wqkv_t,
                                    out_shape=(1, p["w_qkv_t"].shape[0], d), row0=r0)
    t0 = exchange.advance(dkv) if exchange else None
    t1 = exchange.submit("attn", [d_wqkv_t, d_wo_t], big) if exchange else None
    grad_x, dg0 = _matmul_prenorm_bwd(pieces, p["w_qkv_t"], x, tied(0, t0, t1), dx1, name="attn_dx")

    grads = dict(
        norm_g=jnp.concatenate([dg0, dg1, gf0["g_pre"], gf0["g_post"], dg4, dg5, gf1["g_pre"], gf1["g_post"]], axis=0),
        w_qkv_t=d_wqkv_t, w_o_t=d_wo_t, w_pw1_t=d_wpw1_t, b_pw1=db_pw1,
        conv_w_dw=d_cwdw[:CONV_KERNEL], conv_b_dw=d_cbdw, ln_g=d_lng, ln_b=d_lnb, w_pw2=d_wpw2, b_pw2=db_pw2,
        w_up_t=[d_up0, d_up1], ffn_w_dw=jnp.stack([gf0["w_dw"], gf1["w_dw"]]),
        ffn_b_dw=jnp.concatenate([gf0["b_dw"], gf1["b_dw"]], axis=0), w_down=[d_down0, d_down1])
    return loss, grad_x, grads


SMALL_AXIS = dict(norm_g=2, conv_b_pw1=1, conv_w_dw=2, conv_b_dw=1, conv_ln_g=1, conv_ln_b=1, conv_b_pw2=1, ffn_w_dw=2)
SMALL = tuple(SMALL_AXIS)
MATMUL_WEIGHTS = dict(attn_w_qkv=True, conv_w_pw1=True, ffn_w_up=True, conv_w_pw2=False, ffn_w_down=False)


def _pack(arrays, cols, row_multiple):
    flat = jnp.concatenate([a.reshape(-1) for a in arrays])
    rows = -(-flat.shape[0] // cols)
    rows = -(-rows // row_multiple) * row_multiple
    return jnp.pad(flat, (0, rows * cols - flat.shape[0])).reshape(rows, cols)


def _unpack(packed, shapes):
    flat = packed.reshape(packed.shape[:-2] + (-1,))
    out, off = [], 0
    for shp in shapes:
        n = math.prod(shp)
        out.append(flat[..., off:off + n].reshape(packed.shape[:-2] + tuple(shp)))
        off += n
    return out


def _join_shards(stacked, axis):
    moved = jnp.moveaxis(stacked, 0, axis)
    shp = moved.shape
    return moved.reshape(shp[:axis] + (shp[axis] * shp[axis + 1],) + shp[axis + 2:])


def _split_shards(whole, axis):
    shp = whole.shape
    cut = whole.reshape(shp[:axis] + (N_DEV, shp[axis] // N_DEV) + shp[axis + 1:])
    return jnp.moveaxis(cut, axis, 0)


def _row_shard(w, transposed):
    t = jnp.swapaxes(w, 1, 2) if transposed else w
    return t.astype(BF16).reshape(-1, t.shape[-1])


def kernel(x, positions, norm_g, attn_w_qkv, attn_w_o, conv_w_pw1, conv_b_pw1, conv_w_dw, conv_b_dw, conv_ln_g, conv_ln_b, conv_w_pw2, conv_b_pw2, ffn_w_up, ffn_w_dw, ffn_b_dw, ffn_w_down, loss_target, m_norm_g, m_attn_w_qkv, m_attn_w_o, m_conv_w_pw1, m_conv_b_pw1, m_conv_w_dw, m_conv_b_dw, m_conv_ln_g, m_conv_ln_b, m_conv_w_pw2, m_conv_b_pw2, m_ffn_w_up, m_ffn_w_dw, m_ffn_b_dw, m_ffn_w_down, v_norm_g, v_attn_w_qkv, v_attn_w_o, v_conv_w_pw1, v_conv_b_pw1, v_conv_w_dw, v_conv_b_dw, v_conv_ln_g, v_conv_ln_b, v_conv_w_pw2, v_conv_b_pw2, v_ffn_w_up, v_ffn_w_dw, v_ffn_b_dw, v_ffn_w_down):
    w = dict(norm_g=norm_g, attn_w_qkv=attn_w_qkv, attn_w_o=attn_w_o, conv_w_pw1=conv_w_pw1, conv_b_pw1=conv_b_pw1,
             conv_w_dw=conv_w_dw, conv_b_dw=conv_b_dw, conv_ln_g=conv_ln_g, conv_ln_b=conv_ln_b, conv_w_pw2=conv_w_pw2,
             conv_b_pw2=conv_b_pw2, ffn_w_up=ffn_w_up, ffn_w_dw=ffn_w_dw, ffn_w_down=ffn_w_down)
    m = dict(norm_g=m_norm_g, attn_w_qkv=m_attn_w_qkv, attn_w_o=m_attn_w_o, conv_w_pw1=m_conv_w_pw1, conv_b_pw1=m_conv_b_pw1,
             conv_w_dw=m_conv_w_dw, conv_b_dw=m_conv_b_dw, conv_ln_g=m_conv_ln_g, conv_ln_b=m_conv_ln_b, conv_w_pw2=m_conv_w_pw2,
             conv_b_pw2=m_conv_b_pw2, ffn_w_up=m_ffn_w_up, ffn_w_dw=m_ffn_w_dw, ffn_w_down=m_ffn_w_down)
    v = dict(norm_g=v_norm_g, attn_w_qkv=v_attn_w_qkv, attn_w_o=v_attn_w_o, conv_w_pw1=v_conv_w_pw1, conv_b_pw1=v_conv_b_pw1,
             conv_w_dw=v_conv_w_dw, conv_b_dw=v_conv_b_dw, conv_ln_g=v_conv_ln_g, conv_ln_b=v_conv_ln_b, conv_w_pw2=v_conv_w_pw2,
             conv_b_pw2=v_conv_b_pw2, ffn_w_up=v_ffn_w_up, ffn_w_dw=v_ffn_w_dw, ffn_w_down=v_ffn_w_down)
    d = x.shape[-1]

    w_qkv_t = _all_gather(_row_shard(attn_w_qkv, True), "gather_w_qkv").reshape(-1, d)
    w_o_t = _all_gather(_row_shard(attn_w_o, True), "gather_w_o").reshape(d, -1)
    small = _all_gather(_pack([w[n] for n in SMALL], 128, 8), "gather_small_weights")
    sm = {n: _join_shards(stacked, SMALL_AXIS[n])
          for n, stacked in zip(SMALL, _unpack(small, [w[n].shape for n in SMALL]))}
    late = {n: t for n, t in MATMUL_WEIGHTS.items() if n != "attn_w_qkv"}
    shares = [_row_shard(w[n], t) for n, t in late.items()]
    rows = [s_.shape[0] for s_ in shares]
    late_share = jnp.concatenate(shares, axis=0)
    send_sems, recv_sems, share_thru, land_thru, tie = _gather_start(late_share)
    me = 4 * lax.axis_index("x") + 2 * lax.axis_index("y") + lax.axis_index("c")

    def late_weights(after):
        big = _gather_wait(send_sems, recv_sems, share_thru, land_thru, after)
        big = lax.dynamic_update_slice(big, late_share[None], (me, 0, 0))
        whole, r0 = {}, 0
        for n, nr in zip(late, rows):
            layers = w[n].shape[0]
            seg = big[:, r0:r0 + nr].reshape(N_DEV, layers, nr // layers, d)
            whole[n] = [seg[:, l_].reshape(-1, d) for l_ in range(layers)]
            r0 += nr
        return dict(w_pw1_t=whole["conv_w_pw1"][0], w_pw2=whole["conv_w_pw2"][0], w_up_t=whole["ffn_w_up"],
                    w_down=whole["ffn_w_down"])

    p = dict(norm_g=sm["norm_g"].reshape(-1, d), w_qkv_t=w_qkv_t, w_o_t=w_o_t, b_pw1=sm["conv_b_pw1"],
             conv_w_dw=sm["conv_w_dw"][0], conv_b_dw=sm["conv_b_dw"], ln_g=sm["conv_ln_g"], ln_b=sm["conv_ln_b"],
             b_pw2=sm["conv_b_pw2"], ffn_w_dw=sm["ffn_w_dw"], ffn_b_dw=[ffn_b_dw[0:1], ffn_b_dw[1:2]])

    exchange = _GradExchange()
    loss, grad_x, g = _local_step(x[0], positions.reshape(-1, 1), loss_target[0], p, tie, late_weights, exchange)
    loss = lax.psum(loss[0, 0], ("x", "y", "c"))
    gsmall = dict(norm_g=g["norm_g"].reshape(norm_g.shape[0], 4, -1), conv_b_pw1=g["b_pw1"], conv_w_dw=g["conv_w_dw"][None],
                  conv_b_dw=g["conv_b_dw"], conv_ln_g=g["ln_g"], conv_ln_b=g["ln_b"], conv_b_pw2=g["b_pw2"], ffn_w_dw=g["ffn_w_dw"])
    small_contrib = jnp.concatenate([_split_shards(gsmall[n], SMALL_AXIS[n]).reshape(N_DEV, -1) for n in SMALL], axis=1)
    srows = small.shape[1]
    small_contrib = jnp.pad(small_contrib, ((0, 0), (0, srows * 128 - small_contrib.shape[1]))).reshape(1, N_DEV, srows, 128)
    exchange.advance(grad_x)
    small_sums = _rs_chips([_rs_pair_add(small_contrib, _rs_sibling([small_contrib])[0], exchange.core, F32)])[0]

    outs = {}

    def update(n, reduced):
        gsum = jnp.swapaxes(reduced, 1, 2) if n == "attn_w_o" or MATMUL_WEIGHTS.get(n) else reduced
        outs[n] = (gsum, *_adamw(gsum, w[n], m[n], v[n], "adamw"))

    (s_up1, s_down1), (s_pw1, s_pw2), (s_up0, s_down0) = exchange.results()[:3]
    update("conv_w_pw1", s_pw1)
    update("conv_w_pw2", s_pw2)
    update("ffn_w_up", jnp.concatenate([s_up0, s_up1], axis=0))
    update("ffn_w_down", jnp.concatenate([s_down0, s_down1], axis=0))
    sshapes = [w[n].shape for n in SMALL]
    souts = _sum_adamw(small_sums[0], *[_pack([t[n] for n in SMALL], 128, 8) for t in (w, m, v)], name="sum_adamw_small")
    for n, vals in zip(SMALL, zip(*[_unpack(o, sshapes) for o in souts])):
        outs[n] = vals
    bparts = _all_gather(_pack([g["ffn_b_dw"]], 128, 8), "gather_bias_grads")
    bouts = _sum_adamw(bparts, *[_pack([t], 128, 8) for t in (ffn_b_dw, m_ffn_b_dw, v_ffn_b_dw)], name="sum_adamw_bias")
    outs["ffn_b_dw"] = tuple(_unpack(o, [ffn_b_dw.shape])[0] for o in bouts)
    done = [outs[n][1][0, :8, :128] for n in ("conv_w_pw1", "conv_w_pw2", "ffn_w_up", "ffn_w_down")]
    exchange.advance(sum(done) + bouts[1][:8] + souts[1][:8])
    s_qkv, s_wo = exchange.results()[3]
    update("attn_w_qkv", s_qkv)
    update("attn_w_o", s_wo)

    order = ("norm_g", "attn_w_qkv", "attn_w_o", "conv_w_pw1", "conv_b_pw1", "conv_w_dw", "conv_b_dw", "conv_ln_g",
             "conv_ln_b", "conv_w_pw2", "conv_b_pw2", "ffn_w_up", "ffn_w_dw", "ffn_b_dw", "ffn_w_down")
    return (loss, grad_x[None], *[outs[n][0] for n in order], *[outs[n][1] for n in order],
            *[outs[n][2] for n in order], *[outs[n][3] for n in order])
```

```python
import math

import numpy as np
import jax
import jax.numpy as jnp
from jax import lax
from jax.experimental import pallas as pl
from jax.experimental.pallas import tpu as pltpu

F32 = jnp.float32
BF16 = jnp.bfloat16
EPS = 1e-6
N_DEV = 8
HEAD_DIM = 64
GROUP_WIDTH = 512
DILATIONS = (1, 4, 16)
SPAN = 128
ROT_DIM = 16
ROPE_THETA = 500000.0
CONV_KERNEL = 31
CONV_HALO = 32
FFN_CONV = 3
ADAM_LR, ADAM_B1, ADAM_B2, ADAM_EPS, ADAM_WD, ADAM_STEP = 0.001, 0.9, 0.999, 1e-08, 0.01, 10
VMEM_LIMIT_BYTES = 56 * 1024 * 1024
MESH = pl.DeviceIdType.MESH
ANY = pl.BlockSpec(memory_space=pl.ANY)
NT = (((1,), (1,)), ((), ()))
TN = (((0,), (0,)), ((), ()))


def _params(*sem):
    return pltpu.CompilerParams(dimension_semantics=sem, vmem_limit_bytes=VMEM_LIMIT_BYTES)


def _sigmoid(v):
    return 1.0 / (1.0 + jnp.exp(-v))


def _full(shape):
    return pl.BlockSpec(shape, lambda *_: (0,) * len(shape))


def _rows(tm, width):
    return pl.BlockSpec((tm, width), lambda i, *_: (i, 0))


def _tile(n, *multiples_of):
    for t in (1408, 1024, 512, 384, 256, 128):
        if n % t == 0 and all(o % t == 0 for o in multiples_of):
            return t
    raise ValueError((n, multiples_of))


def _all_gather(shard, name):
    r, c_ = shard.shape

    def body(x_ref, out_ref, send_sems, recv_sems, local_sem):
        x, y, c = lax.axis_index("x"), lax.axis_index("y"), lax.axis_index("c")
        me, sibling = (x, y, c), (x, y, 1 - c)
        chips = [(1 - x, y), (x, 1 - y), (1 - x, 1 - y)]

        def rows(px, py, pc):
            return out_ref.at[4 * px + 2 * py + pc]

        def copy(k, block, to, src=None):
            return pltpu.make_async_remote_copy(
                src_ref=rows(*block) if src is None else src, dst_ref=rows(*block),
                send_sem=send_sems.at[k], recv_sem=recv_sems.at[k], device_id=to, device_id_type=MESH)

        mine = pltpu.make_async_copy(x_ref, rows(*me), local_sem)
        mine.start()
        first = [copy(0, me, sibling, src=x_ref)]
        first += [copy(1 + j, me, (*chip, c), src=x_ref) for j, chip in enumerate(chips)]
        for cp in first:
            cp.start()
        passed = [copy(4 + j, (*chip, c), sibling) for j, chip in enumerate(chips)]
        for j, chip in enumerate(chips):
            copy(1 + j, (*chip, c), me).wait_recv()
            passed[j].start()
        copy(0, sibling, me).wait_recv()
        for j, chip in enumerate(chips):
            copy(4 + j, (*chip, 1 - c), me).wait_recv()
        for cp in first + passed:
            cp.wait_send()
        mine.wait()

    return pl.pallas_call(
        body, name=name, out_shape=jax.ShapeDtypeStruct((N_DEV, r, c_), shard.dtype),
        in_specs=[ANY], out_specs=ANY,
        scratch_shapes=[pltpu.SemaphoreType.DMA((7,)), pltpu.SemaphoreType.DMA((7,)), pltpu.SemaphoreType.DMA],
    )(shard)


HBM = pl.BlockSpec(memory_space=pltpu.HBM)
SEM = pl.BlockSpec(memory_space=pltpu.SEMAPHORE)
SIDE_EFFECT = pltpu.CompilerParams(has_side_effects=pltpu.SideEffectType.DATAFLOW_SIDE_EFFECTING)


def _gather_start(shard):
    r, c_ = shard.shape

    def body(x_ref, land_ref, send_sems, recv_sems, x_thru, land_thru, token):
        x, y, c = lax.axis_index("x"), lax.axis_index("y"), lax.axis_index("c")
        me = 4 * x + 2 * y + c
        for k in range(1, N_DEV):
            peer = (1 - x if k & 4 else x, 1 - y if k & 2 else y, 1 - c if k & 1 else c)
            pltpu.make_async_remote_copy(src_ref=x_ref, dst_ref=land_ref.at[me], send_sem=send_sems.at[k - 1],
                                         recv_sem=recv_sems.at[k - 1], device_id=peer, device_id_type=MESH).start()
        token[...] = jnp.zeros_like(token)

    land = pltpu.with_memory_space_constraint(lax.empty((N_DEV, r, c_), shard.dtype), pltpu.HBM)
    return pl.pallas_call(
        body, name="gather_late_weights_start",
        out_shape=(pltpu.SemaphoreType.DMA((N_DEV - 1,)), pltpu.SemaphoreType.DMA((N_DEV - 1,)),
                   pltpu.HBM(shard.shape, shard.dtype), pltpu.HBM((N_DEV, r, c_), shard.dtype),
                   jax.ShapeDtypeStruct((8, 128), F32)),
        in_specs=(HBM, HBM), out_specs=(SEM, SEM, HBM, HBM, pl.BlockSpec(memory_space=pltpu.VMEM)),
        input_output_aliases={0: 2, 1: 3}, compiler_params=SIDE_EFFECT,
    )(pltpu.with_memory_space_constraint(shard, pltpu.HBM), land)


def _gather_wait(send_sems, recv_sems, shard_thru, land_thru, after):
    def body(x_ref, land_ref, send_sems, recv_sems, after_ref, x_dead, got_ref):
        x, y, c = lax.axis_index("x"), lax.axis_index("y"), lax.axis_index("c")
        for k in range(N_DEV - 1):
            copy = pltpu.make_async_remote_copy(src_ref=x_ref, dst_ref=land_ref.at[0], send_sem=send_sems.at[k],
                                                recv_sem=recv_sems.at[k], device_id=(x, y, c), device_id_type=MESH)
            copy.wait_send()
            copy.wait_recv()

    return pl.pallas_call(
        body, name="gather_late_weights_wait",
        out_shape=(pltpu.HBM(shard_thru.shape, shard_thru.dtype), pltpu.HBM(land_thru.shape, land_thru.dtype)),
        in_specs=(HBM, HBM, SEM, SEM, ANY), out_specs=(HBM, HBM), input_output_aliases={0: 0, 1: 1},
        compiler_params=SIDE_EFFECT,
    )(shard_thru, land_thru, send_sems, recv_sems, after)[1]


def _hbm(a):
    return pltpu.with_memory_space_constraint(a, pltpu.HBM)


def _exchange_start(name, arrays, lands, plan, ncopies):
    n = len(arrays)

    def body(*refs):
        send_sems, recv_sems, token = refs[2 * n], refs[2 * n + 1], refs[-1]
        x, y, c = lax.axis_index("x"), lax.axis_index("y"), lax.axis_index("c")
        for k, (src, dst, peer) in enumerate(plan(x, y, c, refs[:n], refs[n:2 * n])):
            pltpu.make_async_remote_copy(src_ref=src, dst_ref=dst, send_sem=send_sems.at[k], recv_sem=recv_sems.at[k],
                                         device_id=peer, device_id_type=MESH).start()
        token[...] = jnp.zeros_like(token)

    both = list(arrays) + list(lands)
    outs = pl.pallas_call(
        body, name=name,
        out_shape=(pltpu.SemaphoreType.DMA((ncopies,)), pltpu.SemaphoreType.DMA((ncopies,)),
                   *[pltpu.HBM(a.shape, a.dtype) for a in both], jax.ShapeDtypeStruct((8, 128), F32)),
        in_specs=(HBM,) * (2 * n), out_specs=(SEM, SEM) + (HBM,) * (2 * n) + (pl.BlockSpec(memory_space=pltpu.VMEM),),
        input_output_aliases={i: 2 + i for i in range(2 * n)}, compiler_params=SIDE_EFFECT,
    )(*[_hbm(a) for a in both])
    return outs[0], outs[1], list(outs[2:2 + n]), list(outs[2 + n:2 + 2 * n]), outs[-1]


def _exchange_wait(name, send_sems, recv_sems, arrays, lands, plan, after):
    n = len(arrays)

    def body(*refs):
        send_sems, recv_sems = refs[2 * n], refs[2 * n + 1]
        x, y, c = lax.axis_index("x"), lax.axis_index("y"), lax.axis_index("c")
        for k, (src, dst, peer) in enumerate(plan(x, y, c, refs[:n], refs[n:2 * n])):
            copy = pltpu.make_async_remote_copy(src_ref=src, dst_ref=dst, send_sem=send_sems.at[k], recv_sem=recv_sems.at[k],
                                                device_id=peer, device_id_type=MESH)
            copy.wait_send()
            copy.wait_recv()

    both = list(arrays) + list(lands)
    outs = pl.pallas_call(
        body, name=name, out_shape=tuple(pltpu.HBM(a.shape, a.dtype) for a in both),
        in_specs=(HBM,) * (2 * n) + (SEM, SEM, ANY), out_specs=(HBM,) * (2 * n),
        input_output_aliases={i: i for i in range(2 * n)}, compiler_params=SIDE_EFFECT,
    )(*both, send_sems, recv_sems, after)
    return list(outs[:n]), list(outs[n:])


def _sibling_plan(x, y, c, g_refs, land_refs):
    return [(g.at[:, 2 * q + (1 - c)], o.at[:, q], (x, y, 1 - c)) for g, o in zip(g_refs, land_refs) for q in range(4)]


def _chips_plan(x, y, c, p_refs, land_refs):
    chips = [(1 - x, y), (x, 1 - y), (1 - x, 1 - y)]
    return [(p_.at[:, 2 * qx + qy], o.at[:, 2 * x + y], (qx, qy, c)) for p_, o in zip(p_refs, land_refs) for qx, qy in chips]


class _GradExchange:
    def __init__(self):
        self.core = lax.axis_index("c").astype(jnp.int32).reshape(1)
        self.chip = 2 * lax.axis_index("x") + lax.axis_index("y")
        self.groups = []

    def submit(self, tag, arrays, dtypes):
        arrays = [a.reshape(a.shape[0], N_DEV, a.shape[1] // N_DEV, a.shape[2]) for a in arrays]
        lands = [lax.empty((a.shape[0], 4) + a.shape[2:], a.dtype) for a in arrays]
        send, recv, arrays, lands, token = _exchange_start(f"rs_pair_start_{tag}", arrays, lands, _sibling_plan, 4 * len(arrays))
        self.groups.append(dict(tag=tag, stage=1, sems=(send, recv), arrays=arrays, lands=lands, dtypes=dtypes))
        return token

    def advance(self, after):
        token = None
        for g in self.groups:
            if g["stage"] == 1:
                arrays, got = _exchange_wait(f"rs_pair_wait_{g['tag']}", *g["sems"], g["arrays"], g["lands"], _sibling_plan, after)
                parts = [_rs_pair_add(a, b, self.core, dt) for a, b, dt in zip(arrays, got, g["dtypes"])]
                lands = [lax.empty(p_.shape, p_.dtype) for p_ in parts]
                send, recv, parts, lands, tok = _exchange_start(f"rs_chip_start_{g['tag']}", parts, lands, _chips_plan, 3 * len(parts))
                g.update(stage=2, sems=(send, recv), arrays=parts, lands=lands)
                token = tok if token is None else token + tok
            elif g["stage"] == 2:
                parts, lands = _exchange_wait(f"rs_chip_wait_{g['tag']}", *g["sems"], g["arrays"], g["lands"], _chips_plan, after)
                sums = []
                for p_, land in zip(parts, lands):
                    l, _, r, c_ = p_.shape
                    own = lax.dynamic_slice(p_, (0, self.chip, 0, 0), (l, 1, r, c_))
                    sums.append(_sum_parts(lax.dynamic_update_slice(land, own, (0, self.chip, 0, 0)), "sum_chips"))
                g.update(stage=3, sums=sums)
        return token

    def results(self):
        return [g.get("sums") for g in self.groups]


def _with_rows(g, n):
    return jax.ShapeDtypeStruct((g.shape[0], n) + tuple(g.shape[2:]), g.dtype)


def _rs_sibling(gs):
    n = len(gs)

    def body(*refs):
        g_refs, o_refs, (send_sems, recv_sems) = refs[:n], refs[n:2 * n], refs[2 * n:]
        x, y, c = lax.axis_index("x"), lax.axis_index("y"), lax.axis_index("c")
        copies = [pltpu.make_async_remote_copy(
            src_ref=g_refs[w].at[:, 2 * q + (1 - c)], dst_ref=o_refs[w].at[:, q], send_sem=send_sems.at[4 * w + q],
            recv_sem=recv_sems.at[4 * w + q], device_id=(x, y, 1 - c), device_id_type=MESH)
            for w in range(n) for q in range(4)]
        for cp in copies:
            cp.start()
        for cp in copies:
            cp.wait_recv()
        for cp in copies:
            cp.wait_send()

    return pl.pallas_call(
        body, name="rs_sibling", out_shape=[_with_rows(g, 4) for g in gs],
        in_specs=[ANY] * n, out_specs=[ANY] * n,
        scratch_shapes=[pltpu.SemaphoreType.DMA((4 * n,)), pltpu.SemaphoreType.DMA((4 * n,))],
    )(*gs)


def _rs_pair_add(g, got, core, out_dtype):
    l, _, r, c_ = g.shape

    def body(core_ref, g_ref, got_ref, o_ref):
        o_ref[...] = (g_ref[...].astype(F32) + got_ref[...].astype(F32)).astype(out_dtype)

    blk = (None, None, r, c_)
    return pl.pallas_call(
        body, name="rs_pair_add", out_shape=jax.ShapeDtypeStruct((l, 4, r, c_), out_dtype),
        grid_spec=pltpu.PrefetchScalarGridSpec(
            num_scalar_prefetch=1, grid=(l, 4),
            in_specs=[pl.BlockSpec(blk, lambda i, q, core_ref: (i, 2 * q + core_ref[0], 0, 0)),
                      pl.BlockSpec(blk, lambda i, q, core_ref: (i, q, 0, 0))],
            out_specs=pl.BlockSpec(blk, lambda i, q, core_ref: (i, q, 0, 0))),
        compiler_params=_params("parallel", "parallel"),
    )(core, g, got)


def _rs_chips(parts):
    n = len(parts)

    def body(*refs):
        p_refs, o_refs, (send_sems, recv_sems, local_sems) = refs[:n], refs[n:2 * n], refs[2 * n:]
        x, y, c = lax.axis_index("x"), lax.axis_index("y"), lax.axis_index("c")
        my_chip = 2 * x + y
        chips = [(1 - x, y), (x, 1 - y), (1 - x, 1 - y)]
        local = [pltpu.make_async_copy(p_refs[w].at[:, my_chip], o_refs[w].at[:, my_chip], local_sems.at[w]) for w in range(n)]
        for cp in local:
            cp.start()
        copies = [pltpu.make_async_remote_copy(
            src_ref=p_refs[w].at[:, 2 * qx + qy], dst_ref=o_refs[w].at[:, my_chip], send_sem=send_sems.at[3 * w + k],
            recv_sem=recv_sems.at[3 * w + k], device_id=(qx, qy, c), device_id_type=MESH)
            for w in range(n) for k, (qx, qy) in enumerate(chips)]
        for cp in copies:
            cp.start()
        for cp in copies:
            cp.wait_recv()
        for cp in copies:
            cp.wait_send()
        for cp in local:
            cp.wait()

    return pl.pallas_call(
        body, name="rs_chips", out_shape=[jax.ShapeDtypeStruct(p.shape, p.dtype) for p in parts],
        in_specs=[ANY] * n, out_specs=[ANY] * n,
        scratch_shapes=[pltpu.SemaphoreType.DMA((3 * n,)), pltpu.SemaphoreType.DMA((3 * n,)), pltpu.SemaphoreType.DMA((n,))],
    )(*parts)


def _sum_parts(parts, name):
    l, n, r, c_ = parts.shape

    def body(p_ref, o_ref):
        g = p_ref[0].astype(F32)
        for s in range(1, n):
            g = g + p_ref[s].astype(F32)
        o_ref[...] = g

    return pl.pallas_call(
        body, name=name, out_shape=jax.ShapeDtypeStruct((l, r, c_), F32), grid=(l,),
        in_specs=[pl.BlockSpec((None, n, r, c_), lambda i: (i, 0, 0, 0))],
        out_specs=pl.BlockSpec((None, r, c_), lambda i: (i, 0, 0)), compiler_params=_params("parallel"),
    )(parts)


def _adamw_math(w, g, m, v):
    m = ADAM_B1 * m + (1.0 - ADAM_B1) * g
    v = ADAM_B2 * v + (1.0 - ADAM_B2) * (g * g)
    m_hat = m / (1.0 - ADAM_B1 ** ADAM_STEP)
    v_hat = v / (1.0 - ADAM_B2 ** ADAM_STEP)
    delta = -ADAM_LR * (m_hat / (jnp.sqrt(v_hat) + ADAM_EPS) + ADAM_WD * w)
    return delta, m, v


def _adamw(g, w, m, v, name):
    l, k, n = w.shape
    tk = 256 if k % 256 == 0 else k

    def body(g_ref, w_ref, m_ref, v_ref, d_ref, nm_ref, nv_ref):
        d_ref[...], nm_ref[...], nv_ref[...] = _adamw_math(w_ref[...], g_ref[...], m_ref[...], v_ref[...])

    spec = pl.BlockSpec((None, tk, n), lambda i, j: (i, j, 0))
    return pl.pallas_call(
        body, name=name, out_shape=[jax.ShapeDtypeStruct((l, k, n), F32)] * 3, grid=(l, k // tk),
        in_specs=[spec] * 4, out_specs=[spec] * 3, compiler_params=_params("parallel", "parallel"),
    )(g, w, m, v)


def _sum_adamw(parts, w, m, v, name):
    n, r, c_ = parts.shape

    def body(p_ref, w_ref, m_ref, v_ref, g_ref, d_ref, nm_ref, nv_ref):
        g = p_ref[0]
        for s in range(1, n):
            g = g + p_ref[s]
        g_ref[...] = g
        d_ref[...], nm_ref[...], nv_ref[...] = _adamw_math(w_ref[...], g, m_ref[...], v_ref[...])

    return pl.pallas_call(
        body, name=name, out_shape=[jax.ShapeDtypeStruct((r, c_), F32)] * 4, grid=(1,),
        in_specs=[_full((n, r, c_))] + [_full((r, c_))] * 3, out_specs=[_full((r, c_))] * 4,
        compiler_params=_params("arbitrary"),
    )(parts, w, m, v)


def _rope_tables(pos_col, freq_row):
    s = pos_col.shape[0]
    tm = min(1024, s)

    def body(p_ref, f_ref, c_ref, su_ref, sd_ref):
        ang = p_ref[...].astype(F32) * f_ref[...]
        lane = lax.broadcasted_iota(jnp.int32, ang.shape, 1) & (HEAD_DIM - 1)
        cs, sn = jnp.cos(ang), jnp.sin(ang)
        c_ref[...] = jnp.where(lane < ROT_DIM, cs, 1.0)
        su_ref[...] = jnp.where((lane >= ROT_DIM // 2) & (lane < ROT_DIM), sn, 0.0)
        sd_ref[...] = jnp.where(lane < ROT_DIM // 2, -sn, 0.0)

    return pl.pallas_call(
        body, name="rope_tables", out_shape=[jax.ShapeDtypeStruct((s, 128), F32)] * 3, grid=(s // tm,),
        in_specs=[pl.BlockSpec((tm, 1), lambda i: (i, 0)), _full((1, 128))],
        out_specs=[_rows(tm, 128)] * 3, compiler_params=_params("parallel"),
    )(pos_col, freq_row)


def _rope_apply(t, cos, sin_up, sin_dn):
    w = t.shape[1]
    return t * cos + pltpu.roll(t, 8, 1) * sin_up + pltpu.roll(t, w - 8, 1) * sin_dn


def _rope_transpose(dr, cos, sin_up, sin_dn):
    w = dr.shape[1]
    return dr * cos + pltpu.roll(dr * sin_up, w - 8, 1) + pltpu.roll(dr * sin_dn, 8, 1)


def _norm_matmul(x, g, wt, *, tn, name, bias=None, tm=1024):
    s, d = x.shape
    n = wt.shape[0]
    tm = min(tm, s)

    def body(*refs):
        x_ref, g_ref, w_ref = refs[:3]
        b_ref = refs[3] if bias is not None else None
        h_ref, o_ref = refs[-2:]

        @pl.when(pl.program_id(1) == 0)
        def _():
            xv = x_ref[...]
            r = lax.rsqrt(jnp.mean(xv * xv, axis=-1, keepdims=True) + EPS)
            h_ref[...] = (xv * r * g_ref[...]).astype(BF16)

        acc = lax.dot_general(h_ref[...], w_ref[...], NT, preferred_element_type=F32)
        if b_ref is not None:
            acc = acc + b_ref[...]
        o_ref[...] = acc.astype(BF16)

    in_specs = [_rows(tm, d), _full((1, d)), pl.BlockSpec((tn, d), lambda i, j: (j, 0))]
    args = [x, g, wt]
    if bias is not None:
        in_specs.append(pl.BlockSpec((1, tn), lambda i, j: (0, j)))
        args.append(bias)
    return pl.pallas_call(
        body, name=name,
        out_shape=[jax.ShapeDtypeStruct((s, d), BF16), jax.ShapeDtypeStruct((s, n), BF16)],
        grid=(s // tm, n // tn), in_specs=in_specs,
        out_specs=[_rows(tm, d), pl.BlockSpec((tm, tn), lambda i, j: (i, j))],
        compiler_params=_params("parallel", "arbitrary"),
    )(*args)


def _class_major(tm, dil):
    p = np.zeros((tm, tm), np.float32)
    per = tm // dil
    for r in range(dil):
        for j in range(per):
            p[r * per + j, j * dil + r] = 1.0
    return jnp.asarray(p, dtype=BF16)


def _qkv_proj(x, g, wt, rope, tm=512):
    s, d = x.shape
    n = wt.shape[0]
    gw3 = 3 * GROUP_WIDTH
    tm = min(tm, s)
    assert n == 3 * gw3

    def body(x_ref, g_ref, w_ref, c_ref, su_ref, sd_ref, p1_ref, p2_ref, h_ref, o0_ref, o1_ref, o2_ref):
        j = pl.program_id(1)

        @pl.when(j == 0)
        def _():
            xv = x_ref[...]
            r = lax.rsqrt(jnp.mean(xv * xv, axis=-1, keepdims=True) + EPS)
            h_ref[...] = (xv * r * g_ref[...]).astype(BF16)

        acc = lax.dot_general(h_ref[...], w_ref[...], NT, preferred_element_type=F32)

        def store(y):
            yb = y.astype(BF16)
            o0_ref[:, pl.ds(pl.multiple_of(j * GROUP_WIDTH, GROUP_WIDTH), GROUP_WIDTH)] = yb[:, :GROUP_WIDTH]
            for grp, o_ref, p_ref in ((1, o1_ref, p1_ref), (2, o2_ref, p2_ref)):
                dil = DILATIONS[grp]
                per = tm // dil
                yp = jnp.dot(p_ref[...], yb[:, grp * GROUP_WIDTH:(grp + 1) * GROUP_WIDTH],
                             preferred_element_type=F32).astype(BF16)
                for r in range(dil):
                    col = pl.multiple_of(r * gw3 + j * GROUP_WIDTH, GROUP_WIDTH)
                    o_ref[:, pl.ds(col, GROUP_WIDTH)] = yp[r * per:(r + 1) * per, :]

        @pl.when(j < 2)
        def _():
            reps = gw3 // 128
            store(_rope_apply(acc, jnp.tile(c_ref[...], (1, reps)), jnp.tile(su_ref[...], (1, reps)),
                              jnp.tile(sd_ref[...], (1, reps))))

        @pl.when(j == 2)
        def _():
            store(acc)

    outs = [jax.ShapeDtypeStruct((s, d), BF16)] + [jax.ShapeDtypeStruct((s // dl, dl * gw3), BF16) for dl in DILATIONS]
    out_specs = [_rows(tm, d)] + [_rows(tm // dl, dl * gw3) for dl in DILATIONS]
    return pl.pallas_call(
        body, name="attn_qkv", out_shape=outs, grid=(s // tm, 3),
        in_specs=[_rows(tm, d), _full((1, d)), pl.BlockSpec((gw3, d), lambda i, j: (j, 0))] + [_rows(tm, 128)] * 3
        + [_full((tm, tm))] * 2,
        out_specs=out_specs, compiler_params=_params("parallel", "arbitrary"),
    )(x, g, wt, *rope, _class_major(tm, DILATIONS[1]), _class_major(tm, DILATIONS[2]))


def _head_masks(rows=SPAN):
    lane = lax.broadcasted_iota(jnp.int32, (rows, 128), 1)
    masks = [lane < HEAD_DIM, lane >= HEAD_DIM]
    lane1 = lax.broadcasted_iota(jnp.int32, (1, 128), 1)
    keep = [jnp.where(lane1 < HEAD_DIM, 1.0, 0.0).astype(BF16), jnp.where(lane1 >= HEAD_DIM, 1.0, 0.0).astype(BF16)]
    return masks, keep


def _attn_fwd(qv, grp, dil):
    l = qv.shape[0]
    s = l * dil
    nb = l // SPAN

    def body(q_ref, kp_ref, kc_ref, vp_ref, vc_ref, o_ref, l_ref):
        b = pl.program_id(1)
        row = lax.broadcasted_iota(jnp.int32, (SPAN, 2 * SPAN), 0)
        col = lax.broadcasted_iota(jnp.int32, (SPAN, 2 * SPAN), 1)
        no_prev = jnp.where(b > 0, 0, 4 * SPAN)
        valid = ((col < SPAN) & (col >= row + no_prev)) | ((col >= SPAN) & (col - SPAN <= row))
        masks, keep = _head_masks()
        for p in range(GROUP_WIDTH // 128):
            sl = slice(p * 128, (p + 1) * 128)
            qp = q_ref[:, sl]
            kk = jnp.concatenate([kp_ref[:, sl], kc_ref[:, sl]], axis=0)
            vv = jnp.concatenate([vp_ref[:, sl], vc_ref[:, sl]], axis=0)
            outs, lses = [], []
            for h in range(2):
                sc = lax.dot_general(qp * keep[h], kk, NT, preferred_element_type=F32) * (HEAD_DIM ** -0.5)
                sc = jnp.where(valid, sc, -1e30)
                mx = jnp.max(sc, axis=-1, keepdims=True)
                pe = jnp.exp(sc - mx)
                den = jnp.sum(pe, axis=-1, keepdims=True)
                pv = jnp.dot(pe.astype(BF16), vv, preferred_element_type=F32)
                outs.append(pv / den)
                lses.append(jnp.broadcast_to(mx + jnp.log(den), (SPAN, 128)))
            o_ref[:, sl] = jnp.where(masks[0], outs[0], outs[1])
            l_ref[:, sl] = jnp.where(masks[0], lses[0], lses[1])

    blk = (SPAN, GROUP_WIDTH)
    cur = lambda t: pl.BlockSpec(blk, lambda r, b: (b, r * 3 + t))
    prev = lambda t: pl.BlockSpec(blk, lambda r, b: (jnp.maximum(b - 1, 0), r * 3 + t))
    out = pl.BlockSpec(blk, lambda r, b: (b, r))
    o, lse = pl.pallas_call(
        body, name=f"attn_fwd_g{grp}", out_shape=[jax.ShapeDtypeStruct((l, dil * GROUP_WIDTH), F32)] * 2,
        grid=(dil, nb), in_specs=[cur(0), prev(1), cur(1), prev(2), cur(2)], out_specs=[out, out],
        compiler_params=_params("parallel", "arbitrary"),
    )(qv, qv, qv, qv, qv)
    return o.reshape(s, GROUP_WIDTH), lse.reshape(s, GROUP_WIDTH)


def _resnorm_store(y, x_ref, g_ref, y_ref, xo_ref):
    r = lax.rsqrt(jnp.mean(y * y, axis=-1, keepdims=True) + EPS)
    y_ref[...] = y
    xo_ref[...] = x_ref[...] + y * r * g_ref[...]


def _mix_wo(os_, ls_, wot, x, g, tm=512):
    s, d = x.shape
    gw = wot.shape[1]
    tm = min(tm, s)

    def body(o0, o1, o2, l0, l1, l2, w_ref, x_ref, g_ref, y_ref, xo_ref, mixed_ref, lse_ref):
        a0, a1, a2 = l0[...], l1[...], l2[...]
        mx = jnp.maximum(jnp.maximum(a0, a1), a2)
        e0, e1, e2 = jnp.exp(a0 - mx), jnp.exp(a1 - mx), jnp.exp(a2 - mx)
        den = e0 + e1 + e2
        mixed = (e0 / den) * o0[...] + (e1 / den) * o1[...] + (e2 / den) * o2[...]
        mixed_ref[...] = mixed.astype(BF16)
        lse_ref[...] = mx + jnp.log(den)
        y = lax.dot_general(mixed.astype(BF16), w_ref[...], NT, preferred_element_type=F32)
        _resnorm_store(y, x_ref, g_ref, y_ref, xo_ref)

    return pl.pallas_call(
        body, name="mix_wo",
        out_shape=[jax.ShapeDtypeStruct((s, d), F32), jax.ShapeDtypeStruct((s, d), F32),
                   jax.ShapeDtypeStruct((s, gw), BF16), jax.ShapeDtypeStruct((s, gw), F32)],
        grid=(s // tm,), in_specs=[_rows(tm, gw)] * 6 + [_full((d, gw)), _rows(tm, d), _full((1, d))],
        out_specs=[_rows(tm, d), _rows(tm, d), _rows(tm, gw), _rows(tm, gw)],
        compiler_params=_params("parallel"),
    )(*os_, *ls_, wot, x, g)


def _matmul_resnorm(a, w, x, g, *, name, bias=None, tm=512):
    s, k = a.shape
    d = w.shape[1]
    tm = min(tm, s)

    def body(*refs):
        a_ref, w_ref = refs[:2]
        b_ref = refs[2] if bias is not None else None
        x_ref, g_ref, y_ref, xo_ref = refs[-4:]
        y = jnp.dot(a_ref[...], w_ref[...], preferred_element_type=F32)
        if b_ref is not None:
            y = y + b_ref[...]
        _resnorm_store(y, x_ref, g_ref, y_ref, xo_ref)

    in_specs = [_rows(tm, k), _full((k, d))] + ([_full((1, d))] if bias is not None else []) + [_rows(tm, d), _full((1, d))]
    args = [a, w] + ([bias] if bias is not None else []) + [x, g]
    return pl.pallas_call(
        body, name=name, out_shape=[jax.ShapeDtypeStruct((s, d), F32)] * 2, grid=(s // tm,),
        in_specs=in_specs, out_specs=[_rows(tm, d)] * 2, compiler_params=_params("parallel"),
    )(*args)


FFN_SUB = 256


def _conv3_rows(z_ref, halo_ref, rb, sub, cs, first):
    zc = z_ref[rb * sub:(rb + 1) * sub, cs].astype(F32)
    if rb == 0:
        halo = halo_ref[:, cs].astype(F32) * jnp.where(first, 0.0, 1.0)
    else:
        halo = z_ref[rb * sub - 16:rb * sub, cs].astype(F32)[8:]
    z2, z1 = _conv3_taps(zc, halo)
    return z2, z1, zc


def _conv3_taps(z, halo):
    row = lax.broadcasted_iota(jnp.int32, (8, z.shape[1]), 0)
    h6, h7 = halo[6:7, :], halo[7:8, :]
    r1, r2 = pltpu.roll(z, 1, 0), pltpu.roll(z, 2, 0)
    z1 = jnp.concatenate([jnp.where(row == 0, h7, r1[0:8]), r1[8:]], axis=0)
    z2 = jnp.concatenate([jnp.where(row == 0, h6, jnp.where(row == 1, h7, r2[0:8])), r2[8:]], axis=0)
    return z2, z1


def _ffn_cols(f):
    return _tile(f)


def _lane_chunks(width, fn):
    def step(k, carry):
        fn(pl.ds(pl.multiple_of(k * 128, 128), 128))
        return carry

    lax.fori_loop(0, width // 128, step, 0)


def _ffn_act(z, w_dw, b_dw, tm=1024):
    s, f2 = z.shape
    f = f2 // 2
    tm = min(tm, s)
    sub = min(FFN_SUB, tm)
    tc = _ffn_cols(f)
    nfc = f // tc

    def body(zu, zg, hu, hg, wu, wg, bu, bg, o_ref):
        first = pl.program_id(0) == 0

        def chunk(cs):
            for rb in range(tm // sub):
                def conv(z_ref, h_ref, w_ref, b_ref):
                    z2, z1, zc = _conv3_rows(z_ref, h_ref, rb, sub, cs, first)
                    return w_ref[0:1, cs] * z2 + w_ref[1:2, cs] * z1 + w_ref[2:3, cs] * zc + b_ref[:, cs]

                up, gate = conv(zu, hu, wu, bu), conv(zg, hg, wg, bg)
                o_ref[rb * sub:(rb + 1) * sub, cs] = (gate * _sigmoid(gate) * up).astype(BF16)

        _lane_chunks(tc, chunk)

    hb = tm // 8
    tile = lambda off: pl.BlockSpec((tm, tc), lambda i, j: (i, off + j))
    halo = lambda off: pl.BlockSpec((8, tc), lambda i, j: (jnp.maximum(i * hb - 1, 0), off + j))
    prm = lambda rows, off: pl.BlockSpec((rows, tc), lambda i, j: (0, off + j))
    return pl.pallas_call(
        body, name="ffn_act", out_shape=jax.ShapeDtypeStruct((s, f), BF16), grid=(s // tm, nfc),
        in_specs=[tile(0), tile(nfc), halo(0), halo(nfc), prm(FFN_CONV, 0), prm(FFN_CONV, nfc), prm(1, 0), prm(1, nfc)],
        out_specs=pl.BlockSpec((tm, tc), lambda i, j: (i, j)), compiler_params=_params("parallel", "parallel"),
    )(z, z, z, z, w_dw, w_dw, b_dw, b_dw)


def _shifted_planes(ext_ref):
    rows = ext_ref.shape[1]
    for s in range(1, 8):
        ext_ref[s, 0:rows - 8, :] = ext_ref[0, s:s + rows - 8, :]


def _window(ext_ref, off, tm, cs):
    s = off % 8
    return ext_ref[s, off - s:off - s + tm, cs]


def _conv_taps(ext_ref, w_ref, offs, tm, out_ref):
    def chunk(cs):
        acc = w_ref[0:1, cs] * _window(ext_ref, offs[0], tm, cs)
        for j in range(1, len(offs)):
            acc = acc + w_ref[j:j + 1, cs] * _window(ext_ref, offs[j], tm, cs)
        out_ref[:, cs] = acc

    _lane_chunks(out_ref.shape[1], chunk)


def _glu_planes(ag_ref, halo_ref, ext_ref, first, c):
    hal = halo_ref[...].astype(F32)
    ext_ref[0, 0:CONV_HALO, :] = hal[:, :c] * _sigmoid(hal[:, c:]) * jnp.where(first, 0.0, 1.0)
    ag = ag_ref[...].astype(F32)
    ext_ref[0, CONV_HALO:, :] = ag[:, :c] * _sigmoid(ag[:, c:])
    _shifted_planes(ext_ref)


def _layernorm_stats(u1):
    mu = jnp.mean(u1, axis=-1, keepdims=True)
    cen = u1 - mu
    rstd = lax.rsqrt(jnp.mean(cen * cen, axis=-1, keepdims=True) + EPS)
    return cen * rstd, rstd


def _conv_mid(ag, w_dw, b_dw, ln_g, ln_b, tm=256):
    s, c2 = ag.shape
    c = c2 // 2
    tm = min(tm, s)

    def body(ag_ref, halo_ref, w_ref, b_ref, g_ref, bb_ref, o_ref, u1_ref, ext_ref):
        _glu_planes(ag_ref, halo_ref, ext_ref, pl.program_id(0) == 0, c)
        base = CONV_HALO - (CONV_KERNEL - 1)
        _conv_taps(ext_ref, w_ref, [base + j for j in range(CONV_KERNEL)], tm, u1_ref)
        xh, _ = _layernorm_stats(u1_ref[...] + b_ref[...])
        u2 = xh * g_ref[...] + bb_ref[...]
        o_ref[...] = (u2 * _sigmoid(u2)).astype(BF16)

    hb = tm // CONV_HALO
    return pl.pallas_call(
        body, name="conv_mid", out_shape=[jax.ShapeDtypeStruct((s, c), BF16), jax.ShapeDtypeStruct((s, c), F32)], grid=(s // tm,),
        in_specs=[_rows(tm, c2), pl.BlockSpec((CONV_HALO, c2), lambda i: (jnp.maximum(i * hb - 1, 0), 0)),
                  _full((CONV_KERNEL, c)), _full((1, c)), _full((1, c)), _full((1, c))],
        out_specs=[_rows(tm, c), _rows(tm, c)], scratch_shapes=[pltpu.VMEM((8, CONV_HALO + tm, c), F32)],
        compiler_params=_params("arbitrary"),
    )(ag, ag, w_dw, b_dw, ln_g, ln_b)


def _loss_grad(xo, target, tm=1024):
    s, d = xo.shape
    tm = min(tm, s)

    def body(x_ref, t_ref, dx_ref, loss_ref):
        @pl.when(pl.program_id(0) == 0)
        def _():
            loss_ref[...] = jnp.zeros_like(loss_ref)

        err = x_ref[...] - t_ref[...]
        dx_ref[...] = err * (1.0 / d)
        loss_ref[...] += 0.5 * jnp.sum(jnp.mean(err * err, axis=-1, keepdims=True))

    return pl.pallas_call(
        body, name="loss_grad", out_shape=[jax.ShapeDtypeStruct((s, d), F32), jax.ShapeDtypeStruct((1, 128), F32)],
        grid=(s // tm,), in_specs=[_rows(tm, d)] * 2, out_specs=[_rows(tm, d), _full((1, 128))],
        compiler_params=_params("arbitrary"),
    )(xo, target)


def _postnorm_bwd(y, g, dxo, *, name, with_bias_grad=False, tm=1024):
    s, d = y.shape
    tm = min(tm, s)

    def body(y_ref, g_ref, dx_ref, dy_ref, dg_ref, *rest):
        @pl.when(pl.program_id(0) == 0)
        def _():
            dg_ref[...] = jnp.zeros_like(dg_ref)
            for r_ in rest:
                r_[...] = jnp.zeros_like(r_)

        yv, dxo_v = y_ref[...], dx_ref[...]
        r = lax.rsqrt(jnp.mean(yv * yv, axis=-1, keepdims=True) + EPS)
        yh = yv * r
        dyh = dxo_v * g_ref[...]
        dy = r * (dyh - yh * jnp.mean(dyh * yh, axis=-1, keepdims=True))
        dy_ref[...] = dy.astype(BF16)
        dg_ref[...] += jnp.sum(dxo_v * yh, axis=0, keepdims=True)
        for r_ in rest:
            r_[...] += jnp.sum(dy, axis=0, keepdims=True)

    nacc = 2 if with_bias_grad else 1
    return pl.pallas_call(
        body, name=name, out_shape=[jax.ShapeDtypeStruct((s, d), BF16)] + [jax.ShapeDtypeStruct((1, d), F32)] * nacc,
        grid=(s // tm,), in_specs=[_rows(tm, d), _full((1, d)), _rows(tm, d)],
        out_specs=[_rows(tm, d)] + [_full((1, d))] * nacc, compiler_params=_params("arbitrary"),
    )(y, g, dxo)


def _matmul(gmat, w, *, name, out_dtype, transposed_w, tm=512):
    s, k = gmat.shape
    n = w.shape[0] if transposed_w else w.shape[1]
    tm = min(tm, s)

    def body(g_ref, w_ref, o_ref):
        if transposed_w:
            acc = lax.dot_general(g_ref[...], w_ref[...], NT, preferred_element_type=F32)
        else:
            acc = jnp.dot(g_ref[...], w_ref[...], preferred_element_type=F32)
        o_ref[...] = acc.astype(out_dtype)

    return pl.pallas_call(
        body, name=name, out_shape=jax.ShapeDtypeStruct((s, n), out_dtype), grid=(s // tm,),
        in_specs=[_rows(tm, k), _full(w.shape)], out_specs=_rows(tm, n), compiler_params=_params("parallel"),
    )(gmat, w)


def _matmul_prenorm_bwd(pieces, wt, x, g, dres, *, name, tm=256):
    s, d = x.shape
    tm = min(tm, s)
    np_ = len(pieces)

    def body(*refs):
        p_refs, w_refs = refs[:np_], refs[np_:2 * np_]
        x_ref, g_ref, r_ref, dx_ref, dg_ref = refs[2 * np_:]

        @pl.when(pl.program_id(0) == 0)
        def _():
            dg_ref[...] = jnp.zeros_like(dg_ref)

        dh = None
        for p_ref, w_ref in zip(p_refs, w_refs):
            t = jnp.dot(p_ref[...], w_ref[...], preferred_element_type=F32)
            dh = t if dh is None else dh + t
        xv = x_ref[...]
        r = lax.rsqrt(jnp.mean(xv * xv, axis=-1, keepdims=True) + EPS)
        xh = xv * r
        dyh = dh * g_ref[...]
        dx_ref[...] = r_ref[...] + r * (dyh - xh * jnp.mean(dyh * xh, axis=-1, keepdims=True))
        dg_ref[...] += jnp.sum(dh * xh, axis=0, keepdims=True)

    in_specs = []
    for _, c0, kc, _ in pieces:
        assert c0 % kc == 0
        in_specs.append(pl.BlockSpec((tm, kc), lambda i, _b=c0 // kc: (i, _b)))
    for _, _, kc, r0 in pieces:
        assert r0 % kc == 0
        in_specs.append(pl.BlockSpec((kc, d), lambda i, _b=r0 // kc: (_b, 0)))
    in_specs += [_rows(tm, d), _full((1, d)), _rows(tm, d)]
    return pl.pallas_call(
        body, name=name, out_shape=[jax.ShapeDtypeStruct((s, d), F32), jax.ShapeDtypeStruct((1, d), F32)],
        grid=(s // tm,), in_specs=in_specs, out_specs=[_rows(tm, d), _full((1, d))],
        compiler_params=_params("arbitrary"),
    )(*[p[0] for p in pieces], *[wt] * np_, x, g, dres)


def _weight_grad(a, gmat, *, name, a_col0=0, ka=None, out=None, out_shape=None, layer=0, row0=0, ts=1024):
    s = a.shape[0]
    ka = a.shape[1] if ka is None else ka
    n = gmat.shape[1]
    ts = min(ts, s)
    tka = _tile(ka, a_col0, row0)
    shape = out.shape if out is not None else out_shape
    nsteps = s // ts

    def body(a_ref, g_ref, *rest):
        o_ref, acc_ref = rest[-2:]
        i = pl.program_id(1)

        @pl.when(i == 0)
        def _():
            acc_ref[...] = jnp.zeros_like(acc_ref)

        acc_ref[...] += lax.dot_general(a_ref[...], g_ref[...], TN, preferred_element_type=F32)

        @pl.when(i == nsteps - 1)
        def _():
            o_ref[...] = acc_ref[...].astype(BF16)

    in_specs = [pl.BlockSpec((ts, tka), lambda k, i: (i, a_col0 // tka + k)), pl.BlockSpec((ts, n), lambda k, i: (i, 0))]
    args = [a, gmat]
    aliases = {}
    if out is not None:
        in_specs.append(ANY)
        args.append(out)
        aliases = {2: 0}
    return pl.pallas_call(
        body, name=name, out_shape=jax.ShapeDtypeStruct(shape, BF16), grid=(ka // tka, nsteps), in_specs=in_specs,
        out_specs=pl.BlockSpec((None, tka, n), lambda k, i: (layer, row0 // tka + k, 0)),
        scratch_shapes=[pltpu.VMEM((tka, n), F32)],
        input_output_aliases=aliases, compiler_params=_params("parallel", "arbitrary"),
    )(*args)


def _ffn_act_bwd(z, dact, w_dw, b_dw, tm=512):
    s, f2 = z.shape
    f = f2 // 2
    tm = min(tm, s)
    sub = min(FFN_SUB // 2, tm)
    tc = _ffn_cols(f)
    nfc = f // tc

    def body(zu, zg, hu, hg, wu, wg, bu, bg, da_ref, du_ref, dgt_ref, dbu_ref, dbg_ref, dwu_ref, dwg_ref):
        i = pl.program_id(1)

        @pl.when(i == 0)
        def _():
            for r_ in (dbu_ref, dbg_ref, dwu_ref, dwg_ref):
                r_[...] = jnp.zeros_like(r_)

        def chunk(cs):
            for rb in range(tm // sub):
                rows = slice(rb * sub, (rb + 1) * sub)

                def conv(z_ref, h_ref, w_ref, b_ref):
                    taps = _conv3_rows(z_ref, h_ref, rb, sub, cs, i == 0)
                    return taps, w_ref[0:1, cs] * taps[0] + w_ref[1:2, cs] * taps[1] + w_ref[2:3, cs] * taps[2] + b_ref[:, cs]

                taps_u, up = conv(zu, hu, wu, bu)
                taps_g, gate = conv(zg, hg, wg, bg)
                da = da_ref[rows, cs].astype(F32)
                sg = _sigmoid(gate)
                d_up = da * (gate * sg)
                d_gate = da * up * (sg * (1.0 + gate * (1.0 - sg)))
                du_ref[rows, cs] = d_up.astype(BF16)
                dgt_ref[rows, cs] = d_gate.astype(BF16)
                for dv, taps, db_ref, dw_ref in ((d_up, taps_u, dbu_ref, dwu_ref), (d_gate, taps_g, dbg_ref, dwg_ref)):
                    db_ref[:, cs] += jnp.sum(dv, axis=0, keepdims=True)
                    for k_, tap in enumerate(taps):
                        dw_ref[k_:k_ + 1, cs] += jnp.sum(dv * tap, axis=0, keepdims=True)

        _lane_chunks(tc, chunk)

    hb = tm // 8
    tile = lambda off: pl.BlockSpec((tm, tc), lambda j, i: (i, off + j))
    halo = lambda off: pl.BlockSpec((8, tc), lambda j, i: (jnp.maximum(i * hb - 1, 0), off + j))
    prm = lambda rows, off: pl.BlockSpec((rows, tc), lambda j, i: (0, off + j))
    acc = lambda rows: pl.BlockSpec((rows, tc), lambda j, i: (0, j))
    return pl.pallas_call(
        body, name="ffn_act_bwd",
        out_shape=[jax.ShapeDtypeStruct((s, f), BF16)] * 2 + [jax.ShapeDtypeStruct((1, f), F32)] * 2
        + [jax.ShapeDtypeStruct((FFN_CONV, f), F32)] * 2,
        grid=(nfc, s // tm),
        in_specs=[tile(0), tile(nfc), halo(0), halo(nfc), prm(FFN_CONV, 0), prm(FFN_CONV, nfc), prm(1, 0), prm(1, nfc), tile(0)],
        out_specs=[tile(0), tile(0), acc(1), acc(1), acc(FFN_CONV), acc(FFN_CONV)],
        compiler_params=_params("parallel", "arbitrary"),
    )(z, z, z, z, w_dw, w_dw, b_dw, b_dw, dact)


def _conv3_transpose(dug, w_dw, col0, tm=1024):
    s, f = dug.shape
    tm = min(tm, s)
    sub = min(FFN_SUB, tm)
    nsub = tm // sub
    tc = _ffn_cols(f)
    nfc = f // tc
    nrow = s // tm
    off = col0 // tc

    def body(d_ref, n_ref, w_ref, o_ref):
        keep_next = jnp.where(pl.program_id(0) == nrow - 1, 0.0, 1.0)

        def chunk(cs):
            for rb in range(nsub):
                rows = slice(rb * sub, (rb + 1) * sub)
                dv = d_ref[rows, cs].astype(F32)
                if rb == nsub - 1:
                    nxt = n_ref[:, cs].astype(F32) * keep_next
                else:
                    nxt = d_ref[(rb + 1) * sub:(rb + 1) * sub + 16, cs].astype(F32)[:8]
                n0, n1 = nxt[0:1, :], nxt[1:2, :]
                row = lax.broadcasted_iota(jnp.int32, (8, dv.shape[1]), 0)
                r1, r2 = pltpu.roll(dv, sub - 1, 0), pltpu.roll(dv, sub - 2, 0)
                d1 = jnp.concatenate([r1[:sub - 8], jnp.where(row == 7, n0, r1[sub - 8:])], axis=0)
                d2 = jnp.concatenate([r2[:sub - 8], jnp.where(row == 7, n1, jnp.where(row == 6, n0, r2[sub - 8:]))], axis=0)
                o_ref[rows, cs] = (w_ref[2:3, cs] * dv + w_ref[1:2, cs] * d1 + w_ref[0:1, cs] * d2).astype(BF16)

        _lane_chunks(tc, chunk)

    hb = tm // 8
    return pl.pallas_call(
        body, name="conv3_transpose", out_shape=jax.ShapeDtypeStruct((s, f), BF16), grid=(nrow, nfc),
        in_specs=[pl.BlockSpec((tm, tc), lambda i, j: (i, j)),
                  pl.BlockSpec((8, tc), lambda i, j: (jnp.minimum((i + 1) * hb, s // 8 - 1), j)),
                  pl.BlockSpec((FFN_CONV, tc), lambda i, j: (0, off + j))],
        out_specs=pl.BlockSpec((tm, tc), lambda i, j: (i, j)), compiler_params=_params("parallel", "parallel"),
    )(dug, dug, w_dw)


def _conv_mid_bwd(ag, u1, du3, b_dw, ln_g, ln_b, tm=256):
    s, c2 = ag.shape
    c = c2 // 2
    tm = min(tm, s)

    def body(ag_ref, halo_ref, u1in_ref, du_ref, b_ref, g_ref, bb_ref, o_ref, dlg_ref, dlb_ref, db_ref, dw_ref, ext_ref, u1_ref):
        @pl.when(pl.program_id(0) == 0)
        def _():
            for r_ in (dlg_ref, dlb_ref, db_ref, dw_ref):
                r_[...] = jnp.zeros_like(r_)

        _glu_planes(ag_ref, halo_ref, ext_ref, pl.program_id(0) == 0, c)
        xh, rstd = _layernorm_stats(u1in_ref[...] + b_ref[...])
        u2 = xh * g_ref[...] + bb_ref[...]
        sg = _sigmoid(u2)
        du2 = du_ref[...] * (sg * (1.0 + u2 * (1.0 - sg)))
        dlg_ref[...] += jnp.sum(du2 * xh, axis=0, keepdims=True)
        dlb_ref[...] += jnp.sum(du2, axis=0, keepdims=True)
        dxh = du2 * g_ref[...]
        du1 = rstd * (dxh - jnp.mean(dxh, axis=-1, keepdims=True) - xh * jnp.mean(dxh * xh, axis=-1, keepdims=True))
        o_ref[...] = du1.astype(BF16)
        db_ref[...] += jnp.sum(du1, axis=0, keepdims=True)
        u1_ref[...] = du1
        base = CONV_HALO - (CONV_KERNEL - 1)

        def chunk(cs):
            dc = u1_ref[:, cs]
            for j in range(CONV_KERNEL):
                dw_ref[j:j + 1, cs] += jnp.sum(dc * _window(ext_ref, base + j, tm, cs), axis=0, keepdims=True)

        _lane_chunks(c, chunk)

    hb = tm // CONV_HALO
    vec = _full((1, c))
    return pl.pallas_call(
        body, name="conv_mid_bwd",
        out_shape=[jax.ShapeDtypeStruct((s, c), BF16)] + [jax.ShapeDtypeStruct((1, c), F32)] * 3
        + [jax.ShapeDtypeStruct((CONV_HALO, c), F32)],
        grid=(s // tm,),
        in_specs=[_rows(tm, c2), pl.BlockSpec((CONV_HALO, c2), lambda i: (jnp.maximum(i * hb - 1, 0), 0)), _rows(tm, c),
                  _rows(tm, c), vec, vec, vec],
        out_specs=[_rows(tm, c), vec, vec, vec, _full((CONV_HALO, c))],
        scratch_shapes=[pltpu.VMEM((8, CONV_HALO + tm, c), F32), pltpu.VMEM((tm, c), F32)],
        compiler_params=_params("arbitrary"),
    )(ag, ag, u1, du3, b_dw, ln_g, ln_b)


def _glu_conv_bwd(du1, ag, w_dw, tm=256):
    s, c = du1.shape
    tm = min(tm, s)
    nrow = s // tm

    def body(d_ref, n_ref, ag_ref, w_ref, o_ref, db_ref, ext_ref, du0_ref):
        @pl.when(pl.program_id(0) == 0)
        def _():
            db_ref[...] = jnp.zeros_like(db_ref)

        ext_ref[0, 0:tm, :] = d_ref[...].astype(F32)
        ext_ref[0, tm:, :] = n_ref[...].astype(F32) * jnp.where(pl.program_id(0) == nrow - 1, 0.0, 1.0)
        _shifted_planes(ext_ref)
        top = CONV_KERNEL - 1
        _conv_taps(ext_ref, w_ref, [top - j for j in range(CONV_KERNEL)], tm, du0_ref)
        du0 = du0_ref[...]
        ag = ag_ref[...].astype(F32)
        a, gt = ag[:, :c], ag[:, c:]
        sg = _sigmoid(gt)
        da = du0 * sg
        dgt = du0 * a * (sg * (1.0 - sg))
        o_ref[:, :c] = da.astype(BF16)
        o_ref[:, c:] = dgt.astype(BF16)
        db_ref[:, :c] += jnp.sum(da, axis=0, keepdims=True)
        db_ref[:, c:] += jnp.sum(dgt, axis=0, keepdims=True)

    hb = tm // CONV_HALO
    return pl.pallas_call(
        body, name="glu_conv_bwd",
        out_shape=[jax.ShapeDtypeStruct((s, 2 * c), BF16), jax.ShapeDtypeStruct((1, 2 * c), F32)], grid=(nrow,),
        in_specs=[_rows(tm, c), pl.BlockSpec((CONV_HALO, c), lambda i: (jnp.minimum((i + 1) * hb, s // CONV_HALO - 1), 0)),
                  _rows(tm, 2 * c), _full((CONV_KERNEL, c))],
        out_specs=[_rows(tm, 2 * c), _full((1, 2 * c))],
        scratch_shapes=[pltpu.VMEM((8, tm + CONV_HALO, c), F32), pltpu.VMEM((tm, c), F32)],
        compiler_params=_params("arbitrary"),
    )(du1, du1, ag, w_dw)


def _head_rows(v, mask):
    return jnp.max(jnp.where(mask, v, -jnp.inf), axis=-1, keepdims=True)


def _attn_bwd(qv, dmix, mixed, lse, rope, grp, dil):
    l = qv.shape[0]
    s = l * dil
    nb = l // SPAN
    view = lambda t: t.reshape(l, dil * t.shape[1])
    scale = HEAD_DIM ** -0.5
    gw = GROUP_WIDTH

    def body(q_ref, kp_ref, kc_ref, vp_ref, vc_ref, do_ref, mx_ref, l_ref, c_ref, su_ref, sd_ref, cp_ref, sup_ref, sdp_ref,
             dq_ref, dkv_ref, carry_ref):
        b = pl.program_id(1)
        prev_tabs = (cp_ref, sup_ref, sdp_ref)

        @pl.when(b < nb)
        def _():
            row = lax.broadcasted_iota(jnp.int32, (SPAN, 2 * SPAN), 0)
            col = lax.broadcasted_iota(jnp.int32, (SPAN, 2 * SPAN), 1)
            no_prev = jnp.where(b > 0, 0, 4 * SPAN)
            valid = ((col < SPAN) & (col >= row + no_prev)) | ((col >= SPAN) & (col - SPAN <= row))
            masks, keep = _head_masks()
            masks2, _ = _head_masks(2 * SPAN)
            for p in range(gw // 128):
                sl = slice(p * 128, (p + 1) * 128)
                sl_v = slice(gw + p * 128, gw + (p + 1) * 128)
                qp, dop = q_ref[:, sl], do_ref[:, sl]
                kk = jnp.concatenate([kp_ref[:, sl], kc_ref[:, sl]], axis=0)
                vv = jnp.concatenate([vp_ref[:, sl], vc_ref[:, sl]], axis=0)
                prod = dop.astype(F32) * mx_ref[:, sl].astype(F32)
                lsep = l_ref[:, sl]
                dqs, dks, dvs = [], [], []
                for h in range(2):
                    qh, doh = qp * keep[h], dop * keep[h]
                    sc = lax.dot_general(qh, kk, NT, preferred_element_type=F32) * scale
                    pe = jnp.where(valid, jnp.exp(sc - _head_rows(lsep, masks[h])), 0.0)
                    dp = lax.dot_general(doh, vv, NT, preferred_element_type=F32)
                    dbar = jnp.sum(jnp.where(masks[h], prod, 0.0), axis=-1, keepdims=True)
                    ds = (pe * (dp - dbar) * scale).astype(BF16)
                    dqs.append(jnp.dot(ds, kk, preferred_element_type=F32))
                    dks.append(lax.dot_general(ds, qp, TN, preferred_element_type=F32))
                    dvs.append(lax.dot_general(pe.astype(BF16), dop, TN, preferred_element_type=F32))
                dq = jnp.where(masks[0], dqs[0], dqs[1])
                dq_ref[:, sl] = _rope_transpose(dq, c_ref[...], su_ref[...], sd_ref[...]).astype(BF16)
                dk = jnp.where(masks2[0], dks[0], dks[1])
                dv = jnp.where(masks2[0], dvs[0], dvs[1])

                @pl.when(b > 0)
                def _():
                    dk_prev = carry_ref[:, sl] + dk[:SPAN]
                    dkv_ref[:, sl] = _rope_transpose(dk_prev, *[t[...] for t in prev_tabs]).astype(BF16)
                    dkv_ref[:, sl_v] = (carry_ref[:, sl_v] + dv[:SPAN]).astype(BF16)

                carry_ref[:, sl] = dk[SPAN:]
                carry_ref[:, sl_v] = dv[SPAN:]

        @pl.when(b == nb)
        def _():
            for p in range(gw // 128):
                sl = slice(p * 128, (p + 1) * 128)
                sl_v = slice(gw + p * 128, gw + (p + 1) * 128)
                dkv_ref[:, sl] = _rope_transpose(carry_ref[:, sl], *[t[...] for t in prev_tabs]).astype(BF16)
                dkv_ref[:, sl_v] = carry_ref[:, sl_v].astype(BF16)

    blk = (SPAN, gw)
    cb = lambda b: jnp.minimum(b, nb - 1)
    cur = lambda t: pl.BlockSpec(blk, lambda r, b: (cb(b), r * 3 + t))
    prev = lambda t: pl.BlockSpec(blk, lambda r, b: (jnp.maximum(cb(b) - 1, 0), r * 3 + t))
    own = pl.BlockSpec(blk, lambda r, b: (cb(b), r))
    tab = pl.BlockSpec((SPAN, 128), lambda r, b: (cb(b), r))
    tab_prev = pl.BlockSpec((SPAN, 128), lambda r, b: (jnp.maximum(b - 1, 0), r))
    tabs = [view(t) for t in rope]
    dq, dkv = pl.pallas_call(
        body, name=f"attn_bwd_g{grp}",
        out_shape=[jax.ShapeDtypeStruct((l, dil * gw), BF16), jax.ShapeDtypeStruct((l, dil * 2 * gw), BF16)],
        grid=(dil, nb + 1),
        in_specs=[cur(0), prev(1), cur(1), prev(2), cur(2), own, own, own, tab, tab, tab, tab_prev, tab_prev, tab_prev],
        out_specs=[own, pl.BlockSpec((SPAN, 2 * gw), lambda r, b: (jnp.maximum(b - 1, 0), r))],
        scratch_shapes=[pltpu.VMEM((SPAN, 2 * gw), F32)], compiler_params=_params("parallel", "arbitrary"),
    )(qv, qv, qv, qv, qv, view(dmix), view(mixed), view(lse), *tabs, *tabs)
    return dq.reshape(s, gw), dkv.reshape(s, 2 * gw)


def _rope_freq_row():
    half = ROT_DIM // 2
    inv = (ROPE_THETA ** (-np.arange(half, dtype=np.float32) / half)).astype(np.float32)
    row = np.zeros((1, 128), np.float32)
    for head in range(128 // HEAD_DIM):
        row[0, head * HEAD_DIM:head * HEAD_DIM + half] = inv
        row[0, head * HEAD_DIM + half:head * HEAD_DIM + ROT_DIM] = inv
    return jnp.asarray(row)


def _ffn_fwd(x, g_pre, g_post, w_up_t, w_dw, b_dw, w_down):
    h, z = _norm_matmul(x, g_pre, w_up_t, tn=_tile(w_up_t.shape[0]), name="ffn_up")
    act = _ffn_act(z, w_dw, b_dw)
    y, xo = _matmul_resnorm(act, w_down, x, g_post, name="ffn_down")
    return xo, (x, h, z, act, y)


def _ffn_bwd(saved, dxo, g_pre, g_post, w_up_t, w_dw, b_dw, w_down):
    x, h, z, act, y = saved
    f = act.shape[1]
    d = x.shape[1]
    dy, dg_post = _postnorm_bwd(y, g_post, dxo, name="ffn_post_bwd")
    dact = _matmul(dy, w_down, name="ffn_dact", out_dtype=BF16, transposed_w=True)
    d_down = _weight_grad(act, dy, name="ffn_dw_down", out_shape=(1, f, d))
    dug_u, dug_g, db_u, db_g, dwd_u, dwd_g = _ffn_act_bwd(z, dact, w_dw, b_dw)
    dz_u = _conv3_transpose(dug_u, w_dw, 0)
    dz_g = _conv3_transpose(dug_g, w_dw, f)
    dx, dg_pre = _matmul_prenorm_bwd([(dz_u, 0, f, 0), (dz_g, 0, f, f)], w_up_t, x, g_pre, dxo, name="ffn_dx")
    d_up_t = _weight_grad(dz_u, h, name="ffn_dw_up", out_shape=(1, 2 * f, d))
    d_up_t = _weight_grad(dz_g, h, name="ffn_dw_up", out=d_up_t, row0=f)
    grads = dict(w_dw=jnp.concatenate([dwd_u, dwd_g], axis=1), b_dw=jnp.concatenate([db_u, db_g], axis=1),
                 g_pre=dg_pre, g_post=dg_post)
    return dx, grads, d_up_t, d_down


def _local_step(x, pos_col, target, p, tie=None, late_weights=None, exchange=None):
    ng = p["norm_g"]
    row = lambda r: ng[r:r + 1]
    freq = _rope_freq_row()
    rope = _rope_tables(pos_col, freq if tie is None else freq + tie[0:1])
    d = x.shape[1]

    h0, *qkv = _qkv_proj(x, row(0), p["w_qkv_t"], rope)
    os_, ls_ = zip(*[_attn_fwd(qkv[g_], g_, d_) for g_, d_ in enumerate(DILATIONS)])
    y_a, x1, mixed, lse = _mix_wo(os_, ls_, p["w_o_t"], x, row(1))
    if late_weights is not None:
        p = {**p, **late_weights(x1)}
    x2, ffn0 = _ffn_fwd(x1, row(2), row(3), p["w_up_t"][0], p["ffn_w_dw"][0], p["ffn_b_dw"][0], p["w_down"][0])
    h1, ag = _norm_matmul(x2, row(4), p["w_pw1_t"], tn=_tile(p["w_pw1_t"].shape[0]), name="conv_pw1", bias=p["b_pw1"])
    u3, u1 = _conv_mid(ag, p["conv_w_dw"], p["conv_b_dw"], p["ln_g"], p["ln_b"])
    y_c, x3 = _matmul_resnorm(u3, p["w_pw2"], x2, row(5), name="conv_pw2", bias=p["b_pw2"])
    x4, ffn1 = _ffn_fwd(x3, row(6), row(7), p["w_up_t"][1], p["ffn_w_dw"][1], p["ffn_b_dw"][1], p["w_down"][1])
    dx4, loss = _loss_grad(x4, target)

    big = [BF16, BF16]

    def tied(r, *tokens):
        tokens = [t for t in tokens if t is not None]
        return row(r) if not tokens else row(r) + jnp.tile(sum(tokens)[0:1], (1, d // 128))

    dx3, gf1, d_up1, d_down1 = _ffn_bwd(ffn1, dx4, row(6), row(7), p["w_up_t"][1], p["ffn_w_dw"][1], p["ffn_b_dw"][1],
                                        p["w_down"][1])
    t0 = exchange.submit("ffn1", [d_up1, d_down1], big) if exchange else None
    dy_c, dg5, db_pw2 = _postnorm_bwd(y_c, tied(5, t0), dx3, name="conv_post_bwd", with_bias_grad=True)
    du3 = _matmul(dy_c, p["w_pw2"], name="conv_du3", out_dtype=F32, transposed_w=True)
    d_wpw2 = _weight_grad(u3, dy_c, name="conv_dw_pw2", out_shape=(1, u3.shape[1], d))
    du1, d_lng, d_lnb, d_cbdw, d_cwdw = _conv_mid_bwd(ag, u1, du3, p["conv_b_dw"], p["ln_g"], p["ln_b"])
    dag, db_pw1 = _glu_conv_bwd(du1, ag, p["conv_w_dw"])
    dx2, dg4 = _matmul_prenorm_bwd([(dag, 0, dag.shape[1], 0)], p["w_pw1_t"], x2, row(4), dx3, name="conv_dx")
    d_wpw1_t = _weight_grad(dag, h1, name="conv_dw_pw1", out_shape=(1, dag.shape[1], d))
    t0 = exchange.advance(dx2) if exchange else None
    t1 = exchange.submit("conv", [d_wpw1_t, d_wpw2], big) if exchange else None
    dx1, gf0, d_up0, d_down0 = _ffn_bwd(ffn0, dx2, row(2), tied(3, t0, t1), p["w_up_t"][0], p["ffn_w_dw"][0], p["ffn_b_dw"][0],
                                        p["w_down"][0])
    t0 = exchange.advance(dx1) if exchange else None
    t1 = exchange.submit("ffn0", [d_up0, d_down0], big) if exchange else None
    dy_a, dg1 = _postnorm_bwd(y_a, tied(1, t0, t1), dx1, name="attn_post_bwd")
    dmix = _matmul(dy_a, p["w_o_t"], name="attn_dmix", out_dtype=BF16, transposed_w=False)
    d_wo_t = _weight_grad(dy_a, mixed, name="attn_dw_o", out_shape=(1, d, GROUP_WIDTH))
    pieces, d_wqkv_t = [], None
    for g_, d_ in enumerate(DILATIONS):
        if exchange and g_ > 0:
            tok = exchange.advance(dkv)
            if tok is not None:
                rope = (rope[0] + tok[0:1], rope[1], rope[2])
        dq, dkv = _attn_bwd(qkv[g_], dmix, mixed, lse, rope, g_, d_)
        for t, (arr, c0) in enumerate(((dq, 0), (dkv, 0), (dkv, GROUP_WIDTH))):
            r0 = (3 * t + g_) * GROUP_WIDTH
            pieces.append((arr, c0, GROUP_WIDTH, r0))
            d_wqkv_t = _weight_grad(arr, h0, name="attn_dw_qkv", a_col0=c0, ka=GROUP_WIDTH, out=d_wqkv_t,
                                    out_shape=(1, p["w_qkv_t"].shape[0], d), row0=r0)
    t0 = exchange.advance(dkv) if exchange else None
    t1 = exchange.submit("attn", [d_wqkv_t, d_wo_t], big) if exchange else None
    grad_x, dg0 = _matmul_prenorm_bwd(pieces, p["w_qkv_t"], x, tied(0, t0, t1), dx1, name="attn_dx")

    grads = dict(
        norm_g=jnp.concatenate([dg0, dg1, gf0["g_pre"], gf0["g_post"], dg4, dg5, gf1["g_pre"], gf1["g_post"]], axis=0),
        w_qkv_t=d_wqkv_t, w_o_t=d_wo_t, w_pw1_t=d_wpw1_t, b_pw1=db_pw1,
        conv_w_dw=d_cwdw[:CONV_KERNEL], conv_b_dw=d_cbdw, ln_g=d_lng, ln_b=d_lnb, w_pw2=d_wpw2, b_pw2=db_pw2,
        w_up_t=[d_up0, d_up1], ffn_w_dw=jnp.stack([gf0["w_dw"], gf1["w_dw"]]),
        ffn_b_dw=jnp.concatenate([gf0["b_dw"], gf1["b_dw"]], axis=0), w_down=[d_down0, d_down1])
    return loss, grad_x, grads


SMALL_AXIS = dict(norm_g=2, conv_b_pw1=1, conv_w_dw=2, conv_b_dw=1, conv_ln_g=1, conv_ln_b=1, conv_b_pw2=1, ffn_w_dw=2)
SMALL = tuple(SMALL_AXIS)
MATMUL_WEIGHTS = dict(attn_w_qkv=True, conv_w_pw1=True, ffn_w_up=True, conv_w_pw2=False, ffn_w_down=False)


def _pack(arrays, cols, row_multiple):
    flat = jnp.concatenate([a.reshape(-1) for a in arrays])
    rows = -(-flat.shape[0] // cols)
    rows = -(-rows // row_multiple) * row_multiple
    return jnp.pad(flat, (0, rows * cols - flat.shape[0])).reshape(rows, cols)


def _unpack(packed, shapes):
    flat = packed.reshape(packed.shape[:-2] + (-1,))
    out, off = [], 0
    for shp in shapes:
        n = math.prod(shp)
        out.append(flat[..., off:off + n].reshape(packed.shape[:-2] + tuple(shp)))
        off += n
    return out


def _join_shards(stacked, axis):
    moved = jnp.moveaxis(stacked, 0, axis)
    shp = moved.shape
    return moved.reshape(shp[:axis] + (shp[axis] * shp[axis + 1],) + shp[axis + 2:])


def _split_shards(whole, axis):
    shp = whole.shape
    cut = whole.reshape(shp[:axis] + (N_DEV, shp[axis] // N_DEV) + shp[axis + 1:])
    return jnp.moveaxis(cut, axis, 0)


def _row_shard(w, transposed):
    t = jnp.swapaxes(w, 1, 2) if transposed else w
    return t.astype(BF16).reshape(-1, t.shape[-1])


def kernel(x, positions, norm_g, attn_w_qkv, attn_w_o, conv_w_pw1, conv_b_pw1, conv_w_dw, conv_b_dw, conv_ln_g, conv_ln_b, conv_w_pw2, conv_b_pw2, ffn_w_up, ffn_w_dw, ffn_b_dw, ffn_w_down, loss_target, m_norm_g, m_attn_w_qkv, m_attn_w_o, m_conv_w_pw1, m_conv_b_pw1, m_conv_w_dw, m_conv_b_dw, m_conv_ln_g, m_conv_ln_b, m_conv_w_pw2, m_conv_b_pw2, m_ffn_w_up, m_ffn_w_dw, m_ffn_b_dw, m_ffn_w_down, v_norm_g, v_attn_w_qkv, v_attn_w_o, v_conv_w_pw1, v_conv_b_pw1, v_conv_w_dw, v_conv_b_dw, v_conv_ln_g, v_conv_ln_b, v_conv_w_pw2, v_conv_b_pw2, v_ffn_w_up, v_ffn_w_dw, v_ffn_b_dw, v_ffn_w_down):
    w = dict(norm_g=norm_g, attn_w_qkv=attn_w_qkv, attn_w_o=attn_w_o, conv_w_pw1=conv_w_pw1, conv_b_pw1=conv_b_pw1,
             conv_w_dw=conv_w_dw, conv_b_dw=conv_b_dw, conv_ln_g=conv_ln_g, conv_ln_b=conv_ln_b, conv_w_pw2=conv_w_pw2,
             conv_b_pw2=conv_b_pw2, ffn_w_up=ffn_w_up, ffn_w_dw=ffn_w_dw, ffn_w_down=ffn_w_down)
    m = dict(norm_g=m_norm_g, attn_w_qkv=m_attn_w_qkv, attn_w_o=m_attn_w_o, conv_w_pw1=m_conv_w_pw1, conv_b_pw1=m_conv_b_pw1,
             conv_w_dw=m_conv_w_dw, conv_b_dw=m_conv_b_dw, conv_ln_g=m_conv_ln_g, conv_ln_b=m_conv_ln_b, conv_w_pw2=m_conv_w_pw2,
             conv_b_pw2=m_conv_b_pw2, ffn_w_up=m_ffn_w_up, ffn_w_dw=m_ffn_w_dw, ffn_w_down=m_ffn_w_down)
    v = dict(norm_g=v_norm_g, attn_w_qkv=v_attn_w_qkv, attn_w_o=v_attn_w_o, conv_w_pw1=v_conv_w_pw1, conv_b_pw1=v_conv_b_pw1,
             conv_w_dw=v_conv_w_dw, conv_b_dw=v_conv_b_dw, conv_ln_g=v_conv_ln_g, conv_ln_b=v_conv_ln_b, conv_w_pw2=v_conv_w_pw2,
             conv_b_pw2=v_conv_b_pw2, ffn_w_up=v_ffn_w_up, ffn_w_dw=v_ffn_w_dw, ffn_w_down=v_ffn_w_down)
    d = x.shape[-1]

    w_qkv_t = _all_gather(_row_shard(attn_w_qkv, True), "gather_w_qkv").reshape(-1, d)
    w_o_t = _all_gather(_row_shard(attn_w_o, True), "gather_w_o").reshape(d, -1)
    small = _all_gather(_pack([w[n] for n in SMALL], 128, 8), "gather_small_weights")
    sm = {n: _join_shards(stacked, SMALL_AXIS[n])
          for n, stacked in zip(SMALL, _unpack(small, [w[n].shape for n in SMALL]))}
    late = {n: t for n, t in MATMUL_WEIGHTS.items() if n != "attn_w_qkv"}
    shares = [_row_shard(w[n], t) for n, t in late.items()]
    rows = [s_.shape[0] for s_ in shares]
    late_share = jnp.concatenate(shares, axis=0)
    send_sems, recv_sems, share_thru, land_thru, tie = _gather_start(late_share)
    me = 4 * lax.axis_index("x") + 2 * lax.axis_index("y") + lax.axis_index("c")

    def late_weights(after):
        big = _gather_wait(send_sems, recv_sems, share_thru, land_thru, after)
        big = lax.dynamic_update_slice(big, late_share[None], (me, 0, 0))
        whole, r0 = {}, 0
        for n, nr in zip(late, rows):
            layers = w[n].shape[0]
            seg = big[:, r0:r0 + nr].reshape(N_DEV, layers, nr // layers, d)
            whole[n] = [seg[:, l_].reshape(-1, d) for l_ in range(layers)]
            r0 += nr
        return dict(w_pw1_t=whole["conv_w_pw1"][0], w_pw2=whole["conv_w_pw2"][0], w_up_t=whole["ffn_w_up"],
                    w_down=whole["ffn_w_down"])

    p = dict(norm_g=sm["norm_g"].reshape(-1, d), w_qkv_t=w_qkv_t, w_o_t=w_o_t, b_pw1=sm["conv_b_pw1"],
             conv_w_dw=sm["conv_w_dw"][0], conv_b_dw=sm["conv_b_dw"], ln_g=sm["conv_ln_g"], ln_b=sm["conv_ln_b"],
             b_pw2=sm["conv_b_pw2"], ffn_w_dw=sm["ffn_w_dw"], ffn_b_dw=[ffn_b_dw[0:1], ffn_b_dw[1:2]])

    exchange = _GradExchange()
    loss, grad_x, g = _local_step(x[0], positions.reshape(-1, 1), loss_target[0], p, tie, late_weights, exchange)
    loss = lax.psum(loss[0, 0], ("x", "y", "c"))
    gsmall = dict(norm_g=g["norm_g"].reshape(norm_g.shape[0], 4, -1), conv_b_pw1=g["b_pw1"], conv_w_dw=g["conv_w_dw"][None],
                  conv_b_dw=g["conv_b_dw"], conv_ln_g=g["ln_g"], conv_ln_b=g["ln_b"], conv_b_pw2=g["b_pw2"], ffn_w_dw=g["ffn_w_dw"])
    small_contrib = jnp.concatenate([_split_shards(gsmall[n], SMALL_AXIS[n]).reshape(N_DEV, -1) for n in SMALL], axis=1)
    srows = small.shape[1]
    small_contrib = jnp.pad(small_contrib, ((0, 0), (0, srows * 128 - small_contrib.shape[1]))).reshape(1, N_DEV, srows, 128)
    exchange.advance(grad_x)
    small_sums = _rs_chips([_rs_pair_add(small_contrib, _rs_sibling([small_contrib])[0], exchange.core, F32)])[0]

    outs = {}

    def update(n, reduced):
        gsum = jnp.swapaxes(reduced, 1, 2) if n == "attn_w_o" or MATMUL_WEIGHTS.get(n) else reduced
        outs[n] = (gsum, *_adamw(gsum, w[n], m[n], v[n], "adamw"))

    (s_up1, s_down1), (s_pw1, s_pw2), (s_up0, s_down0) = exchange.results()[:3]
    update("conv_w_pw1", s_pw1)
    update("conv_w_pw2", s_pw2)
    update("ffn_w_up", jnp.concatenate([s_up0, s_up1], axis=0))
    update("ffn_w_down", jnp.concatenate([s_down0, s_down1], axis=0))
    sshapes = [w[n].shape for n in SMALL]
    souts = _sum_adamw(small_sums[0], *[_pack([t[n] for n in SMALL], 128, 8) for t in (w, m, v)], name="sum_adamw_small")
    for n, vals in zip(SMALL, zip(*[_unpack(o, sshapes) for o in souts])):
        outs[n] = vals
    bparts = _all_gather(_pack([g["ffn_b_dw"]], 128, 8), "gather_bias_grads")
    bouts = _sum_adamw(bparts, *[_pack([t], 128, 8) for t in (ffn_b_dw, m_ffn_b_dw, v_ffn_b_dw)], name="sum_adamw_bias")
    outs["ffn_b_dw"] = tuple(_unpack(o, [ffn_b_dw.shape])[0] for o in bouts)
    done = [outs[n][1][0, :8, :128] for n in ("conv_w_pw1", "conv_w_pw2", "ffn_w_up", "ffn_w_down")]
    exchange.advance(sum(done) + bouts[1][:8] + souts[1][:8])
    s_qkv, s_wo = exchange.results()[3]
    update("attn_w_qkv", s_qkv)
    update("attn_w_o", s_wo)

    order = ("norm_g", "attn_w_qkv", "attn_w_o", "conv_w_pw1", "conv_b_pw1", "conv_w_dw", "conv_b_dw", "conv_ln_g",
             "conv_ln_b", "conv_w_pw2", "conv_b_pw2", "ffn_w_up", "ffn_w_dw", "ffn_b_dw", "ffn_w_down")
    return (loss, grad_x[None], *[outs[n][0] for n in order], *[outs[n][1] for n in order],
            *[outs[n][2] for n in order], *[outs[n][3] for n in order])
```

```python
import math

import numpy as np
import jax
import jax.numpy as jnp
from jax import lax
from jax.experimental import pallas as pl
from jax.experimental.pallas import tpu as pltpu

F32 = jnp.float32
BF16 = jnp.bfloat16
EPS = 1e-6
N_DEV = 8
HEAD_DIM = 64
GROUP_WIDTH = 512
DILATIONS = (1, 4, 16)
SPAN = 128
ROT_DIM = 16
ROPE_THETA = 500000.0
CONV_KERNEL = 31
CONV_HALO = 32
FFN_CONV = 3
ADAM_LR, ADAM_B1, ADAM_B2, ADAM_EPS, ADAM_WD, ADAM_STEP = 0.001, 0.9, 0.999, 1e-08, 0.01, 10
VMEM_LIMIT_BYTES = 56 * 1024 * 1024
MESH = pl.DeviceIdType.MESH
ANY = pl.BlockSpec(memory_space=pl.ANY)
NT = (((1,), (1,)), ((), ()))
TN = (((0,), (0,)), ((), ()))


def _params(*sem):
    return pltpu.CompilerParams(dimension_semantics=sem, vmem_limit_bytes=VMEM_LIMIT_BYTES)


def _sigmoid(v):
    return 1.0 / (1.0 + jnp.exp(-v))


def _full(shape):
    return pl.BlockSpec(shape, lambda *_: (0,) * len(shape))


def _rows(tm, width):
    return pl.BlockSpec((tm, width), lambda i, *_: (i, 0))


def _tile(n, *multiples_of):
    for t in (1408, 1024, 512, 384, 256, 128):
        if n % t == 0 and all(o % t == 0 for o in multiples_of):
            return t
    raise ValueError((n, multiples_of))


def _all_gather(shard, name):
    r, c_ = shard.shape

    def body(x_ref, out_ref, send_sems, recv_sems, local_sem):
        x, y, c = lax.axis_index("x"), lax.axis_index("y"), lax.axis_index("c")
        me, sibling = (x, y, c), (x, y, 1 - c)
        chips = [(1 - x, y), (x, 1 - y), (1 - x, 1 - y)]

        def rows(px, py, pc):
            return out_ref.at[4 * px + 2 * py + pc]

        def copy(k, block, to, src=None):
            return pltpu.make_async_remote_copy(
                src_ref=rows(*block) if src is None else src, dst_ref=rows(*block),
                send_sem=send_sems.at[k], recv_sem=recv_sems.at[k], device_id=to, device_id_type=MESH)

        mine = pltpu.make_async_copy(x_ref, rows(*me), local_sem)
        mine.start()
        first = [copy(0, me, sibling, src=x_ref)]
        first += [copy(1 + j, me, (*chip, c), src=x_ref) for j, chip in enumerate(chips)]
        for cp in first:
            cp.start()
        passed = [copy(4 + j, (*chip, c), sibling) for j, chip in enumerate(chips)]
        for j, chip in enumerate(chips):
            copy(1 + j, (*chip, c), me).wait_recv()
            passed[j].start()
        copy(0, sibling, me).wait_recv()
        for j, chip in enumerate(chips):
            copy(4 + j, (*chip, 1 - c), me).wait_recv()
        for cp in first + passed:
            cp.wait_send()
        mine.wait()

    return pl.pallas_call(
        body, name=name, out_shape=jax.ShapeDtypeStruct((N_DEV, r, c_), shard.dtype),
        in_specs=[ANY], out_specs=ANY,
        scratch_shapes=[pltpu.SemaphoreType.DMA((7,)), pltpu.SemaphoreType.DMA((7,)), pltpu.SemaphoreType.DMA],
    )(shard)


HBM = pl.BlockSpec(memory_space=pltpu.HBM)
SEM = pl.BlockSpec(memory_space=pltpu.SEMAPHORE)
SIDE_EFFECT = pltpu.CompilerParams(has_side_effects=pltpu.SideEffectType.DATAFLOW_SIDE_EFFECTING)


def _gather_start(shard):
    r, c_ = shard.shape

    def body(x_ref, land_ref, send_sems, recv_sems, x_thru, land_thru, token):
        x, y, c = lax.axis_index("x"), lax.axis_index("y"), lax.axis_index("c")
        me = 4 * x + 2 * y + c
        for k in range(1, N_DEV):
            peer = (1 - x if k & 4 else x, 1 - y if k & 2 else y, 1 - c if k & 1 else c)
            pltpu.make_async_remote_copy(src_ref=x_ref, dst_ref=land_ref.at[me], send_sem=send_sems.at[k - 1],
                                         recv_sem=recv_sems.at[k - 1], device_id=peer, device_id_type=MESH).start()
        token[...] = jnp.zeros_like(token)

    land = pltpu.with_memory_space_constraint(lax.empty((N_DEV, r, c_), shard.dtype), pltpu.HBM)
    return pl.pallas_call(
        body, name="gather_late_weights_start",
        out_shape=(pltpu.SemaphoreType.DMA((N_DEV - 1,)), pltpu.SemaphoreType.DMA((N_DEV - 1,)),
                   pltpu.HBM(shard.shape, shard.dtype), pltpu.HBM((N_DEV, r, c_), shard.dtype),
                   jax.ShapeDtypeStruct((8, 128), F32)),
        in_specs=(HBM, HBM), out_specs=(SEM, SEM, HBM, HBM, pl.BlockSpec(memory_space=pltpu.VMEM)),
        input_output_aliases={0: 2, 1: 3}, compiler_params=SIDE_EFFECT,
    )(pltpu.with_memory_space_constraint(shard, pltpu.HBM), land)


def _gather_wait(send_sems, recv_sems, shard_thru, land_thru, after):
    def body(x_ref, land_ref, send_sems, recv_sems, after_ref, x_dead, got_ref):
        x, y, c = lax.axis_index("x"), lax.axis_index("y"), lax.axis_index("c")
        for k in range(N_DEV - 1):
            copy = pltpu.make_async_remote_copy(src_ref=x_ref, dst_ref=land_ref.at[0], send_sem=send_sems.at[k],
                                                recv_sem=recv_sems.at[k], device_id=(x, y, c), device_id_type=MESH)
            copy.wait_send()
            copy.wait_recv()

    return pl.pallas_call(
        body, name="gather_late_weights_wait",
        out_shape=(pltpu.HBM(shard_thru.shape, shard_thru.dtype), pltpu.HBM(land_thru.shape, land_thru.dtype)),
        in_specs=(HBM, HBM, SEM, SEM, ANY), out_specs=(HBM, HBM), input_output_aliases={0: 0, 1: 1},
        compiler_params=SIDE_EFFECT,
    )(shard_thru, land_thru, send_sems, recv_sems, after)[1]


def _hbm(a):
    return pltpu.with_memory_space_constraint(a, pltpu.HBM)


def _exchange_start(name, arrays, lands, plan, ncopies):
    n = len(arrays)

    def body(*refs):
        send_sems, recv_sems, token = refs[2 * n], refs[2 * n + 1], refs[-1]
        x, y, c = lax.axis_index("x"), lax.axis_index("y"), lax.axis_index("c")
        for k, (src, dst, peer) in enumerate(plan(x, y, c, refs[:n], refs[n:2 * n])):
            pltpu.make_async_remote_copy(src_ref=src, dst_ref=dst, send_sem=send_sems.at[k], recv_sem=recv_sems.at[k],
                                         device_id=peer, device_id_type=MESH).start()
        token[...] = jnp.zeros_like(token)

    both = list(arrays) + list(lands)
    outs = pl.pallas_call(
        body, name=name,
        out_shape=(pltpu.SemaphoreType.DMA((ncopies,)), pltpu.SemaphoreType.DMA((ncopies,)),
                   *[pltpu.HBM(a.shape, a.dtype) for a in both], jax.ShapeDtypeStruct((8, 128), F32)),
        in_specs=(HBM,) * (2 * n), out_specs=(SEM, SEM) + (HBM,) * (2 * n) + (pl.BlockSpec(memory_space=pltpu.VMEM),),
        input_output_aliases={i: 2 + i for i in range(2 * n)}, compiler_params=SIDE_EFFECT,
    )(*[_hbm(a) for a in both])
    return outs[0], outs[1], list(outs[2:2 + n]), list(outs[2 + n:2 + 2 * n]), outs[-1]


def _exchange_wait(name, send_sems, recv_sems, arrays, lands, plan, after):
    n = len(arrays)

    def body(*refs):
        send_sems, recv_sems = refs[2 * n], refs[2 * n + 1]
        x, y, c = lax.axis_index("x"), lax.axis_index("y"), lax.axis_index("c")
        for k, (src, dst, peer) in enumerate(plan(x, y, c, refs[:n], refs[n:2 * n])):
            copy = pltpu.make_async_remote_copy(src_ref=src, dst_ref=dst, send_sem=send_sems.at[k], recv_sem=recv_sems.at[k],
                                                device_id=peer, device_id_type=MESH)
            copy.wait_send()
            copy.wait_recv()

    both = list(arrays) + list(lands)
    outs = pl.pallas_call(
        body, name=name, out_shape=tuple(pltpu.HBM(a.shape, a.dtype) for a in both),
        in_specs=(HBM,) * (2 * n) + (SEM, SEM, ANY), out_specs=(HBM,) * (2 * n),
        input_output_aliases={i: i for i in range(2 * n)}, compiler_params=SIDE_EFFECT,
    )(*both, send_sems, recv_sems, after)
    return list(outs[:n]), list(outs[n:])


def _sibling_plan(x, y, c, g_refs, land_refs):
    return [(g.at[:, 2 * q + (1 - c)], o.at[:, q], (x, y, 1 - c)) for g, o in zip(g_refs, land_refs) for q in range(4)]


def _chips_plan(x, y, c, p_refs, land_refs):
    chips = [(1 - x, y), (x, 1 - y), (1 - x, 1 - y)]
    return [(p_.at[:, 2 * qx + qy], o.at[:, 2 * x + y], (qx, qy, c)) for p_, o in zip(p_refs, land_refs) for qx, qy in chips]


class _GradExchange:
    def __init__(self):
        self.core = lax.axis_index("c").astype(jnp.int32).reshape(1)
        self.chip = 2 * lax.axis_index("x") + lax.axis_index("y")
        self.groups = []

    def submit(self, tag, arrays, dtypes):
        arrays = [a.reshape(a.shape[0], N_DEV, a.shape[1] // N_DEV, a.shape[2]) for a in arrays]
        lands = [lax.empty((a.shape[0], 4) + a.shape[2:], a.dtype) for a in arrays]
        send, recv, arrays, lands, token = _exchange_start(f"rs_pair_start_{tag}", arrays, lands, _sibling_plan, 4 * len(arrays))
        self.groups.append(dict(tag=tag, stage=1, sems=(send, recv), arrays=arrays, lands=lands, dtypes=dtypes))
        return token

    def advance(self, after):
        token = None
        for g in self.groups:
            if g["stage"] == 1:
                arrays, got = _exchange_wait(f"rs_pair_wait_{g['tag']}", *g["sems"], g["arrays"], g["lands"], _sibling_plan, after)
                parts = [_rs_pair_add(a, b, self.core, dt) for a, b, dt in zip(arrays, got, g["dtypes"])]
                lands = [lax.empty(p_.shape, p_.dtype) for p_ in parts]
                send, recv, parts, lands, tok = _exchange_start(f"rs_chip_start_{g['tag']}", parts, lands, _chips_plan, 3 * len(parts))
                g.update(stage=2, sems=(send, recv), arrays=parts, lands=lands)
                token = tok if token is None else token + tok
            elif g["stage"] == 2:
                parts, lands = _exchange_wait(f"rs_chip_wait_{g['tag']}", *g["sems"], g["arrays"], g["lands"], _chips_plan, after)
                sums = []
                for p_, land in zip(parts, lands):
                    l, _, r, c_ = p_.shape
                    own = lax.dynamic_slice(p_, (0, self.chip, 0, 0), (l, 1, r, c_))
                    sums.append(_sum_parts(lax.dynamic_update_slice(land, own, (0, self.chip, 0, 0)), "sum_chips"))
                g.update(stage=3, sums=sums)
        return token

    def results(self):
        return [g.get("sums") for g in self.groups]


def _with_rows(g, n):
    return jax.ShapeDtypeStruct((g.shape[0], n) + tuple(g.shape[2:]), g.dtype)


def _rs_sibling(gs):
    n = len(gs)

    def body(*refs):
        g_refs, o_refs, (send_sems, recv_sems) = refs[:n], refs[n:2 * n], refs[2 * n:]
        x, y, c = lax.axis_index("x"), lax.axis_index("y"), lax.axis_index("c")
        copies = [pltpu.make_async_remote_copy(
            src_ref=g_refs[w].at[:, 2 * q + (1 - c)], dst_ref=o_refs[w].at[:, q], send_sem=send_sems.at[4 * w + q],
            recv_sem=recv_sems.at[4 * w + q], device_id=(x, y, 1 - c), device_id_type=MESH)
            for w in range(n) for q in range(4)]
        for cp in copies:
            cp.start()
        for cp in copies:
            cp.wait_recv()
        for cp in copies:
            cp.wait_send()

    return pl.pallas_call(
        body, name="rs_sibling", out_shape=[_with_rows(g, 4) for g in gs],
        in_specs=[ANY] * n, out_specs=[ANY] * n,
        scratch_shapes=[pltpu.SemaphoreType.DMA((4 * n,)), pltpu.SemaphoreType.DMA((4 * n,))],
    )(*gs)


def _rs_pair_add(g, got, core, out_dtype):
    l, _, r, c_ = g.shape

    def body(core_ref, g_ref, got_ref, o_ref):
        o_ref[...] = (g_ref[...].astype(F32) + got_ref[...].astype(F32)).astype(out_dtype)

    blk = (None, None, r, c_)
    return pl.pallas_call(
        body, name="rs_pair_add", out_shape=jax.ShapeDtypeStruct((l, 4, r, c_), out_dtype),
        grid_spec=pltpu.PrefetchScalarGridSpec(
            num_scalar_prefetch=1, grid=(l, 4),
            in_specs=[pl.BlockSpec(blk, lambda i, q, core_ref: (i, 2 * q + core_ref[0], 0, 0)),
                      pl.BlockSpec(blk, lambda i, q, core_ref: (i, q, 0, 0))],
            out_specs=pl.BlockSpec(blk, lambda i, q, core_ref: (i, q, 0, 0))),
        compiler_params=_params("parallel", "parallel"),
    )(core, g, got)


def _rs_chips(parts):
    n = len(parts)

    def body(*refs):
        p_refs, o_refs, (send_sems, recv_sems, local_sems) = refs[:n], refs[n:2 * n], refs[2 * n:]
        x, y, c = lax.axis_index("x"), lax.axis_index("y"), lax.axis_index("c")
        my_chip = 2 * x + y
        chips = [(1 - x, y), (x, 1 - y), (1 - x, 1 - y)]
        local = [pltpu.make_async_copy(p_refs[w].at[:, my_chip], o_refs[w].at[:, my_chip], local_sems.at[w]) for w in range(n)]
        for cp in local:
            cp.start()
        copies = [pltpu.make_async_remote_copy(
            src_ref=p_refs[w].at[:, 2 * qx + qy], dst_ref=o_refs[w].at[:, my_chip], send_sem=send_sems.at[3 * w + k],
            recv_sem=recv_sems.at[3 * w + k], device_id=(qx, qy, c), device_id_type=MESH)
            for w in range(n) for k, (qx, qy) in enumerate(chips)]
        for cp in copies:
            cp.start()
        for cp in copies:
            cp.wait_recv()
        for cp in copies:
            cp.wait_send()
        for cp in local:
            cp.wait()

    return pl.pallas_call(
        body, name="rs_chips", out_shape=[jax.ShapeDtypeStruct(p.shape, p.dtype) for p in parts],
        in_specs=[ANY] * n, out_specs=[ANY] * n,
        scratch_shapes=[pltpu.SemaphoreType.DMA((3 * n,)), pltpu.SemaphoreType.DMA((3 * n,)), pltpu.SemaphoreType.DMA((n,))],
    )(*parts)


def _sum_parts(parts, name):
    l, n, r, c_ = parts.shape

    def body(p_ref, o_ref):
        g = p_ref[0].astype(F32)
        for s in range(1, n):
            g = g + p_ref[s].astype(F32)
        o_ref[...] = g

    return pl.pallas_call(
        body, name=name, out_shape=jax.ShapeDtypeStruct((l, r, c_), F32), grid=(l,),
        in_specs=[pl.BlockSpec((None, n, r, c_), lambda i: (i, 0, 0, 0))],
        out_specs=pl.BlockSpec((None, r, c_), lambda i: (i, 0, 0)), compiler_params=_params("parallel"),
    )(parts)


def _adamw_math(w, g, m, v):
    m = ADAM_B1 * m + (1.0 - ADAM_B1) * g
    v = ADAM_B2 * v + (1.0 - ADAM_B2) * (g * g)
    m_hat = m / (1.0 - ADAM_B1 ** ADAM_STEP)
    v_hat = v / (1.0 - ADAM_B2 ** ADAM_STEP)
    delta = -ADAM_LR * (m_hat / (jnp.sqrt(v_hat) + ADAM_EPS) + ADAM_WD * w)
    return delta, m, v


def _adamw(g, w, m, v, name):
    l, k, n = w.shape
    tk = 256 if k % 256 == 0 else k

    def body(g_ref, w_ref, m_ref, v_ref, d_ref, nm_ref, nv_ref):
        d_ref[...], nm_ref[...], nv_ref[...] = _adamw_math(w_ref[...], g_ref[...], m_ref[...], v_ref[...])

    spec = pl.BlockSpec((None, tk, n), lambda i, j: (i, j, 0))
    return pl.pallas_call(
        body, name=name, out_shape=[jax.ShapeDtypeStruct((l, k, n), F32)] * 3, grid=(l, k // tk),
        in_specs=[spec] * 4, out_specs=[spec] * 3, compiler_params=_params("parallel", "parallel"),
    )(g, w, m, v)


def _sum_adamw(parts, w, m, v, name):
    n, r, c_ = parts.shape

    def body(p_ref, w_ref, m_ref, v_ref, g_ref, d_ref, nm_ref, nv_ref):
        g = p_ref[0]
        for s in range(1, n):
            g = g + p_ref[s]
        g_ref[...] = g
        d_ref[...], nm_ref[...], nv_ref[...] = _adamw_math(w_ref[...], g, m_ref[...], v_ref[...])

    return pl.pallas_call(
        body, name=name, out_shape=[jax.ShapeDtypeStruct((r, c_), F32)] * 4, grid=(1,),
        in_specs=[_full((n, r, c_))] + [_full((r, c_))] * 3, out_specs=[_full((r, c_))] * 4,
        compiler_params=_params("arbitrary"),
    )(parts, w, m, v)


def _rope_tables(pos_col, freq_row):
    s = pos_col.shape[0]
    tm = min(1024, s)

    def body(p_ref, f_ref, c_ref, su_ref, sd_ref):
        ang = p_ref[...].astype(F32) * f_ref[...]
        lane = lax.broadcasted_iota(jnp.int32, ang.shape, 1) & (HEAD_DIM - 1)
        cs, sn = jnp.cos(ang), jnp.sin(ang)
        c_ref[...] = jnp.where(lane < ROT_DIM, cs, 1.0)
        su_ref[...] = jnp.where((lane >= ROT_DIM // 2) & (lane < ROT_DIM), sn, 0.0)
        sd_ref[...] = jnp.where(lane < ROT_DIM // 2, -sn, 0.0)

    return pl.pallas_call(
        body, name="rope_tables", out_shape=[jax.ShapeDtypeStruct((s, 128), F32)] * 3, grid=(s // tm,),
        in_specs=[pl.BlockSpec((tm, 1), lambda i: (i, 0)), _full((1, 128))],
        out_specs=[_rows(tm, 128)] * 3, compiler_params=_params("parallel"),
    )(pos_col, freq_row)


def _rope_apply(t, cos, sin_up, sin_dn):
    w = t.shape[1]
    return t * cos + pltpu.roll(t, 8, 1) * sin_up + pltpu.roll(t, w - 8, 1) * sin_dn


def _rope_transpose(dr, cos, sin_up, sin_dn):
    w = dr.shape[1]
    return dr * cos + pltpu.roll(dr * sin_up, w - 8, 1) + pltpu.roll(dr * sin_dn, 8, 1)


def _norm_matmul(x, g, wt, *, tn, name, bias=None, tm=1024):
    s, d = x.shape
    n = wt.shape[0]
    tm = min(tm, s)

    def body(*refs):
        x_ref, g_ref, w_ref = refs[:3]
        b_ref = refs[3] if bias is not None else None
        h_ref, o_ref = refs[-2:]

        @pl.when(pl.program_id(1) == 0)
        def _():
            xv = x_ref[...]
            r = lax.rsqrt(jnp.mean(xv * xv, axis=-1, keepdims=True) + EPS)
            h_ref[...] = (xv * r * g_ref[...]).astype(BF16)

        acc = lax.dot_general(h_ref[...], w_ref[...], NT, preferred_element_type=F32)
        if b_ref is not None:
            acc = acc + b_ref[...]
        o_ref[...] = acc.astype(BF16)

    in_specs = [_rows(tm, d), _full((1, d)), pl.BlockSpec((tn, d), lambda i, j: (j, 0))]
    args = [x, g, wt]
    if bias is not None:
        in_specs.append(pl.BlockSpec((1, tn), lambda i, j: (0, j)))
        args.append(bias)
    return pl.pallas_call(
        body, name=name,
        out_shape=[jax.ShapeDtypeStruct((s, d), BF16), jax.ShapeDtypeStruct((s, n), BF16)],
        grid=(s // tm, n // tn), in_specs=in_specs,
        out_specs=[_rows(tm, d), pl.BlockSpec((tm, tn), lambda i, j: (i, j))],
        compiler_params=_params("parallel", "arbitrary"),
    )(*args)


def _class_major(tm, dil):
    p = np.zeros((tm, tm), np.float32)
    per = tm // dil
    for r in range(dil):
        for j in range(per):
            p[r * per + j, j * dil + r] = 1.0
    return jnp.asarray(p, dtype=BF16)


def _qkv_proj(x, g, wt, rope, tm=512):
    s, d = x.shape
    n = wt.shape[0]
    gw3 = 3 * GROUP_WIDTH
    tm = min(tm, s)
    assert n == 3 * gw3

    def body(x_ref, g_ref, w_ref, c_ref, su_ref, sd_ref, p1_ref, p2_ref, h_ref, o0_ref, o1_ref, o2_ref):
        j = pl.program_id(1)

        @pl.when(j == 0)
        def _():
            xv = x_ref[...]
            r = lax.rsqrt(jnp.mean(xv * xv, axis=-1, keepdims=True) + EPS)
            h_ref[...] = (xv * r * g_ref[...]).astype(BF16)

        acc = lax.dot_general(h_ref[...], w_ref[...], NT, preferred_element_type=F32)

        def store(y):
            yb = y.astype(BF16)
            o0_ref[:, pl.ds(pl.multiple_of(j * GROUP_WIDTH, GROUP_WIDTH), GROUP_WIDTH)] = yb[:, :GROUP_WIDTH]
            for grp, o_ref, p_ref in ((1, o1_ref, p1_ref), (2, o2_ref, p2_ref)):
                dil = DILATIONS[grp]
                per = tm // dil
                yp = jnp.dot(p_ref[...], yb[:, grp * GROUP_WIDTH:(grp + 1) * GROUP_WIDTH],
                             preferred_element_type=F32).astype(BF16)
                for r in range(dil):
                    col = pl.multiple_of(r * gw3 + j * GROUP_WIDTH, GROUP_WIDTH)
                    o_ref[:, pl.ds(col, GROUP_WIDTH)] = yp[r * per:(r + 1) * per, :]

        @pl.when(j < 2)
        def _():
            reps = gw3 // 128
            store(_rope_apply(acc, jnp.tile(c_ref[...], (1, reps)), jnp.tile(su_ref[...], (1, reps)),
                              jnp.tile(sd_ref[...], (1, reps))))

        @pl.when(j == 2)
        def _():
            store(acc)

    outs = [jax.ShapeDtypeStruct((s, d), BF16)] + [jax.ShapeDtypeStruct((s // dl, dl * gw3), BF16) for dl in DILATIONS]
    out_specs = [_rows(tm, d)] + [_rows(tm // dl, dl * gw3) for dl in DILATIONS]
    return pl.pallas_call(
        body, name="attn_qkv", out_shape=outs, grid=(s // tm, 3),
        in_specs=[_rows(tm, d), _full((1, d)), pl.BlockSpec((gw3, d), lambda i, j: (j, 0))] + [_rows(tm, 128)] * 3
        + [_full((tm, tm))] * 2,
        out_specs=out_specs, compiler_params=_params("parallel", "arbitrary"),
    )(x, g, wt, *rope, _class_major(tm, DILATIONS[1]), _class_major(tm, DILATIONS[2]))


def _head_masks(rows=SPAN):
    lane = lax.broadcasted_iota(jnp.int32, (rows, 128), 1)
    masks = [lane < HEAD_DIM, lane >= HEAD_DIM]
    lane1 = lax.broadcasted_iota(jnp.int32, (1, 128), 1)
    keep = [jnp.where(lane1 < HEAD_DIM, 1.0, 0.0).astype(BF16), jnp.where(lane1 >= HEAD_DIM, 1.0, 0.0).astype(BF16)]
    return masks, keep


def _band_mask(b):
    row = lax.broadcasted_iota(jnp.int32, (2 * SPAN, 2 * SPAN), 0) & (SPAN - 1)
    col = lax.broadcasted_iota(jnp.int32, (2 * SPAN, 2 * SPAN), 1)
    no_prev = jnp.where(b > 0, 0, 4 * SPAN)
    return ((col < SPAN) & (col >= row + no_prev)) | ((col >= SPAN) & (col - SPAN <= row))


def _attn_fwd(qv, grp, dil):
    l = qv.shape[0]
    s = l * dil
    nb = l // SPAN

    def body(q_ref, kp_ref, kc_ref, vp_ref, vc_ref, o_ref, l_ref):
        b = pl.program_id(1)
        valid = _band_mask(b)
        masks, keep = _head_masks()
        for p in range(GROUP_WIDTH // 128):
            sl = slice(p * 128, (p + 1) * 128)
            qp = q_ref[:, sl]
            kk = jnp.concatenate([kp_ref[:, sl], kc_ref[:, sl]], axis=0)
            vv = jnp.concatenate([vp_ref[:, sl], vc_ref[:, sl]], axis=0)
            q2 = jnp.concatenate([qp * keep[0], qp * keep[1]], axis=0)
            sc = lax.dot_general(q2, kk, NT, preferred_element_type=F32) * (HEAD_DIM ** -0.5)
            sc = jnp.where(valid, sc, -1e30)
            mx = jnp.max(sc, axis=-1, keepdims=True)
            pe = jnp.exp(sc - mx)
            den = jnp.sum(pe, axis=-1, keepdims=True)
            out = jnp.dot(pe.astype(BF16), vv, preferred_element_type=F32) / den
            lse = jnp.broadcast_to(mx + jnp.log(den), (2 * SPAN, 128))
            o_ref[:, sl] = jnp.where(masks[0], out[:SPAN], out[SPAN:])
            l_ref[:, sl] = jnp.where(masks[0], lse[:SPAN], lse[SPAN:])

    blk = (SPAN, GROUP_WIDTH)
    cur = lambda t: pl.BlockSpec(blk, lambda r, b: (b, r * 3 + t))
    prev = lambda t: pl.BlockSpec(blk, lambda r, b: (jnp.maximum(b - 1, 0), r * 3 + t))
    out = pl.BlockSpec(blk, lambda r, b: (b, r))
    o, lse = pl.pallas_call(
        body, name=f"attn_fwd_g{grp}", out_shape=[jax.ShapeDtypeStruct((l, dil * GROUP_WIDTH), F32)] * 2,
        grid=(dil, nb), in_specs=[cur(0), prev(1), cur(1), prev(2), cur(2)], out_specs=[out, out],
        compiler_params=_params("parallel", "arbitrary"),
    )(qv, qv, qv, qv, qv)
    return o.reshape(s, GROUP_WIDTH), lse.reshape(s, GROUP_WIDTH)


def _resnorm_store(y, x_ref, g_ref, y_ref, xo_ref):
    r = lax.rsqrt(jnp.mean(y * y, axis=-1, keepdims=True) + EPS)
    y_ref[...] = y
    xo_ref[...] = x_ref[...] + y * r * g_ref[...]


def _mix_wo(os_, ls_, wot, x, g, tm=512):
    s, d = x.shape
    gw = wot.shape[1]
    tm = min(tm, s)

    def body(o0, o1, o2, l0, l1, l2, w_ref, x_ref, g_ref, y_ref, xo_ref, mixed_ref, lse_ref):
        a0, a1, a2 = l0[...], l1[...], l2[...]
        mx = jnp.maximum(jnp.maximum(a0, a1), a2)
        e0, e1, e2 = jnp.exp(a0 - mx), jnp.exp(a1 - mx), jnp.exp(a2 - mx)
        den = e0 + e1 + e2
        mixed = (e0 / den) * o0[...] + (e1 / den) * o1[...] + (e2 / den) * o2[...]
        mixed_ref[...] = mixed.astype(BF16)
        lse_ref[...] = mx + jnp.log(den)
        y = lax.dot_general(mixed.astype(BF16), w_ref[...], NT, preferred_element_type=F32)
        _resnorm_store(y, x_ref, g_ref, y_ref, xo_ref)

    return pl.pallas_call(
        body, name="mix_wo",
        out_shape=[jax.ShapeDtypeStruct((s, d), F32), jax.ShapeDtypeStruct((s, d), F32),
                   jax.ShapeDtypeStruct((s, gw), BF16), jax.ShapeDtypeStruct((s, gw), F32)],
        grid=(s // tm,), in_specs=[_rows(tm, gw)] * 6 + [_full((d, gw)), _rows(tm, d), _full((1, d))],
        out_specs=[_rows(tm, d), _rows(tm, d), _rows(tm, gw), _rows(tm, gw)],
        compiler_params=_params("parallel"),
    )(*os_, *ls_, wot, x, g)


def _matmul_resnorm(a, w, x, g, *, name, bias=None, tm=512):
    s, k = a.shape
    d = w.shape[1]
    tm = min(tm, s)

    def body(*refs):
        a_ref, w_ref = refs[:2]
        b_ref = refs[2] if bias is not None else None
        x_ref, g_ref, y_ref, xo_ref = refs[-4:]
        y = jnp.dot(a_ref[...], w_ref[...], preferred_element_type=F32)
        if b_ref is not None:
            y = y + b_ref[...]
        _resnorm_store(y, x_ref, g_ref, y_ref, xo_ref)

    in_specs = [_rows(tm, k), _full((k, d))] + ([_full((1, d))] if bias is not None else []) + [_rows(tm, d), _full((1, d))]
    args = [a, w] + ([bias] if bias is not None else []) + [x, g]
    return pl.pallas_call(
        body, name=name, out_shape=[jax.ShapeDtypeStruct((s, d), F32)] * 2, grid=(s // tm,),
        in_specs=in_specs, out_specs=[_rows(tm, d)] * 2, compiler_params=_params("parallel"),
    )(*args)


FFN_SUB = 256


def _conv3_rows(z_ref, halo_ref, rb, sub, cs, first):
    zc = z_ref[rb * sub:(rb + 1) * sub, cs].astype(F32)
    if rb == 0:
        halo = halo_ref[:, cs].astype(F32) * jnp.where(first, 0.0, 1.0)
    else:
        halo = z_ref[rb * sub - 16:rb * sub, cs].astype(F32)[8:]
    z2, z1 = _conv3_taps(zc, halo)
    return z2, z1, zc


def _conv3_taps(z, halo):
    row = lax.broadcasted_iota(jnp.int32, (8, z.shape[1]), 0)
    h6, h7 = halo[6:7, :], halo[7:8, :]
    r1, r2 = pltpu.roll(z, 1, 0), pltpu.roll(z, 2, 0)
    z1 = jnp.concatenate([jnp.where(row == 0, h7, r1[0:8]), r1[8:]], axis=0)
    z2 = jnp.concatenate([jnp.where(row == 0, h6, jnp.where(row == 1, h7, r2[0:8])), r2[8:]], axis=0)
    return z2, z1


def _ffn_cols(f):
    return _tile(f)


def _lane_chunks(width, fn):
    def step(k, carry):
        fn(pl.ds(pl.multiple_of(k * 128, 128), 128))
        return carry

    lax.fori_loop(0, width // 128, step, 0)


def _ffn_act(z, w_dw, b_dw, tm=1024):
    s, f2 = z.shape
    f = f2 // 2
    tm = min(tm, s)
    sub = min(FFN_SUB, tm)
    tc = _ffn_cols(f)
    nfc = f // tc

    def body(zu, zg, hu, hg, wu, wg, bu, bg, o_ref):
        first = pl.program_id(0) == 0

        def chunk(cs):
            for rb in range(tm // sub):
                def conv(z_ref, h_ref, w_ref, b_ref):
                    z2, z1, zc = _conv3_rows(z_ref, h_ref, rb, sub, cs, first)
                    return w_ref[0:1, cs] * z2 + w_ref[1:2, cs] * z1 + w_ref[2:3, cs] * zc + b_ref[:, cs]

                up, gate = conv(zu, hu, wu, bu), conv(zg, hg, wg, bg)
                o_ref[rb * sub:(rb + 1) * sub, cs] = (gate * _sigmoid(gate) * up).astype(BF16)

        _lane_chunks(tc, chunk)

    hb = tm // 8
    tile = lambda off: pl.BlockSpec((tm, tc), lambda i, j: (i, off + j))
    halo = lambda off: pl.BlockSpec((8, tc), lambda i, j: (jnp.maximum(i * hb - 1, 0), off + j))
    prm = lambda rows, off: pl.BlockSpec((rows, tc), lambda i, j: (0, off + j))
    return pl.pallas_call(
        body, name="ffn_act", out_shape=jax.ShapeDtypeStruct((s, f), BF16), grid=(s // tm, nfc),
        in_specs=[tile(0), tile(nfc), halo(0), halo(nfc), prm(FFN_CONV, 0), prm(FFN_CONV, nfc), prm(1, 0), prm(1, nfc)],
        out_specs=pl.BlockSpec((tm, tc), lambda i, j: (i, j)), compiler_params=_params("parallel", "parallel"),
    )(z, z, z, z, w_dw, w_dw, b_dw, b_dw)


def _shifted_planes(ext_ref):
    rows = ext_ref.shape[1]
    for s in range(1, 8):
        ext_ref[s, 0:rows - 8, :] = ext_ref[0, s:s + rows - 8, :]


def _window(ext_ref, off, tm, cs):
    s = off % 8
    return ext_ref[s, off - s:off - s + tm, cs]


def _conv_taps(ext_ref, w_ref, offs, tm, out_ref):
    def chunk(cs):
        acc = w_ref[0:1, cs] * _window(ext_ref, offs[0], tm, cs)
        for j in range(1, len(offs)):
            acc = acc + w_ref[j:j + 1, cs] * _window(ext_ref, offs[j], tm, cs)
        out_ref[:, cs] = acc

    _lane_chunks(out_ref.shape[1], chunk)


def _glu_planes(ag_ref, halo_ref, ext_ref, first, c):
    hal = halo_ref[...].astype(F32)
    ext_ref[0, 0:CONV_HALO, :] = hal[:, :c] * _sigmoid(hal[:, c:]) * jnp.where(first, 0.0, 1.0)
    ag = ag_ref[...].astype(F32)
    ext_ref[0, CONV_HALO:, :] = ag[:, :c] * _sigmoid(ag[:, c:])
    _shifted_planes(ext_ref)


def _layernorm_stats(u1):
    mu = jnp.mean(u1, axis=-1, keepdims=True)
    cen = u1 - mu
    rstd = lax.rsqrt(jnp.mean(cen * cen, axis=-1, keepdims=True) + EPS)
    return cen * rstd, rstd


def _conv_mid(ag, w_dw, b_dw, ln_g, ln_b, tm=256):
    s, c2 = ag.shape
    c = c2 // 2
    tm = min(tm, s)

    def body(ag_ref, halo_ref, w_ref, b_ref, g_ref, bb_ref, o_ref, u1_ref, ext_ref):
        _glu_planes(ag_ref, halo_ref, ext_ref, pl.program_id(0) == 0, c)
        base = CONV_HALO - (CONV_KERNEL - 1)
        _conv_taps(ext_ref, w_ref, [base + j for j in range(CONV_KERNEL)], tm, u1_ref)
        xh, _ = _layernorm_stats(u1_ref[...] + b_ref[...])
        u2 = xh * g_ref[...] + bb_ref[...]
        o_ref[...] = (u2 * _sigmoid(u2)).astype(BF16)

    hb = tm // CONV_HALO
    return pl.pallas_call(
        body, name="conv_mid", out_shape=[jax.ShapeDtypeStruct((s, c), BF16), jax.ShapeDtypeStruct((s, c), F32)], grid=(s // tm,),
        in_specs=[_rows(tm, c2), pl.BlockSpec((CONV_HALO, c2), lambda i: (jnp.maximum(i * hb - 1, 0), 0)),
                  _full((CONV_KERNEL, c)), _full((1, c)), _full((1, c)), _full((1, c))],
        out_specs=[_rows(tm, c), _rows(tm, c)], scratch_shapes=[pltpu.VMEM((8, CONV_HALO + tm, c), F32)],
        compiler_params=_params("arbitrary"),
    )(ag, ag, w_dw, b_dw, ln_g, ln_b)


def _loss_grad(xo, target, tm=1024):
    s, d = xo.shape
    tm = min(tm, s)

    def body(x_ref, t_ref, dx_ref, loss_ref):
        @pl.when(pl.program_id(0) == 0)
        def _():
            loss_ref[...] = jnp.zeros_like(loss_ref)

        err = x_ref[...] - t_ref[...]
        dx_ref[...] = err * (1.0 / d)
        loss_ref[...] += 0.5 * jnp.sum(jnp.mean(err * err, axis=-1, keepdims=True))

    return pl.pallas_call(
        body, name="loss_grad", out_shape=[jax.ShapeDtypeStruct((s, d), F32), jax.ShapeDtypeStruct((1, 128), F32)],
        grid=(s // tm,), in_specs=[_rows(tm, d)] * 2, out_specs=[_rows(tm, d), _full((1, 128))],
        compiler_params=_params("arbitrary"),
    )(xo, target)


def _postnorm_bwd(y, g, dxo, *, name, with_bias_grad=False, tm=1024):
    s, d = y.shape
    tm = min(tm, s)

    def body(y_ref, g_ref, dx_ref, dy_ref, dg_ref, *rest):
        @pl.when(pl.program_id(0) == 0)
        def _():
            dg_ref[...] = jnp.zeros_like(dg_ref)
            for r_ in rest:
                r_[...] = jnp.zeros_like(r_)

        yv, dxo_v = y_ref[...], dx_ref[...]
        r = lax.rsqrt(jnp.mean(yv * yv, axis=-1, keepdims=True) + EPS)
        yh = yv * r
        dyh = dxo_v * g_ref[...]
        dy = r * (dyh - yh * jnp.mean(dyh * yh, axis=-1, keepdims=True))
        dy_ref[...] = dy.astype(BF16)
        dg_ref[...] += jnp.sum(dxo_v * yh, axis=0, keepdims=True)
        for r_ in rest:
            r_[...] += jnp.sum(dy, axis=0, keepdims=True)

    nacc = 2 if with_bias_grad else 1
    return pl.pallas_call(
        body, name=name, out_shape=[jax.ShapeDtypeStruct((s, d), BF16)] + [jax.ShapeDtypeStruct((1, d), F32)] * nacc,
        grid=(s // tm,), in_specs=[_rows(tm, d), _full((1, d)), _rows(tm, d)],
        out_specs=[_rows(tm, d)] + [_full((1, d))] * nacc, compiler_params=_params("arbitrary"),
    )(y, g, dxo)


def _matmul(gmat, w, *, name, out_dtype, transposed_w, tm=512):
    s, k = gmat.shape
    n = w.shape[0] if transposed_w else w.shape[1]
    tm = min(tm, s)

    def body(g_ref, w_ref, o_ref):
        if transposed_w:
            acc = lax.dot_general(g_ref[...], w_ref[...], NT, preferred_element_type=F32)
        else:
            acc = jnp.dot(g_ref[...], w_ref[...], preferred_element_type=F32)
        o_ref[...] = acc.astype(out_dtype)

    return pl.pallas_call(
        body, name=name, out_shape=jax.ShapeDtypeStruct((s, n), out_dtype), grid=(s // tm,),
        in_specs=[_rows(tm, k), _full(w.shape)], out_specs=_rows(tm, n), compiler_params=_params("parallel"),
    )(gmat, w)


def _matmul_prenorm_bwd(pieces, wt, x, g, dres, *, name, tm=256):
    s, d = x.shape
    tm = min(tm, s)
    np_ = len(pieces)

    def body(*refs):
        p_refs, w_refs = refs[:np_], refs[np_:2 * np_]
        x_ref, g_ref, r_ref, dx_ref, dg_ref = refs[2 * np_:]

        @pl.when(pl.program_id(0) == 0)
        def _():
            dg_ref[...] = jnp.zeros_like(dg_ref)

        dh = None
        for p_ref, w_ref in zip(p_refs, w_refs):
            t = jnp.dot(p_ref[...], w_ref[...], preferred_element_type=F32)
            dh = t if dh is None else dh + t
        xv = x_ref[...]
        r = lax.rsqrt(jnp.mean(xv * xv, axis=-1, keepdims=True) + EPS)
        xh = xv * r
        dyh = dh * g_ref[...]
        dx_ref[...] = r_ref[...] + r * (dyh - xh * jnp.mean(dyh * xh, axis=-1, keepdims=True))
        dg_ref[...] += jnp.sum(dh * xh, axis=0, keepdims=True)

    in_specs = []
    for _, c0, kc, _ in pieces:
        assert c0 % kc == 0
        in_specs.append(pl.BlockSpec((tm, kc), lambda i, _b=c0 // kc: (i, _b)))
    for _, _, kc, r0 in pieces:
        assert r0 % kc == 0
        in_specs.append(pl.BlockSpec((kc, d), lambda i, _b=r0 // kc: (_b, 0)))
    in_specs += [_rows(tm, d), _full((1, d)), _rows(tm, d)]
    return pl.pallas_call(
        body, name=name, out_shape=[jax.ShapeDtypeStruct((s, d), F32), jax.ShapeDtypeStruct((1, d), F32)],
        grid=(s // tm,), in_specs=in_specs, out_specs=[_rows(tm, d), _full((1, d))],
        compiler_params=_params("arbitrary"),
    )(*[p[0] for p in pieces], *[wt] * np_, x, g, dres)


def _weight_grad(a, gmat, *, name, a_col0=0, ka=None, out=None, out_shape=None, layer=0, row0=0, ts=1024):
    s = a.shape[0]
    ka = a.shape[1] if ka is None else ka
    n = gmat.shape[1]
    ts = min(ts, s)
    tka = _tile(ka, a_col0, row0)
    shape = out.shape if out is not None else out_shape
    nsteps = s // ts

    def body(a_ref, g_ref, *rest):
        o_ref, acc_ref = rest[-2:]
        i = pl.program_id(1)

        @pl.when(i == 0)
        def _():
            acc_ref[...] = jnp.zeros_like(acc_ref)

        acc_ref[...] += lax.dot_general(a_ref[...], g_ref[...], TN, preferred_element_type=F32)

        @pl.when(i == nsteps - 1)
        def _():
            o_ref[...] = acc_ref[...].astype(BF16)

    in_specs = [pl.BlockSpec((ts, tka), lambda k, i: (i, a_col0 // tka + k)), pl.BlockSpec((ts, n), lambda k, i: (i, 0))]
    args = [a, gmat]
    aliases = {}
    if out is not None:
        in_specs.append(ANY)
        args.append(out)
        aliases = {2: 0}
    return pl.pallas_call(
        body, name=name, out_shape=jax.ShapeDtypeStruct(shape, BF16), grid=(ka // tka, nsteps), in_specs=in_specs,
        out_specs=pl.BlockSpec((None, tka, n), lambda k, i: (layer, row0 // tka + k, 0)),
        scratch_shapes=[pltpu.VMEM((tka, n), F32)],
        input_output_aliases=aliases, compiler_params=_params("parallel", "arbitrary"),
    )(*args)


def _ffn_act_bwd(z, dact, w_dw, b_dw, tm=512):
    s, f2 = z.shape
    f = f2 // 2
    tm = min(tm, s)
    sub = min(FFN_SUB // 2, tm)
    tc = _ffn_cols(f)
    nfc = f // tc

    def body(zu, zg, hu, hg, wu, wg, bu, bg, da_ref, du_ref, dgt_ref, dbu_ref, dbg_ref, dwu_ref, dwg_ref):
        i = pl.program_id(1)

        @pl.when(i == 0)
        def _():
            for r_ in (dbu_ref, dbg_ref, dwu_ref, dwg_ref):
                r_[...] = jnp.zeros_like(r_)

        def chunk(cs):
            for rb in range(tm // sub):
                rows = slice(rb * sub, (rb + 1) * sub)

                def conv(z_ref, h_ref, w_ref, b_ref):
                    taps = _conv3_rows(z_ref, h_ref, rb, sub, cs, i == 0)
                    return taps, w_ref[0:1, cs] * taps[0] + w_ref[1:2, cs] * taps[1] + w_ref[2:3, cs] * taps[2] + b_ref[:, cs]

                taps_u, up = conv(zu, hu, wu, bu)
                taps_g, gate = conv(zg, hg, wg, bg)
                da = da_ref[rows, cs].astype(F32)
                sg = _sigmoid(gate)
                d_up = da * (gate * sg)
                d_gate = da * up * (sg * (1.0 + gate * (1.0 - sg)))
                du_ref[rows, cs] = d_up.astype(BF16)
                dgt_ref[rows, cs] = d_gate.astype(BF16)
                for dv, taps, db_ref, dw_ref in ((d_up, taps_u, dbu_ref, dwu_ref), (d_gate, taps_g, dbg_ref, dwg_ref)):
                    db_ref[:, cs] += jnp.sum(dv, axis=0, keepdims=True)
                    for k_, tap in enumerate(taps):
                        dw_ref[k_:k_ + 1, cs] += jnp.sum(dv * tap, axis=0, keepdims=True)

        _lane_chunks(tc, chunk)

    hb = tm // 8
    tile = lambda off: pl.BlockSpec((tm, tc), lambda j, i: (i, off + j))
    halo = lambda off: pl.BlockSpec((8, tc), lambda j, i: (jnp.maximum(i * hb - 1, 0), off + j))
    prm = lambda rows, off: pl.BlockSpec((rows, tc), lambda j, i: (0, off + j))
    acc = lambda rows: pl.BlockSpec((rows, tc), lambda j, i: (0, j))
    return pl.pallas_call(
        body, name="ffn_act_bwd",
        out_shape=[jax.ShapeDtypeStruct((s, f), BF16)] * 2 + [jax.ShapeDtypeStruct((1, f), F32)] * 2
        + [jax.ShapeDtypeStruct((FFN_CONV, f), F32)] * 2,
        grid=(nfc, s // tm),
        in_specs=[tile(0), tile(nfc), halo(0), halo(nfc), prm(FFN_CONV, 0), prm(FFN_CONV, nfc), prm(1, 0), prm(1, nfc), tile(0)],
        out_specs=[tile(0), tile(0), acc(1), acc(1), acc(FFN_CONV), acc(FFN_CONV)],
        compiler_params=_params("parallel", "arbitrary"),
    )(z, z, z, z, w_dw, w_dw, b_dw, b_dw, dact)


def _conv3_transpose(dug, w_dw, col0, tm=1024):
    s, f = dug.shape
    tm = min(tm, s)
    sub = min(FFN_SUB, tm)
    nsub = tm // sub
    tc = _ffn_cols(f)
    nfc = f // tc
    nrow = s // tm
    off = col0 // tc

    def body(d_ref, n_ref, w_ref, o_ref):
        keep_next = jnp.where(pl.program_id(0) == nrow - 1, 0.0, 1.0)

        def chunk(cs):
            for rb in range(nsub):
                rows = slice(rb * sub, (rb + 1) * sub)
                dv = d_ref[rows, cs].astype(F32)
                if rb == nsub - 1:
                    nxt = n_ref[:, cs].astype(F32) * keep_next
                else:
                    nxt = d_ref[(rb + 1) * sub:(rb + 1) * sub + 16, cs].astype(F32)[:8]
                n0, n1 = nxt[0:1, :], nxt[1:2, :]
                row = lax.broadcasted_iota(jnp.int32, (8, dv.shape[1]), 0)
                r1, r2 = pltpu.roll(dv, sub - 1, 0), pltpu.roll(dv, sub - 2, 0)
                d1 = jnp.concatenate([r1[:sub - 8], jnp.where(row == 7, n0, r1[sub - 8:])], axis=0)
                d2 = jnp.concatenate([r2[:sub - 8], jnp.where(row == 7, n1, jnp.where(row == 6, n0, r2[sub - 8:]))], axis=0)
                o_ref[rows, cs] = (w_ref[2:3, cs] * dv + w_ref[1:2, cs] * d1 + w_ref[0:1, cs] * d2).astype(BF16)

        _lane_chunks(tc, chunk)

    hb = tm // 8
    return pl.pallas_call(
        body, name="conv3_transpose", out_shape=jax.ShapeDtypeStruct((s, f), BF16), grid=(nrow, nfc),
        in_specs=[pl.BlockSpec((tm, tc), lambda i, j: (i, j)),
                  pl.BlockSpec((8, tc), lambda i, j: (jnp.minimum((i + 1) * hb, s // 8 - 1), j)),
                  pl.BlockSpec((FFN_CONV, tc), lambda i, j: (0, off + j))],
        out_specs=pl.BlockSpec((tm, tc), lambda i, j: (i, j)), compiler_params=_params("parallel", "parallel"),
    )(dug, dug, w_dw)


def _conv_mid_bwd(ag, u1, du3, b_dw, ln_g, ln_b, tm=256):
    s, c2 = ag.shape
    c = c2 // 2
    tm = min(tm, s)

    def body(ag_ref, halo_ref, u1in_ref, du_ref, b_ref, g_ref, bb_ref, o_ref, dlg_ref, dlb_ref, db_ref, dw_ref, ext_ref, u1_ref):
        @pl.when(pl.program_id(0) == 0)
        def _():
            for r_ in (dlg_ref, dlb_ref, db_ref, dw_ref):
                r_[...] = jnp.zeros_like(r_)

        _glu_planes(ag_ref, halo_ref, ext_ref, pl.program_id(0) == 0, c)
        xh, rstd = _layernorm_stats(u1in_ref[...] + b_ref[...])
        u2 = xh * g_ref[...] + bb_ref[...]
        sg = _sigmoid(u2)
        du2 = du_ref[...] * (sg * (1.0 + u2 * (1.0 - sg)))
        dlg_ref[...] += jnp.sum(du2 * xh, axis=0, keepdims=True)
        dlb_ref[...] += jnp.sum(du2, axis=0, keepdims=True)
        dxh = du2 * g_ref[...]
        du1 = rstd * (dxh - jnp.mean(dxh, axis=-1, keepdims=True) - xh * jnp.mean(dxh * xh, axis=-1, keepdims=True))
        o_ref[...] = du1.astype(BF16)
        db_ref[...] += jnp.sum(du1, axis=0, keepdims=True)
        u1_ref[...] = du1
        base = CONV_HALO - (CONV_KERNEL - 1)

        def chunk(cs):
            dc = u1_ref[:, cs]
            for j in range(CONV_KERNEL):
                dw_ref[j:j + 1, cs] += jnp.sum(dc * _window(ext_ref, base + j, tm, cs), axis=0, keepdims=True)

        _lane_chunks(c, chunk)

    hb = tm // CONV_HALO
    vec = _full((1, c))
    return pl.pallas_call(
        body, name="conv_mid_bwd",
        out_shape=[jax.ShapeDtypeStruct((s, c), BF16)] + [jax.ShapeDtypeStruct((1, c), F32)] * 3
        + [jax.ShapeDtypeStruct((CONV_HALO, c), F32)],
        grid=(s // tm,),
        in_specs=[_rows(tm, c2), pl.BlockSpec((CONV_HALO, c2), lambda i: (jnp.maximum(i * hb - 1, 0), 0)), _rows(tm, c),
                  _rows(tm, c), vec, vec, vec],
        out_specs=[_rows(tm, c), vec, vec, vec, _full((CONV_HALO, c))],
        scratch_shapes=[pltpu.VMEM((8, CONV_HALO + tm, c), F32), pltpu.VMEM((tm, c), F32)],
        compiler_params=_params("arbitrary"),
    )(ag, ag, u1, du3, b_dw, ln_g, ln_b)


def _glu_conv_bwd(du1, ag, w_dw, tm=256):
    s, c = du1.shape
    tm = min(tm, s)
    nrow = s // tm

    def body(d_ref, n_ref, ag_ref, w_ref, o_ref, db_ref, ext_ref, du0_ref):
        @pl.when(pl.program_id(0) == 0)
        def _():
            db_ref[...] = jnp.zeros_like(db_ref)

        ext_ref[0, 0:tm, :] = d_ref[...].astype(F32)
        ext_ref[0, tm:, :] = n_ref[...].astype(F32) * jnp.where(pl.program_id(0) == nrow - 1, 0.0, 1.0)
        _shifted_planes(ext_ref)
        top = CONV_KERNEL - 1
        _conv_taps(ext_ref, w_ref, [top - j for j in range(CONV_KERNEL)], tm, du0_ref)
        du0 = du0_ref[...]
        ag = ag_ref[...].astype(F32)
        a, gt = ag[:, :c], ag[:, c:]
        sg = _sigmoid(gt)
        da = du0 * sg
        dgt = du0 * a * (sg * (1.0 - sg))
        o_ref[:, :c] = da.astype(BF16)
        o_ref[:, c:] = dgt.astype(BF16)
        db_ref[:, :c] += jnp.sum(da, axis=0, keepdims=True)
        db_ref[:, c:] += jnp.sum(dgt, axis=0, keepdims=True)

    hb = tm // CONV_HALO
    return pl.pallas_call(
        body, name="glu_conv_bwd",
        out_shape=[jax.ShapeDtypeStruct((s, 2 * c), BF16), jax.ShapeDtypeStruct((1, 2 * c), F32)], grid=(nrow,),
        in_specs=[_rows(tm, c), pl.BlockSpec((CONV_HALO, c), lambda i: (jnp.minimum((i + 1) * hb, s // CONV_HALO - 1), 0)),
                  _rows(tm, 2 * c), _full((CONV_KERNEL, c))],
        out_specs=[_rows(tm, 2 * c), _full((1, 2 * c))],
        scratch_shapes=[pltpu.VMEM((8, tm + CONV_HALO, c), F32), pltpu.VMEM((tm, c), F32)],
        compiler_params=_params("arbitrary"),
    )(du1, du1, ag, w_dw)


def _head_rows(v, mask):
    return jnp.max(jnp.where(mask, v, -jnp.inf), axis=-1, keepdims=True)


def _attn_bwd(qv, dmix, mixed, lse, rope, grp, dil):
    l = qv.shape[0]
    s = l * dil
    nb = l // SPAN
    view = lambda t: t.reshape(l, dil * t.shape[1])
    scale = HEAD_DIM ** -0.5
    gw = GROUP_WIDTH

    def body(q_ref, kp_ref, kc_ref, vp_ref, vc_ref, do_ref, mx_ref, l_ref, c_ref, su_ref, sd_ref, cp_ref, sup_ref, sdp_ref,
             dq_ref, dkv_ref, carry_ref):
        b = pl.program_id(1)
        prev_tabs = (cp_ref, sup_ref, sdp_ref)

        @pl.when(b < nb)
        def _():
            valid = _band_mask(b)
            masks, keep = _head_masks()
            for p in range(gw // 128):
                sl = slice(p * 128, (p + 1) * 128)
                sl_v = slice(gw + p * 128, gw + (p + 1) * 128)
                qp, dop = q_ref[:, sl], do_ref[:, sl]
                kk = jnp.concatenate([kp_ref[:, sl], kc_ref[:, sl]], axis=0)
                vv = jnp.concatenate([vp_ref[:, sl], vc_ref[:, sl]], axis=0)
                prod = dop.astype(F32) * mx_ref[:, sl].astype(F32)
                lsep = l_ref[:, sl]
                q2 = jnp.concatenate([qp * keep[0], qp * keep[1]], axis=0)
                do2 = jnp.concatenate([dop * keep[0], dop * keep[1]], axis=0)
                lse2 = jnp.concatenate([_head_rows(lsep, masks[h]) for h in range(2)], axis=0)
                dbar2 = jnp.concatenate([jnp.sum(jnp.where(masks[h], prod, 0.0), axis=-1, keepdims=True) for h in range(2)], axis=0)
                sc = lax.dot_general(q2, kk, NT, preferred_element_type=F32) * scale
                pe = jnp.where(valid, jnp.exp(sc - lse2), 0.0)
                dp = lax.dot_general(do2, vv, NT, preferred_element_type=F32)
                ds = (pe * (dp - dbar2) * scale).astype(BF16)
                dq2 = jnp.dot(ds, kk, preferred_element_type=F32)
                dq = jnp.where(masks[0], dq2[:SPAN], dq2[SPAN:])
                dq_ref[:, sl] = _rope_transpose(dq, c_ref[...], su_ref[...], sd_ref[...]).astype(BF16)
                dk = lax.dot_general(ds, q2, TN, preferred_element_type=F32)
                dv = lax.dot_general(pe.astype(BF16), do2, TN, preferred_element_type=F32)

                @pl.when(b > 0)
                def _():
                    dk_prev = carry_ref[:, sl] + dk[:SPAN]
                    dkv_ref[:, sl] = _rope_transpose(dk_prev, *[t[...] for t in prev_tabs]).astype(BF16)
                    dkv_ref[:, sl_v] = (carry_ref[:, sl_v] + dv[:SPAN]).astype(BF16)

                carry_ref[:, sl] = dk[SPAN:]
                carry_ref[:, sl_v] = dv[SPAN:]

        @pl.when(b == nb)
        def _():
            for p in range(gw // 128):
                sl = slice(p * 128, (p + 1) * 128)
                sl_v = slice(gw + p * 128, gw + (p + 1) * 128)
                dkv_ref[:, sl] = _rope_transpose(carry_ref[:, sl], *[t[...] for t in prev_tabs]).astype(BF16)
                dkv_ref[:, sl_v] = carry_ref[:, sl_v].astype(BF16)

    blk = (SPAN, gw)
    cb = lambda b: jnp.minimum(b, nb - 1)
    cur = lambda t: pl.BlockSpec(blk, lambda r, b: (cb(b), r * 3 + t))
    prev = lambda t: pl.BlockSpec(blk, lambda r, b: (jnp.maximum(cb(b) - 1, 0), r * 3 + t))
    own = pl.BlockSpec(blk, lambda r, b: (cb(b), r))
    tab = pl.BlockSpec((SPAN, 128), lambda r, b: (cb(b), r))
    tab_prev = pl.BlockSpec((SPAN, 128), lambda r, b: (jnp.maximum(b - 1, 0), r))
    tabs = [view(t) for t in rope]
    dq, dkv = pl.pallas_call(
        body, name=f"attn_bwd_g{grp}",
        out_shape=[jax.ShapeDtypeStruct((l, dil * gw), BF16), jax.ShapeDtypeStruct((l, dil * 2 * gw), BF16)],
        grid=(dil, nb + 1),
        in_specs=[cur(0), prev(1), cur(1), prev(2), cur(2), own, own, own, tab, tab, tab, tab_prev, tab_prev, tab_prev],
        out_specs=[own, pl.BlockSpec((SPAN, 2 * gw), lambda r, b: (jnp.maximum(b - 1, 0), r))],
        scratch_shapes=[pltpu.VMEM((SPAN, 2 * gw), F32)], compiler_params=_params("parallel", "arbitrary"),
    )(qv, qv, qv, qv, qv, view(dmix), view(mixed), view(lse), *tabs, *tabs)
    return dq.reshape(s, gw), dkv.reshape(s, 2 * gw)


def _rope_freq_row():
    half = ROT_DIM // 2
    inv = (ROPE_THETA ** (-np.arange(half, dtype=np.float32) / half)).astype(np.float32)
    row = np.zeros((1, 128), np.float32)
    for head in range(128 // HEAD_DIM):
        row[0, head * HEAD_DIM:head * HEAD_DIM + half] = inv
        row[0, head * HEAD_DIM + half:head * HEAD_DIM + ROT_DIM] = inv
    return jnp.asarray(row)


def _ffn_fwd(x, g_pre, g_post, w_up_t, w_dw, b_dw, w_down):
    h, z = _norm_matmul(x, g_pre, w_up_t, tn=_tile(w_up_t.shape[0]), name="ffn_up")
    act = _ffn_act(z, w_dw, b_dw)
    y, xo = _matmul_resnorm(act, w_down, x, g_post, name="ffn_down")
    return xo, (x, h, z, act, y)


def _ffn_bwd(saved, dxo, g_pre, g_post, w_up_t, w_dw, b_dw, w_down):
    x, h, z, act, y = saved
    f = act.shape[1]
    d = x.shape[1]
    dy, dg_post = _postnorm_bwd(y, g_post, dxo, name="ffn_post_bwd")
    dact = _matmul(dy, w_down, name="ffn_dact", out_dtype=BF16, transposed_w=True)
    d_down = _weight_grad(act, dy, name="ffn_dw_down", out_shape=(1, f, d))
    dug_u, dug_g, db_u, db_g, dwd_u, dwd_g = _ffn_act_bwd(z, dact, w_dw, b_dw)
    dz_u = _conv3_transpose(dug_u, w_dw, 0)
    dz_g = _conv3_transpose(dug_g, w_dw, f)
    dx, dg_pre = _matmul_prenorm_bwd([(dz_u, 0, f, 0), (dz_g, 0, f, f)], w_up_t, x, g_pre, dxo, name="ffn_dx")
    d_up_t = _weight_grad(dz_u, h, name="ffn_dw_up", out_shape=(1, 2 * f, d))
    d_up_t = _weight_grad(dz_g, h, name="ffn_dw_up", out=d_up_t, row0=f)
    grads = dict(w_dw=jnp.concatenate([dwd_u, dwd_g], axis=1), b_dw=jnp.concatenate([db_u, db_g], axis=1),
                 g_pre=dg_pre, g_post=dg_post)
    return dx, grads, d_up_t, d_down


def _local_step(x, pos_col, target, p, tie=None, late_weights=None, exchange=None):
    ng = p["norm_g"]
    row = lambda r: ng[r:r + 1]
    freq = _rope_freq_row()
    rope = _rope_tables(pos_col, freq if tie is None else freq + tie[0:1])
    d = x.shape[1]

    h0, *qkv = _qkv_proj(x, row(0), p["w_qkv_t"], rope)
    os_, ls_ = zip(*[_attn_fwd(qkv[g_], g_, d_) for g_, d_ in enumerate(DILATIONS)])
    y_a, x1, mixed, lse = _mix_wo(os_, ls_, p["w_o_t"], x, row(1))
    if late_weights is not None:
        p = {**p, **late_weights(x1)}
    x2, ffn0 = _ffn_fwd(x1, row(2), row(3), p["w_up_t"][0], p["ffn_w_dw"][0], p["ffn_b_dw"][0], p["w_down"][0])
    h1, ag = _norm_matmul(x2, row(4), p["w_pw1_t"], tn=_tile(p["w_pw1_t"].shape[0]), name="conv_pw1", bias=p["b_pw1"])
    u3, u1 = _conv_mid(ag, p["conv_w_dw"], p["conv_b_dw"], p["ln_g"], p["ln_b"])
    y_c, x3 = _matmul_resnorm(u3, p["w_pw2"], x2, row(5), name="conv_pw2", bias=p["b_pw2"])
    x4, ffn1 = _ffn_fwd(x3, row(6), row(7), p["w_up_t"][1], p["ffn_w_dw"][1], p["ffn_b_dw"][1], p["w_down"][1])
    dx4, loss = _loss_grad(x4, target)

    big = [BF16, BF16]

    def tied(r, *tokens):
        tokens = [t for t in tokens if t is not None]
        return row(r) if not tokens else row(r) + jnp.tile(sum(tokens)[0:1], (1, d // 128))

    dx3, gf1, d_up1, d_down1 = _ffn_bwd(ffn1, dx4, row(6), row(7), p["w_up_t"][1], p["ffn_w_dw"][1], p["ffn_b_dw"][1],
                                        p["w_down"][1])
    t0 = exchange.submit("ffn1", [d_up1, d_down1], big) if exchange else None
    dy_c, dg5, db_pw2 = _postnorm_bwd(y_c, tied(5, t0), dx3, name="conv_post_bwd", with_bias_grad=True)
    du3 = _matmul(dy_c, p["w_pw2"], name="conv_du3", out_dtype=F32, transposed_w=True)
    d_wpw2 = _weight_grad(u3, dy_c, name="conv_dw_pw2", out_shape=(1, u3.shape[1], d))
    du1, d_lng, d_lnb, d_cbdw, d_cwdw = _conv_mid_bwd(ag, u1, du3, p["conv_b_dw"], p["ln_g"], p["ln_b"])
    dag, db_pw1 = _glu_conv_bwd(du1, ag, p["conv_w_dw"])
    dx2, dg4 = _matmul_prenorm_bwd([(dag, 0, dag.shape[1], 0)], p["w_pw1_t"], x2, row(4), dx3, name="conv_dx")
    d_wpw1_t = _weight_grad(dag, h1, name="conv_dw_pw1", out_shape=(1, dag.shape[1], d))
    t0 = exchange.advance(dx2) if exchange else None
    t1 = exchange.submit("conv", [d_wpw1_t, d_wpw2], big) if exchange else None
    dx1, gf0, d_up0, d_down0 = _ffn_bwd(ffn0, dx2, row(2), tied(3, t0, t1), p["w_up_t"][0], p["ffn_w_dw"][0], p["ffn_b_dw"][0],
                                        p["w_down"][0])
    t0 = exchange.advance(dx1) if exchange else None
    t1 = exchange.submit("ffn0", [d_up0, d_down0], big) if exchange else None
    dy_a, dg1 = _postnorm_bwd(y_a, tied(1, t0, t1), dx1, name="attn_post_bwd")
    dmix = _matmul(dy_a, p["w_o_t"], name="attn_dmix", out_dtype=BF16, transposed_w=False)
    d_wo_t = _weight_grad(dy_a, mixed, name="attn_dw_o", out_shape=(1, d, GROUP_WIDTH))
    pieces, d_wqkv_t = [], None
    for g_, d_ in enumerate(DILATIONS):
        if exchange and g_ > 0:
            tok = exchange.advance(dkv)
            if tok is not None:
                rope = (rope[0] + tok[0:1], rope[1], rope[2])
        dq, dkv = _attn_bwd(qkv[g_], dmix, mixed, lse, rope, g_, d_)
        for t, (arr, c0) in enumerate(((dq, 0), (dkv, 0), (dkv, GROUP_WIDTH))):
            r0 = (3 * t + g_) * GROUP_WIDTH
            pieces.append((arr, c0, GROUP_WIDTH, r0))
            d_wqkv_t = _weight_grad(arr, h0, name="attn_dw_qkv", a_col0=c0, ka=GROUP_WIDTH, out=d_wqkv_t,
                                    out_shape=(1, p["w_qkv_t"].shape[0], d), row0=r0)
    t0 = exchange.advance(dkv) if exchange else None
    t1 = exchange.submit("attn", [d_wqkv_t, d_wo_t], big) if exchange else None
    grad_x, dg0 = _matmul_prenorm_bwd(pieces, p["w_qkv_t"], x, tied(0, t0, t1), dx1, name="attn_dx")

    grads = dict(
        norm_g=jnp.concatenate([dg0, dg1, gf0["g_pre"], gf0["g_post"], dg4, dg5, gf1["g_pre"], gf1["g_post"]], axis=0),
        w_qkv_t=d_wqkv_t, w_o_t=d_wo_t, w_pw1_t=d_wpw1_t, b_pw1=db_pw1,
        conv_w_dw=d_cwdw[:CONV_KERNEL], conv_b_dw=d_cbdw, ln_g=d_lng, ln_b=d_lnb, w_pw2=d_wpw2, b_pw2=db_pw2,
        w_up_t=[d_up0, d_up1], ffn_w_dw=jnp.stack([gf0["w_dw"], gf1["w_dw"]]),
        ffn_b_dw=jnp.concatenate([gf0["b_dw"], gf1["b_dw"]], axis=0), w_down=[d_down0, d_down1])
    return loss, grad_x, grads


SMALL_AXIS = dict(norm_g=2, conv_b_pw1=1, conv_w_dw=2, conv_b_dw=1, conv_ln_g=1, conv_ln_b=1, conv_b_pw2=1, ffn_w_dw=2)
SMALL = tuple(SMALL_AXIS)
MATMUL_WEIGHTS = dict(attn_w_qkv=True, conv_w_pw1=True, ffn_w_up=True, conv_w_pw2=False, ffn_w_down=False)


def _pack(arrays, cols, row_multiple):
    flat = jnp.concatenate([a.reshape(-1) for a in arrays])
    rows = -(-flat.shape[0] // cols)
    rows = -(-rows // row_multiple) * row_multiple
    return jnp.pad(flat, (0, rows * cols - flat.shape[0])).reshape(rows, cols)


def _unpack(packed, shapes):
    flat = packed.reshape(packed.shape[:-2] + (-1,))
    out, off = [], 0
    for shp in shapes:
        n = math.prod(shp)
        out.append(flat[..., off:off + n].reshape(packed.shape[:-2] + tuple(shp)))
        off += n
    return out


def _join_shards(stacked, axis):
    moved = jnp.moveaxis(stacked, 0, axis)
    shp = moved.shape
    return moved.reshape(shp[:axis] + (shp[axis] * shp[axis + 1],) + shp[axis + 2:])


def _split_shards(whole, axis):
    shp = whole.shape
    cut = whole.reshape(shp[:axis] + (N_DEV, shp[axis] // N_DEV) + shp[axis + 1:])
    return jnp.moveaxis(cut, axis, 0)


def _row_shard(w, transposed):
    t = jnp.swapaxes(w, 1, 2) if transposed else w
    return t.astype(BF16).reshape(-1, t.shape[-1])


def kernel(x, positions, norm_g, attn_w_qkv, attn_w_o, conv_w_pw1, conv_b_pw1, conv_w_dw, conv_b_dw, conv_ln_g, conv_ln_b, conv_w_pw2, conv_b_pw2, ffn_w_up, ffn_w_dw, ffn_b_dw, ffn_w_down, loss_target, m_norm_g, m_attn_w_qkv, m_attn_w_o, m_conv_w_pw1, m_conv_b_pw1, m_conv_w_dw, m_conv_b_dw, m_conv_ln_g, m_conv_ln_b, m_conv_w_pw2, m_conv_b_pw2, m_ffn_w_up, m_ffn_w_dw, m_ffn_b_dw, m_ffn_w_down, v_norm_g, v_attn_w_qkv, v_attn_w_o, v_conv_w_pw1, v_conv_b_pw1, v_conv_w_dw, v_conv_b_dw, v_conv_ln_g, v_conv_ln_b, v_conv_w_pw2, v_conv_b_pw2, v_ffn_w_up, v_ffn_w_dw, v_ffn_b_dw, v_ffn_w_down):
    w = dict(norm_g=norm_g, attn_w_qkv=attn_w_qkv, attn_w_o=attn_w_o, conv_w_pw1=conv_w_pw1, conv_b_pw1=conv_b_pw1,
             conv_w_dw=conv_w_dw, conv_b_dw=conv_b_dw, conv_ln_g=conv_ln_g, conv_ln_b=conv_ln_b, conv_w_pw2=conv_w_pw2,
             conv_b_pw2=conv_b_pw2, ffn_w_up=ffn_w_up, ffn_w_dw=ffn_w_dw, ffn_w_down=ffn_w_down)
    m = dict(norm_g=m_norm_g, attn_w_qkv=m_attn_w_qkv, attn_w_o=m_attn_w_o, conv_w_pw1=m_conv_w_pw1, conv_b_pw1=m_conv_b_pw1,
             conv_w_dw=m_conv_w_dw, conv_b_dw=m_conv_b_dw, conv_ln_g=m_conv_ln_g, conv_ln_b=m_conv_ln_b, conv_w_pw2=m_conv_w_pw2,
             conv_b_pw2=m_conv_b_pw2, ffn_w_up=m_ffn_w_up, ffn_w_dw=m_ffn_w_dw, ffn_w_down=m_ffn_w_down)
    v = dict(norm_g=v_norm_g, attn_w_qkv=v_attn_w_qkv, attn_w_o=v_attn_w_o, conv_w_pw1=v_conv_w_pw1, conv_b_pw1=v_conv_b_pw1,
             conv_w_dw=v_conv_w_dw, conv_b_dw=v_conv_b_dw, conv_ln_g=v_conv_ln_g, conv_ln_b=v_conv_ln_b, conv_w_pw2=v_conv_w_pw2,
             conv_b_pw2=v_conv_b_pw2, ffn_w_up=v_ffn_w_up, ffn_w_dw=v_ffn_w_dw, ffn_w_down=v_ffn_w_down)
    d = x.shape[-1]

    w_qkv_t = _all_gather(_row_shard(attn_w_qkv, True), "gather_w_qkv").reshape(-1, d)
    w_o_t = _all_gather(_row_shard(attn_w_o, True), "gather_w_o").reshape(d, -1)
    small = _all_gather(_pack([w[n] for n in SMALL], 128, 8), "gather_small_weights")
    sm = {n: _join_shards(stacked, SMALL_AXIS[n])
          for n, stacked in zip(SMALL, _unpack(small, [w[n].shape for n in SMALL]))}
    late = {n: t for n, t in MATMUL_WEIGHTS.items() if n != "attn_w_qkv"}
    shares = [_row_shard(w[n], t) for n, t in late.items()]
    rows = [s_.shape[0] for s_ in shares]
    late_share = jnp.concatenate(shares, axis=0)
    send_sems, recv_sems, share_thru, land_thru, tie = _gather_start(late_share)
    me = 4 * lax.axis_index("x") + 2 * lax.axis_index("y") + lax.axis_index("c")

    def late_weights(after):
        big = _gather_wait(send_sems, recv_sems, share_thru, land_thru, after)
        big = lax.dynamic_update_slice(big, late_share[None], (me, 0, 0))
        whole, r0 = {}, 0
        for n, nr in zip(late, rows):
            layers = w[n].shape[0]
            seg = big[:, r0:r0 + nr].reshape(N_DEV, layers, nr // layers, d)
            whole[n] = [seg[:, l_].reshape(-1, d) for l_ in range(layers)]
            r0 += nr
        return dict(w_pw1_t=whole["conv_w_pw1"][0], w_pw2=whole["conv_w_pw2"][0], w_up_t=whole["ffn_w_up"],
                    w_down=whole["ffn_w_down"])

    p = dict(norm_g=sm["norm_g"].reshape(-1, d), w_qkv_t=w_qkv_t, w_o_t=w_o_t, b_pw1=sm["conv_b_pw1"],
             conv_w_dw=sm["conv_w_dw"][0], conv_b_dw=sm["conv_b_dw"], ln_g=sm["conv_ln_g"], ln_b=sm["conv_ln_b"],
             b_pw2=sm["conv_b_pw2"], ffn_w_dw=sm["ffn_w_dw"], ffn_b_dw=[ffn_b_dw[0:1], ffn_b_dw[1:2]])

    exchange = _GradExchange()
    loss, grad_x, g = _local_step(x[0], positions.reshape(-1, 1), loss_target[0], p, tie, late_weights, exchange)
    loss = lax.psum(loss[0, 0], ("x", "y", "c"))
    gsmall = dict(norm_g=g["norm_g"].reshape(norm_g.shape[0], 4, -1), conv_b_pw1=g["b_pw1"], conv_w_dw=g["conv_w_dw"][None],
                  conv_b_dw=g["conv_b_dw"], conv_ln_g=g["ln_g"], conv_ln_b=g["ln_b"], conv_b_pw2=g["b_pw2"], ffn_w_dw=g["ffn_w_dw"])
    small_contrib = jnp.concatenate([_split_shards(gsmall[n], SMALL_AXIS[n]).reshape(N_DEV, -1) for n in SMALL], axis=1)
    srows = small.shape[1]
    small_contrib = jnp.pad(small_contrib, ((0, 0), (0, srows * 128 - small_contrib.shape[1]))).reshape(1, N_DEV, srows, 128)
    exchange.advance(grad_x)
    small_sums = _rs_chips([_rs_pair_add(small_contrib, _rs_sibling([small_contrib])[0], exchange.core, F32)])[0]

    outs = {}

    def update(n, reduced):
        gsum = jnp.swapaxes(reduced, 1, 2) if n == "attn_w_o" or MATMUL_WEIGHTS.get(n) else reduced
        outs[n] = (gsum, *_adamw(gsum, w[n], m[n], v[n], "adamw"))

    (s_up1, s_down1), (s_pw1, s_pw2), (s_up0, s_down0) = exchange.results()[:3]
    update("conv_w_pw1", s_pw1)
    update("conv_w_pw2", s_pw2)
    update("ffn_w_up", jnp.concatenate([s_up0, s_up1], axis=0))
    update("ffn_w_down", jnp.concatenate([s_down0, s_down1], axis=0))
    sshapes = [w[n].shape for n in SMALL]
    souts = _sum_adamw(small_sums[0], *[_pack([t[n] for n in SMALL], 128, 8) for t in (w, m, v)], name="sum_adamw_small")
    for n, vals in zip(SMALL, zip(*[_unpack(o, sshapes) for o in souts])):
        outs[n] = vals
    bparts = _all_gather(_pack([g["ffn_b_dw"]], 128, 8), "gather_bias_grads")
    bouts = _sum_adamw(bparts, *[_pack([t], 128, 8) for t in (ffn_b_dw, m_ffn_b_dw, v_ffn_b_dw)], name="sum_adamw_bias")
    outs["ffn_b_dw"] = tuple(_unpack(o, [ffn_b_dw.shape])[0] for o in bouts)
    done = [outs[n][1][0, :8, :128] for n in ("conv_w_pw1", "conv_w_pw2", "ffn_w_up", "ffn_w_down")]
    exchange.advance(sum(done) + bouts[1][:8] + souts[1][:8])
    s_qkv, s_wo = exchange.results()[3]
    update("attn_w_qkv", s_qkv)
    update("attn_w_o", s_wo)

    order = ("norm_g", "attn_w_qkv", "attn_w_o", "conv_w_pw1", "conv_b_pw1", "conv_w_dw", "conv_b_dw", "conv_ln_g",
             "conv_ln_b", "conv_w_pw2", "conv_b_pw2", "ffn_w_up", "ffn_w_dw", "ffn_b_dw", "ffn_w_down")
    return (loss, grad_x[None], *[outs[n][0] for n in order], *[outs[n][1] for n in order],
            *[outs[n][2] for n in order], *[outs[n][3] for n in order])
```

```python
import math

import numpy as np
import jax
import jax.numpy as jnp
from jax import lax
from jax.experimental import pallas as pl
from jax.experimental.pallas import tpu as pltpu

F32 = jnp.float32
BF16 = jnp.bfloat16
EPS = 1e-6
N_DEV = 8
HEAD_DIM = 64
GROUP_WIDTH = 512
DILATIONS = (1, 4, 16)
SPAN = 128
ROT_DIM = 16
ROPE_THETA = 500000.0
CONV_KERNEL = 31
CONV_HALO = 32
FFN_CONV = 3
ADAM_LR, ADAM_B1, ADAM_B2, ADAM_EPS, ADAM_WD, ADAM_STEP = 0.001, 0.9, 0.999, 1e-08, 0.01, 10
VMEM_LIMIT_BYTES = 56 * 1024 * 1024
MESH = pl.DeviceIdType.MESH
ANY = pl.BlockSpec(memory_space=pl.ANY)
NT = (((1,), (1,)), ((), ()))
TN = (((0,), (0,)), ((), ()))


def _params(*sem):
    return pltpu.CompilerParams(dimension_semantics=sem, vmem_limit_bytes=VMEM_LIMIT_BYTES)


def _sigmoid(v):
    return 1.0 / (1.0 + jnp.exp(-v))


def _full(shape):
    return pl.BlockSpec(shape, lambda *_: (0,) * len(shape))


def _rows(tm, width):
    return pl.BlockSpec((tm, width), lambda i, *_: (i, 0))


def _tile(n, *multiples_of):
    for t in (1408, 1024, 512, 384, 256, 128):
        if n % t == 0 and all(o % t == 0 for o in multiples_of):
            return t
    raise ValueError((n, multiples_of))


def _all_gather(shard, name):
    r, c_ = shard.shape

    def body(x_ref, out_ref, send_sems, recv_sems, local_sem):
        x, y, c = lax.axis_index("x"), lax.axis_index("y"), lax.axis_index("c")
        me, sibling = (x, y, c), (x, y, 1 - c)
        chips = [(1 - x, y), (x, 1 - y), (1 - x, 1 - y)]

        def rows(px, py, pc):
            return out_ref.at[4 * px + 2 * py + pc]

        def copy(k, block, to, src=None):
            return pltpu.make_async_remote_copy(
                src_ref=rows(*block) if src is None else src, dst_ref=rows(*block),
                send_sem=send_sems.at[k], recv_sem=recv_sems.at[k], device_id=to, device_id_type=MESH)

        mine = pltpu.make_async_copy(x_ref, rows(*me), local_sem)
        mine.start()
        first = [copy(0, me, sibling, src=x_ref)]
        first += [copy(1 + j, me, (*chip, c), src=x_ref) for j, chip in enumerate(chips)]
        for cp in first:
            cp.start()
        passed = [copy(4 + j, (*chip, c), sibling) for j, chip in enumerate(chips)]
        for j, chip in enumerate(chips):
            copy(1 + j, (*chip, c), me).wait_recv()
            passed[j].start()
        copy(0, sibling, me).wait_recv()
        for j, chip in enumerate(chips):
            copy(4 + j, (*chip, 1 - c), me).wait_recv()
        for cp in first + passed:
            cp.wait_send()
        mine.wait()

    return pl.pallas_call(
        body, name=name, out_shape=jax.ShapeDtypeStruct((N_DEV, r, c_), shard.dtype),
        in_specs=[ANY], out_specs=ANY,
        scratch_shapes=[pltpu.SemaphoreType.DMA((7,)), pltpu.SemaphoreType.DMA((7,)), pltpu.SemaphoreType.DMA],
    )(shard)


HBM = pl.BlockSpec(memory_space=pltpu.HBM)
SEM = pl.BlockSpec(memory_space=pltpu.SEMAPHORE)
SIDE_EFFECT = pltpu.CompilerParams(has_side_effects=pltpu.SideEffectType.DATAFLOW_SIDE_EFFECTING)


def _gather_start(shard):
    r, c_ = shard.shape

    def body(x_ref, land_ref, send_sems, recv_sems, x_thru, land_thru, token):
        x, y, c = lax.axis_index("x"), lax.axis_index("y"), lax.axis_index("c")
        me = 4 * x + 2 * y + c
        for k in range(1, N_DEV):
            peer = (1 - x if k & 4 else x, 1 - y if k & 2 else y, 1 - c if k & 1 else c)
            pltpu.make_async_remote_copy(src_ref=x_ref, dst_ref=land_ref.at[me], send_sem=send_sems.at[k - 1],
                                         recv_sem=recv_sems.at[k - 1], device_id=peer, device_id_type=MESH).start()
        token[...] = jnp.zeros_like(token)

    land = pltpu.with_memory_space_constraint(lax.empty((N_DEV, r, c_), shard.dtype), pltpu.HBM)
    return pl.pallas_call(
        body, name="gather_late_weights_start",
        out_shape=(pltpu.SemaphoreType.DMA((N_DEV - 1,)), pltpu.SemaphoreType.DMA((N_DEV - 1,)),
                   pltpu.HBM(shard.shape, shard.dtype), pltpu.HBM((N_DEV, r, c_), shard.dtype),
                   jax.ShapeDtypeStruct((8, 128), F32)),
        in_specs=(HBM, HBM), out_specs=(SEM, SEM, HBM, HBM, pl.BlockSpec(memory_space=pltpu.VMEM)),
        input_output_aliases={0: 2, 1: 3}, compiler_params=SIDE_EFFECT,
    )(pltpu.with_memory_space_constraint(shard, pltpu.HBM), land)


def _gather_wait(send_sems, recv_sems, shard_thru, land_thru, after):
    def body(x_ref, land_ref, send_sems, recv_sems, after_ref, x_dead, got_ref):
        x, y, c = lax.axis_index("x"), lax.axis_index("y"), lax.axis_index("c")
        for k in range(N_DEV - 1):
            copy = pltpu.make_async_remote_copy(src_ref=x_ref, dst_ref=land_ref.at[0], send_sem=send_sems.at[k],
                                                recv_sem=recv_sems.at[k], device_id=(x, y, c), device_id_type=MESH)
            copy.wait_send()
            copy.wait_recv()

    return pl.pallas_call(
        body, name="gather_late_weights_wait",
        out_shape=(pltpu.HBM(shard_thru.shape, shard_thru.dtype), pltpu.HBM(land_thru.shape, land_thru.dtype)),
        in_specs=(HBM, HBM, SEM, SEM, ANY), out_specs=(HBM, HBM), input_output_aliases={0: 0, 1: 1},
        compiler_params=SIDE_EFFECT,
    )(shard_thru, land_thru, send_sems, recv_sems, after)[1]


def _hbm(a):
    return pltpu.with_memory_space_constraint(a, pltpu.HBM)


def _exchange_start(name, arrays, lands, plan, ncopies):
    n = len(arrays)

    def body(*refs):
        send_sems, recv_sems, token = refs[2 * n], refs[2 * n + 1], refs[-1]
        x, y, c = lax.axis_index("x"), lax.axis_index("y"), lax.axis_index("c")
        for k, (src, dst, peer) in enumerate(plan(x, y, c, refs[:n], refs[n:2 * n])):
            pltpu.make_async_remote_copy(src_ref=src, dst_ref=dst, send_sem=send_sems.at[k], recv_sem=recv_sems.at[k],
                                         device_id=peer, device_id_type=MESH).start()
        token[...] = jnp.zeros_like(token)

    both = list(arrays) + list(lands)
    outs = pl.pallas_call(
        body, name=name,
        out_shape=(pltpu.SemaphoreType.DMA((ncopies,)), pltpu.SemaphoreType.DMA((ncopies,)),
                   *[pltpu.HBM(a.shape, a.dtype) for a in both], jax.ShapeDtypeStruct((8, 128), F32)),
        in_specs=(HBM,) * (2 * n), out_specs=(SEM, SEM) + (HBM,) * (2 * n) + (pl.BlockSpec(memory_space=pltpu.VMEM),),
        input_output_aliases={i: 2 + i for i in range(2 * n)}, compiler_params=SIDE_EFFECT,
    )(*[_hbm(a) for a in both])
    return outs[0], outs[1], list(outs[2:2 + n]), list(outs[2 + n:2 + 2 * n]), outs[-1]


def _exchange_wait(name, send_sems, recv_sems, arrays, lands, plan, after):
    n = len(arrays)

    def body(*refs):
        send_sems, recv_sems = refs[2 * n], refs[2 * n + 1]
        x, y, c = lax.axis_index("x"), lax.axis_index("y"), lax.axis_index("c")
        for k, (src, dst, peer) in enumerate(plan(x, y, c, refs[:n], refs[n:2 * n])):
            copy = pltpu.make_async_remote_copy(src_ref=src, dst_ref=dst, send_sem=send_sems.at[k], recv_sem=recv_sems.at[k],
                                                device_id=peer, device_id_type=MESH)
            copy.wait_send()
            copy.wait_recv()

    both = list(arrays) + list(lands)
    outs = pl.pallas_call(
        body, name=name, out_shape=tuple(pltpu.HBM(a.shape, a.dtype) for a in both),
        in_specs=(HBM,) * (2 * n) + (SEM, SEM, ANY), out_specs=(HBM,) * (2 * n),
        input_output_aliases={i: i for i in range(2 * n)}, compiler_params=SIDE_EFFECT,
    )(*both, send_sems, recv_sems, after)
    return list(outs[:n]), list(outs[n:])


def _sibling_plan(x, y, c, g_refs, land_refs):
    return [(g.at[:, 2 * q + (1 - c)], o.at[:, q], (x, y, 1 - c)) for g, o in zip(g_refs, land_refs) for q in range(4)]


def _chips_plan(x, y, c, p_refs, land_refs):
    chips = [(1 - x, y), (x, 1 - y), (1 - x, 1 - y)]
    return [(p_.at[:, 2 * qx + qy], o.at[:, 2 * x + y], (qx, qy, c)) for p_, o in zip(p_refs, land_refs) for qx, qy in chips]


class _GradExchange:
    def __init__(self):
        self.core = lax.axis_index("c").astype(jnp.int32).reshape(1)
        self.chip = 2 * lax.axis_index("x") + lax.axis_index("y")
        self.groups = []

    def submit(self, tag, arrays, dtypes):
        arrays = [a.reshape(a.shape[0], N_DEV, a.shape[1] // N_DEV, a.shape[2]) for a in arrays]
        lands = [lax.empty((a.shape[0], 4) + a.shape[2:], a.dtype) for a in arrays]
        send, recv, arrays, lands, token = _exchange_start(f"rs_pair_start_{tag}", arrays, lands, _sibling_plan, 4 * len(arrays))
        self.groups.append(dict(tag=tag, stage=1, sems=(send, recv), arrays=arrays, lands=lands, dtypes=dtypes))
        return token

    def advance(self, after):
        token = None
        for g in self.groups:
            if g["stage"] == 1:
                arrays, got = _exchange_wait(f"rs_pair_wait_{g['tag']}", *g["sems"], g["arrays"], g["lands"], _sibling_plan, after)
                parts = [_rs_pair_add(a, b, self.core, dt) for a, b, dt in zip(arrays, got, g["dtypes"])]
                lands = [lax.empty(p_.shape, p_.dtype) for p_ in parts]
                send, recv, parts, lands, tok = _exchange_start(f"rs_chip_start_{g['tag']}", parts, lands, _chips_plan, 3 * len(parts))
                g.update(stage=2, sems=(send, recv), arrays=parts, lands=lands)
                token = tok if token is None else token + tok
            elif g["stage"] == 2:
                parts, lands = _exchange_wait(f"rs_chip_wait_{g['tag']}", *g["sems"], g["arrays"], g["lands"], _chips_plan, after)
                sums = []
                for p_, land in zip(parts, lands):
                    l, _, r, c_ = p_.shape
                    own = lax.dynamic_slice(p_, (0, self.chip, 0, 0), (l, 1, r, c_))
                    sums.append(_sum_parts(lax.dynamic_update_slice(land, own, (0, self.chip, 0, 0)), "sum_chips"))
                g.update(stage=3, sums=sums)
        return token

    def results(self):
        return [g.get("sums") for g in self.groups]


def _with_rows(g, n):
    return jax.ShapeDtypeStruct((g.shape[0], n) + tuple(g.shape[2:]), g.dtype)


def _rs_sibling(gs):
    n = len(gs)

    def body(*refs):
        g_refs, o_refs, (send_sems, recv_sems) = refs[:n], refs[n:2 * n], refs[2 * n:]
        x, y, c = lax.axis_index("x"), lax.axis_index("y"), lax.axis_index("c")
        copies = [pltpu.make_async_remote_copy(
            src_ref=g_refs[w].at[:, 2 * q + (1 - c)], dst_ref=o_refs[w].at[:, q], send_sem=send_sems.at[4 * w + q],
            recv_sem=recv_sems.at[4 * w + q], device_id=(x, y, 1 - c), device_id_type=MESH)
            for w in range(n) for q in range(4)]
        for cp in copies:
            cp.start()
        for cp in copies:
            cp.wait_recv()
        for cp in copies:
            cp.wait_send()

    return pl.pallas_call(
        body, name="rs_sibling", out_shape=[_with_rows(g, 4) for g in gs],
        in_specs=[ANY] * n, out_specs=[ANY] * n,
        scratch_shapes=[pltpu.SemaphoreType.DMA((4 * n,)), pltpu.SemaphoreType.DMA((4 * n,))],
    )(*gs)


def _rs_pair_add(g, got, core, out_dtype):
    l, _, r, c_ = g.shape

    def body(core_ref, g_ref, got_ref, o_ref):
        o_ref[...] = (g_ref[...].astype(F32) + got_ref[...].astype(F32)).astype(out_dtype)

    blk = (None, None, r, c_)
    return pl.pallas_call(
        body, name="rs_pair_add", out_shape=jax.ShapeDtypeStruct((l, 4, r, c_), out_dtype),
        grid_spec=pltpu.PrefetchScalarGridSpec(
            num_scalar_prefetch=1, grid=(l, 4),
            in_specs=[pl.BlockSpec(blk, lambda i, q, core_ref: (i, 2 * q + core_ref[0], 0, 0)),
                      pl.BlockSpec(blk, lambda i, q, core_ref: (i, q, 0, 0))],
            out_specs=pl.BlockSpec(blk, lambda i, q, core_ref: (i, q, 0, 0))),
        compiler_params=_params("parallel", "parallel"),
    )(core, g, got)


def _rs_chips(parts):
    n = len(parts)

    def body(*refs):
        p_refs, o_refs, (send_sems, recv_sems, local_sems) = refs[:n], refs[n:2 * n], refs[2 * n:]
        x, y, c = lax.axis_index("x"), lax.axis_index("y"), lax.axis_index("c")
        my_chip = 2 * x + y
        chips = [(1 - x, y), (x, 1 - y), (1 - x, 1 - y)]
        local = [pltpu.make_async_copy(p_refs[w].at[:, my_chip], o_refs[w].at[:, my_chip], local_sems.at[w]) for w in range(n)]
        for cp in local:
            cp.start()
        copies = [pltpu.make_async_remote_copy(
            src_ref=p_refs[w].at[:, 2 * qx + qy], dst_ref=o_refs[w].at[:, my_chip], send_sem=send_sems.at[3 * w + k],
            recv_sem=recv_sems.at[3 * w + k], device_id=(qx, qy, c), device_id_type=MESH)
            for w in range(n) for k, (qx, qy) in enumerate(chips)]
        for cp in copies:
            cp.start()
        for cp in copies:
            cp.wait_recv()
        for cp in copies:
            cp.wait_send()
        for cp in local:
            cp.wait()

    return pl.pallas_call(
        body, name="rs_chips", out_shape=[jax.ShapeDtypeStruct(p.shape, p.dtype) for p in parts],
        in_specs=[ANY] * n, out_specs=[ANY] * n,
        scratch_shapes=[pltpu.SemaphoreType.DMA((3 * n,)), pltpu.SemaphoreType.DMA((3 * n,)), pltpu.SemaphoreType.DMA((n,))],
    )(*parts)


def _sum_parts(parts, name):
    l, n, r, c_ = parts.shape

    def body(p_ref, o_ref):
        g = p_ref[0].astype(F32)
        for s in range(1, n):
            g = g + p_ref[s].astype(F32)
        o_ref[...] = g

    return pl.pallas_call(
        body, name=name, out_shape=jax.ShapeDtypeStruct((l, r, c_), F32), grid=(l,),
        in_specs=[pl.BlockSpec((None, n, r, c_), lambda i: (i, 0, 0, 0))],
        out_specs=pl.BlockSpec((None, r, c_), lambda i: (i, 0, 0)), compiler_params=_params("parallel"),
    )(parts)


def _adamw_math(w, g, m, v):
    m = ADAM_B1 * m + (1.0 - ADAM_B1) * g
    v = ADAM_B2 * v + (1.0 - ADAM_B2) * (g * g)
    m_hat = m / (1.0 - ADAM_B1 ** ADAM_STEP)
    v_hat = v / (1.0 - ADAM_B2 ** ADAM_STEP)
    delta = -ADAM_LR * (m_hat / (jnp.sqrt(v_hat) + ADAM_EPS) + ADAM_WD * w)
    return delta, m, v


def _adamw(g, w, m, v, name):
    l, k, n = w.shape
    tk = 256 if k % 256 == 0 else k

    def body(g_ref, w_ref, m_ref, v_ref, d_ref, nm_ref, nv_ref):
        d_ref[...], nm_ref[...], nv_ref[...] = _adamw_math(w_ref[...], g_ref[...], m_ref[...], v_ref[...])

    spec = pl.BlockSpec((None, tk, n), lambda i, j: (i, j, 0))
    return pl.pallas_call(
        body, name=name, out_shape=[jax.ShapeDtypeStruct((l, k, n), F32)] * 3, grid=(l, k // tk),
        in_specs=[spec] * 4, out_specs=[spec] * 3, compiler_params=_params("parallel", "parallel"),
    )(g, w, m, v)


def _sum_adamw(parts, w, m, v, name):
    n, r, c_ = parts.shape

    def body(p_ref, w_ref, m_ref, v_ref, g_ref, d_ref, nm_ref, nv_ref):
        g = p_ref[0]
        for s in range(1, n):
            g = g + p_ref[s]
        g_ref[...] = g
        d_ref[...], nm_ref[...], nv_ref[...] = _adamw_math(w_ref[...], g, m_ref[...], v_ref[...])

    return pl.pallas_call(
        body, name=name, out_shape=[jax.ShapeDtypeStruct((r, c_), F32)] * 4, grid=(1,),
        in_specs=[_full((n, r, c_))] + [_full((r, c_))] * 3, out_specs=[_full((r, c_))] * 4,
        compiler_params=_params("arbitrary"),
    )(parts, w, m, v)


def _rope_tables(pos_col, freq_row):
    s = pos_col.shape[0]
    tm = min(1024, s)

    def body(p_ref, f_ref, o_ref):
        ang = p_ref[...].astype(F32) * f_ref[...]
        lane = lax.broadcasted_iota(jnp.int32, ang.shape, 1) & (HEAD_DIM - 1)
        cs, sn = jnp.cos(ang), jnp.sin(ang)
        o_ref[:, 0:128] = jnp.where(lane < ROT_DIM, cs, 1.0)
        o_ref[:, 128:256] = jnp.where((lane >= ROT_DIM // 2) & (lane < ROT_DIM), sn, 0.0)
        o_ref[:, 256:384] = jnp.where(lane < ROT_DIM // 2, -sn, 0.0)

    return pl.pallas_call(
        body, name="rope_tables", out_shape=jax.ShapeDtypeStruct((s, ROPE_COLS), F32), grid=(s // tm,),
        in_specs=[pl.BlockSpec((tm, 1), lambda i: (i, 0)), _full((1, 128))],
        out_specs=_rows(tm, ROPE_COLS), compiler_params=_params("parallel"),
    )(pos_col, freq_row)


ROPE_COLS = 3 * 128


def _rope_parts(tab, reps=1):
    return [jnp.tile(tab[:, k * 128:(k + 1) * 128], (1, reps)) if reps > 1 else tab[:, k * 128:(k + 1) * 128] for k in range(3)]


def _rope_apply(t, tab):
    w = t.shape[1]
    cos, sin_up, sin_dn = _rope_parts(tab, w // 128)
    return t * cos + pltpu.roll(t, 8, 1) * sin_up + pltpu.roll(t, w - 8, 1) * sin_dn


def _rope_transpose(dr, tab):
    w = dr.shape[1]
    cos, sin_up, sin_dn = _rope_parts(tab, w // 128)
    return dr * cos + pltpu.roll(dr * sin_up, w - 8, 1) + pltpu.roll(dr * sin_dn, 8, 1)


def _norm_matmul(x, g, wt, *, tn, name, bias=None, tm=1024):
    s, d = x.shape
    n = wt.shape[0]
    tm = min(tm, s)

    def body(*refs):
        x_ref, g_ref, w_ref = refs[:3]
        b_ref = refs[3] if bias is not None else None
        h_ref, o_ref = refs[-2:]

        @pl.when(pl.program_id(1) == 0)
        def _():
            xv = x_ref[...]
            r = lax.rsqrt(jnp.mean(xv * xv, axis=-1, keepdims=True) + EPS)
            h_ref[...] = (xv * r * g_ref[...]).astype(BF16)

        acc = lax.dot_general(h_ref[...], w_ref[...], NT, preferred_element_type=F32)
        if b_ref is not None:
            acc = acc + b_ref[...]
        o_ref[...] = acc.astype(BF16)

    in_specs = [_rows(tm, d), _full((1, d)), pl.BlockSpec((tn, d), lambda i, j: (j, 0))]
    args = [x, g, wt]
    if bias is not None:
        in_specs.append(pl.BlockSpec((1, tn), lambda i, j: (0, j)))
        args.append(bias)
    return pl.pallas_call(
        body, name=name,
        out_shape=[jax.ShapeDtypeStruct((s, d), BF16), jax.ShapeDtypeStruct((s, n), BF16)],
        grid=(s // tm, n // tn), in_specs=in_specs,
        out_specs=[_rows(tm, d), pl.BlockSpec((tm, tn), lambda i, j: (i, j))],
        compiler_params=_params("parallel", "arbitrary"),
    )(*args)


def _class_major(tm, dil):
    p = np.zeros((tm, tm), np.float32)
    per = tm // dil
    for r in range(dil):
        for j in range(per):
            p[r * per + j, j * dil + r] = 1.0
    return jnp.asarray(p, dtype=BF16)


def _qkv_proj(x, g, wt, rope, tm=512):
    s, d = x.shape
    n = wt.shape[0]
    gw3 = 3 * GROUP_WIDTH
    tm = min(tm, s)
    assert n == 3 * gw3

    def body(x_ref, g_ref, w_ref, tab_ref, p1_ref, p2_ref, h_ref, o0_ref, o1_ref, o2_ref):
        j = pl.program_id(1)

        @pl.when(j == 0)
        def _():
            xv = x_ref[...]
            r = lax.rsqrt(jnp.mean(xv * xv, axis=-1, keepdims=True) + EPS)
            h_ref[...] = (xv * r * g_ref[...]).astype(BF16)

        acc = lax.dot_general(h_ref[...], w_ref[...], NT, preferred_element_type=F32)

        def store(y):
            yb = y.astype(BF16)
            o0_ref[:, pl.ds(pl.multiple_of(j * GROUP_WIDTH, GROUP_WIDTH), GROUP_WIDTH)] = yb[:, :GROUP_WIDTH]
            for grp, o_ref, p_ref in ((1, o1_ref, p1_ref), (2, o2_ref, p2_ref)):
                dil = DILATIONS[grp]
                per = tm // dil
                yp = jnp.dot(p_ref[...], yb[:, grp * GROUP_WIDTH:(grp + 1) * GROUP_WIDTH],
                             preferred_element_type=F32).astype(BF16)
                for r in range(dil):
                    col = pl.multiple_of(r * gw3 + j * GROUP_WIDTH, GROUP_WIDTH)
                    o_ref[:, pl.ds(col, GROUP_WIDTH)] = yp[r * per:(r + 1) * per, :]

        @pl.when(j < 2)
        def _():
            store(_rope_apply(acc, tab_ref[...]))

        @pl.when(j == 2)
        def _():
            store(acc)

    outs = [jax.ShapeDtypeStruct((s, d), BF16)] + [jax.ShapeDtypeStruct((s // dl, dl * gw3), BF16) for dl in DILATIONS]
    out_specs = [_rows(tm, d)] + [_rows(tm // dl, dl * gw3) for dl in DILATIONS]
    return pl.pallas_call(
        body, name="attn_qkv", out_shape=outs, grid=(s // tm, 3),
        in_specs=[_rows(tm, d), _full((1, d)), pl.BlockSpec((gw3, d), lambda i, j: (j, 0)), _rows(tm, ROPE_COLS)]
        + [_full((tm, tm))] * 2,
        out_specs=out_specs, compiler_params=_params("parallel", "arbitrary"),
    )(x, g, wt, rope, _class_major(tm, DILATIONS[1]), _class_major(tm, DILATIONS[2]))


def _head_masks(rows=SPAN):
    lane = lax.broadcasted_iota(jnp.int32, (rows, 128), 1)
    masks = [lane < HEAD_DIM, lane >= HEAD_DIM]
    lane1 = lax.broadcasted_iota(jnp.int32, (1, 128), 1)
    keep = [jnp.where(lane1 < HEAD_DIM, 1.0, 0.0).astype(BF16), jnp.where(lane1 >= HEAD_DIM, 1.0, 0.0).astype(BF16)]
    return masks, keep


def _band_mask(b):
    row = lax.broadcasted_iota(jnp.int32, (2 * SPAN, 2 * SPAN), 0) & (SPAN - 1)
    col = lax.broadcasted_iota(jnp.int32, (2 * SPAN, 2 * SPAN), 1)
    no_prev = jnp.where(b > 0, 0, 4 * SPAN)
    return ((col < SPAN) & (col >= row + no_prev)) | ((col >= SPAN) & (col - SPAN <= row))


def _attn_fwd(qv, grp, dil):
    l = qv.shape[0]
    s = l * dil
    nb = l // SPAN
    nq = next(n for n in (4, 2, 1) if nb % n == 0)

    def body(q_ref, kp_ref, kc_ref, vp_ref, vc_ref, o_ref, l_ref):
        b = pl.program_id(1)
        masks, keep = _head_masks()
        for qb in range(nq):
            valid = _band_mask(b * nq + qb)
            rows = slice(qb * SPAN, (qb + 1) * SPAN)
            before = slice((qb - 1) * SPAN, qb * SPAN)
            for p in range(GROUP_WIDTH // 128):
                sl = slice(p * 128, (p + 1) * 128)
                qp = q_ref[rows, sl]
                kk = jnp.concatenate([kp_ref[:, sl] if qb == 0 else kc_ref[before, sl], kc_ref[rows, sl]], axis=0)
                vv = jnp.concatenate([vp_ref[:, sl] if qb == 0 else vc_ref[before, sl], vc_ref[rows, sl]], axis=0)
                q2 = jnp.concatenate([qp * keep[0], qp * keep[1]], axis=0)
                sc = lax.dot_general(q2, kk, NT, preferred_element_type=F32) * (HEAD_DIM ** -0.5)
                sc = jnp.where(valid, sc, -1e30)
                mx = jnp.max(sc, axis=-1, keepdims=True)
                pe = jnp.exp(sc - mx)
                den = jnp.sum(pe, axis=-1, keepdims=True)
                out = jnp.dot(pe.astype(BF16), vv, preferred_element_type=F32) / den
                lse = jnp.broadcast_to(mx + jnp.log(den), (2 * SPAN, 128))
                o_ref[rows, sl] = jnp.where(masks[0], out[:SPAN], out[SPAN:])
                l_ref[rows, sl] = jnp.where(masks[0], lse[:SPAN], lse[SPAN:])

    blk = (nq * SPAN, GROUP_WIDTH)
    cur = lambda t: pl.BlockSpec(blk, lambda r, b: (b, r * 3 + t))
    prev = lambda t: pl.BlockSpec((SPAN, GROUP_WIDTH), lambda r, b: (jnp.maximum(nq * b - 1, 0), r * 3 + t))
    out = pl.BlockSpec(blk, lambda r, b: (b, r))
    o, lse = pl.pallas_call(
        body, name=f"attn_fwd_g{grp}", out_shape=[jax.ShapeDtypeStruct((l, dil * GROUP_WIDTH), F32)] * 2,
        grid=(dil, nb // nq), in_specs=[cur(0), prev(1), cur(1), prev(2), cur(2)], out_specs=[out, out],
        compiler_params=_params("parallel", "arbitrary"),
    )(qv, qv, qv, qv, qv)
    return o.reshape(s, GROUP_WIDTH), lse.reshape(s, GROUP_WIDTH)


def _resnorm_store(y, x_ref, g_ref, y_ref, xo_ref):
    r = lax.rsqrt(jnp.mean(y * y, axis=-1, keepdims=True) + EPS)
    y_ref[...] = y
    xo_ref[...] = x_ref[...] + y * r * g_ref[...]


def _mix_wo(os_, ls_, wot, x, g, tm=512):
    s, d = x.shape
    gw = wot.shape[1]
    tm = min(tm, s)

    def body(o0, o1, o2, l0, l1, l2, w_ref, x_ref, g_ref, y_ref, xo_ref, mixed_ref, lse_ref):
        a0, a1, a2 = l0[...], l1[...], l2[...]
        mx = jnp.maximum(jnp.maximum(a0, a1), a2)
        e0, e1, e2 = jnp.exp(a0 - mx), jnp.exp(a1 - mx), jnp.exp(a2 - mx)
        den = e0 + e1 + e2
        mixed = (e0 / den) * o0[...] + (e1 / den) * o1[...] + (e2 / den) * o2[...]
        mixed_ref[...] = mixed.astype(BF16)
        lse_ref[...] = mx + jnp.log(den)
        y = lax.dot_general(mixed.astype(BF16), w_ref[...], NT, preferred_element_type=F32)
        _resnorm_store(y, x_ref, g_ref, y_ref, xo_ref)

    return pl.pallas_call(
        body, name="mix_wo",
        out_shape=[jax.ShapeDtypeStruct((s, d), F32), jax.ShapeDtypeStruct((s, d), F32),
                   jax.ShapeDtypeStruct((s, gw), BF16), jax.ShapeDtypeStruct((s, gw), F32)],
        grid=(s // tm,), in_specs=[_rows(tm, gw)] * 6 + [_full((d, gw)), _rows(tm, d), _full((1, d))],
        out_specs=[_rows(tm, d), _rows(tm, d), _rows(tm, gw), _rows(tm, gw)],
        compiler_params=_params("parallel"),
    )(*os_, *ls_, wot, x, g)


def _matmul_resnorm(a, w, x, g, *, name, bias=None, tm=512):
    s, k = a.shape
    d = w.shape[1]
    tm = min(tm, s)

    def body(*refs):
        a_ref, w_ref = refs[:2]
        b_ref = refs[2] if bias is not None else None
        x_ref, g_ref, y_ref, xo_ref = refs[-4:]
        y = jnp.dot(a_ref[...], w_ref[...], preferred_element_type=F32)
        if b_ref is not None:
            y = y + b_ref[...]
        _resnorm_store(y, x_ref, g_ref, y_ref, xo_ref)

    in_specs = [_rows(tm, k), _full((k, d))] + ([_full((1, d))] if bias is not None else []) + [_rows(tm, d), _full((1, d))]
    args = [a, w] + ([bias] if bias is not None else []) + [x, g]
    return pl.pallas_call(
        body, name=name, out_shape=[jax.ShapeDtypeStruct((s, d), F32)] * 2, grid=(s // tm,),
        in_specs=in_specs, out_specs=[_rows(tm, d)] * 2, compiler_params=_params("parallel"),
    )(*args)


FFN_SUB = 256


def _conv3_rows(z_ref, halo_ref, rb, sub, cs, first):
    zc = z_ref[rb * sub:(rb + 1) * sub, cs].astype(F32)
    if rb == 0:
        halo = halo_ref[:, cs].astype(F32) * jnp.where(first, 0.0, 1.0)
    else:
        halo = z_ref[rb * sub - 16:rb * sub, cs].astype(F32)[8:]
    z2, z1 = _conv3_taps(zc, halo)
    return z2, z1, zc


def _conv3_taps(z, halo):
    row = lax.broadcasted_iota(jnp.int32, (8, z.shape[1]), 0)
    h6, h7 = halo[6:7, :], halo[7:8, :]
    r1, r2 = pltpu.roll(z, 1, 0), pltpu.roll(z, 2, 0)
    z1 = jnp.concatenate([jnp.where(row == 0, h7, r1[0:8]), r1[8:]], axis=0)
    z2 = jnp.concatenate([jnp.where(row == 0, h6, jnp.where(row == 1, h7, r2[0:8])), r2[8:]], axis=0)
    return z2, z1


def _ffn_cols(f):
    return _tile(f)


def _lane_chunks(width, fn):
    def step(k, carry):
        fn(pl.ds(pl.multiple_of(k * 128, 128), 128))
        return carry

    lax.fori_loop(0, width // 128, step, 0)


def _ffn_act(z, w_dw, b_dw, tm=1024):
    s, f2 = z.shape
    f = f2 // 2
    tm = min(tm, s)
    sub = min(FFN_SUB, tm)
    tc = _ffn_cols(f)
    nfc = f // tc

    def body(zu, zg, hu, hg, wu, wg, bu, bg, o_ref):
        first = pl.program_id(0) == 0

        def chunk(cs):
            for rb in range(tm // sub):
                def conv(z_ref, h_ref, w_ref, b_ref):
                    z2, z1, zc = _conv3_rows(z_ref, h_ref, rb, sub, cs, first)
                    return w_ref[0:1, cs] * z2 + w_ref[1:2, cs] * z1 + w_ref[2:3, cs] * zc + b_ref[:, cs]

                up, gate = conv(zu, hu, wu, bu), conv(zg, hg, wg, bg)
                o_ref[rb * sub:(rb + 1) * sub, cs] = (gate * _sigmoid(gate) * up).astype(BF16)

        _lane_chunks(tc, chunk)

    hb = tm // 8
    tile = lambda off: pl.BlockSpec((tm, tc), lambda i, j: (i, off + j))
    halo = lambda off: pl.BlockSpec((8, tc), lambda i, j: (jnp.maximum(i * hb - 1, 0), off + j))
    prm = lambda rows, off: pl.BlockSpec((rows, tc), lambda i, j: (0, off + j))
    return pl.pallas_call(
        body, name="ffn_act", out_shape=jax.ShapeDtypeStruct((s, f), BF16), grid=(s // tm, nfc),
        in_specs=[tile(0), tile(nfc), halo(0), halo(nfc), prm(FFN_CONV, 0), prm(FFN_CONV, nfc), prm(1, 0), prm(1, nfc)],
        out_specs=pl.BlockSpec((tm, tc), lambda i, j: (i, j)), compiler_params=_params("parallel", "parallel"),
    )(z, z, z, z, w_dw, w_dw, b_dw, b_dw)


def _shifted_planes(ext_ref):
    rows = ext_ref.shape[1]
    for s in range(1, 8):
        ext_ref[s, 0:rows - 8, :] = ext_ref[0, s:s + rows - 8, :]


def _window(ext_ref, off, tm, cs):
    s = off % 8
    return ext_ref[s, off - s:off - s + tm, cs]


def _conv_taps(ext_ref, w_ref, offs, tm, out_ref):
    def chunk(cs):
        acc = w_ref[0:1, cs] * _window(ext_ref, offs[0], tm, cs)
        for j in range(1, len(offs)):
            acc = acc + w_ref[j:j + 1, cs] * _window(ext_ref, offs[j], tm, cs)
        out_ref[:, cs] = acc

    _lane_chunks(out_ref.shape[1], chunk)


def _glu_planes(ag_ref, halo_ref, ext_ref, first, c):
    hal = halo_ref[...].astype(F32)
    ext_ref[0, 0:CONV_HALO, :] = hal[:, :c] * _sigmoid(hal[:, c:]) * jnp.where(first, 0.0, 1.0)
    ag = ag_ref[...].astype(F32)
    ext_ref[0, CONV_HALO:, :] = ag[:, :c] * _sigmoid(ag[:, c:])
    _shifted_planes(ext_ref)


def _layernorm_stats(u1):
    mu = jnp.mean(u1, axis=-1, keepdims=True)
    cen = u1 - mu
    rstd = lax.rsqrt(jnp.mean(cen * cen, axis=-1, keepdims=True) + EPS)
    return cen * rstd, rstd


def _conv_mid(ag, w_dw, b_dw, ln_g, ln_b, tm=256):
    s, c2 = ag.shape
    c = c2 // 2
    tm = min(tm, s)

    def body(ag_ref, halo_ref, w_ref, b_ref, g_ref, bb_ref, o_ref, u1_ref, ext_ref):
        _glu_planes(ag_ref, halo_ref, ext_ref, pl.program_id(0) == 0, c)
        base = CONV_HALO - (CONV_KERNEL - 1)
        _conv_taps(ext_ref, w_ref, [base + j for j in range(CONV_KERNEL)], tm, u1_ref)
        xh, _ = _layernorm_stats(u1_ref[...] + b_ref[...])
        u2 = xh * g_ref[...] + bb_ref[...]
        o_ref[...] = (u2 * _sigmoid(u2)).astype(BF16)

    hb = tm // CONV_HALO
    return pl.pallas_call(
        body, name="conv_mid", out_shape=[jax.ShapeDtypeStruct((s, c), BF16), jax.ShapeDtypeStruct((s, c), F32)], grid=(s // tm,),
        in_specs=[_rows(tm, c2), pl.BlockSpec((CONV_HALO, c2), lambda i: (jnp.maximum(i * hb - 1, 0), 0)),
                  _full((CONV_KERNEL, c)), _full((1, c)), _full((1, c)), _full((1, c))],
        out_specs=[_rows(tm, c), _rows(tm, c)], scratch_shapes=[pltpu.VMEM((8, CONV_HALO + tm, c), F32)],
        compiler_params=_params("arbitrary"),
    )(ag, ag, w_dw, b_dw, ln_g, ln_b)


def _loss_grad(xo, target, tm=1024):
    s, d = xo.shape
    tm = min(tm, s)

    def body(x_ref, t_ref, dx_ref, loss_ref):
        @pl.when(pl.program_id(0) == 0)
        def _():
            loss_ref[...] = jnp.zeros_like(loss_ref)

        err = x_ref[...] - t_ref[...]
        dx_ref[...] = err * (1.0 / d)
        loss_ref[...] += 0.5 * jnp.sum(jnp.mean(err * err, axis=-1, keepdims=True))

    return pl.pallas_call(
        body, name="loss_grad", out_shape=[jax.ShapeDtypeStruct((s, d), F32), jax.ShapeDtypeStruct((1, 128), F32)],
        grid=(s // tm,), in_specs=[_rows(tm, d)] * 2, out_specs=[_rows(tm, d), _full((1, 128))],
        compiler_params=_params("arbitrary"),
    )(xo, target)


def _postnorm_bwd(y, g, dxo, *, name, with_bias_grad=False, tm=1024):
    s, d = y.shape
    tm = min(tm, s)

    def body(y_ref, g_ref, dx_ref, dy_ref, dg_ref, *rest):
        @pl.when(pl.program_id(0) == 0)
        def _():
            dg_ref[...] = jnp.zeros_like(dg_ref)
            for r_ in rest:
                r_[...] = jnp.zeros_like(r_)

        yv, dxo_v = y_ref[...], dx_ref[...]
        r = lax.rsqrt(jnp.mean(yv * yv, axis=-1, keepdims=True) + EPS)
        yh = yv * r
        dyh = dxo_v * g_ref[...]
        dy = r * (dyh - yh * jnp.mean(dyh * yh, axis=-1, keepdims=True))
        dy_ref[...] = dy.astype(BF16)
        dg_ref[...] += jnp.sum(dxo_v * yh, axis=0, keepdims=True)
        for r_ in rest:
            r_[...] += jnp.sum(dy, axis=0, keepdims=True)

    nacc = 2 if with_bias_grad else 1
    return pl.pallas_call(
        body, name=name, out_shape=[jax.ShapeDtypeStruct((s, d), BF16)] + [jax.ShapeDtypeStruct((1, d), F32)] * nacc,
        grid=(s // tm,), in_specs=[_rows(tm, d), _full((1, d)), _rows(tm, d)],
        out_specs=[_rows(tm, d)] + [_full((1, d))] * nacc, compiler_params=_params("arbitrary"),
    )(y, g, dxo)


def _matmul(gmat, w, *, name, out_dtype, transposed_w, tm=512):
    s, k = gmat.shape
    n = w.shape[0] if transposed_w else w.shape[1]
    tm = min(tm, s)

    def body(g_ref, w_ref, o_ref):
        if transposed_w:
            acc = lax.dot_general(g_ref[...], w_ref[...], NT, preferred_element_type=F32)
        else:
            acc = jnp.dot(g_ref[...], w_ref[...], preferred_element_type=F32)
        o_ref[...] = acc.astype(out_dtype)

    return pl.pallas_call(
        body, name=name, out_shape=jax.ShapeDtypeStruct((s, n), out_dtype), grid=(s // tm,),
        in_specs=[_rows(tm, k), _full(w.shape)], out_specs=_rows(tm, n), compiler_params=_params("parallel"),
    )(gmat, w)


def _matmul_prenorm_bwd(pieces, wt, x, g, dres, *, name, tm=256):
    s, d = x.shape
    tm = min(tm, s)
    np_ = len(pieces)

    def body(*refs):
        p_refs, w_refs = refs[:np_], refs[np_:2 * np_]
        x_ref, g_ref, r_ref, dx_ref, dg_ref = refs[2 * np_:]

        @pl.when(pl.program_id(0) == 0)
        def _():
            dg_ref[...] = jnp.zeros_like(dg_ref)

        dh = None
        for p_ref, w_ref in zip(p_refs, w_refs):
            t = jnp.dot(p_ref[...], w_ref[...], preferred_element_type=F32)
            dh = t if dh is None else dh + t
        xv = x_ref[...]
        r = lax.rsqrt(jnp.mean(xv * xv, axis=-1, keepdims=True) + EPS)
        xh = xv * r
        dyh = dh * g_ref[...]
        dx_ref[...] = r_ref[...] + r * (dyh - xh * jnp.mean(dyh * xh, axis=-1, keepdims=True))
        dg_ref[...] += jnp.sum(dh * xh, axis=0, keepdims=True)

    in_specs = []
    for _, c0, kc, _ in pieces:
        assert c0 % kc == 0
        in_specs.append(pl.BlockSpec((tm, kc), lambda i, _b=c0 // kc: (i, _b)))
    for _, _, kc, r0 in pieces:
        assert r0 % kc == 0
        in_specs.append(pl.BlockSpec((kc, d), lambda i, _b=r0 // kc: (_b, 0)))
    in_specs += [_rows(tm, d), _full((1, d)), _rows(tm, d)]
    return pl.pallas_call(
        body, name=name, out_shape=[jax.ShapeDtypeStruct((s, d), F32), jax.ShapeDtypeStruct((1, d), F32)],
        grid=(s // tm,), in_specs=in_specs, out_specs=[_rows(tm, d), _full((1, d))],
        compiler_params=_params("arbitrary"),
    )(*[p[0] for p in pieces], *[wt] * np_, x, g, dres)


def _weight_grad(a, gmat, *, name, a_col0=0, ka=None, out=None, out_shape=None, layer=0, row0=0, ts=1024):
    s = a.shape[0]
    ka = a.shape[1] if ka is None else ka
    n = gmat.shape[1]
    ts = min(ts, s)
    tka = _tile(ka, a_col0, row0)
    shape = out.shape if out is not None else out_shape
    nsteps = s // ts

    def body(a_ref, g_ref, *rest):
        o_ref, acc_ref = rest[-2:]
        i = pl.program_id(1)

        @pl.when(i == 0)
        def _():
            acc_ref[...] = jnp.zeros_like(acc_ref)

        acc_ref[...] += lax.dot_general(a_ref[...], g_ref[...], TN, preferred_element_type=F32)

        @pl.when(i == nsteps - 1)
        def _():
            o_ref[...] = acc_ref[...].astype(BF16)

    in_specs = [pl.BlockSpec((ts, tka), lambda k, i: (i, a_col0 // tka + k)), pl.BlockSpec((ts, n), lambda k, i: (i, 0))]
    args = [a, gmat]
    aliases = {}
    if out is not None:
        in_specs.append(ANY)
        args.append(out)
        aliases = {2: 0}
    return pl.pallas_call(
        body, name=name, out_shape=jax.ShapeDtypeStruct(shape, BF16), grid=(ka // tka, nsteps), in_specs=in_specs,
        out_specs=pl.BlockSpec((None, tka, n), lambda k, i: (layer, row0 // tka + k, 0)),
        scratch_shapes=[pltpu.VMEM((tka, n), F32)],
        input_output_aliases=aliases, compiler_params=_params("parallel", "arbitrary"),
    )(*args)


def _ffn_act_bwd(z, dact, w_dw, b_dw, tm=512):
    s, f2 = z.shape
    f = f2 // 2
    tm = min(tm, s)
    sub = min(FFN_SUB // 2, tm)
    tc = _ffn_cols(f)
    nfc = f // tc

    def body(zu, zg, hu, hg, wu, wg, bu, bg, da_ref, du_ref, dgt_ref, dbu_ref, dbg_ref, dwu_ref, dwg_ref):
        i = pl.program_id(1)

        @pl.when(i == 0)
        def _():
            for r_ in (dbu_ref, dbg_ref, dwu_ref, dwg_ref):
                r_[...] = jnp.zeros_like(r_)

        def chunk(cs):
            for rb in range(tm // sub):
                rows = slice(rb * sub, (rb + 1) * sub)

                def conv(z_ref, h_ref, w_ref, b_ref):
                    taps = _conv3_rows(z_ref, h_ref, rb, sub, cs, i == 0)
                    return taps, w_ref[0:1, cs] * taps[0] + w_ref[1:2, cs] * taps[1] + w_ref[2:3, cs] * taps[2] + b_ref[:, cs]

                taps_u, up = conv(zu, hu, wu, bu)
                taps_g, gate = conv(zg, hg, wg, bg)
                da = da_ref[rows, cs].astype(F32)
                sg = _sigmoid(gate)
                d_up = da * (gate * sg)
                d_gate = da * up * (sg * (1.0 + gate * (1.0 - sg)))
                du_ref[rows, cs] = d_up.astype(BF16)
                dgt_ref[rows, cs] = d_gate.astype(BF16)
                for dv, taps, db_ref, dw_ref in ((d_up, taps_u, dbu_ref, dwu_ref), (d_gate, taps_g, dbg_ref, dwg_ref)):
                    db_ref[:, cs] += jnp.sum(dv, axis=0, keepdims=True)
                    for k_, tap in enumerate(taps):
                        dw_ref[k_:k_ + 1, cs] += jnp.sum(dv * tap, axis=0, keepdims=True)

        _lane_chunks(tc, chunk)

    hb = tm // 8
    tile = lambda off: pl.BlockSpec((tm, tc), lambda j, i: (i, off + j))
    halo = lambda off: pl.BlockSpec((8, tc), lambda j, i: (jnp.maximum(i * hb - 1, 0), off + j))
    prm = lambda rows, off: pl.BlockSpec((rows, tc), lambda j, i: (0, off + j))
    acc = lambda rows: pl.BlockSpec((rows, tc), lambda j, i: (0, j))
    return pl.pallas_call(
        body, name="ffn_act_bwd",
        out_shape=[jax.ShapeDtypeStruct((s, f), BF16)] * 2 + [jax.ShapeDtypeStruct((1, f), F32)] * 2
        + [jax.ShapeDtypeStruct((FFN_CONV, f), F32)] * 2,
        grid=(nfc, s // tm),
        in_specs=[tile(0), tile(nfc), halo(0), halo(nfc), prm(FFN_CONV, 0), prm(FFN_CONV, nfc), prm(1, 0), prm(1, nfc), tile(0)],
        out_specs=[tile(0), tile(0), acc(1), acc(1), acc(FFN_CONV), acc(FFN_CONV)],
        compiler_params=_params("parallel", "arbitrary"),
    )(z, z, z, z, w_dw, w_dw, b_dw, b_dw, dact)


def _conv3_transpose(dug, w_dw, col0, tm=1024):
    s, f = dug.shape
    tm = min(tm, s)
    sub = min(FFN_SUB, tm)
    nsub = tm // sub
    tc = _ffn_cols(f)
    nfc = f // tc
    nrow = s // tm
    off = col0 // tc

    def body(d_ref, n_ref, w_ref, o_ref):
        keep_next = jnp.where(pl.program_id(0) == nrow - 1, 0.0, 1.0)

        def chunk(cs):
            for rb in range(nsub):
                rows = slice(rb * sub, (rb + 1) * sub)
                dv = d_ref[rows, cs].astype(F32)
                if rb == nsub - 1:
                    nxt = n_ref[:, cs].astype(F32) * keep_next
                else:
                    nxt = d_ref[(rb + 1) * sub:(rb + 1) * sub + 16, cs].astype(F32)[:8]
                n0, n1 = nxt[0:1, :], nxt[1:2, :]
                row = lax.broadcasted_iota(jnp.int32, (8, dv.shape[1]), 0)
                r1, r2 = pltpu.roll(dv, sub - 1, 0), pltpu.roll(dv, sub - 2, 0)
                d1 = jnp.concatenate([r1[:sub - 8], jnp.where(row == 7, n0, r1[sub - 8:])], axis=0)
                d2 = jnp.concatenate([r2[:sub - 8], jnp.where(row == 7, n1, jnp.where(row == 6, n0, r2[sub - 8:]))], axis=0)
                o_ref[rows, cs] = (w_ref[2:3, cs] * dv + w_ref[1:2, cs] * d1 + w_ref[0:1, cs] * d2).astype(BF16)

        _lane_chunks(tc, chunk)

    hb = tm // 8
    return pl.pallas_call(
        body, name="conv3_transpose", out_shape=jax.ShapeDtypeStruct((s, f), BF16), grid=(nrow, nfc),
        in_specs=[pl.BlockSpec((tm, tc), lambda i, j: (i, j)),
                  pl.BlockSpec((8, tc), lambda i, j: (jnp.minimum((i + 1) * hb, s // 8 - 1), j)),
                  pl.BlockSpec((FFN_CONV, tc), lambda i, j: (0, off + j))],
        out_specs=pl.BlockSpec((tm, tc), lambda i, j: (i, j)), compiler_params=_params("parallel", "parallel"),
    )(dug, dug, w_dw)


def _conv_mid_bwd(ag, u1, du3, b_dw, ln_g, ln_b, tm=256):
    s, c2 = ag.shape
    c = c2 // 2
    tm = min(tm, s)

    def body(ag_ref, halo_ref, u1in_ref, du_ref, b_ref, g_ref, bb_ref, o_ref, dlg_ref, dlb_ref, db_ref, dw_ref, ext_ref, u1_ref):
        @pl.when(pl.program_id(0) == 0)
        def _():
            for r_ in (dlg_ref, dlb_ref, db_ref, dw_ref):
                r_[...] = jnp.zeros_like(r_)

        _glu_planes(ag_ref, halo_ref, ext_ref, pl.program_id(0) == 0, c)
        xh, rstd = _layernorm_stats(u1in_ref[...] + b_ref[...])
        u2 = xh * g_ref[...] + bb_ref[...]
        sg = _sigmoid(u2)
        du2 = du_ref[...] * (sg * (1.0 + u2 * (1.0 - sg)))
        dlg_ref[...] += jnp.sum(du2 * xh, axis=0, keepdims=True)
        dlb_ref[...] += jnp.sum(du2, axis=0, keepdims=True)
        dxh = du2 * g_ref[...]
        du1 = rstd * (dxh - jnp.mean(dxh, axis=-1, keepdims=True) - xh * jnp.mean(dxh * xh, axis=-1, keepdims=True))
        o_ref[...] = du1.astype(BF16)
        db_ref[...] += jnp.sum(du1, axis=0, keepdims=True)
        u1_ref[...] = du1
        base = CONV_HALO - (CONV_KERNEL - 1)

        def chunk(cs):
            dc = u1_ref[:, cs]
            for j in range(CONV_KERNEL):
                dw_ref[j:j + 1, cs] += jnp.sum(dc * _window(ext_ref, base + j, tm, cs), axis=0, keepdims=True)

        _lane_chunks(c, chunk)

    hb = tm // CONV_HALO
    vec = _full((1, c))
    return pl.pallas_call(
        body, name="conv_mid_bwd",
        out_shape=[jax.ShapeDtypeStruct((s, c), BF16)] + [jax.ShapeDtypeStruct((1, c), F32)] * 3
        + [jax.ShapeDtypeStruct((CONV_HALO, c), F32)],
        grid=(s // tm,),
        in_specs=[_rows(tm, c2), pl.BlockSpec((CONV_HALO, c2), lambda i: (jnp.maximum(i * hb - 1, 0), 0)), _rows(tm, c),
                  _rows(tm, c), vec, vec, vec],
        out_specs=[_rows(tm, c), vec, vec, vec, _full((CONV_HALO, c))],
        scratch_shapes=[pltpu.VMEM((8, CONV_HALO + tm, c), F32), pltpu.VMEM((tm, c), F32)],
        compiler_params=_params("arbitrary"),
    )(ag, ag, u1, du3, b_dw, ln_g, ln_b)


def _glu_conv_bwd(du1, ag, w_dw, tm=256):
    s, c = du1.shape
    tm = min(tm, s)
    nrow = s // tm

    def body(d_ref, n_ref, ag_ref, w_ref, o_ref, db_ref, ext_ref, du0_ref):
        @pl.when(pl.program_id(0) == 0)
        def _():
            db_ref[...] = jnp.zeros_like(db_ref)

        ext_ref[0, 0:tm, :] = d_ref[...].astype(F32)
        ext_ref[0, tm:, :] = n_ref[...].astype(F32) * jnp.where(pl.program_id(0) == nrow - 1, 0.0, 1.0)
        _shifted_planes(ext_ref)
        top = CONV_KERNEL - 1
        _conv_taps(ext_ref, w_ref, [top - j for j in range(CONV_KERNEL)], tm, du0_ref)
        du0 = du0_ref[...]
        ag = ag_ref[...].astype(F32)
        a, gt = ag[:, :c], ag[:, c:]
        sg = _sigmoid(gt)
        da = du0 * sg
        dgt = du0 * a * (sg * (1.0 - sg))
        o_ref[:, :c] = da.astype(BF16)
        o_ref[:, c:] = dgt.astype(BF16)
        db_ref[:, :c] += jnp.sum(da, axis=0, keepdims=True)
        db_ref[:, c:] += jnp.sum(dgt, axis=0, keepdims=True)

    hb = tm // CONV_HALO
    return pl.pallas_call(
        body, name="glu_conv_bwd",
        out_shape=[jax.ShapeDtypeStruct((s, 2 * c), BF16), jax.ShapeDtypeStruct((1, 2 * c), F32)], grid=(nrow,),
        in_specs=[_rows(tm, c), pl.BlockSpec((CONV_HALO, c), lambda i: (jnp.minimum((i + 1) * hb, s // CONV_HALO - 1), 0)),
                  _rows(tm, 2 * c), _full((CONV_KERNEL, c))],
        out_specs=[_rows(tm, 2 * c), _full((1, 2 * c))],
        scratch_shapes=[pltpu.VMEM((8, tm + CONV_HALO, c), F32), pltpu.VMEM((tm, c), F32)],
        compiler_params=_params("arbitrary"),
    )(du1, du1, ag, w_dw)


def _head_rows(v, mask):
    return jnp.max(jnp.where(mask, v, -jnp.inf), axis=-1, keepdims=True)


def _attn_bwd(qv, dmix, mixed, lse, rope, grp, dil):
    l = qv.shape[0]
    s = l * dil
    nb = l // SPAN
    view = lambda t: t.reshape(l, dil * t.shape[1])
    scale = HEAD_DIM ** -0.5
    gw = GROUP_WIDTH

    def body(q_ref, kp_ref, kc_ref, vp_ref, vc_ref, do_ref, mx_ref, l_ref, tab_ref, tabp_ref, dq_ref, dkv_ref, carry_ref):
        b = pl.program_id(1)

        @pl.when(b < nb)
        def _():
            valid = _band_mask(b)
            masks, keep = _head_masks()
            for p in range(gw // 128):
                sl = slice(p * 128, (p + 1) * 128)
                sl_v = slice(gw + p * 128, gw + (p + 1) * 128)
                qp, dop = q_ref[:, sl], do_ref[:, sl]
                kk = jnp.concatenate([kp_ref[:, sl], kc_ref[:, sl]], axis=0)
                vv = jnp.concatenate([vp_ref[:, sl], vc_ref[:, sl]], axis=0)
                prod = dop.astype(F32) * mx_ref[:, sl].astype(F32)
                lsep = l_ref[:, sl]
                q2 = jnp.concatenate([qp * keep[0], qp * keep[1]], axis=0)
                do2 = jnp.concatenate([dop * keep[0], dop * keep[1]], axis=0)
                lse2 = jnp.concatenate([_head_rows(lsep, masks[h]) for h in range(2)], axis=0)
                dbar2 = jnp.concatenate([jnp.sum(jnp.where(masks[h], prod, 0.0), axis=-1, keepdims=True) for h in range(2)], axis=0)
                sc = lax.dot_general(q2, kk, NT, preferred_element_type=F32) * scale
                pe = jnp.where(valid, jnp.exp(sc - lse2), 0.0)
                dp = lax.dot_general(do2, vv, NT, preferred_element_type=F32)
                ds = (pe * (dp - dbar2) * scale).astype(BF16)
                dq2 = jnp.dot(ds, kk, preferred_element_type=F32)
                dq = jnp.where(masks[0], dq2[:SPAN], dq2[SPAN:])
                dq_ref[:, sl] = _rope_transpose(dq, tab_ref[...]).astype(BF16)
                dk = lax.dot_general(ds, q2, TN, preferred_element_type=F32)
                dv = lax.dot_general(pe.astype(BF16), do2, TN, preferred_element_type=F32)

                @pl.when(b > 0)
                def _():
                    dk_prev = carry_ref[:, sl] + dk[:SPAN]
                    dkv_ref[:, sl] = _rope_transpose(dk_prev, tabp_ref[...]).astype(BF16)
                    dkv_ref[:, sl_v] = (carry_ref[:, sl_v] + dv[:SPAN]).astype(BF16)

                carry_ref[:, sl] = dk[SPAN:]
                carry_ref[:, sl_v] = dv[SPAN:]

        @pl.when(b == nb)
        def _():
            for p in range(gw // 128):
                sl = slice(p * 128, (p + 1) * 128)
                sl_v = slice(gw + p * 128, gw + (p + 1) * 128)
                dkv_ref[:, sl] = _rope_transpose(carry_ref[:, sl], tabp_ref[...]).astype(BF16)
                dkv_ref[:, sl_v] = carry_ref[:, sl_v].astype(BF16)

    blk = (SPAN, gw)
    cb = lambda b: jnp.minimum(b, nb - 1)
    cur = lambda t: pl.BlockSpec(blk, lambda r, b: (cb(b), r * 3 + t))
    prev = lambda t: pl.BlockSpec(blk, lambda r, b: (jnp.maximum(cb(b) - 1, 0), r * 3 + t))
    own = pl.BlockSpec(blk, lambda r, b: (cb(b), r))
    tab = pl.BlockSpec((SPAN, ROPE_COLS), lambda r, b: (cb(b), r))
    tab_prev = pl.BlockSpec((SPAN, ROPE_COLS), lambda r, b: (jnp.maximum(b - 1, 0), r))
    dq, dkv = pl.pallas_call(
        body, name=f"attn_bwd_g{grp}",
        out_shape=[jax.ShapeDtypeStruct((l, dil * gw), BF16), jax.ShapeDtypeStruct((l, dil * 2 * gw), BF16)],
        grid=(dil, nb + 1),
        in_specs=[cur(0), prev(1), cur(1), prev(2), cur(2), own, own, own, tab, tab_prev],
        out_specs=[own, pl.BlockSpec((SPAN, 2 * gw), lambda r, b: (jnp.maximum(b - 1, 0), r))],
        scratch_shapes=[pltpu.VMEM((SPAN, 2 * gw), F32)], compiler_params=_params("parallel", "arbitrary"),
    )(qv, qv, qv, qv, qv, view(dmix), view(mixed), view(lse), view(rope), view(rope))
    return dq.reshape(s, gw), dkv.reshape(s, 2 * gw)


def _rope_freq_row():
    half = ROT_DIM // 2
    inv = (ROPE_THETA ** (-np.arange(half, dtype=np.float32) / half)).astype(np.float32)
    row = np.zeros((1, 128), np.float32)
    for head in range(128 // HEAD_DIM):
        row[0, head * HEAD_DIM:head * HEAD_DIM + half] = inv
        row[0, head * HEAD_DIM + half:head * HEAD_DIM + ROT_DIM] = inv
    return jnp.asarray(row)


def _ffn_fwd(x, g_pre, g_post, w_up_t, w_dw, b_dw, w_down):
    h, z = _norm_matmul(x, g_pre, w_up_t, tn=_tile(w_up_t.shape[0]), name="ffn_up")
    act = _ffn_act(z, w_dw, b_dw)
    y, xo = _matmul_resnorm(act, w_down, x, g_post, name="ffn_down")
    return xo, (x, h, z, act, y)


def _ffn_bwd(saved, dxo, g_pre, g_post, w_up_t, w_dw, b_dw, w_down):
    x, h, z, act, y = saved
    f = act.shape[1]
    d = x.shape[1]
    dy, dg_post = _postnorm_bwd(y, g_post, dxo, name="ffn_post_bwd")
    dact = _matmul(dy, w_down, name="ffn_dact", out_dtype=BF16, transposed_w=True)
    d_down = _weight_grad(act, dy, name="ffn_dw_down", out_shape=(1, f, d))
    dug_u, dug_g, db_u, db_g, dwd_u, dwd_g = _ffn_act_bwd(z, dact, w_dw, b_dw)
    dz_u = _conv3_transpose(dug_u, w_dw, 0)
    dz_g = _conv3_transpose(dug_g, w_dw, f)
    dx, dg_pre = _matmul_prenorm_bwd([(dz_u, 0, f, 0), (dz_g, 0, f, f)], w_up_t, x, g_pre, dxo, name="ffn_dx")
    d_up_t = _weight_grad(dz_u, h, name="ffn_dw_up", out_shape=(1, 2 * f, d))
    d_up_t = _weight_grad(dz_g, h, name="ffn_dw_up", out=d_up_t, row0=f)
    grads = dict(w_dw=jnp.concatenate([dwd_u, dwd_g], axis=1), b_dw=jnp.concatenate([db_u, db_g], axis=1),
                 g_pre=dg_pre, g_post=dg_post)
    return dx, grads, d_up_t, d_down


def _local_step(x, pos_col, target, p, tie=None, late_weights=None, exchange=None):
    ng = p["norm_g"]
    row = lambda r: ng[r:r + 1]
    freq = _rope_freq_row()
    rope = _rope_tables(pos_col, freq if tie is None else freq + tie[0:1])
    d = x.shape[1]

    h0, *qkv = _qkv_proj(x, row(0), p["w_qkv_t"], rope)
    os_, ls_ = zip(*[_attn_fwd(qkv[g_], g_, d_) for g_, d_ in enumerate(DILATIONS)])
    y_a, x1, mixed, lse = _mix_wo(os_, ls_, p["w_o_t"], x, row(1))
    if late_weights is not None:
        p = {**p, **late_weights(x1)}
    x2, ffn0 = _ffn_fwd(x1, row(2), row(3), p["w_up_t"][0], p["ffn_w_dw"][0], p["ffn_b_dw"][0], p["w_down"][0])
    h1, ag = _norm_matmul(x2, row(4), p["w_pw1_t"], tn=_tile(p["w_pw1_t"].shape[0]), name="conv_pw1", bias=p["b_pw1"])
    u3, u1 = _conv_mid(ag, p["conv_w_dw"], p["conv_b_dw"], p["ln_g"], p["ln_b"])
    y_c, x3 = _matmul_resnorm(u3, p["w_pw2"], x2, row(5), name="conv_pw2", bias=p["b_pw2"])
    x4, ffn1 = _ffn_fwd(x3, row(6), row(7), p["w_up_t"][1], p["ffn_w_dw"][1], p["ffn_b_dw"][1], p["w_down"][1])
    dx4, loss = _loss_grad(x4, target)

    big = [BF16, BF16]

    def tied(r, *tokens):
        tokens = [t for t in tokens if t is not None]
        return row(r) if not tokens else row(r) + jnp.tile(sum(tokens)[0:1], (1, d // 128))

    dx3, gf1, d_up1, d_down1 = _ffn_bwd(ffn1, dx4, row(6), row(7), p["w_up_t"][1], p["ffn_w_dw"][1], p["ffn_b_dw"][1],
                                        p["w_down"][1])
    t0 = exchange.submit("ffn1", [d_up1, d_down1], big) if exchange else None
    dy_c, dg5, db_pw2 = _postnorm_bwd(y_c, tied(5, t0), dx3, name="conv_post_bwd", with_bias_grad=True)
    du3 = _matmul(dy_c, p["w_pw2"], name="conv_du3", out_dtype=F32, transposed_w=True)
    d_wpw2 = _weight_grad(u3, dy_c, name="conv_dw_pw2", out_shape=(1, u3.shape[1], d))
    du1, d_lng, d_lnb, d_cbdw, d_cwdw = _conv_mid_bwd(ag, u1, du3, p["conv_b_dw"], p["ln_g"], p["ln_b"])
    dag, db_pw1 = _glu_conv_bwd(du1, ag, p["conv_w_dw"])
    dx2, dg4 = _matmul_prenorm_bwd([(dag, 0, dag.shape[1], 0)], p["w_pw1_t"], x2, row(4), dx3, name="conv_dx")
    d_wpw1_t = _weight_grad(dag, h1, name="conv_dw_pw1", out_shape=(1, dag.shape[1], d))
    t0 = exchange.advance(dx2) if exchange else None
    t1 = exchange.submit("conv", [d_wpw1_t, d_wpw2], big) if exchange else None
    dx1, gf0, d_up0, d_down0 = _ffn_bwd(ffn0, dx2, row(2), tied(3, t0, t1), p["w_up_t"][0], p["ffn_w_dw"][0], p["ffn_b_dw"][0],
                                        p["w_down"][0])
    t0 = exchange.advance(dx1) if exchange else None
    t1 = exchange.submit("ffn0", [d_up0, d_down0], big) if exchange else None
    dy_a, dg1 = _postnorm_bwd(y_a, tied(1, t0, t1), dx1, name="attn_post_bwd")
    dmix = _matmul(dy_a, p["w_o_t"], name="attn_dmix", out_dtype=BF16, transposed_w=False)
    d_wo_t = _weight_grad(dy_a, mixed, name="attn_dw_o", out_shape=(1, d, GROUP_WIDTH))
    pieces, d_wqkv_t = [], None
    for g_, d_ in enumerate(DILATIONS):
        if exchange and g_ > 0:
            tok = exchange.advance(dkv)
            if tok is not None:
                rope = rope + jnp.tile(tok[0:1], (1, ROPE_COLS // 128))
        dq, dkv = _attn_bwd(qkv[g_], dmix, mixed, lse, rope, g_, d_)
        for t, (arr, c0) in enumerate(((dq, 0), (dkv, 0), (dkv, GROUP_WIDTH))):
            r0 = (3 * t + g_) * GROUP_WIDTH
            pieces.append((arr, c0, GROUP_WIDTH, r0))
            d_wqkv_t = _weight_grad(arr, h0, name="attn_dw_qkv", a_col0=c0, ka=GROUP_WIDTH, out=d_wqkv_t,
                                    out_shape=(1, p["w_qkv_t"].shape[0], d), row0=r0)
    t0 = exchange.advance(dkv) if exchange else None
    t1 = exchange.submit("attn", [d_wqkv_t, d_wo_t], big) if exchange else None
    grad_x, dg0 = _matmul_prenorm_bwd(pieces, p["w_qkv_t"], x, tied(0, t0, t1), dx1, name="attn_dx")

    grads = dict(
        norm_g=jnp.concatenate([dg0, dg1, gf0["g_pre"], gf0["g_post"], dg4, dg5, gf1["g_pre"], gf1["g_post"]], axis=0),
        w_qkv_t=d_wqkv_t, w_o_t=d_wo_t, w_pw1_t=d_wpw1_t, b_pw1=db_pw1,
        conv_w_dw=d_cwdw[:CONV_KERNEL], conv_b_dw=d_cbdw, ln_g=d_lng, ln_b=d_lnb, w_pw2=d_wpw2, b_pw2=db_pw2,
        w_up_t=[d_up0, d_up1], ffn_w_dw=jnp.stack([gf0["w_dw"], gf1["w_dw"]]),
        ffn_b_dw=jnp.concatenate([gf0["b_dw"], gf1["b_dw"]], axis=0), w_down=[d_down0, d_down1])
    return loss, grad_x, grads


SMALL_AXIS = dict(norm_g=2, conv_b_pw1=1, conv_w_dw=2, conv_b_dw=1, conv_ln_g=1, conv_ln_b=1, conv_b_pw2=1, ffn_w_dw=2)
SMALL = tuple(SMALL_AXIS)
MATMUL_WEIGHTS = dict(attn_w_qkv=True, conv_w_pw1=True, ffn_w_up=True, conv_w_pw2=False, ffn_w_down=False)


def _pack(arrays, cols, row_multiple):
    flat = jnp.concatenate([a.reshape(-1) for a in arrays])
    rows = -(-flat.shape[0] // cols)
    rows = -(-rows // row_multiple) * row_multiple
    return jnp.pad(flat, (0, rows * cols - flat.shape[0])).reshape(rows, cols)


def _unpack(packed, shapes):
    flat = packed.reshape(packed.shape[:-2] + (-1,))
    out, off = [], 0
    for shp in shapes:
        n = math.prod(shp)
        out.append(flat[..., off:off + n].reshape(packed.shape[:-2] + tuple(shp)))
        off += n
    return out


def _join_shards(stacked, axis):
    moved = jnp.moveaxis(stacked, 0, axis)
    shp = moved.shape
    return moved.reshape(shp[:axis] + (shp[axis] * shp[axis + 1],) + shp[axis + 2:])


def _split_shards(whole, axis):
    shp = whole.shape
    cut = whole.reshape(shp[:axis] + (N_DEV, shp[axis] // N_DEV) + shp[axis + 1:])
    return jnp.moveaxis(cut, axis, 0)


def _row_shard(w, transposed):
    t = jnp.swapaxes(w, 1, 2) if transposed else w
    return t.astype(BF16).reshape(-1, t.shape[-1])


def kernel(x, positions, norm_g, attn_w_qkv, attn_w_o, conv_w_pw1, conv_b_pw1, conv_w_dw, conv_b_dw, conv_ln_g, conv_ln_b, conv_w_pw2, conv_b_pw2, ffn_w_up, ffn_w_dw, ffn_b_dw, ffn_w_down, loss_target, m_norm_g, m_attn_w_qkv, m_attn_w_o, m_conv_w_pw1, m_conv_b_pw1, m_conv_w_dw, m_conv_b_dw, m_conv_ln_g, m_conv_ln_b, m_conv_w_pw2, m_conv_b_pw2, m_ffn_w_up, m_ffn_w_dw, m_ffn_b_dw, m_ffn_w_down, v_norm_g, v_attn_w_qkv, v_attn_w_o, v_conv_w_pw1, v_conv_b_pw1, v_conv_w_dw, v_conv_b_dw, v_conv_ln_g, v_conv_ln_b, v_conv_w_pw2, v_conv_b_pw2, v_ffn_w_up, v_ffn_w_dw, v_ffn_b_dw, v_ffn_w_down):
    w = dict(norm_g=norm_g, attn_w_qkv=attn_w_qkv, attn_w_o=attn_w_o, conv_w_pw1=conv_w_pw1, conv_b_pw1=conv_b_pw1,
             conv_w_dw=conv_w_dw, conv_b_dw=conv_b_dw, conv_ln_g=conv_ln_g, conv_ln_b=conv_ln_b, conv_w_pw2=conv_w_pw2,
             conv_b_pw2=conv_b_pw2, ffn_w_up=ffn_w_up, ffn_w_dw=ffn_w_dw, ffn_w_down=ffn_w_down)
    m = dict(norm_g=m_norm_g, attn_w_qkv=m_attn_w_qkv, attn_w_o=m_attn_w_o, conv_w_pw1=m_conv_w_pw1, conv_b_pw1=m_conv_b_pw1,
             conv_w_dw=m_conv_w_dw, conv_b_dw=m_conv_b_dw, conv_ln_g=m_conv_ln_g, conv_ln_b=m_conv_ln_b, conv_w_pw2=m_conv_w_pw2,
             conv_b_pw2=m_conv_b_pw2, ffn_w_up=m_ffn_w_up, ffn_w_dw=m_ffn_w_dw, ffn_w_down=m_ffn_w_down)
    v = dict(norm_g=v_norm_g, attn_w_qkv=v_attn_w_qkv, attn_w_o=v_attn_w_o, conv_w_pw1=v_conv_w_pw1, conv_b_pw1=v_conv_b_pw1,
             conv_w_dw=v_conv_w_dw, conv_b_dw=v_conv_b_dw, conv_ln_g=v_conv_ln_g, conv_ln_b=v_conv_ln_b, conv_w_pw2=v_conv_w_pw2,
             conv_b_pw2=v_conv_b_pw2, ffn_w_up=v_ffn_w_up, ffn_w_dw=v_ffn_w_dw, ffn_w_down=v_ffn_w_down)
    d = x.shape[-1]

    w_qkv_t = _all_gather(_row_shard(attn_w_qkv, True), "gather_w_qkv").reshape(-1, d)
    w_o_t = _all_gather(_row_shard(attn_w_o, True), "gather_w_o").reshape(d, -1)
    small = _all_gather(_pack([w[n] for n in SMALL], 128, 8), "gather_small_weights")
    sm = {n: _join_shards(stacked, SMALL_AXIS[n])
          for n, stacked in zip(SMALL, _unpack(small, [w[n].shape for n in SMALL]))}
    late = {n: t for n, t in MATMUL_WEIGHTS.items() if n != "attn_w_qkv"}
    shares = [_row_shard(w[n], t) for n, t in late.items()]
    rows = [s_.shape[0] for s_ in shares]
    late_share = jnp.concatenate(shares, axis=0)
    send_sems, recv_sems, share_thru, land_thru, tie = _gather_start(late_share)
    me = 4 * lax.axis_index("x") + 2 * lax.axis_index("y") + lax.axis_index("c")

    def late_weights(after):
        big = _gather_wait(send_sems, recv_sems, share_thru, land_thru, after)
        big = lax.dynamic_update_slice(big, late_share[None], (me, 0, 0))
        whole, r0 = {}, 0
        for n, nr in zip(late, rows):
            layers = w[n].shape[0]
            seg = big[:, r0:r0 + nr].reshape(N_DEV, layers, nr // layers, d)
            whole[n] = [seg[:, l_].reshape(-1, d) for l_ in range(layers)]
            r0 += nr
        return dict(w_pw1_t=whole["conv_w_pw1"][0], w_pw2=whole["conv_w_pw2"][0], w_up_t=whole["ffn_w_up"],
                    w_down=whole["ffn_w_down"])

    p = dict(norm_g=sm["norm_g"].reshape(-1, d), w_qkv_t=w_qkv_t, w_o_t=w_o_t, b_pw1=sm["conv_b_pw1"],
             conv_w_dw=sm["conv_w_dw"][0], conv_b_dw=sm["conv_b_dw"], ln_g=sm["conv_ln_g"], ln_b=sm["conv_ln_b"],
             b_pw2=sm["conv_b_pw2"], ffn_w_dw=sm["ffn_w_dw"], ffn_b_dw=[ffn_b_dw[0:1], ffn_b_dw[1:2]])

    exchange = _GradExchange()
    loss, grad_x, g = _local_step(x[0], positions.reshape(-1, 1), loss_target[0], p, tie, late_weights, exchange)
    loss = lax.psum(loss[0, 0], ("x", "y", "c"))
    gsmall = dict(norm_g=g["norm_g"].reshape(norm_g.shape[0], 4, -1), conv_b_pw1=g["b_pw1"], conv_w_dw=g["conv_w_dw"][None],
                  conv_b_dw=g["conv_b_dw"], conv_ln_g=g["ln_g"], conv_ln_b=g["ln_b"], conv_b_pw2=g["b_pw2"], ffn_w_dw=g["ffn_w_dw"])
    small_contrib = jnp.concatenate([_split_shards(gsmall[n], SMALL_AXIS[n]).reshape(N_DEV, -1) for n in SMALL], axis=1)
    srows = small.shape[1]
    small_contrib = jnp.pad(small_contrib, ((0, 0), (0, srows * 128 - small_contrib.shape[1]))).reshape(1, N_DEV, srows, 128)
    exchange.advance(grad_x)
    small_sums = _rs_chips([_rs_pair_add(small_contrib, _rs_sibling([small_contrib])[0], exchange.core, F32)])[0]

    outs = {}

    def update(n, reduced):
        gsum = jnp.swapaxes(reduced, 1, 2) if n == "attn_w_o" or MATMUL_WEIGHTS.get(n) else reduced
        outs[n] = (gsum, *_adamw(gsum, w[n], m[n], v[n], "adamw"))

    (s_up1, s_down1), (s_pw1, s_pw2), (s_up0, s_down0) = exchange.results()[:3]
    update("conv_w_pw1", s_pw1)
    update("conv_w_pw2", s_pw2)
    update("ffn_w_up", jnp.concatenate([s_up0, s_up1], axis=0))
    update("ffn_w_down", jnp.concatenate([s_down0, s_down1], axis=0))
    sshapes = [w[n].shape for n in SMALL]
    souts = _sum_adamw(small_sums[0], *[_pack([t[n] for n in SMALL], 128, 8) for t in (w, m, v)], name="sum_adamw_small")
    for n, vals in zip(SMALL, zip(*[_unpack(o, sshapes) for o in souts])):
        outs[n] = vals
    bparts = _all_gather(_pack([g["ffn_b_dw"]], 128, 8), "gather_bias_grads")
    bouts = _sum_adamw(bparts, *[_pack([t], 128, 8) for t in (ffn_b_dw, m_ffn_b_dw, v_ffn_b_dw)], name="sum_adamw_bias")
    outs["ffn_b_dw"] = tuple(_unpack(o, [ffn_b_dw.shape])[0] for o in bouts)
    done = [outs[n][1][0, :8, :128] for n in ("conv_w_pw1", "conv_w_pw2", "ffn_w_up", "ffn_w_down")]
    exchange.advance(sum(done) + bouts[1][:8] + souts[1][:8])
    s_qkv, s_wo = exchange.results()[3]
    update("attn_w_qkv", s_qkv)
    update("attn_w_o", s_wo)

    order = ("norm_g", "attn_w_qkv", "attn_w_o", "conv_w_pw1", "conv_b_pw1", "conv_w_dw", "conv_b_dw", "conv_ln_g",
             "conv_ln_b", "conv_w_pw2", "conv_b_pw2", "ffn_w_up", "ffn_w_dw", "ffn_b_dw", "ffn_w_down")
    return (loss, grad_x[None], *[outs[n][0] for n in order], *[outs[n][1] for n in order],
            *[outs[n][2] for n in order], *[outs[n][3] for n in order])
```

```python
import math

import numpy as np
import jax
import jax.numpy as jnp
from jax import lax
from jax.experimental import pallas as pl
from jax.experimental.pallas import tpu as pltpu

F32 = jnp.float32
BF16 = jnp.bfloat16
EPS = 1e-6
N_DEV = 8
HEAD_DIM = 64
GROUP_WIDTH = 512
DILATIONS = (1, 4, 16)
SPAN = 128
ROT_DIM = 16
ROPE_THETA = 500000.0
CONV_KERNEL = 31
CONV_HALO = 32
FFN_CONV = 3
ADAM_LR, ADAM_B1, ADAM_B2, ADAM_EPS, ADAM_WD, ADAM_STEP = 0.001, 0.9, 0.999, 1e-08, 0.01, 10
VMEM_LIMIT_BYTES = 56 * 1024 * 1024
MESH = pl.DeviceIdType.MESH
ANY = pl.BlockSpec(memory_space=pl.ANY)
NT = (((1,), (1,)), ((), ()))
TN = (((0,), (0,)), ((), ()))


def _params(*sem):
    return pltpu.CompilerParams(dimension_semantics=sem, vmem_limit_bytes=VMEM_LIMIT_BYTES)


def _sigmoid(v):
    return pl.reciprocal(1.0 + jnp.exp(-v), approx=True)


def _full(shape):
    return pl.BlockSpec(shape, lambda *_: (0,) * len(shape))


def _rows(tm, width):
    return pl.BlockSpec((tm, width), lambda i, *_: (i, 0))


def _tile(n, *multiples_of):
    for t in (1408, 1024, 512, 384, 256, 128):
        if n % t == 0 and all(o % t == 0 for o in multiples_of):
            return t
    raise ValueError((n, multiples_of))


def _all_gather(shards, name):
    n = len(shards)

    def body(*refs):
        x_refs, out_refs, (send_sems, recv_sems, local_sems) = refs[:n], refs[n:2 * n], refs[2 * n:]
        x, y, c = lax.axis_index("x"), lax.axis_index("y"), lax.axis_index("c")
        me, sibling = (x, y, c), (x, y, 1 - c)
        chips = [(1 - x, y), (x, 1 - y), (1 - x, 1 - y)]

        def rows(w, px, py, pc):
            return out_refs[w].at[4 * px + 2 * py + pc]

        def copy(w, k, block, to, src=None):
            return pltpu.make_async_remote_copy(
                src_ref=rows(w, *block) if src is None else src, dst_ref=rows(w, *block),
                send_sem=send_sems.at[7 * w + k], recv_sem=recv_sems.at[7 * w + k], device_id=to, device_id_type=MESH)

        every = range(n)
        mine = [pltpu.make_async_copy(x_refs[w], rows(w, *me), local_sems.at[w]) for w in every]
        first = [copy(w, 0, me, sibling, src=x_refs[w]) for w in every]
        first += [copy(w, 1 + j, me, (*chip, c), src=x_refs[w]) for w in every for j, chip in enumerate(chips)]
        for cp in mine + first:
            cp.start()
        passed = []
        for j, chip in enumerate(chips):
            for w in every:
                copy(w, 1 + j, (*chip, c), me).wait_recv()
                passed.append(copy(w, 4 + j, (*chip, c), sibling))
                passed[-1].start()
        for w in every:
            copy(w, 0, sibling, me).wait_recv()
            for j, chip in enumerate(chips):
                copy(w, 4 + j, (*chip, 1 - c), me).wait_recv()
        for cp in first + passed:
            cp.wait_send()
        for cp in mine:
            cp.wait()

    return pl.pallas_call(
        body, name=name, out_shape=[jax.ShapeDtypeStruct((N_DEV,) + s_.shape, s_.dtype) for s_ in shards],
        in_specs=[ANY] * n, out_specs=[ANY] * n,
        scratch_shapes=[pltpu.SemaphoreType.DMA((7 * n,)), pltpu.SemaphoreType.DMA((7 * n,)), pltpu.SemaphoreType.DMA((n,))],
    )(*shards)


HBM = pl.BlockSpec(memory_space=pltpu.HBM)
SEM = pl.BlockSpec(memory_space=pltpu.SEMAPHORE)
SIDE_EFFECT = pltpu.CompilerParams(has_side_effects=pltpu.SideEffectType.DATAFLOW_SIDE_EFFECTING)


def _gather_start(shard):
    r, c_ = shard.shape

    def body(x_ref, land_ref, send_sems, recv_sems, x_thru, land_thru, token):
        x, y, c = lax.axis_index("x"), lax.axis_index("y"), lax.axis_index("c")
        me = 4 * x + 2 * y + c
        for k in range(1, N_DEV):
            peer = (1 - x if k & 4 else x, 1 - y if k & 2 else y, 1 - c if k & 1 else c)
            pltpu.make_async_remote_copy(src_ref=x_ref, dst_ref=land_ref.at[me], send_sem=send_sems.at[k - 1],
                                         recv_sem=recv_sems.at[k - 1], device_id=peer, device_id_type=MESH).start()
        token[...] = jnp.zeros_like(token)

    land = pltpu.with_memory_space_constraint(lax.empty((N_DEV, r, c_), shard.dtype), pltpu.HBM)
    return pl.pallas_call(
        body, name="gather_late_weights_start",
        out_shape=(pltpu.SemaphoreType.DMA((N_DEV - 1,)), pltpu.SemaphoreType.DMA((N_DEV - 1,)),
                   pltpu.HBM(shard.shape, shard.dtype), pltpu.HBM((N_DEV, r, c_), shard.dtype),
                   jax.ShapeDtypeStruct((8, 128), F32)),
        in_specs=(HBM, HBM), out_specs=(SEM, SEM, HBM, HBM, pl.BlockSpec(memory_space=pltpu.VMEM)),
        input_output_aliases={0: 2, 1: 3}, compiler_params=SIDE_EFFECT,
    )(pltpu.with_memory_space_constraint(shard, pltpu.HBM), land)


def _gather_wait(send_sems, recv_sems, shard_thru, land_thru, after):
    def body(x_ref, land_ref, send_sems, recv_sems, after_ref, x_dead, got_ref):
        x, y, c = lax.axis_index("x"), lax.axis_index("y"), lax.axis_index("c")
        for k in range(N_DEV - 1):
            copy = pltpu.make_async_remote_copy(src_ref=x_ref, dst_ref=land_ref.at[0], send_sem=send_sems.at[k],
                                                recv_sem=recv_sems.at[k], device_id=(x, y, c), device_id_type=MESH)
            copy.wait_send()
            copy.wait_recv()

    return pl.pallas_call(
        body, name="gather_late_weights_wait",
        out_shape=(pltpu.HBM(shard_thru.shape, shard_thru.dtype), pltpu.HBM(land_thru.shape, land_thru.dtype)),
        in_specs=(HBM, HBM, SEM, SEM, ANY), out_specs=(HBM, HBM), input_output_aliases={0: 0, 1: 1},
        compiler_params=SIDE_EFFECT,
    )(shard_thru, land_thru, send_sems, recv_sems, after)[1]


def _hbm(a):
    return pltpu.with_memory_space_constraint(a, pltpu.HBM)


def _exchange_start(name, arrays, lands, plan, ncopies):
    n = len(arrays)

    def body(*refs):
        send_sems, recv_sems, token = refs[2 * n], refs[2 * n + 1], refs[-1]
        x, y, c = lax.axis_index("x"), lax.axis_index("y"), lax.axis_index("c")
        for k, (src, dst, peer) in enumerate(plan(x, y, c, refs[:n], refs[n:2 * n])):
            pltpu.make_async_remote_copy(src_ref=src, dst_ref=dst, send_sem=send_sems.at[k], recv_sem=recv_sems.at[k],
                                         device_id=peer, device_id_type=MESH).start()
        token[...] = jnp.zeros_like(token)

    both = list(arrays) + list(lands)
    outs = pl.pallas_call(
        body, name=name,
        out_shape=(pltpu.SemaphoreType.DMA((ncopies,)), pltpu.SemaphoreType.DMA((ncopies,)),
                   *[pltpu.HBM(a.shape, a.dtype) for a in both], jax.ShapeDtypeStruct((8, 128), F32)),
        in_specs=(HBM,) * (2 * n), out_specs=(SEM, SEM) + (HBM,) * (2 * n) + (pl.BlockSpec(memory_space=pltpu.VMEM),),
        input_output_aliases={i: 2 + i for i in range(2 * n)}, compiler_params=SIDE_EFFECT,
    )(*[_hbm(a) for a in both])
    return outs[0], outs[1], list(outs[2:2 + n]), list(outs[2 + n:2 + 2 * n]), outs[-1]


def _exchange_wait(name, send_sems, recv_sems, arrays, lands, plan, after):
    n = len(arrays)

    def body(*refs):
        send_sems, recv_sems = refs[2 * n], refs[2 * n + 1]
        x, y, c = lax.axis_index("x"), lax.axis_index("y"), lax.axis_index("c")
        for k, (src, dst, peer) in enumerate(plan(x, y, c, refs[:n], refs[n:2 * n])):
            copy = pltpu.make_async_remote_copy(src_ref=src, dst_ref=dst, send_sem=send_sems.at[k], recv_sem=recv_sems.at[k],
                                                device_id=peer, device_id_type=MESH)
            copy.wait_send()
            copy.wait_recv()

    both = list(arrays) + list(lands)
    outs = pl.pallas_call(
        body, name=name, out_shape=tuple(pltpu.HBM(a.shape, a.dtype) for a in both),
        in_specs=(HBM,) * (2 * n) + (SEM, SEM, ANY), out_specs=(HBM,) * (2 * n),
        input_output_aliases={i: i for i in range(2 * n)}, compiler_params=SIDE_EFFECT,
    )(*both, send_sems, recv_sems, after)
    return list(outs[:n]), list(outs[n:])


def _sibling_plan(x, y, c, g_refs, land_refs):
    return [(g.at[:, 2 * q + (1 - c)], o.at[:, q], (x, y, 1 - c)) for g, o in zip(g_refs, land_refs) for q in range(4)]


def _chips_plan(x, y, c, p_refs, land_refs):
    chips = [(1 - x, y), (x, 1 - y), (1 - x, 1 - y)]
    return [(p_.at[:, 2 * qx + qy], o.at[:, 2 * x + y], (qx, qy, c)) for p_, o in zip(p_refs, land_refs) for qx, qy in chips]


class _GradExchange:
    def __init__(self):
        self.core = lax.axis_index("c").astype(jnp.int32).reshape(1)
        self.chip = 2 * lax.axis_index("x") + lax.axis_index("y")
        self.groups = []

    def submit(self, tag, arrays, dtypes):
        arrays = [a.reshape(a.shape[0], N_DEV, a.shape[1] // N_DEV, a.shape[2]) for a in arrays]
        lands = [lax.empty((a.shape[0], 4) + a.shape[2:], a.dtype) for a in arrays]
        send, recv, arrays, lands, token = _exchange_start(f"rs_pair_start_{tag}", arrays, lands, _sibling_plan, 4 * len(arrays))
        self.groups.append(dict(tag=tag, stage=1, sems=(send, recv), arrays=arrays, lands=lands, dtypes=dtypes))
        return token

    def advance(self, after):
        token = None
        for g in self.groups:
            if g["stage"] == 1:
                arrays, got = _exchange_wait(f"rs_pair_wait_{g['tag']}", *g["sems"], g["arrays"], g["lands"], _sibling_plan, after)
                parts = [_rs_pair_add(a, b, self.core, dt) for a, b, dt in zip(arrays, got, g["dtypes"])]
                lands = [lax.empty(p_.shape, p_.dtype) for p_ in parts]
                send, recv, parts, lands, tok = _exchange_start(f"rs_chip_start_{g['tag']}", parts, lands, _chips_plan, 3 * len(parts))
                g.update(stage=2, sems=(send, recv), arrays=parts, lands=lands)
                token = tok if token is None else token + tok
            elif g["stage"] == 2:
                parts, lands = _exchange_wait(f"rs_chip_wait_{g['tag']}", *g["sems"], g["arrays"], g["lands"], _chips_plan, after)
                sums = []
                for p_, land in zip(parts, lands):
                    l, _, r, c_ = p_.shape
                    own = lax.dynamic_slice(p_, (0, self.chip, 0, 0), (l, 1, r, c_))
                    sums.append(_sum_parts(lax.dynamic_update_slice(land, own, (0, self.chip, 0, 0)), "sum_chips"))
                g.update(stage=3, sums=sums)
        return token

    def results(self):
        return [g.get("sums") for g in self.groups]


def _with_rows(g, n):
    return jax.ShapeDtypeStruct((g.shape[0], n) + tuple(g.shape[2:]), g.dtype)


def _rs_sibling(gs):
    n = len(gs)

    def body(*refs):
        g_refs, o_refs, (send_sems, recv_sems) = refs[:n], refs[n:2 * n], refs[2 * n:]
        x, y, c = lax.axis_index("x"), lax.axis_index("y"), lax.axis_index("c")
        copies = [pltpu.make_async_remote_copy(
            src_ref=g_refs[w].at[:, 2 * q + (1 - c)], dst_ref=o_refs[w].at[:, q], send_sem=send_sems.at[4 * w + q],
            recv_sem=recv_sems.at[4 * w + q], device_id=(x, y, 1 - c), device_id_type=MESH)
            for w in range(n) for q in range(4)]
        for cp in copies:
            cp.start()
        for cp in copies:
            cp.wait_recv()
        for cp in copies:
            cp.wait_send()

    return pl.pallas_call(
        body, name="rs_sibling", out_shape=[_with_rows(g, 4) for g in gs],
        in_specs=[ANY] * n, out_specs=[ANY] * n,
        scratch_shapes=[pltpu.SemaphoreType.DMA((4 * n,)), pltpu.SemaphoreType.DMA((4 * n,))],
    )(*gs)


def _rs_pair_add(g, got, core, out_dtype):
    l, _, r, c_ = g.shape

    def body(core_ref, g_ref, got_ref, o_ref):
        o_ref[...] = (g_ref[...].astype(F32) + got_ref[...].astype(F32)).astype(out_dtype)

    blk = (None, None, r, c_)
    return pl.pallas_call(
        body, name="rs_pair_add", out_shape=jax.ShapeDtypeStruct((l, 4, r, c_), out_dtype),
        grid_spec=pltpu.PrefetchScalarGridSpec(
            num_scalar_prefetch=1, grid=(l, 4),
            in_specs=[pl.BlockSpec(blk, lambda i, q, core_ref: (i, 2 * q + core_ref[0], 0, 0)),
                      pl.BlockSpec(blk, lambda i, q, core_ref: (i, q, 0, 0))],
            out_specs=pl.BlockSpec(blk, lambda i, q, core_ref: (i, q, 0, 0))),
        compiler_params=_params("parallel", "parallel"),
    )(core, g, got)


def _rs_chips(parts):
    n = len(parts)

    def body(*refs):
        p_refs, o_refs, (send_sems, recv_sems, local_sems) = refs[:n], refs[n:2 * n], refs[2 * n:]
        x, y, c = lax.axis_index("x"), lax.axis_index("y"), lax.axis_index("c")
        my_chip = 2 * x + y
        chips = [(1 - x, y), (x, 1 - y), (1 - x, 1 - y)]
        local = [pltpu.make_async_copy(p_refs[w].at[:, my_chip], o_refs[w].at[:, my_chip], local_sems.at[w]) for w in range(n)]
        for cp in local:
            cp.start()
        copies = [pltpu.make_async_remote_copy(
            src_ref=p_refs[w].at[:, 2 * qx + qy], dst_ref=o_refs[w].at[:, my_chip], send_sem=send_sems.at[3 * w + k],
            recv_sem=recv_sems.at[3 * w + k], device_id=(qx, qy, c), device_id_type=MESH)
            for w in range(n) for k, (qx, qy) in enumerate(chips)]
        for cp in copies:
            cp.start()
        for cp in copies:
            cp.wait_recv()
        for cp in copies:
            cp.wait_send()
        for cp in local:
            cp.wait()

    return pl.pallas_call(
        body, name="rs_chips", out_shape=[jax.ShapeDtypeStruct(p.shape, p.dtype) for p in parts],
        in_specs=[ANY] * n, out_specs=[ANY] * n,
        scratch_shapes=[pltpu.SemaphoreType.DMA((3 * n,)), pltpu.SemaphoreType.DMA((3 * n,)), pltpu.SemaphoreType.DMA((n,))],
    )(*parts)


def _sum_parts(parts, name):
    l, n, r, c_ = parts.shape

    def body(p_ref, o_ref):
        g = p_ref[0].astype(F32)
        for s in range(1, n):
            g = g + p_ref[s].astype(F32)
        o_ref[...] = g

    return pl.pallas_call(
        body, name=name, out_shape=jax.ShapeDtypeStruct((l, r, c_), F32), grid=(l,),
        in_specs=[pl.BlockSpec((None, n, r, c_), lambda i: (i, 0, 0, 0))],
        out_specs=pl.BlockSpec((None, r, c_), lambda i: (i, 0, 0)), compiler_params=_params("parallel"),
    )(parts)


def _adamw_math(w, g, m, v):
    m = ADAM_B1 * m + (1.0 - ADAM_B1) * g
    v = ADAM_B2 * v + (1.0 - ADAM_B2) * (g * g)
    m_hat = m / (1.0 - ADAM_B1 ** ADAM_STEP)
    v_hat = v / (1.0 - ADAM_B2 ** ADAM_STEP)
    delta = -ADAM_LR * (m_hat / (jnp.sqrt(v_hat) + ADAM_EPS) + ADAM_WD * w)
    return delta, m, v


def _adamw(g, w, m, v, name):
    l, k, n = w.shape
    tk = 256 if k % 256 == 0 else k

    def body(g_ref, w_ref, m_ref, v_ref, d_ref, nm_ref, nv_ref):
        d_ref[...], nm_ref[...], nv_ref[...] = _adamw_math(w_ref[...], g_ref[...], m_ref[...], v_ref[...])

    spec = pl.BlockSpec((None, tk, n), lambda i, j: (i, j, 0))
    return pl.pallas_call(
        body, name=name, out_shape=[jax.ShapeDtypeStruct((l, k, n), F32)] * 3, grid=(l, k // tk),
        in_specs=[spec] * 4, out_specs=[spec] * 3, compiler_params=_params("parallel", "parallel"),
    )(g, w, m, v)


def _sum_adamw(parts, w, m, v, name):
    n, r, c_ = parts.shape

    def body(p_ref, w_ref, m_ref, v_ref, g_ref, d_ref, nm_ref, nv_ref):
        g = p_ref[0]
        for s in range(1, n):
            g = g + p_ref[s]
        g_ref[...] = g
        d_ref[...], nm_ref[...], nv_ref[...] = _adamw_math(w_ref[...], g, m_ref[...], v_ref[...])

    return pl.pallas_call(
        body, name=name, out_shape=[jax.ShapeDtypeStruct((r, c_), F32)] * 4, grid=(1,),
        in_specs=[_full((n, r, c_))] + [_full((r, c_))] * 3, out_specs=[_full((r, c_))] * 4,
        compiler_params=_params("arbitrary"),
    )(parts, w, m, v)


def _rope_tables(pos_col, freq_row):
    s = pos_col.shape[0]
    tm = min(1024, s)

    def body(p_ref, f_ref, o_ref):
        ang = p_ref[...].astype(F32) * f_ref[...]
        lane = lax.broadcasted_iota(jnp.int32, ang.shape, 1) & (HEAD_DIM - 1)
        cs, sn = jnp.cos(ang), jnp.sin(ang)
        o_ref[:, 0:128] = jnp.where(lane < ROT_DIM, cs, 1.0)
        o_ref[:, 128:256] = jnp.where((lane >= ROT_DIM // 2) & (lane < ROT_DIM), sn, 0.0)
        o_ref[:, 256:384] = jnp.where(lane < ROT_DIM // 2, -sn, 0.0)

    return pl.pallas_call(
        body, name="rope_tables", out_shape=jax.ShapeDtypeStruct((s, ROPE_COLS), F32), grid=(s // tm,),
        in_specs=[pl.BlockSpec((tm, 1), lambda i: (i, 0)), _full((1, 128))],
        out_specs=_rows(tm, ROPE_COLS), compiler_params=_params("parallel"),
    )(pos_col, freq_row)


ROPE_COLS = 3 * 128


def _rope_parts(tab, reps=1):
    return [jnp.tile(tab[:, k * 128:(k + 1) * 128], (1, reps)) if reps > 1 else tab[:, k * 128:(k + 1) * 128] for k in range(3)]


def _rope_apply(t, tab):
    w = t.shape[1]
    cos, sin_up, sin_dn = _rope_parts(tab, w // 128)
    return t * cos + pltpu.roll(t, 8, 1) * sin_up + pltpu.roll(t, w - 8, 1) * sin_dn


def _rope_transpose(dr, tab):
    w = dr.shape[1]
    cos, sin_up, sin_dn = _rope_parts(tab, w // 128)
    return dr * cos + pltpu.roll(dr * sin_up, w - 8, 1) + pltpu.roll(dr * sin_dn, 8, 1)


def _norm_matmul(x, g, wt, *, tn, name, bias=None, tm=1024):
    s, d = x.shape
    n = wt.shape[0]
    tm = min(tm, s)

    def body(*refs):
        x_ref, g_ref, w_ref = refs[:3]
        b_ref = refs[3] if bias is not None else None
        h_ref, o_ref = refs[-2:]

        @pl.when(pl.program_id(1) == 0)
        def _():
            xv = x_ref[...]
            r = lax.rsqrt(jnp.mean(xv * xv, axis=-1, keepdims=True) + EPS)
            h_ref[...] = (xv * r * g_ref[...]).astype(BF16)

        acc = lax.dot_general(h_ref[...], w_ref[...], NT, preferred_element_type=F32)
        if b_ref is not None:
            acc = acc + b_ref[...]
        o_ref[...] = acc.astype(BF16)

    in_specs = [_rows(tm, d), _full((1, d)), pl.BlockSpec((tn, d), lambda i, j: (j, 0))]
    args = [x, g, wt]
    if bias is not None:
        in_specs.append(pl.BlockSpec((1, tn), lambda i, j: (0, j)))
        args.append(bias)
    return pl.pallas_call(
        body, name=name,
        out_shape=[jax.ShapeDtypeStruct((s, d), BF16), jax.ShapeDtypeStruct((s, n), BF16)],
        grid=(s // tm, n // tn), in_specs=in_specs,
        out_specs=[_rows(tm, d), pl.BlockSpec((tm, tn), lambda i, j: (i, j))],
        compiler_params=_params("parallel", "arbitrary"),
    )(*args)


def _class_major(tm, dil):
    p = np.zeros((tm, tm), np.float32)
    per = tm // dil
    for r in range(dil):
        for j in range(per):
            p[r * per + j, j * dil + r] = 1.0
    return jnp.asarray(p, dtype=BF16)


def _qkv_proj(x, g, wt, rope, tm=512):
    s, d = x.shape
    n = wt.shape[0]
    gw3 = 3 * GROUP_WIDTH
    tm = min(tm, s)
    assert n == 3 * gw3

    def body(x_ref, g_ref, w_ref, tab_ref, p1_ref, p2_ref, h_ref, o0_ref, o1_ref, o2_ref):
        j = pl.program_id(1)

        @pl.when(j == 0)
        def _():
            xv = x_ref[...]
            r = lax.rsqrt(jnp.mean(xv * xv, axis=-1, keepdims=True) + EPS)
            h_ref[...] = (xv * r * g_ref[...]).astype(BF16)

        acc = lax.dot_general(h_ref[...], w_ref[...], NT, preferred_element_type=F32)

        def store(y):
            yb = y.astype(BF16)
            o0_ref[:, pl.ds(pl.multiple_of(j * GROUP_WIDTH, GROUP_WIDTH), GROUP_WIDTH)] = yb[:, :GROUP_WIDTH]
            for grp, o_ref, p_ref in ((1, o1_ref, p1_ref), (2, o2_ref, p2_ref)):
                dil = DILATIONS[grp]
                per = tm // dil
                yp = jnp.dot(p_ref[...], yb[:, grp * GROUP_WIDTH:(grp + 1) * GROUP_WIDTH],
                             preferred_element_type=F32).astype(BF16)
                for r in range(dil):
                    col = pl.multiple_of(r * gw3 + j * GROUP_WIDTH, GROUP_WIDTH)
                    o_ref[:, pl.ds(col, GROUP_WIDTH)] = yp[r * per:(r + 1) * per, :]

        @pl.when(j < 2)
        def _():
            store(_rope_apply(acc, tab_ref[...]))

        @pl.when(j == 2)
        def _():
            store(acc)

    outs = [jax.ShapeDtypeStruct((s, d), BF16)] + [jax.ShapeDtypeStruct((s // dl, dl * gw3), BF16) for dl in DILATIONS]
    out_specs = [_rows(tm, d)] + [_rows(tm // dl, dl * gw3) for dl in DILATIONS]
    return pl.pallas_call(
        body, name="attn_qkv", out_shape=outs, grid=(s // tm, 3),
        in_specs=[_rows(tm, d), _full((1, d)), pl.BlockSpec((gw3, d), lambda i, j: (j, 0)), _rows(tm, ROPE_COLS)]
        + [_full((tm, tm))] * 2,
        out_specs=out_specs, compiler_params=_params("parallel", "arbitrary"),
    )(x, g, wt, rope, _class_major(tm, DILATIONS[1]), _class_major(tm, DILATIONS[2]))


def _head_masks(rows=SPAN):
    lane = lax.broadcasted_iota(jnp.int32, (rows, 128), 1)
    masks = [lane < HEAD_DIM, lane >= HEAD_DIM]
    lane1 = lax.broadcasted_iota(jnp.int32, (1, 128), 1)
    keep = [jnp.where(lane1 < HEAD_DIM, 1.0, 0.0).astype(BF16), jnp.where(lane1 >= HEAD_DIM, 1.0, 0.0).astype(BF16)]
    return masks, keep


def _band_mask(b):
    row = lax.broadcasted_iota(jnp.int32, (2 * SPAN, 2 * SPAN), 0) & (SPAN - 1)
    col = lax.broadcasted_iota(jnp.int32, (2 * SPAN, 2 * SPAN), 1)
    no_prev = jnp.where(b > 0, 0, 4 * SPAN)
    return ((col < SPAN) & (col >= row + no_prev)) | ((col >= SPAN) & (col - SPAN <= row))


def _attn_fwd(qv, grp, dil):
    l = qv.shape[0]
    s = l * dil
    nb = l // SPAN
    nq = next(n for n in (4, 2, 1) if nb % n == 0)

    def body(q_ref, kp_ref, kc_ref, vp_ref, vc_ref, o_ref, l_ref):
        b = pl.program_id(1)
        masks, keep = _head_masks()
        for qb in range(nq):
            valid = _band_mask(b * nq + qb)
            rows = slice(qb * SPAN, (qb + 1) * SPAN)
            before = slice((qb - 1) * SPAN, qb * SPAN)
            for p in range(GROUP_WIDTH // 128):
                sl = slice(p * 128, (p + 1) * 128)
                qp = q_ref[rows, sl]
                kk = jnp.concatenate([kp_ref[:, sl] if qb == 0 else kc_ref[before, sl], kc_ref[rows, sl]], axis=0)
                vv = jnp.concatenate([vp_ref[:, sl] if qb == 0 else vc_ref[before, sl], vc_ref[rows, sl]], axis=0)
                q2 = jnp.concatenate([qp * keep[0], qp * keep[1]], axis=0)
                sc = lax.dot_general(q2, kk, NT, preferred_element_type=F32) * (HEAD_DIM ** -0.5)
                sc = jnp.where(valid, sc, -1e30)
                mx = jnp.max(sc, axis=-1, keepdims=True)
                pe = jnp.exp(sc - mx)
                den = jnp.sum(pe, axis=-1, keepdims=True)
                out = jnp.dot(pe.astype(BF16), vv, preferred_element_type=F32) / den
                lse = jnp.broadcast_to(mx + jnp.log(den), (2 * SPAN, 128))
                o_ref[rows, sl] = jnp.where(masks[0], out[:SPAN], out[SPAN:])
                l_ref[rows, sl] = jnp.where(masks[0], lse[:SPAN], lse[SPAN:])

    blk = (nq * SPAN, GROUP_WIDTH)
    cur = lambda t: pl.BlockSpec(blk, lambda r, b: (b, r * 3 + t))
    prev = lambda t: pl.BlockSpec((SPAN, GROUP_WIDTH), lambda r, b: (jnp.maximum(nq * b - 1, 0), r * 3 + t))
    out = pl.BlockSpec(blk, lambda r, b: (b, r))
    o, lse = pl.pallas_call(
        body, name=f"attn_fwd_g{grp}", out_shape=[jax.ShapeDtypeStruct((l, dil * GROUP_WIDTH), F32)] * 2,
        grid=(dil, nb // nq), in_specs=[cur(0), prev(1), cur(1), prev(2), cur(2)], out_specs=[out, out],
        compiler_params=_params("parallel", "arbitrary"),
    )(qv, qv, qv, qv, qv)
    return o.reshape(s, GROUP_WIDTH), lse.reshape(s, GROUP_WIDTH)


def _resnorm_store(y, x_ref, g_ref, y_ref, xo_ref):
    r = lax.rsqrt(jnp.mean(y * y, axis=-1, keepdims=True) + EPS)
    y_ref[...] = y
    xo_ref[...] = x_ref[...] + y * r * g_ref[...]


def _mix_wo(os_, ls_, wot, x, g, tm=512):
    s, d = x.shape
    gw = wot.shape[1]
    tm = min(tm, s)

    def body(o0, o1, o2, l0, l1, l2, w_ref, x_ref, g_ref, y_ref, xo_ref, mixed_ref, lse_ref):
        a0, a1, a2 = l0[...], l1[...], l2[...]
        mx = jnp.maximum(jnp.maximum(a0, a1), a2)
        e0, e1, e2 = jnp.exp(a0 - mx), jnp.exp(a1 - mx), jnp.exp(a2 - mx)
        den = e0 + e1 + e2
        mixed = (e0 / den) * o0[...] + (e1 / den) * o1[...] + (e2 / den) * o2[...]
        mixed_ref[...] = mixed.astype(BF16)
        lse_ref[...] = mx + jnp.log(den)
        y = lax.dot_general(mixed.astype(BF16), w_ref[...], NT, preferred_element_type=F32)
        _resnorm_store(y, x_ref, g_ref, y_ref, xo_ref)

    return pl.pallas_call(
        body, name="mix_wo",
        out_shape=[jax.ShapeDtypeStruct((s, d), F32), jax.ShapeDtypeStruct((s, d), F32),
                   jax.ShapeDtypeStruct((s, gw), BF16), jax.ShapeDtypeStruct((s, gw), F32)],
        grid=(s // tm,), in_specs=[_rows(tm, gw)] * 6 + [_full((d, gw)), _rows(tm, d), _full((1, d))],
        out_specs=[_rows(tm, d), _rows(tm, d), _rows(tm, gw), _rows(tm, gw)],
        compiler_params=_params("parallel"),
    )(*os_, *ls_, wot, x, g)


def _matmul_resnorm(a, w, x, g, *, name, bias=None, tm=512):
    s, k = a.shape
    d = w.shape[1]
    tm = min(tm, s)

    def body(*refs):
        a_ref, w_ref = refs[:2]
        b_ref = refs[2] if bias is not None else None
        x_ref, g_ref, y_ref, xo_ref = refs[-4:]
        y = jnp.dot(a_ref[...], w_ref[...], preferred_element_type=F32)
        if b_ref is not None:
            y = y + b_ref[...]
        _resnorm_store(y, x_ref, g_ref, y_ref, xo_ref)

    in_specs = [_rows(tm, k), _full((k, d))] + ([_full((1, d))] if bias is not None else []) + [_rows(tm, d), _full((1, d))]
    args = [a, w] + ([bias] if bias is not None else []) + [x, g]
    return pl.pallas_call(
        body, name=name, out_shape=[jax.ShapeDtypeStruct((s, d), F32)] * 2, grid=(s // tm,),
        in_specs=in_specs, out_specs=[_rows(tm, d)] * 2, compiler_params=_params("parallel"),
    )(*args)


FFN_SUB = 256


def _conv3_rows(z_ref, halo_ref, rb, sub, cs, first):
    zc = z_ref[rb * sub:(rb + 1) * sub, cs].astype(F32)
    if rb == 0:
        halo = halo_ref[:, cs].astype(F32) * jnp.where(first, 0.0, 1.0)
    else:
        halo = z_ref[rb * sub - 16:rb * sub, cs].astype(F32)[8:]
    z2, z1 = _conv3_taps(zc, halo)
    return z2, z1, zc


def _conv3_taps(z, halo):
    row = lax.broadcasted_iota(jnp.int32, (8, z.shape[1]), 0)
    h6, h7 = halo[6:7, :], halo[7:8, :]
    r1, r2 = pltpu.roll(z, 1, 0), pltpu.roll(z, 2, 0)
    z1 = jnp.concatenate([jnp.where(row == 0, h7, r1[0:8]), r1[8:]], axis=0)
    z2 = jnp.concatenate([jnp.where(row == 0, h6, jnp.where(row == 1, h7, r2[0:8])), r2[8:]], axis=0)
    return z2, z1


def _ffn_cols(f):
    return _tile(f)


def _lane_chunks(width, fn):
    def step(k, carry):
        fn(pl.ds(pl.multiple_of(k * 128, 128), 128))
        return carry

    lax.fori_loop(0, width // 128, step, 0)


def _ffn_act(z, w_dw, b_dw, tm=1024):
    s, f2 = z.shape
    f = f2 // 2
    tm = min(tm, s)
    sub = min(FFN_SUB, tm)
    tc = _ffn_cols(f)
    nfc = f // tc

    def body(zu, zg, hu, hg, wu, wg, bu, bg, o_ref):
        first = pl.program_id(0) == 0

        def chunk(cs):
            for rb in range(tm // sub):
                def conv(z_ref, h_ref, w_ref, b_ref):
                    z2, z1, zc = _conv3_rows(z_ref, h_ref, rb, sub, cs, first)
                    return w_ref[0:1, cs] * z2 + w_ref[1:2, cs] * z1 + w_ref[2:3, cs] * zc + b_ref[:, cs]

                up, gate = conv(zu, hu, wu, bu), conv(zg, hg, wg, bg)
                o_ref[rb * sub:(rb + 1) * sub, cs] = (gate * _sigmoid(gate) * up).astype(BF16)

        _lane_chunks(tc, chunk)

    hb = tm // 8
    tile = lambda off: pl.BlockSpec((tm, tc), lambda i, j: (i, off + j))
    halo = lambda off: pl.BlockSpec((8, tc), lambda i, j: (jnp.maximum(i * hb - 1, 0), off + j))
    prm = lambda rows, off: pl.BlockSpec((rows, tc), lambda i, j: (0, off + j))
    return pl.pallas_call(
        body, name="ffn_act", out_shape=jax.ShapeDtypeStruct((s, f), BF16), grid=(s // tm, nfc),
        in_specs=[tile(0), tile(nfc), halo(0), halo(nfc), prm(FFN_CONV, 0), prm(FFN_CONV, nfc), prm(1, 0), prm(1, nfc)],
        out_specs=pl.BlockSpec((tm, tc), lambda i, j: (i, j)), compiler_params=_params("parallel", "parallel"),
    )(z, z, z, z, w_dw, w_dw, b_dw, b_dw)


def _shifted_planes(ext_ref):
    rows = ext_ref.shape[1]
    for s in range(1, 8):
        ext_ref[s, 0:rows - 8, :] = ext_ref[0, s:s + rows - 8, :]


def _window(ext_ref, off, tm, cs):
    s = off % 8
    return ext_ref[s, off - s:off - s + tm, cs]


def _conv_taps(ext_ref, w_ref, offs, tm, out_ref):
    def chunk(cs):
        acc = w_ref[0:1, cs] * _window(ext_ref, offs[0], tm, cs)
        for j in range(1, len(offs)):
            acc = acc + w_ref[j:j + 1, cs] * _window(ext_ref, offs[j], tm, cs)
        out_ref[:, cs] = acc

    _lane_chunks(out_ref.shape[1], chunk)


def _glu_planes(ag_ref, halo_ref, ext_ref, first, c):
    hal = halo_ref[...].astype(F32)
    ext_ref[0, 0:CONV_HALO, :] = hal[:, :c] * _sigmoid(hal[:, c:]) * jnp.where(first, 0.0, 1.0)
    ag = ag_ref[...].astype(F32)
    ext_ref[0, CONV_HALO:, :] = ag[:, :c] * _sigmoid(ag[:, c:])
    _shifted_planes(ext_ref)


def _layernorm_stats(u1):
    mu = jnp.mean(u1, axis=-1, keepdims=True)
    cen = u1 - mu
    rstd = lax.rsqrt(jnp.mean(cen * cen, axis=-1, keepdims=True) + EPS)
    return cen * rstd, rstd


def _conv_mid(ag, w_dw, b_dw, ln_g, ln_b, tm=256):
    s, c2 = ag.shape
    c = c2 // 2
    tm = min(tm, s)

    def body(ag_ref, halo_ref, w_ref, b_ref, g_ref, bb_ref, o_ref, u1_ref, ext_ref):
        _glu_planes(ag_ref, halo_ref, ext_ref, pl.program_id(0) == 0, c)
        base = CONV_HALO - (CONV_KERNEL - 1)
        _conv_taps(ext_ref, w_ref, [base + j for j in range(CONV_KERNEL)], tm, u1_ref)
        xh, _ = _layernorm_stats(u1_ref[...] + b_ref[...])
        u2 = xh * g_ref[...] + bb_ref[...]
        o_ref[...] = (u2 * _sigmoid(u2)).astype(BF16)

    hb = tm // CONV_HALO
    return pl.pallas_call(
        body, name="conv_mid", out_shape=[jax.ShapeDtypeStruct((s, c), BF16), jax.ShapeDtypeStruct((s, c), F32)], grid=(s // tm,),
        in_specs=[_rows(tm, c2), pl.BlockSpec((CONV_HALO, c2), lambda i: (jnp.maximum(i * hb - 1, 0), 0)),
                  _full((CONV_KERNEL, c)), _full((1, c)), _full((1, c)), _full((1, c))],
        out_specs=[_rows(tm, c), _rows(tm, c)], scratch_shapes=[pltpu.VMEM((8, CONV_HALO + tm, c), F32)],
        compiler_params=_params("arbitrary"),
    )(ag, ag, w_dw, b_dw, ln_g, ln_b)


def _loss_grad(xo, target, tm=1024):
    s, d = xo.shape
    tm = min(tm, s)

    def body(x_ref, t_ref, dx_ref, loss_ref):
        @pl.when(pl.program_id(0) == 0)
        def _():
            loss_ref[...] = jnp.zeros_like(loss_ref)

        err = x_ref[...] - t_ref[...]
        dx_ref[...] = err * (1.0 / d)
        loss_ref[...] += 0.5 * jnp.sum(jnp.mean(err * err, axis=-1, keepdims=True))

    return pl.pallas_call(
        body, name="loss_grad", out_shape=[jax.ShapeDtypeStruct((s, d), F32), jax.ShapeDtypeStruct((1, 128), F32)],
        grid=(s // tm,), in_specs=[_rows(tm, d)] * 2, out_specs=[_rows(tm, d), _full((1, 128))],
        compiler_params=_params("arbitrary"),
    )(xo, target)


def _postnorm_bwd(y, g, dxo, *, name, with_bias_grad=False, tm=1024):
    s, d = y.shape
    tm = min(tm, s)

    def body(y_ref, g_ref, dx_ref, dy_ref, dg_ref, *rest):
        @pl.when(pl.program_id(0) == 0)
        def _():
            dg_ref[...] = jnp.zeros_like(dg_ref)
            for r_ in rest:
                r_[...] = jnp.zeros_like(r_)

        yv, dxo_v = y_ref[...], dx_ref[...]
        r = lax.rsqrt(jnp.mean(yv * yv, axis=-1, keepdims=True) + EPS)
        yh = yv * r
        dyh = dxo_v * g_ref[...]
        dy = r * (dyh - yh * jnp.mean(dyh * yh, axis=-1, keepdims=True))
        dy_ref[...] = dy.astype(BF16)
        dg_ref[...] += jnp.sum(dxo_v * yh, axis=0, keepdims=True)
        for r_ in rest:
            r_[...] += jnp.sum(dy, axis=0, keepdims=True)

    nacc = 2 if with_bias_grad else 1
    return pl.pallas_call(
        body, name=name, out_shape=[jax.ShapeDtypeStruct((s, d), BF16)] + [jax.ShapeDtypeStruct((1, d), F32)] * nacc,
        grid=(s // tm,), in_specs=[_rows(tm, d), _full((1, d)), _rows(tm, d)],
        out_specs=[_rows(tm, d)] + [_full((1, d))] * nacc, compiler_params=_params("arbitrary"),
    )(y, g, dxo)


def _matmul(gmat, w, *, name, out_dtype, transposed_w, tm=512):
    s, k = gmat.shape
    n = w.shape[0] if transposed_w else w.shape[1]
    tm = min(tm, s)

    def body(g_ref, w_ref, o_ref):
        if transposed_w:
            acc = lax.dot_general(g_ref[...], w_ref[...], NT, preferred_element_type=F32)
        else:
            acc = jnp.dot(g_ref[...], w_ref[...], preferred_element_type=F32)
        o_ref[...] = acc.astype(out_dtype)

    return pl.pallas_call(
        body, name=name, out_shape=jax.ShapeDtypeStruct((s, n), out_dtype), grid=(s // tm,),
        in_specs=[_rows(tm, k), _full(w.shape)], out_specs=_rows(tm, n), compiler_params=_params("parallel"),
    )(gmat, w)


def _matmul_prenorm_bwd(pieces, wt, x, g, dres, *, name, tm=256):
    s, d = x.shape
    tm = min(tm, s)
    np_ = len(pieces)

    def body(*refs):
        p_refs, w_refs = refs[:np_], refs[np_:2 * np_]
        x_ref, g_ref, r_ref, dx_ref, dg_ref = refs[2 * np_:]

        @pl.when(pl.program_id(0) == 0)
        def _():
            dg_ref[...] = jnp.zeros_like(dg_ref)

        dh = None
        for p_ref, w_ref in zip(p_refs, w_refs):
            t = jnp.dot(p_ref[...], w_ref[...], preferred_element_type=F32)
            dh = t if dh is None else dh + t
        xv = x_ref[...]
        r = lax.rsqrt(jnp.mean(xv * xv, axis=-1, keepdims=True) + EPS)
        xh = xv * r
        dyh = dh * g_ref[...]
        dx_ref[...] = r_ref[...] + r * (dyh - xh * jnp.mean(dyh * xh, axis=-1, keepdims=True))
        dg_ref[...] += jnp.sum(dh * xh, axis=0, keepdims=True)

    in_specs = []
    for _, c0, kc, _ in pieces:
        assert c0 % kc == 0
        in_specs.append(pl.BlockSpec((tm, kc), lambda i, _b=c0 // kc: (i, _b)))
    for _, _, kc, r0 in pieces:
        assert r0 % kc == 0
        in_specs.append(pl.BlockSpec((kc, d), lambda i, _b=r0 // kc: (_b, 0)))
    in_specs += [_rows(tm, d), _full((1, d)), _rows(tm, d)]
    return pl.pallas_call(
        body, name=name, out_shape=[jax.ShapeDtypeStruct((s, d), F32), jax.ShapeDtypeStruct((1, d), F32)],
        grid=(s // tm,), in_specs=in_specs, out_specs=[_rows(tm, d), _full((1, d))],
        compiler_params=_params("arbitrary"),
    )(*[p[0] for p in pieces], *[wt] * np_, x, g, dres)


def _weight_grad(a, gmat, *, name, a_col0=0, ka=None, out=None, out_shape=None, layer=0, row0=0, ts=1024):
    s = a.shape[0]
    ka = a.shape[1] if ka is None else ka
    n = gmat.shape[1]
    ts = min(ts, s)
    tka = _tile(ka, a_col0, row0)
    shape = out.shape if out is not None else out_shape
    nsteps = s // ts

    def body(a_ref, g_ref, *rest):
        o_ref, acc_ref = rest[-2:]
        i = pl.program_id(1)

        @pl.when(i == 0)
        def _():
            acc_ref[...] = jnp.zeros_like(acc_ref)

        acc_ref[...] += lax.dot_general(a_ref[...], g_ref[...], TN, preferred_element_type=F32)

        @pl.when(i == nsteps - 1)
        def _():
            o_ref[...] = acc_ref[...].astype(BF16)

    in_specs = [pl.BlockSpec((ts, tka), lambda k, i: (i, a_col0 // tka + k)), pl.BlockSpec((ts, n), lambda k, i: (i, 0))]
    args = [a, gmat]
    aliases = {}
    if out is not None:
        in_specs.append(ANY)
        args.append(out)
        aliases = {2: 0}
    return pl.pallas_call(
        body, name=name, out_shape=jax.ShapeDtypeStruct(shape, BF16), grid=(ka // tka, nsteps), in_specs=in_specs,
        out_specs=pl.BlockSpec((None, tka, n), lambda k, i: (layer, row0 // tka + k, 0)),
        scratch_shapes=[pltpu.VMEM((tka, n), F32)],
        input_output_aliases=aliases, compiler_params=_params("parallel", "arbitrary"),
    )(*args)


def _ffn_act_bwd(z, dact, w_dw, b_dw, tm=512):
    s, f2 = z.shape
    f = f2 // 2
    tm = min(tm, s)
    sub = min(FFN_SUB // 2, tm)
    tc = _ffn_cols(f)
    nfc = f // tc

    def body(zu, zg, hu, hg, wu, wg, bu, bg, da_ref, du_ref, dgt_ref, dbu_ref, dbg_ref, dwu_ref, dwg_ref):
        i = pl.program_id(1)

        @pl.when(i == 0)
        def _():
            for r_ in (dbu_ref, dbg_ref, dwu_ref, dwg_ref):
                r_[...] = jnp.zeros_like(r_)

        def chunk(cs):
            for rb in range(tm // sub):
                rows = slice(rb * sub, (rb + 1) * sub)

                def conv(z_ref, h_ref, w_ref, b_ref):
                    taps = _conv3_rows(z_ref, h_ref, rb, sub, cs, i == 0)
                    return taps, w_ref[0:1, cs] * taps[0] + w_ref[1:2, cs] * taps[1] + w_ref[2:3, cs] * taps[2] + b_ref[:, cs]

                taps_u, up = conv(zu, hu, wu, bu)
                taps_g, gate = conv(zg, hg, wg, bg)
                da = da_ref[rows, cs].astype(F32)
                sg = _sigmoid(gate)
                d_up = da * (gate * sg)
                d_gate = da * up * (sg * (1.0 + gate * (1.0 - sg)))
                du_ref[rows, cs] = d_up.astype(BF16)
                dgt_ref[rows, cs] = d_gate.astype(BF16)
                for dv, taps, db_ref, dw_ref in ((d_up, taps_u, dbu_ref, dwu_ref), (d_gate, taps_g, dbg_ref, dwg_ref)):
                    db_ref[:, cs] += jnp.sum(dv, axis=0, keepdims=True)
                    for k_, tap in enumerate(taps):
                        dw_ref[k_:k_ + 1, cs] += jnp.sum(dv * tap, axis=0, keepdims=True)

        _lane_chunks(tc, chunk)

    hb = tm // 8
    tile = lambda off: pl.BlockSpec((tm, tc), lambda j, i: (i, off + j))
    halo = lambda off: pl.BlockSpec((8, tc), lambda j, i: (jnp.maximum(i * hb - 1, 0), off + j))
    prm = lambda rows, off: pl.BlockSpec((rows, tc), lambda j, i: (0, off + j))
    acc = lambda rows: pl.BlockSpec((rows, tc), lambda j, i: (0, j))
    return pl.pallas_call(
        body, name="ffn_act_bwd",
        out_shape=[jax.ShapeDtypeStruct((s, f), BF16)] * 2 + [jax.ShapeDtypeStruct((1, f), F32)] * 2
        + [jax.ShapeDtypeStruct((FFN_CONV, f), F32)] * 2,
        grid=(nfc, s // tm),
        in_specs=[tile(0), tile(nfc), halo(0), halo(nfc), prm(FFN_CONV, 0), prm(FFN_CONV, nfc), prm(1, 0), prm(1, nfc), tile(0)],
        out_specs=[tile(0), tile(0), acc(1), acc(1), acc(FFN_CONV), acc(FFN_CONV)],
        compiler_params=_params("parallel", "arbitrary"),
    )(z, z, z, z, w_dw, w_dw, b_dw, b_dw, dact)


def _conv3_transpose(dug, w_dw, col0, tm=1024):
    s, f = dug.shape
    tm = min(tm, s)
    sub = min(FFN_SUB, tm)
    nsub = tm // sub
    tc = _ffn_cols(f)
    nfc = f // tc
    nrow = s // tm
    off = col0 // tc

    def body(d_ref, n_ref, w_ref, o_ref):
        keep_next = jnp.where(pl.program_id(0) == nrow - 1, 0.0, 1.0)

        def chunk(cs):
            for rb in range(nsub):
                rows = slice(rb * sub, (rb + 1) * sub)
                dv = d_ref[rows, cs].astype(F32)
                if rb == nsub - 1:
                    nxt = n_ref[:, cs].astype(F32) * keep_next
                else:
                    nxt = d_ref[(rb + 1) * sub:(rb + 1) * sub + 16, cs].astype(F32)[:8]
                n0, n1 = nxt[0:1, :], nxt[1:2, :]
                row = lax.broadcasted_iota(jnp.int32, (8, dv.shape[1]), 0)
                r1, r2 = pltpu.roll(dv, sub - 1, 0), pltpu.roll(dv, sub - 2, 0)
                d1 = jnp.concatenate([r1[:sub - 8], jnp.where(row == 7, n0, r1[sub - 8:])], axis=0)
                d2 = jnp.concatenate([r2[:sub - 8], jnp.where(row == 7, n1, jnp.where(row == 6, n0, r2[sub - 8:]))], axis=0)
                o_ref[rows, cs] = (w_ref[2:3, cs] * dv + w_ref[1:2, cs] * d1 + w_ref[0:1, cs] * d2).astype(BF16)

        _lane_chunks(tc, chunk)

    hb = tm // 8
    return pl.pallas_call(
        body, name="conv3_transpose", out_shape=jax.ShapeDtypeStruct((s, f), BF16), grid=(nrow, nfc),
        in_specs=[pl.BlockSpec((tm, tc), lambda i, j: (i, j)),
                  pl.BlockSpec((8, tc), lambda i, j: (jnp.minimum((i + 1) * hb, s // 8 - 1), j)),
                  pl.BlockSpec((FFN_CONV, tc), lambda i, j: (0, off + j))],
        out_specs=pl.BlockSpec((tm, tc), lambda i, j: (i, j)), compiler_params=_params("parallel", "parallel"),
    )(dug, dug, w_dw)


def _conv_mid_bwd(ag, u1, du3, b_dw, ln_g, ln_b, tm=256):
    s, c2 = ag.shape
    c = c2 // 2
    tm = min(tm, s)

    def body(ag_ref, halo_ref, u1in_ref, du_ref, b_ref, g_ref, bb_ref, o_ref, dlg_ref, dlb_ref, db_ref, dw_ref, ext_ref, u1_ref):
        @pl.when(pl.program_id(0) == 0)
        def _():
            for r_ in (dlg_ref, dlb_ref, db_ref, dw_ref):
                r_[...] = jnp.zeros_like(r_)

        _glu_planes(ag_ref, halo_ref, ext_ref, pl.program_id(0) == 0, c)
        xh, rstd = _layernorm_stats(u1in_ref[...] + b_ref[...])
        u2 = xh * g_ref[...] + bb_ref[...]
        sg = _sigmoid(u2)
        du2 = du_ref[...] * (sg * (1.0 + u2 * (1.0 - sg)))
        dlg_ref[...] += jnp.sum(du2 * xh, axis=0, keepdims=True)
        dlb_ref[...] += jnp.sum(du2, axis=0, keepdims=True)
        dxh = du2 * g_ref[...]
        du1 = rstd * (dxh - jnp.mean(dxh, axis=-1, keepdims=True) - xh * jnp.mean(dxh * xh, axis=-1, keepdims=True))
        o_ref[...] = du1.astype(BF16)
        db_ref[...] += jnp.sum(du1, axis=0, keepdims=True)
        u1_ref[...] = du1
        base = CONV_HALO - (CONV_KERNEL - 1)

        def chunk(cs):
            dc = u1_ref[:, cs]
            for j in range(CONV_KERNEL):
                dw_ref[j:j + 1, cs] += jnp.sum(dc * _window(ext_ref, base + j, tm, cs), axis=0, keepdims=True)

        _lane_chunks(c, chunk)

    hb = tm // CONV_HALO
    vec = _full((1, c))
    return pl.pallas_call(
        body, name="conv_mid_bwd",
        out_shape=[jax.ShapeDtypeStruct((s, c), BF16)] + [jax.ShapeDtypeStruct((1, c), F32)] * 3
        + [jax.ShapeDtypeStruct((CONV_HALO, c), F32)],
        grid=(s // tm,),
        in_specs=[_rows(tm, c2), pl.BlockSpec((CONV_HALO, c2), lambda i: (jnp.maximum(i * hb - 1, 0), 0)), _rows(tm, c),
                  _rows(tm, c), vec, vec, vec],
        out_specs=[_rows(tm, c), vec, vec, vec, _full((CONV_HALO, c))],
        scratch_shapes=[pltpu.VMEM((8, CONV_HALO + tm, c), F32), pltpu.VMEM((tm, c), F32)],
        compiler_params=_params("arbitrary"),
    )(ag, ag, u1, du3, b_dw, ln_g, ln_b)


def _glu_conv_bwd(du1, ag, w_dw, tm=256):
    s, c = du1.shape
    tm = min(tm, s)
    nrow = s // tm

    def body(d_ref, n_ref, ag_ref, w_ref, o_ref, db_ref, ext_ref, du0_ref):
        @pl.when(pl.program_id(0) == 0)
        def _():
            db_ref[...] = jnp.zeros_like(db_ref)

        ext_ref[0, 0:tm, :] = d_ref[...].astype(F32)
        ext_ref[0, tm:, :] = n_ref[...].astype(F32) * jnp.where(pl.program_id(0) == nrow - 1, 0.0, 1.0)
        _shifted_planes(ext_ref)
        top = CONV_KERNEL - 1
        _conv_taps(ext_ref, w_ref, [top - j for j in range(CONV_KERNEL)], tm, du0_ref)
        du0 = du0_ref[...]
        ag = ag_ref[...].astype(F32)
        a, gt = ag[:, :c], ag[:, c:]
        sg = _sigmoid(gt)
        da = du0 * sg
        dgt = du0 * a * (sg * (1.0 - sg))
        o_ref[:, :c] = da.astype(BF16)
        o_ref[:, c:] = dgt.astype(BF16)
        db_ref[:, :c] += jnp.sum(da, axis=0, keepdims=True)
        db_ref[:, c:] += jnp.sum(dgt, axis=0, keepdims=True)

    hb = tm // CONV_HALO
    return pl.pallas_call(
        body, name="glu_conv_bwd",
        out_shape=[jax.ShapeDtypeStruct((s, 2 * c), BF16), jax.ShapeDtypeStruct((1, 2 * c), F32)], grid=(nrow,),
        in_specs=[_rows(tm, c), pl.BlockSpec((CONV_HALO, c), lambda i: (jnp.minimum((i + 1) * hb, s // CONV_HALO - 1), 0)),
                  _rows(tm, 2 * c), _full((CONV_KERNEL, c))],
        out_specs=[_rows(tm, 2 * c), _full((1, 2 * c))],
        scratch_shapes=[pltpu.VMEM((8, tm + CONV_HALO, c), F32), pltpu.VMEM((tm, c), F32)],
        compiler_params=_params("arbitrary"),
    )(du1, du1, ag, w_dw)


def _head_rows(v, mask):
    return jnp.max(jnp.where(mask, v, -jnp.inf), axis=-1, keepdims=True)


def _attn_bwd(qv, dmix, mixed, lse, rope, grp, dil):
    l = qv.shape[0]
    s = l * dil
    nb = l // SPAN
    view = lambda t: t.reshape(l, dil * t.shape[1])
    scale = HEAD_DIM ** -0.5
    gw = GROUP_WIDTH

    def body(q_ref, kp_ref, kc_ref, vp_ref, vc_ref, do_ref, mx_ref, l_ref, tab_ref, tabp_ref, dq_ref, dkv_ref, carry_ref):
        b = pl.program_id(1)

        @pl.when(b < nb)
        def _():
            valid = _band_mask(b)
            masks, keep = _head_masks()
            for p in range(gw // 128):
                sl = slice(p * 128, (p + 1) * 128)
                sl_v = slice(gw + p * 128, gw + (p + 1) * 128)
                qp, dop = q_ref[:, sl], do_ref[:, sl]
                kk = jnp.concatenate([kp_ref[:, sl], kc_ref[:, sl]], axis=0)
                vv = jnp.concatenate([vp_ref[:, sl], vc_ref[:, sl]], axis=0)
                prod = dop.astype(F32) * mx_ref[:, sl].astype(F32)
                lsep = l_ref[:, sl]
                q2 = jnp.concatenate([qp * keep[0], qp * keep[1]], axis=0)
                do2 = jnp.concatenate([dop * keep[0], dop * keep[1]], axis=0)
                lse2 = jnp.concatenate([_head_rows(lsep, masks[h]) for h in range(2)], axis=0)
                dbar2 = jnp.concatenate([jnp.sum(jnp.where(masks[h], prod, 0.0), axis=-1, keepdims=True) for h in range(2)], axis=0)
                sc = lax.dot_general(q2, kk, NT, preferred_element_type=F32) * scale
                pe = jnp.where(valid, jnp.exp(sc - lse2), 0.0)
                dp = lax.dot_general(do2, vv, NT, preferred_element_type=F32)
                ds = (pe * (dp - dbar2) * scale).astype(BF16)
                dq2 = jnp.dot(ds, kk, preferred_element_type=F32)
                dq = jnp.where(masks[0], dq2[:SPAN], dq2[SPAN:])
                dq_ref[:, sl] = _rope_transpose(dq, tab_ref[...]).astype(BF16)
                dk = lax.dot_general(ds, q2, TN, preferred_element_type=F32)
                dv = lax.dot_general(pe.astype(BF16), do2, TN, preferred_element_type=F32)

                @pl.when(b > 0)
                def _():
                    dk_prev = carry_ref[:, sl] + dk[:SPAN]
                    dkv_ref[:, sl] = _rope_transpose(dk_prev, tabp_ref[...]).astype(BF16)
                    dkv_ref[:, sl_v] = (carry_ref[:, sl_v] + dv[:SPAN]).astype(BF16)

                carry_ref[:, sl] = dk[SPAN:]
                carry_ref[:, sl_v] = dv[SPAN:]

        @pl.when(b == nb)
        def _():
            for p in range(gw // 128):
                sl = slice(p * 128, (p + 1) * 128)
                sl_v = slice(gw + p * 128, gw + (p + 1) * 128)
                dkv_ref[:, sl] = _rope_transpose(carry_ref[:, sl], tabp_ref[...]).astype(BF16)
                dkv_ref[:, sl_v] = carry_ref[:, sl_v].astype(BF16)

    blk = (SPAN, gw)
    cb = lambda b: jnp.minimum(b, nb - 1)
    cur = lambda t: pl.BlockSpec(blk, lambda r, b: (cb(b), r * 3 + t))
    prev = lambda t: pl.BlockSpec(blk, lambda r, b: (jnp.maximum(cb(b) - 1, 0), r * 3 + t))
    own = pl.BlockSpec(blk, lambda r, b: (cb(b), r))
    tab = pl.BlockSpec((SPAN, ROPE_COLS), lambda r, b: (cb(b), r))
    tab_prev = pl.BlockSpec((SPAN, ROPE_COLS), lambda r, b: (jnp.maximum(b - 1, 0), r))
    dq, dkv = pl.pallas_call(
        body, name=f"attn_bwd_g{grp}",
        out_shape=[jax.ShapeDtypeStruct((l, dil * gw), BF16), jax.ShapeDtypeStruct((l, dil * 2 * gw), BF16)],
        grid=(dil, nb + 1),
        in_specs=[cur(0), prev(1), cur(1), prev(2), cur(2), own, own, own, tab, tab_prev],
        out_specs=[own, pl.BlockSpec((SPAN, 2 * gw), lambda r, b: (jnp.maximum(b - 1, 0), r))],
        scratch_shapes=[pltpu.VMEM((SPAN, 2 * gw), F32)], compiler_params=_params("parallel", "arbitrary"),
    )(qv, qv, qv, qv, qv, view(dmix), view(mixed), view(lse), view(rope), view(rope))
    return dq.reshape(s, gw), dkv.reshape(s, 2 * gw)


def _rope_freq_row():
    half = ROT_DIM // 2
    inv = (ROPE_THETA ** (-np.arange(half, dtype=np.float32) / half)).astype(np.float32)
    row = np.zeros((1, 128), np.float32)
    for head in range(128 // HEAD_DIM):
        row[0, head * HEAD_DIM:head * HEAD_DIM + half] = inv
        row[0, head * HEAD_DIM + half:head * HEAD_DIM + ROT_DIM] = inv
    return jnp.asarray(row)


def _ffn_fwd(x, g_pre, g_post, w_up_t, w_dw, b_dw, w_down):
    h, z = _norm_matmul(x, g_pre, w_up_t, tn=_tile(w_up_t.shape[0]), name="ffn_up")
    act = _ffn_act(z, w_dw, b_dw)
    y, xo = _matmul_resnorm(act, w_down, x, g_post, name="ffn_down")
    return xo, (x, h, z, act, y)


def _ffn_bwd(saved, dxo, g_pre, g_post, w_up_t, w_dw, b_dw, w_down):
    x, h, z, act, y = saved
    f = act.shape[1]
    d = x.shape[1]
    dy, dg_post = _postnorm_bwd(y, g_post, dxo, name="ffn_post_bwd")
    dact = _matmul(dy, w_down, name="ffn_dact", out_dtype=BF16, transposed_w=True)
    d_down = _weight_grad(act, dy, name="ffn_dw_down", out_shape=(1, f, d))
    dug_u, dug_g, db_u, db_g, dwd_u, dwd_g = _ffn_act_bwd(z, dact, w_dw, b_dw)
    dz_u = _conv3_transpose(dug_u, w_dw, 0)
    dz_g = _conv3_transpose(dug_g, w_dw, f)
    dx, dg_pre = _matmul_prenorm_bwd([(dz_u, 0, f, 0), (dz_g, 0, f, f)], w_up_t, x, g_pre, dxo, name="ffn_dx")
    d_up_t = _weight_grad(dz_u, h, name="ffn_dw_up", out_shape=(1, 2 * f, d))
    d_up_t = _weight_grad(dz_g, h, name="ffn_dw_up", out=d_up_t, row0=f)
    grads = dict(w_dw=jnp.concatenate([dwd_u, dwd_g], axis=1), b_dw=jnp.concatenate([db_u, db_g], axis=1),
                 g_pre=dg_pre, g_post=dg_post)
    return dx, grads, d_up_t, d_down


def _local_step(x, pos_col, target, p, tie=None, late_weights=None, exchange=None):
    ng = p["norm_g"]
    row = lambda r: ng[r:r + 1]
    freq = _rope_freq_row()
    rope = _rope_tables(pos_col, freq if tie is None else freq + tie[0:1])
    d = x.shape[1]

    h0, *qkv = _qkv_proj(x, row(0), p["w_qkv_t"], rope)
    os_, ls_ = zip(*[_attn_fwd(qkv[g_], g_, d_) for g_, d_ in enumerate(DILATIONS)])
    y_a, x1, mixed, lse = _mix_wo(os_, ls_, p["w_o_t"], x, row(1))
    if late_weights is not None:
        p = {**p, **late_weights(x1)}
    x2, ffn0 = _ffn_fwd(x1, row(2), row(3), p["w_up_t"][0], p["ffn_w_dw"][0], p["ffn_b_dw"][0], p["w_down"][0])
    h1, ag = _norm_matmul(x2, row(4), p["w_pw1_t"], tn=_tile(p["w_pw1_t"].shape[0]), name="conv_pw1", bias=p["b_pw1"])
    u3, u1 = _conv_mid(ag, p["conv_w_dw"], p["conv_b_dw"], p["ln_g"], p["ln_b"])
    y_c, x3 = _matmul_resnorm(u3, p["w_pw2"], x2, row(5), name="conv_pw2", bias=p["b_pw2"])
    x4, ffn1 = _ffn_fwd(x3, row(6), row(7), p["w_up_t"][1], p["ffn_w_dw"][1], p["ffn_b_dw"][1], p["w_down"][1])
    dx4, loss = _loss_grad(x4, target)

    big = [BF16, BF16]

    def tied(r, *tokens):
        tokens = [t for t in tokens if t is not None]
        return row(r) if not tokens else row(r) + jnp.tile(sum(tokens)[0:1], (1, d // 128))

    dx3, gf1, d_up1, d_down1 = _ffn_bwd(ffn1, dx4, row(6), row(7), p["w_up_t"][1], p["ffn_w_dw"][1], p["ffn_b_dw"][1],
                                        p["w_down"][1])
    t0 = exchange.submit("ffn1", [d_up1, d_down1], big) if exchange else None
    dy_c, dg5, db_pw2 = _postnorm_bwd(y_c, tied(5, t0), dx3, name="conv_post_bwd", with_bias_grad=True)
    du3 = _matmul(dy_c, p["w_pw2"], name="conv_du3", out_dtype=F32, transposed_w=True)
    d_wpw2 = _weight_grad(u3, dy_c, name="conv_dw_pw2", out_shape=(1, u3.shape[1], d))
    du1, d_lng, d_lnb, d_cbdw, d_cwdw = _conv_mid_bwd(ag, u1, du3, p["conv_b_dw"], p["ln_g"], p["ln_b"])
    dag, db_pw1 = _glu_conv_bwd(du1, ag, p["conv_w_dw"])
    dx2, dg4 = _matmul_prenorm_bwd([(dag, 0, dag.shape[1], 0)], p["w_pw1_t"], x2, row(4), dx3, name="conv_dx")
    d_wpw1_t = _weight_grad(dag, h1, name="conv_dw_pw1", out_shape=(1, dag.shape[1], d))
    t0 = exchange.advance(dx2) if exchange else None
    t1 = exchange.submit("conv", [d_wpw1_t, d_wpw2], big) if exchange else None
    dx1, gf0, d_up0, d_down0 = _ffn_bwd(ffn0, dx2, row(2), tied(3, t0, t1), p["w_up_t"][0], p["ffn_w_dw"][0], p["ffn_b_dw"][0],
                                        p["w_down"][0])
    t0 = exchange.advance(dx1) if exchange else None
    t1 = exchange.submit("ffn0", [d_up0, d_down0], big) if exchange else None
    dy_a, dg1 = _postnorm_bwd(y_a, tied(1, t0, t1), dx1, name="attn_post_bwd")
    dmix = _matmul(dy_a, p["w_o_t"], name="attn_dmix", out_dtype=BF16, transposed_w=False)
    d_wo_t = _weight_grad(dy_a, mixed, name="attn_dw_o", out_shape=(1, d, GROUP_WIDTH))
    pieces, d_wqkv_t = [], None
    for g_, d_ in enumerate(DILATIONS):
        if exchange and g_ > 0:
            tok = exchange.advance(dkv)
            if tok is not None:
                rope = rope + jnp.tile(tok[0:1], (1, ROPE_COLS // 128))
        dq, dkv = _attn_bwd(qkv[g_], dmix, mixed, lse, rope, g_, d_)
        for t, (arr, c0) in enumerate(((dq, 0), (dkv, 0), (dkv, GROUP_WIDTH))):
            r0 = (3 * t + g_) * GROUP_WIDTH
            pieces.append((arr, c0, GROUP_WIDTH, r0))
            d_wqkv_t = _weight_grad(arr, h0, name="attn_dw_qkv", a_col0=c0, ka=GROUP_WIDTH, out=d_wqkv_t,
                                    out_shape=(1, p["w_qkv_t"].shape[0], d), row0=r0)
    t0 = exchange.advance(dkv) if exchange else None
    t1 = exchange.submit("attn", [d_wqkv_t, d_wo_t], big) if exchange else None
    grad_x, dg0 = _matmul_prenorm_bwd(pieces, p["w_qkv_t"], x, tied(0, t0, t1), dx1, name="attn_dx")

    grads = dict(
        norm_g=jnp.concatenate([dg0, dg1, gf0["g_pre"], gf0["g_post"], dg4, dg5, gf1["g_pre"], gf1["g_post"]], axis=0),
        w_qkv_t=d_wqkv_t, w_o_t=d_wo_t, w_pw1_t=d_wpw1_t, b_pw1=db_pw1,
        conv_w_dw=d_cwdw[:CONV_KERNEL], conv_b_dw=d_cbdw, ln_g=d_lng, ln_b=d_lnb, w_pw2=d_wpw2, b_pw2=db_pw2,
        w_up_t=[d_up0, d_up1], ffn_w_dw=jnp.stack([gf0["w_dw"], gf1["w_dw"]]),
        ffn_b_dw=jnp.concatenate([gf0["b_dw"], gf1["b_dw"]], axis=0), w_down=[d_down0, d_down1])
    return loss, grad_x, grads


SMALL_AXIS = dict(norm_g=2, conv_b_pw1=1, conv_w_dw=2, conv_b_dw=1, conv_ln_g=1, conv_ln_b=1, conv_b_pw2=1, ffn_w_dw=2)
SMALL = tuple(SMALL_AXIS)
MATMUL_WEIGHTS = dict(attn_w_qkv=True, conv_w_pw1=True, ffn_w_up=True, conv_w_pw2=False, ffn_w_down=False)


def _pack(arrays, cols, row_multiple):
    flat = jnp.concatenate([a.reshape(-1) for a in arrays])
    rows = -(-flat.shape[0] // cols)
    rows = -(-rows // row_multiple) * row_multiple
    return jnp.pad(flat, (0, rows * cols - flat.shape[0])).reshape(rows, cols)


def _unpack(packed, shapes):
    flat = packed.reshape(packed.shape[:-2] + (-1,))
    out, off = [], 0
    for shp in shapes:
        n = math.prod(shp)
        out.append(flat[..., off:off + n].reshape(packed.shape[:-2] + tuple(shp)))
        off += n
    return out


def _join_shards(stacked, axis):
    moved = jnp.moveaxis(stacked, 0, axis)
    shp = moved.shape
    return moved.reshape(shp[:axis] + (shp[axis] * shp[axis + 1],) + shp[axis + 2:])


def _split_shards(whole, axis):
    shp = whole.shape
    cut = whole.reshape(shp[:axis] + (N_DEV, shp[axis] // N_DEV) + shp[axis + 1:])
    return jnp.moveaxis(cut, axis, 0)


def _row_shard(w, transposed):
    t = jnp.swapaxes(w, 1, 2) if transposed else w
    return t.astype(BF16).reshape(-1, t.shape[-1])


def kernel(x, positions, norm_g, attn_w_qkv, attn_w_o, conv_w_pw1, conv_b_pw1, conv_w_dw, conv_b_dw, conv_ln_g, conv_ln_b, conv_w_pw2, conv_b_pw2, ffn_w_up, ffn_w_dw, ffn_b_dw, ffn_w_down, loss_target, m_norm_g, m_attn_w_qkv, m_attn_w_o, m_conv_w_pw1, m_conv_b_pw1, m_conv_w_dw, m_conv_b_dw, m_conv_ln_g, m_conv_ln_b, m_conv_w_pw2, m_conv_b_pw2, m_ffn_w_up, m_ffn_w_dw, m_ffn_b_dw, m_ffn_w_down, v_norm_g, v_attn_w_qkv, v_attn_w_o, v_conv_w_pw1, v_conv_b_pw1, v_conv_w_dw, v_conv_b_dw, v_conv_ln_g, v_conv_ln_b, v_conv_w_pw2, v_conv_b_pw2, v_ffn_w_up, v_ffn_w_dw, v_ffn_b_dw, v_ffn_w_down):
    w = dict(norm_g=norm_g, attn_w_qkv=attn_w_qkv, attn_w_o=attn_w_o, conv_w_pw1=conv_w_pw1, conv_b_pw1=conv_b_pw1,
             conv_w_dw=conv_w_dw, conv_b_dw=conv_b_dw, conv_ln_g=conv_ln_g, conv_ln_b=conv_ln_b, conv_w_pw2=conv_w_pw2,
             conv_b_pw2=conv_b_pw2, ffn_w_up=ffn_w_up, ffn_w_dw=ffn_w_dw, ffn_w_down=ffn_w_down)
    m = dict(norm_g=m_norm_g, attn_w_qkv=m_attn_w_qkv, attn_w_o=m_attn_w_o, conv_w_pw1=m_conv_w_pw1, conv_b_pw1=m_conv_b_pw1,
             conv_w_dw=m_conv_w_dw, conv_b_dw=m_conv_b_dw, conv_ln_g=m_conv_ln_g, conv_ln_b=m_conv_ln_b, conv_w_pw2=m_conv_w_pw2,
             conv_b_pw2=m_conv_b_pw2, ffn_w_up=m_ffn_w_up, ffn_w_dw=m_ffn_w_dw, ffn_w_down=m_ffn_w_down)
    v = dict(norm_g=v_norm_g, attn_w_qkv=v_attn_w_qkv, attn_w_o=v_attn_w_o, conv_w_pw1=v_conv_w_pw1, conv_b_pw1=v_conv_b_pw1,
             conv_w_dw=v_conv_w_dw, conv_b_dw=v_conv_b_dw, conv_ln_g=v_conv_ln_g, conv_ln_b=v_conv_ln_b, conv_w_pw2=v_conv_w_pw2,
             conv_b_pw2=v_conv_b_pw2, ffn_w_up=v_ffn_w_up, ffn_w_dw=v_ffn_w_dw, ffn_w_down=v_ffn_w_down)
    d = x.shape[-1]

    w_qkv_t, w_o_t, small = _all_gather([_row_shard(attn_w_qkv, True), _row_shard(attn_w_o, True),
                                         _pack([w[n] for n in SMALL], 128, 8)], "gather_first_weights")
    w_qkv_t, w_o_t = w_qkv_t.reshape(-1, d), w_o_t.reshape(d, -1)
    sm = {n: _join_shards(stacked, SMALL_AXIS[n])
          for n, stacked in zip(SMALL, _unpack(small, [w[n].shape for n in SMALL]))}
    late = {n: t for n, t in MATMUL_WEIGHTS.items() if n != "attn_w_qkv"}
    shares = [_row_shard(w[n], t) for n, t in late.items()]
    rows = [s_.shape[0] for s_ in shares]
    late_share = jnp.concatenate(shares, axis=0)
    send_sems, recv_sems, share_thru, land_thru, tie = _gather_start(late_share)
    me = 4 * lax.axis_index("x") + 2 * lax.axis_index("y") + lax.axis_index("c")

    def late_weights(after):
        big = _gather_wait(send_sems, recv_sems, share_thru, land_thru, after)
        big = lax.dynamic_update_slice(big, late_share[None], (me, 0, 0))
        whole, r0 = {}, 0
        for n, nr in zip(late, rows):
            layers = w[n].shape[0]
            seg = big[:, r0:r0 + nr].reshape(N_DEV, layers, nr // layers, d)
            whole[n] = [seg[:, l_].reshape(-1, d) for l_ in range(layers)]
            r0 += nr
        return dict(w_pw1_t=whole["conv_w_pw1"][0], w_pw2=whole["conv_w_pw2"][0], w_up_t=whole["ffn_w_up"],
                    w_down=whole["ffn_w_down"])

    p = dict(norm_g=sm["norm_g"].reshape(-1, d), w_qkv_t=w_qkv_t, w_o_t=w_o_t, b_pw1=sm["conv_b_pw1"],
             conv_w_dw=sm["conv_w_dw"][0], conv_b_dw=sm["conv_b_dw"], ln_g=sm["conv_ln_g"], ln_b=sm["conv_ln_b"],
             b_pw2=sm["conv_b_pw2"], ffn_w_dw=sm["ffn_w_dw"], ffn_b_dw=[ffn_b_dw[0:1], ffn_b_dw[1:2]])

    exchange = _GradExchange()
    loss, grad_x, g = _local_step(x[0], positions.reshape(-1, 1), loss_target[0], p, tie, late_weights, exchange)
    loss = lax.psum(loss[0, 0], ("x", "y", "c"))
    gsmall = dict(norm_g=g["norm_g"].reshape(norm_g.shape[0], 4, -1), conv_b_pw1=g["b_pw1"], conv_w_dw=g["conv_w_dw"][None],
                  conv_b_dw=g["conv_b_dw"], conv_ln_g=g["ln_g"], conv_ln_b=g["ln_b"], conv_b_pw2=g["b_pw2"], ffn_w_dw=g["ffn_w_dw"])
    small_contrib = jnp.concatenate([_split_shards(gsmall[n], SMALL_AXIS[n]).reshape(N_DEV, -1) for n in SMALL], axis=1)
    srows = small.shape[1]
    small_contrib = jnp.pad(small_contrib, ((0, 0), (0, srows * 128 - small_contrib.shape[1]))).reshape(1, N_DEV, srows, 128)
    exchange.advance(grad_x)
    small_sums = _rs_chips([_rs_pair_add(small_contrib, _rs_sibling([small_contrib])[0], exchange.core, F32)])[0]

    outs = {}

    def update(n, reduced):
        gsum = jnp.swapaxes(reduced, 1, 2) if n == "attn_w_o" or MATMUL_WEIGHTS.get(n) else reduced
        outs[n] = (gsum, *_adamw(gsum, w[n], m[n], v[n], "adamw"))

    (s_up1, s_down1), (s_pw1, s_pw2), (s_up0, s_down0) = exchange.results()[:3]
    update("conv_w_pw1", s_pw1)
    update("conv_w_pw2", s_pw2)
    update("ffn_w_up", jnp.concatenate([s_up0, s_up1], axis=0))
    update("ffn_w_down", jnp.concatenate([s_down0, s_down1], axis=0))
    sshapes = [w[n].shape for n in SMALL]
    souts = _sum_adamw(small_sums[0], *[_pack([t[n] for n in SMALL], 128, 8) for t in (w, m, v)], name="sum_adamw_small")
    for n, vals in zip(SMALL, zip(*[_unpack(o, sshapes) for o in souts])):
        outs[n] = vals
    bparts, = _all_gather([_pack([g["ffn_b_dw"]], 128, 8)], "gather_bias_grads")
    bouts = _sum_adamw(bparts, *[_pack([t], 128, 8) for t in (ffn_b_dw, m_ffn_b_dw, v_ffn_b_dw)], name="sum_adamw_bias")
    outs["ffn_b_dw"] = tuple(_unpack(o, [ffn_b_dw.shape])[0] for o in bouts)
    done = [outs[n][1][0, :8, :128] for n in ("conv_w_pw1", "conv_w_pw2", "ffn_w_up", "ffn_w_down")]
    exchange.advance(sum(done) + bouts[1][:8] + souts[1][:8])
    s_qkv, s_wo = exchange.results()[3]
    update("attn_w_qkv", s_qkv)
    update("attn_w_o", s_wo)

    order = ("norm_g", "attn_w_qkv", "attn_w_o", "conv_w_pw1", "conv_b_pw1", "conv_w_dw", "conv_b_dw", "conv_ln_g",
             "conv_ln_b", "conv_w_pw2", "conv_b_pw2", "ffn_w_up", "ffn_w_dw", "ffn_b_dw", "ffn_w_down")
    return (loss, grad_x[None], *[outs[n][0] for n in order], *[outs[n][1] for n in order],
            *[outs[n][2] for n in order], *[outs[n][3] for n in order])
```

```python
import math

import numpy as np
import jax
import jax.numpy as jnp
from jax import lax
from jax.experimental import pallas as pl
from jax.experimental.pallas import tpu as pltpu

F32 = jnp.float32
BF16 = jnp.bfloat16
EPS = 1e-6
N_DEV = 8
HEAD_DIM = 64
GROUP_WIDTH = 512
DILATIONS = (1, 4, 16)
SPAN = 128
ROT_DIM = 16
ROPE_THETA = 500000.0
CONV_KERNEL = 31
CONV_HALO = 32
FFN_CONV = 3
ADAM_LR, ADAM_B1, ADAM_B2, ADAM_EPS, ADAM_WD, ADAM_STEP = 0.001, 0.9, 0.999, 1e-08, 0.01, 10
VMEM_LIMIT_BYTES = 56 * 1024 * 1024
MESH = pl.DeviceIdType.MESH
ANY = pl.BlockSpec(memory_space=pl.ANY)
NT = (((1,), (1,)), ((), ()))
TN = (((0,), (0,)), ((), ()))


def _params(*sem):
    return pltpu.CompilerParams(dimension_semantics=sem, vmem_limit_bytes=VMEM_LIMIT_BYTES)


def _sigmoid(v):
    return pl.reciprocal(1.0 + jnp.exp(-v), approx=True)


def _full(shape):
    return pl.BlockSpec(shape, lambda *_: (0,) * len(shape))


def _rows(tm, width):
    return pl.BlockSpec((tm, width), lambda i, *_: (i, 0))


def _tile(n, *multiples_of):
    for t in (1408, 1024, 512, 384, 256, 128):
        if n % t == 0 and all(o % t == 0 for o in multiples_of):
            return t
    raise ValueError((n, multiples_of))


def _all_gather(shards, name):
    n = len(shards)

    def body(*refs):
        x_refs, out_refs, (send_sems, recv_sems, local_sems) = refs[:n], refs[n:2 * n], refs[2 * n:]
        x, y, c = lax.axis_index("x"), lax.axis_index("y"), lax.axis_index("c")
        me, sibling = (x, y, c), (x, y, 1 - c)
        chips = [(1 - x, y), (x, 1 - y), (1 - x, 1 - y)]

        def rows(w, px, py, pc):
            return out_refs[w].at[4 * px + 2 * py + pc]

        def copy(w, k, block, to, src=None):
            return pltpu.make_async_remote_copy(
                src_ref=rows(w, *block) if src is None else src, dst_ref=rows(w, *block),
                send_sem=send_sems.at[7 * w + k], recv_sem=recv_sems.at[7 * w + k], device_id=to, device_id_type=MESH)

        every = range(n)
        mine = [pltpu.make_async_copy(x_refs[w], rows(w, *me), local_sems.at[w]) for w in every]
        first = [copy(w, 0, me, sibling, src=x_refs[w]) for w in every]
        first += [copy(w, 1 + j, me, (*chip, c), src=x_refs[w]) for w in every for j, chip in enumerate(chips)]
        for cp in mine + first:
            cp.start()
        passed = []
        for j, chip in enumerate(chips):
            for w in every:
                copy(w, 1 + j, (*chip, c), me).wait_recv()
                passed.append(copy(w, 4 + j, (*chip, c), sibling))
                passed[-1].start()
        for w in every:
            copy(w, 0, sibling, me).wait_recv()
            for j, chip in enumerate(chips):
                copy(w, 4 + j, (*chip, 1 - c), me).wait_recv()
        for cp in first + passed:
            cp.wait_send()
        for cp in mine:
            cp.wait()

    return pl.pallas_call(
        body, name=name, out_shape=[jax.ShapeDtypeStruct((N_DEV,) + s_.shape, s_.dtype) for s_ in shards],
        in_specs=[ANY] * n, out_specs=[ANY] * n,
        scratch_shapes=[pltpu.SemaphoreType.DMA((7 * n,)), pltpu.SemaphoreType.DMA((7 * n,)), pltpu.SemaphoreType.DMA((n,))],
    )(*shards)


HBM = pl.BlockSpec(memory_space=pltpu.HBM)
SEM = pl.BlockSpec(memory_space=pltpu.SEMAPHORE)
SIDE_EFFECT = pltpu.CompilerParams(has_side_effects=pltpu.SideEffectType.DATAFLOW_SIDE_EFFECTING)


def _gather_start(shard):
    r, c_ = shard.shape

    def body(x_ref, land_ref, send_sems, recv_sems, x_thru, land_thru, token):
        x, y, c = lax.axis_index("x"), lax.axis_index("y"), lax.axis_index("c")
        me = 4 * x + 2 * y + c
        for k in range(1, N_DEV):
            peer = (1 - x if k & 4 else x, 1 - y if k & 2 else y, 1 - c if k & 1 else c)
            pltpu.make_async_remote_copy(src_ref=x_ref, dst_ref=land_ref.at[me], send_sem=send_sems.at[k - 1],
                                         recv_sem=recv_sems.at[k - 1], device_id=peer, device_id_type=MESH).start()
        token[...] = jnp.zeros_like(token)

    land = pltpu.with_memory_space_constraint(lax.empty((N_DEV, r, c_), shard.dtype), pltpu.HBM)
    return pl.pallas_call(
        body, name="gather_late_weights_start",
        out_shape=(pltpu.SemaphoreType.DMA((N_DEV - 1,)), pltpu.SemaphoreType.DMA((N_DEV - 1,)),
                   pltpu.HBM(shard.shape, shard.dtype), pltpu.HBM((N_DEV, r, c_), shard.dtype),
                   jax.ShapeDtypeStruct((8, 128), F32)),
        in_specs=(HBM, HBM), out_specs=(SEM, SEM, HBM, HBM, pl.BlockSpec(memory_space=pltpu.VMEM)),
        input_output_aliases={0: 2, 1: 3}, compiler_params=SIDE_EFFECT,
    )(pltpu.with_memory_space_constraint(shard, pltpu.HBM), land)


def _gather_wait(send_sems, recv_sems, shard_thru, land_thru, after):
    def body(x_ref, land_ref, send_sems, recv_sems, after_ref, x_dead, got_ref):
        x, y, c = lax.axis_index("x"), lax.axis_index("y"), lax.axis_index("c")
        for k in range(N_DEV - 1):
            copy = pltpu.make_async_remote_copy(src_ref=x_ref, dst_ref=land_ref.at[0], send_sem=send_sems.at[k],
                                                recv_sem=recv_sems.at[k], device_id=(x, y, c), device_id_type=MESH)
            copy.wait_send()
            copy.wait_recv()

    return pl.pallas_call(
        body, name="gather_late_weights_wait",
        out_shape=(pltpu.HBM(shard_thru.shape, shard_thru.dtype), pltpu.HBM(land_thru.shape, land_thru.dtype)),
        in_specs=(HBM, HBM, SEM, SEM, ANY), out_specs=(HBM, HBM), input_output_aliases={0: 0, 1: 1},
        compiler_params=SIDE_EFFECT,
    )(shard_thru, land_thru, send_sems, recv_sems, after)[1]


def _hbm(a):
    return pltpu.with_memory_space_constraint(a, pltpu.HBM)


def _exchange_start(name, arrays, lands, plan, ncopies):
    n = len(arrays)

    def body(*refs):
        send_sems, recv_sems, token = refs[2 * n], refs[2 * n + 1], refs[-1]
        x, y, c = lax.axis_index("x"), lax.axis_index("y"), lax.axis_index("c")
        for k, (src, dst, peer) in enumerate(plan(x, y, c, refs[:n], refs[n:2 * n])):
            pltpu.make_async_remote_copy(src_ref=src, dst_ref=dst, send_sem=send_sems.at[k], recv_sem=recv_sems.at[k],
                                         device_id=peer, device_id_type=MESH).start()
        token[...] = jnp.zeros_like(token)

    both = list(arrays) + list(lands)
    outs = pl.pallas_call(
        body, name=name,
        out_shape=(pltpu.SemaphoreType.DMA((ncopies,)), pltpu.SemaphoreType.DMA((ncopies,)),
                   *[pltpu.HBM(a.shape, a.dtype) for a in both], jax.ShapeDtypeStruct((8, 128), F32)),
        in_specs=(HBM,) * (2 * n), out_specs=(SEM, SEM) + (HBM,) * (2 * n) + (pl.BlockSpec(memory_space=pltpu.VMEM),),
        input_output_aliases={i: 2 + i for i in range(2 * n)}, compiler_params=SIDE_EFFECT,
    )(*[_hbm(a) for a in both])
    return outs[0], outs[1], list(outs[2:2 + n]), list(outs[2 + n:2 + 2 * n]), outs[-1]


def _exchange_wait(name, send_sems, recv_sems, arrays, lands, plan, after):
    n = len(arrays)

    def body(*refs):
        send_sems, recv_sems = refs[2 * n], refs[2 * n + 1]
        x, y, c = lax.axis_index("x"), lax.axis_index("y"), lax.axis_index("c")
        for k, (src, dst, peer) in enumerate(plan(x, y, c, refs[:n], refs[n:2 * n])):
            copy = pltpu.make_async_remote_copy(src_ref=src, dst_ref=dst, send_sem=send_sems.at[k], recv_sem=recv_sems.at[k],
                                                device_id=peer, device_id_type=MESH)
            copy.wait_send()
            copy.wait_recv()

    both = list(arrays) + list(lands)
    outs = pl.pallas_call(
        body, name=name, out_shape=tuple(pltpu.HBM(a.shape, a.dtype) for a in both),
        in_specs=(HBM,) * (2 * n) + (SEM, SEM, ANY), out_specs=(HBM,) * (2 * n),
        input_output_aliases={i: i for i in range(2 * n)}, compiler_params=SIDE_EFFECT,
    )(*both, send_sems, recv_sems, after)
    return list(outs[:n]), list(outs[n:])


def _sibling_plan(x, y, c, g_refs, land_refs):
    return [(g.at[:, 2 * q + (1 - c)], o.at[:, q], (x, y, 1 - c)) for g, o in zip(g_refs, land_refs) for q in range(4)]


def _chips_plan(x, y, c, p_refs, land_refs):
    chips = [(1 - x, y), (x, 1 - y), (1 - x, 1 - y)]
    return [(p_.at[:, 2 * qx + qy], o.at[:, 2 * x + y], (qx, qy, c)) for p_, o in zip(p_refs, land_refs) for qx, qy in chips]


class _GradExchange:
    def __init__(self):
        self.core = lax.axis_index("c").astype(jnp.int32).reshape(1)
        self.chip = 2 * lax.axis_index("x") + lax.axis_index("y")
        self.groups = []

    def submit(self, tag, arrays, dtypes):
        arrays = [a.reshape(a.shape[0], N_DEV, a.shape[1] // N_DEV, a.shape[2]) for a in arrays]
        lands = [lax.empty((a.shape[0], 4) + a.shape[2:], a.dtype) for a in arrays]
        send, recv, arrays, lands, token = _exchange_start(f"rs_pair_start_{tag}", arrays, lands, _sibling_plan, 4 * len(arrays))
        self.groups.append(dict(tag=tag, stage=1, sems=(send, recv), arrays=arrays, lands=lands, dtypes=dtypes))
        return token

    def advance(self, after):
        token = None
        for g in self.groups:
            if g["stage"] == 1:
                arrays, got = _exchange_wait(f"rs_pair_wait_{g['tag']}", *g["sems"], g["arrays"], g["lands"], _sibling_plan, after)
                parts = [_rs_pair_add(a, b, self.core, dt) for a, b, dt in zip(arrays, got, g["dtypes"])]
                lands = [lax.empty(p_.shape, p_.dtype) for p_ in parts]
                send, recv, parts, lands, tok = _exchange_start(f"rs_chip_start_{g['tag']}", parts, lands, _chips_plan, 3 * len(parts))
                g.update(stage=2, sems=(send, recv), arrays=parts, lands=lands)
                token = tok if token is None else token + tok
            elif g["stage"] == 2:
                parts, lands = _exchange_wait(f"rs_chip_wait_{g['tag']}", *g["sems"], g["arrays"], g["lands"], _chips_plan, after)
                sums = []
                for p_, land in zip(parts, lands):
                    l, _, r, c_ = p_.shape
                    own = lax.dynamic_slice(p_, (0, self.chip, 0, 0), (l, 1, r, c_))
                    sums.append(_sum_parts(lax.dynamic_update_slice(land, own, (0, self.chip, 0, 0)), "sum_chips"))
                g.update(stage=3, sums=sums)
        return token

    def results(self):
        return [g.get("sums") for g in self.groups]


def _with_rows(g, n):
    return jax.ShapeDtypeStruct((g.shape[0], n) + tuple(g.shape[2:]), g.dtype)


def _rs_sibling(gs):
    n = len(gs)

    def body(*refs):
        g_refs, o_refs, (send_sems, recv_sems) = refs[:n], refs[n:2 * n], refs[2 * n:]
        x, y, c = lax.axis_index("x"), lax.axis_index("y"), lax.axis_index("c")
        copies = [pltpu.make_async_remote_copy(
            src_ref=g_refs[w].at[:, 2 * q + (1 - c)], dst_ref=o_refs[w].at[:, q], send_sem=send_sems.at[4 * w + q],
            recv_sem=recv_sems.at[4 * w + q], device_id=(x, y, 1 - c), device_id_type=MESH)
            for w in range(n) for q in range(4)]
        for cp in copies:
            cp.start()
        for cp in copies:
            cp.wait_recv()
        for cp in copies:
            cp.wait_send()

    return pl.pallas_call(
        body, name="rs_sibling", out_shape=[_with_rows(g, 4) for g in gs],
        in_specs=[ANY] * n, out_specs=[ANY] * n,
        scratch_shapes=[pltpu.SemaphoreType.DMA((4 * n,)), pltpu.SemaphoreType.DMA((4 * n,))],
    )(*gs)


def _rs_pair_add(g, got, core, out_dtype):
    l, _, r, c_ = g.shape

    def body(core_ref, g_ref, got_ref, o_ref):
        o_ref[...] = (g_ref[...].astype(F32) + got_ref[...].astype(F32)).astype(out_dtype)

    blk = (None, None, r, c_)
    return pl.pallas_call(
        body, name="rs_pair_add", out_shape=jax.ShapeDtypeStruct((l, 4, r, c_), out_dtype),
        grid_spec=pltpu.PrefetchScalarGridSpec(
            num_scalar_prefetch=1, grid=(l, 4),
            in_specs=[pl.BlockSpec(blk, lambda i, q, core_ref: (i, 2 * q + core_ref[0], 0, 0)),
                      pl.BlockSpec(blk, lambda i, q, core_ref: (i, q, 0, 0))],
            out_specs=pl.BlockSpec(blk, lambda i, q, core_ref: (i, q, 0, 0))),
        compiler_params=_params("parallel", "parallel"),
    )(core, g, got)


def _rs_chips(parts):
    n = len(parts)

    def body(*refs):
        p_refs, o_refs, (send_sems, recv_sems, local_sems) = refs[:n], refs[n:2 * n], refs[2 * n:]
        x, y, c = lax.axis_index("x"), lax.axis_index("y"), lax.axis_index("c")
        my_chip = 2 * x + y
        chips = [(1 - x, y), (x, 1 - y), (1 - x, 1 - y)]
        local = [pltpu.make_async_copy(p_refs[w].at[:, my_chip], o_refs[w].at[:, my_chip], local_sems.at[w]) for w in range(n)]
        for cp in local:
            cp.start()
        copies = [pltpu.make_async_remote_copy(
            src_ref=p_refs[w].at[:, 2 * qx + qy], dst_ref=o_refs[w].at[:, my_chip], send_sem=send_sems.at[3 * w + k],
            recv_sem=recv_sems.at[3 * w + k], device_id=(qx, qy, c), device_id_type=MESH)
            for w in range(n) for k, (qx, qy) in enumerate(chips)]
        for cp in copies:
            cp.start()
        for cp in copies:
            cp.wait_recv()
        for cp in copies:
            cp.wait_send()
        for cp in local:
            cp.wait()

    return pl.pallas_call(
        body, name="rs_chips", out_shape=[jax.ShapeDtypeStruct(p.shape, p.dtype) for p in parts],
        in_specs=[ANY] * n, out_specs=[ANY] * n,
        scratch_shapes=[pltpu.SemaphoreType.DMA((3 * n,)), pltpu.SemaphoreType.DMA((3 * n,)), pltpu.SemaphoreType.DMA((n,))],
    )(*parts)


def _sum_parts(parts, name):
    l, n, r, c_ = parts.shape

    def body(p_ref, o_ref):
        g = p_ref[0].astype(F32)
        for s in range(1, n):
            g = g + p_ref[s].astype(F32)
        o_ref[...] = g

    return pl.pallas_call(
        body, name=name, out_shape=jax.ShapeDtypeStruct((l, r, c_), F32), grid=(l,),
        in_specs=[pl.BlockSpec((None, n, r, c_), lambda i: (i, 0, 0, 0))],
        out_specs=pl.BlockSpec((None, r, c_), lambda i: (i, 0, 0)), compiler_params=_params("parallel"),
    )(parts)


def _adamw_math(w, g, m, v):
    m = ADAM_B1 * m + (1.0 - ADAM_B1) * g
    v = ADAM_B2 * v + (1.0 - ADAM_B2) * (g * g)
    m_hat = m / (1.0 - ADAM_B1 ** ADAM_STEP)
    v_hat = v / (1.0 - ADAM_B2 ** ADAM_STEP)
    delta = -ADAM_LR * (m_hat / (jnp.sqrt(v_hat) + ADAM_EPS) + ADAM_WD * w)
    return delta, m, v


def _adamw(g, w, m, v, name):
    l, k, n = w.shape
    tk = 256 if k % 256 == 0 else k

    def body(g_ref, w_ref, m_ref, v_ref, d_ref, nm_ref, nv_ref):
        d_ref[...], nm_ref[...], nv_ref[...] = _adamw_math(w_ref[...], g_ref[...], m_ref[...], v_ref[...])

    spec = pl.BlockSpec((None, tk, n), lambda i, j: (i, j, 0))
    return pl.pallas_call(
        body, name=name, out_shape=[jax.ShapeDtypeStruct((l, k, n), F32)] * 3, grid=(l, k // tk),
        in_specs=[spec] * 4, out_specs=[spec] * 3, compiler_params=_params("parallel", "parallel"),
    )(g, w, m, v)


def _sum_adamw(parts, w, m, v, name):
    n, r, c_ = parts.shape

    def body(p_ref, w_ref, m_ref, v_ref, g_ref, d_ref, nm_ref, nv_ref):
        g = p_ref[0]
        for s in range(1, n):
            g = g + p_ref[s]
        g_ref[...] = g
        d_ref[...], nm_ref[...], nv_ref[...] = _adamw_math(w_ref[...], g, m_ref[...], v_ref[...])

    return pl.pallas_call(
        body, name=name, out_shape=[jax.ShapeDtypeStruct((r, c_), F32)] * 4, grid=(1,),
        in_specs=[_full((n, r, c_))] + [_full((r, c_))] * 3, out_specs=[_full((r, c_))] * 4,
        compiler_params=_params("arbitrary"),
    )(parts, w, m, v)


def _rope_tables(pos_col, freq_row):
    s = pos_col.shape[0]
    tm = min(1024, s)

    def body(p_ref, f_ref, o_ref):
        ang = p_ref[...].astype(F32) * f_ref[...]
        lane = lax.broadcasted_iota(jnp.int32, ang.shape, 1) & (HEAD_DIM - 1)
        cs, sn = jnp.cos(ang), jnp.sin(ang)
        o_ref[:, 0:128] = jnp.where(lane < ROT_DIM, cs, 1.0)
        o_ref[:, 128:256] = jnp.where((lane >= ROT_DIM // 2) & (lane < ROT_DIM), sn, 0.0)
        o_ref[:, 256:384] = jnp.where(lane < ROT_DIM // 2, -sn, 0.0)

    return pl.pallas_call(
        body, name="rope_tables", out_shape=jax.ShapeDtypeStruct((s, ROPE_COLS), F32), grid=(s // tm,),
        in_specs=[pl.BlockSpec((tm, 1), lambda i: (i, 0)), _full((1, 128))],
        out_specs=_rows(tm, ROPE_COLS), compiler_params=_params("parallel"),
    )(pos_col, freq_row)


ROPE_COLS = 3 * 128


def _rope_parts(tab, reps=1):
    return [jnp.tile(tab[:, k * 128:(k + 1) * 128], (1, reps)) if reps > 1 else tab[:, k * 128:(k + 1) * 128] for k in range(3)]


def _rope_apply(t, tab):
    w = t.shape[1]
    cos, sin_up, sin_dn = _rope_parts(tab, w // 128)
    return t * cos + pltpu.roll(t, 8, 1) * sin_up + pltpu.roll(t, w - 8, 1) * sin_dn


def _rope_transpose(dr, tab):
    w = dr.shape[1]
    cos, sin_up, sin_dn = _rope_parts(tab, w // 128)
    return dr * cos + pltpu.roll(dr * sin_up, w - 8, 1) + pltpu.roll(dr * sin_dn, 8, 1)


def _norm_matmul(x, g, wt, *, tn, name, bias=None, tm=1024):
    s, d = x.shape
    n = wt.shape[0]
    tm = min(tm, s)

    def body(*refs):
        x_ref, g_ref, w_ref = refs[:3]
        b_ref = refs[3] if bias is not None else None
        h_ref, o_ref = refs[-2:]

        @pl.when(pl.program_id(1) == 0)
        def _():
            xv = x_ref[...]
            r = lax.rsqrt(jnp.mean(xv * xv, axis=-1, keepdims=True) + EPS)
            h_ref[...] = (xv * r * g_ref[...]).astype(BF16)

        acc = lax.dot_general(h_ref[...], w_ref[...], NT, preferred_element_type=F32)
        if b_ref is not None:
            acc = acc + b_ref[...]
        o_ref[...] = acc.astype(BF16)

    in_specs = [_rows(tm, d), _full((1, d)), pl.BlockSpec((tn, d), lambda i, j: (j, 0))]
    args = [x, g, wt]
    if bias is not None:
        in_specs.append(pl.BlockSpec((1, tn), lambda i, j: (0, j)))
        args.append(bias)
    return pl.pallas_call(
        body, name=name,
        out_shape=[jax.ShapeDtypeStruct((s, d), BF16), jax.ShapeDtypeStruct((s, n), BF16)],
        grid=(s // tm, n // tn), in_specs=in_specs,
        out_specs=[_rows(tm, d), pl.BlockSpec((tm, tn), lambda i, j: (i, j))],
        compiler_params=_params("parallel", "arbitrary"),
    )(*args)


def _class_major(tm, dil):
    p = np.zeros((tm, tm), np.float32)
    per = tm // dil
    for r in range(dil):
        for j in range(per):
            p[r * per + j, j * dil + r] = 1.0
    return jnp.asarray(p, dtype=BF16)


def _qkv_proj(x, g, wt, rope, tm=512):
    s, d = x.shape
    n = wt.shape[0]
    gw3 = 3 * GROUP_WIDTH
    tm = min(tm, s)
    assert n == 3 * gw3

    def body(x_ref, g_ref, w_ref, tab_ref, p1_ref, p2_ref, h_ref, o0_ref, o1_ref, o2_ref):
        j = pl.program_id(1)

        @pl.when(j == 0)
        def _():
            xv = x_ref[...]
            r = lax.rsqrt(jnp.mean(xv * xv, axis=-1, keepdims=True) + EPS)
            h_ref[...] = (xv * r * g_ref[...]).astype(BF16)

        acc = lax.dot_general(h_ref[...], w_ref[...], NT, preferred_element_type=F32)

        def store(y):
            yb = y.astype(BF16)
            o0_ref[:, pl.ds(pl.multiple_of(j * GROUP_WIDTH, GROUP_WIDTH), GROUP_WIDTH)] = yb[:, :GROUP_WIDTH]
            for grp, o_ref, p_ref in ((1, o1_ref, p1_ref), (2, o2_ref, p2_ref)):
                dil = DILATIONS[grp]
                per = tm // dil
                yp = jnp.dot(p_ref[...], yb[:, grp * GROUP_WIDTH:(grp + 1) * GROUP_WIDTH],
                             preferred_element_type=F32).astype(BF16)
                for r in range(dil):
                    col = pl.multiple_of(r * gw3 + j * GROUP_WIDTH, GROUP_WIDTH)
                    o_ref[:, pl.ds(col, GROUP_WIDTH)] = yp[r * per:(r + 1) * per, :]

        @pl.when(j < 2)
        def _():
            store(_rope_apply(acc, tab_ref[...]))

        @pl.when(j == 2)
        def _():
            store(acc)

    outs = [jax.ShapeDtypeStruct((s, d), BF16)] + [jax.ShapeDtypeStruct((s // dl, dl * gw3), BF16) for dl in DILATIONS]
    out_specs = [_rows(tm, d)] + [_rows(tm // dl, dl * gw3) for dl in DILATIONS]
    return pl.pallas_call(
        body, name="attn_qkv", out_shape=outs, grid=(s // tm, 3),
        in_specs=[_rows(tm, d), _full((1, d)), pl.BlockSpec((gw3, d), lambda i, j: (j, 0)), _rows(tm, ROPE_COLS)]
        + [_full((tm, tm))] * 2,
        out_specs=out_specs, compiler_params=_params("parallel", "arbitrary"),
    )(x, g, wt, rope, _class_major(tm, DILATIONS[1]), _class_major(tm, DILATIONS[2]))


def _head_masks(rows=SPAN):
    lane = lax.broadcasted_iota(jnp.int32, (rows, 128), 1)
    masks = [lane < HEAD_DIM, lane >= HEAD_DIM]
    lane1 = lax.broadcasted_iota(jnp.int32, (1, 128), 1)
    keep = [jnp.where(lane1 < HEAD_DIM, 1.0, 0.0).astype(BF16), jnp.where(lane1 >= HEAD_DIM, 1.0, 0.0).astype(BF16)]
    return masks, keep


def _band_mask(b):
    row = lax.broadcasted_iota(jnp.int32, (2 * SPAN, 2 * SPAN), 0) & (SPAN - 1)
    col = lax.broadcasted_iota(jnp.int32, (2 * SPAN, 2 * SPAN), 1)
    no_prev = jnp.where(b > 0, 0, 4 * SPAN)
    return ((col < SPAN) & (col >= row + no_prev)) | ((col >= SPAN) & (col - SPAN <= row))


def _attn_fwd(qv, grp, dil):
    l = qv.shape[0]
    s = l * dil
    nb = l // SPAN
    nq = next(n for n in (4, 2, 1) if nb % n == 0)

    def body(q_ref, kp_ref, kc_ref, vp_ref, vc_ref, o_ref, l_ref):
        b = pl.program_id(1)
        masks, keep = _head_masks()
        for qb in range(nq):
            valid = _band_mask(b * nq + qb)
            rows = slice(qb * SPAN, (qb + 1) * SPAN)
            before = slice((qb - 1) * SPAN, qb * SPAN)
            for p in range(GROUP_WIDTH // 128):
                sl = slice(p * 128, (p + 1) * 128)
                qp = q_ref[rows, sl]
                kk = jnp.concatenate([kp_ref[:, sl] if qb == 0 else kc_ref[before, sl], kc_ref[rows, sl]], axis=0)
                vv = jnp.concatenate([vp_ref[:, sl] if qb == 0 else vc_ref[before, sl], vc_ref[rows, sl]], axis=0)
                q2 = jnp.concatenate([qp * keep[0], qp * keep[1]], axis=0)
                sc = lax.dot_general(q2, kk, NT, preferred_element_type=F32) * (HEAD_DIM ** -0.5)
                sc = jnp.where(valid, sc, -1e30)
                mx = jnp.max(sc, axis=-1, keepdims=True)
                pe = jnp.exp(sc - mx)
                den = jnp.sum(pe, axis=-1, keepdims=True)
                out = jnp.dot(pe.astype(BF16), vv, preferred_element_type=F32) / den
                lse = jnp.broadcast_to(mx + jnp.log(den), (2 * SPAN, 128))
                o_ref[rows, sl] = jnp.where(masks[0], out[:SPAN], out[SPAN:]).astype(BF16)
                l_ref[rows, sl] = jnp.where(masks[0], lse[:SPAN], lse[SPAN:])

    blk = (nq * SPAN, GROUP_WIDTH)
    cur = lambda t: pl.BlockSpec(blk, lambda r, b: (b, r * 3 + t))
    prev = lambda t: pl.BlockSpec((SPAN, GROUP_WIDTH), lambda r, b: (jnp.maximum(nq * b - 1, 0), r * 3 + t))
    out = pl.BlockSpec(blk, lambda r, b: (b, r))
    o, lse = pl.pallas_call(
        body, name=f"attn_fwd_g{grp}",
        out_shape=[jax.ShapeDtypeStruct((l, dil * GROUP_WIDTH), BF16), jax.ShapeDtypeStruct((l, dil * GROUP_WIDTH), F32)],
        grid=(dil, nb // nq), in_specs=[cur(0), prev(1), cur(1), prev(2), cur(2)], out_specs=[out, out],
        compiler_params=_params("parallel", "arbitrary"),
    )(qv, qv, qv, qv, qv)
    return o.reshape(s, GROUP_WIDTH), lse.reshape(s, GROUP_WIDTH)


def _resnorm_store(y, x_ref, g_ref, y_ref, xo_ref):
    r = lax.rsqrt(jnp.mean(y * y, axis=-1, keepdims=True) + EPS)
    y_ref[...] = y
    xo_ref[...] = x_ref[...] + y * r * g_ref[...]


def _mix_wo(os_, ls_, wot, x, g, tm=512):
    s, d = x.shape
    gw = wot.shape[1]
    tm = min(tm, s)

    def body(o0, o1, o2, l0, l1, l2, w_ref, x_ref, g_ref, y_ref, xo_ref, mixed_ref, lse_ref):
        a0, a1, a2 = l0[...], l1[...], l2[...]
        mx = jnp.maximum(jnp.maximum(a0, a1), a2)
        e0, e1, e2 = jnp.exp(a0 - mx), jnp.exp(a1 - mx), jnp.exp(a2 - mx)
        den = e0 + e1 + e2
        mixed = (e0 / den) * o0[...].astype(F32) + (e1 / den) * o1[...].astype(F32) + (e2 / den) * o2[...].astype(F32)
        mixed_ref[...] = mixed.astype(BF16)
        lse_ref[...] = mx + jnp.log(den)
        y = lax.dot_general(mixed.astype(BF16), w_ref[...], NT, preferred_element_type=F32)
        _resnorm_store(y, x_ref, g_ref, y_ref, xo_ref)

    return pl.pallas_call(
        body, name="mix_wo",
        out_shape=[jax.ShapeDtypeStruct((s, d), F32), jax.ShapeDtypeStruct((s, d), F32),
                   jax.ShapeDtypeStruct((s, gw), BF16), jax.ShapeDtypeStruct((s, gw), F32)],
        grid=(s // tm,), in_specs=[_rows(tm, gw)] * 6 + [_full((d, gw)), _rows(tm, d), _full((1, d))],
        out_specs=[_rows(tm, d), _rows(tm, d), _rows(tm, gw), _rows(tm, gw)],
        compiler_params=_params("parallel"),
    )(*os_, *ls_, wot, x, g)


def _matmul_resnorm(a, w, x, g, *, name, bias=None, tm=512):
    s, k = a.shape
    d = w.shape[1]
    tm = min(tm, s)

    def body(*refs):
        a_ref, w_ref = refs[:2]
        b_ref = refs[2] if bias is not None else None
        x_ref, g_ref, y_ref, xo_ref = refs[-4:]
        y = jnp.dot(a_ref[...], w_ref[...], preferred_element_type=F32)
        if b_ref is not None:
            y = y + b_ref[...]
        _resnorm_store(y, x_ref, g_ref, y_ref, xo_ref)

    in_specs = [_rows(tm, k), _full((k, d))] + ([_full((1, d))] if bias is not None else []) + [_rows(tm, d), _full((1, d))]
    args = [a, w] + ([bias] if bias is not None else []) + [x, g]
    return pl.pallas_call(
        body, name=name, out_shape=[jax.ShapeDtypeStruct((s, d), F32)] * 2, grid=(s // tm,),
        in_specs=in_specs, out_specs=[_rows(tm, d)] * 2, compiler_params=_params("parallel"),
    )(*args)


FFN_SUB = 256


def _conv3_rows(z_ref, halo_ref, rb, sub, cs, first):
    zc = z_ref[rb * sub:(rb + 1) * sub, cs].astype(F32)
    if rb == 0:
        halo = halo_ref[:, cs].astype(F32) * jnp.where(first, 0.0, 1.0)
    else:
        halo = z_ref[rb * sub - 16:rb * sub, cs].astype(F32)[8:]
    z2, z1 = _conv3_taps(zc, halo)
    return z2, z1, zc


def _conv3_taps(z, halo):
    row = lax.broadcasted_iota(jnp.int32, (8, z.shape[1]), 0)
    h6, h7 = halo[6:7, :], halo[7:8, :]
    r1, r2 = pltpu.roll(z, 1, 0), pltpu.roll(z, 2, 0)
    z1 = jnp.concatenate([jnp.where(row == 0, h7, r1[0:8]), r1[8:]], axis=0)
    z2 = jnp.concatenate([jnp.where(row == 0, h6, jnp.where(row == 1, h7, r2[0:8])), r2[8:]], axis=0)
    return z2, z1


def _ffn_cols(f):
    return _tile(f)


def _lane_chunks(width, fn):
    def step(k, carry):
        fn(pl.ds(pl.multiple_of(k * 128, 128), 128))
        return carry

    lax.fori_loop(0, width // 128, step, 0)


def _ffn_act(z, w_dw, b_dw, tm=1024):
    s, f2 = z.shape
    f = f2 // 2
    tm = min(tm, s)
    sub = min(FFN_SUB, tm)
    tc = _ffn_cols(f)
    nfc = f // tc

    def body(zu, zg, hu, hg, wu, wg, bu, bg, o_ref):
        first = pl.program_id(0) == 0

        def chunk(cs):
            for rb in range(tm // sub):
                def conv(z_ref, h_ref, w_ref, b_ref):
                    z2, z1, zc = _conv3_rows(z_ref, h_ref, rb, sub, cs, first)
                    return w_ref[0:1, cs] * z2 + w_ref[1:2, cs] * z1 + w_ref[2:3, cs] * zc + b_ref[:, cs]

                up, gate = conv(zu, hu, wu, bu), conv(zg, hg, wg, bg)
                o_ref[rb * sub:(rb + 1) * sub, cs] = (gate * _sigmoid(gate) * up).astype(BF16)

        _lane_chunks(tc, chunk)

    hb = tm // 8
    tile = lambda off: pl.BlockSpec((tm, tc), lambda i, j: (i, off + j))
    halo = lambda off: pl.BlockSpec((8, tc), lambda i, j: (jnp.maximum(i * hb - 1, 0), off + j))
    prm = lambda rows, off: pl.BlockSpec((rows, tc), lambda i, j: (0, off + j))
    return pl.pallas_call(
        body, name="ffn_act", out_shape=jax.ShapeDtypeStruct((s, f), BF16), grid=(s // tm, nfc),
        in_specs=[tile(0), tile(nfc), halo(0), halo(nfc), prm(FFN_CONV, 0), prm(FFN_CONV, nfc), prm(1, 0), prm(1, nfc)],
        out_specs=pl.BlockSpec((tm, tc), lambda i, j: (i, j)), compiler_params=_params("parallel", "parallel"),
    )(z, z, z, z, w_dw, w_dw, b_dw, b_dw)


def _shifted_planes(ext_ref):
    rows = ext_ref.shape[1]
    for s in range(1, 8):
        ext_ref[s, 0:rows - 8, :] = ext_ref[0, s:s + rows - 8, :]


def _window(ext_ref, off, tm, cs):
    s = off % 8
    return ext_ref[s, off - s:off - s + tm, cs]


def _conv_taps(ext_ref, w_ref, offs, tm, out_ref):
    def chunk(cs):
        acc = w_ref[0:1, cs] * _window(ext_ref, offs[0], tm, cs)
        for j in range(1, len(offs)):
            acc = acc + w_ref[j:j + 1, cs] * _window(ext_ref, offs[j], tm, cs)
        out_ref[:, cs] = acc

    _lane_chunks(out_ref.shape[1], chunk)


def _glu_planes(ag_ref, halo_ref, ext_ref, first, c):
    hal = halo_ref[...].astype(F32)
    ext_ref[0, 0:CONV_HALO, :] = hal[:, :c] * _sigmoid(hal[:, c:]) * jnp.where(first, 0.0, 1.0)
    ag = ag_ref[...].astype(F32)
    ext_ref[0, CONV_HALO:, :] = ag[:, :c] * _sigmoid(ag[:, c:])
    _shifted_planes(ext_ref)


def _layernorm_stats(u1):
    mu = jnp.mean(u1, axis=-1, keepdims=True)
    cen = u1 - mu
    rstd = lax.rsqrt(jnp.mean(cen * cen, axis=-1, keepdims=True) + EPS)
    return cen * rstd, rstd


def _conv_mid(ag, w_dw, b_dw, ln_g, ln_b, tm=256):
    s, c2 = ag.shape
    c = c2 // 2
    tm = min(tm, s)

    def body(ag_ref, halo_ref, w_ref, b_ref, g_ref, bb_ref, o_ref, u1_ref, ext_ref):
        _glu_planes(ag_ref, halo_ref, ext_ref, pl.program_id(0) == 0, c)
        base = CONV_HALO - (CONV_KERNEL - 1)
        _conv_taps(ext_ref, w_ref, [base + j for j in range(CONV_KERNEL)], tm, u1_ref)
        xh, _ = _layernorm_stats(u1_ref[...] + b_ref[...])
        u2 = xh * g_ref[...] + bb_ref[...]
        o_ref[...] = (u2 * _sigmoid(u2)).astype(BF16)

    hb = tm // CONV_HALO
    return pl.pallas_call(
        body, name="conv_mid", out_shape=[jax.ShapeDtypeStruct((s, c), BF16), jax.ShapeDtypeStruct((s, c), F32)], grid=(s // tm,),
        in_specs=[_rows(tm, c2), pl.BlockSpec((CONV_HALO, c2), lambda i: (jnp.maximum(i * hb - 1, 0), 0)),
                  _full((CONV_KERNEL, c)), _full((1, c)), _full((1, c)), _full((1, c))],
        out_specs=[_rows(tm, c), _rows(tm, c)], scratch_shapes=[pltpu.VMEM((8, CONV_HALO + tm, c), F32)],
        compiler_params=_params("arbitrary"),
    )(ag, ag, w_dw, b_dw, ln_g, ln_b)


def _loss_grad(xo, target, tm=1024):
    s, d = xo.shape
    tm = min(tm, s)

    def body(x_ref, t_ref, dx_ref, loss_ref):
        @pl.when(pl.program_id(0) == 0)
        def _():
            loss_ref[...] = jnp.zeros_like(loss_ref)

        err = x_ref[...] - t_ref[...]
        dx_ref[...] = err * (1.0 / d)
        loss_ref[...] += 0.5 * jnp.sum(jnp.mean(err * err, axis=-1, keepdims=True))

    return pl.pallas_call(
        body, name="loss_grad", out_shape=[jax.ShapeDtypeStruct((s, d), F32), jax.ShapeDtypeStruct((1, 128), F32)],
        grid=(s // tm,), in_specs=[_rows(tm, d)] * 2, out_specs=[_rows(tm, d), _full((1, 128))],
        compiler_params=_params("arbitrary"),
    )(xo, target)


def _postnorm_bwd(y, g, dxo, *, name, with_bias_grad=False, tm=1024):
    s, d = y.shape
    tm = min(tm, s)

    def body(y_ref, g_ref, dx_ref, dy_ref, dg_ref, *rest):
        @pl.when(pl.program_id(0) == 0)
        def _():
            dg_ref[...] = jnp.zeros_like(dg_ref)
            for r_ in rest:
                r_[...] = jnp.zeros_like(r_)

        yv, dxo_v = y_ref[...], dx_ref[...]
        r = lax.rsqrt(jnp.mean(yv * yv, axis=-1, keepdims=True) + EPS)
        yh = yv * r
        dyh = dxo_v * g_ref[...]
        dy = r * (dyh - yh * jnp.mean(dyh * yh, axis=-1, keepdims=True))
        dy_ref[...] = dy.astype(BF16)
        dg_ref[...] += jnp.sum(dxo_v * yh, axis=0, keepdims=True)
        for r_ in rest:
            r_[...] += jnp.sum(dy, axis=0, keepdims=True)

    nacc = 2 if with_bias_grad else 1
    return pl.pallas_call(
        body, name=name, out_shape=[jax.ShapeDtypeStruct((s, d), BF16)] + [jax.ShapeDtypeStruct((1, d), F32)] * nacc,
        grid=(s // tm,), in_specs=[_rows(tm, d), _full((1, d)), _rows(tm, d)],
        out_specs=[_rows(tm, d)] + [_full((1, d))] * nacc, compiler_params=_params("arbitrary"),
    )(y, g, dxo)


def _matmul(gmat, w, *, name, out_dtype, transposed_w, tm=512):
    s, k = gmat.shape
    n = w.shape[0] if transposed_w else w.shape[1]
    tm = min(tm, s)

    def body(g_ref, w_ref, o_ref):
        if transposed_w:
            acc = lax.dot_general(g_ref[...], w_ref[...], NT, preferred_element_type=F32)
        else:
            acc = jnp.dot(g_ref[...], w_ref[...], preferred_element_type=F32)
        o_ref[...] = acc.astype(out_dtype)

    return pl.pallas_call(
        body, name=name, out_shape=jax.ShapeDtypeStruct((s, n), out_dtype), grid=(s // tm,),
        in_specs=[_rows(tm, k), _full(w.shape)], out_specs=_rows(tm, n), compiler_params=_params("parallel"),
    )(gmat, w)


def _matmul_prenorm_bwd(pieces, wt, x, g, dres, *, name, tm=256):
    s, d = x.shape
    tm = min(tm, s)
    np_ = len(pieces)

    def body(*refs):
        p_refs, w_refs = refs[:np_], refs[np_:2 * np_]
        x_ref, g_ref, r_ref, dx_ref, dg_ref = refs[2 * np_:]

        @pl.when(pl.program_id(0) == 0)
        def _():
            dg_ref[...] = jnp.zeros_like(dg_ref)

        dh = None
        for p_ref, w_ref in zip(p_refs, w_refs):
            t = jnp.dot(p_ref[...], w_ref[...], preferred_element_type=F32)
            dh = t if dh is None else dh + t
        xv = x_ref[...]
        r = lax.rsqrt(jnp.mean(xv * xv, axis=-1, keepdims=True) + EPS)
        xh = xv * r
        dyh = dh * g_ref[...]
        dx_ref[...] = r_ref[...] + r * (dyh - xh * jnp.mean(dyh * xh, axis=-1, keepdims=True))
        dg_ref[...] += jnp.sum(dh * xh, axis=0, keepdims=True)

    in_specs = []
    for _, c0, kc, _ in pieces:
        assert c0 % kc == 0
        in_specs.append(pl.BlockSpec((tm, kc), lambda i, _b=c0 // kc: (i, _b)))
    for _, _, kc, r0 in pieces:
        assert r0 % kc == 0
        in_specs.append(pl.BlockSpec((kc, d), lambda i, _b=r0 // kc: (_b, 0)))
    in_specs += [_rows(tm, d), _full((1, d)), _rows(tm, d)]
    return pl.pallas_call(
        body, name=name, out_shape=[jax.ShapeDtypeStruct((s, d), F32), jax.ShapeDtypeStruct((1, d), F32)],
        grid=(s // tm,), in_specs=in_specs, out_specs=[_rows(tm, d), _full((1, d))],
        compiler_params=_params("arbitrary"),
    )(*[p[0] for p in pieces], *[wt] * np_, x, g, dres)


def _weight_grad(a, gmat, *, name, a_col0=0, ka=None, out=None, out_shape=None, layer=0, row0=0, ts=1024):
    s = a.shape[0]
    ka = a.shape[1] if ka is None else ka
    n = gmat.shape[1]
    ts = min(ts, s)
    tka = _tile(ka, a_col0, row0)
    shape = out.shape if out is not None else out_shape
    nsteps = s // ts

    def body(a_ref, g_ref, *rest):
        o_ref, acc_ref = rest[-2:]
        i = pl.program_id(1)

        @pl.when(i == 0)
        def _():
            acc_ref[...] = jnp.zeros_like(acc_ref)

        acc_ref[...] += lax.dot_general(a_ref[...], g_ref[...], TN, preferred_element_type=F32)

        @pl.when(i == nsteps - 1)
        def _():
            o_ref[...] = acc_ref[...].astype(BF16)

    in_specs = [pl.BlockSpec((ts, tka), lambda k, i: (i, a_col0 // tka + k)), pl.BlockSpec((ts, n), lambda k, i: (i, 0))]
    args = [a, gmat]
    aliases = {}
    if out is not None:
        in_specs.append(ANY)
        args.append(out)
        aliases = {2: 0}
    return pl.pallas_call(
        body, name=name, out_shape=jax.ShapeDtypeStruct(shape, BF16), grid=(ka // tka, nsteps), in_specs=in_specs,
        out_specs=pl.BlockSpec((None, tka, n), lambda k, i: (layer, row0 // tka + k, 0)),
        scratch_shapes=[pltpu.VMEM((tka, n), F32)],
        input_output_aliases=aliases, compiler_params=_params("parallel", "arbitrary"),
    )(*args)


def _ffn_act_bwd(z, dact, w_dw, b_dw, tm=512):
    s, f2 = z.shape
    f = f2 // 2
    tm = min(tm, s)
    sub = min(FFN_SUB // 2, tm)
    tc = _ffn_cols(f)
    nfc = f // tc

    def body(zu, zg, hu, hg, wu, wg, bu, bg, da_ref, du_ref, dgt_ref, dbu_ref, dbg_ref, dwu_ref, dwg_ref):
        i = pl.program_id(1)

        @pl.when(i == 0)
        def _():
            for r_ in (dbu_ref, dbg_ref, dwu_ref, dwg_ref):
                r_[...] = jnp.zeros_like(r_)

        def chunk(cs):
            for rb in range(tm // sub):
                rows = slice(rb * sub, (rb + 1) * sub)

                def conv(z_ref, h_ref, w_ref, b_ref):
                    taps = _conv3_rows(z_ref, h_ref, rb, sub, cs, i == 0)
                    return taps, w_ref[0:1, cs] * taps[0] + w_ref[1:2, cs] * taps[1] + w_ref[2:3, cs] * taps[2] + b_ref[:, cs]

                taps_u, up = conv(zu, hu, wu, bu)
                taps_g, gate = conv(zg, hg, wg, bg)
                da = da_ref[rows, cs].astype(F32)
                sg = _sigmoid(gate)
                d_up = da * (gate * sg)
                d_gate = da * up * (sg * (1.0 + gate * (1.0 - sg)))
                du_ref[rows, cs] = d_up.astype(BF16)
                dgt_ref[rows, cs] = d_gate.astype(BF16)
                for dv, taps, db_ref, dw_ref in ((d_up, taps_u, dbu_ref, dwu_ref), (d_gate, taps_g, dbg_ref, dwg_ref)):
                    db_ref[:, cs] += jnp.sum(dv, axis=0, keepdims=True)
                    for k_, tap in enumerate(taps):
                        dw_ref[k_:k_ + 1, cs] += jnp.sum(dv * tap, axis=0, keepdims=True)

        _lane_chunks(tc, chunk)

    hb = tm // 8
    tile = lambda off: pl.BlockSpec((tm, tc), lambda j, i: (i, off + j))
    halo = lambda off: pl.BlockSpec((8, tc), lambda j, i: (jnp.maximum(i * hb - 1, 0), off + j))
    prm = lambda rows, off: pl.BlockSpec((rows, tc), lambda j, i: (0, off + j))
    acc = lambda rows: pl.BlockSpec((rows, tc), lambda j, i: (0, j))
    return pl.pallas_call(
        body, name="ffn_act_bwd",
        out_shape=[jax.ShapeDtypeStruct((s, f), BF16)] * 2 + [jax.ShapeDtypeStruct((1, f), F32)] * 2
        + [jax.ShapeDtypeStruct((FFN_CONV, f), F32)] * 2,
        grid=(nfc, s // tm),
        in_specs=[tile(0), tile(nfc), halo(0), halo(nfc), prm(FFN_CONV, 0), prm(FFN_CONV, nfc), prm(1, 0), prm(1, nfc), tile(0)],
        out_specs=[tile(0), tile(0), acc(1), acc(1), acc(FFN_CONV), acc(FFN_CONV)],
        compiler_params=_params("parallel", "arbitrary"),
    )(z, z, z, z, w_dw, w_dw, b_dw, b_dw, dact)


def _conv3_transpose(dug, w_dw, col0, tm=1024):
    s, f = dug.shape
    tm = min(tm, s)
    sub = min(FFN_SUB, tm)
    nsub = tm // sub
    tc = _ffn_cols(f)
    nfc = f // tc
    nrow = s // tm
    off = col0 // tc

    def body(d_ref, n_ref, w_ref, o_ref):
        keep_next = jnp.where(pl.program_id(0) == nrow - 1, 0.0, 1.0)

        def chunk(cs):
            for rb in range(nsub):
                rows = slice(rb * sub, (rb + 1) * sub)
                dv = d_ref[rows, cs].astype(F32)
                if rb == nsub - 1:
                    nxt = n_ref[:, cs].astype(F32) * keep_next
                else:
                    nxt = d_ref[(rb + 1) * sub:(rb + 1) * sub + 16, cs].astype(F32)[:8]
                n0, n1 = nxt[0:1, :], nxt[1:2, :]
                row = lax.broadcasted_iota(jnp.int32, (8, dv.shape[1]), 0)
                r1, r2 = pltpu.roll(dv, sub - 1, 0), pltpu.roll(dv, sub - 2, 0)
                d1 = jnp.concatenate([r1[:sub - 8], jnp.where(row == 7, n0, r1[sub - 8:])], axis=0)
                d2 = jnp.concatenate([r2[:sub - 8], jnp.where(row == 7, n1, jnp.where(row == 6, n0, r2[sub - 8:]))], axis=0)
                o_ref[rows, cs] = (w_ref[2:3, cs] * dv + w_ref[1:2, cs] * d1 + w_ref[0:1, cs] * d2).astype(BF16)

        _lane_chunks(tc, chunk)

    hb = tm // 8
    return pl.pallas_call(
        body, name="conv3_transpose", out_shape=jax.ShapeDtypeStruct((s, f), BF16), grid=(nrow, nfc),
        in_specs=[pl.BlockSpec((tm, tc), lambda i, j: (i, j)),
                  pl.BlockSpec((8, tc), lambda i, j: (jnp.minimum((i + 1) * hb, s // 8 - 1), j)),
                  pl.BlockSpec((FFN_CONV, tc), lambda i, j: (0, off + j))],
        out_specs=pl.BlockSpec((tm, tc), lambda i, j: (i, j)), compiler_params=_params("parallel", "parallel"),
    )(dug, dug, w_dw)


def _conv_mid_bwd(ag, u1, du3, b_dw, ln_g, ln_b, tm=256):
    s, c2 = ag.shape
    c = c2 // 2
    tm = min(tm, s)

    def body(ag_ref, halo_ref, u1in_ref, du_ref, b_ref, g_ref, bb_ref, o_ref, dlg_ref, dlb_ref, db_ref, dw_ref, ext_ref, u1_ref):
        @pl.when(pl.program_id(0) == 0)
        def _():
            for r_ in (dlg_ref, dlb_ref, db_ref, dw_ref):
                r_[...] = jnp.zeros_like(r_)

        _glu_planes(ag_ref, halo_ref, ext_ref, pl.program_id(0) == 0, c)
        xh, rstd = _layernorm_stats(u1in_ref[...] + b_ref[...])
        u2 = xh * g_ref[...] + bb_ref[...]
        sg = _sigmoid(u2)
        du2 = du_ref[...] * (sg * (1.0 + u2 * (1.0 - sg)))
        dlg_ref[...] += jnp.sum(du2 * xh, axis=0, keepdims=True)
        dlb_ref[...] += jnp.sum(du2, axis=0, keepdims=True)
        dxh = du2 * g_ref[...]
        du1 = rstd * (dxh - jnp.mean(dxh, axis=-1, keepdims=True) - xh * jnp.mean(dxh * xh, axis=-1, keepdims=True))
        o_ref[...] = du1.astype(BF16)
        db_ref[...] += jnp.sum(du1, axis=0, keepdims=True)
        u1_ref[...] = du1
        base = CONV_HALO - (CONV_KERNEL - 1)

        def chunk(cs):
            dc = u1_ref[:, cs]
            for j in range(CONV_KERNEL):
                dw_ref[j:j + 1, cs] += jnp.sum(dc * _window(ext_ref, base + j, tm, cs), axis=0, keepdims=True)

        _lane_chunks(c, chunk)

    hb = tm // CONV_HALO
    vec = _full((1, c))
    return pl.pallas_call(
        body, name="conv_mid_bwd",
        out_shape=[jax.ShapeDtypeStruct((s, c), BF16)] + [jax.ShapeDtypeStruct((1, c), F32)] * 3
        + [jax.ShapeDtypeStruct((CONV_HALO, c), F32)],
        grid=(s // tm,),
        in_specs=[_rows(tm, c2), pl.BlockSpec((CONV_HALO, c2), lambda i: (jnp.maximum(i * hb - 1, 0), 0)), _rows(tm, c),
                  _rows(tm, c), vec, vec, vec],
        out_specs=[_rows(tm, c), vec, vec, vec, _full((CONV_HALO, c))],
        scratch_shapes=[pltpu.VMEM((8, CONV_HALO + tm, c), F32), pltpu.VMEM((tm, c), F32)],
        compiler_params=_params("arbitrary"),
    )(ag, ag, u1, du3, b_dw, ln_g, ln_b)


def _glu_conv_bwd(du1, ag, w_dw, tm=256):
    s, c = du1.shape
    tm = min(tm, s)
    nrow = s // tm

    def body(d_ref, n_ref, ag_ref, w_ref, o_ref, db_ref, ext_ref, du0_ref):
        @pl.when(pl.program_id(0) == 0)
        def _():
            db_ref[...] = jnp.zeros_like(db_ref)

        ext_ref[0, 0:tm, :] = d_ref[...].astype(F32)
        ext_ref[0, tm:, :] = n_ref[...].astype(F32) * jnp.where(pl.program_id(0) == nrow - 1, 0.0, 1.0)
        _shifted_planes(ext_ref)
        top = CONV_KERNEL - 1
        _conv_taps(ext_ref, w_ref, [top - j for j in range(CONV_KERNEL)], tm, du0_ref)
        du0 = du0_ref[...]
        ag = ag_ref[...].astype(F32)
        a, gt = ag[:, :c], ag[:, c:]
        sg = _sigmoid(gt)
        da = du0 * sg
        dgt = du0 * a * (sg * (1.0 - sg))
        o_ref[:, :c] = da.astype(BF16)
        o_ref[:, c:] = dgt.astype(BF16)
        db_ref[:, :c] += jnp.sum(da, axis=0, keepdims=True)
        db_ref[:, c:] += jnp.sum(dgt, axis=0, keepdims=True)

    hb = tm // CONV_HALO
    return pl.pallas_call(
        body, name="glu_conv_bwd",
        out_shape=[jax.ShapeDtypeStruct((s, 2 * c), BF16), jax.ShapeDtypeStruct((1, 2 * c), F32)], grid=(nrow,),
        in_specs=[_rows(tm, c), pl.BlockSpec((CONV_HALO, c), lambda i: (jnp.minimum((i + 1) * hb, s // CONV_HALO - 1), 0)),
                  _rows(tm, 2 * c), _full((CONV_KERNEL, c))],
        out_specs=[_rows(tm, 2 * c), _full((1, 2 * c))],
        scratch_shapes=[pltpu.VMEM((8, tm + CONV_HALO, c), F32), pltpu.VMEM((tm, c), F32)],
        compiler_params=_params("arbitrary"),
    )(du1, du1, ag, w_dw)


def _head_rows(v, mask):
    return jnp.max(jnp.where(mask, v, -jnp.inf), axis=-1, keepdims=True)


def _attn_bwd(qv, dmix, mixed, lse, rope, grp, dil, ties=()):
    l = qv.shape[0]
    s = l * dil
    nb = l // SPAN
    view = lambda t: t.reshape(l, dil * t.shape[1])
    scale = HEAD_DIM ** -0.5
    gw = GROUP_WIDTH

    def body(*refs):
        q_ref, kp_ref, kc_ref, vp_ref, vc_ref, do_ref, mx_ref, l_ref, tab_ref, tabp_ref = refs[:10]
        dq_ref, dkv_ref, carry_ref = refs[-3:]
        b = pl.program_id(1)

        @pl.when(b < nb)
        def _():
            valid = _band_mask(b)
            masks, keep = _head_masks()
            for p in range(gw // 128):
                sl = slice(p * 128, (p + 1) * 128)
                sl_v = slice(gw + p * 128, gw + (p + 1) * 128)
                qp, dop = q_ref[:, sl], do_ref[:, sl]
                kk = jnp.concatenate([kp_ref[:, sl], kc_ref[:, sl]], axis=0)
                vv = jnp.concatenate([vp_ref[:, sl], vc_ref[:, sl]], axis=0)
                prod = dop.astype(F32) * mx_ref[:, sl].astype(F32)
                lsep = l_ref[:, sl]
                q2 = jnp.concatenate([qp * keep[0], qp * keep[1]], axis=0)
                do2 = jnp.concatenate([dop * keep[0], dop * keep[1]], axis=0)
                lse2 = jnp.concatenate([_head_rows(lsep, masks[h]) for h in range(2)], axis=0)
                dbar2 = jnp.concatenate([jnp.sum(jnp.where(masks[h], prod, 0.0), axis=-1, keepdims=True) for h in range(2)], axis=0)
                sc = lax.dot_general(q2, kk, NT, preferred_element_type=F32) * scale
                pe = jnp.where(valid, jnp.exp(sc - lse2), 0.0)
                dp = lax.dot_general(do2, vv, NT, preferred_element_type=F32)
                ds = (pe * (dp - dbar2) * scale).astype(BF16)
                dq2 = jnp.dot(ds, kk, preferred_element_type=F32)
                dq = jnp.where(masks[0], dq2[:SPAN], dq2[SPAN:])
                dq_ref[:, sl] = _rope_transpose(dq, tab_ref[...]).astype(BF16)
                dk = lax.dot_general(ds, q2, TN, preferred_element_type=F32)
                dv = lax.dot_general(pe.astype(BF16), do2, TN, preferred_element_type=F32)

                @pl.when(b > 0)
                def _():
                    dk_prev = carry_ref[:, sl] + dk[:SPAN]
                    dkv_ref[:, sl] = _rope_transpose(dk_prev, tabp_ref[...]).astype(BF16)
                    dkv_ref[:, sl_v] = (carry_ref[:, sl_v] + dv[:SPAN]).astype(BF16)

                carry_ref[:, sl] = dk[SPAN:]
                carry_ref[:, sl_v] = dv[SPAN:]

        @pl.when(b == nb)
        def _():
            for p in range(gw // 128):
                sl = slice(p * 128, (p + 1) * 128)
                sl_v = slice(gw + p * 128, gw + (p + 1) * 128)
                dkv_ref[:, sl] = _rope_transpose(carry_ref[:, sl], tabp_ref[...]).astype(BF16)
                dkv_ref[:, sl_v] = carry_ref[:, sl_v].astype(BF16)

    blk = (SPAN, gw)
    cb = lambda b: jnp.minimum(b, nb - 1)
    cur = lambda t: pl.BlockSpec(blk, lambda r, b: (cb(b), r * 3 + t))
    prev = lambda t: pl.BlockSpec(blk, lambda r, b: (jnp.maximum(cb(b) - 1, 0), r * 3 + t))
    own = pl.BlockSpec(blk, lambda r, b: (cb(b), r))
    tab = pl.BlockSpec((SPAN, ROPE_COLS), lambda r, b: (cb(b), r))
    tab_prev = pl.BlockSpec((SPAN, ROPE_COLS), lambda r, b: (jnp.maximum(b - 1, 0), r))
    dq, dkv = pl.pallas_call(
        body, name=f"attn_bwd_g{grp}",
        out_shape=[jax.ShapeDtypeStruct((l, dil * gw), BF16), jax.ShapeDtypeStruct((l, dil * 2 * gw), BF16)],
        grid=(dil, nb + 1),
        in_specs=[cur(0), prev(1), cur(1), prev(2), cur(2), own, own, own, tab, tab_prev] + [ANY] * len(ties),
        out_specs=[own, pl.BlockSpec((SPAN, 2 * gw), lambda r, b: (jnp.maximum(b - 1, 0), r))],
        scratch_shapes=[pltpu.VMEM((SPAN, 2 * gw), F32)], compiler_params=_params("parallel", "arbitrary"),
    )(qv, qv, qv, qv, qv, view(dmix), view(mixed), view(lse), view(rope), view(rope), *ties)
    return dq.reshape(s, gw), dkv.reshape(s, 2 * gw)


def _rope_freq_row():
    half = ROT_DIM // 2
    inv = (ROPE_THETA ** (-np.arange(half, dtype=np.float32) / half)).astype(np.float32)
    row = np.zeros((1, 128), np.float32)
    for head in range(128 // HEAD_DIM):
        row[0, head * HEAD_DIM:head * HEAD_DIM + half] = inv
        row[0, head * HEAD_DIM + half:head * HEAD_DIM + ROT_DIM] = inv
    return jnp.asarray(row)


def _ffn_fwd(x, g_pre, g_post, w_up_t, w_dw, b_dw, w_down):
    h, z = _norm_matmul(x, g_pre, w_up_t, tn=_tile(w_up_t.shape[0]), name="ffn_up")
    act = _ffn_act(z, w_dw, b_dw)
    y, xo = _matmul_resnorm(act, w_down, x, g_post, name="ffn_down")
    return xo, (x, h, z, act, y)


def _ffn_bwd(saved, dxo, g_pre, g_post, w_up_t, w_dw, b_dw, w_down):
    x, h, z, act, y = saved
    f = act.shape[1]
    d = x.shape[1]
    dy, dg_post = _postnorm_bwd(y, g_post, dxo, name="ffn_post_bwd")
    dact = _matmul(dy, w_down, name="ffn_dact", out_dtype=BF16, transposed_w=True)
    d_down = _weight_grad(act, dy, name="ffn_dw_down", out_shape=(1, f, d))
    dug_u, dug_g, db_u, db_g, dwd_u, dwd_g = _ffn_act_bwd(z, dact, w_dw, b_dw)
    dz_u = _conv3_transpose(dug_u, w_dw, 0)
    dz_g = _conv3_transpose(dug_g, w_dw, f)
    dx, dg_pre = _matmul_prenorm_bwd([(dz_u, 0, f, 0), (dz_g, 0, f, f)], w_up_t, x, g_pre, dxo, name="ffn_dx")
    d_up_t = _weight_grad(dz_u, h, name="ffn_dw_up", out_shape=(1, 2 * f, d))
    d_up_t = _weight_grad(dz_g, h, name="ffn_dw_up", out=d_up_t, row0=f)
    grads = dict(w_dw=jnp.concatenate([dwd_u, dwd_g], axis=1), b_dw=jnp.concatenate([db_u, db_g], axis=1),
                 g_pre=dg_pre, g_post=dg_post)
    return dx, grads, d_up_t, d_down


def _local_step(x, pos_col, target, p, tie=None, late_weights=None, exchange=None):
    ng = p["norm_g"]
    row = lambda r: ng[r:r + 1]
    freq = _rope_freq_row()
    rope = _rope_tables(pos_col, freq if tie is None else freq + tie[0:1])
    d = x.shape[1]

    h0, *qkv = _qkv_proj(x, row(0), p["w_qkv_t"], rope)
    os_, ls_ = zip(*[_attn_fwd(qkv[g_], g_, d_) for g_, d_ in enumerate(DILATIONS)])
    y_a, x1, mixed, lse = _mix_wo(os_, ls_, p["w_o_t"], x, row(1))
    if late_weights is not None:
        p = {**p, **late_weights(x1)}
    x2, ffn0 = _ffn_fwd(x1, row(2), row(3), p["w_up_t"][0], p["ffn_w_dw"][0], p["ffn_b_dw"][0], p["w_down"][0])
    h1, ag = _norm_matmul(x2, row(4), p["w_pw1_t"], tn=_tile(p["w_pw1_t"].shape[0]), name="conv_pw1", bias=p["b_pw1"])
    u3, u1 = _conv_mid(ag, p["conv_w_dw"], p["conv_b_dw"], p["ln_g"], p["ln_b"])
    y_c, x3 = _matmul_resnorm(u3, p["w_pw2"], x2, row(5), name="conv_pw2", bias=p["b_pw2"])
    x4, ffn1 = _ffn_fwd(x3, row(6), row(7), p["w_up_t"][1], p["ffn_w_dw"][1], p["ffn_b_dw"][1], p["w_down"][1])
    dx4, loss = _loss_grad(x4, target)

    big = [BF16, BF16]

    def tied(r, *tokens):
        tokens = [t for t in tokens if t is not None]
        return row(r) if not tokens else row(r) + jnp.tile(sum(tokens)[0:1], (1, d // 128))

    dx3, gf1, d_up1, d_down1 = _ffn_bwd(ffn1, dx4, row(6), row(7), p["w_up_t"][1], p["ffn_w_dw"][1], p["ffn_b_dw"][1],
                                        p["w_down"][1])
    t0 = exchange.submit("ffn1", [d_up1, d_down1], big) if exchange else None
    dy_c, dg5, db_pw2 = _postnorm_bwd(y_c, tied(5, t0), dx3, name="conv_post_bwd", with_bias_grad=True)
    du3 = _matmul(dy_c, p["w_pw2"], name="conv_du3", out_dtype=F32, transposed_w=True)
    d_wpw2 = _weight_grad(u3, dy_c, name="conv_dw_pw2", out_shape=(1, u3.shape[1], d))
    du1, d_lng, d_lnb, d_cbdw, d_cwdw = _conv_mid_bwd(ag, u1, du3, p["conv_b_dw"], p["ln_g"], p["ln_b"])
    dag, db_pw1 = _glu_conv_bwd(du1, ag, p["conv_w_dw"])
    dx2, dg4 = _matmul_prenorm_bwd([(dag, 0, dag.shape[1], 0)], p["w_pw1_t"], x2, row(4), dx3, name="conv_dx")
    d_wpw1_t = _weight_grad(dag, h1, name="conv_dw_pw1", out_shape=(1, dag.shape[1], d))
    t0 = exchange.advance(dx2) if exchange else None
    t1 = exchange.submit("conv", [d_wpw1_t, d_wpw2], big) if exchange else None
    dx1, gf0, d_up0, d_down0 = _ffn_bwd(ffn0, dx2, row(2), tied(3, t0, t1), p["w_up_t"][0], p["ffn_w_dw"][0], p["ffn_b_dw"][0],
                                        p["w_down"][0])
    t0 = exchange.advance(dx1) if exchange else None
    t1 = exchange.submit("ffn0", [d_up0, d_down0], big) if exchange else None
    dy_a, dg1 = _postnorm_bwd(y_a, tied(1, t0, t1), dx1, name="attn_post_bwd")
    dmix = _matmul(dy_a, p["w_o_t"], name="attn_dmix", out_dtype=BF16, transposed_w=False)
    d_wo_t = _weight_grad(dy_a, mixed, name="attn_dw_o", out_shape=(1, d, GROUP_WIDTH))
    pieces, d_wqkv_t = [], None
    for g_, d_ in enumerate(DILATIONS):
        tok = exchange.advance(dkv) if exchange and g_ > 0 else None
        dq, dkv = _attn_bwd(qkv[g_], dmix, mixed, lse, rope, g_, d_, ties=() if tok is None else (tok,))
        for t, (arr, c0) in enumerate(((dq, 0), (dkv, 0), (dkv, GROUP_WIDTH))):
            r0 = (3 * t + g_) * GROUP_WIDTH
            pieces.append((arr, c0, GROUP_WIDTH, r0))
            d_wqkv_t = _weight_grad(arr, h0, name="attn_dw_qkv", a_col0=c0, ka=GROUP_WIDTH, out=d_wqkv_t,
                                    out_shape=(1, p["w_qkv_t"].shape[0], d), row0=r0)
    t0 = exchange.advance(dkv) if exchange else None
    t1 = exchange.submit("attn", [d_wqkv_t, d_wo_t], big) if exchange else None
    grad_x, dg0 = _matmul_prenorm_bwd(pieces, p["w_qkv_t"], x, tied(0, t0, t1), dx1, name="attn_dx")

    grads = dict(
        norm_g=jnp.concatenate([dg0, dg1, gf0["g_pre"], gf0["g_post"], dg4, dg5, gf1["g_pre"], gf1["g_post"]], axis=0),
        w_qkv_t=d_wqkv_t, w_o_t=d_wo_t, w_pw1_t=d_wpw1_t, b_pw1=db_pw1,
        conv_w_dw=d_cwdw[:CONV_KERNEL], conv_b_dw=d_cbdw, ln_g=d_lng, ln_b=d_lnb, w_pw2=d_wpw2, b_pw2=db_pw2,
        w_up_t=[d_up0, d_up1], ffn_w_dw=jnp.stack([gf0["w_dw"], gf1["w_dw"]]),
        ffn_b_dw=jnp.concatenate([gf0["b_dw"], gf1["b_dw"]], axis=0), w_down=[d_down0, d_down1])
    return loss, grad_x, grads


SMALL_AXIS = dict(norm_g=2, conv_b_pw1=1, conv_w_dw=2, conv_b_dw=1, conv_ln_g=1, conv_ln_b=1, conv_b_pw2=1, ffn_w_dw=2)
SMALL = tuple(SMALL_AXIS)
MATMUL_WEIGHTS = dict(attn_w_qkv=True, conv_w_pw1=True, ffn_w_up=True, conv_w_pw2=False, ffn_w_down=False)


def _pack(arrays, cols, row_multiple):
    flat = jnp.concatenate([a.reshape(-1) for a in arrays])
    rows = -(-flat.shape[0] // cols)
    rows = -(-rows // row_multiple) * row_multiple
    return jnp.pad(flat, (0, rows * cols - flat.shape[0])).reshape(rows, cols)


def _unpack(packed, shapes):
    flat = packed.reshape(packed.shape[:-2] + (-1,))
    out, off = [], 0
    for shp in shapes:
        n = math.prod(shp)
        out.append(flat[..., off:off + n].reshape(packed.shape[:-2] + tuple(shp)))
        off += n
    return out


def _join_shards(stacked, axis):
    moved = jnp.moveaxis(stacked, 0, axis)
    shp = moved.shape
    return moved.reshape(shp[:axis] + (shp[axis] * shp[axis + 1],) + shp[axis + 2:])


def _split_shards(whole, axis):
    shp = whole.shape
    cut = whole.reshape(shp[:axis] + (N_DEV, shp[axis] // N_DEV) + shp[axis + 1:])
    return jnp.moveaxis(cut, axis, 0)


def _row_shard(w, transposed):
    t = jnp.swapaxes(w, 1, 2) if transposed else w
    return t.astype(BF16).reshape(-1, t.shape[-1])


def kernel(x, positions, norm_g, attn_w_qkv, attn_w_o, conv_w_pw1, conv_b_pw1, conv_w_dw, conv_b_dw, conv_ln_g, conv_ln_b, conv_w_pw2, conv_b_pw2, ffn_w_up, ffn_w_dw, ffn_b_dw, ffn_w_down, loss_target, m_norm_g, m_attn_w_qkv, m_attn_w_o, m_conv_w_pw1, m_conv_b_pw1, m_conv_w_dw, m_conv_b_dw, m_conv_ln_g, m_conv_ln_b, m_conv_w_pw2, m_conv_b_pw2, m_ffn_w_up, m_ffn_w_dw, m_ffn_b_dw, m_ffn_w_down, v_norm_g, v_attn_w_qkv, v_attn_w_o, v_conv_w_pw1, v_conv_b_pw1, v_conv_w_dw, v_conv_b_dw, v_conv_ln_g, v_conv_ln_b, v_conv_w_pw2, v_conv_b_pw2, v_ffn_w_up, v_ffn_w_dw, v_ffn_b_dw, v_ffn_w_down):
    w = dict(norm_g=norm_g, attn_w_qkv=attn_w_qkv, attn_w_o=attn_w_o, conv_w_pw1=conv_w_pw1, conv_b_pw1=conv_b_pw1,
             conv_w_dw=conv_w_dw, conv_b_dw=conv_b_dw, conv_ln_g=conv_ln_g, conv_ln_b=conv_ln_b, conv_w_pw2=conv_w_pw2,
             conv_b_pw2=conv_b_pw2, ffn_w_up=ffn_w_up, ffn_w_dw=ffn_w_dw, ffn_w_down=ffn_w_down)
    m = dict(norm_g=m_norm_g, attn_w_qkv=m_attn_w_qkv, attn_w_o=m_attn_w_o, conv_w_pw1=m_conv_w_pw1, conv_b_pw1=m_conv_b_pw1,
             conv_w_dw=m_conv_w_dw, conv_b_dw=m_conv_b_dw, conv_ln_g=m_conv_ln_g, conv_ln_b=m_conv_ln_b, conv_w_pw2=m_conv_w_pw2,
             conv_b_pw2=m_conv_b_pw2, ffn_w_up=m_ffn_w_up, ffn_w_dw=m_ffn_w_dw, ffn_w_down=m_ffn_w_down)
    v = dict(norm_g=v_norm_g, attn_w_qkv=v_attn_w_qkv, attn_w_o=v_attn_w_o, conv_w_pw1=v_conv_w_pw1, conv_b_pw1=v_conv_b_pw1,
             conv_w_dw=v_conv_w_dw, conv_b_dw=v_conv_b_dw, conv_ln_g=v_conv_ln_g, conv_ln_b=v_conv_ln_b, conv_w_pw2=v_conv_w_pw2,
             conv_b_pw2=v_conv_b_pw2, ffn_w_up=v_ffn_w_up, ffn_w_dw=v_ffn_w_dw, ffn_w_down=v_ffn_w_down)
    d = x.shape[-1]

    w_qkv_t, w_o_t, small = _all_gather([_row_shard(attn_w_qkv, True), _row_shard(attn_w_o, True),
                                         _pack([w[n] for n in SMALL], 128, 8)], "gather_first_weights")
    w_qkv_t, w_o_t = w_qkv_t.reshape(-1, d), w_o_t.reshape(d, -1)
    sm = {n: _join_shards(stacked, SMALL_AXIS[n])
          for n, stacked in zip(SMALL, _unpack(small, [w[n].shape for n in SMALL]))}
    late = {n: t for n, t in MATMUL_WEIGHTS.items() if n != "attn_w_qkv"}
    shares = [_row_shard(w[n], t) for n, t in late.items()]
    rows = [s_.shape[0] for s_ in shares]
    late_share = jnp.concatenate(shares, axis=0)
    send_sems, recv_sems, share_thru, land_thru, tie = _gather_start(late_share)
    me = 4 * lax.axis_index("x") + 2 * lax.axis_index("y") + lax.axis_index("c")

    def late_weights(after):
        big = _gather_wait(send_sems, recv_sems, share_thru, land_thru, after)
        big = lax.dynamic_update_slice(big, late_share[None], (me, 0, 0))
        whole, r0 = {}, 0
        for n, nr in zip(late, rows):
            layers = w[n].shape[0]
            seg = big[:, r0:r0 + nr].reshape(N_DEV, layers, nr // layers, d)
            whole[n] = [seg[:, l_].reshape(-1, d) for l_ in range(layers)]
            r0 += nr
        return dict(w_pw1_t=whole["conv_w_pw1"][0], w_pw2=whole["conv_w_pw2"][0], w_up_t=whole["ffn_w_up"],
                    w_down=whole["ffn_w_down"])

    p = dict(norm_g=sm["norm_g"].reshape(-1, d), w_qkv_t=w_qkv_t, w_o_t=w_o_t, b_pw1=sm["conv_b_pw1"],
             conv_w_dw=sm["conv_w_dw"][0], conv_b_dw=sm["conv_b_dw"], ln_g=sm["conv_ln_g"], ln_b=sm["conv_ln_b"],
             b_pw2=sm["conv_b_pw2"], ffn_w_dw=sm["ffn_w_dw"], ffn_b_dw=[ffn_b_dw[0:1], ffn_b_dw[1:2]])

    exchange = _GradExchange()
    loss, grad_x, g = _local_step(x[0], positions.reshape(-1, 1), loss_target[0], p, tie, late_weights, exchange)
    loss = lax.psum(loss[0, 0], ("x", "y", "c"))
    gsmall = dict(norm_g=g["norm_g"].reshape(norm_g.shape[0], 4, -1), conv_b_pw1=g["b_pw1"], conv_w_dw=g["conv_w_dw"][None],
                  conv_b_dw=g["conv_b_dw"], conv_ln_g=g["ln_g"], conv_ln_b=g["ln_b"], conv_b_pw2=g["b_pw2"], ffn_w_dw=g["ffn_w_dw"])
    small_contrib = jnp.concatenate([_split_shards(gsmall[n], SMALL_AXIS[n]).reshape(N_DEV, -1) for n in SMALL], axis=1)
    srows = small.shape[1]
    small_contrib = jnp.pad(small_contrib, ((0, 0), (0, srows * 128 - small_contrib.shape[1]))).reshape(1, N_DEV, srows, 128)
    exchange.advance(grad_x)
    small_sums = _rs_chips([_rs_pair_add(small_contrib, _rs_sibling([small_contrib])[0], exchange.core, F32)])[0]

    outs = {}

    def update(n, reduced):
        gsum = jnp.swapaxes(reduced, 1, 2) if n == "attn_w_o" or MATMUL_WEIGHTS.get(n) else reduced
        outs[n] = (gsum, *_adamw(gsum, w[n], m[n], v[n], "adamw"))

    (s_up1, s_down1), (s_pw1, s_pw2), (s_up0, s_down0) = exchange.results()[:3]
    update("conv_w_pw1", s_pw1)
    update("conv_w_pw2", s_pw2)
    update("ffn_w_up", jnp.concatenate([s_up0, s_up1], axis=0))
    update("ffn_w_down", jnp.concatenate([s_down0, s_down1], axis=0))
    sshapes = [w[n].shape for n in SMALL]
    souts = _sum_adamw(small_sums[0], *[_pack([t[n] for n in SMALL], 128, 8) for t in (w, m, v)], name="sum_adamw_small")
    for n, vals in zip(SMALL, zip(*[_unpack(o, sshapes) for o in souts])):
        outs[n] = vals
    bparts, = _all_gather([_pack([g["ffn_b_dw"]], 128, 8)], "gather_bias_grads")
    bouts = _sum_adamw(bparts, *[_pack([t], 128, 8) for t in (ffn_b_dw, m_ffn_b_dw, v_ffn_b_dw)], name="sum_adamw_bias")
    outs["ffn_b_dw"] = tuple(_unpack(o, [ffn_b_dw.shape])[0] for o in bouts)
    done = [outs[n][1][0, :8, :128] for n in ("conv_w_pw1", "conv_w_pw2", "ffn_w_up", "ffn_w_down")]
    exchange.advance(sum(done) + bouts[1][:8] + souts[1][:8])
    s_qkv, s_wo = exchange.results()[3]
    update("attn_w_qkv", s_qkv)
    update("attn_w_o", s_wo)

    order = ("norm_g", "attn_w_qkv", "attn_w_o", "conv_w_pw1", "conv_b_pw1", "conv_w_dw", "conv_b_dw", "conv_ln_g",
             "conv_ln_b", "conv_w_pw2", "conv_b_pw2", "ffn_w_up", "ffn_w_dw", "ffn_b_dw", "ffn_w_down")
    return (loss, grad_x[None], *[outs[n][0] for n in order], *[outs[n][1] for n in order],
            *[outs[n][2] for n in order], *[outs[n][3] for n in order])
```

```python
import math

import numpy as np
import jax
import jax.numpy as jnp
from jax import lax
from jax.experimental import pallas as pl
from jax.experimental.pallas import tpu as pltpu

F32 = jnp.float32
BF16 = jnp.bfloat16
EPS = 1e-6
N_DEV = 8
HEAD_DIM = 64
GROUP_WIDTH = 512
DILATIONS = (1, 4, 16)
SPAN = 128
ROT_DIM = 16
ROPE_THETA = 500000.0
CONV_KERNEL = 31
CONV_HALO = 32
FFN_CONV = 3
ADAM_LR, ADAM_B1, ADAM_B2, ADAM_EPS, ADAM_WD, ADAM_STEP = 0.001, 0.9, 0.999, 1e-08, 0.01, 10
VMEM_LIMIT_BYTES = 56 * 1024 * 1024
MESH = pl.DeviceIdType.MESH
ANY = pl.BlockSpec(memory_space=pl.ANY)
NT = (((1,), (1,)), ((), ()))
TN = (((0,), (0,)), ((), ()))


def _params(*sem):
    return pltpu.CompilerParams(dimension_semantics=sem, vmem_limit_bytes=VMEM_LIMIT_BYTES)


def _sigmoid(v):
    return pl.reciprocal(1.0 + jnp.exp(-v), approx=True)


def _full(shape):
    return pl.BlockSpec(shape, lambda *_: (0,) * len(shape))


def _rows(tm, width):
    return pl.BlockSpec((tm, width), lambda i, *_: (i, 0))


def _tile(n, *multiples_of):
    for t in (1408, 1024, 512, 384, 256, 128):
        if n % t == 0 and all(o % t == 0 for o in multiples_of):
            return t
    raise ValueError((n, multiples_of))


def _all_gather(shards, name):
    n = len(shards)

    def body(*refs):
        x_refs, out_refs, (send_sems, recv_sems, local_sems) = refs[:n], refs[n:2 * n], refs[2 * n:]
        x, y, c = lax.axis_index("x"), lax.axis_index("y"), lax.axis_index("c")
        me, sibling = (x, y, c), (x, y, 1 - c)
        chips = [(1 - x, y), (x, 1 - y), (1 - x, 1 - y)]

        def rows(w, px, py, pc):
            return out_refs[w].at[4 * px + 2 * py + pc]

        def copy(w, k, block, to, src=None):
            return pltpu.make_async_remote_copy(
                src_ref=rows(w, *block) if src is None else src, dst_ref=rows(w, *block),
                send_sem=send_sems.at[7 * w + k], recv_sem=recv_sems.at[7 * w + k], device_id=to, device_id_type=MESH)

        every = range(n)
        mine = [pltpu.make_async_copy(x_refs[w], rows(w, *me), local_sems.at[w]) for w in every]
        first = [copy(w, 0, me, sibling, src=x_refs[w]) for w in every]
        first += [copy(w, 1 + j, me, (*chip, c), src=x_refs[w]) for w in every for j, chip in enumerate(chips)]
        for cp in mine + first:
            cp.start()
        passed = []
        for j, chip in enumerate(chips):
            for w in every:
                copy(w, 1 + j, (*chip, c), me).wait_recv()
                passed.append(copy(w, 4 + j, (*chip, c), sibling))
                passed[-1].start()
        for w in every:
            copy(w, 0, sibling, me).wait_recv()
            for j, chip in enumerate(chips):
                copy(w, 4 + j, (*chip, 1 - c), me).wait_recv()
        for cp in first + passed:
            cp.wait_send()
        for cp in mine:
            cp.wait()

    return pl.pallas_call(
        body, name=name, out_shape=[jax.ShapeDtypeStruct((N_DEV,) + s_.shape, s_.dtype) for s_ in shards],
        in_specs=[ANY] * n, out_specs=[ANY] * n,
        scratch_shapes=[pltpu.SemaphoreType.DMA((7 * n,)), pltpu.SemaphoreType.DMA((7 * n,)), pltpu.SemaphoreType.DMA((n,))],
    )(*shards)


HBM = pl.BlockSpec(memory_space=pltpu.HBM)
SEM = pl.BlockSpec(memory_space=pltpu.SEMAPHORE)
SIDE_EFFECT = pltpu.CompilerParams(has_side_effects=pltpu.SideEffectType.DATAFLOW_SIDE_EFFECTING)


def _gather_start(shard):
    r, c_ = shard.shape

    def body(x_ref, land_ref, send_sems, recv_sems, x_thru, land_thru, token):
        x, y, c = lax.axis_index("x"), lax.axis_index("y"), lax.axis_index("c")
        me = 4 * x + 2 * y + c
        for k in range(1, N_DEV):
            peer = (1 - x if k & 4 else x, 1 - y if k & 2 else y, 1 - c if k & 1 else c)
            pltpu.make_async_remote_copy(src_ref=x_ref, dst_ref=land_ref.at[me], send_sem=send_sems.at[k - 1],
                                         recv_sem=recv_sems.at[k - 1], device_id=peer, device_id_type=MESH).start()
        token[...] = jnp.zeros_like(token)

    land = pltpu.with_memory_space_constraint(lax.empty((N_DEV, r, c_), shard.dtype), pltpu.HBM)
    return pl.pallas_call(
        body, name="gather_late_weights_start",
        out_shape=(pltpu.SemaphoreType.DMA((N_DEV - 1,)), pltpu.SemaphoreType.DMA((N_DEV - 1,)),
                   pltpu.HBM(shard.shape, shard.dtype), pltpu.HBM((N_DEV, r, c_), shard.dtype),
                   jax.ShapeDtypeStruct((8, 128), F32)),
        in_specs=(HBM, HBM), out_specs=(SEM, SEM, HBM, HBM, pl.BlockSpec(memory_space=pltpu.VMEM)),
        input_output_aliases={0: 2, 1: 3}, compiler_params=SIDE_EFFECT,
    )(pltpu.with_memory_space_constraint(shard, pltpu.HBM), land)


def _gather_wait(send_sems, recv_sems, shard_thru, land_thru, after):
    def body(x_ref, land_ref, send_sems, recv_sems, after_ref, x_dead, got_ref):
        x, y, c = lax.axis_index("x"), lax.axis_index("y"), lax.axis_index("c")
        for k in range(N_DEV - 1):
            copy = pltpu.make_async_remote_copy(src_ref=x_ref, dst_ref=land_ref.at[0], send_sem=send_sems.at[k],
                                                recv_sem=recv_sems.at[k], device_id=(x, y, c), device_id_type=MESH)
            copy.wait_send()
            copy.wait_recv()

    return pl.pallas_call(
        body, name="gather_late_weights_wait",
        out_shape=(pltpu.HBM(shard_thru.shape, shard_thru.dtype), pltpu.HBM(land_thru.shape, land_thru.dtype)),
        in_specs=(HBM, HBM, SEM, SEM, ANY), out_specs=(HBM, HBM), input_output_aliases={0: 0, 1: 1},
        compiler_params=SIDE_EFFECT,
    )(shard_thru, land_thru, send_sems, recv_sems, after)[1]


def _hbm(a):
    return pltpu.with_memory_space_constraint(a, pltpu.HBM)


def _exchange_start(name, arrays, lands, plan, ncopies):
    n = len(arrays)

    def body(*refs):
        send_sems, recv_sems, token = refs[2 * n], refs[2 * n + 1], refs[-1]
        x, y, c = lax.axis_index("x"), lax.axis_index("y"), lax.axis_index("c")
        for k, (src, dst, peer) in enumerate(plan(x, y, c, refs[:n], refs[n:2 * n])):
            pltpu.make_async_remote_copy(src_ref=src, dst_ref=dst, send_sem=send_sems.at[k], recv_sem=recv_sems.at[k],
                                         device_id=peer, device_id_type=MESH).start()
        token[...] = jnp.zeros_like(token)

    both = list(arrays) + list(lands)
    outs = pl.pallas_call(
        body, name=name,
        out_shape=(pltpu.SemaphoreType.DMA((ncopies,)), pltpu.SemaphoreType.DMA((ncopies,)),
                   *[pltpu.HBM(a.shape, a.dtype) for a in both], jax.ShapeDtypeStruct((8, 128), F32)),
        in_specs=(HBM,) * (2 * n), out_specs=(SEM, SEM) + (HBM,) * (2 * n) + (pl.BlockSpec(memory_space=pltpu.VMEM),),
        input_output_aliases={i: 2 + i for i in range(2 * n)}, compiler_params=SIDE_EFFECT,
    )(*[_hbm(a) for a in both])
    return outs[0], outs[1], list(outs[2:2 + n]), list(outs[2 + n:2 + 2 * n]), outs[-1]


def _exchange_wait(name, send_sems, recv_sems, arrays, lands, plan, after):
    n = len(arrays)

    def body(*refs):
        send_sems, recv_sems = refs[2 * n], refs[2 * n + 1]
        x, y, c = lax.axis_index("x"), lax.axis_index("y"), lax.axis_index("c")
        for k, (src, dst, peer) in enumerate(plan(x, y, c, refs[:n], refs[n:2 * n])):
            copy = pltpu.make_async_remote_copy(src_ref=src, dst_ref=dst, send_sem=send_sems.at[k], recv_sem=recv_sems.at[k],
                                                device_id=peer, device_id_type=MESH)
            copy.wait_send()
            copy.wait_recv()

    both = list(arrays) + list(lands)
    outs = pl.pallas_call(
        body, name=name, out_shape=tuple(pltpu.HBM(a.shape, a.dtype) for a in both),
        in_specs=(HBM,) * (2 * n) + (SEM, SEM, ANY), out_specs=(HBM,) * (2 * n),
        input_output_aliases={i: i for i in range(2 * n)}, compiler_params=SIDE_EFFECT,
    )(*both, send_sems, recv_sems, after)
    return list(outs[:n]), list(outs[n:])


def _sibling_plan(x, y, c, g_refs, land_refs):
    return [(g.at[:, 2 * q + (1 - c)], o.at[:, q], (x, y, 1 - c)) for g, o in zip(g_refs, land_refs) for q in range(4)]


def _chips_plan(x, y, c, p_refs, land_refs):
    chips = [(1 - x, y), (x, 1 - y), (1 - x, 1 - y)]
    return [(p_.at[:, 2 * qx + qy], o.at[:, 2 * x + y], (qx, qy, c)) for p_, o in zip(p_refs, land_refs) for qx, qy in chips]


class _GradExchange:
    def __init__(self):
        self.core = lax.axis_index("c").astype(jnp.int32).reshape(1)
        self.chip = 2 * lax.axis_index("x") + lax.axis_index("y")
        self.groups = []

    def submit(self, tag, arrays, dtypes):
        arrays = [a.reshape(a.shape[0], N_DEV, a.shape[1] // N_DEV, a.shape[2]) for a in arrays]
        lands = [lax.empty((a.shape[0], 4) + a.shape[2:], a.dtype) for a in arrays]
        send, recv, arrays, lands, token = _exchange_start(f"rs_pair_start_{tag}", arrays, lands, _sibling_plan, 4 * len(arrays))
        self.groups.append(dict(tag=tag, stage=1, sems=(send, recv), arrays=arrays, lands=lands, dtypes=dtypes))
        return token

    def advance(self, after):
        token = None
        for g in self.groups:
            if g["stage"] == 1:
                arrays, got = _exchange_wait(f"rs_pair_wait_{g['tag']}", *g["sems"], g["arrays"], g["lands"], _sibling_plan, after)
                parts = [_rs_pair_add(a, b, self.core, dt) for a, b, dt in zip(arrays, got, g["dtypes"])]
                lands = [lax.empty(p_.shape, p_.dtype) for p_ in parts]
                send, recv, parts, lands, tok = _exchange_start(f"rs_chip_start_{g['tag']}", parts, lands, _chips_plan, 3 * len(parts))
                g.update(stage=2, sems=(send, recv), arrays=parts, lands=lands)
                token = tok if token is None else token + tok
            elif g["stage"] == 2:
                parts, lands = _exchange_wait(f"rs_chip_wait_{g['tag']}", *g["sems"], g["arrays"], g["lands"], _chips_plan, after)
                sums = []
                for p_, land in zip(parts, lands):
                    l, _, r, c_ = p_.shape
                    own = lax.dynamic_slice(p_, (0, self.chip, 0, 0), (l, 1, r, c_))
                    sums.append(_sum_parts(lax.dynamic_update_slice(land, own, (0, self.chip, 0, 0)), "sum_chips"))
                g.update(stage=3, sums=sums)
        return token

    def results(self):
        return [g.get("sums") for g in self.groups]


def _with_rows(g, n):
    return jax.ShapeDtypeStruct((g.shape[0], n) + tuple(g.shape[2:]), g.dtype)


def _rs_sibling(gs):
    n = len(gs)

    def body(*refs):
        g_refs, o_refs, (send_sems, recv_sems) = refs[:n], refs[n:2 * n], refs[2 * n:]
        x, y, c = lax.axis_index("x"), lax.axis_index("y"), lax.axis_index("c")
        copies = [pltpu.make_async_remote_copy(
            src_ref=g_refs[w].at[:, 2 * q + (1 - c)], dst_ref=o_refs[w].at[:, q], send_sem=send_sems.at[4 * w + q],
            recv_sem=recv_sems.at[4 * w + q], device_id=(x, y, 1 - c), device_id_type=MESH)
            for w in range(n) for q in range(4)]
        for cp in copies:
            cp.start()
        for cp in copies:
            cp.wait_recv()
        for cp in copies:
            cp.wait_send()

    return pl.pallas_call(
        body, name="rs_sibling", out_shape=[_with_rows(g, 4) for g in gs],
        in_specs=[ANY] * n, out_specs=[ANY] * n,
        scratch_shapes=[pltpu.SemaphoreType.DMA((4 * n,)), pltpu.SemaphoreType.DMA((4 * n,))],
    )(*gs)


def _rs_pair_add(g, got, core, out_dtype):
    l, _, r, c_ = g.shape

    def body(core_ref, g_ref, got_ref, o_ref):
        o_ref[...] = (g_ref[...].astype(F32) + got_ref[...].astype(F32)).astype(out_dtype)

    blk = (None, None, r, c_)
    return pl.pallas_call(
        body, name="rs_pair_add", out_shape=jax.ShapeDtypeStruct((l, 4, r, c_), out_dtype),
        grid_spec=pltpu.PrefetchScalarGridSpec(
            num_scalar_prefetch=1, grid=(l, 4),
            in_specs=[pl.BlockSpec(blk, lambda i, q, core_ref: (i, 2 * q + core_ref[0], 0, 0)),
                      pl.BlockSpec(blk, lambda i, q, core_ref: (i, q, 0, 0))],
            out_specs=pl.BlockSpec(blk, lambda i, q, core_ref: (i, q, 0, 0))),
        compiler_params=_params("parallel", "parallel"),
    )(core, g, got)


def _rs_chips(parts):
    n = len(parts)

    def body(*refs):
        p_refs, o_refs, (send_sems, recv_sems, local_sems) = refs[:n], refs[n:2 * n], refs[2 * n:]
        x, y, c = lax.axis_index("x"), lax.axis_index("y"), lax.axis_index("c")
        my_chip = 2 * x + y
        chips = [(1 - x, y), (x, 1 - y), (1 - x, 1 - y)]
        local = [pltpu.make_async_copy(p_refs[w].at[:, my_chip], o_refs[w].at[:, my_chip], local_sems.at[w]) for w in range(n)]
        for cp in local:
            cp.start()
        copies = [pltpu.make_async_remote_copy(
            src_ref=p_refs[w].at[:, 2 * qx + qy], dst_ref=o_refs[w].at[:, my_chip], send_sem=send_sems.at[3 * w + k],
            recv_sem=recv_sems.at[3 * w + k], device_id=(qx, qy, c), device_id_type=MESH)
            for w in range(n) for k, (qx, qy) in enumerate(chips)]
        for cp in copies:
            cp.start()
        for cp in copies:
            cp.wait_recv()
        for cp in copies:
            cp.wait_send()
        for cp in local:
            cp.wait()

    return pl.pallas_call(
        body, name="rs_chips", out_shape=[jax.ShapeDtypeStruct(p.shape, p.dtype) for p in parts],
        in_specs=[ANY] * n, out_specs=[ANY] * n,
        scratch_shapes=[pltpu.SemaphoreType.DMA((3 * n,)), pltpu.SemaphoreType.DMA((3 * n,)), pltpu.SemaphoreType.DMA((n,))],
    )(*parts)


def _sum_parts(parts, name):
    l, n, r, c_ = parts.shape

    def body(p_ref, o_ref):
        g = p_ref[0].astype(F32)
        for s in range(1, n):
            g = g + p_ref[s].astype(F32)
        o_ref[...] = g

    return pl.pallas_call(
        body, name=name, out_shape=jax.ShapeDtypeStruct((l, r, c_), F32), grid=(l,),
        in_specs=[pl.BlockSpec((None, n, r, c_), lambda i: (i, 0, 0, 0))],
        out_specs=pl.BlockSpec((None, r, c_), lambda i: (i, 0, 0)), compiler_params=_params("parallel"),
    )(parts)


def _adamw_math(w, g, m, v):
    m = ADAM_B1 * m + (1.0 - ADAM_B1) * g
    v = ADAM_B2 * v + (1.0 - ADAM_B2) * (g * g)
    m_hat = m / (1.0 - ADAM_B1 ** ADAM_STEP)
    v_hat = v / (1.0 - ADAM_B2 ** ADAM_STEP)
    delta = -ADAM_LR * (m_hat / (jnp.sqrt(v_hat) + ADAM_EPS) + ADAM_WD * w)
    return delta, m, v


def _adamw(g, w, m, v, name):
    l, k, n = w.shape
    tk = 256 if k % 256 == 0 else k

    def body(g_ref, w_ref, m_ref, v_ref, d_ref, nm_ref, nv_ref):
        d_ref[...], nm_ref[...], nv_ref[...] = _adamw_math(w_ref[...], g_ref[...], m_ref[...], v_ref[...])

    spec = pl.BlockSpec((None, tk, n), lambda i, j: (i, j, 0))
    return pl.pallas_call(
        body, name=name, out_shape=[jax.ShapeDtypeStruct((l, k, n), F32)] * 3, grid=(l, k // tk),
        in_specs=[spec] * 4, out_specs=[spec] * 3, compiler_params=_params("parallel", "parallel"),
    )(g, w, m, v)


def _sum_adamw(parts, w, m, v, name):
    n, r, c_ = parts.shape

    def body(p_ref, w_ref, m_ref, v_ref, g_ref, d_ref, nm_ref, nv_ref):
        g = p_ref[0]
        for s in range(1, n):
            g = g + p_ref[s]
        g_ref[...] = g
        d_ref[...], nm_ref[...], nv_ref[...] = _adamw_math(w_ref[...], g, m_ref[...], v_ref[...])

    return pl.pallas_call(
        body, name=name, out_shape=[jax.ShapeDtypeStruct((r, c_), F32)] * 4, grid=(1,),
        in_specs=[_full((n, r, c_))] + [_full((r, c_))] * 3, out_specs=[_full((r, c_))] * 4,
        compiler_params=_params("arbitrary"),
    )(parts, w, m, v)


def _rope_tables(pos_col, freq_row):
    s = pos_col.shape[0]
    tm = min(1024, s)

    def body(p_ref, f_ref, o_ref):
        ang = p_ref[...].astype(F32) * f_ref[...]
        lane = lax.broadcasted_iota(jnp.int32, ang.shape, 1) & (HEAD_DIM - 1)
        cs, sn = jnp.cos(ang), jnp.sin(ang)
        o_ref[:, 0:128] = jnp.where(lane < ROT_DIM, cs, 1.0)
        o_ref[:, 128:256] = jnp.where((lane >= ROT_DIM // 2) & (lane < ROT_DIM), sn, 0.0)
        o_ref[:, 256:384] = jnp.where(lane < ROT_DIM // 2, -sn, 0.0)

    return pl.pallas_call(
        body, name="rope_tables", out_shape=jax.ShapeDtypeStruct((s, ROPE_COLS), F32), grid=(s // tm,),
        in_specs=[pl.BlockSpec((tm, 1), lambda i: (i, 0)), _full((1, 128))],
        out_specs=_rows(tm, ROPE_COLS), compiler_params=_params("parallel"),
    )(pos_col, freq_row)


ROPE_COLS = 3 * 128


def _rope_parts(tab, reps=1):
    return [jnp.tile(tab[:, k * 128:(k + 1) * 128], (1, reps)) if reps > 1 else tab[:, k * 128:(k + 1) * 128] for k in range(3)]


def _rope_apply(t, tab):
    w = t.shape[1]
    cos, sin_up, sin_dn = _rope_parts(tab, w // 128)
    return t * cos + pltpu.roll(t, 8, 1) * sin_up + pltpu.roll(t, w - 8, 1) * sin_dn


def _rope_transpose(dr, tab):
    w = dr.shape[1]
    cos, sin_up, sin_dn = _rope_parts(tab, w // 128)
    return dr * cos + pltpu.roll(dr * sin_up, w - 8, 1) + pltpu.roll(dr * sin_dn, 8, 1)


def _norm_matmul(x, g, wt, *, tn, name, bias=None, tm=1024):
    s, d = x.shape
    n = wt.shape[0]
    tm = min(tm, s)

    def body(*refs):
        x_ref, g_ref, w_ref = refs[:3]
        b_ref = refs[3] if bias is not None else None
        h_ref, o_ref = refs[-2:]

        @pl.when(pl.program_id(1) == 0)
        def _():
            xv = x_ref[...]
            r = lax.rsqrt(jnp.mean(xv * xv, axis=-1, keepdims=True) + EPS)
            h_ref[...] = (xv * r * g_ref[...]).astype(BF16)

        acc = lax.dot_general(h_ref[...], w_ref[...], NT, preferred_element_type=F32)
        if b_ref is not None:
            acc = acc + b_ref[...]
        o_ref[...] = acc.astype(BF16)

    in_specs = [_rows(tm, d), _full((1, d)), pl.BlockSpec((tn, d), lambda i, j: (j, 0))]
    args = [x, g, wt]
    if bias is not None:
        in_specs.append(pl.BlockSpec((1, tn), lambda i, j: (0, j)))
        args.append(bias)
    return pl.pallas_call(
        body, name=name,
        out_shape=[jax.ShapeDtypeStruct((s, d), BF16), jax.ShapeDtypeStruct((s, n), BF16)],
        grid=(s // tm, n // tn), in_specs=in_specs,
        out_specs=[_rows(tm, d), pl.BlockSpec((tm, tn), lambda i, j: (i, j))],
        compiler_params=_params("parallel", "arbitrary"),
    )(*args)


def _class_major(tm, dil):
    p = np.zeros((tm, tm), np.float32)
    per = tm // dil
    for r in range(dil):
        for j in range(per):
            p[r * per + j, j * dil + r] = 1.0
    return jnp.asarray(p, dtype=BF16)


def _qkv_proj(x, g, wt, rope, tm=512):
    s, d = x.shape
    n = wt.shape[0]
    gw3 = 3 * GROUP_WIDTH
    tm = min(tm, s)
    assert n == 3 * gw3

    def body(x_ref, g_ref, w_ref, tab_ref, p1_ref, p2_ref, h_ref, o0_ref, o1_ref, o2_ref):
        j = pl.program_id(1)

        @pl.when(j == 0)
        def _():
            xv = x_ref[...]
            r = lax.rsqrt(jnp.mean(xv * xv, axis=-1, keepdims=True) + EPS)
            h_ref[...] = (xv * r * g_ref[...]).astype(BF16)

        acc = lax.dot_general(h_ref[...], w_ref[...], NT, preferred_element_type=F32)

        def store(y):
            yb = y.astype(BF16)
            o0_ref[:, pl.ds(pl.multiple_of(j * GROUP_WIDTH, GROUP_WIDTH), GROUP_WIDTH)] = yb[:, :GROUP_WIDTH]
            for grp, o_ref, p_ref in ((1, o1_ref, p1_ref), (2, o2_ref, p2_ref)):
                dil = DILATIONS[grp]
                per = tm // dil
                yp = jnp.dot(p_ref[...], yb[:, grp * GROUP_WIDTH:(grp + 1) * GROUP_WIDTH],
                             preferred_element_type=F32).astype(BF16)
                for r in range(dil):
                    col = pl.multiple_of(r * gw3 + j * GROUP_WIDTH, GROUP_WIDTH)
                    o_ref[:, pl.ds(col, GROUP_WIDTH)] = yp[r * per:(r + 1) * per, :]

        @pl.when(j < 2)
        def _():
            store(_rope_apply(acc, tab_ref[...]))

        @pl.when(j == 2)
        def _():
            store(acc)

    outs = [jax.ShapeDtypeStruct((s, d), BF16)] + [jax.ShapeDtypeStruct((s // dl, dl * gw3), BF16) for dl in DILATIONS]
    out_specs = [_rows(tm, d)] + [_rows(tm // dl, dl * gw3) for dl in DILATIONS]
    return pl.pallas_call(
        body, name="attn_qkv", out_shape=outs, grid=(s // tm, 3),
        in_specs=[_rows(tm, d), _full((1, d)), pl.BlockSpec((gw3, d), lambda i, j: (j, 0)), _rows(tm, ROPE_COLS)]
        + [_full((tm, tm))] * 2,
        out_specs=out_specs, compiler_params=_params("parallel", "arbitrary"),
    )(x, g, wt, rope, _class_major(tm, DILATIONS[1]), _class_major(tm, DILATIONS[2]))


def _head_masks(rows=SPAN):
    lane = lax.broadcasted_iota(jnp.int32, (rows, 128), 1)
    masks = [lane < HEAD_DIM, lane >= HEAD_DIM]
    lane1 = lax.broadcasted_iota(jnp.int32, (1, 128), 1)
    keep = [jnp.where(lane1 < HEAD_DIM, 1.0, 0.0).astype(BF16), jnp.where(lane1 >= HEAD_DIM, 1.0, 0.0).astype(BF16)]
    return masks, keep


def _band_mask(b):
    row = lax.broadcasted_iota(jnp.int32, (2 * SPAN, 2 * SPAN), 0) & (SPAN - 1)
    col = lax.broadcasted_iota(jnp.int32, (2 * SPAN, 2 * SPAN), 1)
    no_prev = jnp.where(b > 0, 0, 4 * SPAN)
    return ((col < SPAN) & (col >= row + no_prev)) | ((col >= SPAN) & (col - SPAN <= row))


def _attn_fwd(qv, grp, dil):
    l = qv.shape[0]
    s = l * dil
    nb = l // SPAN
    nq = next(n for n in (4, 2, 1) if nb % n == 0)

    def body(q_ref, kp_ref, kc_ref, vp_ref, vc_ref, o_ref, l_ref):
        b = pl.program_id(1)
        masks, keep = _head_masks()
        for qb in range(nq):
            valid = _band_mask(b * nq + qb)
            rows = slice(qb * SPAN, (qb + 1) * SPAN)
            before = slice((qb - 1) * SPAN, qb * SPAN)
            for p in range(GROUP_WIDTH // 128):
                sl = slice(p * 128, (p + 1) * 128)
                qp = q_ref[rows, sl]
                kk = jnp.concatenate([kp_ref[:, sl] if qb == 0 else kc_ref[before, sl], kc_ref[rows, sl]], axis=0)
                vv = jnp.concatenate([vp_ref[:, sl] if qb == 0 else vc_ref[before, sl], vc_ref[rows, sl]], axis=0)
                q2 = jnp.concatenate([qp * keep[0], qp * keep[1]], axis=0)
                sc = lax.dot_general(q2, kk, NT, preferred_element_type=F32) * (HEAD_DIM ** -0.5)
                sc = jnp.where(valid, sc, -1e30)
                mx = jnp.max(sc, axis=-1, keepdims=True)
                pe = jnp.exp(sc - mx)
                den = jnp.sum(pe, axis=-1, keepdims=True)
                out = jnp.dot(pe.astype(BF16), vv, preferred_element_type=F32) / den
                lse = jnp.broadcast_to(mx + jnp.log(den), (2 * SPAN, 128))
                o_ref[rows, sl] = jnp.where(masks[0], out[:SPAN], out[SPAN:]).astype(BF16)
                l_ref[rows, sl] = jnp.where(masks[0], lse[:SPAN], lse[SPAN:])

    blk = (nq * SPAN, GROUP_WIDTH)
    cur = lambda t: pl.BlockSpec(blk, lambda r, b: (b, r * 3 + t))
    prev = lambda t: pl.BlockSpec((SPAN, GROUP_WIDTH), lambda r, b: (jnp.maximum(nq * b - 1, 0), r * 3 + t))
    out = pl.BlockSpec(blk, lambda r, b: (b, r))
    o, lse = pl.pallas_call(
        body, name=f"attn_fwd_g{grp}",
        out_shape=[jax.ShapeDtypeStruct((l, dil * GROUP_WIDTH), BF16), jax.ShapeDtypeStruct((l, dil * GROUP_WIDTH), F32)],
        grid=(dil, nb // nq), in_specs=[cur(0), prev(1), cur(1), prev(2), cur(2)], out_specs=[out, out],
        compiler_params=_params("parallel", "arbitrary"),
    )(qv, qv, qv, qv, qv)
    return o.reshape(s, GROUP_WIDTH), lse.reshape(s, GROUP_WIDTH)


def _resnorm_store(y, x_ref, g_ref, y_ref, xo_ref):
    r = lax.rsqrt(jnp.mean(y * y, axis=-1, keepdims=True) + EPS)
    y_ref[...] = y
    xo_ref[...] = x_ref[...] + y * r * g_ref[...]


def _mix_wo(os_, ls_, wot, x, g, tm=512):
    s, d = x.shape
    gw = wot.shape[1]
    tm = min(tm, s)

    def body(o0, o1, o2, l0, l1, l2, w_ref, x_ref, g_ref, y_ref, xo_ref, mixed_ref, lse_ref):
        a0, a1, a2 = l0[...], l1[...], l2[...]
        mx = jnp.maximum(jnp.maximum(a0, a1), a2)
        e0, e1, e2 = jnp.exp(a0 - mx), jnp.exp(a1 - mx), jnp.exp(a2 - mx)
        den = e0 + e1 + e2
        mixed = (e0 / den) * o0[...].astype(F32) + (e1 / den) * o1[...].astype(F32) + (e2 / den) * o2[...].astype(F32)
        mixed_ref[...] = mixed.astype(BF16)
        lse_ref[...] = mx + jnp.log(den)
        y = lax.dot_general(mixed.astype(BF16), w_ref[...], NT, preferred_element_type=F32)
        _resnorm_store(y, x_ref, g_ref, y_ref, xo_ref)

    return pl.pallas_call(
        body, name="mix_wo",
        out_shape=[jax.ShapeDtypeStruct((s, d), F32), jax.ShapeDtypeStruct((s, d), F32),
                   jax.ShapeDtypeStruct((s, gw), BF16), jax.ShapeDtypeStruct((s, gw), F32)],
        grid=(s // tm,), in_specs=[_rows(tm, gw)] * 6 + [_full((d, gw)), _rows(tm, d), _full((1, d))],
        out_specs=[_rows(tm, d), _rows(tm, d), _rows(tm, gw), _rows(tm, gw)],
        compiler_params=_params("parallel"),
    )(*os_, *ls_, wot, x, g)


def _matmul_resnorm(a, w, x, g, *, name, bias=None, tm=512):
    s, k = a.shape
    d = w.shape[1]
    tm = min(tm, s)

    def body(*refs):
        a_ref, w_ref = refs[:2]
        b_ref = refs[2] if bias is not None else None
        x_ref, g_ref, y_ref, xo_ref = refs[-4:]
        y = jnp.dot(a_ref[...], w_ref[...], preferred_element_type=F32)
        if b_ref is not None:
            y = y + b_ref[...]
        _resnorm_store(y, x_ref, g_ref, y_ref, xo_ref)

    in_specs = [_rows(tm, k), _full((k, d))] + ([_full((1, d))] if bias is not None else []) + [_rows(tm, d), _full((1, d))]
    args = [a, w] + ([bias] if bias is not None else []) + [x, g]
    return pl.pallas_call(
        body, name=name, out_shape=[jax.ShapeDtypeStruct((s, d), F32)] * 2, grid=(s // tm,),
        in_specs=in_specs, out_specs=[_rows(tm, d)] * 2, compiler_params=_params("parallel"),
    )(*args)


FFN_SUB = 256


def _conv3_rows(z_ref, halo_ref, rb, sub, cs, first):
    zc = z_ref[rb * sub:(rb + 1) * sub, cs].astype(F32)
    if rb == 0:
        halo = halo_ref[:, cs].astype(F32) * jnp.where(first, 0.0, 1.0)
    else:
        halo = z_ref[rb * sub - 16:rb * sub, cs].astype(F32)[8:]
    z2, z1 = _conv3_taps(zc, halo)
    return z2, z1, zc


def _conv3_taps(z, halo):
    row = lax.broadcasted_iota(jnp.int32, (8, z.shape[1]), 0)
    h6, h7 = halo[6:7, :], halo[7:8, :]
    r1, r2 = pltpu.roll(z, 1, 0), pltpu.roll(z, 2, 0)
    z1 = jnp.concatenate([jnp.where(row == 0, h7, r1[0:8]), r1[8:]], axis=0)
    z2 = jnp.concatenate([jnp.where(row == 0, h6, jnp.where(row == 1, h7, r2[0:8])), r2[8:]], axis=0)
    return z2, z1


def _ffn_cols(f):
    return _tile(f)


def _lane_chunks(width, fn):
    def step(k, carry):
        fn(pl.ds(pl.multiple_of(k * 128, 128), 128))
        return carry

    lax.fori_loop(0, width // 128, step, 0)


def _ffn_act(z, w_dw, b_dw, tm=1024):
    s, f2 = z.shape
    f = f2 // 2
    tm = min(tm, s)
    sub = min(FFN_SUB, tm)
    tc = _ffn_cols(f)
    nfc = f // tc

    def body(zu, zg, hu, hg, wu, wg, bu, bg, o_ref):
        first = pl.program_id(0) == 0

        def chunk(cs):
            for rb in range(tm // sub):
                def conv(z_ref, h_ref, w_ref, b_ref):
                    z2, z1, zc = _conv3_rows(z_ref, h_ref, rb, sub, cs, first)
                    return w_ref[0:1, cs] * z2 + w_ref[1:2, cs] * z1 + w_ref[2:3, cs] * zc + b_ref[:, cs]

                up, gate = conv(zu, hu, wu, bu), conv(zg, hg, wg, bg)
                o_ref[rb * sub:(rb + 1) * sub, cs] = (gate * _sigmoid(gate) * up).astype(BF16)

        _lane_chunks(tc, chunk)

    hb = tm // 8
    tile = lambda off: pl.BlockSpec((tm, tc), lambda i, j: (i, off + j))
    halo = lambda off: pl.BlockSpec((8, tc), lambda i, j: (jnp.maximum(i * hb - 1, 0), off + j))
    prm = lambda rows, off: pl.BlockSpec((rows, tc), lambda i, j: (0, off + j))
    return pl.pallas_call(
        body, name="ffn_act", out_shape=jax.ShapeDtypeStruct((s, f), BF16), grid=(s // tm, nfc),
        in_specs=[tile(0), tile(nfc), halo(0), halo(nfc), prm(FFN_CONV, 0), prm(FFN_CONV, nfc), prm(1, 0), prm(1, nfc)],
        out_specs=pl.BlockSpec((tm, tc), lambda i, j: (i, j)), compiler_params=_params("parallel", "parallel"),
    )(z, z, z, z, w_dw, w_dw, b_dw, b_dw)


def _shifted_planes(ext_ref):
    rows = ext_ref.shape[1]
    for s in range(1, 8):
        ext_ref[s, 0:rows - 8, :] = ext_ref[0, s:s + rows - 8, :]


def _window(ext_ref, off, tm, cs):
    s = off % 8
    return ext_ref[s, off - s:off - s + tm, cs]


def _conv_taps(ext_ref, w_ref, offs, tm, out_ref):
    def chunk(cs):
        acc = w_ref[0:1, cs] * _window(ext_ref, offs[0], tm, cs)
        for j in range(1, len(offs)):
            acc = acc + w_ref[j:j + 1, cs] * _window(ext_ref, offs[j], tm, cs)
        out_ref[:, cs] = acc

    _lane_chunks(out_ref.shape[1], chunk)


def _glu_planes(ag_ref, halo_ref, ext_ref, first, c):
    hal = halo_ref[...].astype(F32)
    ext_ref[0, 0:CONV_HALO, :] = hal[:, :c] * _sigmoid(hal[:, c:]) * jnp.where(first, 0.0, 1.0)
    ag = ag_ref[...].astype(F32)
    ext_ref[0, CONV_HALO:, :] = ag[:, :c] * _sigmoid(ag[:, c:])
    _shifted_planes(ext_ref)


def _layernorm_stats(u1):
    mu = jnp.mean(u1, axis=-1, keepdims=True)
    cen = u1 - mu
    rstd = lax.rsqrt(jnp.mean(cen * cen, axis=-1, keepdims=True) + EPS)
    return cen * rstd, rstd


def _conv_mid(ag, w_dw, b_dw, ln_g, ln_b, tm=512):
    s, c2 = ag.shape
    c = c2 // 2
    tm = min(tm, s)

    def body(ag_ref, halo_ref, w_ref, b_ref, g_ref, bb_ref, o_ref, u1_ref, ext_ref):
        _glu_planes(ag_ref, halo_ref, ext_ref, pl.program_id(0) == 0, c)
        base = CONV_HALO - (CONV_KERNEL - 1)
        _conv_taps(ext_ref, w_ref, [base + j for j in range(CONV_KERNEL)], tm, u1_ref)
        xh, _ = _layernorm_stats(u1_ref[...] + b_ref[...])
        u2 = xh * g_ref[...] + bb_ref[...]
        o_ref[...] = (u2 * _sigmoid(u2)).astype(BF16)

    hb = tm // CONV_HALO
    return pl.pallas_call(
        body, name="conv_mid", out_shape=[jax.ShapeDtypeStruct((s, c), BF16), jax.ShapeDtypeStruct((s, c), F32)], grid=(s // tm,),
        in_specs=[_rows(tm, c2), pl.BlockSpec((CONV_HALO, c2), lambda i: (jnp.maximum(i * hb - 1, 0), 0)),
                  _full((CONV_KERNEL, c)), _full((1, c)), _full((1, c)), _full((1, c))],
        out_specs=[_rows(tm, c), _rows(tm, c)], scratch_shapes=[pltpu.VMEM((8, CONV_HALO + tm, c), F32)],
        compiler_params=_params("arbitrary"),
    )(ag, ag, w_dw, b_dw, ln_g, ln_b)


def _loss_grad(xo, target, tm=1024):
    s, d = xo.shape
    tm = min(tm, s)

    def body(x_ref, t_ref, dx_ref, loss_ref):
        @pl.when(pl.program_id(0) == 0)
        def _():
            loss_ref[...] = jnp.zeros_like(loss_ref)

        err = x_ref[...] - t_ref[...]
        dx_ref[...] = err * (1.0 / d)
        loss_ref[...] += 0.5 * jnp.sum(jnp.mean(err * err, axis=-1, keepdims=True))

    return pl.pallas_call(
        body, name="loss_grad", out_shape=[jax.ShapeDtypeStruct((s, d), F32), jax.ShapeDtypeStruct((1, 128), F32)],
        grid=(s // tm,), in_specs=[_rows(tm, d)] * 2, out_specs=[_rows(tm, d), _full((1, 128))],
        compiler_params=_params("arbitrary"),
    )(xo, target)


def _postnorm_bwd(y, g, dxo, *, name, with_bias_grad=False, tm=1024):
    s, d = y.shape
    tm = min(tm, s)

    def body(y_ref, g_ref, dx_ref, dy_ref, dg_ref, *rest):
        @pl.when(pl.program_id(0) == 0)
        def _():
            dg_ref[...] = jnp.zeros_like(dg_ref)
            for r_ in rest:
                r_[...] = jnp.zeros_like(r_)

        yv, dxo_v = y_ref[...], dx_ref[...]
        r = lax.rsqrt(jnp.mean(yv * yv, axis=-1, keepdims=True) + EPS)
        yh = yv * r
        dyh = dxo_v * g_ref[...]
        dy = r * (dyh - yh * jnp.mean(dyh * yh, axis=-1, keepdims=True))
        dy_ref[...] = dy.astype(BF16)
        dg_ref[...] += jnp.sum(dxo_v * yh, axis=0, keepdims=True)
        for r_ in rest:
            r_[...] += jnp.sum(dy, axis=0, keepdims=True)

    nacc = 2 if with_bias_grad else 1
    return pl.pallas_call(
        body, name=name, out_shape=[jax.ShapeDtypeStruct((s, d), BF16)] + [jax.ShapeDtypeStruct((1, d), F32)] * nacc,
        grid=(s // tm,), in_specs=[_rows(tm, d), _full((1, d)), _rows(tm, d)],
        out_specs=[_rows(tm, d)] + [_full((1, d))] * nacc, compiler_params=_params("arbitrary"),
    )(y, g, dxo)


def _matmul(gmat, w, *, name, out_dtype, transposed_w, tm=512):
    s, k = gmat.shape
    n = w.shape[0] if transposed_w else w.shape[1]
    tm = min(tm, s)

    def body(g_ref, w_ref, o_ref):
        if transposed_w:
            acc = lax.dot_general(g_ref[...], w_ref[...], NT, preferred_element_type=F32)
        else:
            acc = jnp.dot(g_ref[...], w_ref[...], preferred_element_type=F32)
        o_ref[...] = acc.astype(out_dtype)

    return pl.pallas_call(
        body, name=name, out_shape=jax.ShapeDtypeStruct((s, n), out_dtype), grid=(s // tm,),
        in_specs=[_rows(tm, k), _full(w.shape)], out_specs=_rows(tm, n), compiler_params=_params("parallel"),
    )(gmat, w)


def _matmul_prenorm_bwd(pieces, wt, x, g, dres, *, name, tm=256):
    s, d = x.shape
    tm = min(tm, s)
    np_ = len(pieces)

    def body(*refs):
        p_refs, w_refs = refs[:np_], refs[np_:2 * np_]
        x_ref, g_ref, r_ref, dx_ref, dg_ref = refs[2 * np_:]

        @pl.when(pl.program_id(0) == 0)
        def _():
            dg_ref[...] = jnp.zeros_like(dg_ref)

        dh = None
        for p_ref, w_ref in zip(p_refs, w_refs):
            t = jnp.dot(p_ref[...], w_ref[...], preferred_element_type=F32)
            dh = t if dh is None else dh + t
        xv = x_ref[...]
        r = lax.rsqrt(jnp.mean(xv * xv, axis=-1, keepdims=True) + EPS)
        xh = xv * r
        dyh = dh * g_ref[...]
        dx_ref[...] = r_ref[...] + r * (dyh - xh * jnp.mean(dyh * xh, axis=-1, keepdims=True))
        dg_ref[...] += jnp.sum(dh * xh, axis=0, keepdims=True)

    in_specs = []
    for _, c0, kc, _ in pieces:
        assert c0 % kc == 0
        in_specs.append(pl.BlockSpec((tm, kc), lambda i, _b=c0 // kc: (i, _b)))
    for _, _, kc, r0 in pieces:
        assert r0 % kc == 0
        in_specs.append(pl.BlockSpec((kc, d), lambda i, _b=r0 // kc: (_b, 0)))
    in_specs += [_rows(tm, d), _full((1, d)), _rows(tm, d)]
    return pl.pallas_call(
        body, name=name, out_shape=[jax.ShapeDtypeStruct((s, d), F32), jax.ShapeDtypeStruct((1, d), F32)],
        grid=(s // tm,), in_specs=in_specs, out_specs=[_rows(tm, d), _full((1, d))],
        compiler_params=_params("arbitrary"),
    )(*[p[0] for p in pieces], *[wt] * np_, x, g, dres)


def _weight_grad(a, gmat, *, name, a_col0=0, ka=None, out=None, out_shape=None, layer=0, row0=0, ts=1024):
    s = a.shape[0]
    ka = a.shape[1] if ka is None else ka
    n = gmat.shape[1]
    ts = min(ts, s)
    tka = _tile(ka, a_col0, row0)
    shape = out.shape if out is not None else out_shape
    nsteps = s // ts

    def body(a_ref, g_ref, *rest):
        o_ref, acc_ref = rest[-2:]
        i = pl.program_id(1)

        @pl.when(i == 0)
        def _():
            acc_ref[...] = jnp.zeros_like(acc_ref)

        acc_ref[...] += lax.dot_general(a_ref[...], g_ref[...], TN, preferred_element_type=F32)

        @pl.when(i == nsteps - 1)
        def _():
            o_ref[...] = acc_ref[...].astype(BF16)

    in_specs = [pl.BlockSpec((ts, tka), lambda k, i: (i, a_col0 // tka + k)), pl.BlockSpec((ts, n), lambda k, i: (i, 0))]
    args = [a, gmat]
    aliases = {}
    if out is not None:
        in_specs.append(ANY)
        args.append(out)
        aliases = {2: 0}
    return pl.pallas_call(
        body, name=name, out_shape=jax.ShapeDtypeStruct(shape, BF16), grid=(ka // tka, nsteps), in_specs=in_specs,
        out_specs=pl.BlockSpec((None, tka, n), lambda k, i: (layer, row0 // tka + k, 0)),
        scratch_shapes=[pltpu.VMEM((tka, n), F32)],
        input_output_aliases=aliases, compiler_params=_params("parallel", "arbitrary"),
    )(*args)


def _ffn_act_bwd(z, dact, w_dw, b_dw, tm=512):
    s, f2 = z.shape
    f = f2 // 2
    tm = min(tm, s)
    sub = min(FFN_SUB // 2, tm)
    tc = _ffn_cols(f)
    nfc = f // tc

    def body(zu, zg, hu, hg, wu, wg, bu, bg, da_ref, du_ref, dgt_ref, dbu_ref, dbg_ref, dwu_ref, dwg_ref):
        i = pl.program_id(1)

        @pl.when(i == 0)
        def _():
            for r_ in (dbu_ref, dbg_ref, dwu_ref, dwg_ref):
                r_[...] = jnp.zeros_like(r_)

        def chunk(cs):
            for rb in range(tm // sub):
                rows = slice(rb * sub, (rb + 1) * sub)

                def conv(z_ref, h_ref, w_ref, b_ref):
                    taps = _conv3_rows(z_ref, h_ref, rb, sub, cs, i == 0)
                    return taps, w_ref[0:1, cs] * taps[0] + w_ref[1:2, cs] * taps[1] + w_ref[2:3, cs] * taps[2] + b_ref[:, cs]

                taps_u, up = conv(zu, hu, wu, bu)
                taps_g, gate = conv(zg, hg, wg, bg)
                da = da_ref[rows, cs].astype(F32)
                sg = _sigmoid(gate)
                d_up = da * (gate * sg)
                d_gate = da * up * (sg * (1.0 + gate * (1.0 - sg)))
                du_ref[rows, cs] = d_up.astype(BF16)
                dgt_ref[rows, cs] = d_gate.astype(BF16)
                for dv, taps, db_ref, dw_ref in ((d_up, taps_u, dbu_ref, dwu_ref), (d_gate, taps_g, dbg_ref, dwg_ref)):
                    db_ref[:, cs] += jnp.sum(dv, axis=0, keepdims=True)
                    for k_, tap in enumerate(taps):
                        dw_ref[k_:k_ + 1, cs] += jnp.sum(dv * tap, axis=0, keepdims=True)

        _lane_chunks(tc, chunk)

    hb = tm // 8
    tile = lambda off: pl.BlockSpec((tm, tc), lambda j, i: (i, off + j))
    halo = lambda off: pl.BlockSpec((8, tc), lambda j, i: (jnp.maximum(i * hb - 1, 0), off + j))
    prm = lambda rows, off: pl.BlockSpec((rows, tc), lambda j, i: (0, off + j))
    acc = lambda rows: pl.BlockSpec((rows, tc), lambda j, i: (0, j))
    return pl.pallas_call(
        body, name="ffn_act_bwd",
        out_shape=[jax.ShapeDtypeStruct((s, f), BF16)] * 2 + [jax.ShapeDtypeStruct((1, f), F32)] * 2
        + [jax.ShapeDtypeStruct((FFN_CONV, f), F32)] * 2,
        grid=(nfc, s // tm),
        in_specs=[tile(0), tile(nfc), halo(0), halo(nfc), prm(FFN_CONV, 0), prm(FFN_CONV, nfc), prm(1, 0), prm(1, nfc), tile(0)],
        out_specs=[tile(0), tile(0), acc(1), acc(1), acc(FFN_CONV), acc(FFN_CONV)],
        compiler_params=_params("parallel", "arbitrary"),
    )(z, z, z, z, w_dw, w_dw, b_dw, b_dw, dact)


def _conv3_transpose(dug, w_dw, col0, tm=1024):
    s, f = dug.shape
    tm = min(tm, s)
    sub = min(FFN_SUB, tm)
    nsub = tm // sub
    tc = _ffn_cols(f)
    nfc = f // tc
    nrow = s // tm
    off = col0 // tc

    def body(d_ref, n_ref, w_ref, o_ref):
        keep_next = jnp.where(pl.program_id(0) == nrow - 1, 0.0, 1.0)

        def chunk(cs):
            for rb in range(nsub):
                rows = slice(rb * sub, (rb + 1) * sub)
                dv = d_ref[rows, cs].astype(F32)
                if rb == nsub - 1:
                    nxt = n_ref[:, cs].astype(F32) * keep_next
                else:
                    nxt = d_ref[(rb + 1) * sub:(rb + 1) * sub + 16, cs].astype(F32)[:8]
                n0, n1 = nxt[0:1, :], nxt[1:2, :]
                row = lax.broadcasted_iota(jnp.int32, (8, dv.shape[1]), 0)
                r1, r2 = pltpu.roll(dv, sub - 1, 0), pltpu.roll(dv, sub - 2, 0)
                d1 = jnp.concatenate([r1[:sub - 8], jnp.where(row == 7, n0, r1[sub - 8:])], axis=0)
                d2 = jnp.concatenate([r2[:sub - 8], jnp.where(row == 7, n1, jnp.where(row == 6, n0, r2[sub - 8:]))], axis=0)
                o_ref[rows, cs] = (w_ref[2:3, cs] * dv + w_ref[1:2, cs] * d1 + w_ref[0:1, cs] * d2).astype(BF16)

        _lane_chunks(tc, chunk)

    hb = tm // 8
    return pl.pallas_call(
        body, name="conv3_transpose", out_shape=jax.ShapeDtypeStruct((s, f), BF16), grid=(nrow, nfc),
        in_specs=[pl.BlockSpec((tm, tc), lambda i, j: (i, j)),
                  pl.BlockSpec((8, tc), lambda i, j: (jnp.minimum((i + 1) * hb, s // 8 - 1), j)),
                  pl.BlockSpec((FFN_CONV, tc), lambda i, j: (0, off + j))],
        out_specs=pl.BlockSpec((tm, tc), lambda i, j: (i, j)), compiler_params=_params("parallel", "parallel"),
    )(dug, dug, w_dw)


def _conv_mid_bwd(ag, u1, du3, b_dw, ln_g, ln_b, tm=256):
    s, c2 = ag.shape
    c = c2 // 2
    tm = min(tm, s)

    def body(ag_ref, halo_ref, u1in_ref, du_ref, b_ref, g_ref, bb_ref, o_ref, dlg_ref, dlb_ref, db_ref, dw_ref, ext_ref, u1_ref):
        @pl.when(pl.program_id(0) == 0)
        def _():
            for r_ in (dlg_ref, dlb_ref, db_ref, dw_ref):
                r_[...] = jnp.zeros_like(r_)

        _glu_planes(ag_ref, halo_ref, ext_ref, pl.program_id(0) == 0, c)
        xh, rstd = _layernorm_stats(u1in_ref[...] + b_ref[...])
        u2 = xh * g_ref[...] + bb_ref[...]
        sg = _sigmoid(u2)
        du2 = du_ref[...] * (sg * (1.0 + u2 * (1.0 - sg)))
        dlg_ref[...] += jnp.sum(du2 * xh, axis=0, keepdims=True)
        dlb_ref[...] += jnp.sum(du2, axis=0, keepdims=True)
        dxh = du2 * g_ref[...]
        du1 = rstd * (dxh - jnp.mean(dxh, axis=-1, keepdims=True) - xh * jnp.mean(dxh * xh, axis=-1, keepdims=True))
        o_ref[...] = du1.astype(BF16)
        db_ref[...] += jnp.sum(du1, axis=0, keepdims=True)
        u1_ref[...] = du1
        base = CONV_HALO - (CONV_KERNEL - 1)

        def chunk(cs):
            dc = u1_ref[:, cs]
            for j in range(CONV_KERNEL):
                dw_ref[j:j + 1, cs] += jnp.sum(dc * _window(ext_ref, base + j, tm, cs), axis=0, keepdims=True)

        _lane_chunks(c, chunk)

    hb = tm // CONV_HALO
    vec = _full((1, c))
    return pl.pallas_call(
        body, name="conv_mid_bwd",
        out_shape=[jax.ShapeDtypeStruct((s, c), BF16)] + [jax.ShapeDtypeStruct((1, c), F32)] * 3
        + [jax.ShapeDtypeStruct((CONV_HALO, c), F32)],
        grid=(s // tm,),
        in_specs=[_rows(tm, c2), pl.BlockSpec((CONV_HALO, c2), lambda i: (jnp.maximum(i * hb - 1, 0), 0)), _rows(tm, c),
                  _rows(tm, c), vec, vec, vec],
        out_specs=[_rows(tm, c), vec, vec, vec, _full((CONV_HALO, c))],
        scratch_shapes=[pltpu.VMEM((8, CONV_HALO + tm, c), F32), pltpu.VMEM((tm, c), F32)],
        compiler_params=_params("arbitrary"),
    )(ag, ag, u1, du3, b_dw, ln_g, ln_b)


def _glu_conv_bwd(du1, ag, w_dw, tm=512):
    s, c = du1.shape
    tm = min(tm, s)
    nrow = s // tm

    def body(d_ref, n_ref, ag_ref, w_ref, o_ref, db_ref, ext_ref, du0_ref):
        @pl.when(pl.program_id(0) == 0)
        def _():
            db_ref[...] = jnp.zeros_like(db_ref)

        ext_ref[0, 0:tm, :] = d_ref[...].astype(F32)
        ext_ref[0, tm:, :] = n_ref[...].astype(F32) * jnp.where(pl.program_id(0) == nrow - 1, 0.0, 1.0)
        _shifted_planes(ext_ref)
        top = CONV_KERNEL - 1
        _conv_taps(ext_ref, w_ref, [top - j for j in range(CONV_KERNEL)], tm, du0_ref)
        du0 = du0_ref[...]
        ag = ag_ref[...].astype(F32)
        a, gt = ag[:, :c], ag[:, c:]
        sg = _sigmoid(gt)
        da = du0 * sg
        dgt = du0 * a * (sg * (1.0 - sg))
        o_ref[:, :c] = da.astype(BF16)
        o_ref[:, c:] = dgt.astype(BF16)
        db_ref[:, :c] += jnp.sum(da, axis=0, keepdims=True)
        db_ref[:, c:] += jnp.sum(dgt, axis=0, keepdims=True)

    hb = tm // CONV_HALO
    return pl.pallas_call(
        body, name="glu_conv_bwd",
        out_shape=[jax.ShapeDtypeStruct((s, 2 * c), BF16), jax.ShapeDtypeStruct((1, 2 * c), F32)], grid=(nrow,),
        in_specs=[_rows(tm, c), pl.BlockSpec((CONV_HALO, c), lambda i: (jnp.minimum((i + 1) * hb, s // CONV_HALO - 1), 0)),
                  _rows(tm, 2 * c), _full((CONV_KERNEL, c))],
        out_specs=[_rows(tm, 2 * c), _full((1, 2 * c))],
        scratch_shapes=[pltpu.VMEM((8, tm + CONV_HALO, c), F32), pltpu.VMEM((tm, c), F32)],
        compiler_params=_params("arbitrary"),
    )(du1, du1, ag, w_dw)


def _head_rows(v, mask):
    return jnp.max(jnp.where(mask, v, -jnp.inf), axis=-1, keepdims=True)


def _attn_bwd(qv, dmix, mixed, lse, rope, grp, dil, ties=()):
    l = qv.shape[0]
    s = l * dil
    nb = l // SPAN
    view = lambda t: t.reshape(l, dil * t.shape[1])
    scale = HEAD_DIM ** -0.5
    gw = GROUP_WIDTH

    def body(*refs):
        q_ref, kp_ref, kc_ref, vp_ref, vc_ref, do_ref, mx_ref, l_ref, tab_ref, tabp_ref = refs[:10]
        dq_ref, dkv_ref, carry_ref = refs[-3:]
        b = pl.program_id(1)

        @pl.when(b < nb)
        def _():
            valid = _band_mask(b)
            masks, keep = _head_masks()
            for p in range(gw // 128):
                sl = slice(p * 128, (p + 1) * 128)
                sl_v = slice(gw + p * 128, gw + (p + 1) * 128)
                qp, dop = q_ref[:, sl], do_ref[:, sl]
                kk = jnp.concatenate([kp_ref[:, sl], kc_ref[:, sl]], axis=0)
                vv = jnp.concatenate([vp_ref[:, sl], vc_ref[:, sl]], axis=0)
                prod = dop.astype(F32) * mx_ref[:, sl].astype(F32)
                lsep = l_ref[:, sl]
                q2 = jnp.concatenate([qp * keep[0], qp * keep[1]], axis=0)
                do2 = jnp.concatenate([dop * keep[0], dop * keep[1]], axis=0)
                lse2 = jnp.concatenate([_head_rows(lsep, masks[h]) for h in range(2)], axis=0)
                dbar2 = jnp.concatenate([jnp.sum(jnp.where(masks[h], prod, 0.0), axis=-1, keepdims=True) for h in range(2)], axis=0)
                sc = lax.dot_general(q2, kk, NT, preferred_element_type=F32) * scale
                pe = jnp.where(valid, jnp.exp(sc - lse2), 0.0)
                dp = lax.dot_general(do2, vv, NT, preferred_element_type=F32)
                ds = (pe * (dp - dbar2) * scale).astype(BF16)
                dq2 = jnp.dot(ds, kk, preferred_element_type=F32)
                dq = jnp.where(masks[0], dq2[:SPAN], dq2[SPAN:])
                dq_ref[:, sl] = _rope_transpose(dq, tab_ref[...]).astype(BF16)
                dk = lax.dot_general(ds, q2, TN, preferred_element_type=F32)
                dv = lax.dot_general(pe.astype(BF16), do2, TN, preferred_element_type=F32)

                @pl.when(b > 0)
                def _():
                    dk_prev = carry_ref[:, sl] + dk[:SPAN]
                    dkv_ref[:, sl] = _rope_transpose(dk_prev, tabp_ref[...]).astype(BF16)
                    dkv_ref[:, sl_v] = (carry_ref[:, sl_v] + dv[:SPAN]).astype(BF16)

                carry_ref[:, sl] = dk[SPAN:]
                carry_ref[:, sl_v] = dv[SPAN:]

        @pl.when(b == nb)
        def _():
            for p in range(gw // 128):
                sl = slice(p * 128, (p + 1) * 128)
                sl_v = slice(gw + p * 128, gw + (p + 1) * 128)
                dkv_ref[:, sl] = _rope_transpose(carry_ref[:, sl], tabp_ref[...]).astype(BF16)
                dkv_ref[:, sl_v] = carry_ref[:, sl_v].astype(BF16)

    blk = (SPAN, gw)
    cb = lambda b: jnp.minimum(b, nb - 1)
    cur = lambda t: pl.BlockSpec(blk, lambda r, b: (cb(b), r * 3 + t))
    prev = lambda t: pl.BlockSpec(blk, lambda r, b: (jnp.maximum(cb(b) - 1, 0), r * 3 + t))
    own = pl.BlockSpec(blk, lambda r, b: (cb(b), r))
    tab = pl.BlockSpec((SPAN, ROPE_COLS), lambda r, b: (cb(b), r))
    tab_prev = pl.BlockSpec((SPAN, ROPE_COLS), lambda r, b: (jnp.maximum(b - 1, 0), r))
    dq, dkv = pl.pallas_call(
        body, name=f"attn_bwd_g{grp}",
        out_shape=[jax.ShapeDtypeStruct((l, dil * gw), BF16), jax.ShapeDtypeStruct((l, dil * 2 * gw), BF16)],
        grid=(dil, nb + 1),
        in_specs=[cur(0), prev(1), cur(1), prev(2), cur(2), own, own, own, tab, tab_prev] + [ANY] * len(ties),
        out_specs=[own, pl.BlockSpec((SPAN, 2 * gw), lambda r, b: (jnp.maximum(b - 1, 0), r))],
        scratch_shapes=[pltpu.VMEM((SPAN, 2 * gw), F32)], compiler_params=_params("parallel", "arbitrary"),
    )(qv, qv, qv, qv, qv, view(dmix), view(mixed), view(lse), view(rope), view(rope), *ties)
    return dq.reshape(s, gw), dkv.reshape(s, 2 * gw)


def _rope_freq_row():
    half = ROT_DIM // 2
    inv = (ROPE_THETA ** (-np.arange(half, dtype=np.float32) / half)).astype(np.float32)
    row = np.zeros((1, 128), np.float32)
    for head in range(128 // HEAD_DIM):
        row[0, head * HEAD_DIM:head * HEAD_DIM + half] = inv
        row[0, head * HEAD_DIM + half:head * HEAD_DIM + ROT_DIM] = inv
    return jnp.asarray(row)


def _ffn_fwd(x, g_pre, g_post, w_up_t, w_dw, b_dw, w_down):
    h, z = _norm_matmul(x, g_pre, w_up_t, tn=_tile(w_up_t.shape[0]), name="ffn_up")
    act = _ffn_act(z, w_dw, b_dw)
    y, xo = _matmul_resnorm(act, w_down, x, g_post, name="ffn_down")
    return xo, (x, h, z, act, y)


def _ffn_bwd(saved, dxo, g_pre, g_post, w_up_t, w_dw, b_dw, w_down):
    x, h, z, act, y = saved
    f = act.shape[1]
    d = x.shape[1]
    dy, dg_post = _postnorm_bwd(y, g_post, dxo, name="ffn_post_bwd")
    dact = _matmul(dy, w_down, name="ffn_dact", out_dtype=BF16, transposed_w=True)
    d_down = _weight_grad(act, dy, name="ffn_dw_down", out_shape=(1, f, d))
    dug_u, dug_g, db_u, db_g, dwd_u, dwd_g = _ffn_act_bwd(z, dact, w_dw, b_dw)
    dz_u = _conv3_transpose(dug_u, w_dw, 0)
    dz_g = _conv3_transpose(dug_g, w_dw, f)
    dx, dg_pre = _matmul_prenorm_bwd([(dz_u, 0, f, 0), (dz_g, 0, f, f)], w_up_t, x, g_pre, dxo, name="ffn_dx")
    d_up_t = _weight_grad(dz_u, h, name="ffn_dw_up", out_shape=(1, 2 * f, d))
    d_up_t = _weight_grad(dz_g, h, name="ffn_dw_up", out=d_up_t, row0=f)
    grads = dict(w_dw=jnp.concatenate([dwd_u, dwd_g], axis=1), b_dw=jnp.concatenate([db_u, db_g], axis=1),
                 g_pre=dg_pre, g_post=dg_post)
    return dx, grads, d_up_t, d_down


def _local_step(x, pos_col, target, p, tie=None, late_weights=None, exchange=None):
    ng = p["norm_g"]
    row = lambda r: ng[r:r + 1]
    freq = _rope_freq_row()
    rope = _rope_tables(pos_col, freq if tie is None else freq + tie[0:1])
    d = x.shape[1]

    h0, *qkv = _qkv_proj(x, row(0), p["w_qkv_t"], rope)
    os_, ls_ = zip(*[_attn_fwd(qkv[g_], g_, d_) for g_, d_ in enumerate(DILATIONS)])
    y_a, x1, mixed, lse = _mix_wo(os_, ls_, p["w_o_t"], x, row(1))
    if late_weights is not None:
        p = {**p, **late_weights(x1)}
    x2, ffn0 = _ffn_fwd(x1, row(2), row(3), p["w_up_t"][0], p["ffn_w_dw"][0], p["ffn_b_dw"][0], p["w_down"][0])
    h1, ag = _norm_matmul(x2, row(4), p["w_pw1_t"], tn=_tile(p["w_pw1_t"].shape[0]), name="conv_pw1", bias=p["b_pw1"])
    u3, u1 = _conv_mid(ag, p["conv_w_dw"], p["conv_b_dw"], p["ln_g"], p["ln_b"])
    y_c, x3 = _matmul_resnorm(u3, p["w_pw2"], x2, row(5), name="conv_pw2", bias=p["b_pw2"])
    x4, ffn1 = _ffn_fwd(x3, row(6), row(7), p["w_up_t"][1], p["ffn_w_dw"][1], p["ffn_b_dw"][1], p["w_down"][1])
    dx4, loss = _loss_grad(x4, target)

    big = [BF16, BF16]

    def tied(r, *tokens):
        tokens = [t for t in tokens if t is not None]
        return row(r) if not tokens else row(r) + jnp.tile(sum(tokens)[0:1], (1, d // 128))

    dx3, gf1, d_up1, d_down1 = _ffn_bwd(ffn1, dx4, row(6), row(7), p["w_up_t"][1], p["ffn_w_dw"][1], p["ffn_b_dw"][1],
                                        p["w_down"][1])
    t0 = exchange.submit("ffn1", [d_up1, d_down1], big) if exchange else None
    dy_c, dg5, db_pw2 = _postnorm_bwd(y_c, tied(5, t0), dx3, name="conv_post_bwd", with_bias_grad=True)
    du3 = _matmul(dy_c, p["w_pw2"], name="conv_du3", out_dtype=F32, transposed_w=True)
    d_wpw2 = _weight_grad(u3, dy_c, name="conv_dw_pw2", out_shape=(1, u3.shape[1], d))
    du1, d_lng, d_lnb, d_cbdw, d_cwdw = _conv_mid_bwd(ag, u1, du3, p["conv_b_dw"], p["ln_g"], p["ln_b"])
    dag, db_pw1 = _glu_conv_bwd(du1, ag, p["conv_w_dw"])
    dx2, dg4 = _matmul_prenorm_bwd([(dag, 0, dag.shape[1], 0)], p["w_pw1_t"], x2, row(4), dx3, name="conv_dx")
    d_wpw1_t = _weight_grad(dag, h1, name="conv_dw_pw1", out_shape=(1, dag.shape[1], d))
    t0 = exchange.advance(dx2) if exchange else None
    t1 = exchange.submit("conv", [d_wpw1_t, d_wpw2], big) if exchange else None
    dx1, gf0, d_up0, d_down0 = _ffn_bwd(ffn0, dx2, row(2), tied(3, t0, t1), p["w_up_t"][0], p["ffn_w_dw"][0], p["ffn_b_dw"][0],
                                        p["w_down"][0])
    t0 = exchange.advance(dx1) if exchange else None
    t1 = exchange.submit("ffn0", [d_up0, d_down0], big) if exchange else None
    dy_a, dg1 = _postnorm_bwd(y_a, tied(1, t0, t1), dx1, name="attn_post_bwd")
    dmix = _matmul(dy_a, p["w_o_t"], name="attn_dmix", out_dtype=BF16, transposed_w=False)
    d_wo_t = _weight_grad(dy_a, mixed, name="attn_dw_o", out_shape=(1, d, GROUP_WIDTH))
    pieces, d_wqkv_t = [], None
    for g_, d_ in enumerate(DILATIONS):
        tok = exchange.advance(dkv) if exchange and g_ > 0 else None
        dq, dkv = _attn_bwd(qkv[g_], dmix, mixed, lse, rope, g_, d_, ties=() if tok is None else (tok,))
        for t, (arr, c0) in enumerate(((dq, 0), (dkv, 0), (dkv, GROUP_WIDTH))):
            r0 = (3 * t + g_) * GROUP_WIDTH
            pieces.append((arr, c0, GROUP_WIDTH, r0))
            d_wqkv_t = _weight_grad(arr, h0, name="attn_dw_qkv", a_col0=c0, ka=GROUP_WIDTH, out=d_wqkv_t,
                                    out_shape=(1, p["w_qkv_t"].shape[0], d), row0=r0)
    t0 = exchange.advance(dkv) if exchange else None
    t1 = exchange.submit("attn", [d_wqkv_t, d_wo_t], big) if exchange else None
    grad_x, dg0 = _matmul_prenorm_bwd(pieces, p["w_qkv_t"], x, tied(0, t0, t1), dx1, name="attn_dx")

    grads = dict(
        norm_g=jnp.concatenate([dg0, dg1, gf0["g_pre"], gf0["g_post"], dg4, dg5, gf1["g_pre"], gf1["g_post"]], axis=0),
        w_qkv_t=d_wqkv_t, w_o_t=d_wo_t, w_pw1_t=d_wpw1_t, b_pw1=db_pw1,
        conv_w_dw=d_cwdw[:CONV_KERNEL], conv_b_dw=d_cbdw, ln_g=d_lng, ln_b=d_lnb, w_pw2=d_wpw2, b_pw2=db_pw2,
        w_up_t=[d_up0, d_up1], ffn_w_dw=jnp.stack([gf0["w_dw"], gf1["w_dw"]]),
        ffn_b_dw=jnp.concatenate([gf0["b_dw"], gf1["b_dw"]], axis=0), w_down=[d_down0, d_down1])
    return loss, grad_x, grads


SMALL_AXIS = dict(norm_g=2, conv_b_pw1=1, conv_w_dw=2, conv_b_dw=1, conv_ln_g=1, conv_ln_b=1, conv_b_pw2=1, ffn_w_dw=2)
SMALL = tuple(SMALL_AXIS)
MATMUL_WEIGHTS = dict(attn_w_qkv=True, conv_w_pw1=True, ffn_w_up=True, conv_w_pw2=False, ffn_w_down=False)


def _pack(arrays, cols, row_multiple):
    flat = jnp.concatenate([a.reshape(-1) for a in arrays])
    rows = -(-flat.shape[0] // cols)
    rows = -(-rows // row_multiple) * row_multiple
    return jnp.pad(flat, (0, rows * cols - flat.shape[0])).reshape(rows, cols)


def _unpack(packed, shapes):
    flat = packed.reshape(packed.shape[:-2] + (-1,))
    out, off = [], 0
    for shp in shapes:
        n = math.prod(shp)
        out.append(flat[..., off:off + n].reshape(packed.shape[:-2] + tuple(shp)))
        off += n
    return out


def _join_shards(stacked, axis):
    moved = jnp.moveaxis(stacked, 0, axis)
    shp = moved.shape
    return moved.reshape(shp[:axis] + (shp[axis] * shp[axis + 1],) + shp[axis + 2:])


def _split_shards(whole, axis):
    shp = whole.shape
    cut = whole.reshape(shp[:axis] + (N_DEV, shp[axis] // N_DEV) + shp[axis + 1:])
    return jnp.moveaxis(cut, axis, 0)


def _row_shard(w, transposed):
    t = jnp.swapaxes(w, 1, 2) if transposed else w
    return t.astype(BF16).reshape(-1, t.shape[-1])


def kernel(x, positions, norm_g, attn_w_qkv, attn_w_o, conv_w_pw1, conv_b_pw1, conv_w_dw, conv_b_dw, conv_ln_g, conv_ln_b, conv_w_pw2, conv_b_pw2, ffn_w_up, ffn_w_dw, ffn_b_dw, ffn_w_down, loss_target, m_norm_g, m_attn_w_qkv, m_attn_w_o, m_conv_w_pw1, m_conv_b_pw1, m_conv_w_dw, m_conv_b_dw, m_conv_ln_g, m_conv_ln_b, m_conv_w_pw2, m_conv_b_pw2, m_ffn_w_up, m_ffn_w_dw, m_ffn_b_dw, m_ffn_w_down, v_norm_g, v_attn_w_qkv, v_attn_w_o, v_conv_w_pw1, v_conv_b_pw1, v_conv_w_dw, v_conv_b_dw, v_conv_ln_g, v_conv_ln_b, v_conv_w_pw2, v_conv_b_pw2, v_ffn_w_up, v_ffn_w_dw, v_ffn_b_dw, v_ffn_w_down):
    w = dict(norm_g=norm_g, attn_w_qkv=attn_w_qkv, attn_w_o=attn_w_o, conv_w_pw1=conv_w_pw1, conv_b_pw1=conv_b_pw1,
             conv_w_dw=conv_w_dw, conv_b_dw=conv_b_dw, conv_ln_g=conv_ln_g, conv_ln_b=conv_ln_b, conv_w_pw2=conv_w_pw2,
             conv_b_pw2=conv_b_pw2, ffn_w_up=ffn_w_up, ffn_w_dw=ffn_w_dw, ffn_w_down=ffn_w_down)
    m = dict(norm_g=m_norm_g, attn_w_qkv=m_attn_w_qkv, attn_w_o=m_attn_w_o, conv_w_pw1=m_conv_w_pw1, conv_b_pw1=m_conv_b_pw1,
             conv_w_dw=m_conv_w_dw, conv_b_dw=m_conv_b_dw, conv_ln_g=m_conv_ln_g, conv_ln_b=m_conv_ln_b, conv_w_pw2=m_conv_w_pw2,
             conv_b_pw2=m_conv_b_pw2, ffn_w_up=m_ffn_w_up, ffn_w_dw=m_ffn_w_dw, ffn_w_down=m_ffn_w_down)
    v = dict(norm_g=v_norm_g, attn_w_qkv=v_attn_w_qkv, attn_w_o=v_attn_w_o, conv_w_pw1=v_conv_w_pw1, conv_b_pw1=v_conv_b_pw1,
             conv_w_dw=v_conv_w_dw, conv_b_dw=v_conv_b_dw, conv_ln_g=v_conv_ln_g, conv_ln_b=v_conv_ln_b, conv_w_pw2=v_conv_w_pw2,
             conv_b_pw2=v_conv_b_pw2, ffn_w_up=v_ffn_w_up, ffn_w_dw=v_ffn_w_dw, ffn_w_down=v_ffn_w_down)
    d = x.shape[-1]

    w_qkv_t, w_o_t, small = _all_gather([_row_shard(attn_w_qkv, True), _row_shard(attn_w_o, True),
                                         _pack([w[n] for n in SMALL], 128, 8)], "gather_first_weights")
    w_qkv_t, w_o_t = w_qkv_t.reshape(-1, d), w_o_t.reshape(d, -1)
    sm = {n: _join_shards(stacked, SMALL_AXIS[n])
          for n, stacked in zip(SMALL, _unpack(small, [w[n].shape for n in SMALL]))}
    late = {n: t for n, t in MATMUL_WEIGHTS.items() if n != "attn_w_qkv"}
    shares = [_row_shard(w[n], t) for n, t in late.items()]
    rows = [s_.shape[0] for s_ in shares]
    late_share = jnp.concatenate(shares, axis=0)
    send_sems, recv_sems, share_thru, land_thru, tie = _gather_start(late_share)
    me = 4 * lax.axis_index("x") + 2 * lax.axis_index("y") + lax.axis_index("c")

    def late_weights(after):
        big = _gather_wait(send_sems, recv_sems, share_thru, land_thru, after)
        big = lax.dynamic_update_slice(big, late_share[None], (me, 0, 0))
        whole, r0 = {}, 0
        for n, nr in zip(late, rows):
            layers = w[n].shape[0]
            seg = big[:, r0:r0 + nr].reshape(N_DEV, layers, nr // layers, d)
            whole[n] = [seg[:, l_].reshape(-1, d) for l_ in range(layers)]
            r0 += nr
        return dict(w_pw1_t=whole["conv_w_pw1"][0], w_pw2=whole["conv_w_pw2"][0], w_up_t=whole["ffn_w_up"],
                    w_down=whole["ffn_w_down"])

    p = dict(norm_g=sm["norm_g"].reshape(-1, d), w_qkv_t=w_qkv_t, w_o_t=w_o_t, b_pw1=sm["conv_b_pw1"],
             conv_w_dw=sm["conv_w_dw"][0], conv_b_dw=sm["conv_b_dw"], ln_g=sm["conv_ln_g"], ln_b=sm["conv_ln_b"],
             b_pw2=sm["conv_b_pw2"], ffn_w_dw=sm["ffn_w_dw"], ffn_b_dw=[ffn_b_dw[0:1], ffn_b_dw[1:2]])

    exchange = _GradExchange()
    loss, grad_x, g = _local_step(x[0], positions.reshape(-1, 1), loss_target[0], p, tie, late_weights, exchange)
    loss = lax.psum(loss[0, 0], ("x", "y", "c"))
    gsmall = dict(norm_g=g["norm_g"].reshape(norm_g.shape[0], 4, -1), conv_b_pw1=g["b_pw1"], conv_w_dw=g["conv_w_dw"][None],
                  conv_b_dw=g["conv_b_dw"], conv_ln_g=g["ln_g"], conv_ln_b=g["ln_b"], conv_b_pw2=g["b_pw2"], ffn_w_dw=g["ffn_w_dw"])
    small_contrib = jnp.concatenate([_split_shards(gsmall[n], SMALL_AXIS[n]).reshape(N_DEV, -1) for n in SMALL], axis=1)
    srows = small.shape[1]
    small_contrib = jnp.pad(small_contrib, ((0, 0), (0, srows * 128 - small_contrib.shape[1]))).reshape(1, N_DEV, srows, 128)
    exchange.advance(grad_x)
    small_sums = _rs_chips([_rs_pair_add(small_contrib, _rs_sibling([small_contrib])[0], exchange.core, F32)])[0]

    outs = {}

    def update(n, reduced):
        gsum = jnp.swapaxes(reduced, 1, 2) if n == "attn_w_o" or MATMUL_WEIGHTS.get(n) else reduced
        outs[n] = (gsum, *_adamw(gsum, w[n], m[n], v[n], "adamw"))

    (s_up1, s_down1), (s_pw1, s_pw2), (s_up0, s_down0) = exchange.results()[:3]
    update("conv_w_pw1", s_pw1)
    update("conv_w_pw2", s_pw2)
    update("ffn_w_up", jnp.concatenate([s_up0, s_up1], axis=0))
    update("ffn_w_down", jnp.concatenate([s_down0, s_down1], axis=0))
    sshapes = [w[n].shape for n in SMALL]
    souts = _sum_adamw(small_sums[0], *[_pack([t[n] for n in SMALL], 128, 8) for t in (w, m, v)], name="sum_adamw_small")
    for n, vals in zip(SMALL, zip(*[_unpack(o, sshapes) for o in souts])):
        outs[n] = vals
    bparts, = _all_gather([_pack([g["ffn_b_dw"]], 128, 8)], "gather_bias_grads")
    bouts = _sum_adamw(bparts, *[_pack([t], 128, 8) for t in (ffn_b_dw, m_ffn_b_dw, v_ffn_b_dw)], name="sum_adamw_bias")
    outs["ffn_b_dw"] = tuple(_unpack(o, [ffn_b_dw.shape])[0] for o in bouts)
    done = [outs[n][1][0, :8, :128] for n in ("conv_w_pw1", "conv_w_pw2", "ffn_w_up", "ffn_w_down")]
    exchange.advance(sum(done) + bouts[1][:8] + souts[1][:8])
    s_qkv, s_wo = exchange.results()[3]
    update("attn_w_qkv", s_qkv)
    update("attn_w_o", s_wo)

    order = ("norm_g", "attn_w_qkv", "attn_w_o", "conv_w_pw1", "conv_b_pw1", "conv_w_dw", "conv_b_dw", "conv_ln_g",
             "conv_ln_b", "conv_w_pw2", "conv_b_pw2", "ffn_w_up", "ffn_w_dw", "ffn_b_dw", "ffn_w_down")
    return (loss, grad_x[None], *[outs[n][0] for n in order], *[outs[n][1] for n in order],
            *[outs[n][2] for n in order], *[outs[n][3] for n in order])
```

```python
import math

import numpy as np
import jax
import jax.numpy as jnp
from jax import lax
from jax.experimental import pallas as pl
from jax.experimental.pallas import tpu as pltpu

F32 = jnp.float32
BF16 = jnp.bfloat16
EPS = 1e-6
N_DEV = 8
HEAD_DIM = 64
GROUP_WIDTH = 512
DILATIONS = (1, 4, 16)
SPAN = 128
ROT_DIM = 16
ROPE_THETA = 500000.0
CONV_KERNEL = 31
CONV_HALO = 32
FFN_CONV = 3
ADAM_LR, ADAM_B1, ADAM_B2, ADAM_EPS, ADAM_WD, ADAM_STEP = 0.001, 0.9, 0.999, 1e-08, 0.01, 10
VMEM_LIMIT_BYTES = 56 * 1024 * 1024
MESH = pl.DeviceIdType.MESH
ANY = pl.BlockSpec(memory_space=pl.ANY)
NT = (((1,), (1,)), ((), ()))
TN = (((0,), (0,)), ((), ()))


def _params(*sem):
    return pltpu.CompilerParams(dimension_semantics=sem, vmem_limit_bytes=VMEM_LIMIT_BYTES)


def _sigmoid(v):
    return pl.reciprocal(1.0 + jnp.exp(-v), approx=True)


def _full(shape):
    return pl.BlockSpec(shape, lambda *_: (0,) * len(shape))


def _rows(tm, width):
    return pl.BlockSpec((tm, width), lambda i, *_: (i, 0))


def _tile(n, *multiples_of):
    for t in (1408, 1024, 512, 384, 256, 128):
        if n % t == 0 and all(o % t == 0 for o in multiples_of):
            return t
    raise ValueError((n, multiples_of))


def _all_gather(shards, name):
    n = len(shards)

    def body(*refs):
        x_refs, out_refs, (send_sems, recv_sems, local_sems) = refs[:n], refs[n:2 * n], refs[2 * n:]
        x, y, c = lax.axis_index("x"), lax.axis_index("y"), lax.axis_index("c")
        me, sibling = (x, y, c), (x, y, 1 - c)
        chips = [(1 - x, y), (x, 1 - y), (1 - x, 1 - y)]

        def rows(w, px, py, pc):
            return out_refs[w].at[4 * px + 2 * py + pc]

        def copy(w, k, block, to, src=None):
            return pltpu.make_async_remote_copy(
                src_ref=rows(w, *block) if src is None else src, dst_ref=rows(w, *block),
                send_sem=send_sems.at[7 * w + k], recv_sem=recv_sems.at[7 * w + k], device_id=to, device_id_type=MESH)

        every = range(n)
        mine = [pltpu.make_async_copy(x_refs[w], rows(w, *me), local_sems.at[w]) for w in every]
        first = [copy(w, 0, me, sibling, src=x_refs[w]) for w in every]
        first += [copy(w, 1 + j, me, (*chip, c), src=x_refs[w]) for w in every for j, chip in enumerate(chips)]
        for cp in mine + first:
            cp.start()
        passed = []
        for j, chip in enumerate(chips):
            for w in every:
                copy(w, 1 + j, (*chip, c), me).wait_recv()
                passed.append(copy(w, 4 + j, (*chip, c), sibling))
                passed[-1].start()
        for w in every:
            copy(w, 0, sibling, me).wait_recv()
            for j, chip in enumerate(chips):
                copy(w, 4 + j, (*chip, 1 - c), me).wait_recv()
        for cp in first + passed:
            cp.wait_send()
        for cp in mine:
            cp.wait()

    return pl.pallas_call(
        body, name=name, out_shape=[jax.ShapeDtypeStruct((N_DEV,) + s_.shape, s_.dtype) for s_ in shards],
        in_specs=[ANY] * n, out_specs=[ANY] * n,
        scratch_shapes=[pltpu.SemaphoreType.DMA((7 * n,)), pltpu.SemaphoreType.DMA((7 * n,)), pltpu.SemaphoreType.DMA((n,))],
    )(*shards)


HBM = pl.BlockSpec(memory_space=pltpu.HBM)
SEM = pl.BlockSpec(memory_space=pltpu.SEMAPHORE)
SIDE_EFFECT = pltpu.CompilerParams(has_side_effects=pltpu.SideEffectType.DATAFLOW_SIDE_EFFECTING)


def _gather_start(shard):
    r, c_ = shard.shape

    def body(x_ref, land_ref, send_sems, recv_sems, x_thru, land_thru, token):
        x, y, c = lax.axis_index("x"), lax.axis_index("y"), lax.axis_index("c")
        me = 4 * x + 2 * y + c
        for k in range(1, N_DEV):
            peer = (1 - x if k & 4 else x, 1 - y if k & 2 else y, 1 - c if k & 1 else c)
            pltpu.make_async_remote_copy(src_ref=x_ref, dst_ref=land_ref.at[me], send_sem=send_sems.at[k - 1],
                                         recv_sem=recv_sems.at[k - 1], device_id=peer, device_id_type=MESH).start()
        token[...] = jnp.zeros_like(token)

    land = pltpu.with_memory_space_constraint(lax.empty((N_DEV, r, c_), shard.dtype), pltpu.HBM)
    return pl.pallas_call(
        body, name="gather_late_weights_start",
        out_shape=(pltpu.SemaphoreType.DMA((N_DEV - 1,)), pltpu.SemaphoreType.DMA((N_DEV - 1,)),
                   pltpu.HBM(shard.shape, shard.dtype), pltpu.HBM((N_DEV, r, c_), shard.dtype),
                   jax.ShapeDtypeStruct((8, 128), F32)),
        in_specs=(HBM, HBM), out_specs=(SEM, SEM, HBM, HBM, pl.BlockSpec(memory_space=pltpu.VMEM)),
        input_output_aliases={0: 2, 1: 3}, compiler_params=SIDE_EFFECT,
    )(pltpu.with_memory_space_constraint(shard, pltpu.HBM), land)


def _gather_wait(send_sems, recv_sems, shard_thru, land_thru, after):
    def body(x_ref, land_ref, send_sems, recv_sems, after_ref, x_dead, got_ref):
        x, y, c = lax.axis_index("x"), lax.axis_index("y"), lax.axis_index("c")
        for k in range(N_DEV - 1):
            copy = pltpu.make_async_remote_copy(src_ref=x_ref, dst_ref=land_ref.at[0], send_sem=send_sems.at[k],
                                                recv_sem=recv_sems.at[k], device_id=(x, y, c), device_id_type=MESH)
            copy.wait_send()
            copy.wait_recv()

    return pl.pallas_call(
        body, name="gather_late_weights_wait",
        out_shape=(pltpu.HBM(shard_thru.shape, shard_thru.dtype), pltpu.HBM(land_thru.shape, land_thru.dtype)),
        in_specs=(HBM, HBM, SEM, SEM, ANY), out_specs=(HBM, HBM), input_output_aliases={0: 0, 1: 1},
        compiler_params=SIDE_EFFECT,
    )(shard_thru, land_thru, send_sems, recv_sems, after)[1]


def _hbm(a):
    return pltpu.with_memory_space_constraint(a, pltpu.HBM)


def _exchange_start(name, arrays, lands, plan, ncopies):
    n = len(arrays)

    def body(*refs):
        send_sems, recv_sems, token = refs[2 * n], refs[2 * n + 1], refs[-1]
        x, y, c = lax.axis_index("x"), lax.axis_index("y"), lax.axis_index("c")
        for k, (src, dst, peer) in enumerate(plan(x, y, c, refs[:n], refs[n:2 * n])):
            pltpu.make_async_remote_copy(src_ref=src, dst_ref=dst, send_sem=send_sems.at[k], recv_sem=recv_sems.at[k],
                                         device_id=peer, device_id_type=MESH).start()
        token[...] = jnp.zeros_like(token)

    both = list(arrays) + list(lands)
    outs = pl.pallas_call(
        body, name=name,
        out_shape=(pltpu.SemaphoreType.DMA((ncopies,)), pltpu.SemaphoreType.DMA((ncopies,)),
                   *[pltpu.HBM(a.shape, a.dtype) for a in both], jax.ShapeDtypeStruct((8, 128), F32)),
        in_specs=(HBM,) * (2 * n), out_specs=(SEM, SEM) + (HBM,) * (2 * n) + (pl.BlockSpec(memory_space=pltpu.VMEM),),
        input_output_aliases={i: 2 + i for i in range(2 * n)}, compiler_params=SIDE_EFFECT,
    )(*[_hbm(a) for a in both])
    return outs[0], outs[1], list(outs[2:2 + n]), list(outs[2 + n:2 + 2 * n]), outs[-1]


def _exchange_wait(name, send_sems, recv_sems, arrays, lands, plan, after):
    n = len(arrays)

    def body(*refs):
        send_sems, recv_sems = refs[2 * n], refs[2 * n + 1]
        x, y, c = lax.axis_index("x"), lax.axis_index("y"), lax.axis_index("c")
        for k, (src, dst, peer) in enumerate(plan(x, y, c, refs[:n], refs[n:2 * n])):
            copy = pltpu.make_async_remote_copy(src_ref=src, dst_ref=dst, send_sem=send_sems.at[k], recv_sem=recv_sems.at[k],
                                                device_id=peer, device_id_type=MESH)
            copy.wait_send()
            copy.wait_recv()

    both = list(arrays) + list(lands)
    outs = pl.pallas_call(
        body, name=name, out_shape=tuple(pltpu.HBM(a.shape, a.dtype) for a in both),
        in_specs=(HBM,) * (2 * n) + (SEM, SEM, ANY), out_specs=(HBM,) * (2 * n),
        input_output_aliases={i: i for i in range(2 * n)}, compiler_params=SIDE_EFFECT,
    )(*both, send_sems, recv_sems, after)
    return list(outs[:n]), list(outs[n:])


def _sibling_plan(x, y, c, g_refs, land_refs):
    return [(g.at[:, 2 * q + (1 - c)], o.at[:, q], (x, y, 1 - c)) for g, o in zip(g_refs, land_refs) for q in range(4)]


def _chips_plan(x, y, c, p_refs, land_refs):
    chips = [(1 - x, y), (x, 1 - y), (1 - x, 1 - y)]
    return [(p_.at[:, 2 * qx + qy], o.at[:, 2 * x + y], (qx, qy, c)) for p_, o in zip(p_refs, land_refs) for qx, qy in chips]


class _GradExchange:
    def __init__(self):
        self.core = lax.axis_index("c").astype(jnp.int32).reshape(1)
        self.chip = 2 * lax.axis_index("x") + lax.axis_index("y")
        self.groups = []

    def submit(self, tag, arrays, dtypes):
        arrays = [a.reshape(a.shape[0], N_DEV, a.shape[1] // N_DEV, a.shape[2]) for a in arrays]
        lands = [lax.empty((a.shape[0], 4) + a.shape[2:], a.dtype) for a in arrays]
        send, recv, arrays, lands, token = _exchange_start(f"rs_pair_start_{tag}", arrays, lands, _sibling_plan, 4 * len(arrays))
        self.groups.append(dict(tag=tag, stage=1, sems=(send, recv), arrays=arrays, lands=lands, dtypes=dtypes))
        return token

    def advance(self, after):
        token = None
        for g in self.groups:
            if g["stage"] == 1:
                arrays, got = _exchange_wait(f"rs_pair_wait_{g['tag']}", *g["sems"], g["arrays"], g["lands"], _sibling_plan, after)
                parts = [_rs_pair_add(a, b, self.core, dt) for a, b, dt in zip(arrays, got, g["dtypes"])]
                lands = [lax.empty(p_.shape, p_.dtype) for p_ in parts]
                send, recv, parts, lands, tok = _exchange_start(f"rs_chip_start_{g['tag']}", parts, lands, _chips_plan, 3 * len(parts))
                g.update(stage=2, sems=(send, recv), arrays=parts, lands=lands)
                token = tok if token is None else token + tok
            elif g["stage"] == 2:
                parts, lands = _exchange_wait(f"rs_chip_wait_{g['tag']}", *g["sems"], g["arrays"], g["lands"], _chips_plan, after)
                sums = []
                for p_, land in zip(parts, lands):
                    l, _, r, c_ = p_.shape
                    own = lax.dynamic_slice(p_, (0, self.chip, 0, 0), (l, 1, r, c_))
                    sums.append(_sum_parts(lax.dynamic_update_slice(land, own, (0, self.chip, 0, 0)), "sum_chips"))
                g.update(stage=3, sums=sums)
        return token

    def results(self):
        return [g.get("sums") for g in self.groups]


def _with_rows(g, n):
    return jax.ShapeDtypeStruct((g.shape[0], n) + tuple(g.shape[2:]), g.dtype)


def _rs_sibling(gs):
    n = len(gs)

    def body(*refs):
        g_refs, o_refs, (send_sems, recv_sems) = refs[:n], refs[n:2 * n], refs[2 * n:]
        x, y, c = lax.axis_index("x"), lax.axis_index("y"), lax.axis_index("c")
        copies = [pltpu.make_async_remote_copy(
            src_ref=g_refs[w].at[:, 2 * q + (1 - c)], dst_ref=o_refs[w].at[:, q], send_sem=send_sems.at[4 * w + q],
            recv_sem=recv_sems.at[4 * w + q], device_id=(x, y, 1 - c), device_id_type=MESH)
            for w in range(n) for q in range(4)]
        for cp in copies:
            cp.start()
        for cp in copies:
            cp.wait_recv()
        for cp in copies:
            cp.wait_send()

    return pl.pallas_call(
        body, name="rs_sibling", out_shape=[_with_rows(g, 4) for g in gs],
        in_specs=[ANY] * n, out_specs=[ANY] * n,
        scratch_shapes=[pltpu.SemaphoreType.DMA((4 * n,)), pltpu.SemaphoreType.DMA((4 * n,))],
    )(*gs)


def _rs_pair_add(g, got, core, out_dtype):
    l, _, r, c_ = g.shape

    def body(core_ref, g_ref, got_ref, o_ref):
        o_ref[...] = (g_ref[...].astype(F32) + got_ref[...].astype(F32)).astype(out_dtype)

    blk = (None, None, r, c_)
    return pl.pallas_call(
        body, name="rs_pair_add", out_shape=jax.ShapeDtypeStruct((l, 4, r, c_), out_dtype),
        grid_spec=pltpu.PrefetchScalarGridSpec(
            num_scalar_prefetch=1, grid=(l, 4),
            in_specs=[pl.BlockSpec(blk, lambda i, q, core_ref: (i, 2 * q + core_ref[0], 0, 0)),
                      pl.BlockSpec(blk, lambda i, q, core_ref: (i, q, 0, 0))],
            out_specs=pl.BlockSpec(blk, lambda i, q, core_ref: (i, q, 0, 0))),
        compiler_params=_params("parallel", "parallel"),
    )(core, g, got)


def _rs_chips(parts):
    n = len(parts)

    def body(*refs):
        p_refs, o_refs, (send_sems, recv_sems, local_sems) = refs[:n], refs[n:2 * n], refs[2 * n:]
        x, y, c = lax.axis_index("x"), lax.axis_index("y"), lax.axis_index("c")
        my_chip = 2 * x + y
        chips = [(1 - x, y), (x, 1 - y), (1 - x, 1 - y)]
        local = [pltpu.make_async_copy(p_refs[w].at[:, my_chip], o_refs[w].at[:, my_chip], local_sems.at[w]) for w in range(n)]
        for cp in local:
            cp.start()
        copies = [pltpu.make_async_remote_copy(
            src_ref=p_refs[w].at[:, 2 * qx + qy], dst_ref=o_refs[w].at[:, my_chip], send_sem=send_sems.at[3 * w + k],
            recv_sem=recv_sems.at[3 * w + k], device_id=(qx, qy, c), device_id_type=MESH)
            for w in range(n) for k, (qx, qy) in enumerate(chips)]
        for cp in copies:
            cp.start()
        for cp in copies:
            cp.wait_recv()
        for cp in copies:
            cp.wait_send()
        for cp in local:
            cp.wait()

    return pl.pallas_call(
        body, name="rs_chips", out_shape=[jax.ShapeDtypeStruct(p.shape, p.dtype) for p in parts],
        in_specs=[ANY] * n, out_specs=[ANY] * n,
        scratch_shapes=[pltpu.SemaphoreType.DMA((3 * n,)), pltpu.SemaphoreType.DMA((3 * n,)), pltpu.SemaphoreType.DMA((n,))],
    )(*parts)


def _sum_parts(parts, name):
    l, n, r, c_ = parts.shape

    def body(p_ref, o_ref):
        g = p_ref[0].astype(F32)
        for s in range(1, n):
            g = g + p_ref[s].astype(F32)
        o_ref[...] = g

    return pl.pallas_call(
        body, name=name, out_shape=jax.ShapeDtypeStruct((l, r, c_), F32), grid=(l,),
        in_specs=[pl.BlockSpec((None, n, r, c_), lambda i: (i, 0, 0, 0))],
        out_specs=pl.BlockSpec((None, r, c_), lambda i: (i, 0, 0)), compiler_params=_params("parallel"),
    )(parts)


def _adamw_math(w, g, m, v):
    m = ADAM_B1 * m + (1.0 - ADAM_B1) * g
    v = ADAM_B2 * v + (1.0 - ADAM_B2) * (g * g)
    m_hat = m / (1.0 - ADAM_B1 ** ADAM_STEP)
    v_hat = v / (1.0 - ADAM_B2 ** ADAM_STEP)
    delta = -ADAM_LR * (m_hat / (jnp.sqrt(v_hat) + ADAM_EPS) + ADAM_WD * w)
    return delta, m, v


def _adamw(g, w, m, v, name):
    l, k, n = w.shape
    tk = 256 if k % 256 == 0 else k

    def body(g_ref, w_ref, m_ref, v_ref, d_ref, nm_ref, nv_ref):
        d_ref[...], nm_ref[...], nv_ref[...] = _adamw_math(w_ref[...], g_ref[...], m_ref[...], v_ref[...])

    spec = pl.BlockSpec((None, tk, n), lambda i, j: (i, j, 0))
    return pl.pallas_call(
        body, name=name, out_shape=[jax.ShapeDtypeStruct((l, k, n), F32)] * 3, grid=(l, k // tk),
        in_specs=[spec] * 4, out_specs=[spec] * 3, compiler_params=_params("parallel", "parallel"),
    )(g, w, m, v)


def _sum_adamw(parts, w, m, v, name):
    n, r, c_ = parts.shape

    def body(p_ref, w_ref, m_ref, v_ref, g_ref, d_ref, nm_ref, nv_ref):
        g = p_ref[0]
        for s in range(1, n):
            g = g + p_ref[s]
        g_ref[...] = g
        d_ref[...], nm_ref[...], nv_ref[...] = _adamw_math(w_ref[...], g, m_ref[...], v_ref[...])

    return pl.pallas_call(
        body, name=name, out_shape=[jax.ShapeDtypeStruct((r, c_), F32)] * 4, grid=(1,),
        in_specs=[_full((n, r, c_))] + [_full((r, c_))] * 3, out_specs=[_full((r, c_))] * 4,
        compiler_params=_params("arbitrary"),
    )(parts, w, m, v)


def _rope_tables(pos_col, freq_row):
    s = pos_col.shape[0]
    tm = min(1024, s)

    def body(p_ref, f_ref, o_ref):
        ang = p_ref[...].astype(F32) * f_ref[...]
        lane = lax.broadcasted_iota(jnp.int32, ang.shape, 1) & (HEAD_DIM - 1)
        cs, sn = jnp.cos(ang), jnp.sin(ang)
        o_ref[:, 0:128] = jnp.where(lane < ROT_DIM, cs, 1.0)
        o_ref[:, 128:256] = jnp.where((lane >= ROT_DIM // 2) & (lane < ROT_DIM), sn, 0.0)
        o_ref[:, 256:384] = jnp.where(lane < ROT_DIM // 2, -sn, 0.0)

    return pl.pallas_call(
        body, name="rope_tables", out_shape=jax.ShapeDtypeStruct((s, ROPE_COLS), F32), grid=(s // tm,),
        in_specs=[pl.BlockSpec((tm, 1), lambda i: (i, 0)), _full((1, 128))],
        out_specs=_rows(tm, ROPE_COLS), compiler_params=_params("parallel"),
    )(pos_col, freq_row)


ROPE_COLS = 3 * 128


def _rope_parts(tab, reps=1):
    return [jnp.tile(tab[:, k * 128:(k + 1) * 128], (1, reps)) if reps > 1 else tab[:, k * 128:(k + 1) * 128] for k in range(3)]


def _rope_apply(t, tab):
    w = t.shape[1]
    cos, sin_up, sin_dn = _rope_parts(tab, w // 128)
    return t * cos + pltpu.roll(t, 8, 1) * sin_up + pltpu.roll(t, w - 8, 1) * sin_dn


def _rope_transpose(dr, tab):
    w = dr.shape[1]
    cos, sin_up, sin_dn = _rope_parts(tab, w // 128)
    return dr * cos + pltpu.roll(dr * sin_up, w - 8, 1) + pltpu.roll(dr * sin_dn, 8, 1)


def _norm_matmul(x, g, wt, *, tn, name, bias=None, tm=1024):
    s, d = x.shape
    n = wt.shape[0]
    tm = min(tm, s)

    def body(*refs):
        x_ref, g_ref, w_ref = refs[:3]
        b_ref = refs[3] if bias is not None else None
        h_ref, o_ref = refs[-2:]

        @pl.when(pl.program_id(1) == 0)
        def _():
            xv = x_ref[...]
            r = lax.rsqrt(jnp.mean(xv * xv, axis=-1, keepdims=True) + EPS)
            h_ref[...] = (xv * r * g_ref[...]).astype(BF16)

        acc = lax.dot_general(h_ref[...], w_ref[...], NT, preferred_element_type=F32)
        if b_ref is not None:
            acc = acc + b_ref[...]
        o_ref[...] = acc.astype(BF16)

    in_specs = [_rows(tm, d), _full((1, d)), pl.BlockSpec((tn, d), lambda i, j: (j, 0))]
    args = [x, g, wt]
    if bias is not None:
        in_specs.append(pl.BlockSpec((1, tn), lambda i, j: (0, j)))
        args.append(bias)
    return pl.pallas_call(
        body, name=name,
        out_shape=[jax.ShapeDtypeStruct((s, d), BF16), jax.ShapeDtypeStruct((s, n), BF16)],
        grid=(s // tm, n // tn), in_specs=in_specs,
        out_specs=[_rows(tm, d), pl.BlockSpec((tm, tn), lambda i, j: (i, j))],
        compiler_params=_params("parallel", "arbitrary"),
    )(*args)


def _class_major(tm, dil):
    p = np.zeros((tm, tm), np.float32)
    per = tm // dil
    for r in range(dil):
        for j in range(per):
            p[r * per + j, j * dil + r] = 1.0
    return jnp.asarray(p, dtype=BF16)


def _qkv_proj(x, g, wt, rope, tm=512):
    s, d = x.shape
    n = wt.shape[0]
    gw3 = 3 * GROUP_WIDTH
    tm = min(tm, s)
    assert n == 3 * gw3

    def body(x_ref, g_ref, w_ref, tab_ref, p1_ref, p2_ref, h_ref, o0_ref, o1_ref, o2_ref):
        j = pl.program_id(1)

        @pl.when(j == 0)
        def _():
            xv = x_ref[...]
            r = lax.rsqrt(jnp.mean(xv * xv, axis=-1, keepdims=True) + EPS)
            h_ref[...] = (xv * r * g_ref[...]).astype(BF16)

        acc = lax.dot_general(h_ref[...], w_ref[...], NT, preferred_element_type=F32)

        def store(y):
            yb = y.astype(BF16)
            o0_ref[:, pl.ds(pl.multiple_of(j * GROUP_WIDTH, GROUP_WIDTH), GROUP_WIDTH)] = yb[:, :GROUP_WIDTH]
            for grp, o_ref, p_ref in ((1, o1_ref, p1_ref), (2, o2_ref, p2_ref)):
                dil = DILATIONS[grp]
                per = tm // dil
                yp = jnp.dot(p_ref[...], yb[:, grp * GROUP_WIDTH:(grp + 1) * GROUP_WIDTH],
                             preferred_element_type=F32).astype(BF16)
                for r in range(dil):
                    col = pl.multiple_of(r * gw3 + j * GROUP_WIDTH, GROUP_WIDTH)
                    o_ref[:, pl.ds(col, GROUP_WIDTH)] = yp[r * per:(r + 1) * per, :]

        @pl.when(j < 2)
        def _():
            store(_rope_apply(acc, tab_ref[...]))

        @pl.when(j == 2)
        def _():
            store(acc)

    outs = [jax.ShapeDtypeStruct((s, d), BF16)] + [jax.ShapeDtypeStruct((s // dl, dl * gw3), BF16) for dl in DILATIONS]
    out_specs = [_rows(tm, d)] + [_rows(tm // dl, dl * gw3) for dl in DILATIONS]
    return pl.pallas_call(
        body, name="attn_qkv", out_shape=outs, grid=(s // tm, 3),
        in_specs=[_rows(tm, d), _full((1, d)), pl.BlockSpec((gw3, d), lambda i, j: (j, 0)), _rows(tm, ROPE_COLS)]
        + [_full((tm, tm))] * 2,
        out_specs=out_specs, compiler_params=_params("parallel", "arbitrary"),
    )(x, g, wt, rope, _class_major(tm, DILATIONS[1]), _class_major(tm, DILATIONS[2]))


def _head_masks(rows=SPAN):
    lane = lax.broadcasted_iota(jnp.int32, (rows, 128), 1)
    masks = [lane < HEAD_DIM, lane >= HEAD_DIM]
    lane1 = lax.broadcasted_iota(jnp.int32, (1, 128), 1)
    keep = [jnp.where(lane1 < HEAD_DIM, 1.0, 0.0).astype(BF16), jnp.where(lane1 >= HEAD_DIM, 1.0, 0.0).astype(BF16)]
    return masks, keep


def _band_mask(b):
    row = lax.broadcasted_iota(jnp.int32, (2 * SPAN, 2 * SPAN), 0) & (SPAN - 1)
    col = lax.broadcasted_iota(jnp.int32, (2 * SPAN, 2 * SPAN), 1)
    no_prev = jnp.where(b > 0, 0, 4 * SPAN)
    return ((col < SPAN) & (col >= row + no_prev)) | ((col >= SPAN) & (col - SPAN <= row))


def _attn_fwd(qv, grp, dil):
    l = qv.shape[0]
    s = l * dil
    nb = l // SPAN
    nq = next(n for n in (8, 4, 2, 1) if nb % n == 0)

    def body(q_ref, kp_ref, kc_ref, vp_ref, vc_ref, o_ref, l_ref):
        b = pl.program_id(1)
        masks, keep = _head_masks()
        for qb in range(nq):
            valid = _band_mask(b * nq + qb)
            rows = slice(qb * SPAN, (qb + 1) * SPAN)
            before = slice((qb - 1) * SPAN, qb * SPAN)
            for p in range(GROUP_WIDTH // 128):
                sl = slice(p * 128, (p + 1) * 128)
                qp = q_ref[rows, sl]
                kk = jnp.concatenate([kp_ref[:, sl] if qb == 0 else kc_ref[before, sl], kc_ref[rows, sl]], axis=0)
                vv = jnp.concatenate([vp_ref[:, sl] if qb == 0 else vc_ref[before, sl], vc_ref[rows, sl]], axis=0)
                q2 = jnp.concatenate([qp * keep[0], qp * keep[1]], axis=0)
                sc = lax.dot_general(q2, kk, NT, preferred_element_type=F32) * (HEAD_DIM ** -0.5)
                sc = jnp.where(valid, sc, -1e30)
                mx = jnp.max(sc, axis=-1, keepdims=True)
                pe = jnp.exp(sc - mx)
                den = jnp.sum(pe, axis=-1, keepdims=True)
                out = jnp.dot(pe.astype(BF16), vv, preferred_element_type=F32) / den
                lse = jnp.broadcast_to(mx + jnp.log(den), (2 * SPAN, 128))
                o_ref[rows, sl] = jnp.where(masks[0], out[:SPAN], out[SPAN:]).astype(BF16)
                l_ref[rows, sl] = jnp.where(masks[0], lse[:SPAN], lse[SPAN:])

    blk = (nq * SPAN, GROUP_WIDTH)
    cur = lambda t: pl.BlockSpec(blk, lambda r, b: (b, r * 3 + t))
    prev = lambda t: pl.BlockSpec((SPAN, GROUP_WIDTH), lambda r, b: (jnp.maximum(nq * b - 1, 0), r * 3 + t))
    out = pl.BlockSpec(blk, lambda r, b: (b, r))
    o, lse = pl.pallas_call(
        body, name=f"attn_fwd_g{grp}",
        out_shape=[jax.ShapeDtypeStruct((l, dil * GROUP_WIDTH), BF16), jax.ShapeDtypeStruct((l, dil * GROUP_WIDTH), F32)],
        grid=(dil, nb // nq), in_specs=[cur(0), prev(1), cur(1), prev(2), cur(2)], out_specs=[out, out],
        compiler_params=_params("parallel", "arbitrary"),
    )(qv, qv, qv, qv, qv)
    return o.reshape(s, GROUP_WIDTH), lse.reshape(s, GROUP_WIDTH)


def _resnorm_store(y, x_ref, g_ref, y_ref, xo_ref):
    r = lax.rsqrt(jnp.mean(y * y, axis=-1, keepdims=True) + EPS)
    y_ref[...] = y
    xo_ref[...] = x_ref[...] + y * r * g_ref[...]


def _mix_wo(os_, ls_, wot, x, g, tm=512):
    s, d = x.shape
    gw = wot.shape[1]
    tm = min(tm, s)

    def body(o0, o1, o2, l0, l1, l2, w_ref, x_ref, g_ref, y_ref, xo_ref, mixed_ref, lse_ref):
        a0, a1, a2 = l0[...], l1[...], l2[...]
        mx = jnp.maximum(jnp.maximum(a0, a1), a2)
        e0, e1, e2 = jnp.exp(a0 - mx), jnp.exp(a1 - mx), jnp.exp(a2 - mx)
        den = e0 + e1 + e2
        mixed = (e0 / den) * o0[...].astype(F32) + (e1 / den) * o1[...].astype(F32) + (e2 / den) * o2[...].astype(F32)
        mixed_ref[...] = mixed.astype(BF16)
        lse_ref[...] = mx + jnp.log(den)
        y = lax.dot_general(mixed.astype(BF16), w_ref[...], NT, preferred_element_type=F32)
        _resnorm_store(y, x_ref, g_ref, y_ref, xo_ref)

    return pl.pallas_call(
        body, name="mix_wo",
        out_shape=[jax.ShapeDtypeStruct((s, d), F32), jax.ShapeDtypeStruct((s, d), F32),
                   jax.ShapeDtypeStruct((s, gw), BF16), jax.ShapeDtypeStruct((s, gw), F32)],
        grid=(s // tm,), in_specs=[_rows(tm, gw)] * 6 + [_full((d, gw)), _rows(tm, d), _full((1, d))],
        out_specs=[_rows(tm, d), _rows(tm, d), _rows(tm, gw), _rows(tm, gw)],
        compiler_params=_params("parallel"),
    )(*os_, *ls_, wot, x, g)


def _matmul_resnorm(a, w, x, g, *, name, bias=None, tm=512):
    s, k = a.shape
    d = w.shape[1]
    tm = min(tm, s)

    def body(*refs):
        a_ref, w_ref = refs[:2]
        b_ref = refs[2] if bias is not None else None
        x_ref, g_ref, y_ref, xo_ref = refs[-4:]
        y = jnp.dot(a_ref[...], w_ref[...], preferred_element_type=F32)
        if b_ref is not None:
            y = y + b_ref[...]
        _resnorm_store(y, x_ref, g_ref, y_ref, xo_ref)

    in_specs = [_rows(tm, k), _full((k, d))] + ([_full((1, d))] if bias is not None else []) + [_rows(tm, d), _full((1, d))]
    args = [a, w] + ([bias] if bias is not None else []) + [x, g]
    return pl.pallas_call(
        body, name=name, out_shape=[jax.ShapeDtypeStruct((s, d), F32)] * 2, grid=(s // tm,),
        in_specs=in_specs, out_specs=[_rows(tm, d)] * 2, compiler_params=_params("parallel"),
    )(*args)


FFN_SUB = 256


def _conv3_rows(z_ref, halo_ref, rb, sub, cs, first):
    zc = z_ref[rb * sub:(rb + 1) * sub, cs].astype(F32)
    if rb == 0:
        halo = halo_ref[:, cs].astype(F32) * jnp.where(first, 0.0, 1.0)
    else:
        halo = z_ref[rb * sub - 16:rb * sub, cs].astype(F32)[8:]
    z2, z1 = _conv3_taps(zc, halo)
    return z2, z1, zc


def _conv3_taps(z, halo):
    row = lax.broadcasted_iota(jnp.int32, (8, z.shape[1]), 0)
    h6, h7 = halo[6:7, :], halo[7:8, :]
    r1, r2 = pltpu.roll(z, 1, 0), pltpu.roll(z, 2, 0)
    z1 = jnp.concatenate([jnp.where(row == 0, h7, r1[0:8]), r1[8:]], axis=0)
    z2 = jnp.concatenate([jnp.where(row == 0, h6, jnp.where(row == 1, h7, r2[0:8])), r2[8:]], axis=0)
    return z2, z1


def _ffn_cols(f):
    return _tile(f)


def _lane_chunks(width, fn):
    def step(k, carry):
        fn(pl.ds(pl.multiple_of(k * 128, 128), 128))
        return carry

    lax.fori_loop(0, width // 128, step, 0)


def _ffn_act(z, w_dw, b_dw, tm=1024):
    s, f2 = z.shape
    f = f2 // 2
    tm = min(tm, s)
    sub = min(FFN_SUB, tm)
    tc = _ffn_cols(f)
    nfc = f // tc

    def body(zu, zg, hu, hg, wu, wg, bu, bg, o_ref):
        first = pl.program_id(0) == 0

        def chunk(cs):
            for rb in range(tm // sub):
                def conv(z_ref, h_ref, w_ref, b_ref):
                    z2, z1, zc = _conv3_rows(z_ref, h_ref, rb, sub, cs, first)
                    return w_ref[0:1, cs] * z2 + w_ref[1:2, cs] * z1 + w_ref[2:3, cs] * zc + b_ref[:, cs]

                up, gate = conv(zu, hu, wu, bu), conv(zg, hg, wg, bg)
                o_ref[rb * sub:(rb + 1) * sub, cs] = (gate * _sigmoid(gate) * up).astype(BF16)

        _lane_chunks(tc, chunk)

    hb = tm // 8
    tile = lambda off: pl.BlockSpec((tm, tc), lambda i, j: (i, off + j))
    halo = lambda off: pl.BlockSpec((8, tc), lambda i, j: (jnp.maximum(i * hb - 1, 0), off + j))
    prm = lambda rows, off: pl.BlockSpec((rows, tc), lambda i, j: (0, off + j))
    return pl.pallas_call(
        body, name="ffn_act", out_shape=jax.ShapeDtypeStruct((s, f), BF16), grid=(s // tm, nfc),
        in_specs=[tile(0), tile(nfc), halo(0), halo(nfc), prm(FFN_CONV, 0), prm(FFN_CONV, nfc), prm(1, 0), prm(1, nfc)],
        out_specs=pl.BlockSpec((tm, tc), lambda i, j: (i, j)), compiler_params=_params("parallel", "parallel"),
    )(z, z, z, z, w_dw, w_dw, b_dw, b_dw)


def _shifted_planes(ext_ref):
    rows = ext_ref.shape[1]
    for s in range(1, 8):
        ext_ref[s, 0:rows - 8, :] = ext_ref[0, s:s + rows - 8, :]


def _window(ext_ref, off, tm, cs):
    s = off % 8
    return ext_ref[s, off - s:off - s + tm, cs]


def _conv_taps(ext_ref, w_ref, offs, tm, out_ref):
    def chunk(cs):
        acc = w_ref[0:1, cs] * _window(ext_ref, offs[0], tm, cs)
        for j in range(1, len(offs)):
            acc = acc + w_ref[j:j + 1, cs] * _window(ext_ref, offs[j], tm, cs)
        out_ref[:, cs] = acc

    _lane_chunks(out_ref.shape[1], chunk)


def _glu_planes(ag_ref, halo_ref, ext_ref, first, c):
    hal = halo_ref[...].astype(F32)
    ext_ref[0, 0:CONV_HALO, :] = hal[:, :c] * _sigmoid(hal[:, c:]) * jnp.where(first, 0.0, 1.0)
    ag = ag_ref[...].astype(F32)
    ext_ref[0, CONV_HALO:, :] = ag[:, :c] * _sigmoid(ag[:, c:])
    _shifted_planes(ext_ref)


def _layernorm_stats(u1):
    mu = jnp.mean(u1, axis=-1, keepdims=True)
    cen = u1 - mu
    rstd = lax.rsqrt(jnp.mean(cen * cen, axis=-1, keepdims=True) + EPS)
    return cen * rstd, rstd


def _conv_mid(ag, w_dw, b_dw, ln_g, ln_b, tm=512):
    s, c2 = ag.shape
    c = c2 // 2
    tm = min(tm, s)

    def body(ag_ref, halo_ref, w_ref, b_ref, g_ref, bb_ref, o_ref, u1_ref, ext_ref):
        _glu_planes(ag_ref, halo_ref, ext_ref, pl.program_id(0) == 0, c)
        base = CONV_HALO - (CONV_KERNEL - 1)
        _conv_taps(ext_ref, w_ref, [base + j for j in range(CONV_KERNEL)], tm, u1_ref)
        xh, _ = _layernorm_stats(u1_ref[...] + b_ref[...])
        u2 = xh * g_ref[...] + bb_ref[...]
        o_ref[...] = (u2 * _sigmoid(u2)).astype(BF16)

    hb = tm // CONV_HALO
    return pl.pallas_call(
        body, name="conv_mid", out_shape=[jax.ShapeDtypeStruct((s, c), BF16), jax.ShapeDtypeStruct((s, c), F32)], grid=(s // tm,),
        in_specs=[_rows(tm, c2), pl.BlockSpec((CONV_HALO, c2), lambda i: (jnp.maximum(i * hb - 1, 0), 0)),
                  _full((CONV_KERNEL, c)), _full((1, c)), _full((1, c)), _full((1, c))],
        out_specs=[_rows(tm, c), _rows(tm, c)], scratch_shapes=[pltpu.VMEM((8, CONV_HALO + tm, c), F32)],
        compiler_params=_params("arbitrary"),
    )(ag, ag, w_dw, b_dw, ln_g, ln_b)


def _loss_grad(xo, target, tm=1024):
    s, d = xo.shape
    tm = min(tm, s)

    def body(x_ref, t_ref, dx_ref, loss_ref):
        @pl.when(pl.program_id(0) == 0)
        def _():
            loss_ref[...] = jnp.zeros_like(loss_ref)

        err = x_ref[...] - t_ref[...]
        dx_ref[...] = err * (1.0 / d)
        loss_ref[...] += 0.5 * jnp.sum(jnp.mean(err * err, axis=-1, keepdims=True))

    return pl.pallas_call(
        body, name="loss_grad", out_shape=[jax.ShapeDtypeStruct((s, d), F32), jax.ShapeDtypeStruct((1, 128), F32)],
        grid=(s // tm,), in_specs=[_rows(tm, d)] * 2, out_specs=[_rows(tm, d), _full((1, 128))],
        compiler_params=_params("arbitrary"),
    )(xo, target)


def _postnorm_bwd(y, g, dxo, *, name, with_bias_grad=False, tm=1024):
    s, d = y.shape
    tm = min(tm, s)

    def body(y_ref, g_ref, dx_ref, dy_ref, dg_ref, *rest):
        @pl.when(pl.program_id(0) == 0)
        def _():
            dg_ref[...] = jnp.zeros_like(dg_ref)
            for r_ in rest:
                r_[...] = jnp.zeros_like(r_)

        yv, dxo_v = y_ref[...], dx_ref[...]
        r = lax.rsqrt(jnp.mean(yv * yv, axis=-1, keepdims=True) + EPS)
        yh = yv * r
        dyh = dxo_v * g_ref[...]
        dy = r * (dyh - yh * jnp.mean(dyh * yh, axis=-1, keepdims=True))
        dy_ref[...] = dy.astype(BF16)
        dg_ref[...] += jnp.sum(dxo_v * yh, axis=0, keepdims=True)
        for r_ in rest:
            r_[...] += jnp.sum(dy, axis=0, keepdims=True)

    nacc = 2 if with_bias_grad else 1
    return pl.pallas_call(
        body, name=name, out_shape=[jax.ShapeDtypeStruct((s, d), BF16)] + [jax.ShapeDtypeStruct((1, d), F32)] * nacc,
        grid=(s // tm,), in_specs=[_rows(tm, d), _full((1, d)), _rows(tm, d)],
        out_specs=[_rows(tm, d)] + [_full((1, d))] * nacc, compiler_params=_params("arbitrary"),
    )(y, g, dxo)


def _matmul(gmat, w, *, name, out_dtype, transposed_w, tm=512):
    s, k = gmat.shape
    n = w.shape[0] if transposed_w else w.shape[1]
    tm = min(tm, s)

    def body(g_ref, w_ref, o_ref):
        if transposed_w:
            acc = lax.dot_general(g_ref[...], w_ref[...], NT, preferred_element_type=F32)
        else:
            acc = jnp.dot(g_ref[...], w_ref[...], preferred_element_type=F32)
        o_ref[...] = acc.astype(out_dtype)

    return pl.pallas_call(
        body, name=name, out_shape=jax.ShapeDtypeStruct((s, n), out_dtype), grid=(s // tm,),
        in_specs=[_rows(tm, k), _full(w.shape)], out_specs=_rows(tm, n), compiler_params=_params("parallel"),
    )(gmat, w)


def _matmul_prenorm_bwd(pieces, wt, x, g, dres, *, name, tm=256):
    s, d = x.shape
    tm = min(tm, s)
    np_ = len(pieces)

    def body(*refs):
        p_refs, w_refs = refs[:np_], refs[np_:2 * np_]
        x_ref, g_ref, r_ref, dx_ref, dg_ref = refs[2 * np_:]

        @pl.when(pl.program_id(0) == 0)
        def _():
            dg_ref[...] = jnp.zeros_like(dg_ref)

        dh = None
        for p_ref, w_ref in zip(p_refs, w_refs):
            t = jnp.dot(p_ref[...], w_ref[...], preferred_element_type=F32)
            dh = t if dh is None else dh + t
        xv = x_ref[...]
        r = lax.rsqrt(jnp.mean(xv * xv, axis=-1, keepdims=True) + EPS)
        xh = xv * r
        dyh = dh * g_ref[...]
        dx_ref[...] = r_ref[...] + r * (dyh - xh * jnp.mean(dyh * xh, axis=-1, keepdims=True))
        dg_ref[...] += jnp.sum(dh * xh, axis=0, keepdims=True)

    in_specs = []
    for _, c0, kc, _ in pieces:
        assert c0 % kc == 0
        in_specs.append(pl.BlockSpec((tm, kc), lambda i, _b=c0 // kc: (i, _b)))
    for _, _, kc, r0 in pieces:
        assert r0 % kc == 0
        in_specs.append(pl.BlockSpec((kc, d), lambda i, _b=r0 // kc: (_b, 0)))
    in_specs += [_rows(tm, d), _full((1, d)), _rows(tm, d)]
    return pl.pallas_call(
        body, name=name, out_shape=[jax.ShapeDtypeStruct((s, d), F32), jax.ShapeDtypeStruct((1, d), F32)],
        grid=(s // tm,), in_specs=in_specs, out_specs=[_rows(tm, d), _full((1, d))],
        compiler_params=_params("arbitrary"),
    )(*[p[0] for p in pieces], *[wt] * np_, x, g, dres)


def _weight_grad(a, gmat, *, name, a_col0=0, ka=None, out=None, out_shape=None, layer=0, row0=0, ts=2048):
    s = a.shape[0]
    ka = a.shape[1] if ka is None else ka
    n = gmat.shape[1]
    ts = min(ts, s)
    tka = _tile(ka, a_col0, row0)
    shape = out.shape if out is not None else out_shape
    nsteps = s // ts

    def body(a_ref, g_ref, *rest):
        o_ref, acc_ref = rest[-2:]
        i = pl.program_id(1)

        @pl.when(i == 0)
        def _():
            acc_ref[...] = jnp.zeros_like(acc_ref)

        acc_ref[...] += lax.dot_general(a_ref[...], g_ref[...], TN, preferred_element_type=F32)

        @pl.when(i == nsteps - 1)
        def _():
            o_ref[...] = acc_ref[...].astype(BF16)

    in_specs = [pl.BlockSpec((ts, tka), lambda k, i: (i, a_col0 // tka + k)), pl.BlockSpec((ts, n), lambda k, i: (i, 0))]
    args = [a, gmat]
    aliases = {}
    if out is not None:
        in_specs.append(ANY)
        args.append(out)
        aliases = {2: 0}
    return pl.pallas_call(
        body, name=name, out_shape=jax.ShapeDtypeStruct(shape, BF16), grid=(ka // tka, nsteps), in_specs=in_specs,
        out_specs=pl.BlockSpec((None, tka, n), lambda k, i: (layer, row0 // tka + k, 0)),
        scratch_shapes=[pltpu.VMEM((tka, n), F32)],
        input_output_aliases=aliases, compiler_params=_params("parallel", "arbitrary"),
    )(*args)


def _ffn_act_bwd(z, dact, w_dw, b_dw, tm=512):
    s, f2 = z.shape
    f = f2 // 2
    tm = min(tm, s)
    sub = min(FFN_SUB // 2, tm)
    tc = _ffn_cols(f)
    nfc = f // tc

    def body(zu, zg, hu, hg, wu, wg, bu, bg, da_ref, du_ref, dgt_ref, dbu_ref, dbg_ref, dwu_ref, dwg_ref):
        i = pl.program_id(1)

        @pl.when(i == 0)
        def _():
            for r_ in (dbu_ref, dbg_ref, dwu_ref, dwg_ref):
                r_[...] = jnp.zeros_like(r_)

        def chunk(cs):
            for rb in range(tm // sub):
                rows = slice(rb * sub, (rb + 1) * sub)

                def conv(z_ref, h_ref, w_ref, b_ref):
                    taps = _conv3_rows(z_ref, h_ref, rb, sub, cs, i == 0)
                    return taps, w_ref[0:1, cs] * taps[0] + w_ref[1:2, cs] * taps[1] + w_ref[2:3, cs] * taps[2] + b_ref[:, cs]

                taps_u, up = conv(zu, hu, wu, bu)
                taps_g, gate = conv(zg, hg, wg, bg)
                da = da_ref[rows, cs].astype(F32)
                sg = _sigmoid(gate)
                d_up = da * (gate * sg)
                d_gate = da * up * (sg * (1.0 + gate * (1.0 - sg)))
                du_ref[rows, cs] = d_up.astype(BF16)
                dgt_ref[rows, cs] = d_gate.astype(BF16)
                for dv, taps, db_ref, dw_ref in ((d_up, taps_u, dbu_ref, dwu_ref), (d_gate, taps_g, dbg_ref, dwg_ref)):
                    db_ref[:, cs] += jnp.sum(dv, axis=0, keepdims=True)
                    for k_, tap in enumerate(taps):
                        dw_ref[k_:k_ + 1, cs] += jnp.sum(dv * tap, axis=0, keepdims=True)

        _lane_chunks(tc, chunk)

    hb = tm // 8
    tile = lambda off: pl.BlockSpec((tm, tc), lambda j, i: (i, off + j))
    halo = lambda off: pl.BlockSpec((8, tc), lambda j, i: (jnp.maximum(i * hb - 1, 0), off + j))
    prm = lambda rows, off: pl.BlockSpec((rows, tc), lambda j, i: (0, off + j))
    acc = lambda rows: pl.BlockSpec((rows, tc), lambda j, i: (0, j))
    return pl.pallas_call(
        body, name="ffn_act_bwd",
        out_shape=[jax.ShapeDtypeStruct((s, f), BF16)] * 2 + [jax.ShapeDtypeStruct((1, f), F32)] * 2
        + [jax.ShapeDtypeStruct((FFN_CONV, f), F32)] * 2,
        grid=(nfc, s // tm),
        in_specs=[tile(0), tile(nfc), halo(0), halo(nfc), prm(FFN_CONV, 0), prm(FFN_CONV, nfc), prm(1, 0), prm(1, nfc), tile(0)],
        out_specs=[tile(0), tile(0), acc(1), acc(1), acc(FFN_CONV), acc(FFN_CONV)],
        compiler_params=_params("parallel", "arbitrary"),
    )(z, z, z, z, w_dw, w_dw, b_dw, b_dw, dact)


def _conv3_transpose(dug, w_dw, col0, tm=1024):
    s, f = dug.shape
    tm = min(tm, s)
    sub = min(FFN_SUB, tm)
    nsub = tm // sub
    tc = _ffn_cols(f)
    nfc = f // tc
    nrow = s // tm
    off = col0 // tc

    def body(d_ref, n_ref, w_ref, o_ref):
        keep_next = jnp.where(pl.program_id(0) == nrow - 1, 0.0, 1.0)

        def chunk(cs):
            for rb in range(nsub):
                rows = slice(rb * sub, (rb + 1) * sub)
                dv = d_ref[rows, cs].astype(F32)
                if rb == nsub - 1:
                    nxt = n_ref[:, cs].astype(F32) * keep_next
                else:
                    nxt = d_ref[(rb + 1) * sub:(rb + 1) * sub + 16, cs].astype(F32)[:8]
                n0, n1 = nxt[0:1, :], nxt[1:2, :]
                row = lax.broadcasted_iota(jnp.int32, (8, dv.shape[1]), 0)
                r1, r2 = pltpu.roll(dv, sub - 1, 0), pltpu.roll(dv, sub - 2, 0)
                d1 = jnp.concatenate([r1[:sub - 8], jnp.where(row == 7, n0, r1[sub - 8:])], axis=0)
                d2 = jnp.concatenate([r2[:sub - 8], jnp.where(row == 7, n1, jnp.where(row == 6, n0, r2[sub - 8:]))], axis=0)
                o_ref[rows, cs] = (w_ref[2:3, cs] * dv + w_ref[1:2, cs] * d1 + w_ref[0:1, cs] * d2).astype(BF16)

        _lane_chunks(tc, chunk)

    hb = tm // 8
    return pl.pallas_call(
        body, name="conv3_transpose", out_shape=jax.ShapeDtypeStruct((s, f), BF16), grid=(nrow, nfc),
        in_specs=[pl.BlockSpec((tm, tc), lambda i, j: (i, j)),
                  pl.BlockSpec((8, tc), lambda i, j: (jnp.minimum((i + 1) * hb, s // 8 - 1), j)),
                  pl.BlockSpec((FFN_CONV, tc), lambda i, j: (0, off + j))],
        out_specs=pl.BlockSpec((tm, tc), lambda i, j: (i, j)), compiler_params=_params("parallel", "parallel"),
    )(dug, dug, w_dw)


def _conv_mid_bwd(ag, u1, du3, b_dw, ln_g, ln_b, tm=256):
    s, c2 = ag.shape
    c = c2 // 2
    tm = min(tm, s)

    def body(ag_ref, halo_ref, u1in_ref, du_ref, b_ref, g_ref, bb_ref, o_ref, dlg_ref, dlb_ref, db_ref, dw_ref, ext_ref, u1_ref):
        @pl.when(pl.program_id(0) == 0)
        def _():
            for r_ in (dlg_ref, dlb_ref, db_ref, dw_ref):
                r_[...] = jnp.zeros_like(r_)

        _glu_planes(ag_ref, halo_ref, ext_ref, pl.program_id(0) == 0, c)
        xh, rstd = _layernorm_stats(u1in_ref[...] + b_ref[...])
        u2 = xh * g_ref[...] + bb_ref[...]
        sg = _sigmoid(u2)
        du2 = du_ref[...] * (sg * (1.0 + u2 * (1.0 - sg)))
        dlg_ref[...] += jnp.sum(du2 * xh, axis=0, keepdims=True)
        dlb_ref[...] += jnp.sum(du2, axis=0, keepdims=True)
        dxh = du2 * g_ref[...]
        du1 = rstd * (dxh - jnp.mean(dxh, axis=-1, keepdims=True) - xh * jnp.mean(dxh * xh, axis=-1, keepdims=True))
        o_ref[...] = du1.astype(BF16)
        db_ref[...] += jnp.sum(du1, axis=0, keepdims=True)
        u1_ref[...] = du1
        base = CONV_HALO - (CONV_KERNEL - 1)

        def chunk(cs):
            dc = u1_ref[:, cs]
            for j in range(CONV_KERNEL):
                dw_ref[j:j + 1, cs] += jnp.sum(dc * _window(ext_ref, base + j, tm, cs), axis=0, keepdims=True)

        _lane_chunks(c, chunk)

    hb = tm // CONV_HALO
    vec = _full((1, c))
    return pl.pallas_call(
        body, name="conv_mid_bwd",
        out_shape=[jax.ShapeDtypeStruct((s, c), BF16)] + [jax.ShapeDtypeStruct((1, c), F32)] * 3
        + [jax.ShapeDtypeStruct((CONV_HALO, c), F32)],
        grid=(s // tm,),
        in_specs=[_rows(tm, c2), pl.BlockSpec((CONV_HALO, c2), lambda i: (jnp.maximum(i * hb - 1, 0), 0)), _rows(tm, c),
                  _rows(tm, c), vec, vec, vec],
        out_specs=[_rows(tm, c), vec, vec, vec, _full((CONV_HALO, c))],
        scratch_shapes=[pltpu.VMEM((8, CONV_HALO + tm, c), F32), pltpu.VMEM((tm, c), F32)],
        compiler_params=_params("arbitrary"),
    )(ag, ag, u1, du3, b_dw, ln_g, ln_b)


def _glu_conv_bwd(du1, ag, w_dw, tm=512):
    s, c = du1.shape
    tm = min(tm, s)
    nrow = s // tm

    def body(d_ref, n_ref, ag_ref, w_ref, o_ref, db_ref, ext_ref, du0_ref):
        @pl.when(pl.program_id(0) == 0)
        def _():
            db_ref[...] = jnp.zeros_like(db_ref)

        ext_ref[0, 0:tm, :] = d_ref[...].astype(F32)
        ext_ref[0, tm:, :] = n_ref[...].astype(F32) * jnp.where(pl.program_id(0) == nrow - 1, 0.0, 1.0)
        _shifted_planes(ext_ref)
        top = CONV_KERNEL - 1
        _conv_taps(ext_ref, w_ref, [top - j for j in range(CONV_KERNEL)], tm, du0_ref)
        du0 = du0_ref[...]
        ag = ag_ref[...].astype(F32)
        a, gt = ag[:, :c], ag[:, c:]
        sg = _sigmoid(gt)
        da = du0 * sg
        dgt = du0 * a * (sg * (1.0 - sg))
        o_ref[:, :c] = da.astype(BF16)
        o_ref[:, c:] = dgt.astype(BF16)
        db_ref[:, :c] += jnp.sum(da, axis=0, keepdims=True)
        db_ref[:, c:] += jnp.sum(dgt, axis=0, keepdims=True)

    hb = tm // CONV_HALO
    return pl.pallas_call(
        body, name="glu_conv_bwd",
        out_shape=[jax.ShapeDtypeStruct((s, 2 * c), BF16), jax.ShapeDtypeStruct((1, 2 * c), F32)], grid=(nrow,),
        in_specs=[_rows(tm, c), pl.BlockSpec((CONV_HALO, c), lambda i: (jnp.minimum((i + 1) * hb, s // CONV_HALO - 1), 0)),
                  _rows(tm, 2 * c), _full((CONV_KERNEL, c))],
        out_specs=[_rows(tm, 2 * c), _full((1, 2 * c))],
        scratch_shapes=[pltpu.VMEM((8, tm + CONV_HALO, c), F32), pltpu.VMEM((tm, c), F32)],
        compiler_params=_params("arbitrary"),
    )(du1, du1, ag, w_dw)


def _head_rows(v, mask):
    return jnp.max(jnp.where(mask, v, -jnp.inf), axis=-1, keepdims=True)


def _attn_bwd(qv, dmix, mixed, lse, rope, grp, dil, ties=()):
    l = qv.shape[0]
    s = l * dil
    nb = l // SPAN
    view = lambda t: t.reshape(l, dil * t.shape[1])
    scale = HEAD_DIM ** -0.5
    gw = GROUP_WIDTH

    def body(*refs):
        q_ref, kp_ref, kc_ref, vp_ref, vc_ref, do_ref, mx_ref, l_ref, tab_ref, tabp_ref = refs[:10]
        dq_ref, dkv_ref, carry_ref = refs[-3:]
        b = pl.program_id(1)

        @pl.when(b < nb)
        def _():
            valid = _band_mask(b)
            masks, keep = _head_masks()
            for p in range(gw // 128):
                sl = slice(p * 128, (p + 1) * 128)
                sl_v = slice(gw + p * 128, gw + (p + 1) * 128)
                qp, dop = q_ref[:, sl], do_ref[:, sl]
                kk = jnp.concatenate([kp_ref[:, sl], kc_ref[:, sl]], axis=0)
                vv = jnp.concatenate([vp_ref[:, sl], vc_ref[:, sl]], axis=0)
                prod = dop.astype(F32) * mx_ref[:, sl].astype(F32)
                lsep = l_ref[:, sl]
                q2 = jnp.concatenate([qp * keep[0], qp * keep[1]], axis=0)
                do2 = jnp.concatenate([dop * keep[0], dop * keep[1]], axis=0)
                lse2 = jnp.concatenate([_head_rows(lsep, masks[h]) for h in range(2)], axis=0)
                dbar2 = jnp.concatenate([jnp.sum(jnp.where(masks[h], prod, 0.0), axis=-1, keepdims=True) for h in range(2)], axis=0)
                sc = lax.dot_general(q2, kk, NT, preferred_element_type=F32) * scale
                pe = jnp.where(valid, jnp.exp(sc - lse2), 0.0)
                dp = lax.dot_general(do2, vv, NT, preferred_element_type=F32)
                ds = (pe * (dp - dbar2) * scale).astype(BF16)
                dq2 = jnp.dot(ds, kk, preferred_element_type=F32)
                dq = jnp.where(masks[0], dq2[:SPAN], dq2[SPAN:])
                dq_ref[:, sl] = _rope_transpose(dq, tab_ref[...]).astype(BF16)
                dk = lax.dot_general(ds, q2, TN, preferred_element_type=F32)
                dv = lax.dot_general(pe.astype(BF16), do2, TN, preferred_element_type=F32)

                @pl.when(b > 0)
                def _():
                    dk_prev = carry_ref[:, sl] + dk[:SPAN]
                    dkv_ref[:, sl] = _rope_transpose(dk_prev, tabp_ref[...]).astype(BF16)
                    dkv_ref[:, sl_v] = (carry_ref[:, sl_v] + dv[:SPAN]).astype(BF16)

                carry_ref[:, sl] = dk[SPAN:]
                carry_ref[:, sl_v] = dv[SPAN:]

        @pl.when(b == nb)
        def _():
            for p in range(gw // 128):
                sl = slice(p * 128, (p + 1) * 128)
                sl_v = slice(gw + p * 128, gw + (p + 1) * 128)
                dkv_ref[:, sl] = _rope_transpose(carry_ref[:, sl], tabp_ref[...]).astype(BF16)
                dkv_ref[:, sl_v] = carry_ref[:, sl_v].astype(BF16)

    blk = (SPAN, gw)
    cb = lambda b: jnp.minimum(b, nb - 1)
    cur = lambda t: pl.BlockSpec(blk, lambda r, b: (cb(b), r * 3 + t))
    prev = lambda t: pl.BlockSpec(blk, lambda r, b: (jnp.maximum(cb(b) - 1, 0), r * 3 + t))
    own = pl.BlockSpec(blk, lambda r, b: (cb(b), r))
    tab = pl.BlockSpec((SPAN, ROPE_COLS), lambda r, b: (cb(b), r))
    tab_prev = pl.BlockSpec((SPAN, ROPE_COLS), lambda r, b: (jnp.maximum(b - 1, 0), r))
    dq, dkv = pl.pallas_call(
        body, name=f"attn_bwd_g{grp}",
        out_shape=[jax.ShapeDtypeStruct((l, dil * gw), BF16), jax.ShapeDtypeStruct((l, dil * 2 * gw), BF16)],
        grid=(dil, nb + 1),
        in_specs=[cur(0), prev(1), cur(1), prev(2), cur(2), own, own, own, tab, tab_prev] + [ANY] * len(ties),
        out_specs=[own, pl.BlockSpec((SPAN, 2 * gw), lambda r, b: (jnp.maximum(b - 1, 0), r))],
        scratch_shapes=[pltpu.VMEM((SPAN, 2 * gw), F32)], compiler_params=_params("parallel", "arbitrary"),
    )(qv, qv, qv, qv, qv, view(dmix), view(mixed), view(lse), view(rope), view(rope), *ties)
    return dq.reshape(s, gw), dkv.reshape(s, 2 * gw)


def _rope_freq_row():
    half = ROT_DIM // 2
    inv = (ROPE_THETA ** (-np.arange(half, dtype=np.float32) / half)).astype(np.float32)
    row = np.zeros((1, 128), np.float32)
    for head in range(128 // HEAD_DIM):
        row[0, head * HEAD_DIM:head * HEAD_DIM + half] = inv
        row[0, head * HEAD_DIM + half:head * HEAD_DIM + ROT_DIM] = inv
    return jnp.asarray(row)


def _ffn_fwd(x, g_pre, g_post, w_up_t, w_dw, b_dw, w_down):
    h, z = _norm_matmul(x, g_pre, w_up_t, tn=_tile(w_up_t.shape[0]), name="ffn_up")
    act = _ffn_act(z, w_dw, b_dw)
    y, xo = _matmul_resnorm(act, w_down, x, g_post, name="ffn_down")
    return xo, (x, h, z, act, y)


def _ffn_bwd(saved, dxo, g_pre, g_post, w_up_t, w_dw, b_dw, w_down):
    x, h, z, act, y = saved
    f = act.shape[1]
    d = x.shape[1]
    dy, dg_post = _postnorm_bwd(y, g_post, dxo, name="ffn_post_bwd")
    dact = _matmul(dy, w_down, name="ffn_dact", out_dtype=BF16, transposed_w=True)
    d_down = _weight_grad(act, dy, name="ffn_dw_down", out_shape=(1, f, d))
    dug_u, dug_g, db_u, db_g, dwd_u, dwd_g = _ffn_act_bwd(z, dact, w_dw, b_dw)
    dz_u = _conv3_transpose(dug_u, w_dw, 0)
    dz_g = _conv3_transpose(dug_g, w_dw, f)
    dx, dg_pre = _matmul_prenorm_bwd([(dz_u, 0, f, 0), (dz_g, 0, f, f)], w_up_t, x, g_pre, dxo, name="ffn_dx")
    d_up_t = _weight_grad(dz_u, h, name="ffn_dw_up", out_shape=(1, 2 * f, d))
    d_up_t = _weight_grad(dz_g, h, name="ffn_dw_up", out=d_up_t, row0=f)
    grads = dict(w_dw=jnp.concatenate([dwd_u, dwd_g], axis=1), b_dw=jnp.concatenate([db_u, db_g], axis=1),
                 g_pre=dg_pre, g_post=dg_post)
    return dx, grads, d_up_t, d_down


def _local_step(x, pos_col, target, p, tie=None, late_weights=None, exchange=None):
    ng = p["norm_g"]
    row = lambda r: ng[r:r + 1]
    freq = _rope_freq_row()
    rope = _rope_tables(pos_col, freq if tie is None else freq + tie[0:1])
    d = x.shape[1]

    h0, *qkv = _qkv_proj(x, row(0), p["w_qkv_t"], rope)
    os_, ls_ = zip(*[_attn_fwd(qkv[g_], g_, d_) for g_, d_ in enumerate(DILATIONS)])
    y_a, x1, mixed, lse = _mix_wo(os_, ls_, p["w_o_t"], x, row(1))
    if late_weights is not None:
        p = {**p, **late_weights(x1)}
    x2, ffn0 = _ffn_fwd(x1, row(2), row(3), p["w_up_t"][0], p["ffn_w_dw"][0], p["ffn_b_dw"][0], p["w_down"][0])
    h1, ag = _norm_matmul(x2, row(4), p["w_pw1_t"], tn=_tile(p["w_pw1_t"].shape[0]), name="conv_pw1", bias=p["b_pw1"])
    u3, u1 = _conv_mid(ag, p["conv_w_dw"], p["conv_b_dw"], p["ln_g"], p["ln_b"])
    y_c, x3 = _matmul_resnorm(u3, p["w_pw2"], x2, row(5), name="conv_pw2", bias=p["b_pw2"])
    x4, ffn1 = _ffn_fwd(x3, row(6), row(7), p["w_up_t"][1], p["ffn_w_dw"][1], p["ffn_b_dw"][1], p["w_down"][1])
    dx4, loss = _loss_grad(x4, target)

    big = [BF16, BF16]

    def tied(r, *tokens):
        tokens = [t for t in tokens if t is not None]
        return row(r) if not tokens else row(r) + jnp.tile(sum(tokens)[0:1], (1, d // 128))

    dx3, gf1, d_up1, d_down1 = _ffn_bwd(ffn1, dx4, row(6), row(7), p["w_up_t"][1], p["ffn_w_dw"][1], p["ffn_b_dw"][1],
                                        p["w_down"][1])
    t0 = exchange.submit("ffn1", [d_up1, d_down1], big) if exchange else None
    dy_c, dg5, db_pw2 = _postnorm_bwd(y_c, tied(5, t0), dx3, name="conv_post_bwd", with_bias_grad=True)
    du3 = _matmul(dy_c, p["w_pw2"], name="conv_du3", out_dtype=F32, transposed_w=True)
    d_wpw2 = _weight_grad(u3, dy_c, name="conv_dw_pw2", out_shape=(1, u3.shape[1], d))
    du1, d_lng, d_lnb, d_cbdw, d_cwdw = _conv_mid_bwd(ag, u1, du3, p["conv_b_dw"], p["ln_g"], p["ln_b"])
    dag, db_pw1 = _glu_conv_bwd(du1, ag, p["conv_w_dw"])
    dx2, dg4 = _matmul_prenorm_bwd([(dag, 0, dag.shape[1], 0)], p["w_pw1_t"], x2, row(4), dx3, name="conv_dx")
    d_wpw1_t = _weight_grad(dag, h1, name="conv_dw_pw1", out_shape=(1, dag.shape[1], d))
    t0 = exchange.advance(dx2) if exchange else None
    t1 = exchange.submit("conv", [d_wpw1_t, d_wpw2], big) if exchange else None
    dx1, gf0, d_up0, d_down0 = _ffn_bwd(ffn0, dx2, row(2), tied(3, t0, t1), p["w_up_t"][0], p["ffn_w_dw"][0], p["ffn_b_dw"][0],
                                        p["w_down"][0])
    t0 = exchange.advance(dx1) if exchange else None
    t1 = exchange.submit("ffn0", [d_up0, d_down0], big) if exchange else None
    dy_a, dg1 = _postnorm_bwd(y_a, tied(1, t0, t1), dx1, name="attn_post_bwd")
    dmix = _matmul(dy_a, p["w_o_t"], name="attn_dmix", out_dtype=BF16, transposed_w=False)
    d_wo_t = _weight_grad(dy_a, mixed, name="attn_dw_o", out_shape=(1, d, GROUP_WIDTH))
    pieces, d_wqkv_t = [], None
    for g_, d_ in enumerate(DILATIONS):
        tok = exchange.advance(dkv) if exchange and g_ > 0 else None
        dq, dkv = _attn_bwd(qkv[g_], dmix, mixed, lse, rope, g_, d_, ties=() if tok is None else (tok,))
        for t, (arr, c0) in enumerate(((dq, 0), (dkv, 0), (dkv, GROUP_WIDTH))):
            r0 = (3 * t + g_) * GROUP_WIDTH
            pieces.append((arr, c0, GROUP_WIDTH, r0))
            d_wqkv_t = _weight_grad(arr, h0, name="attn_dw_qkv", a_col0=c0, ka=GROUP_WIDTH, out=d_wqkv_t,
                                    out_shape=(1, p["w_qkv_t"].shape[0], d), row0=r0)
    t0 = exchange.advance(dkv) if exchange else None
    t1 = exchange.submit("attn", [d_wqkv_t, d_wo_t], big) if exchange else None
    grad_x, dg0 = _matmul_prenorm_bwd(pieces, p["w_qkv_t"], x, tied(0, t0, t1), dx1, name="attn_dx")

    grads = dict(
        norm_g=jnp.concatenate([dg0, dg1, gf0["g_pre"], gf0["g_post"], dg4, dg5, gf1["g_pre"], gf1["g_post"]], axis=0),
        w_qkv_t=d_wqkv_t, w_o_t=d_wo_t, w_pw1_t=d_wpw1_t, b_pw1=db_pw1,
        conv_w_dw=d_cwdw[:CONV_KERNEL], conv_b_dw=d_cbdw, ln_g=d_lng, ln_b=d_lnb, w_pw2=d_wpw2, b_pw2=db_pw2,
        w_up_t=[d_up0, d_up1], ffn_w_dw=jnp.stack([gf0["w_dw"], gf1["w_dw"]]),
        ffn_b_dw=jnp.concatenate([gf0["b_dw"], gf1["b_dw"]], axis=0), w_down=[d_down0, d_down1])
    return loss, grad_x, grads


SMALL_AXIS = dict(norm_g=2, conv_b_pw1=1, conv_w_dw=2, conv_b_dw=1, conv_ln_g=1, conv_ln_b=1, conv_b_pw2=1, ffn_w_dw=2)
SMALL = tuple(SMALL_AXIS)
MATMUL_WEIGHTS = dict(attn_w_qkv=True, conv_w_pw1=True, ffn_w_up=True, conv_w_pw2=False, ffn_w_down=False)


def _pack(arrays, cols, row_multiple):
    flat = jnp.concatenate([a.reshape(-1) for a in arrays])
    rows = -(-flat.shape[0] // cols)
    rows = -(-rows // row_multiple) * row_multiple
    return jnp.pad(flat, (0, rows * cols - flat.shape[0])).reshape(rows, cols)


def _unpack(packed, shapes):
    flat = packed.reshape(packed.shape[:-2] + (-1,))
    out, off = [], 0
    for shp in shapes:
        n = math.prod(shp)
        out.append(flat[..., off:off + n].reshape(packed.shape[:-2] + tuple(shp)))
        off += n
    return out


def _join_shards(stacked, axis):
    moved = jnp.moveaxis(stacked, 0, axis)
    shp = moved.shape
    return moved.reshape(shp[:axis] + (shp[axis] * shp[axis + 1],) + shp[axis + 2:])


def _split_shards(whole, axis):
    shp = whole.shape
    cut = whole.reshape(shp[:axis] + (N_DEV, shp[axis] // N_DEV) + shp[axis + 1:])
    return jnp.moveaxis(cut, axis, 0)


def _row_shard(w, transposed):
    t = jnp.swapaxes(w, 1, 2) if transposed else w
    return t.astype(BF16).reshape(-1, t.shape[-1])


def kernel(x, positions, norm_g, attn_w_qkv, attn_w_o, conv_w_pw1, conv_b_pw1, conv_w_dw, conv_b_dw, conv_ln_g, conv_ln_b, conv_w_pw2, conv_b_pw2, ffn_w_up, ffn_w_dw, ffn_b_dw, ffn_w_down, loss_target, m_norm_g, m_attn_w_qkv, m_attn_w_o, m_conv_w_pw1, m_conv_b_pw1, m_conv_w_dw, m_conv_b_dw, m_conv_ln_g, m_conv_ln_b, m_conv_w_pw2, m_conv_b_pw2, m_ffn_w_up, m_ffn_w_dw, m_ffn_b_dw, m_ffn_w_down, v_norm_g, v_attn_w_qkv, v_attn_w_o, v_conv_w_pw1, v_conv_b_pw1, v_conv_w_dw, v_conv_b_dw, v_conv_ln_g, v_conv_ln_b, v_conv_w_pw2, v_conv_b_pw2, v_ffn_w_up, v_ffn_w_dw, v_ffn_b_dw, v_ffn_w_down):
    w = dict(norm_g=norm_g, attn_w_qkv=attn_w_qkv, attn_w_o=attn_w_o, conv_w_pw1=conv_w_pw1, conv_b_pw1=conv_b_pw1,
             conv_w_dw=conv_w_dw, conv_b_dw=conv_b_dw, conv_ln_g=conv_ln_g, conv_ln_b=conv_ln_b, conv_w_pw2=conv_w_pw2,
             conv_b_pw2=conv_b_pw2, ffn_w_up=ffn_w_up, ffn_w_dw=ffn_w_dw, ffn_w_down=ffn_w_down)
    m = dict(norm_g=m_norm_g, attn_w_qkv=m_attn_w_qkv, attn_w_o=m_attn_w_o, conv_w_pw1=m_conv_w_pw1, conv_b_pw1=m_conv_b_pw1,
             conv_w_dw=m_conv_w_dw, conv_b_dw=m_conv_b_dw, conv_ln_g=m_conv_ln_g, conv_ln_b=m_conv_ln_b, conv_w_pw2=m_conv_w_pw2,
             conv_b_pw2=m_conv_b_pw2, ffn_w_up=m_ffn_w_up, ffn_w_dw=m_ffn_w_dw, ffn_w_down=m_ffn_w_down)
    v = dict(norm_g=v_norm_g, attn_w_qkv=v_attn_w_qkv, attn_w_o=v_attn_w_o, conv_w_pw1=v_conv_w_pw1, conv_b_pw1=v_conv_b_pw1,
             conv_w_dw=v_conv_w_dw, conv_b_dw=v_conv_b_dw, conv_ln_g=v_conv_ln_g, conv_ln_b=v_conv_ln_b, conv_w_pw2=v_conv_w_pw2,
             conv_b_pw2=v_conv_b_pw2, ffn_w_up=v_ffn_w_up, ffn_w_dw=v_ffn_w_dw, ffn_w_down=v_ffn_w_down)
    d = x.shape[-1]

    w_qkv_t, w_o_t, small = _all_gather([_row_shard(attn_w_qkv, True), _row_shard(attn_w_o, True),
                                         _pack([w[n] for n in SMALL], 128, 8)], "gather_first_weights")
    w_qkv_t, w_o_t = w_qkv_t.reshape(-1, d), w_o_t.reshape(d, -1)
    sm = {n: _join_shards(stacked, SMALL_AXIS[n])
          for n, stacked in zip(SMALL, _unpack(small, [w[n].shape for n in SMALL]))}
    late = {n: t for n, t in MATMUL_WEIGHTS.items() if n != "attn_w_qkv"}
    shares = [_row_shard(w[n], t) for n, t in late.items()]
    rows = [s_.shape[0] for s_ in shares]
    late_share = jnp.concatenate(shares, axis=0)
    send_sems, recv_sems, share_thru, land_thru, tie = _gather_start(late_share)
    me = 4 * lax.axis_index("x") + 2 * lax.axis_index("y") + lax.axis_index("c")

    def late_weights(after):
        big = _gather_wait(send_sems, recv_sems, share_thru, land_thru, after)
        big = lax.dynamic_update_slice(big, late_share[None], (me, 0, 0))
        whole, r0 = {}, 0
        for n, nr in zip(late, rows):
            layers = w[n].shape[0]
            seg = big[:, r0:r0 + nr].reshape(N_DEV, layers, nr // layers, d)
            whole[n] = [seg[:, l_].reshape(-1, d) for l_ in range(layers)]
            r0 += nr
        return dict(w_pw1_t=whole["conv_w_pw1"][0], w_pw2=whole["conv_w_pw2"][0], w_up_t=whole["ffn_w_up"],
                    w_down=whole["ffn_w_down"])

    p = dict(norm_g=sm["norm_g"].reshape(-1, d), w_qkv_t=w_qkv_t, w_o_t=w_o_t, b_pw1=sm["conv_b_pw1"],
             conv_w_dw=sm["conv_w_dw"][0], conv_b_dw=sm["conv_b_dw"], ln_g=sm["conv_ln_g"], ln_b=sm["conv_ln_b"],
             b_pw2=sm["conv_b_pw2"], ffn_w_dw=sm["ffn_w_dw"], ffn_b_dw=[ffn_b_dw[0:1], ffn_b_dw[1:2]])

    exchange = _GradExchange()
    loss, grad_x, g = _local_step(x[0], positions.reshape(-1, 1), loss_target[0], p, tie, late_weights, exchange)
    loss = lax.psum(loss[0, 0], ("x", "y", "c"))
    gsmall = dict(norm_g=g["norm_g"].reshape(norm_g.shape[0], 4, -1), conv_b_pw1=g["b_pw1"], conv_w_dw=g["conv_w_dw"][None],
                  conv_b_dw=g["conv_b_dw"], conv_ln_g=g["ln_g"], conv_ln_b=g["ln_b"], conv_b_pw2=g["b_pw2"], ffn_w_dw=g["ffn_w_dw"])
    small_contrib = jnp.concatenate([_split_shards(gsmall[n], SMALL_AXIS[n]).reshape(N_DEV, -1) for n in SMALL], axis=1)
    srows = small.shape[1]
    small_contrib = jnp.pad(small_contrib, ((0, 0), (0, srows * 128 - small_contrib.shape[1]))).reshape(1, N_DEV, srows, 128)
    exchange.advance(grad_x)
    small_sums = _rs_chips([_rs_pair_add(small_contrib, _rs_sibling([small_contrib])[0], exchange.core, F32)])[0]

    outs = {}

    def update(n, reduced):
        gsum = jnp.swapaxes(reduced, 1, 2) if n == "attn_w_o" or MATMUL_WEIGHTS.get(n) else reduced
        outs[n] = (gsum, *_adamw(gsum, w[n], m[n], v[n], "adamw"))

    (s_up1, s_down1), (s_pw1, s_pw2), (s_up0, s_down0) = exchange.results()[:3]
    update("conv_w_pw1", s_pw1)
    update("conv_w_pw2", s_pw2)
    update("ffn_w_up", jnp.concatenate([s_up0, s_up1], axis=0))
    update("ffn_w_down", jnp.concatenate([s_down0, s_down1], axis=0))
    sshapes = [w[n].shape for n in SMALL]
    souts = _sum_adamw(small_sums[0], *[_pack([t[n] for n in SMALL], 128, 8) for t in (w, m, v)], name="sum_adamw_small")
    for n, vals in zip(SMALL, zip(*[_unpack(o, sshapes) for o in souts])):
        outs[n] = vals
    bparts, = _all_gather([_pack([g["ffn_b_dw"]], 128, 8)], "gather_bias_grads")
    bouts = _sum_adamw(bparts, *[_pack([t], 128, 8) for t in (ffn_b_dw, m_ffn_b_dw, v_ffn_b_dw)], name="sum_adamw_bias")
    outs["ffn_b_dw"] = tuple(_unpack(o, [ffn_b_dw.shape])[0] for o in bouts)
    done = [outs[n][1][0, :8, :128] for n in ("conv_w_pw1", "conv_w_pw2", "ffn_w_up", "ffn_w_down")]
    exchange.advance(sum(done) + bouts[1][:8] + souts[1][:8])
    s_qkv, s_wo = exchange.results()[3]
    update("attn_w_qkv", s_qkv)
    update("attn_w_o", s_wo)

    order = ("norm_g", "attn_w_qkv", "attn_w_o", "conv_w_pw1", "conv_b_pw1", "conv_w_dw", "conv_b_dw", "conv_ln_g",
             "conv_ln_b", "conv_w_pw2", "conv_b_pw2", "ffn_w_up", "ffn_w_dw", "ffn_b_dw", "ffn_w_down")
    return (loss, grad_x[None], *[outs[n][0] for n in order], *[outs[n][1] for n in order],
            *[outs[n][2] for n in order], *[outs[n][3] for n in order])
```

```python
import math

import numpy as np
import jax
import jax.numpy as jnp
from jax import lax
from jax.experimental import pallas as pl
from jax.experimental.pallas import tpu as pltpu

F32 = jnp.float32
BF16 = jnp.bfloat16
EPS = 1e-6
N_DEV = 8
HEAD_DIM = 64
GROUP_WIDTH = 512
DILATIONS = (1, 4, 16)
SPAN = 128
ROT_DIM = 16
ROPE_THETA = 500000.0
CONV_KERNEL = 31
CONV_HALO = 32
FFN_CONV = 3
ADAM_LR, ADAM_B1, ADAM_B2, ADAM_EPS, ADAM_WD, ADAM_STEP = 0.001, 0.9, 0.999, 1e-08, 0.01, 10
VMEM_LIMIT_BYTES = 56 * 1024 * 1024
MESH = pl.DeviceIdType.MESH
ANY = pl.BlockSpec(memory_space=pl.ANY)
NT = (((1,), (1,)), ((), ()))
TN = (((0,), (0,)), ((), ()))


def _params(*sem):
    return pltpu.CompilerParams(dimension_semantics=sem, vmem_limit_bytes=VMEM_LIMIT_BYTES)


def _sigmoid(v):
    return pl.reciprocal(1.0 + jnp.exp(-v), approx=True)


def _full(shape):
    return pl.BlockSpec(shape, lambda *_: (0,) * len(shape))


def _rows(tm, width):
    return pl.BlockSpec((tm, width), lambda i, *_: (i, 0))


def _tile(n, *multiples_of):
    for t in (1408, 1024, 512, 384, 256, 128):
        if n % t == 0 and all(o % t == 0 for o in multiples_of):
            return t
    raise ValueError((n, multiples_of))


def _all_gather(shards, name):
    n = len(shards)

    def body(*refs):
        x_refs, out_refs, (send_sems, recv_sems, local_sems) = refs[:n], refs[n:2 * n], refs[2 * n:]
        x, y, c = lax.axis_index("x"), lax.axis_index("y"), lax.axis_index("c")
        me, sibling = (x, y, c), (x, y, 1 - c)
        chips = [(1 - x, y), (x, 1 - y), (1 - x, 1 - y)]

        def rows(w, px, py, pc):
            return out_refs[w].at[4 * px + 2 * py + pc]

        def copy(w, k, block, to, src=None):
            return pltpu.make_async_remote_copy(
                src_ref=rows(w, *block) if src is None else src, dst_ref=rows(w, *block),
                send_sem=send_sems.at[7 * w + k], recv_sem=recv_sems.at[7 * w + k], device_id=to, device_id_type=MESH)

        every = range(n)
        mine = [pltpu.make_async_copy(x_refs[w], rows(w, *me), local_sems.at[w]) for w in every]
        first = [copy(w, 0, me, sibling, src=x_refs[w]) for w in every]
        first += [copy(w, 1 + j, me, (*chip, c), src=x_refs[w]) for w in every for j, chip in enumerate(chips)]
        for cp in mine + first:
            cp.start()
        passed = []
        for j, chip in enumerate(chips):
            for w in every:
                copy(w, 1 + j, (*chip, c), me).wait_recv()
                passed.append(copy(w, 4 + j, (*chip, c), sibling))
                passed[-1].start()
        for w in every:
            copy(w, 0, sibling, me).wait_recv()
            for j, chip in enumerate(chips):
                copy(w, 4 + j, (*chip, 1 - c), me).wait_recv()
        for cp in first + passed:
            cp.wait_send()
        for cp in mine:
            cp.wait()

    return pl.pallas_call(
        body, name=name, out_shape=[jax.ShapeDtypeStruct((N_DEV,) + s_.shape, s_.dtype) for s_ in shards],
        in_specs=[ANY] * n, out_specs=[ANY] * n,
        scratch_shapes=[pltpu.SemaphoreType.DMA((7 * n,)), pltpu.SemaphoreType.DMA((7 * n,)), pltpu.SemaphoreType.DMA((n,))],
    )(*shards)


HBM = pl.BlockSpec(memory_space=pltpu.HBM)
SEM = pl.BlockSpec(memory_space=pltpu.SEMAPHORE)
SIDE_EFFECT = pltpu.CompilerParams(has_side_effects=pltpu.SideEffectType.DATAFLOW_SIDE_EFFECTING)


def _gather_start(shard):
    r, c_ = shard.shape

    def body(x_ref, land_ref, send_sems, recv_sems, x_thru, land_thru, token):
        x, y, c = lax.axis_index("x"), lax.axis_index("y"), lax.axis_index("c")
        me = 4 * x + 2 * y + c
        for k in range(1, N_DEV):
            peer = (1 - x if k & 4 else x, 1 - y if k & 2 else y, 1 - c if k & 1 else c)
            pltpu.make_async_remote_copy(src_ref=x_ref, dst_ref=land_ref.at[me], send_sem=send_sems.at[k - 1],
                                         recv_sem=recv_sems.at[k - 1], device_id=peer, device_id_type=MESH).start()
        token[...] = jnp.zeros_like(token)

    land = pltpu.with_memory_space_constraint(lax.empty((N_DEV, r, c_), shard.dtype), pltpu.HBM)
    return pl.pallas_call(
        body, name="gather_late_weights_start",
        out_shape=(pltpu.SemaphoreType.DMA((N_DEV - 1,)), pltpu.SemaphoreType.DMA((N_DEV - 1,)),
                   pltpu.HBM(shard.shape, shard.dtype), pltpu.HBM((N_DEV, r, c_), shard.dtype),
                   jax.ShapeDtypeStruct((8, 128), F32)),
        in_specs=(HBM, HBM), out_specs=(SEM, SEM, HBM, HBM, pl.BlockSpec(memory_space=pltpu.VMEM)),
        input_output_aliases={0: 2, 1: 3}, compiler_params=SIDE_EFFECT,
    )(pltpu.with_memory_space_constraint(shard, pltpu.HBM), land)


def _gather_wait(send_sems, recv_sems, shard_thru, land_thru, after):
    def body(x_ref, land_ref, send_sems, recv_sems, after_ref, x_dead, got_ref):
        x, y, c = lax.axis_index("x"), lax.axis_index("y"), lax.axis_index("c")
        for k in range(N_DEV - 1):
            copy = pltpu.make_async_remote_copy(src_ref=x_ref, dst_ref=land_ref.at[0], send_sem=send_sems.at[k],
                                                recv_sem=recv_sems.at[k], device_id=(x, y, c), device_id_type=MESH)
            copy.wait_send()
            copy.wait_recv()

    return pl.pallas_call(
        body, name="gather_late_weights_wait",
        out_shape=(pltpu.HBM(shard_thru.shape, shard_thru.dtype), pltpu.HBM(land_thru.shape, land_thru.dtype)),
        in_specs=(HBM, HBM, SEM, SEM, ANY), out_specs=(HBM, HBM), input_output_aliases={0: 0, 1: 1},
        compiler_params=SIDE_EFFECT,
    )(shard_thru, land_thru, send_sems, recv_sems, after)[1]


def _hbm(a):
    return pltpu.with_memory_space_constraint(a, pltpu.HBM)


def _exchange_start(name, arrays, lands, plan, ncopies):
    n = len(arrays)

    def body(*refs):
        send_sems, recv_sems, token = refs[2 * n], refs[2 * n + 1], refs[-1]
        x, y, c = lax.axis_index("x"), lax.axis_index("y"), lax.axis_index("c")
        for k, (src, dst, peer) in enumerate(plan(x, y, c, refs[:n], refs[n:2 * n])):
            pltpu.make_async_remote_copy(src_ref=src, dst_ref=dst, send_sem=send_sems.at[k], recv_sem=recv_sems.at[k],
                                         device_id=peer, device_id_type=MESH).start()
        token[...] = jnp.zeros_like(token)

    both = list(arrays) + list(lands)
    outs = pl.pallas_call(
        body, name=name,
        out_shape=(pltpu.SemaphoreType.DMA((ncopies,)), pltpu.SemaphoreType.DMA((ncopies,)),
                   *[pltpu.HBM(a.shape, a.dtype) for a in both], jax.ShapeDtypeStruct((8, 128), F32)),
        in_specs=(HBM,) * (2 * n), out_specs=(SEM, SEM) + (HBM,) * (2 * n) + (pl.BlockSpec(memory_space=pltpu.VMEM),),
        input_output_aliases={i: 2 + i for i in range(2 * n)}, compiler_params=SIDE_EFFECT,
    )(*[_hbm(a) for a in both])
    return outs[0], outs[1], list(outs[2:2 + n]), list(outs[2 + n:2 + 2 * n]), outs[-1]


def _exchange_wait(name, send_sems, recv_sems, arrays, lands, plan, after):
    n = len(arrays)

    def body(*refs):
        send_sems, recv_sems = refs[2 * n], refs[2 * n + 1]
        x, y, c = lax.axis_index("x"), lax.axis_index("y"), lax.axis_index("c")
        for k, (src, dst, peer) in enumerate(plan(x, y, c, refs[:n], refs[n:2 * n])):
            copy = pltpu.make_async_remote_copy(src_ref=src, dst_ref=dst, send_sem=send_sems.at[k], recv_sem=recv_sems.at[k],
                                                device_id=peer, device_id_type=MESH)
            copy.wait_send()
            copy.wait_recv()

    both = list(arrays) + list(lands)
    outs = pl.pallas_call(
        body, name=name, out_shape=tuple(pltpu.HBM(a.shape, a.dtype) for a in both),
        in_specs=(HBM,) * (2 * n) + (SEM, SEM, ANY), out_specs=(HBM,) * (2 * n),
        input_output_aliases={i: i for i in range(2 * n)}, compiler_params=SIDE_EFFECT,
    )(*both, send_sems, recv_sems, after)
    return list(outs[:n]), list(outs[n:])


def _sibling_plan(x, y, c, g_refs, land_refs):
    return [(g.at[:, 2 * q + (1 - c)], o.at[:, q], (x, y, 1 - c)) for g, o in zip(g_refs, land_refs) for q in range(4)]


def _chips_plan(x, y, c, p_refs, land_refs):
    chips = [(1 - x, y), (x, 1 - y), (1 - x, 1 - y)]
    return [(p_.at[:, 2 * qx + qy], o.at[:, 2 * x + y], (qx, qy, c)) for p_, o in zip(p_refs, land_refs) for qx, qy in chips]


class _GradExchange:
    def __init__(self):
        self.core = lax.axis_index("c").astype(jnp.int32).reshape(1)
        self.chip = 2 * lax.axis_index("x") + lax.axis_index("y")
        self.groups = []

    def submit(self, tag, arrays, dtypes):
        arrays = [a.reshape(a.shape[0], N_DEV, a.shape[1] // N_DEV, a.shape[2]) for a in arrays]
        lands = [lax.empty((a.shape[0], 4) + a.shape[2:], a.dtype) for a in arrays]
        send, recv, arrays, lands, token = _exchange_start(f"rs_pair_start_{tag}", arrays, lands, _sibling_plan, 4 * len(arrays))
        self.groups.append(dict(tag=tag, stage=1, sems=(send, recv), arrays=arrays, lands=lands, dtypes=dtypes))
        return token

    def advance(self, after):
        token = None
        for g in self.groups:
            if g["stage"] == 1:
                arrays, got = _exchange_wait(f"rs_pair_wait_{g['tag']}", *g["sems"], g["arrays"], g["lands"], _sibling_plan, after)
                parts = [_rs_pair_add(a, b, self.core, dt) for a, b, dt in zip(arrays, got, g["dtypes"])]
                lands = [lax.empty(p_.shape, p_.dtype) for p_ in parts]
                send, recv, parts, lands, tok = _exchange_start(f"rs_chip_start_{g['tag']}", parts, lands, _chips_plan, 3 * len(parts))
                g.update(stage=2, sems=(send, recv), arrays=parts, lands=lands)
                token = tok if token is None else token + tok
            elif g["stage"] == 2:
                parts, lands = _exchange_wait(f"rs_chip_wait_{g['tag']}", *g["sems"], g["arrays"], g["lands"], _chips_plan, after)
                sums = []
                for p_, land in zip(parts, lands):
                    l, _, r, c_ = p_.shape
                    own = lax.dynamic_slice(p_, (0, self.chip, 0, 0), (l, 1, r, c_))
                    sums.append(_sum_parts(lax.dynamic_update_slice(land, own, (0, self.chip, 0, 0)), "sum_chips"))
                g.update(stage=3, sums=sums)
        return token

    def results(self):
        return [g.get("sums") for g in self.groups]


def _with_rows(g, n):
    return jax.ShapeDtypeStruct((g.shape[0], n) + tuple(g.shape[2:]), g.dtype)


def _rs_sibling(gs):
    n = len(gs)

    def body(*refs):
        g_refs, o_refs, (send_sems, recv_sems) = refs[:n], refs[n:2 * n], refs[2 * n:]
        x, y, c = lax.axis_index("x"), lax.axis_index("y"), lax.axis_index("c")
        copies = [pltpu.make_async_remote_copy(
            src_ref=g_refs[w].at[:, 2 * q + (1 - c)], dst_ref=o_refs[w].at[:, q], send_sem=send_sems.at[4 * w + q],
            recv_sem=recv_sems.at[4 * w + q], device_id=(x, y, 1 - c), device_id_type=MESH)
            for w in range(n) for q in range(4)]
        for cp in copies:
            cp.start()
        for cp in copies:
            cp.wait_recv()
        for cp in copies:
            cp.wait_send()

    return pl.pallas_call(
        body, name="rs_sibling", out_shape=[_with_rows(g, 4) for g in gs],
        in_specs=[ANY] * n, out_specs=[ANY] * n,
        scratch_shapes=[pltpu.SemaphoreType.DMA((4 * n,)), pltpu.SemaphoreType.DMA((4 * n,))],
    )(*gs)


def _rs_pair_add(g, got, core, out_dtype):
    l, _, r, c_ = g.shape

    def body(core_ref, g_ref, got_ref, o_ref):
        o_ref[...] = (g_ref[...].astype(F32) + got_ref[...].astype(F32)).astype(out_dtype)

    blk = (None, None, r, c_)
    return pl.pallas_call(
        body, name="rs_pair_add", out_shape=jax.ShapeDtypeStruct((l, 4, r, c_), out_dtype),
        grid_spec=pltpu.PrefetchScalarGridSpec(
            num_scalar_prefetch=1, grid=(l, 4),
            in_specs=[pl.BlockSpec(blk, lambda i, q, core_ref: (i, 2 * q + core_ref[0], 0, 0)),
                      pl.BlockSpec(blk, lambda i, q, core_ref: (i, q, 0, 0))],
            out_specs=pl.BlockSpec(blk, lambda i, q, core_ref: (i, q, 0, 0))),
        compiler_params=_params("parallel", "parallel"),
    )(core, g, got)


def _rs_chips(parts):
    n = len(parts)

    def body(*refs):
        p_refs, o_refs, (send_sems, recv_sems, local_sems) = refs[:n], refs[n:2 * n], refs[2 * n:]
        x, y, c = lax.axis_index("x"), lax.axis_index("y"), lax.axis_index("c")
        my_chip = 2 * x + y
        chips = [(1 - x, y), (x, 1 - y), (1 - x, 1 - y)]
        local = [pltpu.make_async_copy(p_refs[w].at[:, my_chip], o_refs[w].at[:, my_chip], local_sems.at[w]) for w in range(n)]
        for cp in local:
            cp.start()
        copies = [pltpu.make_async_remote_copy(
            src_ref=p_refs[w].at[:, 2 * qx + qy], dst_ref=o_refs[w].at[:, my_chip], send_sem=send_sems.at[3 * w + k],
            recv_sem=recv_sems.at[3 * w + k], device_id=(qx, qy, c), device_id_type=MESH)
            for w in range(n) for k, (qx, qy) in enumerate(chips)]
        for cp in copies:
            cp.start()
        for cp in copies:
            cp.wait_recv()
        for cp in copies:
            cp.wait_send()
        for cp in local:
            cp.wait()

    return pl.pallas_call(
        body, name="rs_chips", out_shape=[jax.ShapeDtypeStruct(p.shape, p.dtype) for p in parts],
        in_specs=[ANY] * n, out_specs=[ANY] * n,
        scratch_shapes=[pltpu.SemaphoreType.DMA((3 * n,)), pltpu.SemaphoreType.DMA((3 * n,)), pltpu.SemaphoreType.DMA((n,))],
    )(*parts)


def _sum_parts(parts, name):
    l, n, r, c_ = parts.shape

    def body(p_ref, o_ref):
        g = p_ref[0].astype(F32)
        for s in range(1, n):
            g = g + p_ref[s].astype(F32)
        o_ref[...] = g

    return pl.pallas_call(
        body, name=name, out_shape=jax.ShapeDtypeStruct((l, r, c_), F32), grid=(l,),
        in_specs=[pl.BlockSpec((None, n, r, c_), lambda i: (i, 0, 0, 0))],
        out_specs=pl.BlockSpec((None, r, c_), lambda i: (i, 0, 0)), compiler_params=_params("parallel"),
    )(parts)


def _adamw_math(w, g, m, v):
    m = ADAM_B1 * m + (1.0 - ADAM_B1) * g
    v = ADAM_B2 * v + (1.0 - ADAM_B2) * (g * g)
    m_hat = m / (1.0 - ADAM_B1 ** ADAM_STEP)
    v_hat = v / (1.0 - ADAM_B2 ** ADAM_STEP)
    delta = -ADAM_LR * (m_hat / (jnp.sqrt(v_hat) + ADAM_EPS) + ADAM_WD * w)
    return delta, m, v


def _adamw(g, w, m, v, name):
    l, k, n = w.shape
    tk = 256 if k % 256 == 0 else k

    def body(g_ref, w_ref, m_ref, v_ref, d_ref, nm_ref, nv_ref):
        d_ref[...], nm_ref[...], nv_ref[...] = _adamw_math(w_ref[...], g_ref[...], m_ref[...], v_ref[...])

    spec = pl.BlockSpec((None, tk, n), lambda i, j: (i, j, 0))
    return pl.pallas_call(
        body, name=name, out_shape=[jax.ShapeDtypeStruct((l, k, n), F32)] * 3, grid=(l, k // tk),
        in_specs=[spec] * 4, out_specs=[spec] * 3, compiler_params=_params("parallel", "parallel"),
    )(g, w, m, v)


def _sum_adamw(parts, w, m, v, name):
    n, r, c_ = parts.shape

    def body(p_ref, w_ref, m_ref, v_ref, g_ref, d_ref, nm_ref, nv_ref):
        g = p_ref[0]
        for s in range(1, n):
            g = g + p_ref[s]
        g_ref[...] = g
        d_ref[...], nm_ref[...], nv_ref[...] = _adamw_math(w_ref[...], g, m_ref[...], v_ref[...])

    return pl.pallas_call(
        body, name=name, out_shape=[jax.ShapeDtypeStruct((r, c_), F32)] * 4, grid=(1,),
        in_specs=[_full((n, r, c_))] + [_full((r, c_))] * 3, out_specs=[_full((r, c_))] * 4,
        compiler_params=_params("arbitrary"),
    )(parts, w, m, v)


def _rope_tables(pos_col, freq_row):
    s = pos_col.shape[0]
    tm = min(1024, s)

    def body(p_ref, f_ref, o_ref):
        ang = p_ref[...].astype(F32) * f_ref[...]
        lane = lax.broadcasted_iota(jnp.int32, ang.shape, 1) & (HEAD_DIM - 1)
        cs, sn = jnp.cos(ang), jnp.sin(ang)
        o_ref[:, 0:128] = jnp.where(lane < ROT_DIM, cs, 1.0)
        o_ref[:, 128:256] = jnp.where((lane >= ROT_DIM // 2) & (lane < ROT_DIM), sn, 0.0)
        o_ref[:, 256:384] = jnp.where(lane < ROT_DIM // 2, -sn, 0.0)

    return pl.pallas_call(
        body, name="rope_tables", out_shape=jax.ShapeDtypeStruct((s, ROPE_COLS), F32), grid=(s // tm,),
        in_specs=[pl.BlockSpec((tm, 1), lambda i: (i, 0)), _full((1, 128))],
        out_specs=_rows(tm, ROPE_COLS), compiler_params=_params("parallel"),
    )(pos_col, freq_row)


ROPE_COLS = 3 * 128


def _rope_parts(tab, reps=1):
    return [jnp.tile(tab[:, k * 128:(k + 1) * 128], (1, reps)) if reps > 1 else tab[:, k * 128:(k + 1) * 128] for k in range(3)]


def _rope_apply(t, tab):
    w = t.shape[1]
    cos, sin_up, sin_dn = _rope_parts(tab, w // 128)
    return t * cos + pltpu.roll(t, 8, 1) * sin_up + pltpu.roll(t, w - 8, 1) * sin_dn


def _rope_transpose(dr, tab):
    w = dr.shape[1]
    cos, sin_up, sin_dn = _rope_parts(tab, w // 128)
    return dr * cos + pltpu.roll(dr * sin_up, w - 8, 1) + pltpu.roll(dr * sin_dn, 8, 1)


def _norm_matmul(x, g, wt, *, tn, name, bias=None, tm=1024):
    s, d = x.shape
    n = wt.shape[0]
    tm = min(tm, s)

    def body(*refs):
        x_ref, g_ref, w_ref = refs[:3]
        b_ref = refs[3] if bias is not None else None
        h_ref, o_ref = refs[-2:]

        @pl.when(pl.program_id(1) == 0)
        def _():
            xv = x_ref[...]
            r = lax.rsqrt(jnp.mean(xv * xv, axis=-1, keepdims=True) + EPS)
            h_ref[...] = (xv * r * g_ref[...]).astype(BF16)

        acc = lax.dot_general(h_ref[...], w_ref[...], NT, preferred_element_type=F32)
        if b_ref is not None:
            acc = acc + b_ref[...]
        o_ref[...] = acc.astype(BF16)

    in_specs = [_rows(tm, d), _full((1, d)), pl.BlockSpec((tn, d), lambda i, j: (j, 0))]
    args = [x, g, wt]
    if bias is not None:
        in_specs.append(pl.BlockSpec((1, tn), lambda i, j: (0, j)))
        args.append(bias)
    return pl.pallas_call(
        body, name=name,
        out_shape=[jax.ShapeDtypeStruct((s, d), BF16), jax.ShapeDtypeStruct((s, n), BF16)],
        grid=(s // tm, n // tn), in_specs=in_specs,
        out_specs=[_rows(tm, d), pl.BlockSpec((tm, tn), lambda i, j: (i, j))],
        compiler_params=_params("parallel", "arbitrary"),
    )(*args)


def _class_major(tm, dil):
    p = np.zeros((tm, tm), np.float32)
    per = tm // dil
    for r in range(dil):
        for j in range(per):
            p[r * per + j, j * dil + r] = 1.0
    return jnp.asarray(p, dtype=BF16)


def _qkv_proj(x, g, wt, rope, tm=512):
    s, d = x.shape
    n = wt.shape[0]
    gw3 = 3 * GROUP_WIDTH
    tm = min(tm, s)
    assert n == 3 * gw3

    def body(x_ref, g_ref, w_ref, tab_ref, p1_ref, p2_ref, h_ref, o0_ref, o1_ref, o2_ref):
        j = pl.program_id(1)

        @pl.when(j == 0)
        def _():
            xv = x_ref[...]
            r = lax.rsqrt(jnp.mean(xv * xv, axis=-1, keepdims=True) + EPS)
            h_ref[...] = (xv * r * g_ref[...]).astype(BF16)

        acc = lax.dot_general(h_ref[...], w_ref[...], NT, preferred_element_type=F32)

        def store(y):
            yb = y.astype(BF16)
            o0_ref[:, pl.ds(pl.multiple_of(j * GROUP_WIDTH, GROUP_WIDTH), GROUP_WIDTH)] = yb[:, :GROUP_WIDTH]
            for grp, o_ref, p_ref in ((1, o1_ref, p1_ref), (2, o2_ref, p2_ref)):
                dil = DILATIONS[grp]
                per = tm // dil
                yp = jnp.dot(p_ref[...], yb[:, grp * GROUP_WIDTH:(grp + 1) * GROUP_WIDTH],
                             preferred_element_type=F32).astype(BF16)
                for r in range(dil):
                    col = pl.multiple_of(r * gw3 + j * GROUP_WIDTH, GROUP_WIDTH)
                    o_ref[:, pl.ds(col, GROUP_WIDTH)] = yp[r * per:(r + 1) * per, :]

        @pl.when(j < 2)
        def _():
            store(_rope_apply(acc, tab_ref[...]))

        @pl.when(j == 2)
        def _():
            store(acc)

    outs = [jax.ShapeDtypeStruct((s, d), BF16)] + [jax.ShapeDtypeStruct((s // dl, dl * gw3), BF16) for dl in DILATIONS]
    out_specs = [_rows(tm, d)] + [_rows(tm // dl, dl * gw3) for dl in DILATIONS]
    return pl.pallas_call(
        body, name="attn_qkv", out_shape=outs, grid=(s // tm, 3),
        in_specs=[_rows(tm, d), _full((1, d)), pl.BlockSpec((gw3, d), lambda i, j: (j, 0)), _rows(tm, ROPE_COLS)]
        + [_full((tm, tm))] * 2,
        out_specs=out_specs, compiler_params=_params("parallel", "arbitrary"),
    )(x, g, wt, rope, _class_major(tm, DILATIONS[1]), _class_major(tm, DILATIONS[2]))


def _head_masks(rows=SPAN):
    lane = lax.broadcasted_iota(jnp.int32, (rows, 128), 1)
    masks = [lane < HEAD_DIM, lane >= HEAD_DIM]
    lane1 = lax.broadcasted_iota(jnp.int32, (1, 128), 1)
    keep = [jnp.where(lane1 < HEAD_DIM, 1.0, 0.0).astype(BF16), jnp.where(lane1 >= HEAD_DIM, 1.0, 0.0).astype(BF16)]
    return masks, keep


def _band_mask(b):
    row = lax.broadcasted_iota(jnp.int32, (2 * SPAN, 2 * SPAN), 0) & (SPAN - 1)
    col = lax.broadcasted_iota(jnp.int32, (2 * SPAN, 2 * SPAN), 1)
    no_prev = jnp.where(b > 0, 0, 4 * SPAN)
    return ((col < SPAN) & (col >= row + no_prev)) | ((col >= SPAN) & (col - SPAN <= row))


def _attn_fwd(qv, grp, dil):
    l = qv.shape[0]
    s = l * dil
    nb = l // SPAN
    nq = next(n for n in (8, 4, 2, 1) if nb % n == 0)

    def body(q_ref, kp_ref, kc_ref, vp_ref, vc_ref, o_ref, l_ref):
        b = pl.program_id(1)
        masks, keep = _head_masks()
        for qb in range(nq):
            valid = _band_mask(b * nq + qb)
            rows = slice(qb * SPAN, (qb + 1) * SPAN)
            before = slice((qb - 1) * SPAN, qb * SPAN)
            for p in range(GROUP_WIDTH // 128):
                sl = slice(p * 128, (p + 1) * 128)
                qp = q_ref[rows, sl]
                kk = jnp.concatenate([kp_ref[:, sl] if qb == 0 else kc_ref[before, sl], kc_ref[rows, sl]], axis=0)
                vv = jnp.concatenate([vp_ref[:, sl] if qb == 0 else vc_ref[before, sl], vc_ref[rows, sl]], axis=0)
                q2 = jnp.concatenate([qp * keep[0], qp * keep[1]], axis=0)
                sc = lax.dot_general(q2, kk, NT, preferred_element_type=F32) * (HEAD_DIM ** -0.5)
                sc = jnp.where(valid, sc, -1e30)
                mx = jnp.max(sc, axis=-1, keepdims=True)
                pe = jnp.exp(sc - mx)
                den = jnp.sum(pe, axis=-1, keepdims=True)
                out = jnp.dot(pe.astype(BF16), vv, preferred_element_type=F32) / den
                lse = jnp.broadcast_to(mx + jnp.log(den), (2 * SPAN, 128))
                o_ref[rows, sl] = jnp.where(masks[0], out[:SPAN], out[SPAN:]).astype(BF16)
                l_ref[rows, sl] = jnp.where(masks[0], lse[:SPAN], lse[SPAN:])

    blk = (nq * SPAN, GROUP_WIDTH)
    cur = lambda t: pl.BlockSpec(blk, lambda r, b: (b, r * 3 + t))
    prev = lambda t: pl.BlockSpec((SPAN, GROUP_WIDTH), lambda r, b: (jnp.maximum(nq * b - 1, 0), r * 3 + t))
    out = pl.BlockSpec(blk, lambda r, b: (b, r))
    o, lse = pl.pallas_call(
        body, name=f"attn_fwd_g{grp}",
        out_shape=[jax.ShapeDtypeStruct((l, dil * GROUP_WIDTH), BF16), jax.ShapeDtypeStruct((l, dil * GROUP_WIDTH), F32)],
        grid=(dil, nb // nq), in_specs=[cur(0), prev(1), cur(1), prev(2), cur(2)], out_specs=[out, out],
        compiler_params=_params("parallel", "arbitrary"),
    )(qv, qv, qv, qv, qv)
    return o.reshape(s, GROUP_WIDTH), lse.reshape(s, GROUP_WIDTH)


def _resnorm_store(y, x_ref, g_ref, y_ref, xo_ref):
    r = lax.rsqrt(jnp.mean(y * y, axis=-1, keepdims=True) + EPS)
    y_ref[...] = y
    xo_ref[...] = x_ref[...] + y * r * g_ref[...]


def _mix_wo(os_, ls_, wot, x, g, tm=512):
    s, d = x.shape
    gw = wot.shape[1]
    tm = min(tm, s)

    def body(o0, o1, o2, l0, l1, l2, w_ref, x_ref, g_ref, y_ref, xo_ref, mixed_ref, lse_ref):
        a0, a1, a2 = l0[...], l1[...], l2[...]
        mx = jnp.maximum(jnp.maximum(a0, a1), a2)
        e0, e1, e2 = jnp.exp(a0 - mx), jnp.exp(a1 - mx), jnp.exp(a2 - mx)
        den = e0 + e1 + e2
        mixed = (e0 / den) * o0[...].astype(F32) + (e1 / den) * o1[...].astype(F32) + (e2 / den) * o2[...].astype(F32)
        mixed_ref[...] = mixed.astype(BF16)
        lse_ref[...] = mx + jnp.log(den)
        y = lax.dot_general(mixed.astype(BF16), w_ref[...], NT, preferred_element_type=F32)
        _resnorm_store(y, x_ref, g_ref, y_ref, xo_ref)

    return pl.pallas_call(
        body, name="mix_wo",
        out_shape=[jax.ShapeDtypeStruct((s, d), F32), jax.ShapeDtypeStruct((s, d), F32),
                   jax.ShapeDtypeStruct((s, gw), BF16), jax.ShapeDtypeStruct((s, gw), F32)],
        grid=(s // tm,), in_specs=[_rows(tm, gw)] * 6 + [_full((d, gw)), _rows(tm, d), _full((1, d))],
        out_specs=[_rows(tm, d), _rows(tm, d), _rows(tm, gw), _rows(tm, gw)],
        compiler_params=_params("parallel"),
    )(*os_, *ls_, wot, x, g)


def _matmul_resnorm(a, w, x, g, *, name, bias=None, tm=512):
    s, k = a.shape
    d = w.shape[1]
    tm = min(tm, s)

    def body(*refs):
        a_ref, w_ref = refs[:2]
        b_ref = refs[2] if bias is not None else None
        x_ref, g_ref, y_ref, xo_ref = refs[-4:]
        y = jnp.dot(a_ref[...], w_ref[...], preferred_element_type=F32)
        if b_ref is not None:
            y = y + b_ref[...]
        _resnorm_store(y, x_ref, g_ref, y_ref, xo_ref)

    in_specs = [_rows(tm, k), _full((k, d))] + ([_full((1, d))] if bias is not None else []) + [_rows(tm, d), _full((1, d))]
    args = [a, w] + ([bias] if bias is not None else []) + [x, g]
    return pl.pallas_call(
        body, name=name, out_shape=[jax.ShapeDtypeStruct((s, d), F32)] * 2, grid=(s // tm,),
        in_specs=in_specs, out_specs=[_rows(tm, d)] * 2, compiler_params=_params("parallel"),
    )(*args)


FFN_SUB = 256


def _conv3_rows(z_ref, halo_ref, rb, sub, cs, first):
    zc = z_ref[rb * sub:(rb + 1) * sub, cs].astype(F32)
    if rb == 0:
        halo = halo_ref[:, cs].astype(F32) * jnp.where(first, 0.0, 1.0)
    else:
        halo = z_ref[rb * sub - 16:rb * sub, cs].astype(F32)[8:]
    z2, z1 = _conv3_taps(zc, halo)
    return z2, z1, zc


def _conv3_taps(z, halo):
    row = lax.broadcasted_iota(jnp.int32, (8, z.shape[1]), 0)
    h6, h7 = halo[6:7, :], halo[7:8, :]
    r1, r2 = pltpu.roll(z, 1, 0), pltpu.roll(z, 2, 0)
    z1 = jnp.concatenate([jnp.where(row == 0, h7, r1[0:8]), r1[8:]], axis=0)
    z2 = jnp.concatenate([jnp.where(row == 0, h6, jnp.where(row == 1, h7, r2[0:8])), r2[8:]], axis=0)
    return z2, z1


def _ffn_cols(f):
    return _tile(f)


def _lane_chunks(width, fn):
    def step(k, carry):
        fn(pl.ds(pl.multiple_of(k * 128, 128), 128))
        return carry

    lax.fori_loop(0, width // 128, step, 0)


def _ffn_act(z, w_dw, b_dw, tm=1024):
    s, f2 = z.shape
    f = f2 // 2
    tm = min(tm, s)
    sub = min(FFN_SUB, tm)
    tc = _ffn_cols(f)
    nfc = f // tc

    def body(zu, zg, hu, hg, wu, wg, bu, bg, o_ref):
        first = pl.program_id(0) == 0

        def chunk(cs):
            for rb in range(tm // sub):
                def conv(z_ref, h_ref, w_ref, b_ref):
                    z2, z1, zc = _conv3_rows(z_ref, h_ref, rb, sub, cs, first)
                    return w_ref[0:1, cs] * z2 + w_ref[1:2, cs] * z1 + w_ref[2:3, cs] * zc + b_ref[:, cs]

                up, gate = conv(zu, hu, wu, bu), conv(zg, hg, wg, bg)
                o_ref[rb * sub:(rb + 1) * sub, cs] = (gate * _sigmoid(gate) * up).astype(BF16)

        _lane_chunks(tc, chunk)

    hb = tm // 8
    tile = lambda off: pl.BlockSpec((tm, tc), lambda i, j: (i, off + j))
    halo = lambda off: pl.BlockSpec((8, tc), lambda i, j: (jnp.maximum(i * hb - 1, 0), off + j))
    prm = lambda rows, off: pl.BlockSpec((rows, tc), lambda i, j: (0, off + j))
    return pl.pallas_call(
        body, name="ffn_act", out_shape=jax.ShapeDtypeStruct((s, f), BF16), grid=(s // tm, nfc),
        in_specs=[tile(0), tile(nfc), halo(0), halo(nfc), prm(FFN_CONV, 0), prm(FFN_CONV, nfc), prm(1, 0), prm(1, nfc)],
        out_specs=pl.BlockSpec((tm, tc), lambda i, j: (i, j)), compiler_params=_params("parallel", "parallel"),
    )(z, z, z, z, w_dw, w_dw, b_dw, b_dw)


def _shifted_planes(ext_ref):
    rows = ext_ref.shape[1]
    for s in range(1, 8):
        ext_ref[s, 0:rows - 8, :] = ext_ref[0, s:s + rows - 8, :]


def _window(ext_ref, off, tm, cs):
    s = off % 8
    return ext_ref[s, off - s:off - s + tm, cs]


def _conv_taps(ext_ref, w_ref, offs, tm, out_ref):
    def chunk(cs):
        acc = w_ref[0:1, cs] * _window(ext_ref, offs[0], tm, cs)
        for j in range(1, len(offs)):
            acc = acc + w_ref[j:j + 1, cs] * _window(ext_ref, offs[j], tm, cs)
        out_ref[:, cs] = acc

    _lane_chunks(out_ref.shape[1], chunk)


def _glu_planes(ag_ref, halo_ref, ext_ref, first, c):
    hal = halo_ref[...].astype(F32)
    ext_ref[0, 0:CONV_HALO, :] = hal[:, :c] * _sigmoid(hal[:, c:]) * jnp.where(first, 0.0, 1.0)
    ag = ag_ref[...].astype(F32)
    ext_ref[0, CONV_HALO:, :] = ag[:, :c] * _sigmoid(ag[:, c:])
    _shifted_planes(ext_ref)


def _layernorm_stats(u1):
    mu = jnp.mean(u1, axis=-1, keepdims=True)
    cen = u1 - mu
    rstd = lax.rsqrt(jnp.mean(cen * cen, axis=-1, keepdims=True) + EPS)
    return cen * rstd, rstd


def _conv_mid(ag, w_dw, b_dw, ln_g, ln_b, tm=512):
    s, c2 = ag.shape
    c = c2 // 2
    tm = min(tm, s)

    def body(ag_ref, halo_ref, w_ref, b_ref, g_ref, bb_ref, o_ref, u1_ref, ext_ref):
        _glu_planes(ag_ref, halo_ref, ext_ref, pl.program_id(0) == 0, c)
        base = CONV_HALO - (CONV_KERNEL - 1)
        _conv_taps(ext_ref, w_ref, [base + j for j in range(CONV_KERNEL)], tm, u1_ref)
        xh, _ = _layernorm_stats(u1_ref[...] + b_ref[...])
        u2 = xh * g_ref[...] + bb_ref[...]
        o_ref[...] = (u2 * _sigmoid(u2)).astype(BF16)

    hb = tm // CONV_HALO
    return pl.pallas_call(
        body, name="conv_mid", out_shape=[jax.ShapeDtypeStruct((s, c), BF16), jax.ShapeDtypeStruct((s, c), F32)], grid=(s // tm,),
        in_specs=[_rows(tm, c2), pl.BlockSpec((CONV_HALO, c2), lambda i: (jnp.maximum(i * hb - 1, 0), 0)),
                  _full((CONV_KERNEL, c)), _full((1, c)), _full((1, c)), _full((1, c))],
        out_specs=[_rows(tm, c), _rows(tm, c)], scratch_shapes=[pltpu.VMEM((8, CONV_HALO + tm, c), F32)],
        compiler_params=_params("arbitrary"),
    )(ag, ag, w_dw, b_dw, ln_g, ln_b)


def _loss_grad(xo, target, tm=1024):
    s, d = xo.shape
    tm = min(tm, s)

    def body(x_ref, t_ref, dx_ref, loss_ref):
        @pl.when(pl.program_id(0) == 0)
        def _():
            loss_ref[...] = jnp.zeros_like(loss_ref)

        err = x_ref[...] - t_ref[...]
        dx_ref[...] = err * (1.0 / d)
        loss_ref[...] += 0.5 * jnp.sum(jnp.mean(err * err, axis=-1, keepdims=True))

    return pl.pallas_call(
        body, name="loss_grad", out_shape=[jax.ShapeDtypeStruct((s, d), F32), jax.ShapeDtypeStruct((1, 128), F32)],
        grid=(s // tm,), in_specs=[_rows(tm, d)] * 2, out_specs=[_rows(tm, d), _full((1, 128))],
        compiler_params=_params("arbitrary"),
    )(xo, target)


def _postnorm_bwd(y, g, dxo, *, name, with_bias_grad=False, tm=1024):
    s, d = y.shape
    tm = min(tm, s)

    def body(y_ref, g_ref, dx_ref, dy_ref, dg_ref, *rest):
        @pl.when(pl.program_id(0) == 0)
        def _():
            dg_ref[...] = jnp.zeros_like(dg_ref)
            for r_ in rest:
                r_[...] = jnp.zeros_like(r_)

        yv, dxo_v = y_ref[...], dx_ref[...]
        r = lax.rsqrt(jnp.mean(yv * yv, axis=-1, keepdims=True) + EPS)
        yh = yv * r
        dyh = dxo_v * g_ref[...]
        dy = r * (dyh - yh * jnp.mean(dyh * yh, axis=-1, keepdims=True))
        dy_ref[...] = dy.astype(BF16)
        dg_ref[...] += jnp.sum(dxo_v * yh, axis=0, keepdims=True)
        for r_ in rest:
            r_[...] += jnp.sum(dy, axis=0, keepdims=True)

    nacc = 2 if with_bias_grad else 1
    return pl.pallas_call(
        body, name=name, out_shape=[jax.ShapeDtypeStruct((s, d), BF16)] + [jax.ShapeDtypeStruct((1, d), F32)] * nacc,
        grid=(s // tm,), in_specs=[_rows(tm, d), _full((1, d)), _rows(tm, d)],
        out_specs=[_rows(tm, d)] + [_full((1, d))] * nacc, compiler_params=_params("arbitrary"),
    )(y, g, dxo)


def _matmul(gmat, w, *, name, out_dtype, transposed_w, tm=512):
    s, k = gmat.shape
    n = w.shape[0] if transposed_w else w.shape[1]
    tm = min(tm, s)

    def body(g_ref, w_ref, o_ref):
        if transposed_w:
            acc = lax.dot_general(g_ref[...], w_ref[...], NT, preferred_element_type=F32)
        else:
            acc = jnp.dot(g_ref[...], w_ref[...], preferred_element_type=F32)
        o_ref[...] = acc.astype(out_dtype)

    return pl.pallas_call(
        body, name=name, out_shape=jax.ShapeDtypeStruct((s, n), out_dtype), grid=(s // tm,),
        in_specs=[_rows(tm, k), _full(w.shape)], out_specs=_rows(tm, n), compiler_params=_params("parallel"),
    )(gmat, w)


def _matmul_prenorm_bwd(pieces, wt, x, g, dres, *, name, tm=256):
    s, d = x.shape
    tm = min(tm, s)
    np_ = len(pieces)

    def body(*refs):
        p_refs, w_refs = refs[:np_], refs[np_:2 * np_]
        x_ref, g_ref, r_ref, dx_ref, dg_ref = refs[2 * np_:]

        @pl.when(pl.program_id(0) == 0)
        def _():
            dg_ref[...] = jnp.zeros_like(dg_ref)

        dh = None
        for p_ref, w_ref in zip(p_refs, w_refs):
            t = jnp.dot(p_ref[...], w_ref[...], preferred_element_type=F32)
            dh = t if dh is None else dh + t
        xv = x_ref[...]
        r = lax.rsqrt(jnp.mean(xv * xv, axis=-1, keepdims=True) + EPS)
        xh = xv * r
        dyh = dh * g_ref[...]
        dx_ref[...] = r_ref[...] + r * (dyh - xh * jnp.mean(dyh * xh, axis=-1, keepdims=True))
        dg_ref[...] += jnp.sum(dh * xh, axis=0, keepdims=True)

    in_specs = []
    for _, c0, kc, _ in pieces:
        assert c0 % kc == 0
        in_specs.append(pl.BlockSpec((tm, kc), lambda i, _b=c0 // kc: (i, _b)))
    for _, _, kc, r0 in pieces:
        assert r0 % kc == 0
        in_specs.append(pl.BlockSpec((kc, d), lambda i, _b=r0 // kc: (_b, 0)))
    in_specs += [_rows(tm, d), _full((1, d)), _rows(tm, d)]
    return pl.pallas_call(
        body, name=name, out_shape=[jax.ShapeDtypeStruct((s, d), F32), jax.ShapeDtypeStruct((1, d), F32)],
        grid=(s // tm,), in_specs=in_specs, out_specs=[_rows(tm, d), _full((1, d))],
        compiler_params=_params("arbitrary"),
    )(*[p[0] for p in pieces], *[wt] * np_, x, g, dres)


def _weight_grad(a, gmat, *, name, a_col0=0, ka=None, out=None, out_shape=None, layer=0, row0=0, ts=2048):
    s = a.shape[0]
    ka = a.shape[1] if ka is None else ka
    n = gmat.shape[1]
    ts = min(ts, s)
    tka = _tile(ka, a_col0, row0)
    shape = out.shape if out is not None else out_shape
    nsteps = s // ts

    def body(a_ref, g_ref, *rest):
        o_ref, acc_ref = rest[-2:]
        i = pl.program_id(1)

        @pl.when(i == 0)
        def _():
            acc_ref[...] = jnp.zeros_like(acc_ref)

        acc_ref[...] += lax.dot_general(a_ref[...], g_ref[...], TN, preferred_element_type=F32)

        @pl.when(i == nsteps - 1)
        def _():
            o_ref[...] = acc_ref[...].astype(BF16)

    in_specs = [pl.BlockSpec((ts, tka), lambda k, i: (i, a_col0 // tka + k)), pl.BlockSpec((ts, n), lambda k, i: (i, 0))]
    args = [a, gmat]
    aliases = {}
    if out is not None:
        in_specs.append(ANY)
        args.append(out)
        aliases = {2: 0}
    return pl.pallas_call(
        body, name=name, out_shape=jax.ShapeDtypeStruct(shape, BF16), grid=(ka // tka, nsteps), in_specs=in_specs,
        out_specs=pl.BlockSpec((None, tka, n), lambda k, i: (layer, row0 // tka + k, 0)),
        scratch_shapes=[pltpu.VMEM((tka, n), F32)],
        input_output_aliases=aliases, compiler_params=_params("parallel", "arbitrary"),
    )(*args)


def _ffn_act_bwd(z, dact, w_dw, b_dw, tm=512):
    s, f2 = z.shape
    f = f2 // 2
    tm = min(tm, s)
    sub = min(FFN_SUB // 2, tm)
    tc = _ffn_cols(f)
    nfc = f // tc

    def body(zu, zg, hu, hg, wu, wg, bu, bg, da_ref, du_ref, dgt_ref, dbu_ref, dbg_ref, dwu_ref, dwg_ref):
        i = pl.program_id(1)

        @pl.when(i == 0)
        def _():
            for r_ in (dbu_ref, dbg_ref, dwu_ref, dwg_ref):
                r_[...] = jnp.zeros_like(r_)

        def chunk(cs):
            for rb in range(tm // sub):
                rows = slice(rb * sub, (rb + 1) * sub)

                def conv(z_ref, h_ref, w_ref, b_ref):
                    taps = _conv3_rows(z_ref, h_ref, rb, sub, cs, i == 0)
                    return taps, w_ref[0:1, cs] * taps[0] + w_ref[1:2, cs] * taps[1] + w_ref[2:3, cs] * taps[2] + b_ref[:, cs]

                taps_u, up = conv(zu, hu, wu, bu)
                taps_g, gate = conv(zg, hg, wg, bg)
                da = da_ref[rows, cs].astype(F32)
                sg = _sigmoid(gate)
                d_up = da * (gate * sg)
                d_gate = da * up * (sg * (1.0 + gate * (1.0 - sg)))
                du_ref[rows, cs] = d_up.astype(BF16)
                dgt_ref[rows, cs] = d_gate.astype(BF16)
                for dv, taps, db_ref, dw_ref in ((d_up, taps_u, dbu_ref, dwu_ref), (d_gate, taps_g, dbg_ref, dwg_ref)):
                    db_ref[:, cs] += jnp.sum(dv, axis=0, keepdims=True)
                    for k_, tap in enumerate(taps):
                        dw_ref[k_:k_ + 1, cs] += jnp.sum(dv * tap, axis=0, keepdims=True)

        _lane_chunks(tc, chunk)

    hb = tm // 8
    tile = lambda off: pl.BlockSpec((tm, tc), lambda j, i: (i, off + j))
    halo = lambda off: pl.BlockSpec((8, tc), lambda j, i: (jnp.maximum(i * hb - 1, 0), off + j))
    prm = lambda rows, off: pl.BlockSpec((rows, tc), lambda j, i: (0, off + j))
    acc = lambda rows: pl.BlockSpec((rows, tc), lambda j, i: (0, j))
    return pl.pallas_call(
        body, name="ffn_act_bwd",
        out_shape=[jax.ShapeDtypeStruct((s, f), BF16)] * 2 + [jax.ShapeDtypeStruct((1, f), F32)] * 2
        + [jax.ShapeDtypeStruct((FFN_CONV, f), F32)] * 2,
        grid=(nfc, s // tm),
        in_specs=[tile(0), tile(nfc), halo(0), halo(nfc), prm(FFN_CONV, 0), prm(FFN_CONV, nfc), prm(1, 0), prm(1, nfc), tile(0)],
        out_specs=[tile(0), tile(0), acc(1), acc(1), acc(FFN_CONV), acc(FFN_CONV)],
        compiler_params=_params("parallel", "arbitrary"),
    )(z, z, z, z, w_dw, w_dw, b_dw, b_dw, dact)


def _conv3_transpose(dug, w_dw, col0, tm=1024):
    s, f = dug.shape
    tm = min(tm, s)
    sub = min(FFN_SUB, tm)
    nsub = tm // sub
    tc = _ffn_cols(f)
    nfc = f // tc
    nrow = s // tm
    off = col0 // tc

    def body(d_ref, n_ref, w_ref, o_ref):
        keep_next = jnp.where(pl.program_id(0) == nrow - 1, 0.0, 1.0)

        def chunk(cs):
            for rb in range(nsub):
                rows = slice(rb * sub, (rb + 1) * sub)
                dv = d_ref[rows, cs].astype(F32)
                if rb == nsub - 1:
                    nxt = n_ref[:, cs].astype(F32) * keep_next
                else:
                    nxt = d_ref[(rb + 1) * sub:(rb + 1) * sub + 16, cs].astype(F32)[:8]
                n0, n1 = nxt[0:1, :], nxt[1:2, :]
                row = lax.broadcasted_iota(jnp.int32, (8, dv.shape[1]), 0)
                r1, r2 = pltpu.roll(dv, sub - 1, 0), pltpu.roll(dv, sub - 2, 0)
                d1 = jnp.concatenate([r1[:sub - 8], jnp.where(row == 7, n0, r1[sub - 8:])], axis=0)
                d2 = jnp.concatenate([r2[:sub - 8], jnp.where(row == 7, n1, jnp.where(row == 6, n0, r2[sub - 8:]))], axis=0)
                o_ref[rows, cs] = (w_ref[2:3, cs] * dv + w_ref[1:2, cs] * d1 + w_ref[0:1, cs] * d2).astype(BF16)

        _lane_chunks(tc, chunk)

    hb = tm // 8
    return pl.pallas_call(
        body, name="conv3_transpose", out_shape=jax.ShapeDtypeStruct((s, f), BF16), grid=(nrow, nfc),
        in_specs=[pl.BlockSpec((tm, tc), lambda i, j: (i, j)),
                  pl.BlockSpec((8, tc), lambda i, j: (jnp.minimum((i + 1) * hb, s // 8 - 1), j)),
                  pl.BlockSpec((FFN_CONV, tc), lambda i, j: (0, off + j))],
        out_specs=pl.BlockSpec((tm, tc), lambda i, j: (i, j)), compiler_params=_params("parallel", "parallel"),
    )(dug, dug, w_dw)


def _conv_mid_bwd(ag, u1, du3, b_dw, ln_g, ln_b, tm=256):
    s, c2 = ag.shape
    c = c2 // 2
    tm = min(tm, s)

    def body(ag_ref, halo_ref, u1in_ref, du_ref, b_ref, g_ref, bb_ref, o_ref, dlg_ref, dlb_ref, db_ref, dw_ref, ext_ref, u1_ref):
        @pl.when(pl.program_id(0) == 0)
        def _():
            for r_ in (dlg_ref, dlb_ref, db_ref, dw_ref):
                r_[...] = jnp.zeros_like(r_)

        _glu_planes(ag_ref, halo_ref, ext_ref, pl.program_id(0) == 0, c)
        xh, rstd = _layernorm_stats(u1in_ref[...] + b_ref[...])
        u2 = xh * g_ref[...] + bb_ref[...]
        sg = _sigmoid(u2)
        du2 = du_ref[...] * (sg * (1.0 + u2 * (1.0 - sg)))
        dlg_ref[...] += jnp.sum(du2 * xh, axis=0, keepdims=True)
        dlb_ref[...] += jnp.sum(du2, axis=0, keepdims=True)
        dxh = du2 * g_ref[...]
        du1 = rstd * (dxh - jnp.mean(dxh, axis=-1, keepdims=True) - xh * jnp.mean(dxh * xh, axis=-1, keepdims=True))
        o_ref[...] = du1.astype(BF16)
        db_ref[...] += jnp.sum(du1, axis=0, keepdims=True)
        u1_ref[...] = du1
        base = CONV_HALO - (CONV_KERNEL - 1)

        def chunk(cs):
            dc = u1_ref[:, cs]
            for j in range(CONV_KERNEL):
                dw_ref[j:j + 1, cs] += jnp.sum(dc * _window(ext_ref, base + j, tm, cs), axis=0, keepdims=True)

        _lane_chunks(c, chunk)

    hb = tm // CONV_HALO
    vec = _full((1, c))
    return pl.pallas_call(
        body, name="conv_mid_bwd",
        out_shape=[jax.ShapeDtypeStruct((s, c), BF16)] + [jax.ShapeDtypeStruct((1, c), F32)] * 3
        + [jax.ShapeDtypeStruct((CONV_HALO, c), F32)],
        grid=(s // tm,),
        in_specs=[_rows(tm, c2), pl.BlockSpec((CONV_HALO, c2), lambda i: (jnp.maximum(i * hb - 1, 0), 0)), _rows(tm, c),
                  _rows(tm, c), vec, vec, vec],
        out_specs=[_rows(tm, c), vec, vec, vec, _full((CONV_HALO, c))],
        scratch_shapes=[pltpu.VMEM((8, CONV_HALO + tm, c), F32), pltpu.VMEM((tm, c), F32)],
        compiler_params=_params("arbitrary"),
    )(ag, ag, u1, du3, b_dw, ln_g, ln_b)


def _glu_conv_bwd(du1, ag, w_dw, tm=512):
    s, c = du1.shape
    tm = min(tm, s)
    nrow = s // tm

    def body(d_ref, n_ref, ag_ref, w_ref, o_ref, db_ref, ext_ref, du0_ref):
        @pl.when(pl.program_id(0) == 0)
        def _():
            db_ref[...] = jnp.zeros_like(db_ref)

        ext_ref[0, 0:tm, :] = d_ref[...].astype(F32)
        ext_ref[0, tm:, :] = n_ref[...].astype(F32) * jnp.where(pl.program_id(0) == nrow - 1, 0.0, 1.0)
        _shifted_planes(ext_ref)
        top = CONV_KERNEL - 1
        _conv_taps(ext_ref, w_ref, [top - j for j in range(CONV_KERNEL)], tm, du0_ref)
        du0 = du0_ref[...]
        ag = ag_ref[...].astype(F32)
        a, gt = ag[:, :c], ag[:, c:]
        sg = _sigmoid(gt)
        da = du0 * sg
        dgt = du0 * a * (sg * (1.0 - sg))
        o_ref[:, :c] = da.astype(BF16)
        o_ref[:, c:] = dgt.astype(BF16)
        db_ref[:, :c] += jnp.sum(da, axis=0, keepdims=True)
        db_ref[:, c:] += jnp.sum(dgt, axis=0, keepdims=True)

    hb = tm // CONV_HALO
    return pl.pallas_call(
        body, name="glu_conv_bwd",
        out_shape=[jax.ShapeDtypeStruct((s, 2 * c), BF16), jax.ShapeDtypeStruct((1, 2 * c), F32)], grid=(nrow,),
        in_specs=[_rows(tm, c), pl.BlockSpec((CONV_HALO, c), lambda i: (jnp.minimum((i + 1) * hb, s // CONV_HALO - 1), 0)),
                  _rows(tm, 2 * c), _full((CONV_KERNEL, c))],
        out_specs=[_rows(tm, 2 * c), _full((1, 2 * c))],
        scratch_shapes=[pltpu.VMEM((8, tm + CONV_HALO, c), F32), pltpu.VMEM((tm, c), F32)],
        compiler_params=_params("arbitrary"),
    )(du1, du1, ag, w_dw)


def _head_rows(v, mask):
    return jnp.max(jnp.where(mask, v, -jnp.inf), axis=-1, keepdims=True)


def _attn_bwd(qv, dmix, mixed, lse, rope, grp, dil, ties=()):
    l = qv.shape[0]
    s = l * dil
    nb = l // SPAN
    view = lambda t: t.reshape(l, dil * t.shape[1])
    scale = HEAD_DIM ** -0.5
    gw = GROUP_WIDTH

    def body(*refs):
        q_ref, kp_ref, kc_ref, vp_ref, vc_ref, do_ref, mx_ref, l_ref, tab_ref, tabp_ref = refs[:10]
        dq_ref, dkv_ref, carry_ref = refs[-3:]
        b = pl.program_id(1)

        @pl.when(b < nb)
        def _():
            valid = _band_mask(b)
            masks, keep = _head_masks()
            for p in range(gw // 128):
                sl = slice(p * 128, (p + 1) * 128)
                sl_v = slice(gw + p * 128, gw + (p + 1) * 128)
                qp, dop = q_ref[:, sl], do_ref[:, sl]
                kk = jnp.concatenate([kp_ref[:, sl], kc_ref[:, sl]], axis=0)
                vv = jnp.concatenate([vp_ref[:, sl], vc_ref[:, sl]], axis=0)
                prod = dop.astype(F32) * mx_ref[:, sl].astype(F32)
                lsep = l_ref[:, sl]
                q2 = jnp.concatenate([qp * keep[0], qp * keep[1]], axis=0)
                do2 = jnp.concatenate([dop * keep[0], dop * keep[1]], axis=0)
                lse2 = jnp.concatenate([_head_rows(lsep, masks[h]) for h in range(2)], axis=0)
                dbar2 = jnp.concatenate([jnp.sum(jnp.where(masks[h], prod, 0.0), axis=-1, keepdims=True) for h in range(2)], axis=0)
                sc = lax.dot_general(q2, kk, NT, preferred_element_type=F32) * scale
                pe = jnp.where(valid, jnp.exp(sc - lse2), 0.0)
                dp = lax.dot_general(do2, vv, NT, preferred_element_type=F32)
                ds = (pe * (dp - dbar2) * scale).astype(BF16)
                dq2 = jnp.dot(ds, kk, preferred_element_type=F32)
                dq = jnp.where(masks[0], dq2[:SPAN], dq2[SPAN:])
                dq_ref[:, sl] = _rope_transpose(dq, tab_ref[...]).astype(BF16)
                dk = lax.dot_general(ds, q2, TN, preferred_element_type=F32)
                dv = lax.dot_general(pe.astype(BF16), do2, TN, preferred_element_type=F32)

                @pl.when(b > 0)
                def _():
                    dk_prev = carry_ref[:, sl] + dk[:SPAN]
                    dkv_ref[:, sl] = _rope_transpose(dk_prev, tabp_ref[...]).astype(BF16)
                    dkv_ref[:, sl_v] = (carry_ref[:, sl_v] + dv[:SPAN]).astype(BF16)

                carry_ref[:, sl] = dk[SPAN:]
                carry_ref[:, sl_v] = dv[SPAN:]

        @pl.when(b == nb)
        def _():
            for p in range(gw // 128):
                sl = slice(p * 128, (p + 1) * 128)
                sl_v = slice(gw + p * 128, gw + (p + 1) * 128)
                dkv_ref[:, sl] = _rope_transpose(carry_ref[:, sl], tabp_ref[...]).astype(BF16)
                dkv_ref[:, sl_v] = carry_ref[:, sl_v].astype(BF16)

    blk = (SPAN, gw)
    cb = lambda b: jnp.minimum(b, nb - 1)
    cur = lambda t: pl.BlockSpec(blk, lambda r, b: (cb(b), r * 3 + t))
    prev = lambda t: pl.BlockSpec(blk, lambda r, b: (jnp.maximum(cb(b) - 1, 0), r * 3 + t))
    own = pl.BlockSpec(blk, lambda r, b: (cb(b), r))
    tab = pl.BlockSpec((SPAN, ROPE_COLS), lambda r, b: (cb(b), r))
    tab_prev = pl.BlockSpec((SPAN, ROPE_COLS), lambda r, b: (jnp.maximum(b - 1, 0), r))
    dq, dkv = pl.pallas_call(
        body, name=f"attn_bwd_g{grp}",
        out_shape=[jax.ShapeDtypeStruct((l, dil * gw), BF16), jax.ShapeDtypeStruct((l, dil * 2 * gw), BF16)],
        grid=(dil, nb + 1),
        in_specs=[cur(0), prev(1), cur(1), prev(2), cur(2), own, own, own, tab, tab_prev] + [ANY] * len(ties),
        out_specs=[own, pl.BlockSpec((SPAN, 2 * gw), lambda r, b: (jnp.maximum(b - 1, 0), r))],
        scratch_shapes=[pltpu.VMEM((SPAN, 2 * gw), F32)], compiler_params=_params("parallel", "arbitrary"),
    )(qv, qv, qv, qv, qv, view(dmix), view(mixed), view(lse), view(rope), view(rope), *ties)
    return dq.reshape(s, gw), dkv.reshape(s, 2 * gw)


def _rope_freq_row():
    half = ROT_DIM // 2
    inv = (ROPE_THETA ** (-np.arange(half, dtype=np.float32) / half)).astype(np.float32)
    row = np.zeros((1, 128), np.float32)
    for head in range(128 // HEAD_DIM):
        row[0, head * HEAD_DIM:head * HEAD_DIM + half] = inv
        row[0, head * HEAD_DIM + half:head * HEAD_DIM + ROT_DIM] = inv
    return jnp.asarray(row)


def _ffn_fwd(x, g_pre, g_post, w_up_t, w_dw, b_dw, w_down):
    h, z = _norm_matmul(x, g_pre, w_up_t, tn=_tile(w_up_t.shape[0]), name="ffn_up")
    act = _ffn_act(z, w_dw, b_dw)
    y, xo = _matmul_resnorm(act, w_down, x, g_post, name="ffn_down")
    return xo, (x, h, z, act, y)


def _ffn_bwd(saved, dxo, g_pre, g_post, w_up_t, w_dw, b_dw, w_down):
    x, h, z, act, y = saved
    f = act.shape[1]
    d = x.shape[1]
    dy, dg_post = _postnorm_bwd(y, g_post, dxo, name="ffn_post_bwd")
    dact = _matmul(dy, w_down, name="ffn_dact", out_dtype=BF16, transposed_w=True)
    d_down = _weight_grad(act, dy, name="ffn_dw_down", out_shape=(1, f, d))
    dug_u, dug_g, db_u, db_g, dwd_u, dwd_g = _ffn_act_bwd(z, dact, w_dw, b_dw)
    dz_u = _conv3_transpose(dug_u, w_dw, 0)
    dz_g = _conv3_transpose(dug_g, w_dw, f)
    dx, dg_pre = _matmul_prenorm_bwd([(dz_u, 0, f, 0), (dz_g, 0, f, f)], w_up_t, x, g_pre, dxo, name="ffn_dx")
    d_up_t = _weight_grad(dz_u, h, name="ffn_dw_up", out_shape=(1, 2 * f, d))
    d_up_t = _weight_grad(dz_g, h, name="ffn_dw_up", out=d_up_t, row0=f)
    grads = dict(w_dw=jnp.concatenate([dwd_u, dwd_g], axis=1), b_dw=jnp.concatenate([db_u, db_g], axis=1),
                 g_pre=dg_pre, g_post=dg_post)
    return dx, grads, d_up_t, d_down


def _local_step(x, pos_col, target, p, tie=None, late_weights=None, exchange=None):
    ng = p["norm_g"]
    row = lambda r: ng[r:r + 1]
    freq = _rope_freq_row()
    rope = _rope_tables(pos_col, freq if tie is None else freq + tie[0:1])
    d = x.shape[1]

    h0, *qkv = _qkv_proj(x, row(0), p["w_qkv_t"], rope)
    os_, ls_ = zip(*[_attn_fwd(qkv[g_], g_, d_) for g_, d_ in enumerate(DILATIONS)])
    y_a, x1, mixed, lse = _mix_wo(os_, ls_, p["w_o_t"], x, row(1))
    if late_weights is not None:
        p = {**p, **late_weights(x1)}
    x2, ffn0 = _ffn_fwd(x1, row(2), row(3), p["w_up_t"][0], p["ffn_w_dw"][0], p["ffn_b_dw"][0], p["w_down"][0])
    h1, ag = _norm_matmul(x2, row(4), p["w_pw1_t"], tn=_tile(p["w_pw1_t"].shape[0]), name="conv_pw1", bias=p["b_pw1"])
    u3, u1 = _conv_mid(ag, p["conv_w_dw"], p["conv_b_dw"], p["ln_g"], p["ln_b"])
    y_c, x3 = _matmul_resnorm(u3, p["w_pw2"], x2, row(5), name="conv_pw2", bias=p["b_pw2"])
    x4, ffn1 = _ffn_fwd(x3, row(6), row(7), p["w_up_t"][1], p["ffn_w_dw"][1], p["ffn_b_dw"][1], p["w_down"][1])
    dx4, loss = _loss_grad(x4, target)

    big = [BF16, BF16]

    def tied(r, *tokens):
        tokens = [t for t in tokens if t is not None]
        return row(r) if not tokens else row(r) + jnp.tile(sum(tokens)[0:1], (1, d // 128))

    dx3, gf1, d_up1, d_down1 = _ffn_bwd(ffn1, dx4, row(6), row(7), p["w_up_t"][1], p["ffn_w_dw"][1], p["ffn_b_dw"][1],
                                        p["w_down"][1])
    t0 = exchange.submit("ffn1", [d_up1, d_down1], big) if exchange else None
    dy_c, dg5, db_pw2 = _postnorm_bwd(y_c, tied(5, t0), dx3, name="conv_post_bwd", with_bias_grad=True)
    du3 = _matmul(dy_c, p["w_pw2"], name="conv_du3", out_dtype=F32, transposed_w=True)
    d_wpw2 = _weight_grad(u3, dy_c, name="conv_dw_pw2", out_shape=(1, u3.shape[1], d))
    du1, d_lng, d_lnb, d_cbdw, d_cwdw = _conv_mid_bwd(ag, u1, du3, p["conv_b_dw"], p["ln_g"], p["ln_b"])
    dag, db_pw1 = _glu_conv_bwd(du1, ag, p["conv_w_dw"])
    dx2, dg4 = _matmul_prenorm_bwd([(dag, 0, dag.shape[1], 0)], p["w_pw1_t"], x2, row(4), dx3, name="conv_dx")
    d_wpw1_t = _weight_grad(dag, h1, name="conv_dw_pw1", out_shape=(1, dag.shape[1], d))
    t0 = exchange.advance(dx2) if exchange else None
    t1 = exchange.submit("conv", [d_wpw1_t, d_wpw2], big) if exchange else None
    dx1, gf0, d_up0, d_down0 = _ffn_bwd(ffn0, dx2, row(2), tied(3, t0, t1), p["w_up_t"][0], p["ffn_w_dw"][0], p["ffn_b_dw"][0],
                                        p["w_down"][0])
    t0 = exchange.advance(dx1) if exchange else None
    t1 = exchange.submit("ffn0", [d_up0, d_down0], big) if exchange else None
    dy_a, dg1 = _postnorm_bwd(y_a, tied(1, t0, t1), dx1, name="attn_post_bwd")
    dmix = _matmul(dy_a, p["w_o_t"], name="attn_dmix", out_dtype=BF16, transposed_w=False)
    d_wo_t = _weight_grad(dy_a, mixed, name="attn_dw_o", out_shape=(1, d, GROUP_WIDTH))
    pieces, d_wqkv_t = [], None
    for g_, d_ in enumerate(DILATIONS):
        tok = exchange.advance(dkv) if exchange and g_ > 0 else None
        dq, dkv = _attn_bwd(qkv[g_], dmix, mixed, lse, rope, g_, d_, ties=() if tok is None else (tok,))
        for t, (arr, c0) in enumerate(((dq, 0), (dkv, 0), (dkv, GROUP_WIDTH))):
            r0 = (3 * t + g_) * GROUP_WIDTH
            pieces.append((arr, c0, GROUP_WIDTH, r0))
            d_wqkv_t = _weight_grad(arr, h0, name="attn_dw_qkv", a_col0=c0, ka=GROUP_WIDTH, out=d_wqkv_t,
                                    out_shape=(1, p["w_qkv_t"].shape[0], d), row0=r0)
    t0 = exchange.advance(dkv) if exchange else None
    t1 = exchange.submit("attn", [d_wqkv_t, d_wo_t], big) if exchange else None
    t2 = exchange.advance(d_wqkv_t) if exchange else None
    grad_x, dg0 = _matmul_prenorm_bwd(pieces, p["w_qkv_t"], x, tied(0, t0, t1, t2), dx1, name="attn_dx")

    grads = dict(
        norm_g=jnp.concatenate([dg0, dg1, gf0["g_pre"], gf0["g_post"], dg4, dg5, gf1["g_pre"], gf1["g_post"]], axis=0),
        w_qkv_t=d_wqkv_t, w_o_t=d_wo_t, w_pw1_t=d_wpw1_t, b_pw1=db_pw1,
        conv_w_dw=d_cwdw[:CONV_KERNEL], conv_b_dw=d_cbdw, ln_g=d_lng, ln_b=d_lnb, w_pw2=d_wpw2, b_pw2=db_pw2,
        w_up_t=[d_up0, d_up1], ffn_w_dw=jnp.stack([gf0["w_dw"], gf1["w_dw"]]),
        ffn_b_dw=jnp.concatenate([gf0["b_dw"], gf1["b_dw"]], axis=0), w_down=[d_down0, d_down1])
    return loss, grad_x, grads


SMALL_AXIS = dict(norm_g=2, conv_b_pw1=1, conv_w_dw=2, conv_b_dw=1, conv_ln_g=1, conv_ln_b=1, conv_b_pw2=1, ffn_w_dw=2)
SMALL = tuple(SMALL_AXIS)
MATMUL_WEIGHTS = dict(attn_w_qkv=True, conv_w_pw1=True, ffn_w_up=True, conv_w_pw2=False, ffn_w_down=False)


def _pack(arrays, cols, row_multiple):
    flat = jnp.concatenate([a.reshape(-1) for a in arrays])
    rows = -(-flat.shape[0] // cols)
    rows = -(-rows // row_multiple) * row_multiple
    return jnp.pad(flat, (0, rows * cols - flat.shape[0])).reshape(rows, cols)


def _unpack(packed, shapes):
    flat = packed.reshape(packed.shape[:-2] + (-1,))
    out, off = [], 0
    for shp in shapes:
        n = math.prod(shp)
        out.append(flat[..., off:off + n].reshape(packed.shape[:-2] + tuple(shp)))
        off += n
    return out


def _join_shards(stacked, axis):
    moved = jnp.moveaxis(stacked, 0, axis)
    shp = moved.shape
    return moved.reshape(shp[:axis] + (shp[axis] * shp[axis + 1],) + shp[axis + 2:])


def _split_shards(whole, axis):
    shp = whole.shape
    cut = whole.reshape(shp[:axis] + (N_DEV, shp[axis] // N_DEV) + shp[axis + 1:])
    return jnp.moveaxis(cut, axis, 0)


def _row_shard(w, transposed):
    t = jnp.swapaxes(w, 1, 2) if transposed else w
    return t.astype(BF16).reshape(-1, t.shape[-1])


def kernel(x, positions, norm_g, attn_w_qkv, attn_w_o, conv_w_pw1, conv_b_pw1, conv_w_dw, conv_b_dw, conv_ln_g, conv_ln_b, conv_w_pw2, conv_b_pw2, ffn_w_up, ffn_w_dw, ffn_b_dw, ffn_w_down, loss_target, m_norm_g, m_attn_w_qkv, m_attn_w_o, m_conv_w_pw1, m_conv_b_pw1, m_conv_w_dw, m_conv_b_dw, m_conv_ln_g, m_conv_ln_b, m_conv_w_pw2, m_conv_b_pw2, m_ffn_w_up, m_ffn_w_dw, m_ffn_b_dw, m_ffn_w_down, v_norm_g, v_attn_w_qkv, v_attn_w_o, v_conv_w_pw1, v_conv_b_pw1, v_conv_w_dw, v_conv_b_dw, v_conv_ln_g, v_conv_ln_b, v_conv_w_pw2, v_conv_b_pw2, v_ffn_w_up, v_ffn_w_dw, v_ffn_b_dw, v_ffn_w_down):
    w = dict(norm_g=norm_g, attn_w_qkv=attn_w_qkv, attn_w_o=attn_w_o, conv_w_pw1=conv_w_pw1, conv_b_pw1=conv_b_pw1,
             conv_w_dw=conv_w_dw, conv_b_dw=conv_b_dw, conv_ln_g=conv_ln_g, conv_ln_b=conv_ln_b, conv_w_pw2=conv_w_pw2,
             conv_b_pw2=conv_b_pw2, ffn_w_up=ffn_w_up, ffn_w_dw=ffn_w_dw, ffn_w_down=ffn_w_down)
    m = dict(norm_g=m_norm_g, attn_w_qkv=m_attn_w_qkv, attn_w_o=m_attn_w_o, conv_w_pw1=m_conv_w_pw1, conv_b_pw1=m_conv_b_pw1,
             conv_w_dw=m_conv_w_dw, conv_b_dw=m_conv_b_dw, conv_ln_g=m_conv_ln_g, conv_ln_b=m_conv_ln_b, conv_w_pw2=m_conv_w_pw2,
             conv_b_pw2=m_conv_b_pw2, ffn_w_up=m_ffn_w_up, ffn_w_dw=m_ffn_w_dw, ffn_w_down=m_ffn_w_down)
    v = dict(norm_g=v_norm_g, attn_w_qkv=v_attn_w_qkv, attn_w_o=v_attn_w_o, conv_w_pw1=v_conv_w_pw1, conv_b_pw1=v_conv_b_pw1,
             conv_w_dw=v_conv_w_dw, conv_b_dw=v_conv_b_dw, conv_ln_g=v_conv_ln_g, conv_ln_b=v_conv_ln_b, conv_w_pw2=v_conv_w_pw2,
             conv_b_pw2=v_conv_b_pw2, ffn_w_up=v_ffn_w_up, ffn_w_dw=v_ffn_w_dw, ffn_w_down=v_ffn_w_down)
    d = x.shape[-1]

    w_qkv_t, w_o_t, small = _all_gather([_row_shard(attn_w_qkv, True), _row_shard(attn_w_o, True),
                                         _pack([w[n] for n in SMALL], 128, 8)], "gather_first_weights")
    w_qkv_t, w_o_t = w_qkv_t.reshape(-1, d), w_o_t.reshape(d, -1)
    sm = {n: _join_shards(stacked, SMALL_AXIS[n])
          for n, stacked in zip(SMALL, _unpack(small, [w[n].shape for n in SMALL]))}
    late = {n: t for n, t in MATMUL_WEIGHTS.items() if n != "attn_w_qkv"}
    shares = [_row_shard(w[n], t) for n, t in late.items()]
    rows = [s_.shape[0] for s_ in shares]
    late_share = jnp.concatenate(shares, axis=0)
    send_sems, recv_sems, share_thru, land_thru, tie = _gather_start(late_share)
    me = 4 * lax.axis_index("x") + 2 * lax.axis_index("y") + lax.axis_index("c")

    def late_weights(after):
        big = _gather_wait(send_sems, recv_sems, share_thru, land_thru, after)
        big = lax.dynamic_update_slice(big, late_share[None], (me, 0, 0))
        whole, r0 = {}, 0
        for n, nr in zip(late, rows):
            layers = w[n].shape[0]
            seg = big[:, r0:r0 + nr].reshape(N_DEV, layers, nr // layers, d)
            whole[n] = [seg[:, l_].reshape(-1, d) for l_ in range(layers)]
            r0 += nr
        return dict(w_pw1_t=whole["conv_w_pw1"][0], w_pw2=whole["conv_w_pw2"][0], w_up_t=whole["ffn_w_up"],
                    w_down=whole["ffn_w_down"])

    p = dict(norm_g=sm["norm_g"].reshape(-1, d), w_qkv_t=w_qkv_t, w_o_t=w_o_t, b_pw1=sm["conv_b_pw1"],
             conv_w_dw=sm["conv_w_dw"][0], conv_b_dw=sm["conv_b_dw"], ln_g=sm["conv_ln_g"], ln_b=sm["conv_ln_b"],
             b_pw2=sm["conv_b_pw2"], ffn_w_dw=sm["ffn_w_dw"], ffn_b_dw=[ffn_b_dw[0:1], ffn_b_dw[1:2]])

    exchange = _GradExchange()
    loss, grad_x, g = _local_step(x[0], positions.reshape(-1, 1), loss_target[0], p, tie, late_weights, exchange)
    loss = lax.psum(loss[0, 0], ("x", "y", "c"))
    gsmall = dict(norm_g=g["norm_g"].reshape(norm_g.shape[0], 4, -1), conv_b_pw1=g["b_pw1"], conv_w_dw=g["conv_w_dw"][None],
                  conv_b_dw=g["conv_b_dw"], conv_ln_g=g["ln_g"], conv_ln_b=g["ln_b"], conv_b_pw2=g["b_pw2"], ffn_w_dw=g["ffn_w_dw"])
    small_contrib = jnp.concatenate([_split_shards(gsmall[n], SMALL_AXIS[n]).reshape(N_DEV, -1) for n in SMALL], axis=1)
    srows = small.shape[1]
    small_contrib = jnp.pad(small_contrib, ((0, 0), (0, srows * 128 - small_contrib.shape[1]))).reshape(1, N_DEV, srows, 128)
    small_sums = _rs_chips([_rs_pair_add(small_contrib, _rs_sibling([small_contrib])[0], exchange.core, F32)])[0]

    outs = {}

    def update(n, reduced):
        gsum = jnp.swapaxes(reduced, 1, 2) if n == "attn_w_o" or MATMUL_WEIGHTS.get(n) else reduced
        outs[n] = (gsum, *_adamw(gsum, w[n], m[n], v[n], "adamw"))

    (s_up1, s_down1), (s_pw1, s_pw2), (s_up0, s_down0) = exchange.results()[:3]
    update("conv_w_pw1", s_pw1)
    update("conv_w_pw2", s_pw2)
    update("ffn_w_up", jnp.concatenate([s_up0, s_up1], axis=0))
    update("ffn_w_down", jnp.concatenate([s_down0, s_down1], axis=0))
    sshapes = [w[n].shape for n in SMALL]
    souts = _sum_adamw(small_sums[0], *[_pack([t[n] for n in SMALL], 128, 8) for t in (w, m, v)], name="sum_adamw_small")
    for n, vals in zip(SMALL, zip(*[_unpack(o, sshapes) for o in souts])):
        outs[n] = vals
    bparts, = _all_gather([_pack([g["ffn_b_dw"]], 128, 8)], "gather_bias_grads")
    bouts = _sum_adamw(bparts, *[_pack([t], 128, 8) for t in (ffn_b_dw, m_ffn_b_dw, v_ffn_b_dw)], name="sum_adamw_bias")
    outs["ffn_b_dw"] = tuple(_unpack(o, [ffn_b_dw.shape])[0] for o in bouts)
    done = [outs[n][1][0, :8, :128] for n in ("conv_w_pw1", "conv_w_pw2", "ffn_w_up", "ffn_w_down")]
    exchange.advance(sum(done) + bouts[1][:8] + souts[1][:8])
    s_qkv, s_wo = exchange.results()[3]
    update("attn_w_qkv", s_qkv)
    update("attn_w_o", s_wo)

    order = ("norm_g", "attn_w_qkv", "attn_w_o", "conv_w_pw1", "conv_b_pw1", "conv_w_dw", "conv_b_dw", "conv_ln_g",
             "conv_ln_b", "conv_w_pw2", "conv_b_pw2", "ffn_w_up", "ffn_w_dw", "ffn_b_dw", "ffn_w_down")
    return (loss, grad_x[None], *[outs[n][0] for n in order], *[outs[n][1] for n in order],
            *[outs[n][2] for n in order], *[outs[n][3] for n in order])
```

```python
import math

import numpy as np
import jax
import jax.numpy as jnp
from jax import lax
from jax.experimental import pallas as pl
from jax.experimental.pallas import tpu as pltpu

F32 = jnp.float32
BF16 = jnp.bfloat16
EPS = 1e-6
N_DEV = 8
HEAD_DIM = 64
GROUP_WIDTH = 512
DILATIONS = (1, 4, 16)
SPAN = 128
ROT_DIM = 16
ROPE_THETA = 500000.0
CONV_KERNEL = 31
CONV_HALO = 32
FFN_CONV = 3
ADAM_LR, ADAM_B1, ADAM_B2, ADAM_EPS, ADAM_WD, ADAM_STEP = 0.001, 0.9, 0.999, 1e-08, 0.01, 10
VMEM_LIMIT_BYTES = 56 * 1024 * 1024
MESH = pl.DeviceIdType.MESH
ANY = pl.BlockSpec(memory_space=pl.ANY)
NT = (((1,), (1,)), ((), ()))
TN = (((0,), (0,)), ((), ()))


def _params(*sem):
    return pltpu.CompilerParams(dimension_semantics=sem, vmem_limit_bytes=VMEM_LIMIT_BYTES)


def _sigmoid(v):
    return pl.reciprocal(1.0 + jnp.exp(-v), approx=True)


def _full(shape):
    return pl.BlockSpec(shape, lambda *_: (0,) * len(shape))


def _rows(tm, width):
    return pl.BlockSpec((tm, width), lambda i, *_: (i, 0))


def _tile(n, *multiples_of):
    for t in (1408, 1024, 512, 384, 256, 128):
        if n % t == 0 and all(o % t == 0 for o in multiples_of):
            return t
    raise ValueError((n, multiples_of))


def _all_gather(shards, name):
    n = len(shards)

    def body(*refs):
        x_refs, out_refs, (send_sems, recv_sems, local_sems) = refs[:n], refs[n:2 * n], refs[2 * n:]
        x, y, c = lax.axis_index("x"), lax.axis_index("y"), lax.axis_index("c")
        me, sibling = (x, y, c), (x, y, 1 - c)
        chips = [(1 - x, y), (x, 1 - y), (1 - x, 1 - y)]

        def rows(w, px, py, pc):
            return out_refs[w].at[4 * px + 2 * py + pc]

        def copy(w, k, block, to, src=None):
            return pltpu.make_async_remote_copy(
                src_ref=rows(w, *block) if src is None else src, dst_ref=rows(w, *block),
                send_sem=send_sems.at[7 * w + k], recv_sem=recv_sems.at[7 * w + k], device_id=to, device_id_type=MESH)

        every = range(n)
        mine = [pltpu.make_async_copy(x_refs[w], rows(w, *me), local_sems.at[w]) for w in every]
        first = [copy(w, 0, me, sibling, src=x_refs[w]) for w in every]
        first += [copy(w, 1 + j, me, (*chip, c), src=x_refs[w]) for w in every for j, chip in enumerate(chips)]
        for cp in mine + first:
            cp.start()
        passed = []
        for j, chip in enumerate(chips):
            for w in every:
                copy(w, 1 + j, (*chip, c), me).wait_recv()
                passed.append(copy(w, 4 + j, (*chip, c), sibling))
                passed[-1].start()
        for w in every:
            copy(w, 0, sibling, me).wait_recv()
            for j, chip in enumerate(chips):
                copy(w, 4 + j, (*chip, 1 - c), me).wait_recv()
        for cp in first + passed:
            cp.wait_send()
        for cp in mine:
            cp.wait()

    return pl.pallas_call(
        body, name=name, out_shape=[jax.ShapeDtypeStruct((N_DEV,) + s_.shape, s_.dtype) for s_ in shards],
        in_specs=[ANY] * n, out_specs=[ANY] * n,
        scratch_shapes=[pltpu.SemaphoreType.DMA((7 * n,)), pltpu.SemaphoreType.DMA((7 * n,)), pltpu.SemaphoreType.DMA((n,))],
    )(*shards)


HBM = pl.BlockSpec(memory_space=pltpu.HBM)
SEM = pl.BlockSpec(memory_space=pltpu.SEMAPHORE)
SIDE_EFFECT = pltpu.CompilerParams(has_side_effects=pltpu.SideEffectType.DATAFLOW_SIDE_EFFECTING)


def _gather_start(shard):
    r, c_ = shard.shape

    def body(x_ref, land_ref, send_sems, recv_sems, x_thru, land_thru, token):
        x, y, c = lax.axis_index("x"), lax.axis_index("y"), lax.axis_index("c")
        me = 4 * x + 2 * y + c
        for k in range(1, N_DEV):
            peer = (1 - x if k & 4 else x, 1 - y if k & 2 else y, 1 - c if k & 1 else c)
            pltpu.make_async_remote_copy(src_ref=x_ref, dst_ref=land_ref.at[me], send_sem=send_sems.at[k - 1],
                                         recv_sem=recv_sems.at[k - 1], device_id=peer, device_id_type=MESH).start()
        token[...] = jnp.zeros_like(token)

    land = pltpu.with_memory_space_constraint(lax.empty((N_DEV, r, c_), shard.dtype), pltpu.HBM)
    return pl.pallas_call(
        body, name="gather_late_weights_start",
        out_shape=(pltpu.SemaphoreType.DMA((N_DEV - 1,)), pltpu.SemaphoreType.DMA((N_DEV - 1,)),
                   pltpu.HBM(shard.shape, shard.dtype), pltpu.HBM((N_DEV, r, c_), shard.dtype),
                   jax.ShapeDtypeStruct((8, 128), F32)),
        in_specs=(HBM, HBM), out_specs=(SEM, SEM, HBM, HBM, pl.BlockSpec(memory_space=pltpu.VMEM)),
        input_output_aliases={0: 2, 1: 3}, compiler_params=SIDE_EFFECT,
    )(pltpu.with_memory_space_constraint(shard, pltpu.HBM), land)


def _gather_wait(send_sems, recv_sems, shard_thru, land_thru, after):
    def body(x_ref, land_ref, send_sems, recv_sems, after_ref, x_dead, got_ref):
        x, y, c = lax.axis_index("x"), lax.axis_index("y"), lax.axis_index("c")
        for k in range(N_DEV - 1):
            copy = pltpu.make_async_remote_copy(src_ref=x_ref, dst_ref=land_ref.at[0], send_sem=send_sems.at[k],
                                                recv_sem=recv_sems.at[k], device_id=(x, y, c), device_id_type=MESH)
            copy.wait_send()
            copy.wait_recv()

    return pl.pallas_call(
        body, name="gather_late_weights_wait",
        out_shape=(pltpu.HBM(shard_thru.shape, shard_thru.dtype), pltpu.HBM(land_thru.shape, land_thru.dtype)),
        in_specs=(HBM, HBM, SEM, SEM, ANY), out_specs=(HBM, HBM), input_output_aliases={0: 0, 1: 1},
        compiler_params=SIDE_EFFECT,
    )(shard_thru, land_thru, send_sems, recv_sems, after)[1]


def _hbm(a):
    return pltpu.with_memory_space_constraint(a, pltpu.HBM)


def _exchange_start(name, arrays, lands, plan, ncopies):
    n = len(arrays)

    def body(*refs):
        send_sems, recv_sems, token = refs[2 * n], refs[2 * n + 1], refs[-1]
        x, y, c = lax.axis_index("x"), lax.axis_index("y"), lax.axis_index("c")
        for k, (src, dst, peer) in enumerate(plan(x, y, c, refs[:n], refs[n:2 * n])):
            pltpu.make_async_remote_copy(src_ref=src, dst_ref=dst, send_sem=send_sems.at[k], recv_sem=recv_sems.at[k],
                                         device_id=peer, device_id_type=MESH).start()
        token[...] = jnp.zeros_like(token)

    both = list(arrays) + list(lands)
    outs = pl.pallas_call(
        body, name=name,
        out_shape=(pltpu.SemaphoreType.DMA((ncopies,)), pltpu.SemaphoreType.DMA((ncopies,)),
                   *[pltpu.HBM(a.shape, a.dtype) for a in both], jax.ShapeDtypeStruct((8, 128), F32)),
        in_specs=(HBM,) * (2 * n), out_specs=(SEM, SEM) + (HBM,) * (2 * n) + (pl.BlockSpec(memory_space=pltpu.VMEM),),
        input_output_aliases={i: 2 + i for i in range(2 * n)}, compiler_params=SIDE_EFFECT,
    )(*[_hbm(a) for a in both])
    return outs[0], outs[1], list(outs[2:2 + n]), list(outs[2 + n:2 + 2 * n]), outs[-1]


def _exchange_wait(name, send_sems, recv_sems, arrays, lands, plan, after):
    n = len(arrays)

    def body(*refs):
        send_sems, recv_sems = refs[2 * n], refs[2 * n + 1]
        x, y, c = lax.axis_index("x"), lax.axis_index("y"), lax.axis_index("c")
        for k, (src, dst, peer) in enumerate(plan(x, y, c, refs[:n], refs[n:2 * n])):
            copy = pltpu.make_async_remote_copy(src_ref=src, dst_ref=dst, send_sem=send_sems.at[k], recv_sem=recv_sems.at[k],
                                                device_id=peer, device_id_type=MESH)
            copy.wait_send()
            copy.wait_recv()

    both = list(arrays) + list(lands)
    outs = pl.pallas_call(
        body, name=name, out_shape=tuple(pltpu.HBM(a.shape, a.dtype) for a in both),
        in_specs=(HBM,) * (2 * n) + (SEM, SEM, ANY), out_specs=(HBM,) * (2 * n),
        input_output_aliases={i: i for i in range(2 * n)}, compiler_params=SIDE_EFFECT,
    )(*both, send_sems, recv_sems, after)
    return list(outs[:n]), list(outs[n:])


def _sibling_plan(x, y, c, g_refs, land_refs):
    return [(g.at[:, 2 * q + (1 - c)], o.at[:, q], (x, y, 1 - c)) for g, o in zip(g_refs, land_refs) for q in range(4)]


def _chips_plan(x, y, c, p_refs, land_refs):
    chips = [(1 - x, y), (x, 1 - y), (1 - x, 1 - y)]
    return [(p_.at[:, 2 * qx + qy], o.at[:, 2 * x + y], (qx, qy, c)) for p_, o in zip(p_refs, land_refs) for qx, qy in chips]


class _GradExchange:
    def __init__(self):
        self.core = lax.axis_index("c").astype(jnp.int32).reshape(1)
        self.chip = 2 * lax.axis_index("x") + lax.axis_index("y")
        self.groups = []

    def submit(self, tag, arrays, dtypes):
        arrays = [a.reshape(a.shape[0], N_DEV, a.shape[1] // N_DEV, a.shape[2]) for a in arrays]
        lands = [lax.empty((a.shape[0], 4) + a.shape[2:], a.dtype) for a in arrays]
        send, recv, arrays, lands, token = _exchange_start(f"rs_pair_start_{tag}", arrays, lands, _sibling_plan, 4 * len(arrays))
        self.groups.append(dict(tag=tag, stage=1, sems=(send, recv), arrays=arrays, lands=lands, dtypes=dtypes))
        return token

    def advance(self, after):
        token = None
        for g in self.groups:
            if g["stage"] == 1:
                arrays, got = _exchange_wait(f"rs_pair_wait_{g['tag']}", *g["sems"], g["arrays"], g["lands"], _sibling_plan, after)
                parts = [_rs_pair_add(a, b, self.core, dt) for a, b, dt in zip(arrays, got, g["dtypes"])]
                lands = [lax.empty(p_.shape, p_.dtype) for p_ in parts]
                send, recv, parts, lands, tok = _exchange_start(f"rs_chip_start_{g['tag']}", parts, lands, _chips_plan, 3 * len(parts))
                g.update(stage=2, sems=(send, recv), arrays=parts, lands=lands)
                token = tok if token is None else token + tok
            elif g["stage"] == 2:
                parts, lands = _exchange_wait(f"rs_chip_wait_{g['tag']}", *g["sems"], g["arrays"], g["lands"], _chips_plan, after)
                sums = []
                for p_, land in zip(parts, lands):
                    l, _, r, c_ = p_.shape
                    own = lax.dynamic_slice(p_, (0, self.chip, 0, 0), (l, 1, r, c_))
                    sums.append(_sum_parts(lax.dynamic_update_slice(land, own, (0, self.chip, 0, 0)), "sum_chips"))
                g.update(stage=3, sums=sums)
        return token

    def results(self):
        return [g.get("sums") for g in self.groups]


def _with_rows(g, n):
    return jax.ShapeDtypeStruct((g.shape[0], n) + tuple(g.shape[2:]), g.dtype)


def _rs_sibling(gs):
    n = len(gs)

    def body(*refs):
        g_refs, o_refs, (send_sems, recv_sems) = refs[:n], refs[n:2 * n], refs[2 * n:]
        x, y, c = lax.axis_index("x"), lax.axis_index("y"), lax.axis_index("c")
        copies = [pltpu.make_async_remote_copy(
            src_ref=g_refs[w].at[:, 2 * q + (1 - c)], dst_ref=o_refs[w].at[:, q], send_sem=send_sems.at[4 * w + q],
            recv_sem=recv_sems.at[4 * w + q], device_id=(x, y, 1 - c), device_id_type=MESH)
            for w in range(n) for q in range(4)]
        for cp in copies:
            cp.start()
        for cp in copies:
            cp.wait_recv()
        for cp in copies:
            cp.wait_send()

    return pl.pallas_call(
        body, name="rs_sibling", out_shape=[_with_rows(g, 4) for g in gs],
        in_specs=[ANY] * n, out_specs=[ANY] * n,
        scratch_shapes=[pltpu.SemaphoreType.DMA((4 * n,)), pltpu.SemaphoreType.DMA((4 * n,))],
    )(*gs)


def _rs_pair_add(g, got, core, out_dtype):
    l, _, r, c_ = g.shape

    def body(core_ref, g_ref, got_ref, o_ref):
        o_ref[...] = (g_ref[...].astype(F32) + got_ref[...].astype(F32)).astype(out_dtype)

    blk = (None, None, r, c_)
    return pl.pallas_call(
        body, name="rs_pair_add", out_shape=jax.ShapeDtypeStruct((l, 4, r, c_), out_dtype),
        grid_spec=pltpu.PrefetchScalarGridSpec(
            num_scalar_prefetch=1, grid=(l, 4),
            in_specs=[pl.BlockSpec(blk, lambda i, q, core_ref: (i, 2 * q + core_ref[0], 0, 0)),
                      pl.BlockSpec(blk, lambda i, q, core_ref: (i, q, 0, 0))],
            out_specs=pl.BlockSpec(blk, lambda i, q, core_ref: (i, q, 0, 0))),
        compiler_params=_params("parallel", "parallel"),
    )(core, g, got)


def _rs_chips(parts):
    n = len(parts)

    def body(*refs):
        p_refs, o_refs, (send_sems, recv_sems, local_sems) = refs[:n], refs[n:2 * n], refs[2 * n:]
        x, y, c = lax.axis_index("x"), lax.axis_index("y"), lax.axis_index("c")
        my_chip = 2 * x + y
        chips = [(1 - x, y), (x, 1 - y), (1 - x, 1 - y)]
        local = [pltpu.make_async_copy(p_refs[w].at[:, my_chip], o_refs[w].at[:, my_chip], local_sems.at[w]) for w in range(n)]
        for cp in local:
            cp.start()
        copies = [pltpu.make_async_remote_copy(
            src_ref=p_refs[w].at[:, 2 * qx + qy], dst_ref=o_refs[w].at[:, my_chip], send_sem=send_sems.at[3 * w + k],
            recv_sem=recv_sems.at[3 * w + k], device_id=(qx, qy, c), device_id_type=MESH)
            for w in range(n) for k, (qx, qy) in enumerate(chips)]
        for cp in copies:
            cp.start()
        for cp in copies:
            cp.wait_recv()
        for cp in copies:
            cp.wait_send()
        for cp in local:
            cp.wait()

    return pl.pallas_call(
        body, name="rs_chips", out_shape=[jax.ShapeDtypeStruct(p.shape, p.dtype) for p in parts],
        in_specs=[ANY] * n, out_specs=[ANY] * n,
        scratch_shapes=[pltpu.SemaphoreType.DMA((3 * n,)), pltpu.SemaphoreType.DMA((3 * n,)), pltpu.SemaphoreType.DMA((n,))],
    )(*parts)


def _sum_parts(parts, name):
    l, n, r, c_ = parts.shape

    def body(p_ref, o_ref):
        g = p_ref[0].astype(F32)
        for s in range(1, n):
            g = g + p_ref[s].astype(F32)
        o_ref[...] = g

    return pl.pallas_call(
        body, name=name, out_shape=jax.ShapeDtypeStruct((l, r, c_), F32), grid=(l,),
        in_specs=[pl.BlockSpec((None, n, r, c_), lambda i: (i, 0, 0, 0))],
        out_specs=pl.BlockSpec((None, r, c_), lambda i: (i, 0, 0)), compiler_params=_params("parallel"),
    )(parts)


def _adamw_math(w, g, m, v):
    m = ADAM_B1 * m + (1.0 - ADAM_B1) * g
    v = ADAM_B2 * v + (1.0 - ADAM_B2) * (g * g)
    m_hat = m / (1.0 - ADAM_B1 ** ADAM_STEP)
    v_hat = v / (1.0 - ADAM_B2 ** ADAM_STEP)
    delta = -ADAM_LR * (m_hat / (jnp.sqrt(v_hat) + ADAM_EPS) + ADAM_WD * w)
    return delta, m, v


def _adamw(g, w, m, v, name):
    l, k, n = w.shape
    tk = 256 if k % 256 == 0 else k

    def body(g_ref, w_ref, m_ref, v_ref, d_ref, nm_ref, nv_ref):
        d_ref[...], nm_ref[...], nv_ref[...] = _adamw_math(w_ref[...], g_ref[...], m_ref[...], v_ref[...])

    spec = pl.BlockSpec((None, tk, n), lambda i, j: (i, j, 0))
    return pl.pallas_call(
        body, name=name, out_shape=[jax.ShapeDtypeStruct((l, k, n), F32)] * 3, grid=(l, k // tk),
        in_specs=[spec] * 4, out_specs=[spec] * 3, compiler_params=_params("parallel", "parallel"),
    )(g, w, m, v)


def _sum_adamw(parts, w, m, v, name):
    n, r, c_ = parts.shape

    def body(p_ref, w_ref, m_ref, v_ref, g_ref, d_ref, nm_ref, nv_ref):
        g = p_ref[0]
        for s in range(1, n):
            g = g + p_ref[s]
        g_ref[...] = g
        d_ref[...], nm_ref[...], nv_ref[...] = _adamw_math(w_ref[...], g, m_ref[...], v_ref[...])

    return pl.pallas_call(
        body, name=name, out_shape=[jax.ShapeDtypeStruct((r, c_), F32)] * 4, grid=(1,),
        in_specs=[_full((n, r, c_))] + [_full((r, c_))] * 3, out_specs=[_full((r, c_))] * 4,
        compiler_params=_params("arbitrary"),
    )(parts, w, m, v)


def _rope_tables(pos_col, freq_row):
    s = pos_col.shape[0]
    tm = min(1024, s)

    def body(p_ref, f_ref, o_ref):
        ang = p_ref[...].astype(F32) * f_ref[...]
        lane = lax.broadcasted_iota(jnp.int32, ang.shape, 1) & (HEAD_DIM - 1)
        cs, sn = jnp.cos(ang), jnp.sin(ang)
        o_ref[:, 0:128] = jnp.where(lane < ROT_DIM, cs, 1.0)
        o_ref[:, 128:256] = jnp.where((lane >= ROT_DIM // 2) & (lane < ROT_DIM), sn, 0.0)
        o_ref[:, 256:384] = jnp.where(lane < ROT_DIM // 2, -sn, 0.0)

    return pl.pallas_call(
        body, name="rope_tables", out_shape=jax.ShapeDtypeStruct((s, ROPE_COLS), F32), grid=(s // tm,),
        in_specs=[pl.BlockSpec((tm, 1), lambda i: (i, 0)), _full((1, 128))],
        out_specs=_rows(tm, ROPE_COLS), compiler_params=_params("parallel"),
    )(pos_col, freq_row)


ROPE_COLS = 3 * 128


def _rope_parts(tab, reps=1):
    return [jnp.tile(tab[:, k * 128:(k + 1) * 128], (1, reps)) if reps > 1 else tab[:, k * 128:(k + 1) * 128] for k in range(3)]


def _rope_apply(t, tab):
    w = t.shape[1]
    cos, sin_up, sin_dn = _rope_parts(tab, w // 128)
    return t * cos + pltpu.roll(t, 8, 1) * sin_up + pltpu.roll(t, w - 8, 1) * sin_dn


def _rope_transpose(dr, tab):
    w = dr.shape[1]
    cos, sin_up, sin_dn = _rope_parts(tab, w // 128)
    return dr * cos + pltpu.roll(dr * sin_up, w - 8, 1) + pltpu.roll(dr * sin_dn, 8, 1)


def _norm_matmul(x, g, wt, *, tn, name, bias=None, tm=1024):
    s, d = x.shape
    n = wt.shape[0]
    tm = min(tm, s)

    def body(*refs):
        x_ref, g_ref, w_ref = refs[:3]
        b_ref = refs[3] if bias is not None else None
        h_ref, o_ref = refs[-2:]

        @pl.when(pl.program_id(1) == 0)
        def _():
            xv = x_ref[...]
            r = lax.rsqrt(jnp.mean(xv * xv, axis=-1, keepdims=True) + EPS)
            h_ref[...] = (xv * r * g_ref[...]).astype(BF16)

        acc = lax.dot_general(h_ref[...], w_ref[...], NT, preferred_element_type=F32)
        if b_ref is not None:
            acc = acc + b_ref[...]
        o_ref[...] = acc.astype(BF16)

    in_specs = [_rows(tm, d), _full((1, d)), pl.BlockSpec((tn, d), lambda i, j: (j, 0))]
    args = [x, g, wt]
    if bias is not None:
        in_specs.append(pl.BlockSpec((1, tn), lambda i, j: (0, j)))
        args.append(bias)
    return pl.pallas_call(
        body, name=name,
        out_shape=[jax.ShapeDtypeStruct((s, d), BF16), jax.ShapeDtypeStruct((s, n), BF16)],
        grid=(s // tm, n // tn), in_specs=in_specs,
        out_specs=[_rows(tm, d), pl.BlockSpec((tm, tn), lambda i, j: (i, j))],
        compiler_params=_params("parallel", "arbitrary"),
    )(*args)


def _class_major(tm, dil):
    p = np.zeros((tm, tm), np.float32)
    per = tm // dil
    for r in range(dil):
        for j in range(per):
            p[r * per + j, j * dil + r] = 1.0
    return jnp.asarray(p, dtype=BF16)


def _qkv_proj(x, g, wt, rope, tm=512):
    s, d = x.shape
    n = wt.shape[0]
    gw3 = 3 * GROUP_WIDTH
    tm = min(tm, s)
    assert n == 3 * gw3

    def body(x_ref, g_ref, w_ref, tab_ref, p1_ref, p2_ref, h_ref, o0_ref, o1_ref, o2_ref):
        j = pl.program_id(1)

        @pl.when(j == 0)
        def _():
            xv = x_ref[...]
            r = lax.rsqrt(jnp.mean(xv * xv, axis=-1, keepdims=True) + EPS)
            h_ref[...] = (xv * r * g_ref[...]).astype(BF16)

        acc = lax.dot_general(h_ref[...], w_ref[...], NT, preferred_element_type=F32)

        def store(y):
            yb = y.astype(BF16)
            o0_ref[:, pl.ds(pl.multiple_of(j * GROUP_WIDTH, GROUP_WIDTH), GROUP_WIDTH)] = yb[:, :GROUP_WIDTH]
            for grp, o_ref, p_ref in ((1, o1_ref, p1_ref), (2, o2_ref, p2_ref)):
                dil = DILATIONS[grp]
                per = tm // dil
                yp = jnp.dot(p_ref[...], yb[:, grp * GROUP_WIDTH:(grp + 1) * GROUP_WIDTH],
                             preferred_element_type=F32).astype(BF16)
                for r in range(dil):
                    col = pl.multiple_of(r * gw3 + j * GROUP_WIDTH, GROUP_WIDTH)
                    o_ref[:, pl.ds(col, GROUP_WIDTH)] = yp[r * per:(r + 1) * per, :]

        @pl.when(j < 2)
        def _():
            store(_rope_apply(acc, tab_ref[...]))

        @pl.when(j == 2)
        def _():
            store(acc)

    outs = [jax.ShapeDtypeStruct((s, d), BF16)] + [jax.ShapeDtypeStruct((s // dl, dl * gw3), BF16) for dl in DILATIONS]
    out_specs = [_rows(tm, d)] + [_rows(tm // dl, dl * gw3) for dl in DILATIONS]
    return pl.pallas_call(
        body, name="attn_qkv", out_shape=outs, grid=(s // tm, 3),
        in_specs=[_rows(tm, d), _full((1, d)), pl.BlockSpec((gw3, d), lambda i, j: (j, 0)), _rows(tm, ROPE_COLS)]
        + [_full((tm, tm))] * 2,
        out_specs=out_specs, compiler_params=_params("parallel", "arbitrary"),
    )(x, g, wt, rope, _class_major(tm, DILATIONS[1]), _class_major(tm, DILATIONS[2]))


def _head_masks(rows=SPAN):
    lane = lax.broadcasted_iota(jnp.int32, (rows, 128), 1)
    masks = [lane < HEAD_DIM, lane >= HEAD_DIM]
    lane1 = lax.broadcasted_iota(jnp.int32, (1, 128), 1)
    keep = [jnp.where(lane1 < HEAD_DIM, 1.0, 0.0).astype(BF16), jnp.where(lane1 >= HEAD_DIM, 1.0, 0.0).astype(BF16)]
    return masks, keep


def _band_mask(b):
    row = lax.broadcasted_iota(jnp.int32, (2 * SPAN, 2 * SPAN), 0) & (SPAN - 1)
    col = lax.broadcasted_iota(jnp.int32, (2 * SPAN, 2 * SPAN), 1)
    no_prev = jnp.where(b > 0, 0, 4 * SPAN)
    return ((col < SPAN) & (col >= row + no_prev)) | ((col >= SPAN) & (col - SPAN <= row))


def _attn_fwd(qv, grp, dil):
    l = qv.shape[0]
    s = l * dil
    nb = l // SPAN
    nq = next(n for n in (8, 4, 2, 1) if nb % n == 0)

    def body(q_ref, kp_ref, kc_ref, vp_ref, vc_ref, o_ref, l_ref):
        b = pl.program_id(1)
        masks, keep = _head_masks()
        for qb in range(nq):
            valid = _band_mask(b * nq + qb)
            rows = slice(qb * SPAN, (qb + 1) * SPAN)
            before = slice((qb - 1) * SPAN, qb * SPAN)
            for p in range(GROUP_WIDTH // 128):
                sl = slice(p * 128, (p + 1) * 128)
                qp = q_ref[rows, sl]
                kk = jnp.concatenate([kp_ref[:, sl] if qb == 0 else kc_ref[before, sl], kc_ref[rows, sl]], axis=0)
                vv = jnp.concatenate([vp_ref[:, sl] if qb == 0 else vc_ref[before, sl], vc_ref[rows, sl]], axis=0)
                q2 = jnp.concatenate([qp * keep[0], qp * keep[1]], axis=0)
                sc = lax.dot_general(q2, kk, NT, preferred_element_type=F32) * (HEAD_DIM ** -0.5)
                sc = jnp.where(valid, sc, -1e30)
                mx = jnp.max(sc, axis=-1, keepdims=True)
                pe = jnp.exp(sc - mx)
                den = jnp.sum(pe, axis=-1, keepdims=True)
                out = jnp.dot(pe.astype(BF16), vv, preferred_element_type=F32) / den
                lse = jnp.broadcast_to(mx + jnp.log(den), (2 * SPAN, 128))
                o_ref[rows, sl] = jnp.where(masks[0], out[:SPAN], out[SPAN:]).astype(BF16)
                l_ref[rows, sl] = jnp.where(masks[0], lse[:SPAN], lse[SPAN:])

    blk = (nq * SPAN, GROUP_WIDTH)
    cur = lambda t: pl.BlockSpec(blk, lambda r, b: (b, r * 3 + t))
    prev = lambda t: pl.BlockSpec((SPAN, GROUP_WIDTH), lambda r, b: (jnp.maximum(nq * b - 1, 0), r * 3 + t))
    out = pl.BlockSpec(blk, lambda r, b: (b, r))
    o, lse = pl.pallas_call(
        body, name=f"attn_fwd_g{grp}",
        out_shape=[jax.ShapeDtypeStruct((l, dil * GROUP_WIDTH), BF16), jax.ShapeDtypeStruct((l, dil * GROUP_WIDTH), F32)],
        grid=(dil, nb // nq), in_specs=[cur(0), prev(1), cur(1), prev(2), cur(2)], out_specs=[out, out],
        compiler_params=_params("parallel", "arbitrary"),
    )(qv, qv, qv, qv, qv)
    return o.reshape(s, GROUP_WIDTH), lse.reshape(s, GROUP_WIDTH)


def _resnorm_store(y, x_ref, g_ref, y_ref, xo_ref):
    r = lax.rsqrt(jnp.mean(y * y, axis=-1, keepdims=True) + EPS)
    y_ref[...] = y
    xo_ref[...] = x_ref[...] + y * r * g_ref[...]


def _mix_wo(os_, ls_, wot, x, g, tm=512):
    s, d = x.shape
    gw = wot.shape[1]
    tm = min(tm, s)

    def body(o0, o1, o2, l0, l1, l2, w_ref, x_ref, g_ref, y_ref, xo_ref, mixed_ref, lse_ref):
        a0, a1, a2 = l0[...], l1[...], l2[...]
        mx = jnp.maximum(jnp.maximum(a0, a1), a2)
        e0, e1, e2 = jnp.exp(a0 - mx), jnp.exp(a1 - mx), jnp.exp(a2 - mx)
        den = e0 + e1 + e2
        mixed = (e0 / den) * o0[...].astype(F32) + (e1 / den) * o1[...].astype(F32) + (e2 / den) * o2[...].astype(F32)
        mixed_ref[...] = mixed.astype(BF16)
        lse_ref[...] = mx + jnp.log(den)
        y = lax.dot_general(mixed.astype(BF16), w_ref[...], NT, preferred_element_type=F32)
        _resnorm_store(y, x_ref, g_ref, y_ref, xo_ref)

    return pl.pallas_call(
        body, name="mix_wo",
        out_shape=[jax.ShapeDtypeStruct((s, d), F32), jax.ShapeDtypeStruct((s, d), F32),
                   jax.ShapeDtypeStruct((s, gw), BF16), jax.ShapeDtypeStruct((s, gw), F32)],
        grid=(s // tm,), in_specs=[_rows(tm, gw)] * 6 + [_full((d, gw)), _rows(tm, d), _full((1, d))],
        out_specs=[_rows(tm, d), _rows(tm, d), _rows(tm, gw), _rows(tm, gw)],
        compiler_params=_params("parallel"),
    )(*os_, *ls_, wot, x, g)


def _matmul_resnorm(a, w, x, g, *, name, bias=None, tm=512):
    s, k = a.shape
    d = w.shape[1]
    tm = min(tm, s)

    def body(*refs):
        a_ref, w_ref = refs[:2]
        b_ref = refs[2] if bias is not None else None
        x_ref, g_ref, y_ref, xo_ref = refs[-4:]
        y = jnp.dot(a_ref[...], w_ref[...], preferred_element_type=F32)
        if b_ref is not None:
            y = y + b_ref[...]
        _resnorm_store(y, x_ref, g_ref, y_ref, xo_ref)

    in_specs = [_rows(tm, k), _full((k, d))] + ([_full((1, d))] if bias is not None else []) + [_rows(tm, d), _full((1, d))]
    args = [a, w] + ([bias] if bias is not None else []) + [x, g]
    return pl.pallas_call(
        body, name=name, out_shape=[jax.ShapeDtypeStruct((s, d), F32)] * 2, grid=(s // tm,),
        in_specs=in_specs, out_specs=[_rows(tm, d)] * 2, compiler_params=_params("parallel"),
    )(*args)


FFN_SUB = 256


def _conv3_rows(z_ref, halo_ref, rb, sub, cs, first):
    zc = z_ref[rb * sub:(rb + 1) * sub, cs].astype(F32)
    if rb == 0:
        halo = halo_ref[:, cs].astype(F32) * jnp.where(first, 0.0, 1.0)
    else:
        halo = z_ref[rb * sub - 16:rb * sub, cs].astype(F32)[8:]
    z2, z1 = _conv3_taps(zc, halo)
    return z2, z1, zc


def _conv3_taps(z, halo):
    row = lax.broadcasted_iota(jnp.int32, (8, z.shape[1]), 0)
    h6, h7 = halo[6:7, :], halo[7:8, :]
    r1, r2 = pltpu.roll(z, 1, 0), pltpu.roll(z, 2, 0)
    z1 = jnp.concatenate([jnp.where(row == 0, h7, r1[0:8]), r1[8:]], axis=0)
    z2 = jnp.concatenate([jnp.where(row == 0, h6, jnp.where(row == 1, h7, r2[0:8])), r2[8:]], axis=0)
    return z2, z1


def _ffn_cols(f):
    return _tile(f)


def _lane_chunks(width, fn):
    def step(k, carry):
        fn(pl.ds(pl.multiple_of(k * 128, 128), 128))
        return carry

    lax.fori_loop(0, width // 128, step, 0)


def _ffn_act(z, w_dw, b_dw, tm=1024):
    s, f2 = z.shape
    f = f2 // 2
    tm = min(tm, s)
    sub = min(FFN_SUB, tm)
    tc = _ffn_cols(f)
    nfc = f // tc

    def body(zu, zg, hu, hg, wu, wg, bu, bg, o_ref):
        first = pl.program_id(0) == 0

        def chunk(cs):
            for rb in range(tm // sub):
                def conv(z_ref, h_ref, w_ref, b_ref):
                    z2, z1, zc = _conv3_rows(z_ref, h_ref, rb, sub, cs, first)
                    return w_ref[0:1, cs] * z2 + w_ref[1:2, cs] * z1 + w_ref[2:3, cs] * zc + b_ref[:, cs]

                up, gate = conv(zu, hu, wu, bu), conv(zg, hg, wg, bg)
                o_ref[rb * sub:(rb + 1) * sub, cs] = (gate * _sigmoid(gate) * up).astype(BF16)

        _lane_chunks(tc, chunk)

    hb = tm // 8
    tile = lambda off: pl.BlockSpec((tm, tc), lambda i, j: (i, off + j))
    halo = lambda off: pl.BlockSpec((8, tc), lambda i, j: (jnp.maximum(i * hb - 1, 0), off + j))
    prm = lambda rows, off: pl.BlockSpec((rows, tc), lambda i, j: (0, off + j))
    return pl.pallas_call(
        body, name="ffn_act", out_shape=jax.ShapeDtypeStruct((s, f), BF16), grid=(s // tm, nfc),
        in_specs=[tile(0), tile(nfc), halo(0), halo(nfc), prm(FFN_CONV, 0), prm(FFN_CONV, nfc), prm(1, 0), prm(1, nfc)],
        out_specs=pl.BlockSpec((tm, tc), lambda i, j: (i, j)), compiler_params=_params("parallel", "parallel"),
    )(z, z, z, z, w_dw, w_dw, b_dw, b_dw)


def _shifted_planes(ext_ref):
    rows = ext_ref.shape[1]
    for s in range(1, 8):
        ext_ref[s, 0:rows - 8, :] = ext_ref[0, s:s + rows - 8, :]


def _window(ext_ref, off, tm, cs):
    s = off % 8
    return ext_ref[s, off - s:off - s + tm, cs]


def _conv_taps(ext_ref, w_ref, offs, tm, out_ref):
    def chunk(cs):
        acc = w_ref[0:1, cs] * _window(ext_ref, offs[0], tm, cs)
        for j in range(1, len(offs)):
            acc = acc + w_ref[j:j + 1, cs] * _window(ext_ref, offs[j], tm, cs)
        out_ref[:, cs] = acc

    _lane_chunks(out_ref.shape[1], chunk)


def _glu_planes(ag_ref, halo_ref, ext_ref, first, c):
    hal = halo_ref[...].astype(F32)
    ext_ref[0, 0:CONV_HALO, :] = hal[:, :c] * _sigmoid(hal[:, c:]) * jnp.where(first, 0.0, 1.0)
    ag = ag_ref[...].astype(F32)
    ext_ref[0, CONV_HALO:, :] = ag[:, :c] * _sigmoid(ag[:, c:])
    _shifted_planes(ext_ref)


def _layernorm_stats(u1):
    mu = jnp.mean(u1, axis=-1, keepdims=True)
    cen = u1 - mu
    rstd = lax.rsqrt(jnp.mean(cen * cen, axis=-1, keepdims=True) + EPS)
    return cen * rstd, rstd


def _conv_mid(ag, w_dw, b_dw, ln_g, ln_b, tm=512):
    s, c2 = ag.shape
    c = c2 // 2
    tm = min(tm, s)

    def body(ag_ref, halo_ref, w_ref, b_ref, g_ref, bb_ref, o_ref, u1_ref, ext_ref):
        _glu_planes(ag_ref, halo_ref, ext_ref, pl.program_id(0) == 0, c)
        base = CONV_HALO - (CONV_KERNEL - 1)
        _conv_taps(ext_ref, w_ref, [base + j for j in range(CONV_KERNEL)], tm, u1_ref)
        xh, _ = _layernorm_stats(u1_ref[...] + b_ref[...])
        u2 = xh * g_ref[...] + bb_ref[...]
        o_ref[...] = (u2 * _sigmoid(u2)).astype(BF16)

    hb = tm // CONV_HALO
    return pl.pallas_call(
        body, name="conv_mid", out_shape=[jax.ShapeDtypeStruct((s, c), BF16), jax.ShapeDtypeStruct((s, c), F32)], grid=(s // tm,),
        in_specs=[_rows(tm, c2), pl.BlockSpec((CONV_HALO, c2), lambda i: (jnp.maximum(i * hb - 1, 0), 0)),
                  _full((CONV_KERNEL, c)), _full((1, c)), _full((1, c)), _full((1, c))],
        out_specs=[_rows(tm, c), _rows(tm, c)], scratch_shapes=[pltpu.VMEM((8, CONV_HALO + tm, c), F32)],
        compiler_params=_params("arbitrary"),
    )(ag, ag, w_dw, b_dw, ln_g, ln_b)


def _postnorm_bwd(y, g, dxo, *, name, with_bias_grad=False, tm=1024):
    s, d = y.shape
    tm = min(tm, s)

    def body(y_ref, g_ref, dx_ref, dy_ref, dg_ref, *rest):
        @pl.when(pl.program_id(0) == 0)
        def _():
            dg_ref[...] = jnp.zeros_like(dg_ref)
            for r_ in rest:
                r_[...] = jnp.zeros_like(r_)

        yv, dxo_v = y_ref[...], dx_ref[...]
        r = lax.rsqrt(jnp.mean(yv * yv, axis=-1, keepdims=True) + EPS)
        yh = yv * r
        dyh = dxo_v * g_ref[...]
        dy = r * (dyh - yh * jnp.mean(dyh * yh, axis=-1, keepdims=True))
        dy_ref[...] = dy.astype(BF16)
        dg_ref[...] += jnp.sum(dxo_v * yh, axis=0, keepdims=True)
        for r_ in rest:
            r_[...] += jnp.sum(dy, axis=0, keepdims=True)

    nacc = 2 if with_bias_grad else 1
    return pl.pallas_call(
        body, name=name, out_shape=[jax.ShapeDtypeStruct((s, d), BF16)] + [jax.ShapeDtypeStruct((1, d), F32)] * nacc,
        grid=(s // tm,), in_specs=[_rows(tm, d), _full((1, d)), _rows(tm, d)],
        out_specs=[_rows(tm, d)] + [_full((1, d))] * nacc, compiler_params=_params("arbitrary"),
    )(y, g, dxo)


def _loss_postnorm_bwd(y, g, xo, target, tm=512):
    s, d = y.shape
    tm = min(tm, s)

    def body(y_ref, g_ref, x_ref, t_ref, dx_ref, loss_ref, dy_ref, dg_ref):
        @pl.when(pl.program_id(0) == 0)
        def _():
            loss_ref[...] = jnp.zeros_like(loss_ref)
            dg_ref[...] = jnp.zeros_like(dg_ref)

        err = x_ref[...] - t_ref[...]
        dxo_v = err * (1.0 / d)
        dx_ref[...] = dxo_v
        loss_ref[...] += 0.5 * jnp.sum(jnp.mean(err * err, axis=-1, keepdims=True))
        yv = y_ref[...]
        r = lax.rsqrt(jnp.mean(yv * yv, axis=-1, keepdims=True) + EPS)
        yh = yv * r
        dyh = dxo_v * g_ref[...]
        dy_ref[...] = (r * (dyh - yh * jnp.mean(dyh * yh, axis=-1, keepdims=True))).astype(BF16)
        dg_ref[...] += jnp.sum(dxo_v * yh, axis=0, keepdims=True)

    return pl.pallas_call(
        body, name="loss_post_bwd",
        out_shape=[jax.ShapeDtypeStruct((s, d), F32), jax.ShapeDtypeStruct((1, 128), F32),
                   jax.ShapeDtypeStruct((s, d), BF16), jax.ShapeDtypeStruct((1, d), F32)],
        grid=(s // tm,), in_specs=[_rows(tm, d), _full((1, d)), _rows(tm, d), _rows(tm, d)],
        out_specs=[_rows(tm, d), _full((1, 128)), _rows(tm, d), _full((1, d))], compiler_params=_params("arbitrary"),
    )(y, g, xo, target)


def _matmul(gmat, w, *, name, out_dtype, transposed_w, tm=512):
    s, k = gmat.shape
    n = w.shape[0] if transposed_w else w.shape[1]
    tm = min(tm, s)

    def body(g_ref, w_ref, o_ref):
        if transposed_w:
            acc = lax.dot_general(g_ref[...], w_ref[...], NT, preferred_element_type=F32)
        else:
            acc = jnp.dot(g_ref[...], w_ref[...], preferred_element_type=F32)
        o_ref[...] = acc.astype(out_dtype)

    return pl.pallas_call(
        body, name=name, out_shape=jax.ShapeDtypeStruct((s, n), out_dtype), grid=(s // tm,),
        in_specs=[_rows(tm, k), _full(w.shape)], out_specs=_rows(tm, n), compiler_params=_params("parallel"),
    )(gmat, w)


def _matmul_prenorm_bwd(pieces, wt, x, g, dres, *, name, tm=256):
    s, d = x.shape
    tm = min(tm, s)
    np_ = len(pieces)

    def body(*refs):
        p_refs, w_refs = refs[:np_], refs[np_:2 * np_]
        x_ref, g_ref, r_ref, dx_ref, dg_ref = refs[2 * np_:]

        @pl.when(pl.program_id(0) == 0)
        def _():
            dg_ref[...] = jnp.zeros_like(dg_ref)

        dh = None
        for p_ref, w_ref in zip(p_refs, w_refs):
            t = jnp.dot(p_ref[...], w_ref[...], preferred_element_type=F32)
            dh = t if dh is None else dh + t
        xv = x_ref[...]
        r = lax.rsqrt(jnp.mean(xv * xv, axis=-1, keepdims=True) + EPS)
        xh = xv * r
        dyh = dh * g_ref[...]
        dx_ref[...] = r_ref[...] + r * (dyh - xh * jnp.mean(dyh * xh, axis=-1, keepdims=True))
        dg_ref[...] += jnp.sum(dh * xh, axis=0, keepdims=True)

    in_specs = []
    for _, c0, kc, _ in pieces:
        assert c0 % kc == 0
        in_specs.append(pl.BlockSpec((tm, kc), lambda i, _b=c0 // kc: (i, _b)))
    for _, _, kc, r0 in pieces:
        assert r0 % kc == 0
        in_specs.append(pl.BlockSpec((kc, d), lambda i, _b=r0 // kc: (_b, 0)))
    in_specs += [_rows(tm, d), _full((1, d)), _rows(tm, d)]
    return pl.pallas_call(
        body, name=name, out_shape=[jax.ShapeDtypeStruct((s, d), F32), jax.ShapeDtypeStruct((1, d), F32)],
        grid=(s // tm,), in_specs=in_specs, out_specs=[_rows(tm, d), _full((1, d))],
        compiler_params=_params("arbitrary"),
    )(*[p[0] for p in pieces], *[wt] * np_, x, g, dres)


def _weight_grad(a, gmat, *, name, a_col0=0, ka=None, out=None, out_shape=None, layer=0, row0=0, ts=2048):
    s = a.shape[0]
    ka = a.shape[1] if ka is None else ka
    n = gmat.shape[1]
    ts = min(ts, s)
    tka = _tile(ka, a_col0, row0)
    shape = out.shape if out is not None else out_shape
    nsteps = s // ts

    def body(a_ref, g_ref, *rest):
        o_ref, acc_ref = rest[-2:]
        i = pl.program_id(1)

        @pl.when(i == 0)
        def _():
            acc_ref[...] = jnp.zeros_like(acc_ref)

        acc_ref[...] += lax.dot_general(a_ref[...], g_ref[...], TN, preferred_element_type=F32)

        @pl.when(i == nsteps - 1)
        def _():
            o_ref[...] = acc_ref[...].astype(BF16)

    in_specs = [pl.BlockSpec((ts, tka), lambda k, i: (i, a_col0 // tka + k)), pl.BlockSpec((ts, n), lambda k, i: (i, 0))]
    args = [a, gmat]
    aliases = {}
    if out is not None:
        in_specs.append(ANY)
        args.append(out)
        aliases = {2: 0}
    return pl.pallas_call(
        body, name=name, out_shape=jax.ShapeDtypeStruct(shape, BF16), grid=(ka // tka, nsteps), in_specs=in_specs,
        out_specs=pl.BlockSpec((None, tka, n), lambda k, i: (layer, row0 // tka + k, 0)),
        scratch_shapes=[pltpu.VMEM((tka, n), F32)],
        input_output_aliases=aliases, compiler_params=_params("parallel", "arbitrary"),
    )(*args)


def _ffn_act_bwd(z, dact, w_dw, b_dw, tm=512):
    s, f2 = z.shape
    f = f2 // 2
    tm = min(tm, s)
    sub = min(FFN_SUB // 2, tm)
    tc = _ffn_cols(f)
    nfc = f // tc

    def body(zu, zg, hu, hg, wu, wg, bu, bg, da_ref, du_ref, dgt_ref, dbu_ref, dbg_ref, dwu_ref, dwg_ref):
        i = pl.program_id(1)

        @pl.when(i == 0)
        def _():
            for r_ in (dbu_ref, dbg_ref, dwu_ref, dwg_ref):
                r_[...] = jnp.zeros_like(r_)

        def chunk(cs):
            for rb in range(tm // sub):
                rows = slice(rb * sub, (rb + 1) * sub)

                def conv(z_ref, h_ref, w_ref, b_ref):
                    taps = _conv3_rows(z_ref, h_ref, rb, sub, cs, i == 0)
                    return taps, w_ref[0:1, cs] * taps[0] + w_ref[1:2, cs] * taps[1] + w_ref[2:3, cs] * taps[2] + b_ref[:, cs]

                taps_u, up = conv(zu, hu, wu, bu)
                taps_g, gate = conv(zg, hg, wg, bg)
                da = da_ref[rows, cs].astype(F32)
                sg = _sigmoid(gate)
                d_up = da * (gate * sg)
                d_gate = da * up * (sg * (1.0 + gate * (1.0 - sg)))
                du_ref[rows, cs] = d_up.astype(BF16)
                dgt_ref[rows, cs] = d_gate.astype(BF16)
                for dv, taps, db_ref, dw_ref in ((d_up, taps_u, dbu_ref, dwu_ref), (d_gate, taps_g, dbg_ref, dwg_ref)):
                    db_ref[:, cs] += jnp.sum(dv, axis=0, keepdims=True)
                    for k_, tap in enumerate(taps):
                        dw_ref[k_:k_ + 1, cs] += jnp.sum(dv * tap, axis=0, keepdims=True)

        _lane_chunks(tc, chunk)

    hb = tm // 8
    tile = lambda off: pl.BlockSpec((tm, tc), lambda j, i: (i, off + j))
    halo = lambda off: pl.BlockSpec((8, tc), lambda j, i: (jnp.maximum(i * hb - 1, 0), off + j))
    prm = lambda rows, off: pl.BlockSpec((rows, tc), lambda j, i: (0, off + j))
    acc = lambda rows: pl.BlockSpec((rows, tc), lambda j, i: (0, j))
    return pl.pallas_call(
        body, name="ffn_act_bwd",
        out_shape=[jax.ShapeDtypeStruct((s, f), BF16)] * 2 + [jax.ShapeDtypeStruct((1, f), F32)] * 2
        + [jax.ShapeDtypeStruct((FFN_CONV, f), F32)] * 2,
        grid=(nfc, s // tm),
        in_specs=[tile(0), tile(nfc), halo(0), halo(nfc), prm(FFN_CONV, 0), prm(FFN_CONV, nfc), prm(1, 0), prm(1, nfc), tile(0)],
        out_specs=[tile(0), tile(0), acc(1), acc(1), acc(FFN_CONV), acc(FFN_CONV)],
        compiler_params=_params("parallel", "arbitrary"),
    )(z, z, z, z, w_dw, w_dw, b_dw, b_dw, dact)


def _conv3_transpose(dug, w_dw, col0, tm=1024):
    s, f = dug.shape
    tm = min(tm, s)
    sub = min(FFN_SUB, tm)
    nsub = tm // sub
    tc = _ffn_cols(f)
    nfc = f // tc
    nrow = s // tm
    off = col0 // tc

    def body(d_ref, n_ref, w_ref, o_ref):
        keep_next = jnp.where(pl.program_id(0) == nrow - 1, 0.0, 1.0)

        def chunk(cs):
            for rb in range(nsub):
                rows = slice(rb * sub, (rb + 1) * sub)
                dv = d_ref[rows, cs].astype(F32)
                if rb == nsub - 1:
                    nxt = n_ref[:, cs].astype(F32) * keep_next
                else:
                    nxt = d_ref[(rb + 1) * sub:(rb + 1) * sub + 16, cs].astype(F32)[:8]
                n0, n1 = nxt[0:1, :], nxt[1:2, :]
                row = lax.broadcasted_iota(jnp.int32, (8, dv.shape[1]), 0)
                r1, r2 = pltpu.roll(dv, sub - 1, 0), pltpu.roll(dv, sub - 2, 0)
                d1 = jnp.concatenate([r1[:sub - 8], jnp.where(row == 7, n0, r1[sub - 8:])], axis=0)
                d2 = jnp.concatenate([r2[:sub - 8], jnp.where(row == 7, n1, jnp.where(row == 6, n0, r2[sub - 8:]))], axis=0)
                o_ref[rows, cs] = (w_ref[2:3, cs] * dv + w_ref[1:2, cs] * d1 + w_ref[0:1, cs] * d2).astype(BF16)

        _lane_chunks(tc, chunk)

    hb = tm // 8
    return pl.pallas_call(
        body, name="conv3_transpose", out_shape=jax.ShapeDtypeStruct((s, f), BF16), grid=(nrow, nfc),
        in_specs=[pl.BlockSpec((tm, tc), lambda i, j: (i, j)),
                  pl.BlockSpec((8, tc), lambda i, j: (jnp.minimum((i + 1) * hb, s // 8 - 1), j)),
                  pl.BlockSpec((FFN_CONV, tc), lambda i, j: (0, off + j))],
        out_specs=pl.BlockSpec((tm, tc), lambda i, j: (i, j)), compiler_params=_params("parallel", "parallel"),
    )(dug, dug, w_dw)


def _conv_mid_bwd(ag, u1, du3, b_dw, ln_g, ln_b, tm=256):
    s, c2 = ag.shape
    c = c2 // 2
    tm = min(tm, s)

    def body(ag_ref, halo_ref, u1in_ref, du_ref, b_ref, g_ref, bb_ref, o_ref, dlg_ref, dlb_ref, db_ref, dw_ref, ext_ref, u1_ref):
        @pl.when(pl.program_id(0) == 0)
        def _():
            for r_ in (dlg_ref, dlb_ref, db_ref, dw_ref):
                r_[...] = jnp.zeros_like(r_)

        _glu_planes(ag_ref, halo_ref, ext_ref, pl.program_id(0) == 0, c)
        xh, rstd = _layernorm_stats(u1in_ref[...] + b_ref[...])
        u2 = xh * g_ref[...] + bb_ref[...]
        sg = _sigmoid(u2)
        du2 = du_ref[...] * (sg * (1.0 + u2 * (1.0 - sg)))
        dlg_ref[...] += jnp.sum(du2 * xh, axis=0, keepdims=True)
        dlb_ref[...] += jnp.sum(du2, axis=0, keepdims=True)
        dxh = du2 * g_ref[...]
        du1 = rstd * (dxh - jnp.mean(dxh, axis=-1, keepdims=True) - xh * jnp.mean(dxh * xh, axis=-1, keepdims=True))
        o_ref[...] = du1.astype(BF16)
        db_ref[...] += jnp.sum(du1, axis=0, keepdims=True)
        u1_ref[...] = du1
        base = CONV_HALO - (CONV_KERNEL - 1)

        def chunk(cs):
            dc = u1_ref[:, cs]
            for j in range(CONV_KERNEL):
                dw_ref[j:j + 1, cs] += jnp.sum(dc * _window(ext_ref, base + j, tm, cs), axis=0, keepdims=True)

        _lane_chunks(c, chunk)

    hb = tm // CONV_HALO
    vec = _full((1, c))
    return pl.pallas_call(
        body, name="conv_mid_bwd",
        out_shape=[jax.ShapeDtypeStruct((s, c), BF16)] + [jax.ShapeDtypeStruct((1, c), F32)] * 3
        + [jax.ShapeDtypeStruct((CONV_HALO, c), F32)],
        grid=(s // tm,),
        in_specs=[_rows(tm, c2), pl.BlockSpec((CONV_HALO, c2), lambda i: (jnp.maximum(i * hb - 1, 0), 0)), _rows(tm, c),
                  _rows(tm, c), vec, vec, vec],
        out_specs=[_rows(tm, c), vec, vec, vec, _full((CONV_HALO, c))],
        scratch_shapes=[pltpu.VMEM((8, CONV_HALO + tm, c), F32), pltpu.VMEM((tm, c), F32)],
        compiler_params=_params("arbitrary"),
    )(ag, ag, u1, du3, b_dw, ln_g, ln_b)


def _glu_conv_bwd(du1, ag, w_dw, tm=512):
    s, c = du1.shape
    tm = min(tm, s)
    nrow = s // tm

    def body(d_ref, n_ref, ag_ref, w_ref, o_ref, db_ref, ext_ref, du0_ref):
        @pl.when(pl.program_id(0) == 0)
        def _():
            db_ref[...] = jnp.zeros_like(db_ref)

        ext_ref[0, 0:tm, :] = d_ref[...].astype(F32)
        ext_ref[0, tm:, :] = n_ref[...].astype(F32) * jnp.where(pl.program_id(0) == nrow - 1, 0.0, 1.0)
        _shifted_planes(ext_ref)
        top = CONV_KERNEL - 1
        _conv_taps(ext_ref, w_ref, [top - j for j in range(CONV_KERNEL)], tm, du0_ref)
        du0 = du0_ref[...]
        ag = ag_ref[...].astype(F32)
        a, gt = ag[:, :c], ag[:, c:]
        sg = _sigmoid(gt)
        da = du0 * sg
        dgt = du0 * a * (sg * (1.0 - sg))
        o_ref[:, :c] = da.astype(BF16)
        o_ref[:, c:] = dgt.astype(BF16)
        db_ref[:, :c] += jnp.sum(da, axis=0, keepdims=True)
        db_ref[:, c:] += jnp.sum(dgt, axis=0, keepdims=True)

    hb = tm // CONV_HALO
    return pl.pallas_call(
        body, name="glu_conv_bwd",
        out_shape=[jax.ShapeDtypeStruct((s, 2 * c), BF16), jax.ShapeDtypeStruct((1, 2 * c), F32)], grid=(nrow,),
        in_specs=[_rows(tm, c), pl.BlockSpec((CONV_HALO, c), lambda i: (jnp.minimum((i + 1) * hb, s // CONV_HALO - 1), 0)),
                  _rows(tm, 2 * c), _full((CONV_KERNEL, c))],
        out_specs=[_rows(tm, 2 * c), _full((1, 2 * c))],
        scratch_shapes=[pltpu.VMEM((8, tm + CONV_HALO, c), F32), pltpu.VMEM((tm, c), F32)],
        compiler_params=_params("arbitrary"),
    )(du1, du1, ag, w_dw)


def _head_rows(v, mask):
    return jnp.max(jnp.where(mask, v, -jnp.inf), axis=-1, keepdims=True)


def _attn_bwd(qv, dmix, mixed, lse, rope, grp, dil, ties=()):
    l = qv.shape[0]
    s = l * dil
    nb = l // SPAN
    view = lambda t: t.reshape(l, dil * t.shape[1])
    scale = HEAD_DIM ** -0.5
    gw = GROUP_WIDTH

    def body(*refs):
        q_ref, kp_ref, kc_ref, vp_ref, vc_ref, do_ref, mx_ref, l_ref, tab_ref, tabp_ref = refs[:10]
        dq_ref, dkv_ref, carry_ref = refs[-3:]
        b = pl.program_id(1)

        @pl.when(b < nb)
        def _():
            valid = _band_mask(b)
            masks, keep = _head_masks()
            for p in range(gw // 128):
                sl = slice(p * 128, (p + 1) * 128)
                sl_v = slice(gw + p * 128, gw + (p + 1) * 128)
                qp, dop = q_ref[:, sl], do_ref[:, sl]
                kk = jnp.concatenate([kp_ref[:, sl], kc_ref[:, sl]], axis=0)
                vv = jnp.concatenate([vp_ref[:, sl], vc_ref[:, sl]], axis=0)
                prod = dop.astype(F32) * mx_ref[:, sl].astype(F32)
                lsep = l_ref[:, sl]
                q2 = jnp.concatenate([qp * keep[0], qp * keep[1]], axis=0)
                do2 = jnp.concatenate([dop * keep[0], dop * keep[1]], axis=0)
                lse2 = jnp.concatenate([_head_rows(lsep, masks[h]) for h in range(2)], axis=0)
                dbar2 = jnp.concatenate([jnp.sum(jnp.where(masks[h], prod, 0.0), axis=-1, keepdims=True) for h in range(2)], axis=0)
                sc = lax.dot_general(q2, kk, NT, preferred_element_type=F32) * scale
                pe = jnp.where(valid, jnp.exp(sc - lse2), 0.0)
                dp = lax.dot_general(do2, vv, NT, preferred_element_type=F32)
                ds = (pe * (dp - dbar2) * scale).astype(BF16)
                dq2 = jnp.dot(ds, kk, preferred_element_type=F32)
                dq = jnp.where(masks[0], dq2[:SPAN], dq2[SPAN:])
                dq_ref[:, sl] = _rope_transpose(dq, tab_ref[...]).astype(BF16)
                dk = lax.dot_general(ds, q2, TN, preferred_element_type=F32)
                dv = lax.dot_general(pe.astype(BF16), do2, TN, preferred_element_type=F32)

                @pl.when(b > 0)
                def _():
                    dk_prev = carry_ref[:, sl] + dk[:SPAN]
                    dkv_ref[:, sl] = _rope_transpose(dk_prev, tabp_ref[...]).astype(BF16)
                    dkv_ref[:, sl_v] = (carry_ref[:, sl_v] + dv[:SPAN]).astype(BF16)

                carry_ref[:, sl] = dk[SPAN:]
                carry_ref[:, sl_v] = dv[SPAN:]

        @pl.when(b == nb)
        def _():
            for p in range(gw // 128):
                sl = slice(p * 128, (p + 1) * 128)
                sl_v = slice(gw + p * 128, gw + (p + 1) * 128)
                dkv_ref[:, sl] = _rope_transpose(carry_ref[:, sl], tabp_ref[...]).astype(BF16)
                dkv_ref[:, sl_v] = carry_ref[:, sl_v].astype(BF16)

    blk = (SPAN, gw)
    cb = lambda b: jnp.minimum(b, nb - 1)
    cur = lambda t: pl.BlockSpec(blk, lambda r, b: (cb(b), r * 3 + t))
    prev = lambda t: pl.BlockSpec(blk, lambda r, b: (jnp.maximum(cb(b) - 1, 0), r * 3 + t))
    own = pl.BlockSpec(blk, lambda r, b: (cb(b), r))
    tab = pl.BlockSpec((SPAN, ROPE_COLS), lambda r, b: (cb(b), r))
    tab_prev = pl.BlockSpec((SPAN, ROPE_COLS), lambda r, b: (jnp.maximum(b - 1, 0), r))
    dq, dkv = pl.pallas_call(
        body, name=f"attn_bwd_g{grp}",
        out_shape=[jax.ShapeDtypeStruct((l, dil * gw), BF16), jax.ShapeDtypeStruct((l, dil * 2 * gw), BF16)],
        grid=(dil, nb + 1),
        in_specs=[cur(0), prev(1), cur(1), prev(2), cur(2), own, own, own, tab, tab_prev] + [ANY] * len(ties),
        out_specs=[own, pl.BlockSpec((SPAN, 2 * gw), lambda r, b: (jnp.maximum(b - 1, 0), r))],
        scratch_shapes=[pltpu.VMEM((SPAN, 2 * gw), F32)], compiler_params=_params("parallel", "arbitrary"),
    )(qv, qv, qv, qv, qv, view(dmix), view(mixed), view(lse), view(rope), view(rope), *ties)
    return dq.reshape(s, gw), dkv.reshape(s, 2 * gw)


def _rope_freq_row():
    half = ROT_DIM // 2
    inv = (ROPE_THETA ** (-np.arange(half, dtype=np.float32) / half)).astype(np.float32)
    row = np.zeros((1, 128), np.float32)
    for head in range(128 // HEAD_DIM):
        row[0, head * HEAD_DIM:head * HEAD_DIM + half] = inv
        row[0, head * HEAD_DIM + half:head * HEAD_DIM + ROT_DIM] = inv
    return jnp.asarray(row)


def _ffn_fwd(x, g_pre, g_post, w_up_t, w_dw, b_dw, w_down):
    h, z = _norm_matmul(x, g_pre, w_up_t, tn=_tile(w_up_t.shape[0]), name="ffn_up")
    act = _ffn_act(z, w_dw, b_dw)
    y, xo = _matmul_resnorm(act, w_down, x, g_post, name="ffn_down")
    return xo, (x, h, z, act, y)


def _ffn_bwd(saved, dxo, g_pre, g_post, w_up_t, w_dw, b_dw, w_down, post=None):
    x, h, z, act, y = saved
    f = act.shape[1]
    d = x.shape[1]
    dy, dg_post = post if post is not None else _postnorm_bwd(y, g_post, dxo, name="ffn_post_bwd")
    dact = _matmul(dy, w_down, name="ffn_dact", out_dtype=BF16, transposed_w=True)
    d_down = _weight_grad(act, dy, name="ffn_dw_down", out_shape=(1, f, d))
    dug_u, dug_g, db_u, db_g, dwd_u, dwd_g = _ffn_act_bwd(z, dact, w_dw, b_dw)
    dz_u = _conv3_transpose(dug_u, w_dw, 0)
    dz_g = _conv3_transpose(dug_g, w_dw, f)
    dx, dg_pre = _matmul_prenorm_bwd([(dz_u, 0, f, 0), (dz_g, 0, f, f)], w_up_t, x, g_pre, dxo, name="ffn_dx")
    d_up_t = _weight_grad(dz_u, h, name="ffn_dw_up", out_shape=(1, 2 * f, d))
    d_up_t = _weight_grad(dz_g, h, name="ffn_dw_up", out=d_up_t, row0=f)
    grads = dict(w_dw=jnp.concatenate([dwd_u, dwd_g], axis=1), b_dw=jnp.concatenate([db_u, db_g], axis=1),
                 g_pre=dg_pre, g_post=dg_post)
    return dx, grads, d_up_t, d_down


def _local_step(x, pos_col, target, p, tie=None, late_weights=None, exchange=None):
    ng = p["norm_g"]
    row = lambda r: ng[r:r + 1]
    freq = _rope_freq_row()
    rope = _rope_tables(pos_col, freq if tie is None else freq + tie[0:1])
    d = x.shape[1]

    h0, *qkv = _qkv_proj(x, row(0), p["w_qkv_t"], rope)
    os_, ls_ = zip(*[_attn_fwd(qkv[g_], g_, d_) for g_, d_ in enumerate(DILATIONS)])
    y_a, x1, mixed, lse = _mix_wo(os_, ls_, p["w_o_t"], x, row(1))
    if late_weights is not None:
        p = {**p, **late_weights(x1)}
    x2, ffn0 = _ffn_fwd(x1, row(2), row(3), p["w_up_t"][0], p["ffn_w_dw"][0], p["ffn_b_dw"][0], p["w_down"][0])
    h1, ag = _norm_matmul(x2, row(4), p["w_pw1_t"], tn=_tile(p["w_pw1_t"].shape[0]), name="conv_pw1", bias=p["b_pw1"])
    u3, u1 = _conv_mid(ag, p["conv_w_dw"], p["conv_b_dw"], p["ln_g"], p["ln_b"])
    y_c, x3 = _matmul_resnorm(u3, p["w_pw2"], x2, row(5), name="conv_pw2", bias=p["b_pw2"])
    x4, ffn1 = _ffn_fwd(x3, row(6), row(7), p["w_up_t"][1], p["ffn_w_dw"][1], p["ffn_b_dw"][1], p["w_down"][1])
    dx4, loss, dy4, dg7 = _loss_postnorm_bwd(ffn1[4], row(7), x4, target)

    big = [BF16, BF16]

    def tied(r, *tokens):
        tokens = [t for t in tokens if t is not None]
        return row(r) if not tokens else row(r) + jnp.tile(sum(tokens)[0:1], (1, d // 128))

    dx3, gf1, d_up1, d_down1 = _ffn_bwd(ffn1, dx4, row(6), row(7), p["w_up_t"][1], p["ffn_w_dw"][1], p["ffn_b_dw"][1],
                                        p["w_down"][1], post=(dy4, dg7))
    t0 = exchange.submit("ffn1", [d_up1, d_down1], big) if exchange else None
    dy_c, dg5, db_pw2 = _postnorm_bwd(y_c, tied(5, t0), dx3, name="conv_post_bwd", with_bias_grad=True)
    du3 = _matmul(dy_c, p["w_pw2"], name="conv_du3", out_dtype=F32, transposed_w=True)
    d_wpw2 = _weight_grad(u3, dy_c, name="conv_dw_pw2", out_shape=(1, u3.shape[1], d))
    du1, d_lng, d_lnb, d_cbdw, d_cwdw = _conv_mid_bwd(ag, u1, du3, p["conv_b_dw"], p["ln_g"], p["ln_b"])
    dag, db_pw1 = _glu_conv_bwd(du1, ag, p["conv_w_dw"])
    dx2, dg4 = _matmul_prenorm_bwd([(dag, 0, dag.shape[1], 0)], p["w_pw1_t"], x2, row(4), dx3, name="conv_dx")
    d_wpw1_t = _weight_grad(dag, h1, name="conv_dw_pw1", out_shape=(1, dag.shape[1], d))
    t0 = exchange.advance(dx2) if exchange else None
    t1 = exchange.submit("conv", [d_wpw1_t, d_wpw2], big) if exchange else None
    dx1, gf0, d_up0, d_down0 = _ffn_bwd(ffn0, dx2, row(2), tied(3, t0, t1), p["w_up_t"][0], p["ffn_w_dw"][0], p["ffn_b_dw"][0],
                                        p["w_down"][0])
    t0 = exchange.advance(dx1) if exchange else None
    t1 = exchange.submit("ffn0", [d_up0, d_down0], big) if exchange else None
    dy_a, dg1 = _postnorm_bwd(y_a, tied(1, t0, t1), dx1, name="attn_post_bwd")
    dmix = _matmul(dy_a, p["w_o_t"], name="attn_dmix", out_dtype=BF16, transposed_w=False)
    d_wo_t = _weight_grad(dy_a, mixed, name="attn_dw_o", out_shape=(1, d, GROUP_WIDTH))
    pieces, d_wqkv_t = [], None
    for g_, d_ in enumerate(DILATIONS):
        tok = exchange.advance(dkv) if exchange and g_ > 0 else None
        dq, dkv = _attn_bwd(qkv[g_], dmix, mixed, lse, rope, g_, d_, ties=() if tok is None else (tok,))
        for t, (arr, c0) in enumerate(((dq, 0), (dkv, 0), (dkv, GROUP_WIDTH))):
            r0 = (3 * t + g_) * GROUP_WIDTH
            pieces.append((arr, c0, GROUP_WIDTH, r0))
            d_wqkv_t = _weight_grad(arr, h0, name="attn_dw_qkv", a_col0=c0, ka=GROUP_WIDTH, out=d_wqkv_t,
                                    out_shape=(1, p["w_qkv_t"].shape[0], d), row0=r0)
    t0 = exchange.advance(dkv) if exchange else None
    t1 = exchange.submit("attn", [d_wqkv_t, d_wo_t], big) if exchange else None
    t2 = exchange.advance(d_wqkv_t) if exchange else None
    grad_x, dg0 = _matmul_prenorm_bwd(pieces, p["w_qkv_t"], x, tied(0, t0, t1, t2), dx1, name="attn_dx")

    grads = dict(
        norm_g=jnp.concatenate([dg0, dg1, gf0["g_pre"], gf0["g_post"], dg4, dg5, gf1["g_pre"], gf1["g_post"]], axis=0),
        w_qkv_t=d_wqkv_t, w_o_t=d_wo_t, w_pw1_t=d_wpw1_t, b_pw1=db_pw1,
        conv_w_dw=d_cwdw[:CONV_KERNEL], conv_b_dw=d_cbdw, ln_g=d_lng, ln_b=d_lnb, w_pw2=d_wpw2, b_pw2=db_pw2,
        w_up_t=[d_up0, d_up1], ffn_w_dw=jnp.stack([gf0["w_dw"], gf1["w_dw"]]),
        ffn_b_dw=jnp.concatenate([gf0["b_dw"], gf1["b_dw"]], axis=0), w_down=[d_down0, d_down1])
    return loss, grad_x, grads


SMALL_AXIS = dict(norm_g=2, conv_b_pw1=1, conv_w_dw=2, conv_b_dw=1, conv_ln_g=1, conv_ln_b=1, conv_b_pw2=1, ffn_w_dw=2)
SMALL = tuple(SMALL_AXIS)
MATMUL_WEIGHTS = dict(attn_w_qkv=True, conv_w_pw1=True, ffn_w_up=True, conv_w_pw2=False, ffn_w_down=False)


def _pack(arrays, cols, row_multiple):
    flat = jnp.concatenate([a.reshape(-1) for a in arrays])
    rows = -(-flat.shape[0] // cols)
    rows = -(-rows // row_multiple) * row_multiple
    return jnp.pad(flat, (0, rows * cols - flat.shape[0])).reshape(rows, cols)


def _unpack(packed, shapes):
    flat = packed.reshape(packed.shape[:-2] + (-1,))
    out, off = [], 0
    for shp in shapes:
        n = math.prod(shp)
        out.append(flat[..., off:off + n].reshape(packed.shape[:-2] + tuple(shp)))
        off += n
    return out


def _join_shards(stacked, axis):
    moved = jnp.moveaxis(stacked, 0, axis)
    shp = moved.shape
    return moved.reshape(shp[:axis] + (shp[axis] * shp[axis + 1],) + shp[axis + 2:])


def _split_shards(whole, axis):
    shp = whole.shape
    cut = whole.reshape(shp[:axis] + (N_DEV, shp[axis] // N_DEV) + shp[axis + 1:])
    return jnp.moveaxis(cut, axis, 0)


def _row_shard(w, transposed):
    t = jnp.swapaxes(w, 1, 2) if transposed else w
    return t.astype(BF16).reshape(-1, t.shape[-1])


def kernel(x, positions, norm_g, attn_w_qkv, attn_w_o, conv_w_pw1, conv_b_pw1, conv_w_dw, conv_b_dw, conv_ln_g, conv_ln_b, conv_w_pw2, conv_b_pw2, ffn_w_up, ffn_w_dw, ffn_b_dw, ffn_w_down, loss_target, m_norm_g, m_attn_w_qkv, m_attn_w_o, m_conv_w_pw1, m_conv_b_pw1, m_conv_w_dw, m_conv_b_dw, m_conv_ln_g, m_conv_ln_b, m_conv_w_pw2, m_conv_b_pw2, m_ffn_w_up, m_ffn_w_dw, m_ffn_b_dw, m_ffn_w_down, v_norm_g, v_attn_w_qkv, v_attn_w_o, v_conv_w_pw1, v_conv_b_pw1, v_conv_w_dw, v_conv_b_dw, v_conv_ln_g, v_conv_ln_b, v_conv_w_pw2, v_conv_b_pw2, v_ffn_w_up, v_ffn_w_dw, v_ffn_b_dw, v_ffn_w_down):
    w = dict(norm_g=norm_g, attn_w_qkv=attn_w_qkv, attn_w_o=attn_w_o, conv_w_pw1=conv_w_pw1, conv_b_pw1=conv_b_pw1,
             conv_w_dw=conv_w_dw, conv_b_dw=conv_b_dw, conv_ln_g=conv_ln_g, conv_ln_b=conv_ln_b, conv_w_pw2=conv_w_pw2,
             conv_b_pw2=conv_b_pw2, ffn_w_up=ffn_w_up, ffn_w_dw=ffn_w_dw, ffn_w_down=ffn_w_down)
    m = dict(norm_g=m_norm_g, attn_w_qkv=m_attn_w_qkv, attn_w_o=m_attn_w_o, conv_w_pw1=m_conv_w_pw1, conv_b_pw1=m_conv_b_pw1,
             conv_w_dw=m_conv_w_dw, conv_b_dw=m_conv_b_dw, conv_ln_g=m_conv_ln_g, conv_ln_b=m_conv_ln_b, conv_w_pw2=m_conv_w_pw2,
             conv_b_pw2=m_conv_b_pw2, ffn_w_up=m_ffn_w_up, ffn_w_dw=m_ffn_w_dw, ffn_w_down=m_ffn_w_down)
    v = dict(norm_g=v_norm_g, attn_w_qkv=v_attn_w_qkv, attn_w_o=v_attn_w_o, conv_w_pw1=v_conv_w_pw1, conv_b_pw1=v_conv_b_pw1,
             conv_w_dw=v_conv_w_dw, conv_b_dw=v_conv_b_dw, conv_ln_g=v_conv_ln_g, conv_ln_b=v_conv_ln_b, conv_w_pw2=v_conv_w_pw2,
             conv_b_pw2=v_conv_b_pw2, ffn_w_up=v_ffn_w_up, ffn_w_dw=v_ffn_w_dw, ffn_w_down=v_ffn_w_down)
    d = x.shape[-1]

    w_qkv_t, w_o_t, small = _all_gather([_row_shard(attn_w_qkv, True), _row_shard(attn_w_o, True),
                                         _pack([w[n] for n in SMALL], 128, 8)], "gather_first_weights")
    w_qkv_t, w_o_t = w_qkv_t.reshape(-1, d), w_o_t.reshape(d, -1)
    sm = {n: _join_shards(stacked, SMALL_AXIS[n])
          for n, stacked in zip(SMALL, _unpack(small, [w[n].shape for n in SMALL]))}
    late = {n: t for n, t in MATMUL_WEIGHTS.items() if n != "attn_w_qkv"}
    shares = [_row_shard(w[n], t) for n, t in late.items()]
    rows = [s_.shape[0] for s_ in shares]
    late_share = jnp.concatenate(shares, axis=0)
    send_sems, recv_sems, share_thru, land_thru, tie = _gather_start(late_share)
    me = 4 * lax.axis_index("x") + 2 * lax.axis_index("y") + lax.axis_index("c")

    def late_weights(after):
        big = _gather_wait(send_sems, recv_sems, share_thru, land_thru, after)
        big = lax.dynamic_update_slice(big, late_share[None], (me, 0, 0))
        whole, r0 = {}, 0
        for n, nr in zip(late, rows):
            layers = w[n].shape[0]
            seg = big[:, r0:r0 + nr].reshape(N_DEV, layers, nr // layers, d)
            whole[n] = [seg[:, l_].reshape(-1, d) for l_ in range(layers)]
            r0 += nr
        return dict(w_pw1_t=whole["conv_w_pw1"][0], w_pw2=whole["conv_w_pw2"][0], w_up_t=whole["ffn_w_up"],
                    w_down=whole["ffn_w_down"])

    p = dict(norm_g=sm["norm_g"].reshape(-1, d), w_qkv_t=w_qkv_t, w_o_t=w_o_t, b_pw1=sm["conv_b_pw1"],
             conv_w_dw=sm["conv_w_dw"][0], conv_b_dw=sm["conv_b_dw"], ln_g=sm["conv_ln_g"], ln_b=sm["conv_ln_b"],
             b_pw2=sm["conv_b_pw2"], ffn_w_dw=sm["ffn_w_dw"], ffn_b_dw=[ffn_b_dw[0:1], ffn_b_dw[1:2]])

    exchange = _GradExchange()
    loss, grad_x, g = _local_step(x[0], positions.reshape(-1, 1), loss_target[0], p, tie, late_weights, exchange)
    loss = lax.psum(loss[0, 0], ("x", "y", "c"))
    gsmall = dict(norm_g=g["norm_g"].reshape(norm_g.shape[0], 4, -1), conv_b_pw1=g["b_pw1"], conv_w_dw=g["conv_w_dw"][None],
                  conv_b_dw=g["conv_b_dw"], conv_ln_g=g["ln_g"], conv_ln_b=g["ln_b"], conv_b_pw2=g["b_pw2"], ffn_w_dw=g["ffn_w_dw"])
    small_contrib = jnp.concatenate([_split_shards(gsmall[n], SMALL_AXIS[n]).reshape(N_DEV, -1) for n in SMALL], axis=1)
    srows = small.shape[1]
    small_contrib = jnp.pad(small_contrib, ((0, 0), (0, srows * 128 - small_contrib.shape[1]))).reshape(1, N_DEV, srows, 128)
    small_sums = _rs_chips([_rs_pair_add(small_contrib, _rs_sibling([small_contrib])[0], exchange.core, F32)])[0]

    outs = {}

    def update(n, reduced):
        gsum = jnp.swapaxes(reduced, 1, 2) if n == "attn_w_o" or MATMUL_WEIGHTS.get(n) else reduced
        outs[n] = (gsum, *_adamw(gsum, w[n], m[n], v[n], "adamw"))

    (s_up1, s_down1), (s_pw1, s_pw2), (s_up0, s_down0) = exchange.results()[:3]
    update("conv_w_pw1", s_pw1)
    update("conv_w_pw2", s_pw2)
    update("ffn_w_up", jnp.concatenate([s_up0, s_up1], axis=0))
    update("ffn_w_down", jnp.concatenate([s_down0, s_down1], axis=0))
    sshapes = [w[n].shape for n in SMALL]
    souts = _sum_adamw(small_sums[0], *[_pack([t[n] for n in SMALL], 128, 8) for t in (w, m, v)], name="sum_adamw_small")
    for n, vals in zip(SMALL, zip(*[_unpack(o, sshapes) for o in souts])):
        outs[n] = vals
    bparts, = _all_gather([_pack([g["ffn_b_dw"]], 128, 8)], "gather_bias_grads")
    bouts = _sum_adamw(bparts, *[_pack([t], 128, 8) for t in (ffn_b_dw, m_ffn_b_dw, v_ffn_b_dw)], name="sum_adamw_bias")
    outs["ffn_b_dw"] = tuple(_unpack(o, [ffn_b_dw.shape])[0] for o in bouts)
    done = [outs[n][1][0, :8, :128] for n in ("conv_w_pw1", "conv_w_pw2", "ffn_w_up", "ffn_w_down")]
    exchange.advance(sum(done) + bouts[1][:8] + souts[1][:8])
    s_qkv, s_wo = exchange.results()[3]
    update("attn_w_qkv", s_qkv)
    update("attn_w_o", s_wo)

    order = ("norm_g", "attn_w_qkv", "attn_w_o", "conv_w_pw1", "conv_b_pw1", "conv_w_dw", "conv_b_dw", "conv_ln_g",
             "conv_ln_b", "conv_w_pw2", "conv_b_pw2", "ffn_w_up", "ffn_w_dw", "ffn_b_dw", "ffn_w_down")
    return (loss, grad_x[None], *[outs[n][0] for n in order], *[outs[n][1] for n in order],
            *[outs[n][2] for n in order], *[outs[n][3] for n in order])
```
